```python
import math
import jax, jax.numpy as jnp
from jax import lax
import numpy as np

D_MODEL = 1024
BATCH = 8
SEQ = 4096
DEPTH = 4

NORM_EPS = 1e-6
DN_HEAD_DIM = 128
DN_WIDTH = D_MODEL // 2
DN_HEADS = DN_WIDTH // DN_HEAD_DIM
DN_CHUNK = 64
DN_CONV = 4
GM_GROUP_DIM = 64
GM_WIDTH = D_MODEL // 4
GM_GROUPS = GM_WIDTH // GM_GROUP_DIM
GM_CHUNK = 128
SW_HEAD_DIM = 64
SW_WIDTH = D_MODEL // 4
SW_HEADS = SW_WIDTH // SW_HEAD_DIM
SW_PATTERNS = ((128, 1), (512, 4), (2048, 16))
SW_BLOCK = 128
ROPE_THETA = 500000.0
ROPE_DIM = SW_HEAD_DIM // 4
MIX_WIDTH = DN_WIDTH + GM_WIDTH + SW_WIDTH
IN_SPLITS = (3 * DN_WIDTH, 4 * DN_WIDTH, 4 * DN_WIDTH + DN_HEADS, 4 * DN_WIDTH + 2 * DN_HEADS,
             4 * DN_WIDTH + 2 * DN_HEADS + 2 * GM_WIDTH)
IN_WIDTH = IN_SPLITS[-1] + len(SW_PATTERNS) * 3 * SW_WIDTH
FFN_HIDDEN = -(-8 * D_MODEL // (3 * 256)) * 256

kernel_name = "hybrid_parallel_heads_decoder"


def rms_norm(x, w):
    xf = x.astype(jnp.float32)
    y = xf * lax.rsqrt(jnp.mean(xf * xf, axis=-1, keepdims=True) + NORM_EPS)
    return (y * w.astype(jnp.float32)).astype(x.dtype)


def layer_norm(x, g, b):
    xf = x.astype(jnp.float32)
    mu = jnp.mean(xf, axis=-1, keepdims=True)
    xc = xf - mu
    var = jnp.mean(xc * xc, axis=-1, keepdims=True)
    return (xc * lax.rsqrt(var + NORM_EPS) * g.astype(jnp.float32) + b.astype(jnp.float32)).astype(x.dtype)


def l2_norm(x):
    return x * lax.rsqrt(jnp.sum(x * x, axis=-1, keepdims=True) + NORM_EPS)


def rotary_tables(seq):
    inv = ROPE_THETA ** (-jnp.arange(0, ROPE_DIM, 2, dtype=jnp.float32) / ROPE_DIM)
    ang = jnp.arange(seq, dtype=jnp.float32)[:, None] * inv[None, :]
    return jnp.cos(ang), jnp.sin(ang)


def apply_partial_rotary(x, cos, sin):
    half = ROPE_DIM // 2
    x1, x2, xp = x[..., :half], x[..., half:ROPE_DIM], x[..., ROPE_DIM:]
    c = cos[None, :, None, :]
    s = sin[None, :, None, :]
    return jnp.concatenate([x1 * c - x2 * s, x2 * c + x1 * s, xp], axis=-1)


def causal_dwconv_silu(x, w):
    k, ch = w.shape
    y = lax.conv_general_dilated(x, w[:, None, :].astype(x.dtype), window_strides=(1,),
                                 padding=[(k - 1, 0)], dimension_numbers=('NWC', 'WIO', 'NWC'),
                                 feature_group_count=ch)
    return jax.nn.silu(y)


def gated_delta_rule(q, k, v, g, beta):
    bsz, h, t, dk = q.shape
    dv = v.shape[-1]
    n = t // DN_CHUNK
    q = q.reshape(bsz, h, n, DN_CHUNK, dk) * (dk ** -0.5)
    k = k.reshape(bsz, h, n, DN_CHUNK, dk)
    v = v.reshape(bsz, h, n, DN_CHUNK, dv)
    beta = beta.reshape(bsz, h, n, DN_CHUNK)
    gcum = jnp.cumsum(g.reshape(bsz, h, n, DN_CHUNK), axis=-1)
    idx = jnp.arange(DN_CHUNK)
    causal = idx[:, None] >= idx[None, :]
    strict = idx[:, None] > idx[None, :]
    decay = jnp.exp(jnp.where(causal, gcum[..., :, None] - gcum[..., None, :], -jnp.inf))
    kb = k * beta[..., None]
    a_low = jnp.where(strict, jnp.einsum('bhnid,bhnjd->bhnij', kb, k) * decay, 0.0)
    eye = jnp.eye(DN_CHUNK, dtype=q.dtype)
    rhs = jnp.concatenate([v * beta[..., None], kb * jnp.exp(gcum)[..., None]], axis=-1)
    sol = lax.linalg.triangular_solve(a_low + eye, rhs, left_side=True, lower=True)
    u, w = sol[..., :dv], sol[..., dv:]
    a_qk = jnp.einsum('bhnid,bhnjd->bhnij', q, k) * decay
    q_dec = q * jnp.exp(gcum)[..., None]
    g_last = gcum[..., -1]
    k_dec = k * jnp.exp(g_last[..., None] - gcum)[..., None]

    def step(state, xs):
        u_c, w_c, qd_c, a_c, kd_c, gl_c = xs
        v_new = u_c - jnp.einsum('bhcd,bhde->bhce', w_c, state)
        o = jnp.einsum('bhcd,bhde->bhce', qd_c, state) + jnp.einsum('bhij,bhje->bhie', a_c, v_new)
        state = state * jnp.exp(gl_c)[..., None, None] + jnp.einsum('bhcd,bhce->bhde', kd_c, v_new)
        return state, o

    xs = (jnp.moveaxis(u, 2, 0), jnp.moveaxis(w, 2, 0), jnp.moveaxis(q_dec, 2, 0),
          jnp.moveaxis(a_qk, 2, 0), jnp.moveaxis(k_dec, 2, 0), jnp.moveaxis(g_last, 2, 0))
    s0 = jnp.zeros((bsz, h, dk, dv), q.dtype)
    _, o = lax.scan(step, s0, xs)
    return jnp.moveaxis(o, 0, 2).reshape(bsz, h, t, dv)


def deltanet_mixer(qkv, z, a, b, conv_w, a_log, dt_bias, out_norm_w):
    bsz, t, _ = qkv.shape
    qkv = causal_dwconv_silu(qkv, conv_w).astype(jnp.float32)
    q, k, v = jnp.split(qkv, 3, axis=-1)
    heads = lambda y: y.reshape(bsz, t, DN_HEADS, DN_HEAD_DIM).transpose(0, 2, 1, 3)
    q, k, v = l2_norm(heads(q)), l2_norm(heads(k)), heads(v)
    g = -jnp.exp(a_log.astype(jnp.float32)) * jax.nn.softplus(a.astype(jnp.float32) + dt_bias.astype(jnp.float32))
    beta = jax.nn.sigmoid(b.astype(jnp.float32))
    o = gated_delta_rule(q, k, v, g.transpose(0, 2, 1), beta.transpose(0, 2, 1))
    o = o.transpose(0, 2, 1, 3)
    zg = z.astype(jnp.float32).reshape(bsz, t, DN_HEADS, DN_HEAD_DIM)
    o = rms_norm(o, out_norm_w) * jax.nn.silu(zg)
    return o.reshape(bsz, t, DN_WIDTH)


def spatial_gating_mixer(uv, ln_g, ln_b, w_s, b_s):
    bsz, t, _ = uv.shape
    zz = jax.nn.gelu(uv.astype(jnp.float32), approximate=False)
    u, v = jnp.split(zz, 2, axis=-1)
    v = layer_norm(v, ln_g, ln_b)
    n = t // GM_CHUNK
    v = v.reshape(bsz, n, GM_CHUNK, GM_GROUPS, GM_GROUP_DIM)
    causal = jnp.tril(jnp.ones((GM_CHUNK, GM_CHUNK), dtype=bool))
    ws = jnp.where(causal, w_s.astype(jnp.float32), 0.0)
    sv = jnp.einsum('gij,bnjgc->bnigc', ws, v) + b_s.astype(jnp.float32).T[None, None, :, :, None]
    return u * sv.reshape(bsz, t, GM_WIDTH)


def dilated_window_attention(q, k, v, dilation, span):
    bsz, t, h, d = q.shape
    length = t // dilation
    nb = -(-length // SW_BLOCK)
    lp = nb * SW_BLOCK

    def to_sub(y):
        y = y.reshape(bsz, length, dilation, h, d).transpose(0, 2, 3, 1, 4)
        y = jnp.pad(y, ((0, 0), (0, 0), (0, 0), (0, lp - length), (0, 0)))
        return y.reshape(bsz, dilation, h, nb, SW_BLOCK, d)

    def with_prev(y):
        prev = jnp.pad(y, ((0, 0), (0, 0), (0, 0), (1, 0), (0, 0), (0, 0)))[:, :, :, :-1]
        return jnp.concatenate([prev, y], axis=-2)

    qs = to_sub(q)
    kk, vv = with_prev(to_sub(k)), with_prev(to_sub(v))
    s = jnp.einsum('brhnie,brhnje->brhnij', qs, kk) * (d ** -0.5)
    blk = jnp.arange(nb)[:, None, None] * SW_BLOCK
    qpos = blk + jnp.arange(SW_BLOCK)[None, :, None]
    kpos = blk - SW_BLOCK + jnp.arange(2 * SW_BLOCK)[None, None, :]
    dist = qpos - kpos
    valid = (dist >= 0) & (dist <= span) & (kpos >= 0)
    s = jnp.where(valid, s, -jnp.inf)
    m = jnp.max(s, axis=-1, keepdims=True)
    p = jnp.exp(s - m)
    l = jnp.sum(p, axis=-1, keepdims=True)
    o = jnp.einsum('brhnij,brhnje->brhnie', p, vv) / l
    lse = (m + jnp.log(l))[..., 0]
    o = o.reshape(bsz, dilation, h, lp, d)[:, :, :, :length]
    o = o.transpose(0, 3, 1, 2, 4).reshape(bsz, t, h, d)
    lse = lse.reshape(bsz, dilation, h, lp)[:, :, :, :length]
    lse = lse.transpose(0, 3, 1, 2).reshape(bsz, t, h)
    return o, lse


def dilated_attention_mixer(qkv, q_norm_w, k_norm_w, cos, sin):
    bsz, t, _ = qkv.shape
    parts = qkv.astype(jnp.float32).reshape(bsz, t, len(SW_PATTERNS), 3, SW_HEADS, SW_HEAD_DIM)
    outs, lses = [], []
    for gi, (window, dilation) in enumerate(SW_PATTERNS):
        q = apply_partial_rotary(rms_norm(parts[:, :, gi, 0], q_norm_w), cos, sin)
        k = apply_partial_rotary(rms_norm(parts[:, :, gi, 1], k_norm_w), cos, sin)
        o, lse = dilated_window_attention(q, k, parts[:, :, gi, 2], dilation, window // dilation)
        outs.append(o)
        lses.append(lse)
    o = jnp.stack(outs, axis=0)
    wts = jax.nn.softmax(jnp.stack(lses, axis=0), axis=0)
    return jnp.sum(wts[..., None] * o, axis=0).reshape(bsz, t, SW_WIDTH)


def _fwd_setup_inputs(seed: int = 0) -> dict:
    key = jax.random.key(seed)
    ks = jax.random.split(key, 20)
    f32 = jnp.float32
    nl = DEPTH

    def nrm(k, shape, scale):
        return jax.random.normal(k, shape, f32) * scale

    dt = jnp.exp(jax.random.uniform(ks[10], (nl, DN_HEADS), f32, math.log(1e-3), math.log(1e-1)))
    return {
        'x': nrm(ks[0], (BATCH, SEQ, D_MODEL), 1.0),
        'c': nrm(ks[1], (BATCH, D_MODEL), 1.0),
        'w_mod': nrm(ks[2], (nl, D_MODEL, 6 * D_MODEL), 0.5 * D_MODEL ** -0.5),
        'b_mod': nrm(ks[3], (nl, 6 * D_MODEL), 0.01),
        'mix_norm_w': 1.0 + nrm(ks[4], (nl, D_MODEL), 0.02),
        'ffn_norm_w': 1.0 + nrm(ks[5], (nl, D_MODEL), 0.02),
        'w_in': nrm(ks[6], (nl, D_MODEL, IN_WIDTH), D_MODEL ** -0.5),
        'w_out': nrm(ks[7], (nl, MIX_WIDTH, D_MODEL), MIX_WIDTH ** -0.5),
        'dn_conv_w': nrm(ks[8], (nl, DN_CONV, 3 * DN_WIDTH), DN_CONV ** -0.5),
        'dn_a_log': jnp.log(jax.random.uniform(ks[9], (nl, DN_HEADS), f32, 1.0, 16.0)),
        'dn_dt_bias': dt + jnp.log(-jnp.expm1(-dt)),
        'dn_out_norm_w': 1.0 + nrm(ks[11], (nl, DN_HEAD_DIM), 0.02),
        'gm_ln_g': 1.0 + nrm(ks[12], (nl, GM_WIDTH), 0.02),
        'gm_ln_b': nrm(ks[13], (nl, GM_WIDTH), 0.02),
        'gm_w_s': nrm(ks[14], (nl, GM_GROUPS, GM_CHUNK, GM_CHUNK), GM_CHUNK ** -0.5),
        'gm_b_s': 1.0 + nrm(ks[15], (nl, GM_GROUPS, GM_CHUNK), 0.01),
        'sw_q_norm_w': 1.0 + nrm(ks[16], (nl, SW_HEAD_DIM), 0.02),
        'sw_k_norm_w': 1.0 + nrm(ks[17], (nl, SW_HEAD_DIM), 0.02),
        'w_ffn_in': nrm(ks[18], (nl, D_MODEL, 2 * FFN_HIDDEN), D_MODEL ** -0.5),
        'w_ffn_out': nrm(ks[19], (nl, FFN_HIDDEN, D_MODEL), FFN_HIDDEN ** -0.5),
    }


def _fwd_reference(x, c, w_mod, b_mod, mix_norm_w, ffn_norm_w, w_in, w_out, dn_conv_w, dn_a_log,
              dn_dt_bias, dn_out_norm_w, gm_ln_g, gm_ln_b, gm_w_s, gm_b_s, sw_q_norm_w,
              sw_k_norm_w, w_ffn_in, w_ffn_out):
    bsz, t, _ = x.shape
    cos, sin = rotary_tables(t)
    c_act = jax.nn.silu(c)
    for layer in range(DEPTH):
        mod = jnp.einsum('bd,de->be', c_act, w_mod[layer]) + b_mod[layer]
        shift1, scale1, gate1, shift2, scale2, gate2 = [m[:, None, :] for m in jnp.split(mod, 6, axis=-1)]
        h = rms_norm(x, mix_norm_w[layer]) * (1.0 + scale1) + shift1
        proj = jnp.einsum('btd,de->bte', h, w_in[layer])
        dn_qkv, dn_z, dn_a, dn_b, gm_uv, sw_qkv = jnp.split(proj, IN_SPLITS, axis=-1)
        y_a = deltanet_mixer(dn_qkv, dn_z, dn_a, dn_b, dn_conv_w[layer], dn_a_log[layer],
                             dn_dt_bias[layer], dn_out_norm_w[layer])
        y_b = spatial_gating_mixer(gm_uv, gm_ln_g[layer], gm_ln_b[layer], gm_w_s[layer], gm_b_s[layer])
        y_c = dilated_attention_mixer(sw_qkv, sw_q_norm_w[layer], sw_k_norm_w[layer], cos, sin)
        y = jnp.concatenate([y_a, y_b, y_c], axis=-1).astype(x.dtype)
        x = x + gate1 * jnp.einsum('bte,ed->btd', y, w_out[layer])
        h = rms_norm(x, ffn_norm_w[layer]) * (1.0 + scale2) + shift2
        gate, up = jnp.split(jnp.einsum('btd,df->btf', h, w_ffn_in[layer]), 2, axis=-1)
        x = x + gate2 * jnp.einsum('btf,fd->btd', jax.nn.silu(gate) * up, w_ffn_out[layer])
    return x


import jax as _jax
import jax.numpy as _jnp

TWIN_FORMAT = 'train_step'
FWD_PARAMS = ['x', 'c', 'w_mod', 'b_mod', 'mix_norm_w', 'ffn_norm_w', 'w_in', 'w_out', 'dn_conv_w', 'dn_a_log', 'dn_dt_bias', 'dn_out_norm_w', 'gm_ln_g', 'gm_ln_b', 'gm_w_s', 'gm_b_s', 'sw_q_norm_w', 'sw_k_norm_w', 'w_ffn_in', 'w_ffn_out']
TWIN_WEIGHTS = ['w_mod', 'b_mod', 'mix_norm_w', 'ffn_norm_w', 'w_in', 'w_out', 'dn_conv_w', 'dn_a_log', 'dn_dt_bias', 'dn_out_norm_w', 'gm_ln_g', 'gm_ln_b', 'gm_w_s', 'gm_b_s', 'sw_q_norm_w', 'sw_k_norm_w', 'w_ffn_in', 'w_ffn_out']
TWIN_DIFF_INPUT = 'x'
TWIN_INPUTS = ['x', 'c', 'w_mod', 'b_mod', 'mix_norm_w', 'ffn_norm_w', 'w_in', 'w_out', 'dn_conv_w', 'dn_a_log', 'dn_dt_bias', 'dn_out_norm_w', 'gm_ln_g', 'gm_ln_b', 'gm_w_s', 'gm_b_s', 'sw_q_norm_w', 'sw_k_norm_w', 'w_ffn_in', 'w_ffn_out', 'loss_target', 'm_w_mod', 'm_b_mod', 'm_mix_norm_w', 'm_ffn_norm_w', 'm_w_in', 'm_w_out', 'm_dn_conv_w', 'm_dn_a_log', 'm_dn_dt_bias', 'm_dn_out_norm_w', 'm_gm_ln_g', 'm_gm_ln_b', 'm_gm_w_s', 'm_gm_b_s', 'm_sw_q_norm_w', 'm_sw_k_norm_w', 'm_w_ffn_in', 'm_w_ffn_out', 'v_w_mod', 'v_b_mod', 'v_mix_norm_w', 'v_ffn_norm_w', 'v_w_in', 'v_w_out', 'v_dn_conv_w', 'v_dn_a_log', 'v_dn_dt_bias', 'v_dn_out_norm_w', 'v_gm_ln_g', 'v_gm_ln_b', 'v_gm_w_s', 'v_gm_b_s', 'v_sw_q_norm_w', 'v_sw_k_norm_w', 'v_w_ffn_in', 'v_w_ffn_out']
TWIN_OUTPUTS = ['loss', 'grad_x', 'grad_w_mod', 'grad_b_mod', 'grad_mix_norm_w', 'grad_ffn_norm_w', 'grad_w_in', 'grad_w_out', 'grad_dn_conv_w', 'grad_dn_a_log', 'grad_dn_dt_bias', 'grad_dn_out_norm_w', 'grad_gm_ln_g', 'grad_gm_ln_b', 'grad_gm_w_s', 'grad_gm_b_s', 'grad_sw_q_norm_w', 'grad_sw_k_norm_w', 'grad_w_ffn_in', 'grad_w_ffn_out', 'delta_w_mod', 'delta_b_mod', 'delta_mix_norm_w', 'delta_ffn_norm_w', 'delta_w_in', 'delta_w_out', 'delta_dn_conv_w', 'delta_dn_a_log', 'delta_dn_dt_bias', 'delta_dn_out_norm_w', 'delta_gm_ln_g', 'delta_gm_ln_b', 'delta_gm_w_s', 'delta_gm_b_s', 'delta_sw_q_norm_w', 'delta_sw_k_norm_w', 'delta_w_ffn_in', 'delta_w_ffn_out', 'new_m_w_mod', 'new_m_b_mod', 'new_m_mix_norm_w', 'new_m_ffn_norm_w', 'new_m_w_in', 'new_m_w_out', 'new_m_dn_conv_w', 'new_m_dn_a_log', 'new_m_dn_dt_bias', 'new_m_dn_out_norm_w', 'new_m_gm_ln_g', 'new_m_gm_ln_b', 'new_m_gm_w_s', 'new_m_gm_b_s', 'new_m_sw_q_norm_w', 'new_m_sw_k_norm_w', 'new_m_w_ffn_in', 'new_m_w_ffn_out', 'new_v_w_mod', 'new_v_b_mod', 'new_v_mix_norm_w', 'new_v_ffn_norm_w', 'new_v_w_in', 'new_v_w_out', 'new_v_dn_conv_w', 'new_v_dn_a_log', 'new_v_dn_dt_bias', 'new_v_dn_out_norm_w', 'new_v_gm_ln_g', 'new_v_gm_ln_b', 'new_v_gm_w_s', 'new_v_gm_b_s', 'new_v_sw_q_norm_w', 'new_v_sw_k_norm_w', 'new_v_w_ffn_in', 'new_v_w_ffn_out']
TWIN_LEAF_KINDS = {'loss': 'loss', 'grad_x': 'grad_x', 'grad_w_mod': 'grad_w', 'grad_b_mod': 'grad_w', 'grad_mix_norm_w': 'grad_w', 'grad_ffn_norm_w': 'grad_w', 'grad_w_in': 'grad_w', 'grad_w_out': 'grad_w', 'grad_dn_conv_w': 'grad_w', 'grad_dn_a_log': 'grad_w', 'grad_dn_dt_bias': 'grad_w', 'grad_dn_out_norm_w': 'grad_w', 'grad_gm_ln_g': 'grad_w', 'grad_gm_ln_b': 'grad_w', 'grad_gm_w_s': 'grad_w', 'grad_gm_b_s': 'grad_w', 'grad_sw_q_norm_w': 'grad_w', 'grad_sw_k_norm_w': 'grad_w', 'grad_w_ffn_in': 'grad_w', 'grad_w_ffn_out': 'grad_w', 'delta_w_mod': 'delta_w', 'delta_b_mod': 'delta_w', 'delta_mix_norm_w': 'delta_w', 'delta_ffn_norm_w': 'delta_w', 'delta_w_in': 'delta_w', 'delta_w_out': 'delta_w', 'delta_dn_conv_w': 'delta_w', 'delta_dn_a_log': 'delta_w', 'delta_dn_dt_bias': 'delta_w', 'delta_dn_out_norm_w': 'delta_w', 'delta_gm_ln_g': 'delta_w', 'delta_gm_ln_b': 'delta_w', 'delta_gm_w_s': 'delta_w', 'delta_gm_b_s': 'delta_w', 'delta_sw_q_norm_w': 'delta_w', 'delta_sw_k_norm_w': 'delta_w', 'delta_w_ffn_in': 'delta_w', 'delta_w_ffn_out': 'delta_w', 'new_m_w_mod': 'new_m', 'new_m_b_mod': 'new_m', 'new_m_mix_norm_w': 'new_m', 'new_m_ffn_norm_w': 'new_m', 'new_m_w_in': 'new_m', 'new_m_w_out': 'new_m', 'new_m_dn_conv_w': 'new_m', 'new_m_dn_a_log': 'new_m', 'new_m_dn_dt_bias': 'new_m', 'new_m_dn_out_norm_w': 'new_m', 'new_m_gm_ln_g': 'new_m', 'new_m_gm_ln_b': 'new_m', 'new_m_gm_w_s': 'new_m', 'new_m_gm_b_s': 'new_m', 'new_m_sw_q_norm_w': 'new_m', 'new_m_sw_k_norm_w': 'new_m', 'new_m_w_ffn_in': 'new_m', 'new_m_w_ffn_out': 'new_m', 'new_v_w_mod': 'new_v', 'new_v_b_mod': 'new_v', 'new_v_mix_norm_w': 'new_v', 'new_v_ffn_norm_w': 'new_v', 'new_v_w_in': 'new_v', 'new_v_w_out': 'new_v', 'new_v_dn_conv_w': 'new_v', 'new_v_dn_a_log': 'new_v', 'new_v_dn_dt_bias': 'new_v', 'new_v_dn_out_norm_w': 'new_v', 'new_v_gm_ln_g': 'new_v', 'new_v_gm_ln_b': 'new_v', 'new_v_gm_w_s': 'new_v', 'new_v_gm_b_s': 'new_v', 'new_v_sw_q_norm_w': 'new_v', 'new_v_sw_k_norm_w': 'new_v', 'new_v_w_ffn_in': 'new_v', 'new_v_w_ffn_out': 'new_v'}


def _forward(args):
    return _fwd_reference(*[args[k] for k in FWD_PARAMS])


def _output_shape():
    out = _jax.eval_shape(lambda: _forward(_fwd_setup_inputs(0)))
    return out.shape, out.dtype

N_MICROBATCH = 1
ADAM_LR = 0.001
ADAM_B1 = 0.9
ADAM_B2 = 0.999
ADAM_EPS = 1e-08
ADAM_WD = 0.01
ADAM_STEP = 10
PER_EXAMPLE_BATCH_AXIS = {'x': 0, 'c': 0, 'loss_target': 0}
SHARED_INPUTS = []
_WEIGHT_DTYPES = {'w_mod': _jnp.float32, 'b_mod': _jnp.float32, 'mix_norm_w': _jnp.float32, 'ffn_norm_w': _jnp.float32, 'w_in': _jnp.float32, 'w_out': _jnp.float32, 'dn_conv_w': _jnp.float32, 'dn_a_log': _jnp.float32, 'dn_dt_bias': _jnp.float32, 'dn_out_norm_w': _jnp.float32, 'gm_ln_g': _jnp.float32, 'gm_ln_b': _jnp.float32, 'gm_w_s': _jnp.float32, 'gm_b_s': _jnp.float32, 'sw_q_norm_w': _jnp.float32, 'sw_k_norm_w': _jnp.float32, 'w_ffn_in': _jnp.float32, 'w_ffn_out': _jnp.float32}
MOMENT_SCALE = {'w_mod': 9.864879e-01, 'b_mod': 2.130676e+00, 'mix_norm_w': 1.331080e+00, 'ffn_norm_w': 3.327552e+00, 'w_in': 1.312533e-01, 'w_out': 2.325267e-01, 'dn_conv_w': 1.553290e-01, 'dn_a_log': 3.292783e+00, 'dn_dt_bias': 3.142189e+00, 'dn_out_norm_w': 6.068110e+00, 'gm_ln_g': 7.803797e-01, 'gm_ln_b': 1.156441e-01, 'gm_w_s': 5.095654e-02, 'gm_b_s': 7.422666e-01, 'sw_q_norm_w': 7.892093e-02, 'sw_k_norm_w': 7.927867e-02, 'w_ffn_in': 7.095720e-02, 'w_ffn_out': 1.011517e-01}


def _to_microbatches(a, axis):
    t = _jnp.moveaxis(a, axis, 0)
    t = t.reshape((N_MICROBATCH, t.shape[0] // N_MICROBATCH) + t.shape[1:])
    return _jnp.moveaxis(t, 1, axis + 1)


def setup_inputs(seed: int = 0) -> dict:
    inp = _fwd_setup_inputs(seed)
    key = _jax.random.fold_in(_jax.random.key(seed), 7919)
    shape, _ = _output_shape()
    out = dict(inp)
    out["loss_target"] = _jax.random.normal(_jax.random.fold_in(key, 0), shape, _jnp.float32)
    for i, name in enumerate(TWIN_WEIGHTS):
        w = inp[name].astype(_jnp.float32)
        if MOMENT_SCALE is None:
            s = _jnp.sqrt(_jnp.mean(_jnp.square(w)) + 1e-30)
        else:
            s = MOMENT_SCALE[name]
        km, kv = _jax.random.split(_jax.random.fold_in(key, i + 1))
        out[name] = w
        out["m_" + name] = s * _jax.random.normal(km, w.shape, _jnp.float32)
        out["v_" + name] = (s * s) * _jax.random.uniform(kv, w.shape, _jnp.float32, 0.5, 1.5)
    if N_MICROBATCH > 1:
        for name, axis in PER_EXAMPLE_BATCH_AXIS.items():
            out[name] = _to_microbatches(out[name], axis)
    return {'x': out['x'], 'c': out['c'], 'w_mod': out['w_mod'], 'b_mod': out['b_mod'], 'mix_norm_w': out['mix_norm_w'], 'ffn_norm_w': out['ffn_norm_w'], 'w_in': out['w_in'], 'w_out': out['w_out'], 'dn_conv_w': out['dn_conv_w'], 'dn_a_log': out['dn_a_log'], 'dn_dt_bias': out['dn_dt_bias'], 'dn_out_norm_w': out['dn_out_norm_w'], 'gm_ln_g': out['gm_ln_g'], 'gm_ln_b': out['gm_ln_b'], 'gm_w_s': out['gm_w_s'], 'gm_b_s': out['gm_b_s'], 'sw_q_norm_w': out['sw_q_norm_w'], 'sw_k_norm_w': out['sw_k_norm_w'], 'w_ffn_in': out['w_ffn_in'], 'w_ffn_out': out['w_ffn_out'], 'loss_target': out['loss_target'], 'm_w_mod': out['m_w_mod'], 'm_b_mod': out['m_b_mod'], 'm_mix_norm_w': out['m_mix_norm_w'], 'm_ffn_norm_w': out['m_ffn_norm_w'], 'm_w_in': out['m_w_in'], 'm_w_out': out['m_w_out'], 'm_dn_conv_w': out['m_dn_conv_w'], 'm_dn_a_log': out['m_dn_a_log'], 'm_dn_dt_bias': out['m_dn_dt_bias'], 'm_dn_out_norm_w': out['m_dn_out_norm_w'], 'm_gm_ln_g': out['m_gm_ln_g'], 'm_gm_ln_b': out['m_gm_ln_b'], 'm_gm_w_s': out['m_gm_w_s'], 'm_gm_b_s': out['m_gm_b_s'], 'm_sw_q_norm_w': out['m_sw_q_norm_w'], 'm_sw_k_norm_w': out['m_sw_k_norm_w'], 'm_w_ffn_in': out['m_w_ffn_in'], 'm_w_ffn_out': out['m_w_ffn_out'], 'v_w_mod': out['v_w_mod'], 'v_b_mod': out['v_b_mod'], 'v_mix_norm_w': out['v_mix_norm_w'], 'v_ffn_norm_w': out['v_ffn_norm_w'], 'v_w_in': out['v_w_in'], 'v_w_out': out['v_w_out'], 'v_dn_conv_w': out['v_dn_conv_w'], 'v_dn_a_log': out['v_dn_a_log'], 'v_dn_dt_bias': out['v_dn_dt_bias'], 'v_dn_out_norm_w': out['v_dn_out_norm_w'], 'v_gm_ln_g': out['v_gm_ln_g'], 'v_gm_ln_b': out['v_gm_ln_b'], 'v_gm_w_s': out['v_gm_w_s'], 'v_gm_b_s': out['v_gm_b_s'], 'v_sw_q_norm_w': out['v_sw_q_norm_w'], 'v_sw_k_norm_w': out['v_sw_k_norm_w'], 'v_w_ffn_in': out['v_w_ffn_in'], 'v_w_ffn_out': out['v_w_ffn_out']}


def _loss(weights, diff, rest, loss_target):
    with _jax.named_scope("forward"):
        args = {**rest, TWIN_DIFF_INPUT: diff, **{k: w.astype(_WEIGHT_DTYPES[k]) for k, w in weights.items()}}
        y = _forward(args)
    with _jax.named_scope("loss_head"):
        err = _jnp.square(y.astype(_jnp.float32) - loss_target)
        return 0.5 * _jnp.sum(_jnp.mean(err, axis=-1)) if err.ndim else 0.5 * err


def _adamw(w, g, m, v):
    m = ADAM_B1 * m + (1.0 - ADAM_B1) * g
    v = ADAM_B2 * v + (1.0 - ADAM_B2) * _jnp.square(g)
    m_hat = m / (1.0 - ADAM_B1 ** ADAM_STEP)
    v_hat = v / (1.0 - ADAM_B2 ** ADAM_STEP)
    delta = -ADAM_LR * (m_hat / (_jnp.sqrt(v_hat) + ADAM_EPS) + ADAM_WD * w)
    return delta, m, v


def reference(x, c, w_mod, b_mod, mix_norm_w, ffn_norm_w, w_in, w_out, dn_conv_w, dn_a_log, dn_dt_bias, dn_out_norm_w, gm_ln_g, gm_ln_b, gm_w_s, gm_b_s, sw_q_norm_w, sw_k_norm_w, w_ffn_in, w_ffn_out, loss_target, m_w_mod, m_b_mod, m_mix_norm_w, m_ffn_norm_w, m_w_in, m_w_out, m_dn_conv_w, m_dn_a_log, m_dn_dt_bias, m_dn_out_norm_w, m_gm_ln_g, m_gm_ln_b, m_gm_w_s, m_gm_b_s, m_sw_q_norm_w, m_sw_k_norm_w, m_w_ffn_in, m_w_ffn_out, v_w_mod, v_b_mod, v_mix_norm_w, v_ffn_norm_w, v_w_in, v_w_out, v_dn_conv_w, v_dn_a_log, v_dn_dt_bias, v_dn_out_norm_w, v_gm_ln_g, v_gm_ln_b, v_gm_w_s, v_gm_b_s, v_sw_q_norm_w, v_sw_k_norm_w, v_w_ffn_in, v_w_ffn_out):
    given = dict(x=x, c=c, w_mod=w_mod, b_mod=b_mod, mix_norm_w=mix_norm_w, ffn_norm_w=ffn_norm_w, w_in=w_in, w_out=w_out, dn_conv_w=dn_conv_w, dn_a_log=dn_a_log, dn_dt_bias=dn_dt_bias, dn_out_norm_w=dn_out_norm_w, gm_ln_g=gm_ln_g, gm_ln_b=gm_ln_b, gm_w_s=gm_w_s, gm_b_s=gm_b_s, sw_q_norm_w=sw_q_norm_w, sw_k_norm_w=sw_k_norm_w, w_ffn_in=w_ffn_in, w_ffn_out=w_ffn_out, loss_target=loss_target, m_w_mod=m_w_mod, m_b_mod=m_b_mod, m_mix_norm_w=m_mix_norm_w, m_ffn_norm_w=m_ffn_norm_w, m_w_in=m_w_in, m_w_out=m_w_out, m_dn_conv_w=m_dn_conv_w, m_dn_a_log=m_dn_a_log, m_dn_dt_bias=m_dn_dt_bias, m_dn_out_norm_w=m_dn_out_norm_w, m_gm_ln_g=m_gm_ln_g, m_gm_ln_b=m_gm_ln_b, m_gm_w_s=m_gm_w_s, m_gm_b_s=m_gm_b_s, m_sw_q_norm_w=m_sw_q_norm_w, m_sw_k_norm_w=m_sw_k_norm_w, m_w_ffn_in=m_w_ffn_in, m_w_ffn_out=m_w_ffn_out, v_w_mod=v_w_mod, v_b_mod=v_b_mod, v_mix_norm_w=v_mix_norm_w, v_ffn_norm_w=v_ffn_norm_w, v_w_in=v_w_in, v_w_out=v_w_out, v_dn_conv_w=v_dn_conv_w, v_dn_a_log=v_dn_a_log, v_dn_dt_bias=v_dn_dt_bias, v_dn_out_norm_w=v_dn_out_norm_w, v_gm_ln_g=v_gm_ln_g, v_gm_ln_b=v_gm_ln_b, v_gm_w_s=v_gm_w_s, v_gm_b_s=v_gm_b_s, v_sw_q_norm_w=v_sw_q_norm_w, v_sw_k_norm_w=v_sw_k_norm_w, v_w_ffn_in=v_w_ffn_in, v_w_ffn_out=v_w_ffn_out)
    weights = {n: given[n] for n in TWIN_WEIGHTS}
    shared = {n: given[n] for n in SHARED_INPUTS}
    per_example = {n: given[n] for n in ['x', 'c']}
    grad_fn = _jax.value_and_grad(_loss, argnums=(0, 1))

    def one_microbatch(ex, loss_target):
        ex = dict(ex)
        diff = ex.pop(TWIN_DIFF_INPUT)
        return grad_fn(weights, diff, {**shared, **ex}, loss_target)

    if N_MICROBATCH == 1:
        loss, (grad_w, grad_x) = one_microbatch(per_example, given["loss_target"])
    else:
        def body(carry, xs):
            loss_sum, grad_sum = carry
            l_k, (gw_k, gx_k) = one_microbatch(xs[0], xs[1])
            with _jax.named_scope("update"):
                return (loss_sum + l_k, _jax.tree.map(_jnp.add, grad_sum, gw_k)), gx_k

        init = (_jnp.zeros((), _jnp.float32), _jax.tree.map(_jnp.zeros_like, weights))
        (loss, grad_w), grad_x = _jax.lax.scan(body, init, (per_example, given["loss_target"]))
    with _jax.named_scope("update"):
        delta_w, new_m, new_v = {}, {}, {}
        for n in TWIN_WEIGHTS:
            delta_w[n], new_m[n], new_v[n] = _adamw(weights[n], grad_w[n], given["m_" + n], given["v_" + n])
    return (loss, grad_x, *[grad_w[n] for n in TWIN_WEIGHTS], *[delta_w[n] for n in TWIN_WEIGHTS],
            *[new_m[n] for n in TWIN_WEIGHTS], *[new_v[n] for n in TWIN_WEIGHTS])
```

```python
import functools
import math

import jax
import jax.numpy as jnp
from jax import lax
from jax.experimental import pallas as pl
from jax.experimental.pallas import tpu as pltpu

F32 = jnp.float32
BF16 = jnp.bfloat16
HI = lax.Precision.HIGHEST

NORM_EPS = 1e-6
DN_HEADS = 4
DN_HEAD_DIM = 128
DN_WIDTH = 512
DN_CHUNK = 64
DN_CONV = 4
GM_WIDTH = 256
GM_GROUPS = 4
GM_GROUP_DIM = 64
GM_CHUNK = 128
SW_HEADS = 4
SW_HEAD_DIM = 64
SW_WIDTH = 256
SW_DILATIONS = (1, 4, 16)
SW_BLOCK = 128
ROPE_THETA = 500000.0
ROPE_DIM = 16
LANE = 128

C_QKV = 0
C_Z = 1536
C_UV = 2048
C_SW = 2560
C_AB = 4864
IN_WIDTH = 4872
IN_PAD = 4992

ADAM_LR = 0.001
ADAM_B1 = 0.9
ADAM_B2 = 0.999
ADAM_EPS = 1e-08
ADAM_WD = 0.01
ADAM_STEP = 10

MESH = pl.DeviceIdType.MESH


def _call(body, *, name, grid, in_specs, out_specs, out_shape, scratch_shapes=(), semantics=None):
    if semantics is None:
        semantics = ("arbitrary",) * len(grid)
    return pl.pallas_call(
        body, name=name, grid=grid, in_specs=in_specs, out_specs=out_specs, out_shape=out_shape,
        scratch_shapes=list(scratch_shapes),
        compiler_params=pltpu.CompilerParams(dimension_semantics=semantics),
    )


def _dot(a, b, ca, cb, prec=None):
    return lax.dot_general(a, b, (((ca,), (cb,)), ((), ())), preferred_element_type=F32, precision=prec)


def _bdot(a, b, ca=1, cb=0):
    return _dot(a.astype(BF16), b.astype(BF16), ca, cb)


def _hdot(a, b, ca=1, cb=0):
    return _dot(a.astype(F32), b.astype(F32), ca, cb, HI)


def _sigmoid(x):
    return 1.0 / (1.0 + jnp.exp(-x))


def _silu(x):
    return x * _sigmoid(x)


def _dsilu(x):
    s = _sigmoid(x)
    return s * (1.0 + x * (1.0 - s))


def _softplus(x):
    return jnp.maximum(x, 0.0) + jnp.log(1.0 + jnp.exp(-jnp.abs(x)))


def _iota2(shape, dim):
    return lax.broadcasted_iota(jnp.int32, shape, dim)


def _rowsum(x):
    return jnp.sum(x, axis=-1, keepdims=True)


def _colsum(x):
    return jnp.sum(x, axis=0, keepdims=True)


def _full(shape):
    return pl.BlockSpec(shape, lambda *_: (0,) * len(shape))


def _norm_mod(x, nw, scale, shift):
    r = lax.rsqrt(jnp.mean(x * x, axis=-1, keepdims=True) + NORM_EPS)
    xn = x * r
    return xn, r, (xn * nw) * (1.0 + scale) + shift


def norm_mm(x, nw, scale, shift, w, *, swiglu, name, tm=256):
    t, d = x.shape
    n = w.shape[1]
    half = n // 2

    def body(x_ref, nw_ref, sc_ref, sh_ref, w_ref, h_ref, y_ref, *act_ref):
        _, _, h = _norm_mod(x_ref[...], nw_ref[...], sc_ref[...], sh_ref[...])
        hb = h.astype(BF16)
        h_ref[...] = hb
        y = _dot(hb, w_ref[...], 1, 0)
        y_ref[...] = y
        if swiglu:
            act_ref[0][...] = (_silu(y[:, :half]) * y[:, half:]).astype(BF16)

    row = lambda i: (i, 0)
    out_shape = [jax.ShapeDtypeStruct((t, d), BF16), jax.ShapeDtypeStruct((t, n), F32)]
    out_specs = [pl.BlockSpec((tm, d), row), pl.BlockSpec((tm, n), row)]
    if swiglu:
        out_shape.append(jax.ShapeDtypeStruct((t, half), BF16))
        out_specs.append(pl.BlockSpec((tm, half), row))
    return _call(
        body, name=name, grid=(t // tm,),
        in_specs=[pl.BlockSpec((tm, d), row), _full((1, d)), _full((1, d)), _full((1, d)), _full((d, n))],
        out_specs=out_specs, out_shape=out_shape, semantics=("parallel",),
    )(x, nw, scale, shift, w)


def resid_mm(y, w, x, gate, *, name, tm=256):
    t, k = y.shape
    d = w.shape[1]

    def body(y_ref, w_ref, x_ref, g_ref, xo_ref, o_ref):
        o = _dot(y_ref[...].astype(BF16), w_ref[...], 1, 0)
        o_ref[...] = o
        xo_ref[...] = x_ref[...] + g_ref[...] * o

    row = lambda i: (i, 0)
    return _call(
        body, name=name, grid=(t // tm,),
        in_specs=[pl.BlockSpec((tm, k), row), _full((k, d)), pl.BlockSpec((tm, d), row), _full((1, d))],
        out_specs=[pl.BlockSpec((tm, d), row), pl.BlockSpec((tm, d), row)],
        out_shape=[jax.ShapeDtypeStruct((t, d), F32), jax.ShapeDtypeStruct((t, d), F32)],
        semantics=("parallel",),
    )(y, w, x, gate)


def resid_mm_bwd(dx, gate, o, w, gu, *, name, tm=256):
    t, d = dx.shape
    k = w.shape[0]
    swiglu = gu is not None

    def body(dx_ref, g_ref, o_ref, w_ref, *rest):
        if swiglu:
            gu_ref, dy_ref, gx_ref, dg_ref = rest
        else:
            dy_ref, gx_ref, dg_ref = rest
        i = pl.program_id(0)
        dxv = dx_ref[...]
        gx = (dxv * g_ref[...]).astype(BF16)
        gx_ref[...] = gx
        part = _colsum(dxv * o_ref[...])

        @pl.when(i == 0)
        def _():
            dg_ref[...] = jnp.zeros_like(dg_ref)

        dg_ref[...] += part
        da = _dot(gx, w_ref[...], 1, 1)
        if swiglu:
            g = gu_ref[:, :k]
            u = gu_ref[:, k:]
            dy_ref[:, :k] = (da * u * _dsilu(g)).astype(BF16)
            dy_ref[:, k:] = (da * _silu(g)).astype(BF16)
        else:
            dy_ref[...] = da

    row = lambda i: (i, 0)
    in_specs = [pl.BlockSpec((tm, d), row), _full((1, d)), pl.BlockSpec((tm, d), row), _full((k, d))]
    args = [dx, gate, o, w]
    if swiglu:
        in_specs.append(pl.BlockSpec((tm, 2 * k), row))
        args.append(gu)
        dy_shape = jax.ShapeDtypeStruct((t, 2 * k), BF16)
        dy_spec = pl.BlockSpec((tm, 2 * k), row)
    else:
        dy_shape = jax.ShapeDtypeStruct((t, k), F32)
        dy_spec = pl.BlockSpec((tm, k), row)
    return _call(
        body, name=name, grid=(t // tm,), in_specs=in_specs,
        out_specs=[dy_spec, pl.BlockSpec((tm, d), row), _full((1, d))],
        out_shape=[dy_shape, jax.ShapeDtypeStruct((t, d), BF16), jax.ShapeDtypeStruct((1, d), F32)],
    )(*args)


def norm_mm_bwd(dy, w, x, nw, scale, dres, *, name, tm=256):
    t, n = dy.shape
    d = x.shape[1]
    steps = t // tm

    def body(dy_ref, w_ref, x_ref, nw_ref, sc_ref, dres_ref, dx_ref, dnw_ref, dsc_ref, dsh_ref):
        i = pl.program_id(0)
        dh = _dot(dy_ref[...].astype(BF16), w_ref[...], 1, 1)
        x = x_ref[...]
        r = lax.rsqrt(jnp.mean(x * x, axis=-1, keepdims=True) + NORM_EPS)
        xn = x * r
        a = nw_ref[...] * (1.0 + sc_ref[...])

        @pl.when(i == 0)
        def _():
            dnw_ref[...] = jnp.zeros_like(dnw_ref)
            dsh_ref[...] = jnp.zeros_like(dsh_ref)

        dnw_ref[...] += _colsum(dh * xn)
        dsh_ref[...] += _colsum(dh)
        dxn = dh * a
        dx_ref[...] = r * (dxn - xn * jnp.mean(dxn * xn, axis=-1, keepdims=True)) + dres_ref[...]

        @pl.when(i == steps - 1)
        def _():
            da = dnw_ref[...]
            dsc_ref[...] = da * nw_ref[...]
            dnw_ref[...] = da * (1.0 + sc_ref[...])

    row = lambda i: (i, 0)
    vec = jax.ShapeDtypeStruct((1, d), F32)
    return _call(
        body, name=name, grid=(steps,),
        in_specs=[pl.BlockSpec((tm, n), row), _full((d, n)), pl.BlockSpec((tm, d), row), _full((1, d)),
                  _full((1, d)), pl.BlockSpec((tm, d), row)],
        out_specs=[pl.BlockSpec((tm, d), row), _full((1, d)), _full((1, d)), _full((1, d))],
        out_shape=[jax.ShapeDtypeStruct((t, d), F32), vec, vec, vec],
    )(dy, w, x, nw, scale, dres)


def _pick_tn(n, k, budget=6 << 20):
    best = LANE
    for m in range(1, n // LANE + 1):
        tn = m * LANE
        if n % tn == 0 and k * tn * 4 <= budget:
            best = tn
    return best


def mm_tn(a, g, *, name, tt=512):
    t, k = a.shape
    n = g.shape[1]
    tn = _pick_tn(n, k)

    def body(a_ref, g_ref, o_ref):
        @pl.when(pl.program_id(1) == 0)
        def _():
            o_ref[...] = jnp.zeros_like(o_ref)

        o_ref[...] += _dot(a_ref[...].astype(BF16), g_ref[...].astype(BF16), 0, 0)

    return _call(
        body, name=name, grid=(n // tn, t // tt),
        in_specs=[pl.BlockSpec((tt, k), lambda j, i: (i, 0)), pl.BlockSpec((tt, tn), lambda j, i: (i, j))],
        out_specs=pl.BlockSpec((k, tn), lambda j, i: (0, j)),
        out_shape=jax.ShapeDtypeStruct((k, n), F32), semantics=("parallel", "arbitrary"),
    )(a, g)


def loss_head(y, target, *, tm=512):
    t, d = y.shape
    steps = t // tm

    def body(y_ref, t_ref, dy_ref, l_ref, acc_ref):
        i = pl.program_id(0)

        @pl.when(i == 0)
        def _():
            acc_ref[...] = jnp.zeros_like(acc_ref)

        e = y_ref[...] - t_ref[...]
        dy_ref[...] = e * (1.0 / d)
        acc_ref[...] += _colsum(e * e)

        @pl.when(i == steps - 1)
        def _():
            tot = jnp.sum(acc_ref[...], axis=-1, keepdims=True) * (0.5 / d)
            l_ref[...] = jnp.broadcast_to(tot, l_ref.shape)

    row = lambda i: (i, 0)
    return _call(
        body, name="loss_head", grid=(steps,),
        in_specs=[pl.BlockSpec((tm, d), row), pl.BlockSpec((tm, d), row)],
        out_specs=[pl.BlockSpec((tm, d), row), _full((8, LANE))],
        out_shape=[jax.ShapeDtypeStruct((t, d), F32), jax.ShapeDtypeStruct((8, LANE), F32)],
        scratch_shapes=[pltpu.VMEM((1, d), F32)],
    )(y, target)


def _shift_rows(x, s):
    if s == 0:
        return x
    t = x.shape[0]
    ri = _iota2(x.shape, 0)
    rolled = pltpu.roll(x, s % t, axis=0)
    if s > 0:
        return jnp.where(ri >= s, rolled, 0.0)
    return jnp.where(ri < t + s, rolled, 0.0)


def _conv_pre(x, w):
    acc = x * w[DN_CONV - 1:DN_CONV, :]
    for j in range(DN_CONV - 1):
        acc = acc + _shift_rows(x, DN_CONV - 1 - j) * w[j:j + 1, :]
    return acc


def dn_conv(proj, conv_w):
    t = proj.shape[0]
    width = 3 * DN_WIDTH

    def body(x_ref, w_ref, o_ref):
        o_ref[...] = _silu(_conv_pre(x_ref[...], w_ref[...]))

    col = lambda j: (0, j)
    return _call(
        body, name="dn_conv", grid=(width // LANE,),
        in_specs=[pl.BlockSpec((t, LANE), col), pl.BlockSpec((8, LANE), col)],
        out_specs=pl.BlockSpec((t, LANE), col),
        out_shape=jax.ShapeDtypeStruct((t, width), F32), semantics=("parallel",),
    )(proj, conv_w)


def dn_conv_bwd(proj, conv_w, dact):
    t = proj.shape[0]
    width = 3 * DN_WIDTH

    def body(x_ref, w_ref, d_ref, dx_ref, dw_ref):
        x = x_ref[...]
        w = w_ref[...]
        dc = d_ref[...] * _dsilu(_conv_pre(x, w))
        dx = dc * w[DN_CONV - 1:DN_CONV, :]
        rows = []
        for j in range(DN_CONV - 1):
            s = DN_CONV - 1 - j
            dx = dx + _shift_rows(dc, -s) * w[j:j + 1, :]
            rows.append(_colsum(dc * _shift_rows(x, s)))
        rows.append(_colsum(dc * x))
        dx_ref[...] = dx
        ri = _iota2((8, LANE), 0)
        dw = jnp.zeros((8, LANE), F32)
        for j in range(DN_CONV):
            dw = dw + jnp.where(ri == j, rows[j], 0.0)
        dw_ref[...] = dw

    col = lambda j: (0, j)
    return _call(
        body, name="dn_conv_bwd", grid=(width // LANE,),
        in_specs=[pl.BlockSpec((t, LANE), col), pl.BlockSpec((8, LANE), col), pl.BlockSpec((t, LANE), col)],
        out_specs=[pl.BlockSpec((t, LANE), col), pl.BlockSpec((8, LANE), col)],
        out_shape=[jax.ShapeDtypeStruct((t, width), F32), jax.ShapeDtypeStruct((8, width), F32)],
        semantics=("parallel",),
    )(proj, conv_w, dact)


def _inv_unit_lower(a):
    c = a.shape[0]
    eye = (_iota2((c, c), 0) == _iota2((c, c), 1)).astype(F32)
    x = eye - a
    p = _hdot(a, a)
    steps = int(math.log2(c)) - 1
    for i in range(steps):
        x = x + _hdot(x, p)
        if i < steps - 1:
            p = _hdot(p, p)
    return x


def _dn_chunk(q, k, v, a, b, alog, dtb, s_in):
    c, d = q.shape
    rq = lax.rsqrt(_rowsum(q * q) + NORM_EPS)
    rk = lax.rsqrt(_rowsum(k * k) + NORM_EPS)
    qh = q * rq
    kn = k * rk
    qs = qh * (d ** -0.5)
    g = -jnp.exp(alog) * _softplus(a + dtb)
    beta = _sigmoid(b)
    ri = _iota2((c, c), 0)
    ci = _iota2((c, c), 1)
    causal = ri >= ci
    strict = ri > ci
    gb = jnp.broadcast_to(g, (c, d))
    gcb = _hdot(causal.astype(F32), gb)
    gc = gcb[:, :1]
    gl = _colsum(gb)[:, :1]
    dec = jnp.exp(jnp.where(causal, gc - gcb.T[:c, :], -1e30))
    kb = kn * beta
    amat = jnp.where(strict, _bdot(kb, kn, 1, 1) * dec, 0.0)
    tinv = _inv_unit_lower(amat)
    e = jnp.exp(gc)
    f = jnp.exp(gl - gc)
    rw = kb * e
    sol = _hdot(tinv, jnp.concatenate([v * beta, rw], axis=1))
    u = sol[:, :d]
    w = sol[:, d:]
    pmat = jnp.where(causal, _bdot(qs, kn, 1, 1) * dec, 0.0)
    qd = qs * e
    kd = kn * f
    vnew = u - _bdot(w, s_in)
    o = _bdot(qd, s_in) + _bdot(pmat, vnew)
    s_out = s_in * jnp.exp(gl) + _bdot(kd, vnew, 0, 0)
    return dict(rq=rq, rk=rk, qh=qh, kn=kn, qs=qs, g=g, beta=beta, causal=causal, strict=strict, gl=gl,
                dec=dec, kb=kb, amat=amat, tinv=tinv, e=e, f=f, rw=rw, u=u, w=w, pmat=pmat, qd=qd, kd=kd,
                vnew=vnew, o=o, s_out=s_out)


def _dn_chunk_bwd(m, q, v, a, alog, dtb, s_in, do, ds_out):
    c, d = q.shape
    kn, qs, kb, u, w, e, f = m["kn"], m["qs"], m["kb"], m["u"], m["w"], m["e"], m["f"]
    beta, dec, tinv, vnew, kd, qd = m["beta"], m["dec"], m["tinv"], m["vnew"], m["kd"], m["qd"]
    el = jnp.exp(m["gl"])
    dvnew = _bdot(m["pmat"], do, 0, 0) + _bdot(kd, ds_out)
    dp = jnp.where(m["causal"], _bdot(do, vnew, 1, 1), 0.0)
    dqd = _bdot(do, s_in, 1, 1)
    dkd = _bdot(vnew, ds_out, 1, 1)
    ds_in = _bdot(qd, do, 0, 0) + el * ds_out - _bdot(w, dvnew, 0, 0)
    dgl = el * jnp.sum(_rowsum(s_in * ds_out), axis=0, keepdims=True)
    dw = -_bdot(dvnew, s_in, 1, 1)
    dsol = _hdot(tinv, jnp.concatenate([dvnew, dw], axis=1), 0, 0)
    dru = dsol[:, :d]
    drw = dsol[:, d:]
    da_m = -jnp.where(m["strict"], _bdot(dsol, jnp.concatenate([u, w], axis=1), 1, 1), 0.0)
    db_m = da_m * dec
    dq_m = dp * dec
    dkb = _bdot(db_m, kn)
    dkn = _bdot(db_m, kb, 0, 0) + _bdot(dq_m, qs, 0, 0)
    dqs = _bdot(dq_m, kn)
    gmat = da_m * m["amat"] + dp * m["pmat"]
    ones = jnp.ones((c, d), F32)
    dgam = (_hdot(gmat, ones) - _hdot(gmat, ones, 0, 0))[:, :1]
    dqs = dqs + dqd * e
    dgam = dgam + _rowsum(dqd * qd)
    dkn = dkn + dkd * f
    tk = _rowsum(dkd * kd)
    dgam = dgam - tk
    dgl = dgl + jnp.sum(tk, axis=0, keepdims=True)
    dkb = dkb + drw * e
    dgam = dgam + _rowsum(drw * m["rw"])
    dv = dru * beta
    dbeta = _rowsum(dru * v) + _rowsum(dkb * kn)
    dkn = dkn + dkb * beta
    last = (_iota2((c, 1), 0) == c - 1).astype(F32)
    dgam = dgam + last * dgl
    upper = (_iota2((c, c), 0) <= _iota2((c, c), 1)).astype(F32)
    dg = _hdot(upper, jnp.broadcast_to(dgam, (c, d)))[:, :1]
    dqh = dqs * (d ** -0.5)
    dq = m["rq"] * (dqh - m["qh"] * _rowsum(dqh * m["qh"]))
    dk = m["rk"] * (dkn - kn * _rowsum(dkn * kn))
    sg = _sigmoid(a + dtb)
    da = dg * (-jnp.exp(alog)) * sg
    dalog = jnp.sum(dg * m["g"], axis=0, keepdims=True)
    ddtb = jnp.sum(da, axis=0, keepdims=True)
    db = dbeta * beta * (1.0 - beta)
    return dq, dk, dv, da, db, dalog, ddtb, ds_in


def _dn_gate(o, z, wn):
    ro = lax.rsqrt(jnp.mean(o * o, axis=-1, keepdims=True) + NORM_EPS)
    n = o * ro
    return n, ro, n * wn * _silu(z)


def dn_fwd(act, proj, scal, wn):
    t = act.shape[0]
    n = t // DN_CHUNK
    d = DN_HEAD_DIM

    def body(act_ref, z_ref, ab_ref, sc_ref, wn_ref, y_ref, st_ref, s_ref):
        @pl.when(pl.program_id(0) == 0)
        def _():
            s_ref[...] = jnp.zeros_like(s_ref)

        ab = ab_ref[...]
        sc = sc_ref[...]
        for h in range(DN_HEADS):
            q = act_ref[:, h * d:(h + 1) * d]
            k = act_ref[:, DN_WIDTH + h * d:DN_WIDTH + (h + 1) * d]
            v = act_ref[:, 2 * DN_WIDTH + h * d:2 * DN_WIDTH + (h + 1) * d]
            s_in = s_ref[h]
            st_ref[0, h] = s_in
            m = _dn_chunk(q, k, v, ab[:, h:h + 1], ab[:, DN_HEADS + h:DN_HEADS + h + 1],
                          sc[0:1, h:h + 1], sc[1:2, h:h + 1], s_in)
            s_ref[h] = m["s_out"]
            y_ref[:, h * d:(h + 1) * d] = _dn_gate(m["o"], z_ref[:, h * d:(h + 1) * d], wn_ref[...])[2]

    return _call(
        body, name="dn_fwd", grid=(n,),
        in_specs=[pl.BlockSpec((DN_CHUNK, 3 * DN_WIDTH), lambda i: (i, 0)),
                  pl.BlockSpec((DN_CHUNK, DN_WIDTH), lambda i: (i, C_Z // DN_WIDTH)),
                  pl.BlockSpec((DN_CHUNK, LANE), lambda i: (i, C_AB // LANE)),
                  _full((8, LANE)), _full((1, d))],
        out_specs=[pl.BlockSpec((DN_CHUNK, DN_WIDTH), lambda i: (i, 0)),
                   pl.BlockSpec((1, DN_HEADS, d, d), lambda i: (i, 0, 0, 0))],
        out_shape=[jax.ShapeDtypeStruct((t, DN_WIDTH), F32), jax.ShapeDtypeStruct((n, DN_HEADS, d, d), F32)],
        scratch_shapes=[pltpu.VMEM((DN_HEADS, d, d), F32)],
    )(act, proj, proj, scal, wn)


def dn_bwd(act, proj, scal, wn, states, dy):
    t = act.shape[0]
    n = t // DN_CHUNK
    d = DN_HEAD_DIM

    def body(act_ref, z_ref, ab_ref, sc_ref, wn_ref, st_ref, dy_ref, dact_ref, dz_ref, dab_ref, dpar_ref, ds_ref):
        @pl.when(pl.program_id(0) == 0)
        def _():
            ds_ref[...] = jnp.zeros_like(ds_ref)
            dpar_ref[...] = jnp.zeros_like(dpar_ref)

        ab = ab_ref[...]
        sc = sc_ref[...]
        wnv = wn_ref[...]
        lane = _iota2((DN_CHUNK, LANE), 1)
        prow = _iota2((8, LANE), 0)
        plane = _iota2((8, LANE), 1)
        dab = jnp.zeros((DN_CHUNK, LANE), F32)
        dpar = jnp.zeros((8, LANE), F32)
        for h in range(DN_HEADS):
            q = act_ref[:, h * d:(h + 1) * d]
            k = act_ref[:, DN_WIDTH + h * d:DN_WIDTH + (h + 1) * d]
            v = act_ref[:, 2 * DN_WIDTH + h * d:2 * DN_WIDTH + (h + 1) * d]
            a = ab[:, h:h + 1]
            b = ab[:, DN_HEADS + h:DN_HEADS + h + 1]
            alog = sc[0:1, h:h + 1]
            dtb = sc[1:2, h:h + 1]
            s_in = st_ref[0, h]
            z = z_ref[:, h * d:(h + 1) * d]
            dyh = dy_ref[:, h * d:(h + 1) * d]
            m = _dn_chunk(q, k, v, a, b, alog, dtb, s_in)
            nrm, ro, _ = _dn_gate(m["o"], z, wnv)
            sz = _silu(z)
            dz_ref[:, h * d:(h + 1) * d] = dyh * nrm * wnv * _dsilu(z)
            dn = dyh * wnv * sz
            dwn = _colsum(dyh * nrm * sz)
            do = ro * (dn - nrm * jnp.mean(dn * nrm, axis=-1, keepdims=True))
            dq, dk, dv, da, db, dalog, ddtb, ds_in = _dn_chunk_bwd(m, q, v, a, alog, dtb, s_in, do, ds_ref[h])
            ds_ref[h] = ds_in
            dact_ref[:, h * d:(h + 1) * d] = dq
            dact_ref[:, DN_WIDTH + h * d:DN_WIDTH + (h + 1) * d] = dk
            dact_ref[:, 2 * DN_WIDTH + h * d:2 * DN_WIDTH + (h + 1) * d] = dv
            dab = dab + jnp.where(lane == h, da, 0.0) + jnp.where(lane == DN_HEADS + h, db, 0.0)
            dpar = dpar + jnp.where((prow == 0) & (plane == h), dalog, 0.0)
            dpar = dpar + jnp.where((prow == 1) & (plane == h), ddtb, 0.0)
            dpar = dpar + jnp.where(prow == 2, dwn, 0.0)
        dab_ref[...] = dab
        dpar_ref[...] += dpar

    rev = lambda i: (n - 1 - i, 0)
    return _call(
        body, name="dn_bwd", grid=(n,),
        in_specs=[pl.BlockSpec((DN_CHUNK, 3 * DN_WIDTH), rev),
                  pl.BlockSpec((DN_CHUNK, DN_WIDTH), lambda i: (n - 1 - i, C_Z // DN_WIDTH)),
                  pl.BlockSpec((DN_CHUNK, LANE), lambda i: (n - 1 - i, C_AB // LANE)),
                  _full((8, LANE)), _full((1, d)),
                  pl.BlockSpec((1, DN_HEADS, d, d), lambda i: (n - 1 - i, 0, 0, 0)),
                  pl.BlockSpec((DN_CHUNK, DN_WIDTH), rev)],
        out_specs=[pl.BlockSpec((DN_CHUNK, 3 * DN_WIDTH), rev), pl.BlockSpec((DN_CHUNK, DN_WIDTH), rev),
                   pl.BlockSpec((DN_CHUNK, LANE), rev), _full((8, LANE))],
        out_shape=[jax.ShapeDtypeStruct((t, 3 * DN_WIDTH), F32), jax.ShapeDtypeStruct((t, DN_WIDTH), F32),
                   jax.ShapeDtypeStruct((t, LANE), F32), jax.ShapeDtypeStruct((8, LANE), F32)],
        scratch_shapes=[pltpu.VMEM((DN_HEADS, d, d), F32)],
    )(act, proj, proj, scal, wn, states, dy)


_INV_SQRT2 = 0.7071067811865476
_INV_SQRT2PI = 0.3989422804014327


def _gelu(x):
    return 0.5 * x * (1.0 + lax.erf(x * _INV_SQRT2))


def _dgelu(x):
    return 0.5 * (1.0 + lax.erf(x * _INV_SQRT2)) + x * jnp.exp(-0.5 * x * x) * _INV_SQRT2PI


def _gm_core(uv, lng, lnb, ws_ref, bst):
    c = uv.shape[0]
    zz = _gelu(uv)
    u = zz[:, :GM_WIDTH]
    vv = zz[:, GM_WIDTH:]
    xc = vv - jnp.mean(vv, axis=-1, keepdims=True)
    rs = lax.rsqrt(jnp.mean(xc * xc, axis=-1, keepdims=True) + NORM_EPS)
    xh = xc * rs
    vn = xh * lng + lnb
    grp = _iota2((c, GM_WIDTH), 1) // GM_GROUP_DIM
    tril = _iota2((c, c), 0) >= _iota2((c, c), 1)
    sv = jnp.zeros((c, GM_WIDTH), F32)
    masks = []
    for g in range(GM_GROUPS):
        mk = grp == g
        masks.append(mk)
        ws = jnp.where(tril, ws_ref[g], 0.0)
        sv = sv + _bdot(ws, jnp.where(mk, vn, 0.0)) + jnp.where(mk, bst[:, g:g + 1], 0.0)
    return u, xh, rs, vn, sv, masks, tril


def gm_fwd(proj, lng, lnb, w_s, bst):
    t = proj.shape[0]

    def body(uv_ref, g_ref, b_ref, ws_ref, bst_ref, y_ref):
        u, _, _, _, sv, _, _ = _gm_core(uv_ref[...], g_ref[...], b_ref[...], ws_ref, bst_ref[...])
        y_ref[...] = u * sv

    return _call(
        body, name="gm_fwd", grid=(t // GM_CHUNK,),
        in_specs=[pl.BlockSpec((GM_CHUNK, 2 * GM_WIDTH), lambda i: (i, C_UV // (2 * GM_WIDTH))),
                  _full((1, GM_WIDTH)), _full((1, GM_WIDTH)), _full((GM_GROUPS, GM_CHUNK, GM_CHUNK)),
                  _full((GM_CHUNK, LANE))],
        out_specs=pl.BlockSpec((GM_CHUNK, GM_WIDTH), lambda i: (i, 0)),
        out_shape=jax.ShapeDtypeStruct((t, GM_WIDTH), F32), semantics=("parallel",),
    )(proj, lng, lnb, w_s, bst)


def gm_bwd(proj, lng, lnb, w_s, bst, dy, dy_col=0):
    t = proj.shape[0]

    def body(uv_ref, g_ref, b_ref, ws_ref, bst_ref, dy_ref, duv_ref, dws_ref, dbst_ref, dln_ref):
        @pl.when(pl.program_id(0) == 0)
        def _():
            dws_ref[...] = jnp.zeros_like(dws_ref)
            dbst_ref[...] = jnp.zeros_like(dbst_ref)
            dln_ref[...] = jnp.zeros_like(dln_ref)

        uv = uv_ref[...]
        lng = g_ref[...]
        u, xh, rs, vn, sv, masks, tril = _gm_core(uv, lng, b_ref[...], ws_ref, bst_ref[...])
        dyv = dy_ref[...]
        dsv = dyv * u
        lane = _iota2((GM_CHUNK, LANE), 1)
        dvn = jnp.zeros_like(dsv)
        dbst = jnp.zeros((GM_CHUNK, LANE), F32)
        for g in range(GM_GROUPS):
            ws = jnp.where(tril, ws_ref[g], 0.0)
            dsg = jnp.where(masks[g], dsv, 0.0)
            dvn = dvn + jnp.where(masks[g], _bdot(ws, dsv, 0, 0), 0.0)
            dws_ref[g] += jnp.where(tril, _bdot(dsg, vn, 1, 1), 0.0)
            dbst = dbst + jnp.where(lane == g, _rowsum(dsg), 0.0)
        dbst_ref[...] += dbst
        row = _iota2((8, GM_WIDTH), 0)
        dln_ref[...] += jnp.where(row == 0, _colsum(dvn * xh), 0.0) + jnp.where(row == 1, _colsum(dvn), 0.0)
        dxh = dvn * lng
        dvv = rs * (dxh - jnp.mean(dxh, axis=-1, keepdims=True) - xh * jnp.mean(dxh * xh, axis=-1, keepdims=True))
        dg = _dgelu(uv)
        duv_ref[:, :GM_WIDTH] = dyv * sv * dg[:, :GM_WIDTH]
        duv_ref[:, GM_WIDTH:] = dvv * dg[:, GM_WIDTH:]

    return _call(
        body, name="gm_bwd", grid=(t // GM_CHUNK,),
        in_specs=[pl.BlockSpec((GM_CHUNK, 2 * GM_WIDTH), lambda i: (i, C_UV // (2 * GM_WIDTH))),
                  _full((1, GM_WIDTH)), _full((1, GM_WIDTH)), _full((GM_GROUPS, GM_CHUNK, GM_CHUNK)),
                  _full((GM_CHUNK, LANE)), pl.BlockSpec((GM_CHUNK, GM_WIDTH), lambda i: (i, dy_col))],
        out_specs=[pl.BlockSpec((GM_CHUNK, 2 * GM_WIDTH), lambda i: (i, 0)),
                   _full((GM_GROUPS, GM_CHUNK, GM_CHUNK)), _full((GM_CHUNK, LANE)), _full((8, GM_WIDTH))],
        out_shape=[jax.ShapeDtypeStruct((t, 2 * GM_WIDTH), F32),
                   jax.ShapeDtypeStruct((GM_GROUPS, GM_CHUNK, GM_CHUNK), F32),
                   jax.ShapeDtypeStruct((GM_CHUNK, LANE), F32), jax.ShapeDtypeStruct((8, GM_WIDTH), F32)],
    )(proj, lng, lnb, w_s, bst, dy)


def _head_mats():
    r = _iota2((SW_WIDTH, SW_WIDTH), 0)
    c = _iota2((SW_WIDTH, SW_WIDTH), 1)
    same = (r // SW_HEAD_DIM) == (c // SW_HEAD_DIM)
    cc = c % SW_HEAD_DIM
    half = ROPE_DIM // 2
    rot = jnp.where((cc < half) & (r == c + half), -1.0, 0.0) + jnp.where((cc >= half) & (cc < ROPE_DIM) & (r == c - half), 1.0, 0.0)
    return same.astype(F32), rot


def _seg_col(s):
    return C_SW // SW_WIDTH + (s // 2) * 3 + s % 2


def sw_prep(proj, nw2, cos_t, sin_t, *, tm=512):
    t = proj.shape[0]

    def body(x_ref, w_ref, c_ref, s_ref, o_ref):
        same, rot = _head_mats()
        x = x_ref[...]
        r = lax.rsqrt(_hdot(x * x, same) * (1.0 / SW_HEAD_DIM) + NORM_EPS)
        xn = x * r * w_ref[0]
        o_ref[0] = xn * c_ref[...] + _hdot(xn, rot) * s_ref[...]

    return _call(
        body, name="sw_prep", grid=(6, t // tm),
        in_specs=[pl.BlockSpec((tm, SW_WIDTH), lambda s, i: (i, _seg_col(s))),
                  pl.BlockSpec((1, 1, SW_WIDTH), lambda s, i: (s % 2, 0, 0)),
                  pl.BlockSpec((tm, SW_WIDTH), lambda s, i: (i, 0)),
                  pl.BlockSpec((tm, SW_WIDTH), lambda s, i: (i, 0))],
        out_specs=pl.BlockSpec((1, tm, SW_WIDTH), lambda s, i: (s, i, 0)),
        out_shape=jax.ShapeDtypeStruct((6, t, SW_WIDTH), F32), semantics=("parallel", "parallel"),
    )(proj, nw2, cos_t, sin_t)


def sw_prep_bwd(proj, nw2, cos_t, sin_t, dqk, *, tm=512):
    t = proj.shape[0]

    def body(x_ref, w_ref, c_ref, s_ref, d_ref, dx_ref, dw_ref):
        @pl.when(pl.program_id(1) == 0)
        def _():
            dw_ref[...] = jnp.zeros_like(dw_ref)

        same, rot = _head_mats()
        x = x_ref[...]
        w = w_ref[0]
        r = lax.rsqrt(_hdot(x * x, same) * (1.0 / SW_HEAD_DIM) + NORM_EPS)
        xh = x * r
        dout = d_ref[0]
        dxn = dout * c_ref[...] + _hdot(dout * s_ref[...], rot, 1, 1)
        dw_ref[0] += _colsum(dxn * xh)
        dxh = dxn * w
        dx_ref[0] = r * (dxh - xh * (_hdot(dxh * xh, same) * (1.0 / SW_HEAD_DIM)))

    return _call(
        body, name="sw_prep_bwd", grid=(6, t // tm),
        in_specs=[pl.BlockSpec((tm, SW_WIDTH), lambda s, i: (i, _seg_col(s))),
                  pl.BlockSpec((1, 1, SW_WIDTH), lambda s, i: (s % 2, 0, 0)),
                  pl.BlockSpec((tm, SW_WIDTH), lambda s, i: (i, 0)),
                  pl.BlockSpec((tm, SW_WIDTH), lambda s, i: (i, 0)),
                  pl.BlockSpec((1, tm, SW_WIDTH), lambda s, i: (s, i, 0))],
        out_specs=[pl.BlockSpec((1, tm, SW_WIDTH), lambda s, i: (s, i, 0)),
                   pl.BlockSpec((1, 1, SW_WIDTH), lambda s, i: (s, 0, 0))],
        out_shape=[jax.ShapeDtypeStruct((6, t, SW_WIDTH), F32), jax.ShapeDtypeStruct((6, 1, SW_WIDTH), F32)],
        semantics=("parallel", "arbitrary"),
    )(proj, nw2, cos_t, sin_t, dqk)


_SW_SCALE = SW_HEAD_DIM ** -0.5
_NEG = -1e30


def _sw_masks(has_prev):
    ri = _iota2((SW_BLOCK, SW_BLOCK), 0)
    ci = _iota2((SW_BLOCK, SW_BLOCK), 1)
    return ri >= ci, (ci >= ri) & has_prev


def _blk(j):
    return pl.BlockSpec((SW_HEADS, SW_BLOCK, SW_HEAD_DIM), j)


def sw_attn(q, k, v, nbs, *, name):
    _, t, _ = q.shape
    nblk = t // SW_BLOCK

    def body(q_ref, kc_ref, kp_ref, vc_ref, vp_ref, o_ref, l_ref):
        j = pl.program_id(0)
        mc, mp = _sw_masks((j % nbs) != 0)
        for h in range(SW_HEADS):
            qh = q_ref[h]
            sc = jnp.where(mc, _bdot(qh, kc_ref[h], 1, 1) * _SW_SCALE, _NEG)
            sp = jnp.where(mp, _bdot(qh, kp_ref[h], 1, 1) * _SW_SCALE, _NEG)
            mx = jnp.maximum(jnp.max(sc, axis=-1, keepdims=True), jnp.max(sp, axis=-1, keepdims=True))
            pc = jnp.exp(sc - mx)
            pp = jnp.exp(sp - mx)
            den = _rowsum(pc) + _rowsum(pp)
            o_ref[h] = (_bdot(pc, vc_ref[h]) + _bdot(pp, vp_ref[h])) / den
            l_ref[h] = jnp.broadcast_to(mx + jnp.log(den), (SW_BLOCK, SW_HEAD_DIM))

    cur = lambda j: (0, j, 0)
    prev = lambda j: (0, jnp.maximum(j - 1, 0), 0)
    shp = jax.ShapeDtypeStruct((SW_HEADS, t, SW_HEAD_DIM), F32)
    return _call(
        body, name=name, grid=(nblk,),
        in_specs=[_blk(cur), _blk(cur), _blk(prev), _blk(cur), _blk(prev)],
        out_specs=[_blk(cur), _blk(cur)], out_shape=[shp, shp], semantics=("parallel",),
    )(q, k, k, v, v)


def sw_attn_dq(q, k, v, do, lg, dm, nbs, *, name):
    _, t, _ = q.shape
    nblk = t // SW_BLOCK

    def body(q_ref, kc_ref, kp_ref, vc_ref, vp_ref, do_ref, l_ref, d_ref, dq_ref):
        j = pl.program_id(0)
        mc, mp = _sw_masks((j % nbs) != 0)
        for h in range(SW_HEADS):
            qh = q_ref[h]
            doh = do_ref[h]
            lse = l_ref[h][:, :1]
            dd = d_ref[h][:, :1]
            pc = jnp.exp(jnp.where(mc, _bdot(qh, kc_ref[h], 1, 1) * _SW_SCALE, _NEG) - lse)
            pp = jnp.exp(jnp.where(mp, _bdot(qh, kp_ref[h], 1, 1) * _SW_SCALE, _NEG) - lse)
            dsc = pc * (_bdot(doh, vc_ref[h], 1, 1) - dd)
            dsp = pp * (_bdot(doh, vp_ref[h], 1, 1) - dd)
            dq_ref[h] = (_bdot(dsc, kc_ref[h]) + _bdot(dsp, kp_ref[h])) * _SW_SCALE

    cur = lambda j: (0, j, 0)
    prev = lambda j: (0, jnp.maximum(j - 1, 0), 0)
    return _call(
        body, name=name, grid=(nblk,),
        in_specs=[_blk(cur), _blk(cur), _blk(prev), _blk(cur), _blk(prev), _blk(cur), _blk(cur), _blk(cur)],
        out_specs=_blk(cur), out_shape=jax.ShapeDtypeStruct((SW_HEADS, t, SW_HEAD_DIM), F32),
        semantics=("parallel",),
    )(q, k, k, v, v, do, lg, dm)


def sw_attn_dkv(q, k, v, do, lg, dm, nbs, *, name):
    _, t, _ = q.shape
    nblk = t // SW_BLOCK

    def body(k_ref, v_ref, qc_ref, qn_ref, doc_ref, don_ref, lc_ref, ln_ref, dc_ref, dn_ref, dk_ref, dv_ref):
        j = pl.program_id(0)
        mc, mn = _sw_masks((j + 1 < nblk) & (((j + 1) % nbs) != 0))
        for h in range(SW_HEADS):
            kh = k_ref[h]
            vh = v_ref[h]
            dk = jnp.zeros((SW_BLOCK, SW_HEAD_DIM), F32)
            dv = jnp.zeros((SW_BLOCK, SW_HEAD_DIM), F32)
            for q_ref, do_ref, l_ref, d_ref, mk in ((qc_ref, doc_ref, lc_ref, dc_ref, mc),
                                                    (qn_ref, don_ref, ln_ref, dn_ref, mn)):
                qh = q_ref[h]
                doh = do_ref[h]
                p = jnp.exp(jnp.where(mk, _bdot(qh, kh, 1, 1) * _SW_SCALE, _NEG) - l_ref[h][:, :1])
                dv = dv + _bdot(p, doh, 0, 0)
                ds = p * (_bdot(doh, vh, 1, 1) - d_ref[h][:, :1])
                dk = dk + _bdot(ds, qh, 0, 0)
            dk_ref[h] = dk * _SW_SCALE
            dv_ref[h] = dv

    cur = lambda j: (0, j, 0)
    nxt = lambda j: (0, jnp.minimum(j + 1, nblk - 1), 0)
    shp = jax.ShapeDtypeStruct((SW_HEADS, t, SW_HEAD_DIM), F32)
    return _call(
        body, name=name, grid=(nblk,),
        in_specs=[_blk(cur), _blk(cur), _blk(cur), _blk(nxt), _blk(cur), _blk(nxt), _blk(cur), _blk(nxt),
                  _blk(cur), _blk(nxt)],
        out_specs=[_blk(cur), _blk(cur)], out_shape=[shp, shp], semantics=("parallel",),
    )(k, v, q, q, do, do, lg, lg, dm, dm)


def sw_merge(o3, l3, *, tm=512):
    _, t, w = o3.shape

    def body(o_ref, l_ref, y_ref, lg_ref):
        l0, l1, l2 = l_ref[0], l_ref[1], l_ref[2]
        mx = jnp.maximum(jnp.maximum(l0, l1), l2)
        lg = mx + jnp.log(jnp.exp(l0 - mx) + jnp.exp(l1 - mx) + jnp.exp(l2 - mx))
        lg_ref[...] = lg
        y_ref[...] = jnp.exp(l0 - lg) * o_ref[0] + jnp.exp(l1 - lg) * o_ref[1] + jnp.exp(l2 - lg) * o_ref[2]

    b3 = pl.BlockSpec((3, tm, w), lambda i: (0, i, 0))
    b1 = pl.BlockSpec((tm, w), lambda i: (i, 0))
    shp = jax.ShapeDtypeStruct((t, w), F32)
    return _call(body, name="sw_merge", grid=(t // tm,), in_specs=[b3, b3], out_specs=[b1, b1],
                 out_shape=[shp, shp], semantics=("parallel",))(o3, l3)


def sw_delta(dy, y, *, tm=512):
    t, w = y.shape

    def body(dy_ref, y_ref, o_ref):
        same, _ = _head_mats()
        o_ref[...] = _hdot(dy_ref[...] * y_ref[...], same)

    b1 = pl.BlockSpec((tm, w), lambda i: (i, 0))
    return _call(body, name="sw_delta", grid=(t // tm,), in_specs=[b1, b1], out_specs=b1,
                 out_shape=jax.ShapeDtypeStruct((t, w), F32), semantics=("parallel",))(dy, y)


def _to_sub(a, dil):
    t = a.shape[0]
    return a.reshape(t // dil, dil, SW_HEADS, SW_HEAD_DIM).transpose(2, 1, 0, 3).reshape(SW_HEADS, t, SW_HEAD_DIM)


def _from_sub(a, dil):
    t = a.shape[1]
    return a.reshape(SW_HEADS, dil, t // dil, SW_HEAD_DIM).transpose(2, 1, 0, 3).reshape(t, SW_WIDTH)


def _rope_tables(t):
    inv = ROPE_THETA ** (-jnp.arange(0, ROPE_DIM, 2, dtype=F32) / ROPE_DIM)
    ang = jnp.arange(t, dtype=F32)[:, None] * inv[None, :]
    pad1 = jnp.ones((t, SW_HEAD_DIM - ROPE_DIM), F32)
    pad0 = jnp.zeros((t, SW_HEAD_DIM - ROPE_DIM), F32)
    cos_h = jnp.concatenate([jnp.cos(ang), jnp.cos(ang), pad1], axis=1)
    sin_h = jnp.concatenate([jnp.sin(ang), jnp.sin(ang), pad0], axis=1)
    return jnp.tile(cos_h, (1, SW_HEADS)), jnp.tile(sin_h, (1, SW_HEADS))


def sw_forward(proj, nw2, cos_t, sin_t):
    t = proj.shape[0]
    qk = sw_prep(proj, nw2, cos_t, sin_t)
    subs, outs, lses = [], [], []
    for p, dil in enumerate(SW_DILATIONS):
        v0 = C_SW + 3 * SW_WIDTH * p + 2 * SW_WIDTH
        q = _to_sub(qk[2 * p], dil)
        k = _to_sub(qk[2 * p + 1], dil)
        v = _to_sub(proj[:, v0:v0 + SW_WIDTH], dil)
        o, lse = sw_attn(q, k, v, (t // dil) // SW_BLOCK, name=f"sw_attn{p}")
        subs.append((q, k, v))
        outs.append(_from_sub(o, dil))
        lses.append(_from_sub(lse, dil))
    y, lg = sw_merge(jnp.stack(outs), jnp.stack(lses))
    return y, (subs, y, lg)


def sw_backward(proj, nw2, cos_t, sin_t, res, dy):
    subs, y, lg = res
    t = proj.shape[0]
    dm = sw_delta(dy, y)
    dqk, dvs = [], []
    for p, dil in enumerate(SW_DILATIONS):
        q, k, v = subs[p]
        nbs = (t // dil) // SW_BLOCK
        do_s, lg_s, dm_s = _to_sub(dy, dil), _to_sub(lg, dil), _to_sub(dm, dil)
        dq = sw_attn_dq(q, k, v, do_s, lg_s, dm_s, nbs, name=f"sw_dq{p}")
        dk, dv = sw_attn_dkv(q, k, v, do_s, lg_s, dm_s, nbs, name=f"sw_dkv{p}")
        dqk += [_from_sub(dq, dil), _from_sub(dk, dil)]
        dvs.append(_from_sub(dv, dil))
    dx, dnw = sw_prep_bwd(proj, nw2, cos_t, sin_t, jnp.stack(dqk))
    cols = []
    for p in range(3):
        cols += [dx[2 * p], dx[2 * p + 1], dvs[p]]
    return cols, dnw


def _pad_rows(a, rows):
    return jnp.zeros((rows,) + a.shape[1:], a.dtype).at[:a.shape[0]].set(a)


def _layer_consts(sp, layer):
    d = {}
    d["mix_nw"] = sp["mix_norm_w"][layer][None, :]
    d["ffn_nw"] = sp["ffn_norm_w"][layer][None, :]
    d["cw8"] = _pad_rows(sp["dn_conv_w"][layer], 8)
    scal = jnp.zeros((8, LANE), F32)
    d["scal"] = scal.at[0, :DN_HEADS].set(sp["dn_a_log"][layer]).at[1, :DN_HEADS].set(sp["dn_dt_bias"][layer])
    d["wn"] = sp["dn_out_norm_w"][layer][None, :]
    d["lng"] = sp["gm_ln_g"][layer][None, :]
    d["lnb"] = sp["gm_ln_b"][layer][None, :]
    d["w_s"] = sp["gm_w_s"][layer]
    d["bst"] = jnp.zeros((GM_CHUNK, LANE), F32).at[:, :GM_GROUPS].set(sp["gm_b_s"][layer].T)
    d["nw2"] = jnp.stack([jnp.tile(sp["sw_q_norm_w"][layer], SW_HEADS),
                          jnp.tile(sp["sw_k_norm_w"][layer], SW_HEADS)])[:, None, :]
    return d


def _layer_fwd(x, mod, wb, cs, tabs):
    h1, proj = norm_mm(x, cs["mix_nw"], mod[1], mod[0], wb["w_in"], swiglu=False, name="in_proj")
    act = dn_conv(proj, cs["cw8"])
    ya, states = dn_fwd(act, proj, cs["scal"], cs["wn"])
    yb = gm_fwd(proj, cs["lng"], cs["lnb"], cs["w_s"], cs["bst"])
    yc, swres = sw_forward(proj, cs["nw2"], *tabs)
    y = jnp.concatenate([ya, yb, yc], axis=1)
    x1, o1 = resid_mm(y, wb["w_out"], x, mod[2], name="out_proj")
    h2, gu, actf = norm_mm(x1, cs["ffn_nw"], mod[4], mod[3], wb["w_ffn_in"], swiglu=True, name="ffn_in")
    x2, o2 = resid_mm(actf, wb["w_ffn_out"], x1, mod[5], name="ffn_out")
    res = dict(x=x, h1=h1, proj=proj, act=act, states=states, swres=swres, y=y, x1=x1, o1=o1, h2=h2, gu=gu,
               actf=actf, o2=o2)
    return x2, res


def _layer_bwd(dx2, res, mod, wb, cs, tabs):
    dgu, gx2, dgate2 = resid_mm_bwd(dx2, mod[5], res["o2"], wb["w_ffn_out"], res["gu"], name="ffn_out_bwd")
    g_wfo = mm_tn(res["actf"], gx2, name="wg_ffn_out")
    g_wfi = mm_tn(res["h2"], dgu, name="wg_ffn_in")
    dx1, d_ffn_nw, dscale2, dshift2 = norm_mm_bwd(dgu, wb["w_ffn_in"], res["x1"], cs["ffn_nw"], mod[4], dx2,
                                                  name="ffn_in_bwd")
    dy, gx1, dgate1 = resid_mm_bwd(dx1, mod[2], res["o1"], wb["w_out"], None, name="out_proj_bwd")
    g_wout = mm_tn(res["y"], gx1, name="wg_out")
    proj = res["proj"]
    dact, dz, dab, dpar = dn_bwd(res["act"], proj, cs["scal"], cs["wn"], res["states"], dy)
    dqkv, dcw = dn_conv_bwd(proj, cs["cw8"], dact)
    duv, dws, dbst, dln = gm_bwd(proj, cs["lng"], cs["lnb"], cs["w_s"], cs["bst"], dy,
                                 dy_col=DN_WIDTH // GM_WIDTH)
    cols, dnw = sw_backward(proj, cs["nw2"], *tabs, res["swres"], dy[:, DN_WIDTH + GM_WIDTH:])
    dproj = jnp.concatenate([dqkv, dz, duv] + cols + [dab], axis=1)
    g_win = mm_tn(res["h1"], dproj, name="wg_in")
    dx, d_mix_nw, dscale1, dshift1 = norm_mm_bwd(dproj, wb["w_in"], res["x"], cs["mix_nw"], mod[1], dx1,
                                                 name="in_proj_bwd")
    dmod = jnp.concatenate([dshift1, dscale1, dgate1, dshift2, dscale2, dgate2], axis=1)
    dnw = dnw.reshape(3, 2, SW_HEADS, SW_HEAD_DIM).sum((0, 2))
    small = dict(mix_norm_w=d_mix_nw[0], ffn_norm_w=d_ffn_nw[0], dn_conv_w=dcw[:DN_CONV],
                 dn_a_log=dpar[0, :DN_HEADS], dn_dt_bias=dpar[1, :DN_HEADS], dn_out_norm_w=dpar[2],
                 gm_ln_g=dln[0], gm_ln_b=dln[1], gm_w_s=dws, gm_b_s=dbst[:, :GM_GROUPS].T,
                 sw_q_norm_w=dnw[0], sw_k_norm_w=dnw[1])
    big = dict(w_in=g_win, w_out=g_wout, w_ffn_in=g_wfi, w_ffn_out=g_wfo)
    return dx, big, small, dmod


def _permute_w_in(w):
    pad = jnp.zeros(w.shape[:-1] + (IN_PAD - IN_WIDTH,), w.dtype)
    return jnp.concatenate([w[..., 0:2048], w[..., 2056:2568], w[..., 2568:IN_WIDTH], w[..., 2048:2056], pad], axis=-1)


def _unpermute_w_in(g):
    return jnp.concatenate([g[..., 0:2048], g[..., C_AB:C_AB + 8], g[..., 2048:2560], g[..., 2560:C_AB]], axis=-1)


def _local_step(x, target, mods, wfull, sp):
    layers = mods.shape[0]
    t, d = x.shape
    tabs = _rope_tables(t)
    saved = []
    for layer in range(layers):
        mod = mods[layer].reshape(6, 1, d)
        wb = {k: v[layer] for k, v in wfull.items()}
        cs = _layer_consts(sp, layer)
        x, res = _layer_fwd(x, mod, wb, cs, tabs)
        saved.append((res, mod, wb, cs))
    dx, loss = loss_head(x, target)
    bigs, smalls, dmods = [], [], []
    for layer in reversed(range(layers)):
        res, mod, wb, cs = saved[layer]
        dx, big, small, dmod = _layer_bwd(dx, res, mod, wb, cs, tabs)
        bigs.append(big)
        smalls.append(small)
        dmods.append(dmod[0])
    bigs, smalls, dmods = bigs[::-1], smalls[::-1], dmods[::-1]
    big = {k: [b[k] for b in bigs] for k in bigs[0]}
    small = {k: jnp.stack([s[k] for s in smalls]) for k in smalls[0]}
    return loss, dx, big, small, jnp.stack(dmods)


def mod_fwd(c_all, w_mod, b_shard):
    layers, d, n = w_mod.shape

    def body(c_ref, w_ref, b_ref, o_ref):
        ca = _silu(c_ref[...]).astype(BF16)
        o_ref[0] = _dot(ca, w_ref[0].astype(BF16), 1, 0) + b_ref[0]

    return _call(
        body, name="mod_fwd", grid=(layers,),
        in_specs=[_full((8, d)), pl.BlockSpec((1, d, n), lambda i: (i, 0, 0)),
                  pl.BlockSpec((1, 1, n), lambda i: (i, 0, 0))],
        out_specs=pl.BlockSpec((1, 8, n), lambda i: (i, 0, 0)),
        out_shape=jax.ShapeDtypeStruct((layers, 8, n), F32), semantics=("parallel",),
    )(c_all, w_mod, b_shard)


def mod_bwd(c_all, dmod):
    layers, _, n = dmod.shape
    d = c_all.shape[1]

    def body(c_ref, g_ref, o_ref):
        ca = _silu(c_ref[...]).astype(BF16)
        o_ref[0] = _dot(ca, g_ref[0].astype(BF16), 0, 0)

    return _call(
        body, name="mod_bwd", grid=(layers,),
        in_specs=[_full((8, d)), pl.BlockSpec((1, 8, n), lambda i: (i, 0, 0))],
        out_specs=pl.BlockSpec((1, d, n), lambda i: (i, 0, 0)),
        out_shape=jax.ShapeDtypeStruct((layers, d, n), F32), semantics=("parallel",),
    )(c_all, dmod)


N_DEV = 8
ANY = pl.BlockSpec(memory_space=pl.ANY)


def _place():
    return lax.axis_index("x"), lax.axis_index("y"), lax.axis_index("c")


def _other_chips(x, y):
    return [(1 - x, y), (x, 1 - y), (1 - x, 1 - y)]


def allgather8(x_shard, *, name):
    m_per, n = x_shard.shape

    def body(x_ref, out_ref, send_sems, recv_sems, local_sem):
        x, y, c = _place()
        me, sibling = (x, y, c), (x, y, 1 - c)
        chips = _other_chips(x, y)

        def rows(px, py, pc):
            return out_ref.at[pl.ds((4 * px + 2 * py + pc) * m_per, m_per), :]

        def copy(k, block, to, src=None):
            return pltpu.make_async_remote_copy(
                src_ref=rows(*block) if src is None else src, dst_ref=rows(*block),
                send_sem=send_sems.at[k], recv_sem=recv_sems.at[k], device_id=to, device_id_type=MESH)

        mine = pltpu.make_async_copy(x_ref, rows(*me), local_sem)
        mine.start()
        first = [copy(0, me, sibling, src=x_ref)]
        first += [copy(1 + j, me, (*chip, c), src=x_ref) for j, chip in enumerate(chips)]
        for cp in first:
            cp.start()
        passed = [copy(4 + j, (*chip, c), sibling) for j, chip in enumerate(chips)]
        for j, chip in enumerate(chips):
            copy(1 + j, (*chip, c), me).wait_recv()
            passed[j].start()
        copy(0, sibling, me).wait_recv()
        for j, chip in enumerate(chips):
            copy(4 + j, (*chip, 1 - c), me).wait_recv()
        for cp in first + passed:
            cp.wait_send()
        mine.wait()

    return pl.pallas_call(
        body, name=name, out_shape=jax.ShapeDtypeStruct((N_DEV * m_per, n), x_shard.dtype),
        in_specs=[pl.BlockSpec(memory_space=pltpu.VMEM)], out_specs=pl.BlockSpec(memory_space=pltpu.VMEM),
        scratch_shapes=[pltpu.SemaphoreType.DMA((7,)), pltpu.SemaphoreType.DMA((7,)), pltpu.SemaphoreType.DMA],
    )(x_shard)


def gather_shards(shards):
    n = len(shards)

    def body(*refs):
        ins, outs = refs[:n], refs[n:2 * n]
        send_sems, recv_sems, local_sems = refs[2 * n:]
        x, y, c = _place()
        chips = _other_chips(x, y)
        me_s = 2 * x + y
        local = []
        for a in range(n):
            cp = pltpu.make_async_copy(ins[a], outs[a].at[me_s], local_sems.at[a])
            cp.start()
            local.append(cp)
        sends = []
        for a in range(n):
            for j, chip in enumerate(chips):
                k = 3 * a + j
                cp = pltpu.make_async_remote_copy(
                    src_ref=ins[a], dst_ref=outs[a].at[me_s], send_sem=send_sems.at[k], recv_sem=recv_sems.at[k],
                    device_id=(*chip, c), device_id_type=MESH)
                cp.start()
                sends.append(cp)
        for a in range(n):
            for j, chip in enumerate(chips):
                k = 3 * a + j
                pltpu.make_async_remote_copy(
                    src_ref=ins[a], dst_ref=outs[a].at[2 * chip[0] + chip[1]], send_sem=send_sems.at[k],
                    recv_sem=recv_sems.at[k], device_id=(*chip, c), device_id_type=MESH).wait_recv()
        for cp in sends:
            cp.wait_send()
        for cp in local:
            cp.wait()

    return pl.pallas_call(
        body, name="gather_shards",
        out_shape=[jax.ShapeDtypeStruct((4,) + s.shape, s.dtype) for s in shards],
        in_specs=[ANY] * n, out_specs=[ANY] * n,
        scratch_shapes=[pltpu.SemaphoreType.DMA((3 * n,)), pltpu.SemaphoreType.DMA((3 * n,)),
                        pltpu.SemaphoreType.DMA((n,))],
    )(*shards)


def scatter_slices(grads):
    n = len(grads)

    def body(*refs):
        ins, outs = refs[:n], refs[n:2 * n]
        send_sems, recv_sems, local_sems = refs[2 * n:]
        x, y, c = _place()
        chips = _other_chips(x, y)
        me_s = 2 * x + y
        local = []
        for a in range(n):
            cp = pltpu.make_async_copy(ins[a].at[me_s], outs[a].at[me_s], local_sems.at[a])
            cp.start()
            local.append(cp)
        sends = []
        for a in range(n):
            for j, chip in enumerate(chips):
                k = 3 * a + j
                cp = pltpu.make_async_remote_copy(
                    src_ref=ins[a].at[2 * chip[0] + chip[1]], dst_ref=outs[a].at[me_s], send_sem=send_sems.at[k],
                    recv_sem=recv_sems.at[k], device_id=(*chip, c), device_id_type=MESH)
                cp.start()
                sends.append(cp)
        for a in range(n):
            for j, chip in enumerate(chips):
                k = 3 * a + j
                pltpu.make_async_remote_copy(
                    src_ref=ins[a].at[me_s], dst_ref=outs[a].at[2 * chip[0] + chip[1]], send_sem=send_sems.at[k],
                    recv_sem=recv_sems.at[k], device_id=(*chip, c), device_id_type=MESH).wait_recv()
        for cp in sends:
            cp.wait_send()
        for cp in local:
            cp.wait()

    return pl.pallas_call(
        body, name="scatter_slices",
        out_shape=[jax.ShapeDtypeStruct(g.shape, g.dtype) for g in grads],
        in_specs=[ANY] * n, out_specs=[ANY] * n,
        scratch_shapes=[pltpu.SemaphoreType.DMA((3 * n,)), pltpu.SemaphoreType.DMA((3 * n,)),
                        pltpu.SemaphoreType.DMA((n,))],
    )(*grads)


def sibling_swap(parts):
    n = len(parts)

    def body(*refs):
        ins, outs = refs[:n], refs[n:2 * n]
        send_sems, recv_sems = refs[2 * n:]
        x, y, c = _place()
        cps = []
        for a in range(n):
            cp = pltpu.make_async_remote_copy(
                src_ref=ins[a], dst_ref=outs[a], send_sem=send_sems.at[a], recv_sem=recv_sems.at[a],
                device_id=(x, y, 1 - c), device_id_type=MESH)
            cp.start()
            cps.append(cp)
        for cp in cps:
            cp.wait()

    return pl.pallas_call(
        body, name="sibling_swap", out_shape=[jax.ShapeDtypeStruct(p.shape, p.dtype) for p in parts],
        in_specs=[ANY] * n, out_specs=[ANY] * n,
        scratch_shapes=[pltpu.SemaphoreType.DMA((n,)), pltpu.SemaphoreType.DMA((n,))],
    )(*parts)


def _row_block(rows, cols, budget=1 << 20):
    best = rows if rows % 8 else 8
    for tr in range(8, rows + 1, 8):
        if rows % tr == 0 and tr * cols * 4 <= budget:
            best = tr
    return best


def chip_sum(own, recv, me_s, *, name):
    r, n = own.shape
    tr = _row_block(r, n)

    def body(me_ref, own_ref, recv_ref, o_ref):
        me = me_ref[0]
        acc = jnp.zeros((tr, n), F32)
        for s in range(4):
            acc = acc + jnp.where(me == s, own_ref[...], recv_ref[s].astype(F32))
        o_ref[...] = acc

    return pl.pallas_call(
        body, name=name, out_shape=jax.ShapeDtypeStruct((r, n), F32),
        grid_spec=pltpu.PrefetchScalarGridSpec(
            num_scalar_prefetch=1, grid=(r // tr,),
            in_specs=[pl.BlockSpec((tr, n), lambda i, me: (i, 0)), pl.BlockSpec((4, tr, n), lambda i, me: (0, i, 0))],
            out_specs=pl.BlockSpec((tr, n), lambda i, me: (i, 0))),
        compiler_params=pltpu.CompilerParams(dimension_semantics=("parallel",)),
    )(me_s, own, recv)


def _adam_update(w, g, m, v):
    m2 = ADAM_B1 * m + (1.0 - ADAM_B1) * g
    v2 = ADAM_B2 * v + (1.0 - ADAM_B2) * (g * g)
    m_hat = m2 / (1.0 - ADAM_B1 ** ADAM_STEP)
    v_hat = v2 / (1.0 - ADAM_B2 ** ADAM_STEP)
    delta = -ADAM_LR * (m_hat / (jnp.sqrt(v_hat) + ADAM_EPS) + ADAM_WD * w)
    return delta, m2, v2


def adamw(w, g_parts, m, v, *, name):
    r, n = w.shape
    tr = _row_block(r, n)
    k = len(g_parts)

    def body(*refs):
        w_ref, m_ref, v_ref = refs[k], refs[k + 1], refs[k + 2]
        g_ref, d_ref, m2_ref, v2_ref = refs[k + 3:]
        g = refs[0][...]
        for p in refs[1:k]:
            g = g + p[...]
        g_ref[...] = g
        d_ref[...], m2_ref[...], v2_ref[...] = _adam_update(w_ref[...], g, m_ref[...], v_ref[...])

    blk = pl.BlockSpec((tr, n), lambda i: (i, 0))
    shp = jax.ShapeDtypeStruct((r, n), F32)
    return _call(body, name=name, grid=(r // tr,), in_specs=[blk] * (k + 3), out_specs=[blk] * 4,
                 out_shape=[shp] * 4, semantics=("parallel",))(*g_parts, w, m, v)


def adamw_gathered(g_all, w, m, v, *, name):
    _, r, n = g_all.shape
    tr = _row_block(r, n * 4)

    def body(ga_ref, w_ref, m_ref, v_ref, g_ref, d_ref, m2_ref, v2_ref):
        g = ga_ref[0]
        for dev in range(1, N_DEV):
            g = g + ga_ref[dev]
        g_ref[...] = g
        d_ref[...], m2_ref[...], v2_ref[...] = _adam_update(w_ref[...], g, m_ref[...], v_ref[...])

    blk = pl.BlockSpec((tr, n), lambda i: (i, 0))
    shp = jax.ShapeDtypeStruct((r, n), F32)
    return _call(body, name=name, grid=(r // tr,),
                 in_specs=[pl.BlockSpec((N_DEV, tr, n), lambda i: (0, i, 0)), blk, blk, blk], out_specs=[blk] * 4,
                 out_shape=[shp] * 4, semantics=("parallel",))(g_all, w, m, v)


BIG = ("w_in", "w_out", "w_ffn_in", "w_ffn_out")
SMALL = ("b_mod", "mix_norm_w", "ffn_norm_w", "dn_conv_w", "dn_a_log", "dn_dt_bias", "dn_out_norm_w", "gm_ln_g",
         "gm_ln_b", "gm_w_s", "gm_b_s", "sw_q_norm_w", "sw_k_norm_w")
WEIGHTS = ("w_mod", "b_mod", "mix_norm_w", "ffn_norm_w", "w_in", "w_out", "dn_conv_w", "dn_a_log", "dn_dt_bias",
           "dn_out_norm_w", "gm_ln_g", "gm_ln_b", "gm_w_s", "gm_b_s", "sw_q_norm_w", "sw_k_norm_w", "w_ffn_in",
           "w_ffn_out")
PACK_ROWS = 8


def _pack(arrs):
    out = []
    for a in arrs:
        flat = a.reshape(-1).astype(F32)
        rows = -(-flat.shape[0] // (LANE * PACK_ROWS)) * PACK_ROWS
        out.append(jnp.zeros((rows * LANE,), F32).at[:flat.shape[0]].set(flat).reshape(rows, LANE))
    return jnp.concatenate(out, axis=0)


def _unpack(packed, shapes):
    out, r0 = [], 0
    for shp in shapes:
        size = math.prod(shp)
        rows = -(-size // (LANE * PACK_ROWS)) * PACK_ROWS
        out.append(packed[r0:r0 + rows].reshape(-1)[:size].reshape(shp))
        r0 += rows
    return out


def _shard_major(per_layer, axis, unpermute=False):
    g = jnp.stack(per_layer)
    if unpermute:
        g = _unpermute_w_in(g)
    return jnp.stack(jnp.split(g, 4, axis=axis + 1))


def kernel(x, c, w_mod, b_mod, mix_norm_w, ffn_norm_w, w_in, w_out, dn_conv_w, dn_a_log, dn_dt_bias, dn_out_norm_w, gm_ln_g, gm_ln_b, gm_w_s, gm_b_s, sw_q_norm_w, sw_k_norm_w, w_ffn_in, w_ffn_out, loss_target, m_w_mod, m_b_mod, m_mix_norm_w, m_ffn_norm_w, m_w_in, m_w_out, m_dn_conv_w, m_dn_a_log, m_dn_dt_bias, m_dn_out_norm_w, m_gm_ln_g, m_gm_ln_b, m_gm_w_s, m_gm_b_s, m_sw_q_norm_w, m_sw_k_norm_w, m_w_ffn_in, m_w_ffn_out, v_w_mod, v_b_mod, v_mix_norm_w, v_ffn_norm_w, v_w_in, v_w_out, v_dn_conv_w, v_dn_a_log, v_dn_dt_bias, v_dn_out_norm_w, v_gm_ln_g, v_gm_ln_b, v_gm_w_s, v_gm_b_s, v_sw_q_norm_w, v_sw_k_norm_w, v_w_ffn_in, v_w_ffn_out):
    w = dict(w_mod=w_mod, b_mod=b_mod, mix_norm_w=mix_norm_w, ffn_norm_w=ffn_norm_w, w_in=w_in, w_out=w_out,
             dn_conv_w=dn_conv_w, dn_a_log=dn_a_log, dn_dt_bias=dn_dt_bias, dn_out_norm_w=dn_out_norm_w,
             gm_ln_g=gm_ln_g, gm_ln_b=gm_ln_b, gm_w_s=gm_w_s, gm_b_s=gm_b_s, sw_q_norm_w=sw_q_norm_w,
             sw_k_norm_w=sw_k_norm_w, w_ffn_in=w_ffn_in, w_ffn_out=w_ffn_out)
    m = dict(w_mod=m_w_mod, b_mod=m_b_mod, mix_norm_w=m_mix_norm_w, ffn_norm_w=m_ffn_norm_w, w_in=m_w_in,
             w_out=m_w_out, dn_conv_w=m_dn_conv_w, dn_a_log=m_dn_a_log, dn_dt_bias=m_dn_dt_bias,
             dn_out_norm_w=m_dn_out_norm_w, gm_ln_g=m_gm_ln_g, gm_ln_b=m_gm_ln_b, gm_w_s=m_gm_w_s, gm_b_s=m_gm_b_s,
             sw_q_norm_w=m_sw_q_norm_w, sw_k_norm_w=m_sw_k_norm_w, w_ffn_in=m_w_ffn_in, w_ffn_out=m_w_ffn_out)
    v = dict(w_mod=v_w_mod, b_mod=v_b_mod, mix_norm_w=v_mix_norm_w, ffn_norm_w=v_ffn_norm_w, w_in=v_w_in,
             w_out=v_w_out, dn_conv_w=v_dn_conv_w, dn_a_log=v_dn_a_log, dn_dt_bias=v_dn_dt_bias,
             dn_out_norm_w=v_dn_out_norm_w, gm_ln_g=v_gm_ln_g, gm_ln_b=v_gm_ln_b, gm_w_s=v_gm_w_s, gm_b_s=v_gm_b_s,
             sw_q_norm_w=v_sw_q_norm_w, sw_k_norm_w=v_sw_k_norm_w, w_ffn_in=v_w_ffn_in, w_ffn_out=v_w_ffn_out)
    layers, d, mod_n = w_mod.shape
    mx, my, mc = _place()
    me_s = 2 * mx + my
    me_dev = 4 * mx + 2 * my + mc

    c_all = allgather8(_pad_rows(c, 8), name="gather_c").reshape(N_DEV, 8, d)[:, 0]
    b_shard = lax.dynamic_slice_in_dim(b_mod, me_s * mod_n, mod_n, axis=1)[:, None, :]
    mod_part = mod_fwd(c_all, w_mod, b_shard)
    mod_parts = allgather8(mod_part.reshape(layers * 8, mod_n), name="gather_mod")
    mod_parts = mod_parts.reshape(4, 2, layers, 8, mod_n)[:, 0]
    mod_all = mod_parts.transpose(1, 2, 0, 3).reshape(layers, 8, 4 * mod_n)
    mods = lax.dynamic_index_in_dim(mod_all, me_dev, axis=1, keepdims=False)

    gathered = gather_shards([w[k].astype(BF16) for k in BIG])
    g_in, g_out, g_fi, g_fo = gathered
    wfull = dict(
        w_in=_permute_w_in(jnp.concatenate([g_in[s] for s in range(4)], axis=-1)),
        w_out=jnp.concatenate([g_out[s] for s in range(4)], axis=1),
        w_ffn_in=jnp.concatenate([g_fi[s] for s in range(4)], axis=-1),
        w_ffn_out=jnp.concatenate([g_fo[s] for s in range(4)], axis=1))

    cw = dn_conv_w.shape[-1]
    conv_rows = -(-layers * DN_CONV // 8) * 8
    conv_parts = allgather8(_pad_rows(dn_conv_w.reshape(layers * DN_CONV, cw), conv_rows), name="gather_conv")
    conv_parts = conv_parts.reshape(4, 2, conv_rows, cw)[:, 0, :layers * DN_CONV]
    conv_full = conv_parts.reshape(4, layers, DN_CONV, cw).transpose(1, 2, 0, 3).reshape(layers, DN_CONV, 4 * cw)

    sp = {k: w[k] for k in SMALL}
    sp["dn_conv_w"] = conv_full
    loss_blk, grad_x, big, small, dmods = _local_step(x[0], loss_target[0], mods, wfull, sp)
    loss = lax.psum(loss_blk[0, 0], ("x", "y", "c"))

    axes = dict(w_in=2, w_out=1, w_ffn_in=2, w_ffn_out=1)
    stacked = {k: _shard_major(big[k], axes[k] - 1, unpermute=(k == "w_in")) for k in BIG}
    recv = scatter_slices([stacked[k].astype(BF16) for k in BIG])
    me_arr = jnp.reshape(me_s, (1,)).astype(jnp.int32)
    partial = []
    for k, r in zip(BIG, recv):
        own = lax.dynamic_index_in_dim(stacked[k], me_s, axis=0, keepdims=False)
        n_last = own.shape[-1]
        partial.append(chip_sum(own.reshape(-1, n_last), r.reshape(4, -1, n_last), me_arr, name="chip_sum_" + k))
    theirs = sibling_swap(partial)
    outs = {}
    for k, mine, other in zip(BIG, partial, theirs):
        shp = w[k].shape
        flat = lambda a: a.reshape(-1, shp[-1])
        res = adamw(flat(w[k]), [mine, other], flat(m[k]), flat(v[k]), name="adamw_" + k)
        outs[k] = [a.reshape(shp) for a in res]

    small = dict(small, b_mod=dmods)
    packed = _pack([small[k] for k in SMALL])
    rows = packed.shape[0]
    g_all = allgather8(packed, name="gather_small").reshape(N_DEV, rows, LANE)
    conv_zero = jnp.zeros((layers, DN_CONV, 3 * DN_WIDTH), F32)
    pk = lambda src: _pack([conv_zero if k == "dn_conv_w" else src[k] for k in SMALL])
    res = adamw_gathered(g_all, pk(w), pk(m), pk(v), name="adamw_small")
    shapes = [small[k].shape for k in SMALL]
    un = [_unpack(a, shapes) for a in res]
    for i, k in enumerate(SMALL):
        outs[k] = [un[j][i] for j in range(4)]
    g_conv = lax.dynamic_slice_in_dim(outs["dn_conv_w"][0], me_s * cw, cw, axis=2)
    flat = lambda a: a.reshape(-1, cw)
    res = adamw(flat(dn_conv_w), [flat(g_conv)], flat(m["dn_conv_w"]), flat(v["dn_conv_w"]), name="adamw_conv")
    outs["dn_conv_w"] = [a.reshape(dn_conv_w.shape) for a in res]

    b_rows = layers * 6 * d // LANE
    dmod_all = g_all[:, :b_rows].reshape(N_DEV, layers, 6 * d).transpose(1, 0, 2)
    dmod_shard = lax.dynamic_slice_in_dim(dmod_all, me_s * mod_n, mod_n, axis=2)
    g_wmod = mod_bwd(c_all, dmod_shard)
    flat = lambda a: a.reshape(-1, mod_n)
    res = adamw(flat(w_mod), [flat(g_wmod)], flat(m_w_mod), flat(v_w_mod), name="adamw_w_mod")
    outs["w_mod"] = [a.reshape(w_mod.shape) for a in res]

    result = [loss, grad_x[None]]
    for j in range(4):
        result += [outs[k][j] for k in WEIGHTS]
    return tuple(result)
```

```python
import functools
import math

import jax
import jax.numpy as jnp
from jax import lax
from jax.experimental import pallas as pl
from jax.experimental.pallas import tpu as pltpu

F32 = jnp.float32
BF16 = jnp.bfloat16
HI = lax.Precision.HIGH

NORM_EPS = 1e-6
DN_HEADS = 4
DN_HEAD_DIM = 128
DN_WIDTH = 512
DN_CHUNK = 64
DN_CONV = 4
GM_WIDTH = 256
GM_GROUPS = 4
GM_GROUP_DIM = 64
GM_CHUNK = 128
SW_HEADS = 4
SW_HEAD_DIM = 64
SW_WIDTH = 256
SW_DILATIONS = (1, 4, 16)
SW_BLOCK = 128
ROPE_THETA = 500000.0
ROPE_DIM = 16
LANE = 128

C_QKV = 0
C_Z = 1536
C_UV = 2048
C_SW = 2560
C_AB = 4864
IN_WIDTH = 4872
IN_PAD = 4992

ADAM_LR = 0.001
ADAM_B1 = 0.9
ADAM_B2 = 0.999
ADAM_EPS = 1e-08
ADAM_WD = 0.01
ADAM_STEP = 10

MESH = pl.DeviceIdType.MESH


def _call(body, *, name, grid, in_specs, out_specs, out_shape, scratch_shapes=(), semantics=None):
    if semantics is None:
        semantics = ("arbitrary",) * len(grid)
    return pl.pallas_call(
        body, name=name, grid=grid, in_specs=in_specs, out_specs=out_specs, out_shape=out_shape,
        scratch_shapes=list(scratch_shapes),
        compiler_params=pltpu.CompilerParams(dimension_semantics=semantics),
    )


def _dot(a, b, ca, cb, prec=None):
    if a.ndim == 3:
        dims = (((ca + 1,), (cb + 1,)), ((0,), (0,)))
    else:
        dims = (((ca,), (cb,)), ((), ()))
    return lax.dot_general(a, b, dims, preferred_element_type=F32, precision=prec)


def _bdot(a, b, ca=1, cb=0):
    return _dot(a.astype(BF16), b.astype(BF16), ca, cb)


def _hdot(a, b, ca=1, cb=0):
    return _dot(a.astype(F32), b.astype(F32), ca, cb, HI)


def _sigmoid(x):
    return 1.0 / (1.0 + jnp.exp(-x))


def _silu(x):
    return x * _sigmoid(x)


def _dsilu(x):
    s = _sigmoid(x)
    return s * (1.0 + x * (1.0 - s))


def _softplus(x):
    return jnp.maximum(x, 0.0) + jnp.log(1.0 + jnp.exp(-jnp.abs(x)))


def _iota2(shape, dim):
    return lax.broadcasted_iota(jnp.int32, shape, dim)


def _rowsum(x):
    return jnp.sum(x, axis=-1, keepdims=True)


def _colsum(x):
    return jnp.sum(x, axis=-2, keepdims=True)


def _full(shape):
    return pl.BlockSpec(shape, lambda *_: (0,) * len(shape))


def _norm_mod(x, nw, scale, shift):
    r = lax.rsqrt(jnp.mean(x * x, axis=-1, keepdims=True) + NORM_EPS)
    xn = x * r
    return xn, r, (xn * nw) * (1.0 + scale) + shift


def norm_mm(x, nw, scale, shift, w, *, swiglu, name, tm=256):
    t, d = x.shape
    n = w.shape[1]
    half = n // 2

    def body(x_ref, nw_ref, sc_ref, sh_ref, w_ref, h_ref, y_ref, *act_ref):
        _, _, h = _norm_mod(x_ref[...], nw_ref[...], sc_ref[...], sh_ref[...])
        hb = h.astype(BF16)
        h_ref[...] = hb
        y = _dot(hb, w_ref[...], 1, 0)
        y_ref[...] = y
        if swiglu:
            act_ref[0][...] = (_silu(y[:, :half]) * y[:, half:]).astype(BF16)

    row = lambda i: (i, 0)
    out_shape = [jax.ShapeDtypeStruct((t, d), BF16), jax.ShapeDtypeStruct((t, n), F32)]
    out_specs = [pl.BlockSpec((tm, d), row), pl.BlockSpec((tm, n), row)]
    if swiglu:
        out_shape.append(jax.ShapeDtypeStruct((t, half), BF16))
        out_specs.append(pl.BlockSpec((tm, half), row))
    return _call(
        body, name=name, grid=(t // tm,),
        in_specs=[pl.BlockSpec((tm, d), row), _full((1, d)), _full((1, d)), _full((1, d)), _full((d, n))],
        out_specs=out_specs, out_shape=out_shape, semantics=("parallel",),
    )(x, nw, scale, shift, w)


def resid_mm(y, w, x, gate, *, name, tm=256):
    t, k = y.shape
    d = w.shape[1]

    def body(y_ref, w_ref, x_ref, g_ref, xo_ref, o_ref):
        o = _dot(y_ref[...].astype(BF16), w_ref[...], 1, 0)
        o_ref[...] = o
        xo_ref[...] = x_ref[...] + g_ref[...] * o

    row = lambda i: (i, 0)
    return _call(
        body, name=name, grid=(t // tm,),
        in_specs=[pl.BlockSpec((tm, k), row), _full((k, d)), pl.BlockSpec((tm, d), row), _full((1, d))],
        out_specs=[pl.BlockSpec((tm, d), row), pl.BlockSpec((tm, d), row)],
        out_shape=[jax.ShapeDtypeStruct((t, d), F32), jax.ShapeDtypeStruct((t, d), F32)],
        semantics=("parallel",),
    )(y, w, x, gate)


def resid_mm_bwd(dx, gate, o, w, gu, *, name, tm=256):
    t, d = dx.shape
    k = w.shape[0]
    swiglu = gu is not None

    def body(dx_ref, g_ref, o_ref, w_ref, *rest):
        if swiglu:
            gu_ref, dy_ref, gx_ref, dg_ref = rest
        else:
            dy_ref, gx_ref, dg_ref = rest
        i = pl.program_id(0)
        dxv = dx_ref[...]
        gx = (dxv * g_ref[...]).astype(BF16)
        gx_ref[...] = gx
        part = _colsum(dxv * o_ref[...])

        @pl.when(i == 0)
        def _():
            dg_ref[...] = jnp.zeros_like(dg_ref)

        dg_ref[...] += part
        da = _dot(gx, w_ref[...], 1, 1)
        if swiglu:
            g = gu_ref[:, :k]
            u = gu_ref[:, k:]
            dy_ref[:, :k] = (da * u * _dsilu(g)).astype(BF16)
            dy_ref[:, k:] = (da * _silu(g)).astype(BF16)
        else:
            dy_ref[...] = da

    row = lambda i: (i, 0)
    in_specs = [pl.BlockSpec((tm, d), row), _full((1, d)), pl.BlockSpec((tm, d), row), _full((k, d))]
    args = [dx, gate, o, w]
    if swiglu:
        in_specs.append(pl.BlockSpec((tm, 2 * k), row))
        args.append(gu)
        dy_shape = jax.ShapeDtypeStruct((t, 2 * k), BF16)
        dy_spec = pl.BlockSpec((tm, 2 * k), row)
    else:
        dy_shape = jax.ShapeDtypeStruct((t, k), F32)
        dy_spec = pl.BlockSpec((tm, k), row)
    return _call(
        body, name=name, grid=(t // tm,), in_specs=in_specs,
        out_specs=[dy_spec, pl.BlockSpec((tm, d), row), _full((1, d))],
        out_shape=[dy_shape, jax.ShapeDtypeStruct((t, d), BF16), jax.ShapeDtypeStruct((1, d), F32)],
    )(*args)


def norm_mm_bwd(dy, w, x, nw, scale, dres, *, name, tm=256):
    t, n = dy.shape
    d = x.shape[1]
    steps = t // tm

    def body(dy_ref, w_ref, x_ref, nw_ref, sc_ref, dres_ref, dx_ref, dnw_ref, dsc_ref, dsh_ref):
        i = pl.program_id(0)
        dh = _dot(dy_ref[...].astype(BF16), w_ref[...], 1, 1)
        x = x_ref[...]
        r = lax.rsqrt(jnp.mean(x * x, axis=-1, keepdims=True) + NORM_EPS)
        xn = x * r
        a = nw_ref[...] * (1.0 + sc_ref[...])

        @pl.when(i == 0)
        def _():
            dnw_ref[...] = jnp.zeros_like(dnw_ref)
            dsh_ref[...] = jnp.zeros_like(dsh_ref)

        dnw_ref[...] += _colsum(dh * xn)
        dsh_ref[...] += _colsum(dh)
        dxn = dh * a
        dx_ref[...] = r * (dxn - xn * jnp.mean(dxn * xn, axis=-1, keepdims=True)) + dres_ref[...]

        @pl.when(i == steps - 1)
        def _():
            da = dnw_ref[...]
            dsc_ref[...] = da * nw_ref[...]
            dnw_ref[...] = da * (1.0 + sc_ref[...])

    row = lambda i: (i, 0)
    vec = jax.ShapeDtypeStruct((1, d), F32)
    return _call(
        body, name=name, grid=(steps,),
        in_specs=[pl.BlockSpec((tm, n), row), _full((d, n)), pl.BlockSpec((tm, d), row), _full((1, d)),
                  _full((1, d)), pl.BlockSpec((tm, d), row)],
        out_specs=[pl.BlockSpec((tm, d), row), _full((1, d)), _full((1, d)), _full((1, d))],
        out_shape=[jax.ShapeDtypeStruct((t, d), F32), vec, vec, vec],
    )(dy, w, x, nw, scale, dres)


def _pick_tn(n, k, budget=6 << 20):
    best = LANE
    for m in range(1, n // LANE + 1):
        tn = m * LANE
        if n % tn == 0 and k * tn * 4 <= budget:
            best = tn
    return best


def mm_tn(a, g, *, name, tt=512):
    t, k = a.shape
    n = g.shape[1]
    tn = _pick_tn(n, k)

    def body(a_ref, g_ref, o_ref):
        @pl.when(pl.program_id(1) == 0)
        def _():
            o_ref[...] = jnp.zeros_like(o_ref)

        o_ref[...] += _dot(a_ref[...].astype(BF16), g_ref[...].astype(BF16), 0, 0)

    return _call(
        body, name=name, grid=(n // tn, t // tt),
        in_specs=[pl.BlockSpec((tt, k), lambda j, i: (i, 0)), pl.BlockSpec((tt, tn), lambda j, i: (i, j))],
        out_specs=pl.BlockSpec((k, tn), lambda j, i: (0, j)),
        out_shape=jax.ShapeDtypeStruct((k, n), F32), semantics=("parallel", "arbitrary"),
    )(a, g)


def loss_head(y, target, *, tm=512):
    t, d = y.shape
    steps = t // tm

    def body(y_ref, t_ref, dy_ref, l_ref, acc_ref):
        i = pl.program_id(0)

        @pl.when(i == 0)
        def _():
            acc_ref[...] = jnp.zeros_like(acc_ref)

        e = y_ref[...] - t_ref[...]
        dy_ref[...] = e * (1.0 / d)
        acc_ref[...] += _colsum(e * e)

        @pl.when(i == steps - 1)
        def _():
            tot = jnp.sum(acc_ref[...], axis=-1, keepdims=True) * (0.5 / d)
            l_ref[...] = jnp.broadcast_to(tot, l_ref.shape)

    row = lambda i: (i, 0)
    return _call(
        body, name="loss_head", grid=(steps,),
        in_specs=[pl.BlockSpec((tm, d), row), pl.BlockSpec((tm, d), row)],
        out_specs=[pl.BlockSpec((tm, d), row), _full((8, LANE))],
        out_shape=[jax.ShapeDtypeStruct((t, d), F32), jax.ShapeDtypeStruct((8, LANE), F32)],
        scratch_shapes=[pltpu.VMEM((1, d), F32)],
    )(y, target)


def _shift_rows(x, s):
    if s == 0:
        return x
    t = x.shape[0]
    ri = _iota2(x.shape, 0)
    rolled = pltpu.roll(x, s % t, axis=0)
    if s > 0:
        return jnp.where(ri >= s, rolled, 0.0)
    return jnp.where(ri < t + s, rolled, 0.0)


def _conv_pre(x, w):
    acc = x * w[DN_CONV - 1:DN_CONV, :]
    for j in range(DN_CONV - 1):
        acc = acc + _shift_rows(x, DN_CONV - 1 - j) * w[j:j + 1, :]
    return acc


def dn_conv(proj, conv_w):
    t = proj.shape[0]
    width = 3 * DN_WIDTH

    def body(x_ref, w_ref, o_ref):
        o_ref[...] = _silu(_conv_pre(x_ref[...], w_ref[...]))

    col = lambda j: (0, j)
    return _call(
        body, name="dn_conv", grid=(width // LANE,),
        in_specs=[pl.BlockSpec((t, LANE), col), pl.BlockSpec((8, LANE), col)],
        out_specs=pl.BlockSpec((t, LANE), col),
        out_shape=jax.ShapeDtypeStruct((t, width), F32), semantics=("parallel",),
    )(proj, conv_w)


def dn_conv_bwd(proj, conv_w, dact):
    t = proj.shape[0]
    width = 3 * DN_WIDTH

    def body(x_ref, w_ref, d_ref, dx_ref, dw_ref):
        x = x_ref[...]
        w = w_ref[...]
        dc = d_ref[...] * _dsilu(_conv_pre(x, w))
        dx = dc * w[DN_CONV - 1:DN_CONV, :]
        rows = []
        for j in range(DN_CONV - 1):
            s = DN_CONV - 1 - j
            dx = dx + _shift_rows(dc, -s) * w[j:j + 1, :]
            rows.append(_colsum(dc * _shift_rows(x, s)))
        rows.append(_colsum(dc * x))
        dx_ref[...] = dx
        ri = _iota2((8, LANE), 0)
        dw = jnp.zeros((8, LANE), F32)
        for j in range(DN_CONV):
            dw = dw + jnp.where(ri == j, rows[j], 0.0)
        dw_ref[...] = dw

    col = lambda j: (0, j)
    return _call(
        body, name="dn_conv_bwd", grid=(width // LANE,),
        in_specs=[pl.BlockSpec((t, LANE), col), pl.BlockSpec((8, LANE), col), pl.BlockSpec((t, LANE), col)],
        out_specs=[pl.BlockSpec((t, LANE), col), pl.BlockSpec((8, LANE), col)],
        out_shape=[jax.ShapeDtypeStruct((t, width), F32), jax.ShapeDtypeStruct((8, width), F32)],
        semantics=("parallel",),
    )(proj, conv_w, dact)


def _t(x):
    return jnp.swapaxes(x, -1, -2)


def _inv_unit_lower(a):
    c = a.shape[-1]
    eye = (_iota2((c, c), 0) == _iota2((c, c), 1)).astype(F32)
    x = eye - a
    p = _hdot(a, a)
    steps = int(math.log2(c)) - 1
    for i in range(steps):
        x = x + _hdot(x, p)
        if i < steps - 1:
            p = _hdot(p, p)
    return x


def _dn_chunk(q, k, v, a, b, alog, dtb, s_in):
    nh, c, d = q.shape
    rq = lax.rsqrt(_rowsum(q * q) + NORM_EPS)
    rk = lax.rsqrt(_rowsum(k * k) + NORM_EPS)
    qh = q * rq
    kn = k * rk
    qs = qh * (d ** -0.5)
    g = -jnp.exp(alog) * _softplus(a + dtb)
    beta = _sigmoid(b)
    ri = _iota2((c, c), 0)
    ci = _iota2((c, c), 1)
    causal = ri >= ci
    strict = ri > ci
    gb = jnp.broadcast_to(g, (nh, c, d))
    gcb = _hdot(jnp.broadcast_to(causal.astype(F32), (nh, c, c)), gb)
    gc = gcb[..., :1]
    gl = _colsum(gb)[..., :1]
    dec = jnp.exp(jnp.where(causal, gc - _t(gcb)[:, :c, :], -1e30))
    kb = kn * beta
    amat = jnp.where(strict, _bdot(kb, kn, 1, 1) * dec, 0.0)
    tinv = _inv_unit_lower(amat)
    e = jnp.exp(gc)
    f = jnp.exp(gl - gc)
    rw = kb * e
    sol = _hdot(tinv, jnp.concatenate([v * beta, rw], axis=-1))
    u = sol[..., :d]
    w = sol[..., d:]
    pmat = jnp.where(causal, _bdot(qs, kn, 1, 1) * dec, 0.0)
    qd = qs * e
    kd = kn * f
    vnew = u - _bdot(w, s_in)
    o = _bdot(qd, s_in) + _bdot(pmat, vnew)
    s_out = s_in * jnp.exp(gl) + _bdot(kd, vnew, 0, 0)
    return dict(rq=rq, rk=rk, qh=qh, kn=kn, qs=qs, g=g, beta=beta, causal=causal, strict=strict, gl=gl,
                dec=dec, kb=kb, amat=amat, tinv=tinv, e=e, f=f, rw=rw, u=u, w=w, pmat=pmat, qd=qd, kd=kd,
                vnew=vnew, o=o, s_out=s_out)


def _dn_chunk_bwd(m, q, v, a, alog, dtb, s_in, do, ds_out):
    nh, c, d = q.shape
    kn, qs, kb, u, w, e, f = m["kn"], m["qs"], m["kb"], m["u"], m["w"], m["e"], m["f"]
    beta, dec, tinv, vnew, kd, qd = m["beta"], m["dec"], m["tinv"], m["vnew"], m["kd"], m["qd"]
    el = jnp.exp(m["gl"])
    dvnew = _bdot(m["pmat"], do, 0, 0) + _bdot(kd, ds_out)
    dp = jnp.where(m["causal"], _bdot(do, vnew, 1, 1), 0.0)
    dqd = _bdot(do, s_in, 1, 1)
    dkd = _bdot(vnew, ds_out, 1, 1)
    ds_in = _bdot(qd, do, 0, 0) + el * ds_out - _bdot(w, dvnew, 0, 0)
    dgl = el * _colsum(_rowsum(s_in * ds_out))
    dw = -_bdot(dvnew, s_in, 1, 1)
    dsol = _hdot(tinv, jnp.concatenate([dvnew, dw], axis=-1), 0, 0)
    dru = dsol[..., :d]
    drw = dsol[..., d:]
    da_m = -jnp.where(m["strict"], _bdot(dsol, jnp.concatenate([u, w], axis=-1), 1, 1), 0.0)
    db_m = da_m * dec
    dq_m = dp * dec
    dkb = _bdot(db_m, kn)
    dkn = _bdot(db_m, kb, 0, 0) + _bdot(dq_m, qs, 0, 0)
    dqs = _bdot(dq_m, kn)
    gmat = da_m * m["amat"] + dp * m["pmat"]
    ones = jnp.ones((nh, c, d), F32)
    dgam = (_hdot(gmat, ones) - _hdot(gmat, ones, 0, 0))[..., :1]
    dqs = dqs + dqd * e
    dgam = dgam + _rowsum(dqd * qd)
    dkn = dkn + dkd * f
    tk = _rowsum(dkd * kd)
    dgam = dgam - tk
    dgl = dgl + _colsum(tk)
    dkb = dkb + drw * e
    dgam = dgam + _rowsum(drw * m["rw"])
    dv = dru * beta
    dbeta = _rowsum(dru * v) + _rowsum(dkb * kn)
    dkn = dkn + dkb * beta
    last = (_iota2((c, 1), 0) == c - 1).astype(F32)
    dgam = dgam + last * dgl
    upper = (_iota2((c, c), 0) <= _iota2((c, c), 1)).astype(F32)
    dg = _hdot(jnp.broadcast_to(upper, (nh, c, c)), jnp.broadcast_to(dgam, (nh, c, d)))[..., :1]
    dqh = dqs * (d ** -0.5)
    dq = m["rq"] * (dqh - m["qh"] * _rowsum(dqh * m["qh"]))
    dk = m["rk"] * (dkn - kn * _rowsum(dkn * kn))
    sg = _sigmoid(a + dtb)
    da = dg * (-jnp.exp(alog)) * sg
    dalog = _colsum(dg * m["g"])
    ddtb = _colsum(da)
    db = dbeta * beta * (1.0 - beta)
    return dq, dk, dv, da, db, dalog, ddtb, ds_in


def _dn_gate(o, z, wn):
    ro = lax.rsqrt(jnp.mean(o * o, axis=-1, keepdims=True) + NORM_EPS)
    n = o * ro
    return n, ro, n * wn * _silu(z)


def _heads(ref, col0):
    d = DN_HEAD_DIM
    return jnp.stack([ref[:, col0 + h * d:col0 + (h + 1) * d] for h in range(DN_HEADS)])


def _dn_inputs(act_ref, ab_ref, sc_ref):
    ab = ab_ref[...]
    sc = sc_ref[...]
    q = _heads(act_ref, 0)
    k = _heads(act_ref, DN_WIDTH)
    v = _heads(act_ref, 2 * DN_WIDTH)
    a = jnp.stack([ab[:, h:h + 1] for h in range(DN_HEADS)])
    b = jnp.stack([ab[:, DN_HEADS + h:DN_HEADS + h + 1] for h in range(DN_HEADS)])
    alog = jnp.stack([sc[0:1, h:h + 1] for h in range(DN_HEADS)])
    dtb = jnp.stack([sc[1:2, h:h + 1] for h in range(DN_HEADS)])
    return q, k, v, a, b, alog, dtb


def dn_fwd(act, proj, scal, wn):
    t = act.shape[0]
    n = t // DN_CHUNK
    d = DN_HEAD_DIM

    def body(act_ref, z_ref, ab_ref, sc_ref, wn_ref, y_ref, st_ref, s_ref):
        @pl.when(pl.program_id(0) == 0)
        def _():
            s_ref[...] = jnp.zeros_like(s_ref)

        q, k, v, a, b, alog, dtb = _dn_inputs(act_ref, ab_ref, sc_ref)
        s_in = s_ref[...]
        st_ref[0] = s_in
        m = _dn_chunk(q, k, v, a, b, alog, dtb, s_in)
        s_ref[...] = m["s_out"]
        y = _dn_gate(m["o"], _heads(z_ref, 0), wn_ref[...])[2]
        for h in range(DN_HEADS):
            y_ref[:, h * d:(h + 1) * d] = y[h]

    return _call(
        body, name="dn_fwd", grid=(n,),
        in_specs=[pl.BlockSpec((DN_CHUNK, 3 * DN_WIDTH), lambda i: (i, 0)),
                  pl.BlockSpec((DN_CHUNK, DN_WIDTH), lambda i: (i, C_Z // DN_WIDTH)),
                  pl.BlockSpec((DN_CHUNK, LANE), lambda i: (i, C_AB // LANE)),
                  _full((8, LANE)), _full((1, d))],
        out_specs=[pl.BlockSpec((DN_CHUNK, DN_WIDTH), lambda i: (i, 0)),
                   pl.BlockSpec((1, DN_HEADS, d, d), lambda i: (i, 0, 0, 0))],
        out_shape=[jax.ShapeDtypeStruct((t, DN_WIDTH), F32), jax.ShapeDtypeStruct((n, DN_HEADS, d, d), F32)],
        scratch_shapes=[pltpu.VMEM((DN_HEADS, d, d), F32)],
    )(act, proj, proj, scal, wn)


def dn_bwd(act, proj, scal, wn, states, dy):
    t = act.shape[0]
    n = t // DN_CHUNK
    d = DN_HEAD_DIM

    def body(act_ref, z_ref, ab_ref, sc_ref, wn_ref, st_ref, dy_ref, dact_ref, dz_ref, dab_ref, dpar_ref, ds_ref):
        @pl.when(pl.program_id(0) == 0)
        def _():
            ds_ref[...] = jnp.zeros_like(ds_ref)
            dpar_ref[...] = jnp.zeros_like(dpar_ref)

        wnv = wn_ref[...]
        q, k, v, a, b, alog, dtb = _dn_inputs(act_ref, ab_ref, sc_ref)
        s_in = st_ref[0]
        z = _heads(z_ref, 0)
        dyh = _heads(dy_ref, 0)
        m = _dn_chunk(q, k, v, a, b, alog, dtb, s_in)
        nrm, ro, _ = _dn_gate(m["o"], z, wnv)
        sz = _silu(z)
        dz = dyh * nrm * wnv * _dsilu(z)
        dn = dyh * wnv * sz
        dwn = _colsum(dyh * nrm * sz)
        do = ro * (dn - nrm * jnp.mean(dn * nrm, axis=-1, keepdims=True))
        dq, dk, dv, da, db, dalog, ddtb, ds_in = _dn_chunk_bwd(m, q, v, a, alog, dtb, s_in, do, ds_ref[...])
        ds_ref[...] = ds_in
        lane = _iota2((DN_CHUNK, LANE), 1)
        prow = _iota2((8, LANE), 0)
        plane = _iota2((8, LANE), 1)
        dab = jnp.zeros((DN_CHUNK, LANE), F32)
        dpar = jnp.zeros((8, LANE), F32)
        for h in range(DN_HEADS):
            dz_ref[:, h * d:(h + 1) * d] = dz[h]
            dact_ref[:, h * d:(h + 1) * d] = dq[h]
            dact_ref[:, DN_WIDTH + h * d:DN_WIDTH + (h + 1) * d] = dk[h]
            dact_ref[:, 2 * DN_WIDTH + h * d:2 * DN_WIDTH + (h + 1) * d] = dv[h]
            dab = dab + jnp.where(lane == h, da[h], 0.0) + jnp.where(lane == DN_HEADS + h, db[h], 0.0)
            dpar = dpar + jnp.where((prow == 0) & (plane == h), dalog[h], 0.0)
            dpar = dpar + jnp.where((prow == 1) & (plane == h), ddtb[h], 0.0)
            dpar = dpar + jnp.where(prow == 2, dwn[h], 0.0)
        dab_ref[...] = dab
        dpar_ref[...] += dpar

    rev = lambda i: (n - 1 - i, 0)
    return _call(
        body, name="dn_bwd", grid=(n,),
        in_specs=[pl.BlockSpec((DN_CHUNK, 3 * DN_WIDTH), rev),
                  pl.BlockSpec((DN_CHUNK, DN_WIDTH), lambda i: (n - 1 - i, C_Z // DN_WIDTH)),
                  pl.BlockSpec((DN_CHUNK, LANE), lambda i: (n - 1 - i, C_AB // LANE)),
                  _full((8, LANE)), _full((1, d)),
                  pl.BlockSpec((1, DN_HEADS, d, d), lambda i: (n - 1 - i, 0, 0, 0)),
                  pl.BlockSpec((DN_CHUNK, DN_WIDTH), rev)],
        out_specs=[pl.BlockSpec((DN_CHUNK, 3 * DN_WIDTH), rev), pl.BlockSpec((DN_CHUNK, DN_WIDTH), rev),
                   pl.BlockSpec((DN_CHUNK, LANE), rev), _full((8, LANE))],
        out_shape=[jax.ShapeDtypeStruct((t, 3 * DN_WIDTH), F32), jax.ShapeDtypeStruct((t, DN_WIDTH), F32),
                   jax.ShapeDtypeStruct((t, LANE), F32), jax.ShapeDtypeStruct((8, LANE), F32)],
        scratch_shapes=[pltpu.VMEM((DN_HEADS, d, d), F32)],
    )(act, proj, proj, scal, wn, states, dy)


_INV_SQRT2 = 0.7071067811865476
_INV_SQRT2PI = 0.3989422804014327


def _gelu(x):
    return 0.5 * x * (1.0 + lax.erf(x * _INV_SQRT2))


def _dgelu(x):
    return 0.5 * (1.0 + lax.erf(x * _INV_SQRT2)) + x * jnp.exp(-0.5 * x * x) * _INV_SQRT2PI


def _gm_core(uv, lng, lnb, ws_ref, bst):
    c = uv.shape[0]
    zz = _gelu(uv)
    u = zz[:, :GM_WIDTH]
    vv = zz[:, GM_WIDTH:]
    xc = vv - jnp.mean(vv, axis=-1, keepdims=True)
    rs = lax.rsqrt(jnp.mean(xc * xc, axis=-1, keepdims=True) + NORM_EPS)
    xh = xc * rs
    vn = xh * lng + lnb
    grp = _iota2((c, GM_WIDTH), 1) // GM_GROUP_DIM
    tril = _iota2((c, c), 0) >= _iota2((c, c), 1)
    sv = jnp.zeros((c, GM_WIDTH), F32)
    masks = []
    for g in range(GM_GROUPS):
        mk = grp == g
        masks.append(mk)
        ws = jnp.where(tril, ws_ref[g], 0.0)
        sv = sv + _bdot(ws, jnp.where(mk, vn, 0.0)) + jnp.where(mk, bst[:, g:g + 1], 0.0)
    return u, xh, rs, vn, sv, masks, tril


def gm_fwd(proj, lng, lnb, w_s, bst):
    t = proj.shape[0]

    def body(uv_ref, g_ref, b_ref, ws_ref, bst_ref, y_ref):
        u, _, _, _, sv, _, _ = _gm_core(uv_ref[...], g_ref[...], b_ref[...], ws_ref, bst_ref[...])
        y_ref[...] = u * sv

    return _call(
        body, name="gm_fwd", grid=(t // GM_CHUNK,),
        in_specs=[pl.BlockSpec((GM_CHUNK, 2 * GM_WIDTH), lambda i: (i, C_UV // (2 * GM_WIDTH))),
                  _full((1, GM_WIDTH)), _full((1, GM_WIDTH)), _full((GM_GROUPS, GM_CHUNK, GM_CHUNK)),
                  _full((GM_CHUNK, LANE))],
        out_specs=pl.BlockSpec((GM_CHUNK, GM_WIDTH), lambda i: (i, 0)),
        out_shape=jax.ShapeDtypeStruct((t, GM_WIDTH), F32), semantics=("parallel",),
    )(proj, lng, lnb, w_s, bst)


def gm_bwd(proj, lng, lnb, w_s, bst, dy, dy_col=0):
    t = proj.shape[0]

    def body(uv_ref, g_ref, b_ref, ws_ref, bst_ref, dy_ref, duv_ref, dws_ref, dbst_ref, dln_ref):
        @pl.when(pl.program_id(0) == 0)
        def _():
            dws_ref[...] = jnp.zeros_like(dws_ref)
            dbst_ref[...] = jnp.zeros_like(dbst_ref)
            dln_ref[...] = jnp.zeros_like(dln_ref)

        uv = uv_ref[...]
        lng = g_ref[...]
        u, xh, rs, vn, sv, masks, tril = _gm_core(uv, lng, b_ref[...], ws_ref, bst_ref[...])
        dyv = dy_ref[...]
        dsv = dyv * u
        lane = _iota2((GM_CHUNK, LANE), 1)
        dvn = jnp.zeros_like(dsv)
        dbst = jnp.zeros((GM_CHUNK, LANE), F32)
        for g in range(GM_GROUPS):
            ws = jnp.where(tril, ws_ref[g], 0.0)
            dsg = jnp.where(masks[g], dsv, 0.0)
            dvn = dvn + jnp.where(masks[g], _bdot(ws, dsv, 0, 0), 0.0)
            dws_ref[g] += jnp.where(tril, _bdot(dsg, vn, 1, 1), 0.0)
            dbst = dbst + jnp.where(lane == g, _rowsum(dsg), 0.0)
        dbst_ref[...] += dbst
        row = _iota2((8, GM_WIDTH), 0)
        dln_ref[...] += jnp.where(row == 0, _colsum(dvn * xh), 0.0) + jnp.where(row == 1, _colsum(dvn), 0.0)
        dxh = dvn * lng
        dvv = rs * (dxh - jnp.mean(dxh, axis=-1, keepdims=True) - xh * jnp.mean(dxh * xh, axis=-1, keepdims=True))
        dg = _dgelu(uv)
        duv_ref[:, :GM_WIDTH] = dyv * sv * dg[:, :GM_WIDTH]
        duv_ref[:, GM_WIDTH:] = dvv * dg[:, GM_WIDTH:]

    return _call(
        body, name="gm_bwd", grid=(t // GM_CHUNK,),
        in_specs=[pl.BlockSpec((GM_CHUNK, 2 * GM_WIDTH), lambda i: (i, C_UV // (2 * GM_WIDTH))),
                  _full((1, GM_WIDTH)), _full((1, GM_WIDTH)), _full((GM_GROUPS, GM_CHUNK, GM_CHUNK)),
                  _full((GM_CHUNK, LANE)), pl.BlockSpec((GM_CHUNK, GM_WIDTH), lambda i: (i, dy_col))],
        out_specs=[pl.BlockSpec((GM_CHUNK, 2 * GM_WIDTH), lambda i: (i, 0)),
                   _full((GM_GROUPS, GM_CHUNK, GM_CHUNK)), _full((GM_CHUNK, LANE)), _full((8, GM_WIDTH))],
        out_shape=[jax.ShapeDtypeStruct((t, 2 * GM_WIDTH), F32),
                   jax.ShapeDtypeStruct((GM_GROUPS, GM_CHUNK, GM_CHUNK), F32),
                   jax.ShapeDtypeStruct((GM_CHUNK, LANE), F32), jax.ShapeDtypeStruct((8, GM_WIDTH), F32)],
    )(proj, lng, lnb, w_s, bst, dy)


def _head_mats():
    r = _iota2((SW_WIDTH, SW_WIDTH), 0)
    c = _iota2((SW_WIDTH, SW_WIDTH), 1)
    same = (r // SW_HEAD_DIM) == (c // SW_HEAD_DIM)
    cc = c % SW_HEAD_DIM
    half = ROPE_DIM // 2
    rot = jnp.where((cc < half) & (r == c + half), -1.0, 0.0) + jnp.where((cc >= half) & (cc < ROPE_DIM) & (r == c - half), 1.0, 0.0)
    return same.astype(F32), rot


def _seg_col(s):
    return C_SW // SW_WIDTH + (s // 2) * 3 + s % 2


def sw_prep(proj, nw2, cos_t, sin_t, *, tm=512):
    t = proj.shape[0]

    def body(x_ref, w_ref, c_ref, s_ref, o_ref):
        same, rot = _head_mats()
        x = x_ref[...]
        r = lax.rsqrt(_hdot(x * x, same) * (1.0 / SW_HEAD_DIM) + NORM_EPS)
        xn = x * r * w_ref[0]
        o_ref[0] = xn * c_ref[...] + _hdot(xn, rot) * s_ref[...]

    return _call(
        body, name="sw_prep", grid=(6, t // tm),
        in_specs=[pl.BlockSpec((tm, SW_WIDTH), lambda s, i: (i, _seg_col(s))),
                  pl.BlockSpec((1, 1, SW_WIDTH), lambda s, i: (s % 2, 0, 0)),
                  pl.BlockSpec((tm, SW_WIDTH), lambda s, i: (i, 0)),
                  pl.BlockSpec((tm, SW_WIDTH), lambda s, i: (i, 0))],
        out_specs=pl.BlockSpec((1, tm, SW_WIDTH), lambda s, i: (s, i, 0)),
        out_shape=jax.ShapeDtypeStruct((6, t, SW_WIDTH), F32), semantics=("parallel", "parallel"),
    )(proj, nw2, cos_t, sin_t)


def sw_prep_bwd(proj, nw2, cos_t, sin_t, dqk, *, tm=512):
    t = proj.shape[0]

    def body(x_ref, w_ref, c_ref, s_ref, d_ref, dx_ref, dw_ref):
        @pl.when(pl.program_id(1) == 0)
        def _():
            dw_ref[...] = jnp.zeros_like(dw_ref)

        same, rot = _head_mats()
        x = x_ref[...]
        w = w_ref[0]
        r = lax.rsqrt(_hdot(x * x, same) * (1.0 / SW_HEAD_DIM) + NORM_EPS)
        xh = x * r
        dout = d_ref[0]
        dxn = dout * c_ref[...] + _hdot(dout * s_ref[...], rot, 1, 1)
        dw_ref[0] += _colsum(dxn * xh)
        dxh = dxn * w
        dx_ref[0] = r * (dxh - xh * (_hdot(dxh * xh, same) * (1.0 / SW_HEAD_DIM)))

    return _call(
        body, name="sw_prep_bwd", grid=(6, t // tm),
        in_specs=[pl.BlockSpec((tm, SW_WIDTH), lambda s, i: (i, _seg_col(s))),
                  pl.BlockSpec((1, 1, SW_WIDTH), lambda s, i: (s % 2, 0, 0)),
                  pl.BlockSpec((tm, SW_WIDTH), lambda s, i: (i, 0)),
                  pl.BlockSpec((tm, SW_WIDTH), lambda s, i: (i, 0)),
                  pl.BlockSpec((1, tm, SW_WIDTH), lambda s, i: (s, i, 0))],
        out_specs=[pl.BlockSpec((1, tm, SW_WIDTH), lambda s, i: (s, i, 0)),
                   pl.BlockSpec((1, 1, SW_WIDTH), lambda s, i: (s, 0, 0))],
        out_shape=[jax.ShapeDtypeStruct((6, t, SW_WIDTH), F32), jax.ShapeDtypeStruct((6, 1, SW_WIDTH), F32)],
        semantics=("parallel", "arbitrary"),
    )(proj, nw2, cos_t, sin_t, dqk)


_SW_SCALE = SW_HEAD_DIM ** -0.5
_NEG = -1e30


def _sw_masks(has_prev):
    ri = _iota2((SW_BLOCK, SW_BLOCK), 0)
    ci = _iota2((SW_BLOCK, SW_BLOCK), 1)
    return ri >= ci, (ci >= ri) & has_prev


def _blk(j):
    return pl.BlockSpec((SW_HEADS, SW_BLOCK, SW_HEAD_DIM), j)


def sw_attn(q, k, v, nbs, *, name):
    _, t, _ = q.shape
    nblk = t // SW_BLOCK

    def body(q_ref, kc_ref, kp_ref, vc_ref, vp_ref, o_ref, l_ref):
        j = pl.program_id(0)
        mc, mp = _sw_masks((j % nbs) != 0)
        qh = q_ref[...]
        sc = jnp.where(mc, _bdot(qh, kc_ref[...], 1, 1) * _SW_SCALE, _NEG)
        sp = jnp.where(mp, _bdot(qh, kp_ref[...], 1, 1) * _SW_SCALE, _NEG)
        mx = jnp.maximum(jnp.max(sc, axis=-1, keepdims=True), jnp.max(sp, axis=-1, keepdims=True))
        pc = jnp.exp(sc - mx)
        pp = jnp.exp(sp - mx)
        den = _rowsum(pc) + _rowsum(pp)
        o_ref[...] = (_bdot(pc, vc_ref[...]) + _bdot(pp, vp_ref[...])) / den
        l_ref[...] = jnp.broadcast_to(mx + jnp.log(den), l_ref.shape)

    cur = lambda j: (0, j, 0)
    prev = lambda j: (0, jnp.maximum(j - 1, 0), 0)
    shp = jax.ShapeDtypeStruct((SW_HEADS, t, SW_HEAD_DIM), F32)
    return _call(
        body, name=name, grid=(nblk,),
        in_specs=[_blk(cur), _blk(cur), _blk(prev), _blk(cur), _blk(prev)],
        out_specs=[_blk(cur), _blk(cur)], out_shape=[shp, shp], semantics=("parallel",),
    )(q, k, k, v, v)


def sw_attn_dq(q, k, v, do, lg, dm, nbs, *, name):
    _, t, _ = q.shape
    nblk = t // SW_BLOCK

    def body(q_ref, kc_ref, kp_ref, vc_ref, vp_ref, do_ref, l_ref, d_ref, dq_ref):
        j = pl.program_id(0)
        mc, mp = _sw_masks((j % nbs) != 0)
        qh = q_ref[...]
        doh = do_ref[...]
        lse = l_ref[...][..., :1]
        dd = d_ref[...][..., :1]
        kc, kp = kc_ref[...], kp_ref[...]
        pc = jnp.exp(jnp.where(mc, _bdot(qh, kc, 1, 1) * _SW_SCALE, _NEG) - lse)
        pp = jnp.exp(jnp.where(mp, _bdot(qh, kp, 1, 1) * _SW_SCALE, _NEG) - lse)
        dsc = pc * (_bdot(doh, vc_ref[...], 1, 1) - dd)
        dsp = pp * (_bdot(doh, vp_ref[...], 1, 1) - dd)
        dq_ref[...] = (_bdot(dsc, kc) + _bdot(dsp, kp)) * _SW_SCALE

    cur = lambda j: (0, j, 0)
    prev = lambda j: (0, jnp.maximum(j - 1, 0), 0)
    return _call(
        body, name=name, grid=(nblk,),
        in_specs=[_blk(cur), _blk(cur), _blk(prev), _blk(cur), _blk(prev), _blk(cur), _blk(cur), _blk(cur)],
        out_specs=_blk(cur), out_shape=jax.ShapeDtypeStruct((SW_HEADS, t, SW_HEAD_DIM), F32),
        semantics=("parallel",),
    )(q, k, k, v, v, do, lg, dm)


def sw_attn_dkv(q, k, v, do, lg, dm, nbs, *, name):
    _, t, _ = q.shape
    nblk = t // SW_BLOCK

    def body(k_ref, v_ref, qc_ref, qn_ref, doc_ref, don_ref, lc_ref, ln_ref, dc_ref, dn_ref, dk_ref, dv_ref):
        j = pl.program_id(0)
        mc, mn = _sw_masks((j + 1 < nblk) & (((j + 1) % nbs) != 0))
        kh = k_ref[...]
        vh = v_ref[...]
        dk = jnp.zeros(dk_ref.shape, F32)
        dv = jnp.zeros(dv_ref.shape, F32)
        for q_ref, do_ref, l_ref, d_ref, mk in ((qc_ref, doc_ref, lc_ref, dc_ref, mc),
                                                (qn_ref, don_ref, ln_ref, dn_ref, mn)):
            qh = q_ref[...]
            doh = do_ref[...]
            p = jnp.exp(jnp.where(mk, _bdot(qh, kh, 1, 1) * _SW_SCALE, _NEG) - l_ref[...][..., :1])
            dv = dv + _bdot(p, doh, 0, 0)
            ds = p * (_bdot(doh, vh, 1, 1) - d_ref[...][..., :1])
            dk = dk + _bdot(ds, qh, 0, 0)
        dk_ref[...] = dk * _SW_SCALE
        dv_ref[...] = dv

    cur = lambda j: (0, j, 0)
    nxt = lambda j: (0, jnp.minimum(j + 1, nblk - 1), 0)
    shp = jax.ShapeDtypeStruct((SW_HEADS, t, SW_HEAD_DIM), F32)
    return _call(
        body, name=name, grid=(nblk,),
        in_specs=[_blk(cur), _blk(cur), _blk(cur), _blk(nxt), _blk(cur), _blk(nxt), _blk(cur), _blk(nxt),
                  _blk(cur), _blk(nxt)],
        out_specs=[_blk(cur), _blk(cur)], out_shape=[shp, shp], semantics=("parallel",),
    )(k, v, q, q, do, do, lg, lg, dm, dm)


def sw_merge(o3, l3, *, tm=512):
    _, t, w = o3.shape

    def body(o_ref, l_ref, y_ref, lg_ref):
        l0, l1, l2 = l_ref[0], l_ref[1], l_ref[2]
        mx = jnp.maximum(jnp.maximum(l0, l1), l2)
        lg = mx + jnp.log(jnp.exp(l0 - mx) + jnp.exp(l1 - mx) + jnp.exp(l2 - mx))
        lg_ref[...] = lg
        y_ref[...] = jnp.exp(l0 - lg) * o_ref[0] + jnp.exp(l1 - lg) * o_ref[1] + jnp.exp(l2 - lg) * o_ref[2]

    b3 = pl.BlockSpec((3, tm, w), lambda i: (0, i, 0))
    b1 = pl.BlockSpec((tm, w), lambda i: (i, 0))
    shp = jax.ShapeDtypeStruct((t, w), F32)
    return _call(body, name="sw_merge", grid=(t // tm,), in_specs=[b3, b3], out_specs=[b1, b1],
                 out_shape=[shp, shp], semantics=("parallel",))(o3, l3)


def sw_delta(dy, y, *, tm=512):
    t, w = y.shape

    def body(dy_ref, y_ref, o_ref):
        same, _ = _head_mats()
        o_ref[...] = _hdot(dy_ref[...] * y_ref[...], same)

    b1 = pl.BlockSpec((tm, w), lambda i: (i, 0))
    return _call(body, name="sw_delta", grid=(t // tm,), in_specs=[b1, b1], out_specs=b1,
                 out_shape=jax.ShapeDtypeStruct((t, w), F32), semantics=("parallel",))(dy, y)


def _to_sub(a, dil):
    t = a.shape[0]
    return a.reshape(t // dil, dil, SW_HEADS, SW_HEAD_DIM).transpose(2, 1, 0, 3).reshape(SW_HEADS, t, SW_HEAD_DIM)


def _from_sub(a, dil):
    t = a.shape[1]
    return a.reshape(SW_HEADS, dil, t // dil, SW_HEAD_DIM).transpose(2, 1, 0, 3).reshape(t, SW_WIDTH)


def _rope_tables(t):
    inv = ROPE_THETA ** (-jnp.arange(0, ROPE_DIM, 2, dtype=F32) / ROPE_DIM)
    ang = jnp.arange(t, dtype=F32)[:, None] * inv[None, :]
    pad1 = jnp.ones((t, SW_HEAD_DIM - ROPE_DIM), F32)
    pad0 = jnp.zeros((t, SW_HEAD_DIM - ROPE_DIM), F32)
    cos_h = jnp.concatenate([jnp.cos(ang), jnp.cos(ang), pad1], axis=1)
    sin_h = jnp.concatenate([jnp.sin(ang), jnp.sin(ang), pad0], axis=1)
    return jnp.tile(cos_h, (1, SW_HEADS)), jnp.tile(sin_h, (1, SW_HEADS))


def sw_forward(proj, nw2, cos_t, sin_t):
    t = proj.shape[0]
    qk = sw_prep(proj, nw2, cos_t, sin_t)
    subs, outs, lses = [], [], []
    for p, dil in enumerate(SW_DILATIONS):
        v0 = C_SW + 3 * SW_WIDTH * p + 2 * SW_WIDTH
        q = _to_sub(qk[2 * p], dil)
        k = _to_sub(qk[2 * p + 1], dil)
        v = _to_sub(proj[:, v0:v0 + SW_WIDTH], dil)
        o, lse = sw_attn(q, k, v, (t // dil) // SW_BLOCK, name=f"sw_attn{p}")
        subs.append((q, k, v))
        outs.append(_from_sub(o, dil))
        lses.append(_from_sub(lse, dil))
    y, lg = sw_merge(jnp.stack(outs), jnp.stack(lses))
    return y, (subs, y, lg)


def sw_backward(proj, nw2, cos_t, sin_t, res, dy):
    subs, y, lg = res
    t = proj.shape[0]
    dm = sw_delta(dy, y)
    dqk, dvs = [], []
    for p, dil in enumerate(SW_DILATIONS):
        q, k, v = subs[p]
        nbs = (t // dil) // SW_BLOCK
        do_s, lg_s, dm_s = _to_sub(dy, dil), _to_sub(lg, dil), _to_sub(dm, dil)
        dq = sw_attn_dq(q, k, v, do_s, lg_s, dm_s, nbs, name=f"sw_dq{p}")
        dk, dv = sw_attn_dkv(q, k, v, do_s, lg_s, dm_s, nbs, name=f"sw_dkv{p}")
        dqk += [_from_sub(dq, dil), _from_sub(dk, dil)]
        dvs.append(_from_sub(dv, dil))
    dx, dnw = sw_prep_bwd(proj, nw2, cos_t, sin_t, jnp.stack(dqk))
    cols = []
    for p in range(3):
        cols += [dx[2 * p], dx[2 * p + 1], dvs[p]]
    return cols, dnw


def _pad_rows(a, rows):
    return jnp.zeros((rows,) + a.shape[1:], a.dtype).at[:a.shape[0]].set(a)


def _layer_consts(sp, layer):
    d = {}
    d["mix_nw"] = sp["mix_norm_w"][layer][None, :]
    d["ffn_nw"] = sp["ffn_norm_w"][layer][None, :]
    d["cw8"] = _pad_rows(sp["dn_conv_w"][layer], 8)
    scal = jnp.zeros((8, LANE), F32)
    d["scal"] = scal.at[0, :DN_HEADS].set(sp["dn_a_log"][layer]).at[1, :DN_HEADS].set(sp["dn_dt_bias"][layer])
    d["wn"] = sp["dn_out_norm_w"][layer][None, :]
    d["lng"] = sp["gm_ln_g"][layer][None, :]
    d["lnb"] = sp["gm_ln_b"][layer][None, :]
    d["w_s"] = sp["gm_w_s"][layer]
    d["bst"] = jnp.zeros((GM_CHUNK, LANE), F32).at[:, :GM_GROUPS].set(sp["gm_b_s"][layer].T)
    d["nw2"] = jnp.stack([jnp.tile(sp["sw_q_norm_w"][layer], SW_HEADS),
                          jnp.tile(sp["sw_k_norm_w"][layer], SW_HEADS)])[:, None, :]
    return d


def _layer_fwd(x, mod, wb, cs, tabs):
    h1, proj = norm_mm(x, cs["mix_nw"], mod[1], mod[0], wb["w_in"], swiglu=False, name="in_proj")
    act = dn_conv(proj, cs["cw8"])
    ya, states = dn_fwd(act, proj, cs["scal"], cs["wn"])
    yb = gm_fwd(proj, cs["lng"], cs["lnb"], cs["w_s"], cs["bst"])
    yc, swres = sw_forward(proj, cs["nw2"], *tabs)
    y = jnp.concatenate([ya, yb, yc], axis=1)
    x1, o1 = resid_mm(y, wb["w_out"], x, mod[2], name="out_proj")
    h2, gu, actf = norm_mm(x1, cs["ffn_nw"], mod[4], mod[3], wb["w_ffn_in"], swiglu=True, name="ffn_in")
    x2, o2 = resid_mm(actf, wb["w_ffn_out"], x1, mod[5], name="ffn_out")
    res = dict(x=x, h1=h1, proj=proj, act=act, states=states, swres=swres, y=y, x1=x1, o1=o1, h2=h2, gu=gu,
               actf=actf, o2=o2)
    return x2, res


def _layer_bwd(dx2, res, mod, wb, cs, tabs):
    dgu, gx2, dgate2 = resid_mm_bwd(dx2, mod[5], res["o2"], wb["w_ffn_out"], res["gu"], name="ffn_out_bwd")
    g_wfo = mm_tn(res["actf"], gx2, name="wg_ffn_out")
    g_wfi = mm_tn(res["h2"], dgu, name="wg_ffn_in")
    dx1, d_ffn_nw, dscale2, dshift2 = norm_mm_bwd(dgu, wb["w_ffn_in"], res["x1"], cs["ffn_nw"], mod[4], dx2,
                                                  name="ffn_in_bwd")
    dy, gx1, dgate1 = resid_mm_bwd(dx1, mod[2], res["o1"], wb["w_out"], None, name="out_proj_bwd")
    g_wout = mm_tn(res["y"], gx1, name="wg_out")
    proj = res["proj"]
    dact, dz, dab, dpar = dn_bwd(res["act"], proj, cs["scal"], cs["wn"], res["states"], dy)
    dqkv, dcw = dn_conv_bwd(proj, cs["cw8"], dact)
    duv, dws, dbst, dln = gm_bwd(proj, cs["lng"], cs["lnb"], cs["w_s"], cs["bst"], dy,
                                 dy_col=DN_WIDTH // GM_WIDTH)
    cols, dnw = sw_backward(proj, cs["nw2"], *tabs, res["swres"], dy[:, DN_WIDTH + GM_WIDTH:])
    dproj = jnp.concatenate([dqkv, dz, duv] + cols + [dab], axis=1)
    g_win = mm_tn(res["h1"], dproj, name="wg_in")
    dx, d_mix_nw, dscale1, dshift1 = norm_mm_bwd(dproj, wb["w_in"], res["x"], cs["mix_nw"], mod[1], dx1,
                                                 name="in_proj_bwd")
    dmod = jnp.concatenate([dshift1, dscale1, dgate1, dshift2, dscale2, dgate2], axis=1)
    dnw = dnw.reshape(3, 2, SW_HEADS, SW_HEAD_DIM).sum((0, 2))
    small = dict(mix_norm_w=d_mix_nw[0], ffn_norm_w=d_ffn_nw[0], dn_conv_w=dcw[:DN_CONV],
                 dn_a_log=dpar[0, :DN_HEADS], dn_dt_bias=dpar[1, :DN_HEADS], dn_out_norm_w=dpar[2],
                 gm_ln_g=dln[0], gm_ln_b=dln[1], gm_w_s=dws, gm_b_s=dbst[:, :GM_GROUPS].T,
                 sw_q_norm_w=dnw[0], sw_k_norm_w=dnw[1])
    big = dict(w_in=g_win, w_out=g_wout, w_ffn_in=g_wfi, w_ffn_out=g_wfo)
    return dx, big, small, dmod


def _permute_w_in(w):
    pad = jnp.zeros(w.shape[:-1] + (IN_PAD - IN_WIDTH,), w.dtype)
    return jnp.concatenate([w[..., 0:2048], w[..., 2056:2568], w[..., 2568:IN_WIDTH], w[..., 2048:2056], pad], axis=-1)


def _unpermute_w_in(g):
    return jnp.concatenate([g[..., 0:2048], g[..., C_AB:C_AB + 8], g[..., 2048:2560], g[..., 2560:C_AB]], axis=-1)


def _local_step(x, target, mods, wfull, sp):
    layers = mods.shape[0]
    t, d = x.shape
    tabs = _rope_tables(t)
    saved = []
    for layer in range(layers):
        mod = mods[layer].reshape(6, 1, d)
        wb = {k: v[layer] for k, v in wfull.items()}
        cs = _layer_consts(sp, layer)
        x, res = _layer_fwd(x, mod, wb, cs, tabs)
        saved.append((res, mod, wb, cs))
    dx, loss = loss_head(x, target)
    bigs, smalls, dmods = [], [], []
    for layer in reversed(range(layers)):
        res, mod, wb, cs = saved[layer]
        dx, big, small, dmod = _layer_bwd(dx, res, mod, wb, cs, tabs)
        bigs.append(big)
        smalls.append(small)
        dmods.append(dmod[0])
    bigs, smalls, dmods = bigs[::-1], smalls[::-1], dmods[::-1]
    big = {k: [b[k] for b in bigs] for k in bigs[0]}
    small = {k: jnp.stack([s[k] for s in smalls]) for k in smalls[0]}
    return loss, dx, big, small, jnp.stack(dmods)


def mod_fwd(c_all, w_mod, b_shard):
    layers, d, n = w_mod.shape

    def body(c_ref, w_ref, b_ref, o_ref):
        ca = _silu(c_ref[...]).astype(BF16)
        o_ref[0] = _dot(ca, w_ref[0].astype(BF16), 1, 0) + b_ref[0]

    return _call(
        body, name="mod_fwd", grid=(layers,),
        in_specs=[_full((8, d)), pl.BlockSpec((1, d, n), lambda i: (i, 0, 0)),
                  pl.BlockSpec((1, 1, n), lambda i: (i, 0, 0))],
        out_specs=pl.BlockSpec((1, 8, n), lambda i: (i, 0, 0)),
        out_shape=jax.ShapeDtypeStruct((layers, 8, n), F32), semantics=("parallel",),
    )(c_all, w_mod, b_shard)


def mod_bwd(c_all, dmod):
    layers, _, n = dmod.shape
    d = c_all.shape[1]

    def body(c_ref, g_ref, o_ref):
        ca = _silu(c_ref[...]).astype(BF16)
        o_ref[0] = _dot(ca, g_ref[0].astype(BF16), 0, 0)

    return _call(
        body, name="mod_bwd", grid=(layers,),
        in_specs=[_full((8, d)), pl.BlockSpec((1, 8, n), lambda i: (i, 0, 0))],
        out_specs=pl.BlockSpec((1, d, n), lambda i: (i, 0, 0)),
        out_shape=jax.ShapeDtypeStruct((layers, d, n), F32), semantics=("parallel",),
    )(c_all, dmod)


N_DEV = 8
ANY = pl.BlockSpec(memory_space=pl.ANY)


def _place():
    return lax.axis_index("x"), lax.axis_index("y"), lax.axis_index("c")


def _other_chips(x, y):
    return [(1 - x, y), (x, 1 - y), (1 - x, 1 - y)]


def allgather8(x_shard, *, name):
    m_per, n = x_shard.shape

    def body(x_ref, out_ref, send_sems, recv_sems, local_sem):
        x, y, c = _place()
        me, sibling = (x, y, c), (x, y, 1 - c)
        chips = _other_chips(x, y)

        def rows(px, py, pc):
            return out_ref.at[pl.ds((4 * px + 2 * py + pc) * m_per, m_per), :]

        def copy(k, block, to, src=None):
            return pltpu.make_async_remote_copy(
                src_ref=rows(*block) if src is None else src, dst_ref=rows(*block),
                send_sem=send_sems.at[k], recv_sem=recv_sems.at[k], device_id=to, device_id_type=MESH)

        mine = pltpu.make_async_copy(x_ref, rows(*me), local_sem)
        mine.start()
        first = [copy(0, me, sibling, src=x_ref)]
        first += [copy(1 + j, me, (*chip, c), src=x_ref) for j, chip in enumerate(chips)]
        for cp in first:
            cp.start()
        passed = [copy(4 + j, (*chip, c), sibling) for j, chip in enumerate(chips)]
        for j, chip in enumerate(chips):
            copy(1 + j, (*chip, c), me).wait_recv()
            passed[j].start()
        copy(0, sibling, me).wait_recv()
        for j, chip in enumerate(chips):
            copy(4 + j, (*chip, 1 - c), me).wait_recv()
        for cp in first + passed:
            cp.wait_send()
        mine.wait()

    return pl.pallas_call(
        body, name=name, out_shape=jax.ShapeDtypeStruct((N_DEV * m_per, n), x_shard.dtype),
        in_specs=[pl.BlockSpec(memory_space=pltpu.VMEM)], out_specs=pl.BlockSpec(memory_space=pltpu.VMEM),
        scratch_shapes=[pltpu.SemaphoreType.DMA((7,)), pltpu.SemaphoreType.DMA((7,)), pltpu.SemaphoreType.DMA],
    )(x_shard)


def gather_shards(shards):
    n = len(shards)

    def body(*refs):
        ins, outs = refs[:n], refs[n:2 * n]
        send_sems, recv_sems, local_sems = refs[2 * n:]
        x, y, c = _place()
        chips = _other_chips(x, y)
        me_s = 2 * x + y
        local = []
        for a in range(n):
            cp = pltpu.make_async_copy(ins[a], outs[a].at[me_s], local_sems.at[a])
            cp.start()
            local.append(cp)
        sends = []
        for a in range(n):
            for j, chip in enumerate(chips):
                k = 3 * a + j
                cp = pltpu.make_async_remote_copy(
                    src_ref=ins[a], dst_ref=outs[a].at[me_s], send_sem=send_sems.at[k], recv_sem=recv_sems.at[k],
                    device_id=(*chip, c), device_id_type=MESH)
                cp.start()
                sends.append(cp)
        for a in range(n):
            for j, chip in enumerate(chips):
                k = 3 * a + j
                pltpu.make_async_remote_copy(
                    src_ref=ins[a], dst_ref=outs[a].at[2 * chip[0] + chip[1]], send_sem=send_sems.at[k],
                    recv_sem=recv_sems.at[k], device_id=(*chip, c), device_id_type=MESH).wait_recv()
        for cp in sends:
            cp.wait_send()
        for cp in local:
            cp.wait()

    return pl.pallas_call(
        body, name="gather_shards",
        out_shape=[jax.ShapeDtypeStruct((4,) + s.shape, s.dtype) for s in shards],
        in_specs=[ANY] * n, out_specs=[ANY] * n,
        scratch_shapes=[pltpu.SemaphoreType.DMA((3 * n,)), pltpu.SemaphoreType.DMA((3 * n,)),
                        pltpu.SemaphoreType.DMA((n,))],
    )(*shards)


def scatter_slices(grads):
    n = len(grads)

    def body(*refs):
        ins, outs = refs[:n], refs[n:2 * n]
        send_sems, recv_sems, local_sems = refs[2 * n:]
        x, y, c = _place()
        chips = _other_chips(x, y)
        me_s = 2 * x + y
        local = []
        for a in range(n):
            cp = pltpu.make_async_copy(ins[a].at[me_s], outs[a].at[me_s], local_sems.at[a])
            cp.start()
            local.append(cp)
        sends = []
        for a in range(n):
            for j, chip in enumerate(chips):
                k = 3 * a + j
                cp = pltpu.make_async_remote_copy(
                    src_ref=ins[a].at[2 * chip[0] + chip[1]], dst_ref=outs[a].at[me_s], send_sem=send_sems.at[k],
                    recv_sem=recv_sems.at[k], device_id=(*chip, c), device_id_type=MESH)
                cp.start()
                sends.append(cp)
        for a in range(n):
            for j, chip in enumerate(chips):
                k = 3 * a + j
                pltpu.make_async_remote_copy(
                    src_ref=ins[a].at[me_s], dst_ref=outs[a].at[2 * chip[0] + chip[1]], send_sem=send_sems.at[k],
                    recv_sem=recv_sems.at[k], device_id=(*chip, c), device_id_type=MESH).wait_recv()
        for cp in sends:
            cp.wait_send()
        for cp in local:
            cp.wait()

    return pl.pallas_call(
        body, name="scatter_slices",
        out_shape=[jax.ShapeDtypeStruct(g.shape, g.dtype) for g in grads],
        in_specs=[ANY] * n, out_specs=[ANY] * n,
        scratch_shapes=[pltpu.SemaphoreType.DMA((3 * n,)), pltpu.SemaphoreType.DMA((3 * n,)),
                        pltpu.SemaphoreType.DMA((n,))],
    )(*grads)


def sibling_swap(parts):
    n = len(parts)

    def body(*refs):
        ins, outs = refs[:n], refs[n:2 * n]
        send_sems, recv_sems = refs[2 * n:]
        x, y, c = _place()
        cps = []
        for a in range(n):
            cp = pltpu.make_async_remote_copy(
                src_ref=ins[a], dst_ref=outs[a], send_sem=send_sems.at[a], recv_sem=recv_sems.at[a],
                device_id=(x, y, 1 - c), device_id_type=MESH)
            cp.start()
            cps.append(cp)
        for cp in cps:
            cp.wait()

    return pl.pallas_call(
        body, name="sibling_swap", out_shape=[jax.ShapeDtypeStruct(p.shape, p.dtype) for p in parts],
        in_specs=[ANY] * n, out_specs=[ANY] * n,
        scratch_shapes=[pltpu.SemaphoreType.DMA((n,)), pltpu.SemaphoreType.DMA((n,))],
    )(*parts)


def _row_block(rows, cols, budget=1 << 20):
    best = rows if rows % 8 else 8
    for tr in range(8, rows + 1, 8):
        if rows % tr == 0 and tr * cols * 4 <= budget:
            best = tr
    return best


def chip_sum(own, recv, me_s, *, name):
    r, n = own.shape
    tr = _row_block(r, n)

    def body(me_ref, own_ref, recv_ref, o_ref):
        me = me_ref[0]
        acc = jnp.zeros((tr, n), F32)
        for s in range(4):
            acc = acc + jnp.where(me == s, own_ref[...], recv_ref[s].astype(F32))
        o_ref[...] = acc

    return pl.pallas_call(
        body, name=name, out_shape=jax.ShapeDtypeStruct((r, n), F32),
        grid_spec=pltpu.PrefetchScalarGridSpec(
            num_scalar_prefetch=1, grid=(r // tr,),
            in_specs=[pl.BlockSpec((tr, n), lambda i, me: (i, 0)), pl.BlockSpec((4, tr, n), lambda i, me: (0, i, 0))],
            out_specs=pl.BlockSpec((tr, n), lambda i, me: (i, 0))),
        compiler_params=pltpu.CompilerParams(dimension_semantics=("parallel",)),
    )(me_s, own, recv)


def _adam_update(w, g, m, v):
    m2 = ADAM_B1 * m + (1.0 - ADAM_B1) * g
    v2 = ADAM_B2 * v + (1.0 - ADAM_B2) * (g * g)
    m_hat = m2 / (1.0 - ADAM_B1 ** ADAM_STEP)
    v_hat = v2 / (1.0 - ADAM_B2 ** ADAM_STEP)
    delta = -ADAM_LR * (m_hat / (jnp.sqrt(v_hat) + ADAM_EPS) + ADAM_WD * w)
    return delta, m2, v2


def adamw(w, g_parts, m, v, *, name):
    r, n = w.shape
    tr = _row_block(r, n)
    k = len(g_parts)

    def body(*refs):
        w_ref, m_ref, v_ref = refs[k], refs[k + 1], refs[k + 2]
        g_ref, d_ref, m2_ref, v2_ref = refs[k + 3:]
        g = refs[0][...]
        for p in refs[1:k]:
            g = g + p[...]
        g_ref[...] = g
        d_ref[...], m2_ref[...], v2_ref[...] = _adam_update(w_ref[...], g, m_ref[...], v_ref[...])

    blk = pl.BlockSpec((tr, n), lambda i: (i, 0))
    shp = jax.ShapeDtypeStruct((r, n), F32)
    return _call(body, name=name, grid=(r // tr,), in_specs=[blk] * (k + 3), out_specs=[blk] * 4,
                 out_shape=[shp] * 4, semantics=("parallel",))(*g_parts, w, m, v)


def adamw_gathered(g_all, w, m, v, *, name):
    _, r, n = g_all.shape
    tr = _row_block(r, n * 4)

    def body(ga_ref, w_ref, m_ref, v_ref, g_ref, d_ref, m2_ref, v2_ref):
        g = ga_ref[0]
        for dev in range(1, N_DEV):
            g = g + ga_ref[dev]
        g_ref[...] = g
        d_ref[...], m2_ref[...], v2_ref[...] = _adam_update(w_ref[...], g, m_ref[...], v_ref[...])

    blk = pl.BlockSpec((tr, n), lambda i: (i, 0))
    shp = jax.ShapeDtypeStruct((r, n), F32)
    return _call(body, name=name, grid=(r // tr,),
                 in_specs=[pl.BlockSpec((N_DEV, tr, n), lambda i: (0, i, 0)), blk, blk, blk], out_specs=[blk] * 4,
                 out_shape=[shp] * 4, semantics=("parallel",))(g_all, w, m, v)


BIG = ("w_in", "w_out", "w_ffn_in", "w_ffn_out")
SMALL = ("b_mod", "mix_norm_w", "ffn_norm_w", "dn_conv_w", "dn_a_log", "dn_dt_bias", "dn_out_norm_w", "gm_ln_g",
         "gm_ln_b", "gm_w_s", "gm_b_s", "sw_q_norm_w", "sw_k_norm_w")
WEIGHTS = ("w_mod", "b_mod", "mix_norm_w", "ffn_norm_w", "w_in", "w_out", "dn_conv_w", "dn_a_log", "dn_dt_bias",
           "dn_out_norm_w", "gm_ln_g", "gm_ln_b", "gm_w_s", "gm_b_s", "sw_q_norm_w", "sw_k_norm_w", "w_ffn_in",
           "w_ffn_out")
PACK_ROWS = 8


def _pack(arrs):
    out = []
    for a in arrs:
        flat = a.reshape(-1).astype(F32)
        rows = -(-flat.shape[0] // (LANE * PACK_ROWS)) * PACK_ROWS
        out.append(jnp.zeros((rows * LANE,), F32).at[:flat.shape[0]].set(flat).reshape(rows, LANE))
    return jnp.concatenate(out, axis=0)


def _unpack(packed, shapes):
    out, r0 = [], 0
    for shp in shapes:
        size = math.prod(shp)
        rows = -(-size // (LANE * PACK_ROWS)) * PACK_ROWS
        out.append(packed[r0:r0 + rows].reshape(-1)[:size].reshape(shp))
        r0 += rows
    return out


def _shard_major(per_layer, axis, unpermute=False):
    g = jnp.stack(per_layer)
    if unpermute:
        g = _unpermute_w_in(g)
    return jnp.stack(jnp.split(g, 4, axis=axis + 1))


def kernel(x, c, w_mod, b_mod, mix_norm_w, ffn_norm_w, w_in, w_out, dn_conv_w, dn_a_log, dn_dt_bias, dn_out_norm_w, gm_ln_g, gm_ln_b, gm_w_s, gm_b_s, sw_q_norm_w, sw_k_norm_w, w_ffn_in, w_ffn_out, loss_target, m_w_mod, m_b_mod, m_mix_norm_w, m_ffn_norm_w, m_w_in, m_w_out, m_dn_conv_w, m_dn_a_log, m_dn_dt_bias, m_dn_out_norm_w, m_gm_ln_g, m_gm_ln_b, m_gm_w_s, m_gm_b_s, m_sw_q_norm_w, m_sw_k_norm_w, m_w_ffn_in, m_w_ffn_out, v_w_mod, v_b_mod, v_mix_norm_w, v_ffn_norm_w, v_w_in, v_w_out, v_dn_conv_w, v_dn_a_log, v_dn_dt_bias, v_dn_out_norm_w, v_gm_ln_g, v_gm_ln_b, v_gm_w_s, v_gm_b_s, v_sw_q_norm_w, v_sw_k_norm_w, v_w_ffn_in, v_w_ffn_out):
    w = dict(w_mod=w_mod, b_mod=b_mod, mix_norm_w=mix_norm_w, ffn_norm_w=ffn_norm_w, w_in=w_in, w_out=w_out,
             dn_conv_w=dn_conv_w, dn_a_log=dn_a_log, dn_dt_bias=dn_dt_bias, dn_out_norm_w=dn_out_norm_w,
             gm_ln_g=gm_ln_g, gm_ln_b=gm_ln_b, gm_w_s=gm_w_s, gm_b_s=gm_b_s, sw_q_norm_w=sw_q_norm_w,
             sw_k_norm_w=sw_k_norm_w, w_ffn_in=w_ffn_in, w_ffn_out=w_ffn_out)
    m = dict(w_mod=m_w_mod, b_mod=m_b_mod, mix_norm_w=m_mix_norm_w, ffn_norm_w=m_ffn_norm_w, w_in=m_w_in,
             w_out=m_w_out, dn_conv_w=m_dn_conv_w, dn_a_log=m_dn_a_log, dn_dt_bias=m_dn_dt_bias,
             dn_out_norm_w=m_dn_out_norm_w, gm_ln_g=m_gm_ln_g, gm_ln_b=m_gm_ln_b, gm_w_s=m_gm_w_s, gm_b_s=m_gm_b_s,
             sw_q_norm_w=m_sw_q_norm_w, sw_k_norm_w=m_sw_k_norm_w, w_ffn_in=m_w_ffn_in, w_ffn_out=m_w_ffn_out)
    v = dict(w_mod=v_w_mod, b_mod=v_b_mod, mix_norm_w=v_mix_norm_w, ffn_norm_w=v_ffn_norm_w, w_in=v_w_in,
             w_out=v_w_out, dn_conv_w=v_dn_conv_w, dn_a_log=v_dn_a_log, dn_dt_bias=v_dn_dt_bias,
             dn_out_norm_w=v_dn_out_norm_w, gm_ln_g=v_gm_ln_g, gm_ln_b=v_gm_ln_b, gm_w_s=v_gm_w_s, gm_b_s=v_gm_b_s,
             sw_q_norm_w=v_sw_q_norm_w, sw_k_norm_w=v_sw_k_norm_w, w_ffn_in=v_w_ffn_in, w_ffn_out=v_w_ffn_out)
    layers, d, mod_n = w_mod.shape
    mx, my, mc = _place()
    me_s = 2 * mx + my
    me_dev = 4 * mx + 2 * my + mc

    c_all = allgather8(_pad_rows(c, 8), name="gather_c").reshape(N_DEV, 8, d)[:, 0]
    b_shard = lax.dynamic_slice_in_dim(b_mod, me_s * mod_n, mod_n, axis=1)[:, None, :]
    mod_part = mod_fwd(c_all, w_mod, b_shard)
    mod_parts = allgather8(mod_part.reshape(layers * 8, mod_n), name="gather_mod")
    mod_parts = mod_parts.reshape(4, 2, layers, 8, mod_n)[:, 0]
    mod_all = mod_parts.transpose(1, 2, 0, 3).reshape(layers, 8, 4 * mod_n)
    mods = lax.dynamic_index_in_dim(mod_all, me_dev, axis=1, keepdims=False)

    gathered = gather_shards([w[k].astype(BF16) for k in BIG])
    g_in, g_out, g_fi, g_fo = gathered
    wfull = dict(
        w_in=_permute_w_in(jnp.concatenate([g_in[s] for s in range(4)], axis=-1)),
        w_out=jnp.concatenate([g_out[s] for s in range(4)], axis=1),
        w_ffn_in=jnp.concatenate([g_fi[s] for s in range(4)], axis=-1),
        w_ffn_out=jnp.concatenate([g_fo[s] for s in range(4)], axis=1))

    cw = dn_conv_w.shape[-1]
    conv_rows = -(-layers * DN_CONV // 8) * 8
    conv_parts = allgather8(_pad_rows(dn_conv_w.reshape(layers * DN_CONV, cw), conv_rows), name="gather_conv")
    conv_parts = conv_parts.reshape(4, 2, conv_rows, cw)[:, 0, :layers * DN_CONV]
    conv_full = conv_parts.reshape(4, layers, DN_CONV, cw).transpose(1, 2, 0, 3).reshape(layers, DN_CONV, 4 * cw)

    sp = {k: w[k] for k in SMALL}
    sp["dn_conv_w"] = conv_full
    loss_blk, grad_x, big, small, dmods = _local_step(x[0], loss_target[0], mods, wfull, sp)
    loss = lax.psum(loss_blk[0, 0], ("x", "y", "c"))

    axes = dict(w_in=2, w_out=1, w_ffn_in=2, w_ffn_out=1)
    stacked = {k: _shard_major(big[k], axes[k] - 1, unpermute=(k == "w_in")) for k in BIG}
    recv = scatter_slices([stacked[k].astype(BF16) for k in BIG])
    me_arr = jnp.reshape(me_s, (1,)).astype(jnp.int32)
    partial = []
    for k, r in zip(BIG, recv):
        own = lax.dynamic_index_in_dim(stacked[k], me_s, axis=0, keepdims=False)
        n_last = own.shape[-1]
        partial.append(chip_sum(own.reshape(-1, n_last), r.reshape(4, -1, n_last), me_arr, name="chip_sum_" + k))
    theirs = sibling_swap(partial)
    outs = {}
    for k, mine, other in zip(BIG, partial, theirs):
        shp = w[k].shape
        flat = lambda a: a.reshape(-1, shp[-1])
        res = adamw(flat(w[k]), [mine, other], flat(m[k]), flat(v[k]), name="adamw_" + k)
        outs[k] = [a.reshape(shp) for a in res]

    small = dict(small, b_mod=dmods)
    packed = _pack([small[k] for k in SMALL])
    rows = packed.shape[0]
    g_all = allgather8(packed, name="gather_small").reshape(N_DEV, rows, LANE)
    conv_zero = jnp.zeros((layers, DN_CONV, 3 * DN_WIDTH), F32)
    pk = lambda src: _pack([conv_zero if k == "dn_conv_w" else src[k] for k in SMALL])
    res = adamw_gathered(g_all, pk(w), pk(m), pk(v), name="adamw_small")
    shapes = [small[k].shape for k in SMALL]
    un = [_unpack(a, shapes) for a in res]
    for i, k in enumerate(SMALL):
        outs[k] = [un[j][i] for j in range(4)]
    g_conv = lax.dynamic_slice_in_dim(outs["dn_conv_w"][0], me_s * cw, cw, axis=2)
    flat = lambda a: a.reshape(-1, cw)
    res = adamw(flat(dn_conv_w), [flat(g_conv)], flat(m["dn_conv_w"]), flat(v["dn_conv_w"]), name="adamw_conv")
    outs["dn_conv_w"] = [a.reshape(dn_conv_w.shape) for a in res]

    b_rows = layers * 6 * d // LANE
    dmod_all = g_all[:, :b_rows].reshape(N_DEV, layers, 6 * d).transpose(1, 0, 2)
    dmod_shard = lax.dynamic_slice_in_dim(dmod_all, me_s * mod_n, mod_n, axis=2)
    g_wmod = mod_bwd(c_all, dmod_shard)
    flat = lambda a: a.reshape(-1, mod_n)
    res = adamw(flat(w_mod), [flat(g_wmod)], flat(m_w_mod), flat(v_w_mod), name="adamw_w_mod")
    outs["w_mod"] = [a.reshape(w_mod.shape) for a in res]

    result = [loss, grad_x[None]]
    for j in range(4):
        result += [outs[k][j] for k in WEIGHTS]
    return tuple(result)
```

```python
import functools
import math

import jax
import jax.numpy as jnp
from jax import lax
from jax.experimental import pallas as pl
from jax.experimental.pallas import tpu as pltpu

F32 = jnp.float32
BF16 = jnp.bfloat16
HI = lax.Precision.HIGH

NORM_EPS = 1e-6
DN_HEADS = 4
DN_HEAD_DIM = 128
DN_WIDTH = 512
DN_CHUNK = 64
DN_CONV = 4
GM_WIDTH = 256
GM_GROUPS = 4
GM_GROUP_DIM = 64
GM_CHUNK = 128
SW_HEADS = 4
SW_HEAD_DIM = 64
SW_WIDTH = 256
SW_DILATIONS = (1, 4, 16)
SW_BLOCK = 128
ROPE_THETA = 500000.0
ROPE_DIM = 16
LANE = 128

C_QKV = 0
C_Z = 1536
C_AB = 2048
C_SW = 2304
C_UV = 4608
IN_WIDTH = 4872
IN_PAD = 5120
AB_PAD = C_SW - C_AB
MIX_WIDTH = 1024

ADAM_LR = 0.001
ADAM_B1 = 0.9
ADAM_B2 = 0.999
ADAM_EPS = 1e-08
ADAM_WD = 0.01
ADAM_STEP = 10

MESH = pl.DeviceIdType.MESH


def _call(body, *, name, grid, in_specs, out_specs, out_shape, scratch_shapes=(), semantics=None, aliases=None):
    if semantics is None:
        semantics = ("arbitrary",) * len(grid)
    return pl.pallas_call(
        body, name=name, grid=grid, in_specs=in_specs, out_specs=out_specs, out_shape=out_shape,
        scratch_shapes=list(scratch_shapes), input_output_aliases=aliases or {},
        compiler_params=pltpu.CompilerParams(dimension_semantics=semantics),
    )


def _dot(a, b, ca, cb, prec=None):
    if a.ndim == 3:
        dims = (((ca + 1,), (cb + 1,)), ((0,), (0,)))
    else:
        dims = (((ca,), (cb,)), ((), ()))
    return lax.dot_general(a, b, dims, preferred_element_type=F32, precision=prec)


def _bdot(a, b, ca=1, cb=0):
    return _dot(a.astype(BF16), b.astype(BF16), ca, cb)


def _hdot(a, b, ca=1, cb=0):
    return _dot(a.astype(F32), b.astype(F32), ca, cb, HI)


def _sigmoid(x):
    return 1.0 / (1.0 + jnp.exp(-x))


def _silu(x):
    return x * _sigmoid(x)


def _dsilu(x):
    s = _sigmoid(x)
    return s * (1.0 + x * (1.0 - s))


def _softplus(x):
    return jnp.maximum(x, 0.0) + jnp.log(1.0 + jnp.exp(-jnp.abs(x)))


def _iota2(shape, dim):
    return lax.broadcasted_iota(jnp.int32, shape, dim)


def _rowsum(x):
    return jnp.sum(x, axis=-1, keepdims=True)


def _colsum(x):
    return jnp.sum(x, axis=-2, keepdims=True)


def _full(shape):
    return pl.BlockSpec(shape, lambda *_: (0,) * len(shape))


ANY = pl.BlockSpec(memory_space=pl.ANY)


def _norm_mod(x, nw, scale, shift):
    r = lax.rsqrt(jnp.mean(x * x, axis=-1, keepdims=True) + NORM_EPS)
    xn = x * r
    return xn, r, (xn * nw) * (1.0 + scale) + shift


def norm_mm(x, nw, scale, shift, w, *, swiglu, name, tm=256):
    t, d = x.shape
    n = w.shape[1]
    half = n // 2

    def body(x_ref, nw_ref, sc_ref, sh_ref, w_ref, h_ref, y_ref, *act_ref):
        _, _, h = _norm_mod(x_ref[...], nw_ref[...], sc_ref[...], sh_ref[...])
        hb = h.astype(BF16)
        h_ref[...] = hb
        y = _dot(hb, w_ref[...], 1, 0)
        y_ref[...] = y
        if swiglu:
            act_ref[0][...] = (_silu(y[:, :half]) * y[:, half:]).astype(BF16)

    row = lambda i: (i, 0)
    out_shape = [jax.ShapeDtypeStruct((t, d), BF16), jax.ShapeDtypeStruct((t, n), F32)]
    out_specs = [pl.BlockSpec((tm, d), row), pl.BlockSpec((tm, n), row)]
    if swiglu:
        out_shape.append(jax.ShapeDtypeStruct((t, half), BF16))
        out_specs.append(pl.BlockSpec((tm, half), row))
    return _call(
        body, name=name, grid=(t // tm,),
        in_specs=[pl.BlockSpec((tm, d), row), _full((1, d)), _full((1, d)), _full((1, d)), _full((d, n))],
        out_specs=out_specs, out_shape=out_shape, semantics=("parallel",),
    )(x, nw, scale, shift, w)


def resid_mm(y, w, x, gate, *, name, tm=256):
    t, k = y.shape
    d = w.shape[1]

    def body(y_ref, w_ref, x_ref, g_ref, xo_ref, o_ref):
        o = _dot(y_ref[...].astype(BF16), w_ref[...], 1, 0)
        o_ref[...] = o
        xo_ref[...] = x_ref[...] + g_ref[...] * o

    row = lambda i: (i, 0)
    return _call(
        body, name=name, grid=(t // tm,),
        in_specs=[pl.BlockSpec((tm, k), row), _full((k, d)), pl.BlockSpec((tm, d), row), _full((1, d))],
        out_specs=[pl.BlockSpec((tm, d), row), pl.BlockSpec((tm, d), row)],
        out_shape=[jax.ShapeDtypeStruct((t, d), F32), jax.ShapeDtypeStruct((t, d), F32)],
        semantics=("parallel",),
    )(y, w, x, gate)


def resid_mm_bwd(dx, gate, o, w, gu, *, name, tm=256):
    t, d = dx.shape
    k = w.shape[0]
    swiglu = gu is not None

    def body(dx_ref, g_ref, o_ref, w_ref, *rest):
        if swiglu:
            gu_ref, dy_ref, gx_ref, dg_ref = rest
        else:
            dy_ref, gx_ref, dg_ref = rest
        i = pl.program_id(0)
        dxv = dx_ref[...]
        gx = (dxv * g_ref[...]).astype(BF16)
        gx_ref[...] = gx
        part = _colsum(dxv * o_ref[...])

        @pl.when(i == 0)
        def _():
            dg_ref[...] = jnp.zeros_like(dg_ref)

        dg_ref[...] += part
        da = _dot(gx, w_ref[...], 1, 1)
        if swiglu:
            g = gu_ref[:, :k]
            u = gu_ref[:, k:]
            dy_ref[:, :k] = (da * u * _dsilu(g)).astype(BF16)
            dy_ref[:, k:] = (da * _silu(g)).astype(BF16)
        else:
            dy_ref[...] = da

    row = lambda i: (i, 0)
    in_specs = [pl.BlockSpec((tm, d), row), _full((1, d)), pl.BlockSpec((tm, d), row), _full((k, d))]
    args = [dx, gate, o, w]
    if swiglu:
        in_specs.append(pl.BlockSpec((tm, 2 * k), row))
        args.append(gu)
        dy_shape = jax.ShapeDtypeStruct((t, 2 * k), BF16)
        dy_spec = pl.BlockSpec((tm, 2 * k), row)
    else:
        dy_shape = jax.ShapeDtypeStruct((t, k), F32)
        dy_spec = pl.BlockSpec((tm, k), row)
    return _call(
        body, name=name, grid=(t // tm,), in_specs=in_specs,
        out_specs=[dy_spec, pl.BlockSpec((tm, d), row), _full((1, d))],
        out_shape=[dy_shape, jax.ShapeDtypeStruct((t, d), BF16), jax.ShapeDtypeStruct((1, d), F32)],
    )(*args)


def norm_mm_bwd(dy, w, x, nw, scale, dres, *, name, tm=256):
    t, n = dy.shape
    d = x.shape[1]
    steps = t // tm

    def body(dy_ref, w_ref, x_ref, nw_ref, sc_ref, dres_ref, dx_ref, dnw_ref, dsc_ref, dsh_ref):
        i = pl.program_id(0)
        dh = _dot(dy_ref[...].astype(BF16), w_ref[...], 1, 1)
        x = x_ref[...]
        r = lax.rsqrt(jnp.mean(x * x, axis=-1, keepdims=True) + NORM_EPS)
        xn = x * r
        a = nw_ref[...] * (1.0 + sc_ref[...])

        @pl.when(i == 0)
        def _():
            dnw_ref[...] = jnp.zeros_like(dnw_ref)
            dsh_ref[...] = jnp.zeros_like(dsh_ref)

        dnw_ref[...] += _colsum(dh * xn)
        dsh_ref[...] += _colsum(dh)
        dxn = dh * a
        dx_ref[...] = r * (dxn - xn * jnp.mean(dxn * xn, axis=-1, keepdims=True)) + dres_ref[...]

        @pl.when(i == steps - 1)
        def _():
            da = dnw_ref[...]
            dsc_ref[...] = da * nw_ref[...]
            dnw_ref[...] = da * (1.0 + sc_ref[...])

    row = lambda i: (i, 0)
    vec = jax.ShapeDtypeStruct((1, d), F32)
    return _call(
        body, name=name, grid=(steps,),
        in_specs=[pl.BlockSpec((tm, n), row), _full((d, n)), pl.BlockSpec((tm, d), row), _full((1, d)),
                  _full((1, d)), pl.BlockSpec((tm, d), row)],
        out_specs=[pl.BlockSpec((tm, d), row), _full((1, d)), _full((1, d)), _full((1, d))],
        out_shape=[jax.ShapeDtypeStruct((t, d), F32), vec, vec, vec],
    )(dy, w, x, nw, scale, dres)


def _pick_tn(n, k, budget=6 << 20):
    best = LANE
    for m in range(1, n // LANE + 1):
        tn = m * LANE
        if n % tn == 0 and k * tn * 4 <= budget:
            best = tn
    return best


def mm_tn(a, g, *, name, tt=512):
    t, k = a.shape
    n = g.shape[1]
    tn = _pick_tn(n, k)

    def body(a_ref, g_ref, o_ref):
        @pl.when(pl.program_id(1) == 0)
        def _():
            o_ref[...] = jnp.zeros_like(o_ref)

        o_ref[...] += _dot(a_ref[...].astype(BF16), g_ref[...].astype(BF16), 0, 0)

    return _call(
        body, name=name, grid=(n // tn, t // tt),
        in_specs=[pl.BlockSpec((tt, k), lambda j, i: (i, 0)), pl.BlockSpec((tt, tn), lambda j, i: (i, j))],
        out_specs=pl.BlockSpec((k, tn), lambda j, i: (0, j)),
        out_shape=jax.ShapeDtypeStruct((k, n), F32), semantics=("parallel", "arbitrary"),
    )(a, g)


def loss_head(y, target, *, tm=512):
    t, d = y.shape
    steps = t // tm

    def body(y_ref, t_ref, dy_ref, l_ref, acc_ref):
        i = pl.program_id(0)

        @pl.when(i == 0)
        def _():
            acc_ref[...] = jnp.zeros_like(acc_ref)

        e = y_ref[...] - t_ref[...]
        dy_ref[...] = e * (1.0 / d)
        acc_ref[...] += _colsum(e * e)

        @pl.when(i == steps - 1)
        def _():
            tot = jnp.sum(acc_ref[...], axis=-1, keepdims=True) * (0.5 / d)
            l_ref[...] = jnp.broadcast_to(tot, l_ref.shape)

    row = lambda i: (i, 0)
    return _call(
        body, name="loss_head", grid=(steps,),
        in_specs=[pl.BlockSpec((tm, d), row), pl.BlockSpec((tm, d), row)],
        out_specs=[pl.BlockSpec((tm, d), row), _full((8, LANE))],
        out_shape=[jax.ShapeDtypeStruct((t, d), F32), jax.ShapeDtypeStruct((8, LANE), F32)],
        scratch_shapes=[pltpu.VMEM((1, d), F32)],
    )(y, target)


def _shift_rows(x, s):
    if s == 0:
        return x
    t = x.shape[0]
    ri = _iota2(x.shape, 0)
    rolled = pltpu.roll(x, s % t, axis=0)
    if s > 0:
        return jnp.where(ri >= s, rolled, 0.0)
    return jnp.where(ri < t + s, rolled, 0.0)


def _conv_pre(x, w):
    acc = x * w[DN_CONV - 1:DN_CONV, :]
    for j in range(DN_CONV - 1):
        acc = acc + _shift_rows(x, DN_CONV - 1 - j) * w[j:j + 1, :]
    return acc


def dn_conv(proj, conv_w):
    t = proj.shape[0]
    width = 3 * DN_WIDTH

    def body(x_ref, w_ref, o_ref):
        o_ref[...] = _silu(_conv_pre(x_ref[...], w_ref[...]))

    col = lambda j: (0, j)
    return _call(
        body, name="dn_conv", grid=(width // LANE,),
        in_specs=[pl.BlockSpec((t, LANE), col), pl.BlockSpec((8, LANE), col)],
        out_specs=pl.BlockSpec((t, LANE), col),
        out_shape=jax.ShapeDtypeStruct((t, width), F32), semantics=("parallel",),
    )(proj, conv_w)


def dn_conv_bwd(proj, conv_w, dact, dproj):
    t = proj.shape[0]
    width = 3 * DN_WIDTH

    def body(x_ref, w_ref, d_ref, _, dx_ref, dw_ref):
        x = x_ref[...]
        w = w_ref[...]
        dc = d_ref[...] * _dsilu(_conv_pre(x, w))
        dx = dc * w[DN_CONV - 1:DN_CONV, :]
        rows = []
        for j in range(DN_CONV - 1):
            s = DN_CONV - 1 - j
            dx = dx + _shift_rows(dc, -s) * w[j:j + 1, :]
            rows.append(_colsum(dc * _shift_rows(x, s)))
        rows.append(_colsum(dc * x))
        dx_ref[...] = dx
        ri = _iota2((8, LANE), 0)
        dw = jnp.zeros((8, LANE), F32)
        for j in range(DN_CONV):
            dw = dw + jnp.where(ri == j, rows[j], 0.0)
        dw_ref[...] = dw

    col = lambda j: (0, j)
    return _call(
        body, name="dn_conv_bwd", grid=(width // LANE,),
        in_specs=[pl.BlockSpec((t, LANE), col), pl.BlockSpec((8, LANE), col), pl.BlockSpec((t, LANE), col), ANY],
        out_specs=[pl.BlockSpec((t, LANE), col), pl.BlockSpec((8, LANE), col)],
        out_shape=[jax.ShapeDtypeStruct(dproj.shape, F32), jax.ShapeDtypeStruct((8, width), F32)],
        semantics=("parallel",), aliases={3: 0},
    )(proj, conv_w, dact, dproj)


def _t(x):
    return jnp.swapaxes(x, -1, -2)


def _inv_unit_lower(a):
    c = a.shape[-1]
    eye = (_iota2((c, c), 0) == _iota2((c, c), 1)).astype(F32)
    x = eye - a
    p = _hdot(a, a)
    steps = int(math.log2(c)) - 1
    for i in range(steps):
        x = x + _hdot(x, p)
        if i < steps - 1:
            p = _hdot(p, p)
    return x


def _dn_chunk(q, k, v, a, b, alog, dtb, s_in):
    nh, c, d = q.shape
    rq = lax.rsqrt(_rowsum(q * q) + NORM_EPS)
    rk = lax.rsqrt(_rowsum(k * k) + NORM_EPS)
    qh = q * rq
    kn = k * rk
    qs = qh * (d ** -0.5)
    g = -jnp.exp(alog) * _softplus(a + dtb)
    beta = _sigmoid(b)
    ri = _iota2((c, c), 0)
    ci = _iota2((c, c), 1)
    causal = ri >= ci
    strict = ri > ci
    gb = jnp.broadcast_to(g, (nh, c, d))
    gcb = _hdot(jnp.broadcast_to(causal.astype(F32), (nh, c, c)), gb)
    gc = gcb[..., :1]
    gl = _colsum(gb)[..., :1]
    dec = jnp.exp(jnp.where(causal, gc - _t(gcb)[:, :c, :], -1e30))
    kb = kn * beta
    amat = jnp.where(strict, _bdot(kb, kn, 1, 1) * dec, 0.0)
    tinv = _inv_unit_lower(amat)
    e = jnp.exp(gc)
    f = jnp.exp(gl - gc)
    rw = kb * e
    sol = _hdot(tinv, jnp.concatenate([v * beta, rw], axis=-1))
    u = sol[..., :d]
    w = sol[..., d:]
    pmat = jnp.where(causal, _bdot(qs, kn, 1, 1) * dec, 0.0)
    qd = qs * e
    kd = kn * f
    vnew = u - _bdot(w, s_in)
    o = _bdot(qd, s_in) + _bdot(pmat, vnew)
    s_out = s_in * jnp.exp(gl) + _bdot(kd, vnew, 0, 0)
    return dict(rq=rq, rk=rk, qh=qh, kn=kn, qs=qs, g=g, beta=beta, causal=causal, strict=strict, gl=gl,
                dec=dec, kb=kb, amat=amat, tinv=tinv, e=e, f=f, rw=rw, u=u, w=w, pmat=pmat, qd=qd, kd=kd,
                vnew=vnew, o=o, s_out=s_out)


def _dn_chunk_bwd(m, q, v, a, alog, dtb, s_in, do, ds_out):
    nh, c, d = q.shape
    kn, qs, kb, u, w, e, f = m["kn"], m["qs"], m["kb"], m["u"], m["w"], m["e"], m["f"]
    beta, dec, tinv, vnew, kd, qd = m["beta"], m["dec"], m["tinv"], m["vnew"], m["kd"], m["qd"]
    el = jnp.exp(m["gl"])
    dvnew = _bdot(m["pmat"], do, 0, 0) + _bdot(kd, ds_out)
    dp = jnp.where(m["causal"], _bdot(do, vnew, 1, 1), 0.0)
    dqd = _bdot(do, s_in, 1, 1)
    dkd = _bdot(vnew, ds_out, 1, 1)
    ds_in = _bdot(qd, do, 0, 0) + el * ds_out - _bdot(w, dvnew, 0, 0)
    dgl = el * _colsum(_rowsum(s_in * ds_out))
    dw = -_bdot(dvnew, s_in, 1, 1)
    dsol = _hdot(tinv, jnp.concatenate([dvnew, dw], axis=-1), 0, 0)
    dru = dsol[..., :d]
    drw = dsol[..., d:]
    da_m = -jnp.where(m["strict"], _bdot(dsol, jnp.concatenate([u, w], axis=-1), 1, 1), 0.0)
    db_m = da_m * dec
    dq_m = dp * dec
    dkb = _bdot(db_m, kn)
    dkn = _bdot(db_m, kb, 0, 0) + _bdot(dq_m, qs, 0, 0)
    dqs = _bdot(dq_m, kn)
    gmat = da_m * m["amat"] + dp * m["pmat"]
    ones = jnp.ones((nh, c, d), F32)
    dgam = (_hdot(gmat, ones) - _hdot(gmat, ones, 0, 0))[..., :1]
    dqs = dqs + dqd * e
    dgam = dgam + _rowsum(dqd * qd)
    dkn = dkn + dkd * f
    tk = _rowsum(dkd * kd)
    dgam = dgam - tk
    dgl = dgl + _colsum(tk)
    dkb = dkb + drw * e
    dgam = dgam + _rowsum(drw * m["rw"])
    dv = dru * beta
    dbeta = _rowsum(dru * v) + _rowsum(dkb * kn)
    dkn = dkn + dkb * beta
    last = (_iota2((c, 1), 0) == c - 1).astype(F32)
    dgam = dgam + last * dgl
    upper = (_iota2((c, c), 0) <= _iota2((c, c), 1)).astype(F32)
    dg = _hdot(jnp.broadcast_to(upper, (nh, c, c)), jnp.broadcast_to(dgam, (nh, c, d)))[..., :1]
    dqh = dqs * (d ** -0.5)
    dq = m["rq"] * (dqh - m["qh"] * _rowsum(dqh * m["qh"]))
    dk = m["rk"] * (dkn - kn * _rowsum(dkn * kn))
    sg = _sigmoid(a + dtb)
    da = dg * (-jnp.exp(alog)) * sg
    dalog = _colsum(dg * m["g"])
    ddtb = _colsum(da)
    db = dbeta * beta * (1.0 - beta)
    return dq, dk, dv, da, db, dalog, ddtb, ds_in


def _dn_gate(o, z, wn):
    ro = lax.rsqrt(jnp.mean(o * o, axis=-1, keepdims=True) + NORM_EPS)
    n = o * ro
    return n, ro, n * wn * _silu(z)


def _heads(ref, col0):
    d = DN_HEAD_DIM
    return jnp.stack([ref[:, col0 + h * d:col0 + (h + 1) * d] for h in range(DN_HEADS)])


def _dn_inputs(act_ref, ab_ref, sc_ref):
    ab = ab_ref[...]
    sc = sc_ref[...]
    q = _heads(act_ref, 0)
    k = _heads(act_ref, DN_WIDTH)
    v = _heads(act_ref, 2 * DN_WIDTH)
    a = jnp.stack([ab[:, h:h + 1] for h in range(DN_HEADS)])
    b = jnp.stack([ab[:, DN_HEADS + h:DN_HEADS + h + 1] for h in range(DN_HEADS)])
    alog = jnp.stack([sc[0:1, h:h + 1] for h in range(DN_HEADS)])
    dtb = jnp.stack([sc[1:2, h:h + 1] for h in range(DN_HEADS)])
    return q, k, v, a, b, alog, dtb


def dn_fwd(act, proj, scal, wn):
    t = act.shape[0]
    n = t // DN_CHUNK
    d = DN_HEAD_DIM

    def body(act_ref, z_ref, ab_ref, sc_ref, wn_ref, y_ref, st_ref, s_ref):
        @pl.when(pl.program_id(0) == 0)
        def _():
            s_ref[...] = jnp.zeros_like(s_ref)

        q, k, v, a, b, alog, dtb = _dn_inputs(act_ref, ab_ref, sc_ref)
        s_in = s_ref[...]
        st_ref[0] = s_in
        m = _dn_chunk(q, k, v, a, b, alog, dtb, s_in)
        s_ref[...] = m["s_out"]
        y = _dn_gate(m["o"], _heads(z_ref, 0), wn_ref[...])[2]
        for h in range(DN_HEADS):
            y_ref[:, h * d:(h + 1) * d] = y[h]

    return _call(
        body, name="dn_fwd", grid=(n,),
        in_specs=[pl.BlockSpec((DN_CHUNK, 3 * DN_WIDTH), lambda i: (i, 0)),
                  pl.BlockSpec((DN_CHUNK, DN_WIDTH), lambda i: (i, C_Z // DN_WIDTH)),
                  pl.BlockSpec((DN_CHUNK, LANE), lambda i: (i, C_AB // LANE)),
                  _full((8, LANE)), _full((1, d))],
        out_specs=[pl.BlockSpec((DN_CHUNK, DN_WIDTH), lambda i: (i, 0)),
                   pl.BlockSpec((1, DN_HEADS, d, d), lambda i: (i, 0, 0, 0))],
        out_shape=[jax.ShapeDtypeStruct((t, MIX_WIDTH), F32), jax.ShapeDtypeStruct((n, DN_HEADS, d, d), F32)],
        scratch_shapes=[pltpu.VMEM((DN_HEADS, d, d), F32)],
    )(act, proj, proj, scal, wn)


def dn_bwd(act, proj, scal, wn, states, dy):
    t = act.shape[0]
    n = t // DN_CHUNK
    d = DN_HEAD_DIM
    zab = DN_WIDTH + AB_PAD

    def body(act_ref, z_ref, ab_ref, sc_ref, wn_ref, st_ref, dy_ref, dact_ref, dzab_ref, dpar_ref, ds_ref):
        @pl.when(pl.program_id(0) == 0)
        def _():
            ds_ref[...] = jnp.zeros_like(ds_ref)
            dpar_ref[...] = jnp.zeros_like(dpar_ref)

        wnv = wn_ref[...]
        q, k, v, a, b, alog, dtb = _dn_inputs(act_ref, ab_ref, sc_ref)
        s_in = st_ref[0]
        z = _heads(z_ref, 0)
        dyh = _heads(dy_ref, 0)
        m = _dn_chunk(q, k, v, a, b, alog, dtb, s_in)
        nrm, ro, _ = _dn_gate(m["o"], z, wnv)
        sz = _silu(z)
        dz = dyh * nrm * wnv * _dsilu(z)
        dn = dyh * wnv * sz
        dwn = _colsum(dyh * nrm * sz)
        do = ro * (dn - nrm * jnp.mean(dn * nrm, axis=-1, keepdims=True))
        dq, dk, dv, da, db, dalog, ddtb, ds_in = _dn_chunk_bwd(m, q, v, a, alog, dtb, s_in, do, ds_ref[...])
        ds_ref[...] = ds_in
        lane = _iota2((DN_CHUNK, LANE), 1)
        prow = _iota2((8, LANE), 0)
        plane = _iota2((8, LANE), 1)
        dab = jnp.zeros((DN_CHUNK, LANE), F32)
        dpar = jnp.zeros((8, LANE), F32)
        for h in range(DN_HEADS):
            dzab_ref[:, h * d:(h + 1) * d] = dz[h]
            dact_ref[:, h * d:(h + 1) * d] = dq[h]
            dact_ref[:, DN_WIDTH + h * d:DN_WIDTH + (h + 1) * d] = dk[h]
            dact_ref[:, 2 * DN_WIDTH + h * d:2 * DN_WIDTH + (h + 1) * d] = dv[h]
            dab = dab + jnp.where(lane == h, da[h], 0.0) + jnp.where(lane == DN_HEADS + h, db[h], 0.0)
            dpar = dpar + jnp.where((prow == 0) & (plane == h), dalog[h], 0.0)
            dpar = dpar + jnp.where((prow == 1) & (plane == h), ddtb[h], 0.0)
            dpar = dpar + jnp.where(prow == 2, dwn[h], 0.0)
        dzab_ref[:, DN_WIDTH:DN_WIDTH + LANE] = dab
        dzab_ref[:, DN_WIDTH + LANE:] = jnp.zeros((DN_CHUNK, AB_PAD - LANE), F32)
        dpar_ref[...] += dpar

    rev = lambda i: (n - 1 - i, 0)
    return _call(
        body, name="dn_bwd", grid=(n,),
        in_specs=[pl.BlockSpec((DN_CHUNK, 3 * DN_WIDTH), rev),
                  pl.BlockSpec((DN_CHUNK, DN_WIDTH), lambda i: (n - 1 - i, C_Z // DN_WIDTH)),
                  pl.BlockSpec((DN_CHUNK, LANE), lambda i: (n - 1 - i, C_AB // LANE)),
                  _full((8, LANE)), _full((1, d)),
                  pl.BlockSpec((1, DN_HEADS, d, d), lambda i: (n - 1 - i, 0, 0, 0)),
                  pl.BlockSpec((DN_CHUNK, DN_WIDTH), rev)],
        out_specs=[pl.BlockSpec((DN_CHUNK, 3 * DN_WIDTH), rev),
                   pl.BlockSpec((DN_CHUNK, zab), lambda i: (n - 1 - i, C_Z // zab)), _full((8, LANE))],
        out_shape=[jax.ShapeDtypeStruct((t, 3 * DN_WIDTH), F32), jax.ShapeDtypeStruct((t, IN_PAD), F32),
                   jax.ShapeDtypeStruct((8, LANE), F32)],
        scratch_shapes=[pltpu.VMEM((DN_HEADS, d, d), F32)],
    )(act, proj, proj, scal, wn, states, dy)


_INV_SQRT2 = 0.7071067811865476
_INV_SQRT2PI = 0.3989422804014327


def _gelu(x):
    return 0.5 * x * (1.0 + lax.erf(x * _INV_SQRT2))


def _dgelu(x):
    return 0.5 * (1.0 + lax.erf(x * _INV_SQRT2)) + x * jnp.exp(-0.5 * x * x) * _INV_SQRT2PI


def _gm_core(uv, lng, lnb, ws_ref, bst):
    c = uv.shape[0]
    zz = _gelu(uv)
    u = zz[:, :GM_WIDTH]
    vv = zz[:, GM_WIDTH:]
    xc = vv - jnp.mean(vv, axis=-1, keepdims=True)
    rs = lax.rsqrt(jnp.mean(xc * xc, axis=-1, keepdims=True) + NORM_EPS)
    xh = xc * rs
    vn = xh * lng + lnb
    grp = _iota2((c, GM_WIDTH), 1) // GM_GROUP_DIM
    tril = _iota2((c, c), 0) >= _iota2((c, c), 1)
    sv = jnp.zeros((c, GM_WIDTH), F32)
    masks = []
    for g in range(GM_GROUPS):
        mk = grp == g
        masks.append(mk)
        ws = jnp.where(tril, ws_ref[g], 0.0)
        sv = sv + _bdot(ws, jnp.where(mk, vn, 0.0)) + jnp.where(mk, bst[:, g:g + 1], 0.0)
    return u, xh, rs, vn, sv, masks, tril


def gm_fwd(proj, lng, lnb, w_s, bst, ybuf):
    t = proj.shape[0]

    def body(uv_ref, g_ref, b_ref, ws_ref, bst_ref, _, y_ref):
        u, _, _, _, sv, _, _ = _gm_core(uv_ref[...], g_ref[...], b_ref[...], ws_ref, bst_ref[...])
        y_ref[...] = u * sv

    return _call(
        body, name="gm_fwd", grid=(t // GM_CHUNK,),
        in_specs=[pl.BlockSpec((GM_CHUNK, 2 * GM_WIDTH), lambda i: (i, C_UV // (2 * GM_WIDTH))),
                  _full((1, GM_WIDTH)), _full((1, GM_WIDTH)), _full((GM_GROUPS, GM_CHUNK, GM_CHUNK)),
                  _full((GM_CHUNK, LANE)), ANY],
        out_specs=pl.BlockSpec((GM_CHUNK, GM_WIDTH), lambda i: (i, DN_WIDTH // GM_WIDTH)),
        out_shape=jax.ShapeDtypeStruct(ybuf.shape, F32), semantics=("parallel",), aliases={5: 0},
    )(proj, lng, lnb, w_s, bst, ybuf)


def gm_bwd(proj, lng, lnb, w_s, bst, dy, dproj):
    t = proj.shape[0]

    def body(uv_ref, g_ref, b_ref, ws_ref, bst_ref, dy_ref, _, duv_ref, dws_ref, dbst_ref, dln_ref):
        @pl.when(pl.program_id(0) == 0)
        def _():
            dws_ref[...] = jnp.zeros_like(dws_ref)
            dbst_ref[...] = jnp.zeros_like(dbst_ref)
            dln_ref[...] = jnp.zeros_like(dln_ref)

        uv = uv_ref[...]
        lng = g_ref[...]
        u, xh, rs, vn, sv, masks, tril = _gm_core(uv, lng, b_ref[...], ws_ref, bst_ref[...])
        dyv = dy_ref[...]
        dsv = dyv * u
        lane = _iota2((GM_CHUNK, LANE), 1)
        dvn = jnp.zeros_like(dsv)
        dbst = jnp.zeros((GM_CHUNK, LANE), F32)
        for g in range(GM_GROUPS):
            ws = jnp.where(tril, ws_ref[g], 0.0)
            dsg = jnp.where(masks[g], dsv, 0.0)
            dvn = dvn + jnp.where(masks[g], _bdot(ws, dsv, 0, 0), 0.0)
            dws_ref[g] += jnp.where(tril, _bdot(dsg, vn, 1, 1), 0.0)
            dbst = dbst + jnp.where(lane == g, _rowsum(dsg), 0.0)
        dbst_ref[...] += dbst
        row = _iota2((8, GM_WIDTH), 0)
        dln_ref[...] += jnp.where(row == 0, _colsum(dvn * xh), 0.0) + jnp.where(row == 1, _colsum(dvn), 0.0)
        dxh = dvn * lng
        dvv = rs * (dxh - jnp.mean(dxh, axis=-1, keepdims=True) - xh * jnp.mean(dxh * xh, axis=-1, keepdims=True))
        dg = _dgelu(uv)
        duv_ref[:, :GM_WIDTH] = dyv * sv * dg[:, :GM_WIDTH]
        duv_ref[:, GM_WIDTH:] = dvv * dg[:, GM_WIDTH:]

    return _call(
        body, name="gm_bwd", grid=(t // GM_CHUNK,),
        in_specs=[pl.BlockSpec((GM_CHUNK, 2 * GM_WIDTH), lambda i: (i, C_UV // (2 * GM_WIDTH))),
                  _full((1, GM_WIDTH)), _full((1, GM_WIDTH)), _full((GM_GROUPS, GM_CHUNK, GM_CHUNK)),
                  _full((GM_CHUNK, LANE)),
                  pl.BlockSpec((GM_CHUNK, GM_WIDTH), lambda i: (i, DN_WIDTH // GM_WIDTH)), ANY],
        out_specs=[pl.BlockSpec((GM_CHUNK, 2 * GM_WIDTH), lambda i: (i, C_UV // (2 * GM_WIDTH))),
                   _full((GM_GROUPS, GM_CHUNK, GM_CHUNK)), _full((GM_CHUNK, LANE)), _full((8, GM_WIDTH))],
        out_shape=[jax.ShapeDtypeStruct(dproj.shape, F32),
                   jax.ShapeDtypeStruct((GM_GROUPS, GM_CHUNK, GM_CHUNK), F32),
                   jax.ShapeDtypeStruct((GM_CHUNK, LANE), F32), jax.ShapeDtypeStruct((8, GM_WIDTH), F32)],
        aliases={6: 0},
    )(proj, lng, lnb, w_s, bst, dy, dproj)


def _head_mats():
    r = _iota2((SW_WIDTH, SW_WIDTH), 0)
    c = _iota2((SW_WIDTH, SW_WIDTH), 1)
    same = (r // SW_HEAD_DIM) == (c // SW_HEAD_DIM)
    cc = c % SW_HEAD_DIM
    half = ROPE_DIM // 2
    rot = jnp.where((cc < half) & (r == c + half), -1.0, 0.0) + jnp.where((cc >= half) & (cc < ROPE_DIM) & (r == c - half), 1.0, 0.0)
    return same.astype(F32), rot


def _seg_col(s):
    return C_SW // SW_WIDTH + (s // 2) * 3 + s % 2


def _halves(x):
    return x[:, :LANE], x[:, LANE:]


def sw_prep(proj, nw2, cos_t, sin_t, *, tm=512):
    t = proj.shape[0]

    def body(x_ref, w_ref, c_ref, s_ref, o_ref):
        same, rot = _head_mats()
        x = x_ref[...]
        r = lax.rsqrt(_hdot(x * x, same) * (1.0 / SW_HEAD_DIM) + NORM_EPS)
        xn = x * r * w_ref[0]
        o_ref[0, 0], o_ref[0, 1] = _halves(xn * c_ref[...] + _hdot(xn, rot) * s_ref[...])

    return _call(
        body, name="sw_prep", grid=(6, t // tm),
        in_specs=[pl.BlockSpec((tm, SW_WIDTH), lambda s, i: (i, _seg_col(s))),
                  pl.BlockSpec((1, 1, SW_WIDTH), lambda s, i: (s % 2, 0, 0)),
                  pl.BlockSpec((tm, SW_WIDTH), lambda s, i: (i, 0)),
                  pl.BlockSpec((tm, SW_WIDTH), lambda s, i: (i, 0))],
        out_specs=pl.BlockSpec((1, 2, tm, LANE), lambda s, i: (s, 0, i, 0)),
        out_shape=jax.ShapeDtypeStruct((6, 2, t, LANE), F32), semantics=("parallel", "parallel"),
    )(proj, nw2, cos_t, sin_t)


def sw_prep_bwd(proj, nw2, cos_t, sin_t, dkvq, dproj, dnw, p, *, tm=512):
    t = proj.shape[0]
    col0 = C_SW // SW_WIDTH + 3 * p
    seg_col = lambda s: col0 + (s + 1) % 3

    def body(x_ref, w_ref, c_ref, s_ref, d_ref, _, dw0_ref, dx_ref, dw_ref):
        s = pl.program_id(0)
        dout = jnp.concatenate([d_ref[0, 0], d_ref[0, 1]], axis=1)

        @pl.when(s == 1)
        def _():
            dx_ref[...] = dout

        @pl.when((s != 1) & (pl.program_id(1) == 0))
        def _():
            dw_ref[...] = dw0_ref[...]

        @pl.when(s != 1)
        def _():
            same, rot = _head_mats()
            x = x_ref[...]
            w = w_ref[0]
            r = lax.rsqrt(_hdot(x * x, same) * (1.0 / SW_HEAD_DIM) + NORM_EPS)
            xh = x * r
            dxn = dout * c_ref[...] + _hdot(dout * s_ref[...], rot, 1, 1)
            dw_ref[0] += _colsum(dxn * xh)
            dxh = dxn * w
            dx_ref[...] = r * (dxh - xh * (_hdot(dxh * xh, same) * (1.0 / SW_HEAD_DIM)))

    return _call(
        body, name=f"sw_prep_bwd{p}", grid=(3, t // tm),
        in_specs=[pl.BlockSpec((tm, SW_WIDTH), lambda s, i: (i, seg_col(s))),
                  pl.BlockSpec((1, 1, SW_WIDTH), lambda s, i: (1 - s // 2, 0, 0)),
                  pl.BlockSpec((tm, SW_WIDTH), lambda s, i: (i, 0)),
                  pl.BlockSpec((tm, SW_WIDTH), lambda s, i: (i, 0)),
                  pl.BlockSpec((1, 2, tm, LANE), lambda s, i: (s, 0, i, 0)), ANY,
                  pl.BlockSpec((1, 1, SW_WIDTH), lambda s, i: (s // 2, 0, 0))],
        out_specs=[pl.BlockSpec((tm, SW_WIDTH), lambda s, i: (i, seg_col(s))),
                   pl.BlockSpec((1, 1, SW_WIDTH), lambda s, i: (s // 2, 0, 0))],
        out_shape=[jax.ShapeDtypeStruct(dproj.shape, F32), jax.ShapeDtypeStruct((2, 1, SW_WIDTH), F32)],
        semantics=("arbitrary", "arbitrary"), aliases={5: 0},
    )(proj, nw2, cos_t, sin_t, dkvq, dproj, dnw)


_SW_SCALE = SW_HEAD_DIM ** -0.5
_NEG = -1e30


def _sw_masks(has_other):
    ri = _iota2((SW_BLOCK, SW_BLOCK), 0)
    ci = _iota2((SW_BLOCK, SW_BLOCK), 1)
    return ri >= ci, (ci >= ri) & has_other


def _pair(x):
    first = _iota2((1, LANE), 1) < SW_HEAD_DIM
    return jnp.stack([jnp.where(first, x, 0.0), jnp.where(first, 0.0, x)])


def _both(x):
    return jnp.broadcast_to(x.astype(BF16)[None], (2,) + x.shape)


def _unpair(x2):
    first = _iota2((1, LANE), 1) < SW_HEAD_DIM
    return jnp.where(first, x2[0], x2[1])


def _head_cols(x):
    return jnp.stack([x[:, 0:1], x[:, SW_HEAD_DIM:SW_HEAD_DIM + 1]])


def _residues(dil, body):
    if dil == 1:
        body(pl.ds(0, SW_BLOCK))
    else:
        def step(r, carry):
            body(pl.ds(r, SW_BLOCK, stride=dil))
            return carry

        lax.fori_loop(0, dil, step, 0)


def _sw_geometry(t, p):
    dil = SW_DILATIONS[p]
    span = SW_BLOCK * dil
    return dil, span, t // span


def sw_attn(qk, proj, p):
    t = proj.shape[0]
    dil, span, nsp = _sw_geometry(t, p)
    vcol = (C_SW + 3 * SW_WIDTH * p + 2 * SW_WIDTH) // LANE

    def body(q_ref, kc_ref, kp_ref, vc_ref, vp_ref, o_ref, l_ref):
        mc, mp = _sw_masks(pl.program_id(1) != 0)

        def one(rows):
            q2 = _pair(q_ref.at[0, 0][rows, :])
            sc = jnp.where(mc, _bdot(q2, _both(kc_ref.at[0, 0][rows, :]), 1, 1) * _SW_SCALE, _NEG)
            sp = jnp.where(mp, _bdot(q2, _both(kp_ref.at[0, 0][rows, :]), 1, 1) * _SW_SCALE, _NEG)
            mx = jnp.maximum(jnp.max(sc, axis=-1, keepdims=True), jnp.max(sp, axis=-1, keepdims=True))
            pc = jnp.exp(sc - mx)
            pp = jnp.exp(sp - mx)
            den = _rowsum(pc) + _rowsum(pp)
            o2 = (_bdot(pc, _both(vc_ref[rows, :])) + _bdot(pp, _both(vp_ref[rows, :]))) / den
            o_ref.at[0][rows, :] = _unpair(o2)
            l_ref.at[0][rows, :] = _unpair(jnp.broadcast_to(mx + jnp.log(den), o2.shape))

        _residues(dil, one)

    prev = lambda j: jnp.maximum(j - 1, 0)
    seg = lambda s, at: pl.BlockSpec((1, 1, span, LANE), lambda h, j: (s, h, at(j), 0))
    val = lambda at: pl.BlockSpec((span, LANE), lambda h, j: (at(j), vcol + h))
    out = pl.BlockSpec((1, span, LANE), lambda h, j: (h, j, 0))
    cur = lambda j: j
    shp = jax.ShapeDtypeStruct((2, t, LANE), F32)
    return _call(
        body, name=f"sw_attn{p}", grid=(2, nsp),
        in_specs=[seg(2 * p, cur), seg(2 * p + 1, cur), seg(2 * p + 1, prev), val(cur), val(prev)],
        out_specs=[out, out], out_shape=[shp, shp], semantics=("parallel", "parallel"),
    )(qk, qk, qk, proj, proj)


def sw_attn_dkv(qk, proj, dy, lg, dm, p):
    t = proj.shape[0]
    dil, span, nsp = _sw_geometry(t, p)
    vcol = (C_SW + 3 * SW_WIDTH * p + 2 * SW_WIDTH) // LANE
    ycol = (DN_WIDTH + GM_WIDTH) // LANE

    def body(k_ref, v_ref, qc_ref, qn_ref, doc_ref, don_ref, lc_ref, ln_ref, dc_ref, dn_ref, o_ref):
        mc, mn = _sw_masks(pl.program_id(1) + 1 < nsp)

        def one(rows):
            k2 = _both(k_ref.at[0, 0][rows, :])
            v2 = _both(v_ref[rows, :])
            dk = jnp.zeros((2, SW_BLOCK, LANE), F32)
            dv = jnp.zeros((2, SW_BLOCK, LANE), F32)
            for q_ref, do_ref, l_ref, d_ref, mk in ((qc_ref, doc_ref, lc_ref, dc_ref, mc),
                                                    (qn_ref, don_ref, ln_ref, dn_ref, mn)):
                q2 = _pair(q_ref.at[0, 0][rows, :])
                do2 = _pair(do_ref[rows, :])
                pr = jnp.exp(jnp.where(mk, _bdot(q2, k2, 1, 1) * _SW_SCALE, _NEG) - _head_cols(l_ref.at[0][rows, :]))
                dv = dv + _bdot(pr, do2, 0, 0)
                ds = pr * (_bdot(do2, v2, 1, 1) - _head_cols(d_ref.at[0][rows, :]))
                dk = dk + _bdot(ds, q2, 0, 0)
            o_ref.at[0, 0][rows, :] = (dk[0] + dk[1]) * _SW_SCALE
            o_ref.at[1, 0][rows, :] = dv[0] + dv[1]

        _residues(dil, one)

    cur = lambda j: j
    nxt = lambda j: jnp.minimum(j + 1, nsp - 1)
    seg = lambda s, at: pl.BlockSpec((1, 1, span, LANE), lambda h, j: (s, h, at(j), 0))
    col = lambda c0, at: pl.BlockSpec((span, LANE), lambda h, j: (at(j), c0 + h))
    hp = lambda at: pl.BlockSpec((1, span, LANE), lambda h, j: (h, at(j), 0))
    return _call(
        body, name=f"sw_dkv{p}", grid=(2, nsp),
        in_specs=[seg(2 * p + 1, cur), col(vcol, cur), seg(2 * p, cur), seg(2 * p, nxt), col(ycol, cur),
                  col(ycol, nxt), hp(cur), hp(nxt), hp(cur), hp(nxt)],
        out_specs=pl.BlockSpec((2, 1, span, LANE), lambda h, j: (0, h, j, 0)),
        out_shape=jax.ShapeDtypeStruct((3, 2, t, LANE), F32), semantics=("parallel", "parallel"),
    )(qk, proj, qk, qk, dy, dy, lg, lg, dm, dm)


def sw_attn_dq(qk, proj, dy, lg, dm, dkvq, p):
    t = proj.shape[0]
    dil, span, nsp = _sw_geometry(t, p)
    vcol = (C_SW + 3 * SW_WIDTH * p + 2 * SW_WIDTH) // LANE
    ycol = (DN_WIDTH + GM_WIDTH) // LANE

    def body(q_ref, kc_ref, kp_ref, vc_ref, vp_ref, do_ref, l_ref, d_ref, _, dq_ref):
        mc, mp = _sw_masks(pl.program_id(1) != 0)

        def one(rows):
            q2 = _pair(q_ref.at[0, 0][rows, :])
            do2 = _pair(do_ref[rows, :])
            lse = _head_cols(l_ref.at[0][rows, :])
            dd = _head_cols(d_ref.at[0][rows, :])
            kc, kp = _both(kc_ref.at[0, 0][rows, :]), _both(kp_ref.at[0, 0][rows, :])
            pc = jnp.exp(jnp.where(mc, _bdot(q2, kc, 1, 1) * _SW_SCALE, _NEG) - lse)
            pp = jnp.exp(jnp.where(mp, _bdot(q2, kp, 1, 1) * _SW_SCALE, _NEG) - lse)
            dsc = pc * (_bdot(do2, _both(vc_ref[rows, :]), 1, 1) - dd)
            dsp = pp * (_bdot(do2, _both(vp_ref[rows, :]), 1, 1) - dd)
            dq_ref.at[0, 0][rows, :] = _unpair(_bdot(dsc, kc) + _bdot(dsp, kp)) * _SW_SCALE

        _residues(dil, one)

    cur = lambda j: j
    prev = lambda j: jnp.maximum(j - 1, 0)
    seg = lambda s, at: pl.BlockSpec((1, 1, span, LANE), lambda h, j: (s, h, at(j), 0))
    col = lambda c0, at: pl.BlockSpec((span, LANE), lambda h, j: (at(j), c0 + h))
    hp = pl.BlockSpec((1, span, LANE), lambda h, j: (h, j, 0))
    return _call(
        body, name=f"sw_dq{p}", grid=(2, nsp),
        in_specs=[seg(2 * p, cur), seg(2 * p + 1, cur), seg(2 * p + 1, prev), col(vcol, cur), col(vcol, prev),
                  col(ycol, cur), hp, hp, ANY],
        out_specs=pl.BlockSpec((1, 1, span, LANE), lambda h, j: (2, h, j, 0)),
        out_shape=jax.ShapeDtypeStruct(dkvq.shape, F32), semantics=("parallel", "parallel"), aliases={8: 0},
    )(qk, qk, qk, proj, proj, dy, lg, dm, dkvq)


def sw_merge(outs, lses, ybuf, *, tm=512):
    t = ybuf.shape[0]

    def body(o0, o1, o2, l0_ref, l1_ref, l2_ref, _, y_ref, lg_ref):
        l0, l1, l2 = l0_ref[...], l1_ref[...], l2_ref[...]
        mx = jnp.maximum(jnp.maximum(l0, l1), l2)
        lg = mx + jnp.log(jnp.exp(l0 - mx) + jnp.exp(l1 - mx) + jnp.exp(l2 - mx))
        lg_ref[...] = lg
        y = jnp.exp(l0 - lg) * o0[...] + jnp.exp(l1 - lg) * o1[...] + jnp.exp(l2 - lg) * o2[...]
        y_ref[...] = jnp.concatenate([y[0], y[1]], axis=1)

    hp = pl.BlockSpec((2, tm, LANE), lambda i: (0, i, 0))
    return _call(
        body, name="sw_merge", grid=(t // tm,), in_specs=[hp] * 6 + [ANY],
        out_specs=[pl.BlockSpec((tm, SW_WIDTH), lambda i: (i, (DN_WIDTH + GM_WIDTH) // SW_WIDTH)), hp],
        out_shape=[jax.ShapeDtypeStruct(ybuf.shape, F32), jax.ShapeDtypeStruct((2, t, LANE), F32)],
        semantics=("parallel",), aliases={6: 0},
    )(*outs, *lses, ybuf)


def sw_delta(dy, ybuf, *, tm=512):
    t = ybuf.shape[0]

    def body(dy_ref, y_ref, o_ref):
        same, _ = _head_mats()
        o_ref[0], o_ref[1] = _halves(_hdot(dy_ref[...] * y_ref[...], same))

    b1 = pl.BlockSpec((tm, SW_WIDTH), lambda i: (i, (DN_WIDTH + GM_WIDTH) // SW_WIDTH))
    return _call(body, name="sw_delta", grid=(t // tm,), in_specs=[b1, b1],
                 out_specs=pl.BlockSpec((2, tm, LANE), lambda i: (0, i, 0)),
                 out_shape=jax.ShapeDtypeStruct((2, t, LANE), F32), semantics=("parallel",))(dy, ybuf)


def _rope_tables(t):
    inv = ROPE_THETA ** (-jnp.arange(0, ROPE_DIM, 2, dtype=F32) / ROPE_DIM)
    ang = jnp.arange(t, dtype=F32)[:, None] * inv[None, :]
    pad1 = jnp.ones((t, SW_HEAD_DIM - ROPE_DIM), F32)
    pad0 = jnp.zeros((t, SW_HEAD_DIM - ROPE_DIM), F32)
    cos_h = jnp.concatenate([jnp.cos(ang), jnp.cos(ang), pad1], axis=1)
    sin_h = jnp.concatenate([jnp.sin(ang), jnp.sin(ang), pad0], axis=1)
    return jnp.tile(cos_h, (1, SW_HEADS)), jnp.tile(sin_h, (1, SW_HEADS))


def sw_forward(proj, nw2, cos_t, sin_t, ybuf):
    qk = sw_prep(proj, nw2, cos_t, sin_t)
    outs, lses = [], []
    for p in range(len(SW_DILATIONS)):
        o, lse = sw_attn(qk, proj, p)
        outs.append(o)
        lses.append(lse)
    ybuf, lg = sw_merge(outs, lses, ybuf)
    return ybuf, (qk, lg)


def sw_backward(proj, nw2, cos_t, sin_t, res, ybuf, dy, dproj):
    qk, lg = res
    dm = sw_delta(dy, ybuf)
    dnw = jnp.zeros((2, 1, SW_WIDTH), F32)
    for p in range(len(SW_DILATIONS)):
        dkvq = sw_attn_dkv(qk, proj, dy, lg, dm, p)
        dkvq = sw_attn_dq(qk, proj, dy, lg, dm, dkvq, p)
        dproj, dnw = sw_prep_bwd(proj, nw2, cos_t, sin_t, dkvq, dproj, dnw, p)
    return dproj, dnw[::-1, 0]


def _pad_rows(a, rows):
    return jnp.zeros((rows,) + a.shape[1:], a.dtype).at[:a.shape[0]].set(a)


def _consts(sp):
    d = {}
    d["mix_nw"] = sp["mix_norm_w"][:, None, :]
    d["ffn_nw"] = sp["ffn_norm_w"][:, None, :]
    d["cw8"] = jnp.pad(sp["dn_conv_w"], ((0, 0), (0, 8 - DN_CONV), (0, 0)))
    d["scal"] = jnp.pad(jnp.stack([sp["dn_a_log"], sp["dn_dt_bias"]], axis=1), ((0, 0), (0, 6), (0, LANE - DN_HEADS)))
    d["wn"] = sp["dn_out_norm_w"][:, None, :]
    d["lng"] = sp["gm_ln_g"][:, None, :]
    d["lnb"] = sp["gm_ln_b"][:, None, :]
    d["w_s"] = sp["gm_w_s"]
    d["bst"] = jnp.pad(jnp.swapaxes(sp["gm_b_s"], 1, 2), ((0, 0), (0, 0), (0, LANE - GM_GROUPS)))
    d["nw2"] = jnp.stack([jnp.tile(sp["sw_q_norm_w"], (1, SW_HEADS)),
                          jnp.tile(sp["sw_k_norm_w"], (1, SW_HEADS))], axis=1)[:, :, None, :]
    return d


def _layer_fwd(x, mod, wb, cs, tabs):
    h1, proj = norm_mm(x, cs["mix_nw"], mod[1], mod[0], wb["w_in"], swiglu=False, name="in_proj")
    act = dn_conv(proj, cs["cw8"])
    y, states = dn_fwd(act, proj, cs["scal"], cs["wn"])
    y = gm_fwd(proj, cs["lng"], cs["lnb"], cs["w_s"], cs["bst"], y)
    y, swres = sw_forward(proj, cs["nw2"], *tabs, y)
    x1, o1 = resid_mm(y, wb["w_out"], x, mod[2], name="out_proj")
    h2, gu, actf = norm_mm(x1, cs["ffn_nw"], mod[4], mod[3], wb["w_ffn_in"], swiglu=True, name="ffn_in")
    x2, o2 = resid_mm(actf, wb["w_ffn_out"], x1, mod[5], name="ffn_out")
    res = dict(x=x, h1=h1, proj=proj, act=act, states=states, swres=swres, y=y, x1=x1, o1=o1, h2=h2, gu=gu,
               actf=actf, o2=o2)
    return x2, res


def _layer_bwd(dx2, res, mod, wb, cs, tabs):
    dgu, gx2, dgate2 = resid_mm_bwd(dx2, mod[5], res["o2"], wb["w_ffn_out"], res["gu"], name="ffn_out_bwd")
    g_wfo = mm_tn(res["actf"], gx2, name="wg_ffn_out")
    g_wfi = mm_tn(res["h2"], dgu, name="wg_ffn_in")
    dx1, d_ffn_nw, dscale2, dshift2 = norm_mm_bwd(dgu, wb["w_ffn_in"], res["x1"], cs["ffn_nw"], mod[4], dx2,
                                                  name="ffn_in_bwd")
    dy, gx1, dgate1 = resid_mm_bwd(dx1, mod[2], res["o1"], wb["w_out"], None, name="out_proj_bwd")
    g_wout = mm_tn(res["y"], gx1, name="wg_out")
    proj = res["proj"]
    dact, dproj, dpar = dn_bwd(res["act"], proj, cs["scal"], cs["wn"], res["states"], dy)
    dproj, dcw = dn_conv_bwd(proj, cs["cw8"], dact, dproj)
    dproj, dws, dbst, dln = gm_bwd(proj, cs["lng"], cs["lnb"], cs["w_s"], cs["bst"], dy, dproj)
    dproj, dnw = sw_backward(proj, cs["nw2"], *tabs, res["swres"], res["y"], dy, dproj)
    g_win = mm_tn(res["h1"], dproj, name="wg_in")
    dx, d_mix_nw, dscale1, dshift1 = norm_mm_bwd(dproj, wb["w_in"], res["x"], cs["mix_nw"], mod[1], dx1,
                                                 name="in_proj_bwd")
    dmod = jnp.concatenate([dshift1, dscale1, dgate1, dshift2, dscale2, dgate2], axis=1)
    dnw = dnw.reshape(2, SW_HEADS, SW_HEAD_DIM).sum(1)
    small = dict(mix_norm_w=d_mix_nw[0], ffn_norm_w=d_ffn_nw[0], dn_conv_w=dcw[:DN_CONV],
                 dn_a_log=dpar[0, :DN_HEADS], dn_dt_bias=dpar[1, :DN_HEADS], dn_out_norm_w=dpar[2],
                 gm_ln_g=dln[0], gm_ln_b=dln[1], gm_w_s=dws, gm_b_s=dbst[:, :GM_GROUPS].T,
                 sw_q_norm_w=dnw[0], sw_k_norm_w=dnw[1])
    big = dict(w_in=g_win, w_out=g_wout, w_ffn_in=g_wfi, w_ffn_out=g_wfo)
    return dx, big, small, dmod


def _permute_w_in(w):
    pad = jnp.zeros(w.shape[:-1] + (AB_PAD - 8,), w.dtype)
    return jnp.concatenate([w[..., 0:2056], pad, w[..., 2568:IN_WIDTH], w[..., 2056:2568]], axis=-1)


def _unpermute_w_in(g):
    return jnp.concatenate([g[..., 0:2056], g[..., C_UV:IN_PAD], g[..., C_SW:C_UV]], axis=-1)


def _local_step(x, target, mods, wfull, sp):
    layers = mods.shape[0]
    t, d = x.shape
    tabs = _rope_tables(t)
    consts = _consts(sp)
    saved = []
    for layer in range(layers):
        mod = mods[layer].reshape(6, 1, d)
        wb = {k: v[layer] for k, v in wfull.items()}
        cs = {k: v[layer] for k, v in consts.items()}
        x, res = _layer_fwd(x, mod, wb, cs, tabs)
        saved.append((res, mod, wb, cs))
    dx, loss = loss_head(x, target)
    bigs, smalls, dmods = [], [], []
    for layer in reversed(range(layers)):
        res, mod, wb, cs = saved[layer]
        dx, big, small, dmod = _layer_bwd(dx, res, mod, wb, cs, tabs)
        bigs.append(big)
        smalls.append(small)
        dmods.append(dmod[0])
    bigs, smalls, dmods = bigs[::-1], smalls[::-1], dmods[::-1]
    big = {k: [b[k] for b in bigs] for k in bigs[0]}
    small = {k: jnp.stack([s[k] for s in smalls]) for k in smalls[0]}
    return loss, dx, big, small, jnp.stack(dmods)


def mod_fwd(c_all, w_mod, b_shard):
    layers, d, n = w_mod.shape

    def body(c_ref, w_ref, b_ref, o_ref):
        ca = _silu(c_ref[...]).astype(BF16)
        o_ref[0] = _dot(ca, w_ref[0].astype(BF16), 1, 0) + b_ref[0]

    return _call(
        body, name="mod_fwd", grid=(layers,),
        in_specs=[_full((8, d)), pl.BlockSpec((1, d, n), lambda i: (i, 0, 0)),
                  pl.BlockSpec((1, 1, n), lambda i: (i, 0, 0))],
        out_specs=pl.BlockSpec((1, 8, n), lambda i: (i, 0, 0)),
        out_shape=jax.ShapeDtypeStruct((layers, 8, n), F32), semantics=("parallel",),
    )(c_all, w_mod, b_shard)


def mod_bwd(c_all, dmod):
    layers, _, n = dmod.shape
    d = c_all.shape[1]

    def body(c_ref, g_ref, o_ref):
        ca = _silu(c_ref[...]).astype(BF16)
        o_ref[0] = _dot(ca, g_ref[0].astype(BF16), 0, 0)

    return _call(
        body, name="mod_bwd", grid=(layers,),
        in_specs=[_full((8, d)), pl.BlockSpec((1, 8, n), lambda i: (i, 0, 0))],
        out_specs=pl.BlockSpec((1, d, n), lambda i: (i, 0, 0)),
        out_shape=jax.ShapeDtypeStruct((layers, d, n), F32), semantics=("parallel",),
    )(c_all, dmod)


N_DEV = 8


def _place():
    return lax.axis_index("x"), lax.axis_index("y"), lax.axis_index("c")


def _other_chips(x, y):
    return [(1 - x, y), (x, 1 - y), (1 - x, 1 - y)]


def allgather8(x_shard, *, name):
    m_per, n = x_shard.shape

    def body(x_ref, out_ref, send_sems, recv_sems, local_sem):
        x, y, c = _place()
        me, sibling = (x, y, c), (x, y, 1 - c)
        chips = _other_chips(x, y)

        def rows(px, py, pc):
            return out_ref.at[pl.ds((4 * px + 2 * py + pc) * m_per, m_per), :]

        def copy(k, block, to, src=None):
            return pltpu.make_async_remote_copy(
                src_ref=rows(*block) if src is None else src, dst_ref=rows(*block),
                send_sem=send_sems.at[k], recv_sem=recv_sems.at[k], device_id=to, device_id_type=MESH)

        mine = pltpu.make_async_copy(x_ref, rows(*me), local_sem)
        mine.start()
        first = [copy(0, me, sibling, src=x_ref)]
        first += [copy(1 + j, me, (*chip, c), src=x_ref) for j, chip in enumerate(chips)]
        for cp in first:
            cp.start()
        passed = [copy(4 + j, (*chip, c), sibling) for j, chip in enumerate(chips)]
        for j, chip in enumerate(chips):
            copy(1 + j, (*chip, c), me).wait_recv()
            passed[j].start()
        copy(0, sibling, me).wait_recv()
        for j, chip in enumerate(chips):
            copy(4 + j, (*chip, 1 - c), me).wait_recv()
        for cp in first + passed:
            cp.wait_send()
        mine.wait()

    return pl.pallas_call(
        body, name=name, out_shape=jax.ShapeDtypeStruct((N_DEV * m_per, n), x_shard.dtype),
        in_specs=[pl.BlockSpec(memory_space=pltpu.VMEM)], out_specs=pl.BlockSpec(memory_space=pltpu.VMEM),
        scratch_shapes=[pltpu.SemaphoreType.DMA((7,)), pltpu.SemaphoreType.DMA((7,)), pltpu.SemaphoreType.DMA],
    )(x_shard)


def gather_shards(shards):
    n = len(shards)

    def body(*refs):
        ins, outs = refs[:n], refs[n:2 * n]
        send_sems, recv_sems, local_sems = refs[2 * n:]
        x, y, c = _place()
        chips = _other_chips(x, y)
        me_s = 2 * x + y
        local = []
        for a in range(n):
            cp = pltpu.make_async_copy(ins[a], outs[a].at[me_s], local_sems.at[a])
            cp.start()
            local.append(cp)
        sends = []
        for a in range(n):
            for j, chip in enumerate(chips):
                k = 3 * a + j
                cp = pltpu.make_async_remote_copy(
                    src_ref=ins[a], dst_ref=outs[a].at[me_s], send_sem=send_sems.at[k], recv_sem=recv_sems.at[k],
                    device_id=(*chip, c), device_id_type=MESH)
                cp.start()
                sends.append(cp)
        for a in range(n):
            for j, chip in enumerate(chips):
                k = 3 * a + j
                pltpu.make_async_remote_copy(
                    src_ref=ins[a], dst_ref=outs[a].at[2 * chip[0] + chip[1]], send_sem=send_sems.at[k],
                    recv_sem=recv_sems.at[k], device_id=(*chip, c), device_id_type=MESH).wait_recv()
        for cp in sends:
            cp.wait_send()
        for cp in local:
            cp.wait()

    return pl.pallas_call(
        body, name="gather_shards",
        out_shape=[jax.ShapeDtypeStruct((4,) + s.shape, s.dtype) for s in shards],
        in_specs=[ANY] * n, out_specs=[ANY] * n,
        scratch_shapes=[pltpu.SemaphoreType.DMA((3 * n,)), pltpu.SemaphoreType.DMA((3 * n,)),
                        pltpu.SemaphoreType.DMA((n,))],
    )(*shards)


def scatter_slices(grads):
    n = len(grads)

    def body(*refs):
        ins, outs = refs[:n], refs[n:2 * n]
        send_sems, recv_sems, local_sems = refs[2 * n:]
        x, y, c = _place()
        chips = _other_chips(x, y)
        me_s = 2 * x + y
        local = []
        for a in range(n):
            cp = pltpu.make_async_copy(ins[a].at[me_s], outs[a].at[me_s], local_sems.at[a])
            cp.start()
            local.append(cp)
        sends = []
        for a in range(n):
            for j, chip in enumerate(chips):
                k = 3 * a + j
                cp = pltpu.make_async_remote_copy(
                    src_ref=ins[a].at[2 * chip[0] + chip[1]], dst_ref=outs[a].at[me_s], send_sem=send_sems.at[k],
                    recv_sem=recv_sems.at[k], device_id=(*chip, c), device_id_type=MESH)
                cp.start()
                sends.append(cp)
        for a in range(n):
            for j, chip in enumerate(chips):
                k = 3 * a + j
                pltpu.make_async_remote_copy(
                    src_ref=ins[a].at[me_s], dst_ref=outs[a].at[2 * chip[0] + chip[1]], send_sem=send_sems.at[k],
                    recv_sem=recv_sems.at[k], device_id=(*chip, c), device_id_type=MESH).wait_recv()
        for cp in sends:
            cp.wait_send()
        for cp in local:
            cp.wait()

    return pl.pallas_call(
        body, name="scatter_slices",
        out_shape=[jax.ShapeDtypeStruct(g.shape, g.dtype) for g in grads],
        in_specs=[ANY] * n, out_specs=[ANY] * n,
        scratch_shapes=[pltpu.SemaphoreType.DMA((3 * n,)), pltpu.SemaphoreType.DMA((3 * n,)),
                        pltpu.SemaphoreType.DMA((n,))],
    )(*grads)


def sibling_swap(parts):
    n = len(parts)

    def body(*refs):
        ins, outs = refs[:n], refs[n:2 * n]
        send_sems, recv_sems = refs[2 * n:]
        x, y, c = _place()
        cps = []
        for a in range(n):
            cp = pltpu.make_async_remote_copy(
                src_ref=ins[a], dst_ref=outs[a], send_sem=send_sems.at[a], recv_sem=recv_sems.at[a],
                device_id=(x, y, 1 - c), device_id_type=MESH)
            cp.start()
            cps.append(cp)
        for cp in cps:
            cp.wait()

    return pl.pallas_call(
        body, name="sibling_swap", out_shape=[jax.ShapeDtypeStruct(p.shape, p.dtype) for p in parts],
        in_specs=[ANY] * n, out_specs=[ANY] * n,
        scratch_shapes=[pltpu.SemaphoreType.DMA((n,)), pltpu.SemaphoreType.DMA((n,))],
    )(*parts)


def _row_block(rows, cols, budget=1 << 20):
    best = rows if rows % 8 else 8
    for tr in range(8, rows + 1, 8):
        if rows % tr == 0 and tr * cols * 4 <= budget:
            best = tr
    return best


def chip_sum(own, recv, me_s, *, name):
    r, n = own.shape
    tr = _row_block(r, n)

    def body(me_ref, own_ref, recv_ref, o_ref):
        me = me_ref[0]
        acc = jnp.zeros((tr, n), F32)
        for s in range(4):
            acc = acc + jnp.where(me == s, own_ref[...], recv_ref[s].astype(F32))
        o_ref[...] = acc

    return pl.pallas_call(
        body, name=name, out_shape=jax.ShapeDtypeStruct((r, n), F32),
        grid_spec=pltpu.PrefetchScalarGridSpec(
            num_scalar_prefetch=1, grid=(r // tr,),
            in_specs=[pl.BlockSpec((tr, n), lambda i, me: (i, 0)), pl.BlockSpec((4, tr, n), lambda i, me: (0, i, 0))],
            out_specs=pl.BlockSpec((tr, n), lambda i, me: (i, 0))),
        compiler_params=pltpu.CompilerParams(dimension_semantics=("parallel",)),
    )(me_s, own, recv)


def _adam_update(w, g, m, v):
    m2 = ADAM_B1 * m + (1.0 - ADAM_B1) * g
    v2 = ADAM_B2 * v + (1.0 - ADAM_B2) * (g * g)
    m_hat = m2 / (1.0 - ADAM_B1 ** ADAM_STEP)
    v_hat = v2 / (1.0 - ADAM_B2 ** ADAM_STEP)
    delta = -ADAM_LR * (m_hat / (jnp.sqrt(v_hat) + ADAM_EPS) + ADAM_WD * w)
    return delta, m2, v2


def adamw(w, g_parts, m, v, *, name):
    r, n = w.shape
    tr = _row_block(r, n)
    k = len(g_parts)

    def body(*refs):
        w_ref, m_ref, v_ref = refs[k], refs[k + 1], refs[k + 2]
        g_ref, d_ref, m2_ref, v2_ref = refs[k + 3:]
        g = refs[0][...]
        for p in refs[1:k]:
            g = g + p[...]
        g_ref[...] = g
        d_ref[...], m2_ref[...], v2_ref[...] = _adam_update(w_ref[...], g, m_ref[...], v_ref[...])

    blk = pl.BlockSpec((tr, n), lambda i: (i, 0))
    shp = jax.ShapeDtypeStruct((r, n), F32)
    return _call(body, name=name, grid=(r // tr,), in_specs=[blk] * (k + 3), out_specs=[blk] * 4,
                 out_shape=[shp] * 4, semantics=("parallel",))(*g_parts, w, m, v)


def adamw_gathered(g_all, w, m, v, *, name):
    _, r, n = g_all.shape
    tr = _row_block(r, n * 4)

    def body(ga_ref, w_ref, m_ref, v_ref, g_ref, d_ref, m2_ref, v2_ref):
        g = ga_ref[0]
        for dev in range(1, N_DEV):
            g = g + ga_ref[dev]
        g_ref[...] = g
        d_ref[...], m2_ref[...], v2_ref[...] = _adam_update(w_ref[...], g, m_ref[...], v_ref[...])

    blk = pl.BlockSpec((tr, n), lambda i: (i, 0))
    shp = jax.ShapeDtypeStruct((r, n), F32)
    return _call(body, name=name, grid=(r // tr,),
                 in_specs=[pl.BlockSpec((N_DEV, tr, n), lambda i: (0, i, 0)), blk, blk, blk], out_specs=[blk] * 4,
                 out_shape=[shp] * 4, semantics=("parallel",))(g_all, w, m, v)


BIG = ("w_in", "w_out", "w_ffn_in", "w_ffn_out")
SMALL = ("b_mod", "mix_norm_w", "ffn_norm_w", "dn_conv_w", "dn_a_log", "dn_dt_bias", "dn_out_norm_w", "gm_ln_g",
         "gm_ln_b", "gm_w_s", "gm_b_s", "sw_q_norm_w", "sw_k_norm_w")
WEIGHTS = ("w_mod", "b_mod", "mix_norm_w", "ffn_norm_w", "w_in", "w_out", "dn_conv_w", "dn_a_log", "dn_dt_bias",
           "dn_out_norm_w", "gm_ln_g", "gm_ln_b", "gm_w_s", "gm_b_s", "sw_q_norm_w", "sw_k_norm_w", "w_ffn_in",
           "w_ffn_out")
PACK_ROWS = 8


def _pack(arrs):
    out = []
    for a in arrs:
        flat = a.reshape(-1).astype(F32)
        rows = -(-flat.shape[0] // (LANE * PACK_ROWS)) * PACK_ROWS
        out.append(jnp.pad(flat, (0, rows * LANE - flat.shape[0])).reshape(rows, LANE))
    return jnp.concatenate(out, axis=0)


def _unpack(packed, shapes):
    out, r0 = [], 0
    for shp in shapes:
        size = math.prod(shp)
        rows = -(-size // (LANE * PACK_ROWS)) * PACK_ROWS
        out.append(packed[r0:r0 + rows].reshape(-1)[:size].reshape(shp))
        r0 += rows
    return out


def _shard_major(per_layer, axis, unpermute=False):
    g = jnp.stack(per_layer)
    if unpermute:
        g = _unpermute_w_in(g)
    return jnp.stack(jnp.split(g, 4, axis=axis + 1))


def kernel(x, c, w_mod, b_mod, mix_norm_w, ffn_norm_w, w_in, w_out, dn_conv_w, dn_a_log, dn_dt_bias, dn_out_norm_w, gm_ln_g, gm_ln_b, gm_w_s, gm_b_s, sw_q_norm_w, sw_k_norm_w, w_ffn_in, w_ffn_out, loss_target, m_w_mod, m_b_mod, m_mix_norm_w, m_ffn_norm_w, m_w_in, m_w_out, m_dn_conv_w, m_dn_a_log, m_dn_dt_bias, m_dn_out_norm_w, m_gm_ln_g, m_gm_ln_b, m_gm_w_s, m_gm_b_s, m_sw_q_norm_w, m_sw_k_norm_w, m_w_ffn_in, m_w_ffn_out, v_w_mod, v_b_mod, v_mix_norm_w, v_ffn_norm_w, v_w_in, v_w_out, v_dn_conv_w, v_dn_a_log, v_dn_dt_bias, v_dn_out_norm_w, v_gm_ln_g, v_gm_ln_b, v_gm_w_s, v_gm_b_s, v_sw_q_norm_w, v_sw_k_norm_w, v_w_ffn_in, v_w_ffn_out):
    w = dict(w_mod=w_mod, b_mod=b_mod, mix_norm_w=mix_norm_w, ffn_norm_w=ffn_norm_w, w_in=w_in, w_out=w_out,
             dn_conv_w=dn_conv_w, dn_a_log=dn_a_log, dn_dt_bias=dn_dt_bias, dn_out_norm_w=dn_out_norm_w,
             gm_ln_g=gm_ln_g, gm_ln_b=gm_ln_b, gm_w_s=gm_w_s, gm_b_s=gm_b_s, sw_q_norm_w=sw_q_norm_w,
             sw_k_norm_w=sw_k_norm_w, w_ffn_in=w_ffn_in, w_ffn_out=w_ffn_out)
    m = dict(w_mod=m_w_mod, b_mod=m_b_mod, mix_norm_w=m_mix_norm_w, ffn_norm_w=m_ffn_norm_w, w_in=m_w_in,
             w_out=m_w_out, dn_conv_w=m_dn_conv_w, dn_a_log=m_dn_a_log, dn_dt_bias=m_dn_dt_bias,
             dn_out_norm_w=m_dn_out_norm_w, gm_ln_g=m_gm_ln_g, gm_ln_b=m_gm_ln_b, gm_w_s=m_gm_w_s, gm_b_s=m_gm_b_s,
             sw_q_norm_w=m_sw_q_norm_w, sw_k_norm_w=m_sw_k_norm_w, w_ffn_in=m_w_ffn_in, w_ffn_out=m_w_ffn_out)
    v = dict(w_mod=v_w_mod, b_mod=v_b_mod, mix_norm_w=v_mix_norm_w, ffn_norm_w=v_ffn_norm_w, w_in=v_w_in,
             w_out=v_w_out, dn_conv_w=v_dn_conv_w, dn_a_log=v_dn_a_log, dn_dt_bias=v_dn_dt_bias,
             dn_out_norm_w=v_dn_out_norm_w, gm_ln_g=v_gm_ln_g, gm_ln_b=v_gm_ln_b, gm_w_s=v_gm_w_s, gm_b_s=v_gm_b_s,
             sw_q_norm_w=v_sw_q_norm_w, sw_k_norm_w=v_sw_k_norm_w, w_ffn_in=v_w_ffn_in, w_ffn_out=v_w_ffn_out)
    layers, d, mod_n = w_mod.shape
    mx, my, mc = _place()
    me_s = 2 * mx + my
    me_dev = 4 * mx + 2 * my + mc

    c_all = allgather8(_pad_rows(c, 8), name="gather_c").reshape(N_DEV, 8, d)[:, 0]
    b_shard = lax.dynamic_slice_in_dim(b_mod, me_s * mod_n, mod_n, axis=1)[:, None, :]
    mod_part = mod_fwd(c_all, w_mod, b_shard)
    mod_parts = allgather8(mod_part.reshape(layers * 8, mod_n), name="gather_mod")
    mod_parts = mod_parts.reshape(4, 2, layers, 8, mod_n)[:, 0]
    mod_all = mod_parts.transpose(1, 2, 0, 3).reshape(layers, 8, 4 * mod_n)
    mods = lax.dynamic_index_in_dim(mod_all, me_dev, axis=1, keepdims=False)

    gathered = gather_shards([w[k].astype(BF16) for k in BIG])
    g_in, g_out, g_fi, g_fo = gathered
    wfull = dict(
        w_in=_permute_w_in(jnp.concatenate([g_in[s] for s in range(4)], axis=-1)),
        w_out=jnp.concatenate([g_out[s] for s in range(4)], axis=1),
        w_ffn_in=jnp.concatenate([g_fi[s] for s in range(4)], axis=-1),
        w_ffn_out=jnp.concatenate([g_fo[s] for s in range(4)], axis=1))

    cw = dn_conv_w.shape[-1]
    conv_rows = -(-layers * DN_CONV // 8) * 8
    conv_parts = allgather8(_pad_rows(dn_conv_w.reshape(layers * DN_CONV, cw), conv_rows), name="gather_conv")
    conv_parts = conv_parts.reshape(4, 2, conv_rows, cw)[:, 0, :layers * DN_CONV]
    conv_full = conv_parts.reshape(4, layers, DN_CONV, cw).transpose(1, 2, 0, 3).reshape(layers, DN_CONV, 4 * cw)

    sp = {k: w[k] for k in SMALL}
    sp["dn_conv_w"] = conv_full
    loss_blk, grad_x, big, small, dmods = _local_step(x[0], loss_target[0], mods, wfull, sp)
    loss = lax.psum(loss_blk[0, 0], ("x", "y", "c"))

    axes = dict(w_in=2, w_out=1, w_ffn_in=2, w_ffn_out=1)
    stacked = {k: _shard_major(big[k], axes[k] - 1, unpermute=(k == "w_in")) for k in BIG}
    recv = scatter_slices([stacked[k].astype(BF16) for k in BIG])
    me_arr = jnp.reshape(me_s, (1,)).astype(jnp.int32)
    partial = []
    for k, r in zip(BIG, recv):
        own = lax.dynamic_index_in_dim(stacked[k], me_s, axis=0, keepdims=False)
        n_last = own.shape[-1]
        partial.append(chip_sum(own.reshape(-1, n_last), r.reshape(4, -1, n_last), me_arr, name="chip_sum_" + k))
    theirs = sibling_swap(partial)
    outs = {}
    for k, mine, other in zip(BIG, partial, theirs):
        shp = w[k].shape
        flat = lambda a: a.reshape(-1, shp[-1])
        res = adamw(flat(w[k]), [mine, other], flat(m[k]), flat(v[k]), name="adamw_" + k)
        outs[k] = [a.reshape(shp) for a in res]

    small = dict(small, b_mod=dmods)
    packed = _pack([small[k] for k in SMALL])
    rows = packed.shape[0]
    g_all = allgather8(packed, name="gather_small").reshape(N_DEV, rows, LANE)
    conv_zero = jnp.zeros((layers, DN_CONV, 3 * DN_WIDTH), F32)
    pk = lambda src: _pack([conv_zero if k == "dn_conv_w" else src[k] for k in SMALL])
    res = adamw_gathered(g_all, pk(w), pk(m), pk(v), name="adamw_small")
    shapes = [small[k].shape for k in SMALL]
    un = [_unpack(a, shapes) for a in res]
    for i, k in enumerate(SMALL):
        outs[k] = [un[j][i] for j in range(4)]
    g_conv = lax.dynamic_slice_in_dim(outs["dn_conv_w"][0], me_s * cw, cw, axis=2)
    flat = lambda a: a.reshape(-1, cw)
    res = adamw(flat(dn_conv_w), [flat(g_conv)], flat(m["dn_conv_w"]), flat(v["dn_conv_w"]), name="adamw_conv")
    outs["dn_conv_w"] = [a.reshape(dn_conv_w.shape) for a in res]

    b_rows = layers * 6 * d // LANE
    dmod_all = g_all[:, :b_rows].reshape(N_DEV, layers, 6 * d).transpose(1, 0, 2)
    dmod_shard = lax.dynamic_slice_in_dim(dmod_all, me_s * mod_n, mod_n, axis=2)
    g_wmod = mod_bwd(c_all, dmod_shard)
    flat = lambda a: a.reshape(-1, mod_n)
    res = adamw(flat(w_mod), [flat(g_wmod)], flat(m_w_mod), flat(v_w_mod), name="adamw_w_mod")
    outs["w_mod"] = [a.reshape(w_mod.shape) for a in res]

    result = [loss, grad_x[None]]
    for j in range(4):
        result += [outs[k][j] for k in WEIGHTS]
    return tuple(result)
```

```python
import functools
import math

import jax
import jax.numpy as jnp
from jax import lax
from jax.experimental import pallas as pl
from jax.experimental.pallas import tpu as pltpu

F32 = jnp.float32
BF16 = jnp.bfloat16
HI = lax.Precision.HIGH

NORM_EPS = 1e-6
DN_HEADS = 4
DN_HEAD_DIM = 128
DN_WIDTH = 512
DN_CHUNK = 64
DN_CONV = 4
GM_WIDTH = 256
GM_GROUPS = 4
GM_GROUP_DIM = 64
GM_CHUNK = 128
SW_HEADS = 4
SW_HEAD_DIM = 64
SW_WIDTH = 256
SW_DILATIONS = (1, 4, 16)
SW_BLOCK = 128
ROPE_THETA = 500000.0
ROPE_DIM = 16
LANE = 128

C_QKV = 0
C_Z = 1536
C_AB = 2048
C_SW = 2304
C_UV = 4608
IN_WIDTH = 4872
IN_PAD = 5120
AB_PAD = C_SW - C_AB
MIX_WIDTH = 1024

ADAM_LR = 0.001
ADAM_B1 = 0.9
ADAM_B2 = 0.999
ADAM_EPS = 1e-08
ADAM_WD = 0.01
ADAM_STEP = 10

MESH = pl.DeviceIdType.MESH


def _call(body, *, name, grid, in_specs, out_specs, out_shape, scratch_shapes=(), semantics=None, aliases=None):
    if semantics is None:
        semantics = ("arbitrary",) * len(grid)
    return pl.pallas_call(
        body, name=name, grid=grid, in_specs=in_specs, out_specs=out_specs, out_shape=out_shape,
        scratch_shapes=list(scratch_shapes), input_output_aliases=aliases or {},
        compiler_params=pltpu.CompilerParams(dimension_semantics=semantics),
    )


def _dot(a, b, ca, cb, prec=None):
    if a.ndim == 3:
        dims = (((ca + 1,), (cb + 1,)), ((0,), (0,)))
    else:
        dims = (((ca,), (cb,)), ((), ()))
    return lax.dot_general(a, b, dims, preferred_element_type=F32, precision=prec)


def _bdot(a, b, ca=1, cb=0):
    return _dot(a.astype(BF16), b.astype(BF16), ca, cb)


def _hdot(a, b, ca=1, cb=0):
    return _dot(a.astype(F32), b.astype(F32), ca, cb, HI)


def _sigmoid(x):
    return 1.0 / (1.0 + jnp.exp(-x))


def _silu(x):
    return x * _sigmoid(x)


def _dsilu(x):
    s = _sigmoid(x)
    return s * (1.0 + x * (1.0 - s))


def _softplus(x):
    return jnp.maximum(x, 0.0) + jnp.log(1.0 + jnp.exp(-jnp.abs(x)))


def _iota2(shape, dim):
    return lax.broadcasted_iota(jnp.int32, shape, dim)


def _rowsum(x):
    return jnp.sum(x, axis=-1, keepdims=True)


def _colsum(x):
    return jnp.sum(x, axis=-2, keepdims=True)


def _full(shape):
    return pl.BlockSpec(shape, lambda *_: (0,) * len(shape))


ANY = pl.BlockSpec(memory_space=pl.ANY)


def _norm_mod(x, nw, scale, shift):
    r = lax.rsqrt(jnp.mean(x * x, axis=-1, keepdims=True) + NORM_EPS)
    xn = x * r
    return xn, r, (xn * nw) * (1.0 + scale) + shift


def norm_mm(x, nw, scale, shift, w, *, swiglu, name, tm=256):
    t, d = x.shape
    n = w.shape[1]
    half = n // 2

    def body(x_ref, nw_ref, sc_ref, sh_ref, w_ref, h_ref, y_ref, *act_ref):
        _, _, h = _norm_mod(x_ref[...], nw_ref[...], sc_ref[...], sh_ref[...])
        hb = h.astype(BF16)
        h_ref[...] = hb
        y = _dot(hb, w_ref[...], 1, 0)
        y_ref[...] = y
        if swiglu:
            act_ref[0][...] = (_silu(y[:, :half]) * y[:, half:]).astype(BF16)

    row = lambda i: (i, 0)
    out_shape = [jax.ShapeDtypeStruct((t, d), BF16), jax.ShapeDtypeStruct((t, n), F32)]
    out_specs = [pl.BlockSpec((tm, d), row), pl.BlockSpec((tm, n), row)]
    if swiglu:
        out_shape.append(jax.ShapeDtypeStruct((t, half), BF16))
        out_specs.append(pl.BlockSpec((tm, half), row))
    return _call(
        body, name=name, grid=(t // tm,),
        in_specs=[pl.BlockSpec((tm, d), row), _full((1, d)), _full((1, d)), _full((1, d)), _full((d, n))],
        out_specs=out_specs, out_shape=out_shape, semantics=("parallel",),
    )(x, nw, scale, shift, w)


def resid_mm(y, w, x, gate, *, name, tm=256):
    t, k = y.shape
    d = w.shape[1]

    def body(y_ref, w_ref, x_ref, g_ref, xo_ref, o_ref):
        o = _dot(y_ref[...].astype(BF16), w_ref[...], 1, 0)
        o_ref[...] = o
        xo_ref[...] = x_ref[...] + g_ref[...] * o

    row = lambda i: (i, 0)
    return _call(
        body, name=name, grid=(t // tm,),
        in_specs=[pl.BlockSpec((tm, k), row), _full((k, d)), pl.BlockSpec((tm, d), row), _full((1, d))],
        out_specs=[pl.BlockSpec((tm, d), row), pl.BlockSpec((tm, d), row)],
        out_shape=[jax.ShapeDtypeStruct((t, d), F32), jax.ShapeDtypeStruct((t, d), F32)],
        semantics=("parallel",),
    )(y, w, x, gate)


def resid_mm_bwd(dx, gate, o, w, gu, *, name, tm=256):
    t, d = dx.shape
    k = w.shape[0]
    swiglu = gu is not None

    def body(dx_ref, g_ref, o_ref, w_ref, *rest):
        if swiglu:
            gu_ref, dy_ref, gx_ref, dg_ref = rest
        else:
            dy_ref, gx_ref, dg_ref = rest
        i = pl.program_id(0)
        dxv = dx_ref[...]
        gx = (dxv * g_ref[...]).astype(BF16)
        gx_ref[...] = gx
        part = _colsum(dxv * o_ref[...])

        @pl.when(i == 0)
        def _():
            dg_ref[...] = jnp.zeros_like(dg_ref)

        dg_ref[...] += part
        da = _dot(gx, w_ref[...], 1, 1)
        if swiglu:
            g = gu_ref[:, :k]
            u = gu_ref[:, k:]
            dy_ref[:, :k] = (da * u * _dsilu(g)).astype(BF16)
            dy_ref[:, k:] = (da * _silu(g)).astype(BF16)
        else:
            dy_ref[...] = da

    row = lambda i: (i, 0)
    in_specs = [pl.BlockSpec((tm, d), row), _full((1, d)), pl.BlockSpec((tm, d), row), _full((k, d))]
    args = [dx, gate, o, w]
    if swiglu:
        in_specs.append(pl.BlockSpec((tm, 2 * k), row))
        args.append(gu)
        dy_shape = jax.ShapeDtypeStruct((t, 2 * k), BF16)
        dy_spec = pl.BlockSpec((tm, 2 * k), row)
    else:
        dy_shape = jax.ShapeDtypeStruct((t, k), F32)
        dy_spec = pl.BlockSpec((tm, k), row)
    return _call(
        body, name=name, grid=(t // tm,), in_specs=in_specs,
        out_specs=[dy_spec, pl.BlockSpec((tm, d), row), _full((1, d))],
        out_shape=[dy_shape, jax.ShapeDtypeStruct((t, d), BF16), jax.ShapeDtypeStruct((1, d), F32)],
    )(*args)


def norm_mm_bwd(dy, w, x, nw, scale, dres, *, name, tm=256):
    t, n = dy.shape
    d = x.shape[1]
    steps = t // tm

    def body(dy_ref, w_ref, x_ref, nw_ref, sc_ref, dres_ref, dx_ref, dnw_ref, dsc_ref, dsh_ref):
        i = pl.program_id(0)
        dh = _dot(dy_ref[...].astype(BF16), w_ref[...], 1, 1)
        x = x_ref[...]
        r = lax.rsqrt(jnp.mean(x * x, axis=-1, keepdims=True) + NORM_EPS)
        xn = x * r
        a = nw_ref[...] * (1.0 + sc_ref[...])

        @pl.when(i == 0)
        def _():
            dnw_ref[...] = jnp.zeros_like(dnw_ref)
            dsh_ref[...] = jnp.zeros_like(dsh_ref)

        dnw_ref[...] += _colsum(dh * xn)
        dsh_ref[...] += _colsum(dh)
        dxn = dh * a
        dx_ref[...] = r * (dxn - xn * jnp.mean(dxn * xn, axis=-1, keepdims=True)) + dres_ref[...]

        @pl.when(i == steps - 1)
        def _():
            da = dnw_ref[...]
            dsc_ref[...] = da * nw_ref[...]
            dnw_ref[...] = da * (1.0 + sc_ref[...])

    row = lambda i: (i, 0)
    vec = jax.ShapeDtypeStruct((1, d), F32)
    return _call(
        body, name=name, grid=(steps,),
        in_specs=[pl.BlockSpec((tm, n), row), _full((d, n)), pl.BlockSpec((tm, d), row), _full((1, d)),
                  _full((1, d)), pl.BlockSpec((tm, d), row)],
        out_specs=[pl.BlockSpec((tm, d), row), _full((1, d)), _full((1, d)), _full((1, d))],
        out_shape=[jax.ShapeDtypeStruct((t, d), F32), vec, vec, vec],
    )(dy, w, x, nw, scale, dres)


def _pick_tn(n, k, budget=6 << 20):
    best = LANE
    for m in range(1, n // LANE + 1):
        tn = m * LANE
        if n % tn == 0 and k * tn * 4 <= budget:
            best = tn
    return best


def mm_tn(a, g, *, name, tt=512):
    t, k = a.shape
    n = g.shape[1]
    tn = _pick_tn(n, k)

    def body(a_ref, g_ref, o_ref):
        @pl.when(pl.program_id(1) == 0)
        def _():
            o_ref[...] = jnp.zeros_like(o_ref)

        o_ref[...] += _dot(a_ref[...].astype(BF16), g_ref[...].astype(BF16), 0, 0)

    return _call(
        body, name=name, grid=(n // tn, t // tt),
        in_specs=[pl.BlockSpec((tt, k), lambda j, i: (i, 0)), pl.BlockSpec((tt, tn), lambda j, i: (i, j))],
        out_specs=pl.BlockSpec((k, tn), lambda j, i: (0, j)),
        out_shape=jax.ShapeDtypeStruct((k, n), F32), semantics=("parallel", "arbitrary"),
    )(a, g)


def loss_head(y, target, *, tm=512):
    t, d = y.shape
    steps = t // tm

    def body(y_ref, t_ref, dy_ref, l_ref, acc_ref):
        i = pl.program_id(0)

        @pl.when(i == 0)
        def _():
            acc_ref[...] = jnp.zeros_like(acc_ref)

        e = y_ref[...] - t_ref[...]
        dy_ref[...] = e * (1.0 / d)
        acc_ref[...] += _colsum(e * e)

        @pl.when(i == steps - 1)
        def _():
            tot = jnp.sum(acc_ref[...], axis=-1, keepdims=True) * (0.5 / d)
            l_ref[...] = jnp.broadcast_to(tot, l_ref.shape)

    row = lambda i: (i, 0)
    return _call(
        body, name="loss_head", grid=(steps,),
        in_specs=[pl.BlockSpec((tm, d), row), pl.BlockSpec((tm, d), row)],
        out_specs=[pl.BlockSpec((tm, d), row), _full((8, LANE))],
        out_shape=[jax.ShapeDtypeStruct((t, d), F32), jax.ShapeDtypeStruct((8, LANE), F32)],
        scratch_shapes=[pltpu.VMEM((1, d), F32)],
    )(y, target)


def _shift_rows(x, s):
    if s == 0:
        return x
    t = x.shape[0]
    ri = _iota2(x.shape, 0)
    rolled = pltpu.roll(x, s % t, axis=0)
    if s > 0:
        return jnp.where(ri >= s, rolled, 0.0)
    return jnp.where(ri < t + s, rolled, 0.0)


def _conv_pre(x, w):
    acc = x * w[DN_CONV - 1:DN_CONV, :]
    for j in range(DN_CONV - 1):
        acc = acc + _shift_rows(x, DN_CONV - 1 - j) * w[j:j + 1, :]
    return acc


def dn_conv(proj, conv_w):
    t = proj.shape[0]
    width = 3 * DN_WIDTH

    def body(x_ref, w_ref, o_ref):
        o_ref[...] = _silu(_conv_pre(x_ref[...], w_ref[...]))

    col = lambda j: (0, j)
    return _call(
        body, name="dn_conv", grid=(width // LANE,),
        in_specs=[pl.BlockSpec((t, LANE), col), pl.BlockSpec((8, LANE), col)],
        out_specs=pl.BlockSpec((t, LANE), col),
        out_shape=jax.ShapeDtypeStruct((t, width), F32), semantics=("parallel",),
    )(proj, conv_w)


def dn_conv_bwd(proj, conv_w, dact, dproj):
    t = proj.shape[0]
    width = 3 * DN_WIDTH

    def body(x_ref, w_ref, d_ref, _, dx_ref, dw_ref):
        x = x_ref[...]
        w = w_ref[...]
        dc = d_ref[...] * _dsilu(_conv_pre(x, w))
        dx = dc * w[DN_CONV - 1:DN_CONV, :]
        rows = []
        for j in range(DN_CONV - 1):
            s = DN_CONV - 1 - j
            dx = dx + _shift_rows(dc, -s) * w[j:j + 1, :]
            rows.append(_colsum(dc * _shift_rows(x, s)))
        rows.append(_colsum(dc * x))
        dx_ref[...] = dx
        ri = _iota2((8, LANE), 0)
        dw = jnp.zeros((8, LANE), F32)
        for j in range(DN_CONV):
            dw = dw + jnp.where(ri == j, rows[j], 0.0)
        dw_ref[...] = dw

    col = lambda j: (0, j)
    return _call(
        body, name="dn_conv_bwd", grid=(width // LANE,),
        in_specs=[pl.BlockSpec((t, LANE), col), pl.BlockSpec((8, LANE), col), pl.BlockSpec((t, LANE), col), ANY],
        out_specs=[pl.BlockSpec((t, LANE), col), pl.BlockSpec((8, LANE), col)],
        out_shape=[jax.ShapeDtypeStruct(dproj.shape, F32), jax.ShapeDtypeStruct((8, width), F32)],
        semantics=("parallel",), aliases={3: 0},
    )(proj, conv_w, dact, dproj)


def _t(x):
    return jnp.swapaxes(x, -1, -2)


def _inv_unit_lower(a):
    c = a.shape[-1]
    eye = (_iota2((c, c), 0) == _iota2((c, c), 1)).astype(F32)
    x = eye - a
    p = _hdot(a, a)
    steps = int(math.log2(c)) - 1
    for i in range(steps):
        x = x + _hdot(x, p)
        if i < steps - 1:
            p = _hdot(p, p)
    return x


def _dn_chunk(q, k, v, a, b, alog, dtb, s_in):
    nh, c, d = q.shape
    rq = lax.rsqrt(_rowsum(q * q) + NORM_EPS)
    rk = lax.rsqrt(_rowsum(k * k) + NORM_EPS)
    qh = q * rq
    kn = k * rk
    qs = qh * (d ** -0.5)
    g = -jnp.exp(alog) * _softplus(a + dtb)
    beta = _sigmoid(b)
    ri = _iota2((c, c), 0)
    ci = _iota2((c, c), 1)
    causal = ri >= ci
    strict = ri > ci
    gb = jnp.broadcast_to(g, (nh, c, d))
    gcb = _hdot(jnp.broadcast_to(causal.astype(F32), (nh, c, c)), gb)
    gc = gcb[..., :1]
    gl = _colsum(gb)[..., :1]
    dec = jnp.exp(jnp.where(causal, gc - _t(gcb)[:, :c, :], -1e30))
    kb = kn * beta
    amat = jnp.where(strict, _bdot(kb, kn, 1, 1) * dec, 0.0)
    tinv = _inv_unit_lower(amat)
    e = jnp.exp(gc)
    f = jnp.exp(gl - gc)
    rw = kb * e
    sol = _hdot(tinv, jnp.concatenate([v * beta, rw], axis=-1))
    u = sol[..., :d]
    w = sol[..., d:]
    pmat = jnp.where(causal, _bdot(qs, kn, 1, 1) * dec, 0.0)
    qd = qs * e
    kd = kn * f
    vnew = u - _bdot(w, s_in)
    o = _bdot(qd, s_in) + _bdot(pmat, vnew)
    s_out = s_in * jnp.exp(gl) + _bdot(kd, vnew, 0, 0)
    return dict(rq=rq, rk=rk, qh=qh, kn=kn, qs=qs, g=g, beta=beta, causal=causal, strict=strict, gl=gl,
                dec=dec, kb=kb, amat=amat, tinv=tinv, e=e, f=f, rw=rw, u=u, w=w, pmat=pmat, qd=qd, kd=kd,
                vnew=vnew, o=o, s_out=s_out)


def _dn_chunk_bwd(m, q, v, a, alog, dtb, s_in, do, ds_out):
    nh, c, d = q.shape
    kn, qs, kb, u, w, e, f = m["kn"], m["qs"], m["kb"], m["u"], m["w"], m["e"], m["f"]
    beta, dec, tinv, vnew, kd, qd = m["beta"], m["dec"], m["tinv"], m["vnew"], m["kd"], m["qd"]
    el = jnp.exp(m["gl"])
    dvnew = _bdot(m["pmat"], do, 0, 0) + _bdot(kd, ds_out)
    dp = jnp.where(m["causal"], _bdot(do, vnew, 1, 1), 0.0)
    dqd = _bdot(do, s_in, 1, 1)
    dkd = _bdot(vnew, ds_out, 1, 1)
    ds_in = _bdot(qd, do, 0, 0) + el * ds_out - _bdot(w, dvnew, 0, 0)
    dgl = el * _colsum(_rowsum(s_in * ds_out))
    dw = -_bdot(dvnew, s_in, 1, 1)
    dsol = _hdot(tinv, jnp.concatenate([dvnew, dw], axis=-1), 0, 0)
    dru = dsol[..., :d]
    drw = dsol[..., d:]
    da_m = -jnp.where(m["strict"], _bdot(dsol, jnp.concatenate([u, w], axis=-1), 1, 1), 0.0)
    db_m = da_m * dec
    dq_m = dp * dec
    dkb = _bdot(db_m, kn)
    dkn = _bdot(db_m, kb, 0, 0) + _bdot(dq_m, qs, 0, 0)
    dqs = _bdot(dq_m, kn)
    gmat = da_m * m["amat"] + dp * m["pmat"]
    ones = jnp.ones((nh, c, d), F32)
    dgam = (_hdot(gmat, ones) - _hdot(gmat, ones, 0, 0))[..., :1]
    dqs = dqs + dqd * e
    dgam = dgam + _rowsum(dqd * qd)
    dkn = dkn + dkd * f
    tk = _rowsum(dkd * kd)
    dgam = dgam - tk
    dgl = dgl + _colsum(tk)
    dkb = dkb + drw * e
    dgam = dgam + _rowsum(drw * m["rw"])
    dv = dru * beta
    dbeta = _rowsum(dru * v) + _rowsum(dkb * kn)
    dkn = dkn + dkb * beta
    last = (_iota2((c, 1), 0) == c - 1).astype(F32)
    dgam = dgam + last * dgl
    upper = (_iota2((c, c), 0) <= _iota2((c, c), 1)).astype(F32)
    dg = _hdot(jnp.broadcast_to(upper, (nh, c, c)), jnp.broadcast_to(dgam, (nh, c, d)))[..., :1]
    dqh = dqs * (d ** -0.5)
    dq = m["rq"] * (dqh - m["qh"] * _rowsum(dqh * m["qh"]))
    dk = m["rk"] * (dkn - kn * _rowsum(dkn * kn))
    sg = _sigmoid(a + dtb)
    da = dg * (-jnp.exp(alog)) * sg
    dalog = _colsum(dg * m["g"])
    ddtb = _colsum(da)
    db = dbeta * beta * (1.0 - beta)
    return dq, dk, dv, da, db, dalog, ddtb, ds_in


def _dn_gate(o, z, wn):
    ro = lax.rsqrt(jnp.mean(o * o, axis=-1, keepdims=True) + NORM_EPS)
    n = o * ro
    return n, ro, n * wn * _silu(z)


def _heads(ref, col0):
    d = DN_HEAD_DIM
    return jnp.stack([ref[:, col0 + h * d:col0 + (h + 1) * d] for h in range(DN_HEADS)])


def _dn_inputs(act_ref, ab_ref, sc_ref):
    ab = ab_ref[...]
    sc = sc_ref[...]
    q = _heads(act_ref, 0)
    k = _heads(act_ref, DN_WIDTH)
    v = _heads(act_ref, 2 * DN_WIDTH)
    a = jnp.stack([ab[:, h:h + 1] for h in range(DN_HEADS)])
    b = jnp.stack([ab[:, DN_HEADS + h:DN_HEADS + h + 1] for h in range(DN_HEADS)])
    alog = jnp.stack([sc[0:1, h:h + 1] for h in range(DN_HEADS)])
    dtb = jnp.stack([sc[1:2, h:h + 1] for h in range(DN_HEADS)])
    return q, k, v, a, b, alog, dtb


def dn_fwd(act, proj, scal, wn):
    t = act.shape[0]
    n = t // DN_CHUNK
    d = DN_HEAD_DIM

    def body(act_ref, z_ref, ab_ref, sc_ref, wn_ref, y_ref, st_ref, s_ref):
        @pl.when(pl.program_id(0) == 0)
        def _():
            s_ref[...] = jnp.zeros_like(s_ref)

        q, k, v, a, b, alog, dtb = _dn_inputs(act_ref, ab_ref, sc_ref)
        s_in = s_ref[...]
        st_ref[0] = s_in
        m = _dn_chunk(q, k, v, a, b, alog, dtb, s_in)
        s_ref[...] = m["s_out"]
        y = _dn_gate(m["o"], _heads(z_ref, 0), wn_ref[...])[2]
        for h in range(DN_HEADS):
            y_ref[:, h * d:(h + 1) * d] = y[h]

    return _call(
        body, name="dn_fwd", grid=(n,),
        in_specs=[pl.BlockSpec((DN_CHUNK, 3 * DN_WIDTH), lambda i: (i, 0)),
                  pl.BlockSpec((DN_CHUNK, DN_WIDTH), lambda i: (i, C_Z // DN_WIDTH)),
                  pl.BlockSpec((DN_CHUNK, LANE), lambda i: (i, C_AB // LANE)),
                  _full((8, LANE)), _full((1, d))],
        out_specs=[pl.BlockSpec((DN_CHUNK, DN_WIDTH), lambda i: (i, 0)),
                   pl.BlockSpec((1, DN_HEADS, d, d), lambda i: (i, 0, 0, 0))],
        out_shape=[jax.ShapeDtypeStruct((t, MIX_WIDTH), F32), jax.ShapeDtypeStruct((n, DN_HEADS, d, d), F32)],
        scratch_shapes=[pltpu.VMEM((DN_HEADS, d, d), F32)],
    )(act, proj, proj, scal, wn)


def dn_bwd(act, proj, scal, wn, states, dy):
    t = act.shape[0]
    n = t // DN_CHUNK
    d = DN_HEAD_DIM
    zab = DN_WIDTH + AB_PAD

    def body(act_ref, z_ref, ab_ref, sc_ref, wn_ref, st_ref, dy_ref, dact_ref, dzab_ref, dpar_ref, ds_ref):
        @pl.when(pl.program_id(0) == 0)
        def _():
            ds_ref[...] = jnp.zeros_like(ds_ref)
            dpar_ref[...] = jnp.zeros_like(dpar_ref)

        wnv = wn_ref[...]
        q, k, v, a, b, alog, dtb = _dn_inputs(act_ref, ab_ref, sc_ref)
        s_in = st_ref[0]
        z = _heads(z_ref, 0)
        dyh = _heads(dy_ref, 0)
        m = _dn_chunk(q, k, v, a, b, alog, dtb, s_in)
        nrm, ro, _ = _dn_gate(m["o"], z, wnv)
        sz = _silu(z)
        dz = dyh * nrm * wnv * _dsilu(z)
        dn = dyh * wnv * sz
        dwn = _colsum(dyh * nrm * sz)
        do = ro * (dn - nrm * jnp.mean(dn * nrm, axis=-1, keepdims=True))
        dq, dk, dv, da, db, dalog, ddtb, ds_in = _dn_chunk_bwd(m, q, v, a, alog, dtb, s_in, do, ds_ref[...])
        ds_ref[...] = ds_in
        lane = _iota2((DN_CHUNK, LANE), 1)
        prow = _iota2((8, LANE), 0)
        plane = _iota2((8, LANE), 1)
        dab = jnp.zeros((DN_CHUNK, LANE), F32)
        dpar = jnp.zeros((8, LANE), F32)
        for h in range(DN_HEADS):
            dzab_ref[:, h * d:(h + 1) * d] = dz[h]
            dact_ref[:, h * d:(h + 1) * d] = dq[h]
            dact_ref[:, DN_WIDTH + h * d:DN_WIDTH + (h + 1) * d] = dk[h]
            dact_ref[:, 2 * DN_WIDTH + h * d:2 * DN_WIDTH + (h + 1) * d] = dv[h]
            dab = dab + jnp.where(lane == h, da[h], 0.0) + jnp.where(lane == DN_HEADS + h, db[h], 0.0)
            dpar = dpar + jnp.where((prow == 0) & (plane == h), dalog[h], 0.0)
            dpar = dpar + jnp.where((prow == 1) & (plane == h), ddtb[h], 0.0)
            dpar = dpar + jnp.where(prow == 2, dwn[h], 0.0)
        dzab_ref[:, DN_WIDTH:DN_WIDTH + LANE] = dab
        dzab_ref[:, DN_WIDTH + LANE:] = jnp.zeros((DN_CHUNK, AB_PAD - LANE), F32)
        dpar_ref[...] += dpar

    rev = lambda i: (n - 1 - i, 0)
    return _call(
        body, name="dn_bwd", grid=(n,),
        in_specs=[pl.BlockSpec((DN_CHUNK, 3 * DN_WIDTH), rev),
                  pl.BlockSpec((DN_CHUNK, DN_WIDTH), lambda i: (n - 1 - i, C_Z // DN_WIDTH)),
                  pl.BlockSpec((DN_CHUNK, LANE), lambda i: (n - 1 - i, C_AB // LANE)),
                  _full((8, LANE)), _full((1, d)),
                  pl.BlockSpec((1, DN_HEADS, d, d), lambda i: (n - 1 - i, 0, 0, 0)),
                  pl.BlockSpec((DN_CHUNK, DN_WIDTH), rev)],
        out_specs=[pl.BlockSpec((DN_CHUNK, 3 * DN_WIDTH), rev),
                   pl.BlockSpec((DN_CHUNK, zab), lambda i: (n - 1 - i, C_Z // zab)), _full((8, LANE))],
        out_shape=[jax.ShapeDtypeStruct((t, 3 * DN_WIDTH), F32), jax.ShapeDtypeStruct((t, IN_PAD), F32),
                   jax.ShapeDtypeStruct((8, LANE), F32)],
        scratch_shapes=[pltpu.VMEM((DN_HEADS, d, d), F32)],
    )(act, proj, proj, scal, wn, states, dy)


_INV_SQRT2 = 0.7071067811865476
_INV_SQRT2PI = 0.3989422804014327


def _gelu(x):
    return 0.5 * x * (1.0 + lax.erf(x * _INV_SQRT2))


def _dgelu(x):
    return 0.5 * (1.0 + lax.erf(x * _INV_SQRT2)) + x * jnp.exp(-0.5 * x * x) * _INV_SQRT2PI


def _gm_core(uv, lng, lnb, ws_ref, bst):
    c = uv.shape[0]
    zz = _gelu(uv)
    u = zz[:, :GM_WIDTH]
    vv = zz[:, GM_WIDTH:]
    xc = vv - jnp.mean(vv, axis=-1, keepdims=True)
    rs = lax.rsqrt(jnp.mean(xc * xc, axis=-1, keepdims=True) + NORM_EPS)
    xh = xc * rs
    vn = xh * lng + lnb
    grp = _iota2((c, GM_WIDTH), 1) // GM_GROUP_DIM
    tril = _iota2((c, c), 0) >= _iota2((c, c), 1)
    sv = jnp.zeros((c, GM_WIDTH), F32)
    masks = []
    for g in range(GM_GROUPS):
        mk = grp == g
        masks.append(mk)
        ws = jnp.where(tril, ws_ref[g], 0.0)
        sv = sv + _bdot(ws, jnp.where(mk, vn, 0.0)) + jnp.where(mk, bst[:, g:g + 1], 0.0)
    return u, xh, rs, vn, sv, masks, tril


def gm_fwd(proj, lng, lnb, w_s, bst, ybuf):
    t = proj.shape[0]

    def body(uv_ref, g_ref, b_ref, ws_ref, bst_ref, _, y_ref):
        u, _, _, _, sv, _, _ = _gm_core(uv_ref[...], g_ref[...], b_ref[...], ws_ref, bst_ref[...])
        y_ref[...] = u * sv

    return _call(
        body, name="gm_fwd", grid=(t // GM_CHUNK,),
        in_specs=[pl.BlockSpec((GM_CHUNK, 2 * GM_WIDTH), lambda i: (i, C_UV // (2 * GM_WIDTH))),
                  _full((1, GM_WIDTH)), _full((1, GM_WIDTH)), _full((GM_GROUPS, GM_CHUNK, GM_CHUNK)),
                  _full((GM_CHUNK, LANE)), ANY],
        out_specs=pl.BlockSpec((GM_CHUNK, GM_WIDTH), lambda i: (i, DN_WIDTH // GM_WIDTH)),
        out_shape=jax.ShapeDtypeStruct(ybuf.shape, F32), semantics=("parallel",), aliases={5: 0},
    )(proj, lng, lnb, w_s, bst, ybuf)


def gm_bwd(proj, lng, lnb, w_s, bst, dy, dproj):
    t = proj.shape[0]

    def body(uv_ref, g_ref, b_ref, ws_ref, bst_ref, dy_ref, _, duv_ref, dws_ref, dbst_ref, dln_ref):
        @pl.when(pl.program_id(0) == 0)
        def _():
            dws_ref[...] = jnp.zeros_like(dws_ref)
            dbst_ref[...] = jnp.zeros_like(dbst_ref)
            dln_ref[...] = jnp.zeros_like(dln_ref)

        uv = uv_ref[...]
        lng = g_ref[...]
        u, xh, rs, vn, sv, masks, tril = _gm_core(uv, lng, b_ref[...], ws_ref, bst_ref[...])
        dyv = dy_ref[...]
        dsv = dyv * u
        lane = _iota2((GM_CHUNK, LANE), 1)
        dvn = jnp.zeros_like(dsv)
        dbst = jnp.zeros((GM_CHUNK, LANE), F32)
        for g in range(GM_GROUPS):
            ws = jnp.where(tril, ws_ref[g], 0.0)
            dsg = jnp.where(masks[g], dsv, 0.0)
            dvn = dvn + jnp.where(masks[g], _bdot(ws, dsv, 0, 0), 0.0)
            dws_ref[g] += jnp.where(tril, _bdot(dsg, vn, 1, 1), 0.0)
            dbst = dbst + jnp.where(lane == g, _rowsum(dsg), 0.0)
        dbst_ref[...] += dbst
        row = _iota2((8, GM_WIDTH), 0)
        dln_ref[...] += jnp.where(row == 0, _colsum(dvn * xh), 0.0) + jnp.where(row == 1, _colsum(dvn), 0.0)
        dxh = dvn * lng
        dvv = rs * (dxh - jnp.mean(dxh, axis=-1, keepdims=True) - xh * jnp.mean(dxh * xh, axis=-1, keepdims=True))
        dg = _dgelu(uv)
        duv_ref[:, :GM_WIDTH] = dyv * sv * dg[:, :GM_WIDTH]
        duv_ref[:, GM_WIDTH:] = dvv * dg[:, GM_WIDTH:]

    return _call(
        body, name="gm_bwd", grid=(t // GM_CHUNK,),
        in_specs=[pl.BlockSpec((GM_CHUNK, 2 * GM_WIDTH), lambda i: (i, C_UV // (2 * GM_WIDTH))),
                  _full((1, GM_WIDTH)), _full((1, GM_WIDTH)), _full((GM_GROUPS, GM_CHUNK, GM_CHUNK)),
                  _full((GM_CHUNK, LANE)),
                  pl.BlockSpec((GM_CHUNK, GM_WIDTH), lambda i: (i, DN_WIDTH // GM_WIDTH)), ANY],
        out_specs=[pl.BlockSpec((GM_CHUNK, 2 * GM_WIDTH), lambda i: (i, C_UV // (2 * GM_WIDTH))),
                   _full((GM_GROUPS, GM_CHUNK, GM_CHUNK)), _full((GM_CHUNK, LANE)), _full((8, GM_WIDTH))],
        out_shape=[jax.ShapeDtypeStruct(dproj.shape, F32),
                   jax.ShapeDtypeStruct((GM_GROUPS, GM_CHUNK, GM_CHUNK), F32),
                   jax.ShapeDtypeStruct((GM_CHUNK, LANE), F32), jax.ShapeDtypeStruct((8, GM_WIDTH), F32)],
        aliases={6: 0},
    )(proj, lng, lnb, w_s, bst, dy, dproj)


def _head_mats():
    r = _iota2((SW_WIDTH, SW_WIDTH), 0)
    c = _iota2((SW_WIDTH, SW_WIDTH), 1)
    same = (r // SW_HEAD_DIM) == (c // SW_HEAD_DIM)
    cc = c % SW_HEAD_DIM
    half = ROPE_DIM // 2
    rot = jnp.where((cc < half) & (r == c + half), -1.0, 0.0) + jnp.where((cc >= half) & (cc < ROPE_DIM) & (r == c - half), 1.0, 0.0)
    return same.astype(F32), rot


def _seg_col(s):
    return C_SW // SW_WIDTH + (s // 2) * 3 + s % 2


def _halves(x):
    return x[:, :LANE], x[:, LANE:]


def sw_prep(proj, nw2, cos_t, sin_t, *, tm=512):
    t = proj.shape[0]

    def body(x_ref, w_ref, c_ref, s_ref, o_ref):
        same, rot = _head_mats()
        x = x_ref[...]
        r = lax.rsqrt(_hdot(x * x, same) * (1.0 / SW_HEAD_DIM) + NORM_EPS)
        xn = x * r * w_ref[0]
        o_ref[0, 0], o_ref[0, 1] = _halves(xn * c_ref[...] + _hdot(xn, rot) * s_ref[...])

    return _call(
        body, name="sw_prep", grid=(6, t // tm),
        in_specs=[pl.BlockSpec((tm, SW_WIDTH), lambda s, i: (i, _seg_col(s))),
                  pl.BlockSpec((1, 1, SW_WIDTH), lambda s, i: (s % 2, 0, 0)),
                  pl.BlockSpec((tm, SW_WIDTH), lambda s, i: (i, 0)),
                  pl.BlockSpec((tm, SW_WIDTH), lambda s, i: (i, 0))],
        out_specs=pl.BlockSpec((1, 2, tm, LANE), lambda s, i: (s, 0, i, 0)),
        out_shape=jax.ShapeDtypeStruct((6, 2, t, LANE), F32), semantics=("parallel", "parallel"),
    )(proj, nw2, cos_t, sin_t)


def sw_prep_bwd(proj, nw2, cos_t, sin_t, dkvq, dproj, dnw, p, *, tm=512):
    t = proj.shape[0]
    col0 = C_SW // SW_WIDTH + 3 * p
    seg_col = lambda s: col0 + (s + 1) % 3

    def body(x_ref, w_ref, c_ref, s_ref, d_ref, _, dw0_ref, dx_ref, dw_ref):
        s = pl.program_id(0)
        dout = jnp.concatenate([d_ref[0, 0], d_ref[0, 1]], axis=1)

        @pl.when(s == 1)
        def _():
            dx_ref[...] = dout

        @pl.when((s != 1) & (pl.program_id(1) == 0))
        def _():
            dw_ref[...] = dw0_ref[...]

        @pl.when(s != 1)
        def _():
            same, rot = _head_mats()
            x = x_ref[...]
            w = w_ref[0]
            r = lax.rsqrt(_hdot(x * x, same) * (1.0 / SW_HEAD_DIM) + NORM_EPS)
            xh = x * r
            dxn = dout * c_ref[...] + _hdot(dout * s_ref[...], rot, 1, 1)
            dw_ref[0] += _colsum(dxn * xh)
            dxh = dxn * w
            dx_ref[...] = r * (dxh - xh * (_hdot(dxh * xh, same) * (1.0 / SW_HEAD_DIM)))

    return _call(
        body, name=f"sw_prep_bwd{p}", grid=(3, t // tm),
        in_specs=[pl.BlockSpec((tm, SW_WIDTH), lambda s, i: (i, seg_col(s))),
                  pl.BlockSpec((1, 1, SW_WIDTH), lambda s, i: (1 - s // 2, 0, 0)),
                  pl.BlockSpec((tm, SW_WIDTH), lambda s, i: (i, 0)),
                  pl.BlockSpec((tm, SW_WIDTH), lambda s, i: (i, 0)),
                  pl.BlockSpec((1, 2, tm, LANE), lambda s, i: (s, 0, i, 0)), ANY,
                  pl.BlockSpec((1, 1, SW_WIDTH), lambda s, i: (s // 2, 0, 0))],
        out_specs=[pl.BlockSpec((tm, SW_WIDTH), lambda s, i: (i, seg_col(s))),
                   pl.BlockSpec((1, 1, SW_WIDTH), lambda s, i: (s // 2, 0, 0))],
        out_shape=[jax.ShapeDtypeStruct(dproj.shape, F32), jax.ShapeDtypeStruct((2, 1, SW_WIDTH), F32)],
        semantics=("arbitrary", "arbitrary"), aliases={5: 0},
    )(proj, nw2, cos_t, sin_t, dkvq, dproj, dnw)


_SW_SCALE = SW_HEAD_DIM ** -0.5
_NEG = -1e30


def _sw_masks(has_other):
    ri = _iota2((SW_BLOCK, SW_BLOCK), 0)
    ci = _iota2((SW_BLOCK, SW_BLOCK), 1)
    return ri >= ci, (ci >= ri) & has_other


def _pair(x):
    first = _iota2((1, LANE), 1) < SW_HEAD_DIM
    return jnp.stack([jnp.where(first, x, 0.0), jnp.where(first, 0.0, x)])


def _both(x):
    return jnp.broadcast_to(x.astype(BF16)[None], (2,) + x.shape)


def _unpair(x2):
    first = _iota2((1, LANE), 1) < SW_HEAD_DIM
    return jnp.where(first, x2[0], x2[1])


def _head_cols(x):
    return jnp.stack([x[:, 0:1], x[:, SW_HEAD_DIM:SW_HEAD_DIM + 1]])


def _residues(dil, body):
    if dil == 1:
        body(pl.ds(0, SW_BLOCK))
    else:
        def step(r, carry):
            body(pl.ds(r, SW_BLOCK, stride=dil))
            return carry

        lax.fori_loop(0, dil, step, 0)


def _sw_geometry(t, p):
    dil = SW_DILATIONS[p]
    span = SW_BLOCK * dil
    return dil, span, t // span


def sw_attn(qk, proj, p):
    t = proj.shape[0]
    dil, span, nsp = _sw_geometry(t, p)
    vcol = (C_SW + 3 * SW_WIDTH * p + 2 * SW_WIDTH) // LANE

    def body(q_ref, kc_ref, kp_ref, vc_ref, vp_ref, o_ref, l_ref):
        mc, mp = _sw_masks(pl.program_id(1) != 0)

        def one(rows):
            q2 = _pair(q_ref.at[0, 0][rows, :])
            sc = jnp.where(mc, _bdot(q2, _both(kc_ref.at[0, 0][rows, :]), 1, 1) * _SW_SCALE, _NEG)
            sp = jnp.where(mp, _bdot(q2, _both(kp_ref.at[0, 0][rows, :]), 1, 1) * _SW_SCALE, _NEG)
            mx = jnp.maximum(jnp.max(sc, axis=-1, keepdims=True), jnp.max(sp, axis=-1, keepdims=True))
            pc = jnp.exp(sc - mx)
            pp = jnp.exp(sp - mx)
            den = _rowsum(pc) + _rowsum(pp)
            o2 = (_bdot(pc, _both(vc_ref[rows, :])) + _bdot(pp, _both(vp_ref[rows, :]))) / den
            o_ref.at[0][rows, :] = _unpair(o2)
            l_ref.at[0][rows, :] = _unpair(jnp.broadcast_to(mx + jnp.log(den), o2.shape))

        _residues(dil, one)

    prev = lambda j: jnp.maximum(j - 1, 0)
    seg = lambda s, at: pl.BlockSpec((1, 1, span, LANE), lambda h, j: (s, h, at(j), 0))
    val = lambda at: pl.BlockSpec((span, LANE), lambda h, j: (at(j), vcol + h))
    out = pl.BlockSpec((1, span, LANE), lambda h, j: (h, j, 0))
    cur = lambda j: j
    shp = jax.ShapeDtypeStruct((2, t, LANE), F32)
    return _call(
        body, name=f"sw_attn{p}", grid=(2, nsp),
        in_specs=[seg(2 * p, cur), seg(2 * p + 1, cur), seg(2 * p + 1, prev), val(cur), val(prev)],
        out_specs=[out, out], out_shape=[shp, shp], semantics=("parallel", "parallel"),
    )(qk, qk, qk, proj, proj)


def sw_attn_dkv(qk, proj, dy, lg, dm, p):
    t = proj.shape[0]
    dil, span, nsp = _sw_geometry(t, p)
    vcol = (C_SW + 3 * SW_WIDTH * p + 2 * SW_WIDTH) // LANE
    ycol = (DN_WIDTH + GM_WIDTH) // LANE

    def body(k_ref, v_ref, qc_ref, qn_ref, doc_ref, don_ref, lc_ref, ln_ref, dc_ref, dn_ref, o_ref):
        mc, mn = _sw_masks(pl.program_id(1) + 1 < nsp)

        def one(rows):
            k2 = _both(k_ref.at[0, 0][rows, :])
            v2 = _both(v_ref[rows, :])
            dk = jnp.zeros((2, SW_BLOCK, LANE), F32)
            dv = jnp.zeros((2, SW_BLOCK, LANE), F32)
            for q_ref, do_ref, l_ref, d_ref, mk in ((qc_ref, doc_ref, lc_ref, dc_ref, mc),
                                                    (qn_ref, don_ref, ln_ref, dn_ref, mn)):
                q2 = _pair(q_ref.at[0, 0][rows, :])
                do2 = _pair(do_ref[rows, :])
                pr = jnp.exp(jnp.where(mk, _bdot(q2, k2, 1, 1) * _SW_SCALE, _NEG) - _head_cols(l_ref.at[0][rows, :]))
                dv = dv + _bdot(pr, do2, 0, 0)
                ds = pr * (_bdot(do2, v2, 1, 1) - _head_cols(d_ref.at[0][rows, :]))
                dk = dk + _bdot(ds, q2, 0, 0)
            o_ref.at[0, 0][rows, :] = (dk[0] + dk[1]) * _SW_SCALE
            o_ref.at[1, 0][rows, :] = dv[0] + dv[1]

        _residues(dil, one)

    cur = lambda j: j
    nxt = lambda j: jnp.minimum(j + 1, nsp - 1)
    seg = lambda s, at: pl.BlockSpec((1, 1, span, LANE), lambda h, j: (s, h, at(j), 0))
    col = lambda c0, at: pl.BlockSpec((span, LANE), lambda h, j: (at(j), c0 + h))
    hp = lambda at: pl.BlockSpec((1, span, LANE), lambda h, j: (h, at(j), 0))
    return _call(
        body, name=f"sw_dkv{p}", grid=(2, nsp),
        in_specs=[seg(2 * p + 1, cur), col(vcol, cur), seg(2 * p, cur), seg(2 * p, nxt), col(ycol, cur),
                  col(ycol, nxt), hp(cur), hp(nxt), hp(cur), hp(nxt)],
        out_specs=pl.BlockSpec((2, 1, span, LANE), lambda h, j: (0, h, j, 0)),
        out_shape=jax.ShapeDtypeStruct((3, 2, t, LANE), F32), semantics=("parallel", "parallel"),
    )(qk, proj, qk, qk, dy, dy, lg, lg, dm, dm)


def sw_attn_dq(qk, proj, dy, lg, dm, dkvq, p):
    t = proj.shape[0]
    dil, span, nsp = _sw_geometry(t, p)
    vcol = (C_SW + 3 * SW_WIDTH * p + 2 * SW_WIDTH) // LANE
    ycol = (DN_WIDTH + GM_WIDTH) // LANE

    def body(q_ref, kc_ref, kp_ref, vc_ref, vp_ref, do_ref, l_ref, d_ref, _, dq_ref):
        mc, mp = _sw_masks(pl.program_id(1) != 0)

        def one(rows):
            q2 = _pair(q_ref.at[0, 0][rows, :])
            do2 = _pair(do_ref[rows, :])
            lse = _head_cols(l_ref.at[0][rows, :])
            dd = _head_cols(d_ref.at[0][rows, :])
            kc, kp = _both(kc_ref.at[0, 0][rows, :]), _both(kp_ref.at[0, 0][rows, :])
            pc = jnp.exp(jnp.where(mc, _bdot(q2, kc, 1, 1) * _SW_SCALE, _NEG) - lse)
            pp = jnp.exp(jnp.where(mp, _bdot(q2, kp, 1, 1) * _SW_SCALE, _NEG) - lse)
            dsc = pc * (_bdot(do2, _both(vc_ref[rows, :]), 1, 1) - dd)
            dsp = pp * (_bdot(do2, _both(vp_ref[rows, :]), 1, 1) - dd)
            dq_ref.at[0, 0][rows, :] = _unpair(_bdot(dsc, kc) + _bdot(dsp, kp)) * _SW_SCALE

        _residues(dil, one)

    cur = lambda j: j
    prev = lambda j: jnp.maximum(j - 1, 0)
    seg = lambda s, at: pl.BlockSpec((1, 1, span, LANE), lambda h, j: (s, h, at(j), 0))
    col = lambda c0, at: pl.BlockSpec((span, LANE), lambda h, j: (at(j), c0 + h))
    hp = pl.BlockSpec((1, span, LANE), lambda h, j: (h, j, 0))
    return _call(
        body, name=f"sw_dq{p}", grid=(2, nsp),
        in_specs=[seg(2 * p, cur), seg(2 * p + 1, cur), seg(2 * p + 1, prev), col(vcol, cur), col(vcol, prev),
                  col(ycol, cur), hp, hp, ANY],
        out_specs=pl.BlockSpec((1, 1, span, LANE), lambda h, j: (2, h, j, 0)),
        out_shape=jax.ShapeDtypeStruct(dkvq.shape, F32), semantics=("parallel", "parallel"), aliases={8: 0},
    )(qk, qk, qk, proj, proj, dy, lg, dm, dkvq)


def sw_merge(outs, lses, ybuf, *, tm=512):
    t = ybuf.shape[0]

    def body(o0, o1, o2, l0_ref, l1_ref, l2_ref, _, y_ref, lg_ref):
        l0, l1, l2 = l0_ref[...], l1_ref[...], l2_ref[...]
        mx = jnp.maximum(jnp.maximum(l0, l1), l2)
        lg = mx + jnp.log(jnp.exp(l0 - mx) + jnp.exp(l1 - mx) + jnp.exp(l2 - mx))
        lg_ref[...] = lg
        y = jnp.exp(l0 - lg) * o0[...] + jnp.exp(l1 - lg) * o1[...] + jnp.exp(l2 - lg) * o2[...]
        y_ref[...] = jnp.concatenate([y[0], y[1]], axis=1)

    hp = pl.BlockSpec((2, tm, LANE), lambda i: (0, i, 0))
    return _call(
        body, name="sw_merge", grid=(t // tm,), in_specs=[hp] * 6 + [ANY],
        out_specs=[pl.BlockSpec((tm, SW_WIDTH), lambda i: (i, (DN_WIDTH + GM_WIDTH) // SW_WIDTH)), hp],
        out_shape=[jax.ShapeDtypeStruct(ybuf.shape, F32), jax.ShapeDtypeStruct((2, t, LANE), F32)],
        semantics=("parallel",), aliases={6: 0},
    )(*outs, *lses, ybuf)


def sw_delta(dy, ybuf, *, tm=512):
    t = ybuf.shape[0]

    def body(dy_ref, y_ref, o_ref):
        same, _ = _head_mats()
        o_ref[0], o_ref[1] = _halves(_hdot(dy_ref[...] * y_ref[...], same))

    b1 = pl.BlockSpec((tm, SW_WIDTH), lambda i: (i, (DN_WIDTH + GM_WIDTH) // SW_WIDTH))
    return _call(body, name="sw_delta", grid=(t // tm,), in_specs=[b1, b1],
                 out_specs=pl.BlockSpec((2, tm, LANE), lambda i: (0, i, 0)),
                 out_shape=jax.ShapeDtypeStruct((2, t, LANE), F32), semantics=("parallel",))(dy, ybuf)


def _rope_tables(t):
    inv = ROPE_THETA ** (-jnp.arange(0, ROPE_DIM, 2, dtype=F32) / ROPE_DIM)
    ang = jnp.arange(t, dtype=F32)[:, None] * inv[None, :]
    pad1 = jnp.ones((t, SW_HEAD_DIM - ROPE_DIM), F32)
    pad0 = jnp.zeros((t, SW_HEAD_DIM - ROPE_DIM), F32)
    cos_h = jnp.concatenate([jnp.cos(ang), jnp.cos(ang), pad1], axis=1)
    sin_h = jnp.concatenate([jnp.sin(ang), jnp.sin(ang), pad0], axis=1)
    return jnp.tile(cos_h, (1, SW_HEADS)), jnp.tile(sin_h, (1, SW_HEADS))


def sw_forward(proj, nw2, cos_t, sin_t, ybuf):
    qk = sw_prep(proj, nw2, cos_t, sin_t)
    outs, lses = [], []
    for p in range(len(SW_DILATIONS)):
        o, lse = sw_attn(qk, proj, p)
        outs.append(o)
        lses.append(lse)
    ybuf, lg = sw_merge(outs, lses, ybuf)
    return ybuf, (qk, lg)


def sw_backward(proj, nw2, cos_t, sin_t, res, ybuf, dy, dproj):
    qk, lg = res
    dm = sw_delta(dy, ybuf)
    dnw = jnp.zeros((2, 1, SW_WIDTH), F32)
    for p in range(len(SW_DILATIONS)):
        dkvq = sw_attn_dkv(qk, proj, dy, lg, dm, p)
        dkvq = sw_attn_dq(qk, proj, dy, lg, dm, dkvq, p)
        dproj, dnw = sw_prep_bwd(proj, nw2, cos_t, sin_t, dkvq, dproj, dnw, p)
    return dproj, dnw[::-1, 0]


def _pad_rows(a, rows):
    return jnp.zeros((rows,) + a.shape[1:], a.dtype).at[:a.shape[0]].set(a)


def _consts(sp):
    d = {}
    d["mix_nw"] = sp["mix_norm_w"][:, None, :]
    d["ffn_nw"] = sp["ffn_norm_w"][:, None, :]
    d["cw8"] = jnp.pad(sp["dn_conv_w"], ((0, 0), (0, 8 - DN_CONV), (0, 0)))
    d["scal"] = jnp.pad(jnp.stack([sp["dn_a_log"], sp["dn_dt_bias"]], axis=1), ((0, 0), (0, 6), (0, LANE - DN_HEADS)))
    d["wn"] = sp["dn_out_norm_w"][:, None, :]
    d["lng"] = sp["gm_ln_g"][:, None, :]
    d["lnb"] = sp["gm_ln_b"][:, None, :]
    d["w_s"] = sp["gm_w_s"]
    d["bst"] = jnp.pad(jnp.swapaxes(sp["gm_b_s"], 1, 2), ((0, 0), (0, 0), (0, LANE - GM_GROUPS)))
    d["nw2"] = jnp.stack([jnp.tile(sp["sw_q_norm_w"], (1, SW_HEADS)),
                          jnp.tile(sp["sw_k_norm_w"], (1, SW_HEADS))], axis=1)[:, :, None, :]
    return d


def _layer_fwd(x, mod, wb, cs, tabs):
    h1, proj = norm_mm(x, cs["mix_nw"], mod[1], mod[0], wb["w_in"], swiglu=False, name="in_proj")
    act = dn_conv(proj, cs["cw8"])
    y, states = dn_fwd(act, proj, cs["scal"], cs["wn"])
    y = gm_fwd(proj, cs["lng"], cs["lnb"], cs["w_s"], cs["bst"], y)
    y, swres = sw_forward(proj, cs["nw2"], *tabs, y)
    x1, o1 = resid_mm(y, wb["w_out"], x, mod[2], name="out_proj")
    h2, gu, actf = norm_mm(x1, cs["ffn_nw"], mod[4], mod[3], wb["w_ffn_in"], swiglu=True, name="ffn_in")
    x2, o2 = resid_mm(actf, wb["w_ffn_out"], x1, mod[5], name="ffn_out")
    res = dict(x=x, h1=h1, proj=proj, act=act, states=states, swres=swres, y=y, x1=x1, o1=o1, h2=h2, gu=gu,
               actf=actf, o2=o2)
    return x2, res


def _layer_bwd(dx2, res, mod, wb, cs, tabs):
    dgu, gx2, dgate2 = resid_mm_bwd(dx2, mod[5], res["o2"], wb["w_ffn_out"], res["gu"], name="ffn_out_bwd")
    g_wfo = mm_tn(res["actf"], gx2, name="wg_ffn_out")
    g_wfi = mm_tn(res["h2"], dgu, name="wg_ffn_in")
    dx1, d_ffn_nw, dscale2, dshift2 = norm_mm_bwd(dgu, wb["w_ffn_in"], res["x1"], cs["ffn_nw"], mod[4], dx2,
                                                  name="ffn_in_bwd")
    dy, gx1, dgate1 = resid_mm_bwd(dx1, mod[2], res["o1"], wb["w_out"], None, name="out_proj_bwd")
    g_wout = mm_tn(res["y"], gx1, name="wg_out")
    proj = res["proj"]
    dact, dproj, dpar = dn_bwd(res["act"], proj, cs["scal"], cs["wn"], res["states"], dy)
    dproj, dcw = dn_conv_bwd(proj, cs["cw8"], dact, dproj)
    dproj, dws, dbst, dln = gm_bwd(proj, cs["lng"], cs["lnb"], cs["w_s"], cs["bst"], dy, dproj)
    dproj, dnw = sw_backward(proj, cs["nw2"], *tabs, res["swres"], res["y"], dy, dproj)
    g_win = mm_tn(res["h1"], dproj, name="wg_in")
    dx, d_mix_nw, dscale1, dshift1 = norm_mm_bwd(dproj, wb["w_in"], res["x"], cs["mix_nw"], mod[1], dx1,
                                                 name="in_proj_bwd")
    dmod = jnp.concatenate([dshift1, dscale1, dgate1, dshift2, dscale2, dgate2], axis=1)
    dnw = dnw.reshape(2, SW_HEADS, SW_HEAD_DIM).sum(1)
    small = dict(mix_norm_w=d_mix_nw[0], ffn_norm_w=d_ffn_nw[0], dn_conv_w=dcw[:DN_CONV],
                 dn_a_log=dpar[0, :DN_HEADS], dn_dt_bias=dpar[1, :DN_HEADS], dn_out_norm_w=dpar[2],
                 gm_ln_g=dln[0], gm_ln_b=dln[1], gm_w_s=dws, gm_b_s=dbst[:, :GM_GROUPS].T,
                 sw_q_norm_w=dnw[0], sw_k_norm_w=dnw[1])
    big = dict(w_in=g_win, w_out=g_wout, w_ffn_in=g_wfi, w_ffn_out=g_wfo)
    return dx, big, small, dmod


def _permute_w_in(w):
    pad = jnp.zeros(w.shape[:-1] + (AB_PAD - 8,), w.dtype)
    return jnp.concatenate([w[..., 0:2056], pad, w[..., 2568:IN_WIDTH], w[..., 2056:2568]], axis=-1)


def _unpermute_w_in(g):
    return jnp.concatenate([g[..., 0:2056], g[..., C_UV:IN_PAD], g[..., C_SW:C_UV]], axis=-1)


def _local_step(x, target, mods, weights_of, grads_done, sp):
    layers = mods.shape[0]
    t, d = x.shape
    tabs = _rope_tables(t)
    consts = _consts(sp)
    saved = []
    for layer in range(layers):
        mod = mods[layer].reshape(6, 1, d)
        wb = weights_of(layer, x)
        cs = {k: v[layer] for k, v in consts.items()}
        x, res = _layer_fwd(x, mod, wb, cs, tabs)
        saved.append((res, mod, wb, cs))
    dx, loss = loss_head(x, target)
    smalls, dmods = [], []
    token = jnp.zeros((1, 1), F32)
    for layer in reversed(range(layers)):
        res, mod, wb, cs = saved[layer]
        dx, big, small, dmod = _layer_bwd(dx, res, mod + token, wb, cs, tabs)
        token = grads_done(layer, big)
        smalls.append(small)
        dmods.append(dmod[0])
    smalls, dmods = smalls[::-1], dmods[::-1]
    small = {k: jnp.stack([s[k] for s in smalls]) for k in smalls[0]}
    return loss, dx, small, jnp.stack(dmods)


def mod_fwd(c_all, w_mod, b_shard):
    layers, d, n = w_mod.shape

    def body(c_ref, w_ref, b_ref, o_ref):
        ca = _silu(c_ref[...]).astype(BF16)
        o_ref[0] = _dot(ca, w_ref[0].astype(BF16), 1, 0) + b_ref[0]

    return _call(
        body, name="mod_fwd", grid=(layers,),
        in_specs=[_full((8, d)), pl.BlockSpec((1, d, n), lambda i: (i, 0, 0)),
                  pl.BlockSpec((1, 1, n), lambda i: (i, 0, 0))],
        out_specs=pl.BlockSpec((1, 8, n), lambda i: (i, 0, 0)),
        out_shape=jax.ShapeDtypeStruct((layers, 8, n), F32), semantics=("parallel",),
    )(c_all, w_mod, b_shard)


def mod_bwd(c_all, dmod):
    layers, _, n = dmod.shape
    d = c_all.shape[1]

    def body(c_ref, g_ref, o_ref):
        ca = _silu(c_ref[...]).astype(BF16)
        o_ref[0] = _dot(ca, g_ref[0].astype(BF16), 0, 0)

    return _call(
        body, name="mod_bwd", grid=(layers,),
        in_specs=[_full((8, d)), pl.BlockSpec((1, 8, n), lambda i: (i, 0, 0))],
        out_specs=pl.BlockSpec((1, d, n), lambda i: (i, 0, 0)),
        out_shape=jax.ShapeDtypeStruct((layers, d, n), F32), semantics=("parallel",),
    )(c_all, dmod)


N_DEV = 8


def _place():
    return lax.axis_index("x"), lax.axis_index("y"), lax.axis_index("c")


def _other_chips(x, y):
    return [(1 - x, y), (x, 1 - y), (1 - x, 1 - y)]


def allgather8(x_shard, *, name):
    m_per, n = x_shard.shape

    def body(x_ref, out_ref, send_sems, recv_sems, local_sem):
        x, y, c = _place()
        me, sibling = (x, y, c), (x, y, 1 - c)
        chips = _other_chips(x, y)

        def rows(px, py, pc):
            return out_ref.at[pl.ds((4 * px + 2 * py + pc) * m_per, m_per), :]

        def copy(k, block, to, src=None):
            return pltpu.make_async_remote_copy(
                src_ref=rows(*block) if src is None else src, dst_ref=rows(*block),
                send_sem=send_sems.at[k], recv_sem=recv_sems.at[k], device_id=to, device_id_type=MESH)

        mine = pltpu.make_async_copy(x_ref, rows(*me), local_sem)
        mine.start()
        first = [copy(0, me, sibling, src=x_ref)]
        first += [copy(1 + j, me, (*chip, c), src=x_ref) for j, chip in enumerate(chips)]
        for cp in first:
            cp.start()
        passed = [copy(4 + j, (*chip, c), sibling) for j, chip in enumerate(chips)]
        for j, chip in enumerate(chips):
            copy(1 + j, (*chip, c), me).wait_recv()
            passed[j].start()
        copy(0, sibling, me).wait_recv()
        for j, chip in enumerate(chips):
            copy(4 + j, (*chip, 1 - c), me).wait_recv()
        for cp in first + passed:
            cp.wait_send()
        mine.wait()

    return pl.pallas_call(
        body, name=name, out_shape=jax.ShapeDtypeStruct((N_DEV * m_per, n), x_shard.dtype),
        in_specs=[pl.BlockSpec(memory_space=pltpu.VMEM)], out_specs=pl.BlockSpec(memory_space=pltpu.VMEM),
        scratch_shapes=[pltpu.SemaphoreType.DMA((7,)), pltpu.SemaphoreType.DMA((7,)), pltpu.SemaphoreType.DMA],
    )(x_shard)


HBM = pl.BlockSpec(memory_space=pltpu.HBM)
SEM = pl.BlockSpec(memory_space=pltpu.SEMAPHORE)
_EFFECT = pltpu.SideEffectType.DATAFLOW_SIDE_EFFECTING


def _piece(ref, sliced, chip):
    return ref.at[2 * chip[0] + chip[1]] if sliced else ref


def exchange_start(srcs, *, sliced, name):
    n = len(srcs)
    piece = lambda s: s.shape[1:] if sliced else s.shape

    def body(*refs):
        ins, lands = refs[:n], refs[n:2 * n]
        send_sems, recv_sems = refs[2 * n:2 * n + 2]
        token = refs[-1]
        x, y, c = _place()
        me_s = 2 * x + y
        for a in range(n):
            for j, chip in enumerate(_other_chips(x, y)):
                pltpu.make_async_remote_copy(
                    src_ref=_piece(ins[a], sliced, chip), dst_ref=lands[a].at[me_s], send_sem=send_sems.at[3 * a + j],
                    recv_sem=recv_sems.at[3 * a + j], device_id=(*chip, c), device_id_type=MESH).start()
        token[...] = jnp.zeros_like(token)

    zones = [pltpu.with_memory_space_constraint(lax.empty((4,) + piece(s), s.dtype), pltpu.HBM) for s in srcs]
    srcs = [pltpu.with_memory_space_constraint(s, pltpu.HBM) for s in srcs]
    out = pl.pallas_call(
        body, name=name,
        out_shape=(pltpu.SemaphoreType.DMA((3 * n,)), pltpu.SemaphoreType.DMA((3 * n,)),
                   *[pltpu.HBM(s.shape, s.dtype) for s in srcs], *[pltpu.HBM(z.shape, z.dtype) for z in zones],
                   jax.ShapeDtypeStruct((8, LANE), F32)),
        in_specs=[HBM] * (2 * n),
        out_specs=(SEM, SEM, *[HBM] * (2 * n), pl.BlockSpec(memory_space=pltpu.VMEM)),
        input_output_aliases={i: 2 + i for i in range(2 * n)},
        compiler_params=pltpu.CompilerParams(has_side_effects=_EFFECT),
    )(*srcs, *zones)
    return out[0], out[1], out[2:2 + n], out[2 + n:2 + 2 * n], out[-1]


def exchange_wait(send_sems, recv_sems, srcs, zones, after, *, sliced, name):
    n = len(srcs)

    def body(*refs):
        ins, lands = refs[:n], refs[n:2 * n]
        send_sems, recv_sems = refs[2 * n:2 * n + 2]
        x, y, c = _place()
        for a in range(n):
            for j, chip in enumerate(_other_chips(x, y)):
                copy = pltpu.make_async_remote_copy(
                    src_ref=_piece(ins[a], sliced, chip), dst_ref=lands[a].at[2 * chip[0] + chip[1]],
                    send_sem=send_sems.at[3 * a + j], recv_sem=recv_sems.at[3 * a + j], device_id=(*chip, c),
                    device_id_type=MESH)
                copy.wait_send()
                copy.wait_recv()

    out = pl.pallas_call(
        body, name=name,
        out_shape=tuple(pltpu.HBM(s.shape, s.dtype) for s in (*srcs, *zones)),
        in_specs=[HBM] * (2 * n) + [SEM, SEM, ANY], out_specs=tuple([HBM] * (2 * n)),
        input_output_aliases={i: i for i in range(2 * n)},
        compiler_params=pltpu.CompilerParams(has_side_effects=_EFFECT),
    )(*srcs, *zones, send_sems, recv_sems, after)
    return out[n:]


def sibling_swap(parts):
    n = len(parts)

    def body(*refs):
        ins, outs = refs[:n], refs[n:2 * n]
        send_sems, recv_sems = refs[2 * n:]
        x, y, c = _place()
        cps = []
        for a in range(n):
            cp = pltpu.make_async_remote_copy(
                src_ref=ins[a], dst_ref=outs[a], send_sem=send_sems.at[a], recv_sem=recv_sems.at[a],
                device_id=(x, y, 1 - c), device_id_type=MESH)
            cp.start()
            cps.append(cp)
        for cp in cps:
            cp.wait()

    return pl.pallas_call(
        body, name="sibling_swap", out_shape=[jax.ShapeDtypeStruct(p.shape, p.dtype) for p in parts],
        in_specs=[ANY] * n, out_specs=[ANY] * n,
        scratch_shapes=[pltpu.SemaphoreType.DMA((n,)), pltpu.SemaphoreType.DMA((n,))],
    )(*parts)


def _row_block(rows, cols, budget=1 << 20):
    best = rows if rows % 8 else 8
    for tr in range(8, rows + 1, 8):
        if rows % tr == 0 and tr * cols * 4 <= budget:
            best = tr
    return best


def chip_sum(own, recv, me_s, buf, layer, layers, *, name):
    _, r, n = own.shape
    tr = _row_block(r, n)
    steps = r // tr

    def body(me_ref, own_ref, recv_ref, *rest):
        o_ref = rest[-1]
        me = me_ref[0]
        acc = jnp.zeros((tr, n), F32)
        for s in range(4):
            acc = acc + jnp.where(me == s, own_ref[0], recv_ref[s].astype(F32))
        o_ref[...] = acc

    in_specs = [pl.BlockSpec((1, tr, n), lambda i, me: (me[0], i, 0)), pl.BlockSpec((4, tr, n), lambda i, me: (0, i, 0))]
    args = [me_s, own, recv]
    aliases = {}
    if buf is not None:
        in_specs.append(ANY)
        args.append(buf)
        aliases = {3: 0}
    return pl.pallas_call(
        body, name=name, out_shape=jax.ShapeDtypeStruct((layers * r, n), F32),
        grid_spec=pltpu.PrefetchScalarGridSpec(
            num_scalar_prefetch=1, grid=(steps,), in_specs=in_specs,
            out_specs=pl.BlockSpec((tr, n), lambda i, me: (layer * steps + i, 0))),
        input_output_aliases=aliases,
        compiler_params=pltpu.CompilerParams(dimension_semantics=("parallel",)),
    )(*args)


def _adam_update(w, g, m, v):
    m2 = ADAM_B1 * m + (1.0 - ADAM_B1) * g
    v2 = ADAM_B2 * v + (1.0 - ADAM_B2) * (g * g)
    m_hat = m2 / (1.0 - ADAM_B1 ** ADAM_STEP)
    v_hat = v2 / (1.0 - ADAM_B2 ** ADAM_STEP)
    delta = -ADAM_LR * (m_hat / (jnp.sqrt(v_hat) + ADAM_EPS) + ADAM_WD * w)
    return delta, m2, v2


def adamw(w, g_parts, m, v, *, name):
    r, n = w.shape
    tr = _row_block(r, n)
    k = len(g_parts)

    def body(*refs):
        w_ref, m_ref, v_ref = refs[k], refs[k + 1], refs[k + 2]
        g_ref, d_ref, m2_ref, v2_ref = refs[k + 3:]
        g = refs[0][...]
        for p in refs[1:k]:
            g = g + p[...]
        g_ref[...] = g
        d_ref[...], m2_ref[...], v2_ref[...] = _adam_update(w_ref[...], g, m_ref[...], v_ref[...])

    blk = pl.BlockSpec((tr, n), lambda i: (i, 0))
    shp = jax.ShapeDtypeStruct((r, n), F32)
    return _call(body, name=name, grid=(r // tr,), in_specs=[blk] * (k + 3), out_specs=[blk] * 4,
                 out_shape=[shp] * 4, semantics=("parallel",))(*g_parts, w, m, v)


def adamw_gathered(g_all, w, m, v, *, name):
    _, r, n = g_all.shape
    tr = _row_block(r, n * 4)

    def body(ga_ref, w_ref, m_ref, v_ref, g_ref, d_ref, m2_ref, v2_ref):
        g = ga_ref[0]
        for dev in range(1, N_DEV):
            g = g + ga_ref[dev]
        g_ref[...] = g
        d_ref[...], m2_ref[...], v2_ref[...] = _adam_update(w_ref[...], g, m_ref[...], v_ref[...])

    blk = pl.BlockSpec((tr, n), lambda i: (i, 0))
    shp = jax.ShapeDtypeStruct((r, n), F32)
    return _call(body, name=name, grid=(r // tr,),
                 in_specs=[pl.BlockSpec((N_DEV, tr, n), lambda i: (0, i, 0)), blk, blk, blk], out_specs=[blk] * 4,
                 out_shape=[shp] * 4, semantics=("parallel",))(g_all, w, m, v)


BIG = ("w_in", "w_out", "w_ffn_in", "w_ffn_out")
SMALL = ("b_mod", "mix_norm_w", "ffn_norm_w", "dn_conv_w", "dn_a_log", "dn_dt_bias", "dn_out_norm_w", "gm_ln_g",
         "gm_ln_b", "gm_w_s", "gm_b_s", "sw_q_norm_w", "sw_k_norm_w")
WEIGHTS = ("w_mod", "b_mod", "mix_norm_w", "ffn_norm_w", "w_in", "w_out", "dn_conv_w", "dn_a_log", "dn_dt_bias",
           "dn_out_norm_w", "gm_ln_g", "gm_ln_b", "gm_w_s", "gm_b_s", "sw_q_norm_w", "sw_k_norm_w", "w_ffn_in",
           "w_ffn_out")
PACK_ROWS = 8


def _pack(arrs):
    out = []
    for a in arrs:
        flat = a.reshape(-1).astype(F32)
        rows = -(-flat.shape[0] // (LANE * PACK_ROWS)) * PACK_ROWS
        out.append(jnp.pad(flat, (0, rows * LANE - flat.shape[0])).reshape(rows, LANE))
    return jnp.concatenate(out, axis=0)


def _unpack(packed, shapes):
    out, r0 = [], 0
    for shp in shapes:
        size = math.prod(shp)
        rows = -(-size // (LANE * PACK_ROWS)) * PACK_ROWS
        out.append(packed[r0:r0 + rows].reshape(-1)[:size].reshape(shp))
        r0 += rows
    return out


def kernel(x, c, w_mod, b_mod, mix_norm_w, ffn_norm_w, w_in, w_out, dn_conv_w, dn_a_log, dn_dt_bias, dn_out_norm_w, gm_ln_g, gm_ln_b, gm_w_s, gm_b_s, sw_q_norm_w, sw_k_norm_w, w_ffn_in, w_ffn_out, loss_target, m_w_mod, m_b_mod, m_mix_norm_w, m_ffn_norm_w, m_w_in, m_w_out, m_dn_conv_w, m_dn_a_log, m_dn_dt_bias, m_dn_out_norm_w, m_gm_ln_g, m_gm_ln_b, m_gm_w_s, m_gm_b_s, m_sw_q_norm_w, m_sw_k_norm_w, m_w_ffn_in, m_w_ffn_out, v_w_mod, v_b_mod, v_mix_norm_w, v_ffn_norm_w, v_w_in, v_w_out, v_dn_conv_w, v_dn_a_log, v_dn_dt_bias, v_dn_out_norm_w, v_gm_ln_g, v_gm_ln_b, v_gm_w_s, v_gm_b_s, v_sw_q_norm_w, v_sw_k_norm_w, v_w_ffn_in, v_w_ffn_out):
    w = dict(w_mod=w_mod, b_mod=b_mod, mix_norm_w=mix_norm_w, ffn_norm_w=ffn_norm_w, w_in=w_in, w_out=w_out,
             dn_conv_w=dn_conv_w, dn_a_log=dn_a_log, dn_dt_bias=dn_dt_bias, dn_out_norm_w=dn_out_norm_w,
             gm_ln_g=gm_ln_g, gm_ln_b=gm_ln_b, gm_w_s=gm_w_s, gm_b_s=gm_b_s, sw_q_norm_w=sw_q_norm_w,
             sw_k_norm_w=sw_k_norm_w, w_ffn_in=w_ffn_in, w_ffn_out=w_ffn_out)
    m = dict(w_mod=m_w_mod, b_mod=m_b_mod, mix_norm_w=m_mix_norm_w, ffn_norm_w=m_ffn_norm_w, w_in=m_w_in,
             w_out=m_w_out, dn_conv_w=m_dn_conv_w, dn_a_log=m_dn_a_log, dn_dt_bias=m_dn_dt_bias,
             dn_out_norm_w=m_dn_out_norm_w, gm_ln_g=m_gm_ln_g, gm_ln_b=m_gm_ln_b, gm_w_s=m_gm_w_s, gm_b_s=m_gm_b_s,
             sw_q_norm_w=m_sw_q_norm_w, sw_k_norm_w=m_sw_k_norm_w, w_ffn_in=m_w_ffn_in, w_ffn_out=m_w_ffn_out)
    v = dict(w_mod=v_w_mod, b_mod=v_b_mod, mix_norm_w=v_mix_norm_w, ffn_norm_w=v_ffn_norm_w, w_in=v_w_in,
             w_out=v_w_out, dn_conv_w=v_dn_conv_w, dn_a_log=v_dn_a_log, dn_dt_bias=v_dn_dt_bias,
             dn_out_norm_w=v_dn_out_norm_w, gm_ln_g=v_gm_ln_g, gm_ln_b=v_gm_ln_b, gm_w_s=v_gm_w_s, gm_b_s=v_gm_b_s,
             sw_q_norm_w=v_sw_q_norm_w, sw_k_norm_w=v_sw_k_norm_w, w_ffn_in=v_w_ffn_in, w_ffn_out=v_w_ffn_out)
    layers, d, mod_n = w_mod.shape
    mx, my, mc = _place()
    me_s = 2 * mx + my
    me_dev = 4 * mx + 2 * my + mc

    c_all = allgather8(_pad_rows(c, 8), name="gather_c").reshape(N_DEV, 8, d)[:, 0]
    b_shard = lax.dynamic_slice_in_dim(b_mod, me_s * mod_n, mod_n, axis=1)[:, None, :]
    mod_part = mod_fwd(c_all, w_mod, b_shard)
    mod_parts = allgather8(mod_part.reshape(layers * 8, mod_n), name="gather_mod")
    mod_parts = mod_parts.reshape(4, 2, layers, 8, mod_n)[:, 0]
    mod_all = mod_parts.transpose(1, 2, 0, 3).reshape(layers, 8, 4 * mod_n)
    mods = lax.dynamic_index_in_dim(mod_all, me_dev, axis=1, keepdims=False)

    shards = {k: w[k].astype(BF16) for k in BIG}
    gathers = [exchange_start([shards[k][layer] for k in BIG], sliced=False, name=f"gather_start{layer}")
               for layer in range(layers)]
    mods = mods + sum(g[4][0, 0] for g in gathers)

    def weights_of(layer, x_in):
        send_sems, recv_sems, srcs, zones, _ = gathers[layer]
        zones = exchange_wait(send_sems, recv_sems, srcs, zones, x_in, sliced=False, name=f"gather_wait{layer}")
        g_in, g_out, g_fi, g_fo = [lax.dynamic_update_index_in_dim(z, shards[k][layer], me_s, 0)
                                   for k, z in zip(BIG, zones)]
        return dict(w_in=_permute_w_in(jnp.concatenate([g_in[s] for s in range(4)], axis=-1)),
                    w_out=g_out.reshape(-1, d),
                    w_ffn_in=jnp.concatenate([g_fi[s] for s in range(4)], axis=-1),
                    w_ffn_out=g_fo.reshape(-1, d))

    scatters = {}

    def grads_done(layer, big):
        cut = lambda g, axis: jnp.stack(jnp.split(g, 4, axis=axis))
        own = [cut(_unpermute_w_in(big["w_in"]), 1), cut(big["w_out"], 0), cut(big["w_ffn_in"], 1),
               cut(big["w_ffn_out"], 0)]
        started = exchange_start([g.astype(BF16) for g in own], sliced=True, name=f"scatter_start{layer}")
        scatters[layer] = (started, own)
        return started[4][:1, :1]

    cw = dn_conv_w.shape[-1]
    conv_rows = -(-layers * DN_CONV // 8) * 8
    conv_parts = allgather8(_pad_rows(dn_conv_w.reshape(layers * DN_CONV, cw), conv_rows), name="gather_conv")
    conv_parts = conv_parts.reshape(4, 2, conv_rows, cw)[:, 0, :layers * DN_CONV]
    conv_full = conv_parts.reshape(4, layers, DN_CONV, cw).transpose(1, 2, 0, 3).reshape(layers, DN_CONV, 4 * cw)

    sp = {k: w[k] for k in SMALL}
    sp["dn_conv_w"] = conv_full
    loss_blk, grad_x, small, dmods = _local_step(x[0], loss_target[0], mods, weights_of, grads_done, sp)
    loss = lax.psum(loss_blk[0, 0], ("x", "y", "c"))

    me_arr = jnp.reshape(me_s, (1,)).astype(jnp.int32)
    partial = [None] * len(BIG)
    for layer in range(layers):
        (send_sems, recv_sems, srcs, zones, _), own = scatters[layer]
        zones = exchange_wait(send_sems, recv_sems, srcs, zones, grad_x, sliced=True, name=f"scatter_wait{layer}")
        for a, k in enumerate(BIG):
            partial[a] = chip_sum(own[a], zones[a], me_arr, partial[a], layer, layers, name=f"chip_sum_{k}{layer}")
    theirs = sibling_swap(partial)
    outs = {}
    for k, mine, other in zip(BIG, partial, theirs):
        shp = w[k].shape
        flat = lambda a: a.reshape(-1, shp[-1])
        res = adamw(flat(w[k]), [mine, other], flat(m[k]), flat(v[k]), name="adamw_" + k)
        outs[k] = [a.reshape(shp) for a in res]

    small = dict(small, b_mod=dmods)
    packed = _pack([small[k] for k in SMALL])
    rows = packed.shape[0]
    g_all = allgather8(packed, name="gather_small").reshape(N_DEV, rows, LANE)
    conv_zero = jnp.zeros((layers, DN_CONV, 3 * DN_WIDTH), F32)
    pk = lambda src: _pack([conv_zero if k == "dn_conv_w" else src[k] for k in SMALL])
    res = adamw_gathered(g_all, pk(w), pk(m), pk(v), name="adamw_small")
    shapes = [small[k].shape for k in SMALL]
    un = [_unpack(a, shapes) for a in res]
    for i, k in enumerate(SMALL):
        outs[k] = [un[j][i] for j in range(4)]
    g_conv = lax.dynamic_slice_in_dim(outs["dn_conv_w"][0], me_s * cw, cw, axis=2)
    flat = lambda a: a.reshape(-1, cw)
    res = adamw(flat(dn_conv_w), [flat(g_conv)], flat(m["dn_conv_w"]), flat(v["dn_conv_w"]), name="adamw_conv")
    outs["dn_conv_w"] = [a.reshape(dn_conv_w.shape) for a in res]

    b_rows = layers * 6 * d // LANE
    dmod_all = g_all[:, :b_rows].reshape(N_DEV, layers, 6 * d).transpose(1, 0, 2)
    dmod_shard = lax.dynamic_slice_in_dim(dmod_all, me_s * mod_n, mod_n, axis=2)
    g_wmod = mod_bwd(c_all, dmod_shard)
    flat = lambda a: a.reshape(-1, mod_n)
    res = adamw(flat(w_mod), [flat(g_wmod)], flat(m_w_mod), flat(v_w_mod), name="adamw_w_mod")
    outs["w_mod"] = [a.reshape(w_mod.shape) for a in res]

    result = [loss, grad_x[None]]
    for j in range(4):
        result += [outs[k][j] for k in WEIGHTS]
    return tuple(result)
```

```python
import functools
import math

import jax
import jax.numpy as jnp
from jax import lax
from jax.experimental import pallas as pl
from jax.experimental.pallas import tpu as pltpu

F32 = jnp.float32
BF16 = jnp.bfloat16
HI = lax.Precision.HIGH

NORM_EPS = 1e-6
DN_HEADS = 4
DN_HEAD_DIM = 128
DN_WIDTH = 512
DN_CHUNK = 64
DN_CONV = 4
GM_WIDTH = 256
GM_GROUPS = 4
GM_GROUP_DIM = 64
GM_CHUNK = 128
SW_HEADS = 4
SW_HEAD_DIM = 64
SW_WIDTH = 256
SW_DILATIONS = (1, 4, 16)
SW_BLOCK = 128
ROPE_THETA = 500000.0
ROPE_DIM = 16
LANE = 128

C_QKV = 0
C_Z = 1536
C_AB = 2048
C_SW = 2304
C_UV = 4608
IN_WIDTH = 4872
IN_PAD = 5120
AB_PAD = C_SW - C_AB
MIX_WIDTH = 1024

ADAM_LR = 0.001
ADAM_B1 = 0.9
ADAM_B2 = 0.999
ADAM_EPS = 1e-08
ADAM_WD = 0.01
ADAM_STEP = 10

MESH = pl.DeviceIdType.MESH


def _call(body, *, name, grid, in_specs, out_specs, out_shape, scratch_shapes=(), semantics=None, aliases=None):
    if semantics is None:
        semantics = ("arbitrary",) * len(grid)
    return pl.pallas_call(
        body, name=name, grid=grid, in_specs=in_specs, out_specs=out_specs, out_shape=out_shape,
        scratch_shapes=list(scratch_shapes), input_output_aliases=aliases or {},
        compiler_params=pltpu.CompilerParams(dimension_semantics=semantics),
    )


def _dot(a, b, ca, cb, prec=None):
    if a.ndim == 3:
        dims = (((ca + 1,), (cb + 1,)), ((0,), (0,)))
    else:
        dims = (((ca,), (cb,)), ((), ()))
    return lax.dot_general(a, b, dims, preferred_element_type=F32, precision=prec)


def _bdot(a, b, ca=1, cb=0):
    return _dot(a.astype(BF16), b.astype(BF16), ca, cb)


def _hdot(a, b, ca=1, cb=0):
    return _dot(a.astype(F32), b.astype(F32), ca, cb, HI)


def _sigmoid(x):
    return 1.0 / (1.0 + jnp.exp(-x))


def _silu(x):
    return x * _sigmoid(x)


def _dsilu(x):
    s = _sigmoid(x)
    return s * (1.0 + x * (1.0 - s))


def _softplus(x):
    return jnp.maximum(x, 0.0) + jnp.log(1.0 + jnp.exp(-jnp.abs(x)))


def _iota2(shape, dim):
    return lax.broadcasted_iota(jnp.int32, shape, dim)


def _rowsum(x):
    return jnp.sum(x, axis=-1, keepdims=True)


def _colsum(x):
    return jnp.sum(x, axis=-2, keepdims=True)


def _full(shape):
    return pl.BlockSpec(shape, lambda *_: (0,) * len(shape))


ANY = pl.BlockSpec(memory_space=pl.ANY)


def _norm_mod(x, nw, scale, shift):
    r = lax.rsqrt(jnp.mean(x * x, axis=-1, keepdims=True) + NORM_EPS)
    xn = x * r
    return xn, r, (xn * nw) * (1.0 + scale) + shift


def norm_mm(x, nw, scale, shift, w, *, swiglu, name, tm=256):
    t, d = x.shape
    n = w.shape[1]
    half = n // 2

    def body(x_ref, nw_ref, sc_ref, sh_ref, w_ref, h_ref, y_ref, *act_ref):
        _, _, h = _norm_mod(x_ref[...], nw_ref[...], sc_ref[...], sh_ref[...])
        hb = h.astype(BF16)
        h_ref[...] = hb
        y = _dot(hb, w_ref[...], 1, 0)
        y_ref[...] = y
        if swiglu:
            act_ref[0][...] = (_silu(y[:, :half]) * y[:, half:]).astype(BF16)

    row = lambda i: (i, 0)
    out_shape = [jax.ShapeDtypeStruct((t, d), BF16), jax.ShapeDtypeStruct((t, n), F32)]
    out_specs = [pl.BlockSpec((tm, d), row), pl.BlockSpec((tm, n), row)]
    if swiglu:
        out_shape.append(jax.ShapeDtypeStruct((t, half), BF16))
        out_specs.append(pl.BlockSpec((tm, half), row))
    return _call(
        body, name=name, grid=(t // tm,),
        in_specs=[pl.BlockSpec((tm, d), row), _full((1, d)), _full((1, d)), _full((1, d)), _full((d, n))],
        out_specs=out_specs, out_shape=out_shape, semantics=("parallel",),
    )(x, nw, scale, shift, w)


def resid_mm(y, w, x, gate, *, name, tm=256):
    t, k = y.shape
    d = w.shape[1]

    def body(y_ref, w_ref, x_ref, g_ref, xo_ref, o_ref):
        o = _dot(y_ref[...].astype(BF16), w_ref[...], 1, 0)
        o_ref[...] = o
        xo_ref[...] = x_ref[...] + g_ref[...] * o

    row = lambda i: (i, 0)
    return _call(
        body, name=name, grid=(t // tm,),
        in_specs=[pl.BlockSpec((tm, k), row), _full((k, d)), pl.BlockSpec((tm, d), row), _full((1, d))],
        out_specs=[pl.BlockSpec((tm, d), row), pl.BlockSpec((tm, d), row)],
        out_shape=[jax.ShapeDtypeStruct((t, d), F32), jax.ShapeDtypeStruct((t, d), F32)],
        semantics=("parallel",),
    )(y, w, x, gate)


def resid_mm_bwd(dx, gate, o, w, gu, *, name, tm=256):
    t, d = dx.shape
    k = w.shape[0]
    swiglu = gu is not None

    def body(dx_ref, g_ref, o_ref, w_ref, *rest):
        if swiglu:
            gu_ref, dy_ref, gx_ref, dg_ref = rest
        else:
            dy_ref, gx_ref, dg_ref = rest
        i = pl.program_id(0)
        dxv = dx_ref[...]
        gx = (dxv * g_ref[...]).astype(BF16)
        gx_ref[...] = gx
        part = _colsum(dxv * o_ref[...])

        @pl.when(i == 0)
        def _():
            dg_ref[...] = jnp.zeros_like(dg_ref)

        dg_ref[...] += part
        da = _dot(gx, w_ref[...], 1, 1)
        if swiglu:
            g = gu_ref[:, :k]
            u = gu_ref[:, k:]
            dy_ref[:, :k] = (da * u * _dsilu(g)).astype(BF16)
            dy_ref[:, k:] = (da * _silu(g)).astype(BF16)
        else:
            dy_ref[...] = da

    row = lambda i: (i, 0)
    in_specs = [pl.BlockSpec((tm, d), row), _full((1, d)), pl.BlockSpec((tm, d), row), _full((k, d))]
    args = [dx, gate, o, w]
    if swiglu:
        in_specs.append(pl.BlockSpec((tm, 2 * k), row))
        args.append(gu)
        dy_shape = jax.ShapeDtypeStruct((t, 2 * k), BF16)
        dy_spec = pl.BlockSpec((tm, 2 * k), row)
    else:
        dy_shape = jax.ShapeDtypeStruct((t, k), F32)
        dy_spec = pl.BlockSpec((tm, k), row)
    return _call(
        body, name=name, grid=(t // tm,), in_specs=in_specs,
        out_specs=[dy_spec, pl.BlockSpec((tm, d), row), _full((1, d))],
        out_shape=[dy_shape, jax.ShapeDtypeStruct((t, d), BF16), jax.ShapeDtypeStruct((1, d), F32)],
    )(*args)


def norm_mm_bwd(dy, w, x, nw, scale, dres, *, name, tm=256):
    t, n = dy.shape
    d = x.shape[1]
    steps = t // tm

    def body(dy_ref, w_ref, x_ref, nw_ref, sc_ref, dres_ref, dx_ref, dnw_ref, dsc_ref, dsh_ref):
        i = pl.program_id(0)
        dh = _dot(dy_ref[...].astype(BF16), w_ref[...], 1, 1)
        x = x_ref[...]
        r = lax.rsqrt(jnp.mean(x * x, axis=-1, keepdims=True) + NORM_EPS)
        xn = x * r
        a = nw_ref[...] * (1.0 + sc_ref[...])

        @pl.when(i == 0)
        def _():
            dnw_ref[...] = jnp.zeros_like(dnw_ref)
            dsh_ref[...] = jnp.zeros_like(dsh_ref)

        dnw_ref[...] += _colsum(dh * xn)
        dsh_ref[...] += _colsum(dh)
        dxn = dh * a
        dx_ref[...] = r * (dxn - xn * jnp.mean(dxn * xn, axis=-1, keepdims=True)) + dres_ref[...]

        @pl.when(i == steps - 1)
        def _():
            da = dnw_ref[...]
            dsc_ref[...] = da * nw_ref[...]
            dnw_ref[...] = da * (1.0 + sc_ref[...])

    row = lambda i: (i, 0)
    vec = jax.ShapeDtypeStruct((1, d), F32)
    return _call(
        body, name=name, grid=(steps,),
        in_specs=[pl.BlockSpec((tm, n), row), _full((d, n)), pl.BlockSpec((tm, d), row), _full((1, d)),
                  _full((1, d)), pl.BlockSpec((tm, d), row)],
        out_specs=[pl.BlockSpec((tm, d), row), _full((1, d)), _full((1, d)), _full((1, d))],
        out_shape=[jax.ShapeDtypeStruct((t, d), F32), vec, vec, vec],
    )(dy, w, x, nw, scale, dres)


def _pick_tn(n, k, budget=6 << 20):
    best = LANE
    for m in range(1, n // LANE + 1):
        tn = m * LANE
        if n % tn == 0 and k * tn * 4 <= budget:
            best = tn
    return best


def mm_tn(a, g, *, name, tt=512):
    t, k = a.shape
    n = g.shape[1]
    tn = _pick_tn(n, k)

    def body(a_ref, g_ref, o_ref):
        @pl.when(pl.program_id(1) == 0)
        def _():
            o_ref[...] = jnp.zeros_like(o_ref)

        o_ref[...] += _dot(a_ref[...].astype(BF16), g_ref[...].astype(BF16), 0, 0)

    return _call(
        body, name=name, grid=(n // tn, t // tt),
        in_specs=[pl.BlockSpec((tt, k), lambda j, i: (i, 0)), pl.BlockSpec((tt, tn), lambda j, i: (i, j))],
        out_specs=pl.BlockSpec((k, tn), lambda j, i: (0, j)),
        out_shape=jax.ShapeDtypeStruct((k, n), F32), semantics=("parallel", "arbitrary"),
    )(a, g)


def loss_head(y, target, *, tm=512):
    t, d = y.shape
    steps = t // tm

    def body(y_ref, t_ref, dy_ref, l_ref, acc_ref):
        i = pl.program_id(0)

        @pl.when(i == 0)
        def _():
            acc_ref[...] = jnp.zeros_like(acc_ref)

        e = y_ref[...] - t_ref[...]
        dy_ref[...] = e * (1.0 / d)
        acc_ref[...] += _colsum(e * e)

        @pl.when(i == steps - 1)
        def _():
            tot = jnp.sum(acc_ref[...], axis=-1, keepdims=True) * (0.5 / d)
            l_ref[...] = jnp.broadcast_to(tot, l_ref.shape)

    row = lambda i: (i, 0)
    return _call(
        body, name="loss_head", grid=(steps,),
        in_specs=[pl.BlockSpec((tm, d), row), pl.BlockSpec((tm, d), row)],
        out_specs=[pl.BlockSpec((tm, d), row), _full((8, LANE))],
        out_shape=[jax.ShapeDtypeStruct((t, d), F32), jax.ShapeDtypeStruct((8, LANE), F32)],
        scratch_shapes=[pltpu.VMEM((1, d), F32)],
    )(y, target)


def _shift_rows(x, s):
    if s == 0:
        return x
    t = x.shape[0]
    ri = _iota2(x.shape, 0)
    rolled = pltpu.roll(x, s % t, axis=0)
    if s > 0:
        return jnp.where(ri >= s, rolled, 0.0)
    return jnp.where(ri < t + s, rolled, 0.0)


def _conv_pre(x, w):
    acc = x * w[DN_CONV - 1:DN_CONV, :]
    for j in range(DN_CONV - 1):
        acc = acc + _shift_rows(x, DN_CONV - 1 - j) * w[j:j + 1, :]
    return acc


def dn_conv(proj, conv_w):
    t = proj.shape[0]
    width = 3 * DN_WIDTH

    def body(x_ref, w_ref, o_ref):
        o_ref[...] = _silu(_conv_pre(x_ref[...], w_ref[...]))

    col = lambda j: (0, j)
    return _call(
        body, name="dn_conv", grid=(width // LANE,),
        in_specs=[pl.BlockSpec((t, LANE), col), pl.BlockSpec((8, LANE), col)],
        out_specs=pl.BlockSpec((t, LANE), col),
        out_shape=jax.ShapeDtypeStruct((t, width), F32), semantics=("parallel",),
    )(proj, conv_w)


def dn_conv_bwd(proj, conv_w, dact, dproj):
    t = proj.shape[0]
    width = 3 * DN_WIDTH

    def body(x_ref, w_ref, d_ref, _, dx_ref, dw_ref):
        x = x_ref[...]
        w = w_ref[...]
        dc = d_ref[...] * _dsilu(_conv_pre(x, w))
        dx = dc * w[DN_CONV - 1:DN_CONV, :]
        rows = []
        for j in range(DN_CONV - 1):
            s = DN_CONV - 1 - j
            dx = dx + _shift_rows(dc, -s) * w[j:j + 1, :]
            rows.append(_colsum(dc * _shift_rows(x, s)))
        rows.append(_colsum(dc * x))
        dx_ref[...] = dx
        ri = _iota2((8, LANE), 0)
        dw = jnp.zeros((8, LANE), F32)
        for j in range(DN_CONV):
            dw = dw + jnp.where(ri == j, rows[j], 0.0)
        dw_ref[...] = dw

    col = lambda j: (0, j)
    return _call(
        body, name="dn_conv_bwd", grid=(width // LANE,),
        in_specs=[pl.BlockSpec((t, LANE), col), pl.BlockSpec((8, LANE), col), pl.BlockSpec((t, LANE), col), ANY],
        out_specs=[pl.BlockSpec((t, LANE), col), pl.BlockSpec((8, LANE), col)],
        out_shape=[jax.ShapeDtypeStruct(dproj.shape, F32), jax.ShapeDtypeStruct((8, width), F32)],
        semantics=("parallel",), aliases={3: 0},
    )(proj, conv_w, dact, dproj)


def _t(x):
    return jnp.swapaxes(x, -1, -2)


def _inv_unit_lower(a):
    c = a.shape[-1]
    eye = (_iota2((c, c), 0) == _iota2((c, c), 1)).astype(F32)
    x = eye - a
    p = _hdot(a, a)
    steps = int(math.log2(c)) - 1
    for i in range(steps):
        x = x + _hdot(x, p)
        if i < steps - 1:
            p = _hdot(p, p)
    return x


def _dn_chunk(q, k, v, a, b, alog, dtb, s_in):
    nh, c, d = q.shape
    rq = lax.rsqrt(_rowsum(q * q) + NORM_EPS)
    rk = lax.rsqrt(_rowsum(k * k) + NORM_EPS)
    qh = q * rq
    kn = k * rk
    qs = qh * (d ** -0.5)
    g = -jnp.exp(alog) * _softplus(a + dtb)
    beta = _sigmoid(b)
    ri = _iota2((c, c), 0)
    ci = _iota2((c, c), 1)
    causal = ri >= ci
    strict = ri > ci
    gb = jnp.broadcast_to(g, (nh, c, d))
    gcb = _hdot(jnp.broadcast_to(causal.astype(F32), (nh, c, c)), gb)
    gc = gcb[..., :1]
    gl = _colsum(gb)[..., :1]
    dec = jnp.exp(jnp.where(causal, gc - _t(gcb)[:, :c, :], -1e30))
    kb = kn * beta
    amat = jnp.where(strict, _bdot(kb, kn, 1, 1) * dec, 0.0)
    tinv = _inv_unit_lower(amat)
    e = jnp.exp(gc)
    f = jnp.exp(gl - gc)
    rw = kb * e
    sol = _hdot(tinv, jnp.concatenate([v * beta, rw], axis=-1))
    u = sol[..., :d]
    w = sol[..., d:]
    pmat = jnp.where(causal, _bdot(qs, kn, 1, 1) * dec, 0.0)
    qd = qs * e
    kd = kn * f
    vnew = u - _bdot(w, s_in)
    o = _bdot(qd, s_in) + _bdot(pmat, vnew)
    s_out = s_in * jnp.exp(gl) + _bdot(kd, vnew, 0, 0)
    return dict(rq=rq, rk=rk, qh=qh, kn=kn, qs=qs, g=g, beta=beta, causal=causal, strict=strict, gl=gl,
                dec=dec, kb=kb, amat=amat, tinv=tinv, e=e, f=f, rw=rw, u=u, w=w, pmat=pmat, qd=qd, kd=kd,
                vnew=vnew, o=o, s_out=s_out)


def _dn_chunk_bwd(m, q, v, a, alog, dtb, s_in, do, ds_out):
    nh, c, d = q.shape
    kn, qs, kb, u, w, e, f = m["kn"], m["qs"], m["kb"], m["u"], m["w"], m["e"], m["f"]
    beta, dec, tinv, vnew, kd, qd = m["beta"], m["dec"], m["tinv"], m["vnew"], m["kd"], m["qd"]
    el = jnp.exp(m["gl"])
    dvnew = _bdot(m["pmat"], do, 0, 0) + _bdot(kd, ds_out)
    dp = jnp.where(m["causal"], _bdot(do, vnew, 1, 1), 0.0)
    dqd = _bdot(do, s_in, 1, 1)
    dkd = _bdot(vnew, ds_out, 1, 1)
    ds_in = _bdot(qd, do, 0, 0) + el * ds_out - _bdot(w, dvnew, 0, 0)
    dgl = el * _colsum(_rowsum(s_in * ds_out))
    dw = -_bdot(dvnew, s_in, 1, 1)
    dsol = _hdot(tinv, jnp.concatenate([dvnew, dw], axis=-1), 0, 0)
    dru = dsol[..., :d]
    drw = dsol[..., d:]
    da_m = -jnp.where(m["strict"], _bdot(dsol, jnp.concatenate([u, w], axis=-1), 1, 1), 0.0)
    db_m = da_m * dec
    dq_m = dp * dec
    dkb = _bdot(db_m, kn)
    dkn = _bdot(db_m, kb, 0, 0) + _bdot(dq_m, qs, 0, 0)
    dqs = _bdot(dq_m, kn)
    gmat = da_m * m["amat"] + dp * m["pmat"]
    ones = jnp.ones((nh, c, d), F32)
    dgam = (_hdot(gmat, ones) - _hdot(gmat, ones, 0, 0))[..., :1]
    dqs = dqs + dqd * e
    dgam = dgam + _rowsum(dqd * qd)
    dkn = dkn + dkd * f
    tk = _rowsum(dkd * kd)
    dgam = dgam - tk
    dgl = dgl + _colsum(tk)
    dkb = dkb + drw * e
    dgam = dgam + _rowsum(drw * m["rw"])
    dv = dru * beta
    dbeta = _rowsum(dru * v) + _rowsum(dkb * kn)
    dkn = dkn + dkb * beta
    last = (_iota2((c, 1), 0) == c - 1).astype(F32)
    dgam = dgam + last * dgl
    upper = (_iota2((c, c), 0) <= _iota2((c, c), 1)).astype(F32)
    dg = _hdot(jnp.broadcast_to(upper, (nh, c, c)), jnp.broadcast_to(dgam, (nh, c, d)))[..., :1]
    dqh = dqs * (d ** -0.5)
    dq = m["rq"] * (dqh - m["qh"] * _rowsum(dqh * m["qh"]))
    dk = m["rk"] * (dkn - kn * _rowsum(dkn * kn))
    sg = _sigmoid(a + dtb)
    da = dg * (-jnp.exp(alog)) * sg
    dalog = _colsum(dg * m["g"])
    ddtb = _colsum(da)
    db = dbeta * beta * (1.0 - beta)
    return dq, dk, dv, da, db, dalog, ddtb, ds_in


def _dn_gate(o, z, wn):
    ro = lax.rsqrt(jnp.mean(o * o, axis=-1, keepdims=True) + NORM_EPS)
    n = o * ro
    return n, ro, n * wn * _silu(z)


def _heads(ref, col0):
    d = DN_HEAD_DIM
    return jnp.stack([ref[:, col0 + h * d:col0 + (h + 1) * d] for h in range(DN_HEADS)])


def _dn_inputs(act_ref, ab_ref, sc_ref):
    ab = ab_ref[...]
    sc = sc_ref[...]
    q = _heads(act_ref, 0)
    k = _heads(act_ref, DN_WIDTH)
    v = _heads(act_ref, 2 * DN_WIDTH)
    a = jnp.stack([ab[:, h:h + 1] for h in range(DN_HEADS)])
    b = jnp.stack([ab[:, DN_HEADS + h:DN_HEADS + h + 1] for h in range(DN_HEADS)])
    alog = jnp.stack([sc[0:1, h:h + 1] for h in range(DN_HEADS)])
    dtb = jnp.stack([sc[1:2, h:h + 1] for h in range(DN_HEADS)])
    return q, k, v, a, b, alog, dtb


def dn_fwd(act, proj, scal, wn):
    t = act.shape[0]
    n = t // DN_CHUNK
    d = DN_HEAD_DIM

    def body(act_ref, z_ref, ab_ref, sc_ref, wn_ref, y_ref, st_ref, s_ref):
        @pl.when(pl.program_id(0) == 0)
        def _():
            s_ref[...] = jnp.zeros_like(s_ref)

        q, k, v, a, b, alog, dtb = _dn_inputs(act_ref, ab_ref, sc_ref)
        s_in = s_ref[...]
        st_ref[0] = s_in
        m = _dn_chunk(q, k, v, a, b, alog, dtb, s_in)
        s_ref[...] = m["s_out"]
        y = _dn_gate(m["o"], _heads(z_ref, 0), wn_ref[...])[2]
        for h in range(DN_HEADS):
            y_ref[:, h * d:(h + 1) * d] = y[h]

    return _call(
        body, name="dn_fwd", grid=(n,),
        in_specs=[pl.BlockSpec((DN_CHUNK, 3 * DN_WIDTH), lambda i: (i, 0)),
                  pl.BlockSpec((DN_CHUNK, DN_WIDTH), lambda i: (i, C_Z // DN_WIDTH)),
                  pl.BlockSpec((DN_CHUNK, LANE), lambda i: (i, C_AB // LANE)),
                  _full((8, LANE)), _full((1, d))],
        out_specs=[pl.BlockSpec((DN_CHUNK, DN_WIDTH), lambda i: (i, 0)),
                   pl.BlockSpec((1, DN_HEADS, d, d), lambda i: (i, 0, 0, 0))],
        out_shape=[jax.ShapeDtypeStruct((t, MIX_WIDTH), F32), jax.ShapeDtypeStruct((n, DN_HEADS, d, d), F32)],
        scratch_shapes=[pltpu.VMEM((DN_HEADS, d, d), F32)],
    )(act, proj, proj, scal, wn)


def dn_bwd(act, proj, scal, wn, states, dy):
    t = act.shape[0]
    n = t // DN_CHUNK
    d = DN_HEAD_DIM
    zab = DN_WIDTH + AB_PAD

    def body(act_ref, z_ref, ab_ref, sc_ref, wn_ref, st_ref, dy_ref, dact_ref, dzab_ref, dpar_ref, ds_ref):
        @pl.when(pl.program_id(0) == 0)
        def _():
            ds_ref[...] = jnp.zeros_like(ds_ref)
            dpar_ref[...] = jnp.zeros_like(dpar_ref)

        wnv = wn_ref[...]
        q, k, v, a, b, alog, dtb = _dn_inputs(act_ref, ab_ref, sc_ref)
        s_in = st_ref[0]
        z = _heads(z_ref, 0)
        dyh = _heads(dy_ref, 0)
        m = _dn_chunk(q, k, v, a, b, alog, dtb, s_in)
        nrm, ro, _ = _dn_gate(m["o"], z, wnv)
        sz = _silu(z)
        dz = dyh * nrm * wnv * _dsilu(z)
        dn = dyh * wnv * sz
        dwn = _colsum(dyh * nrm * sz)
        do = ro * (dn - nrm * jnp.mean(dn * nrm, axis=-1, keepdims=True))
        dq, dk, dv, da, db, dalog, ddtb, ds_in = _dn_chunk_bwd(m, q, v, a, alog, dtb, s_in, do, ds_ref[...])
        ds_ref[...] = ds_in
        lane = _iota2((DN_CHUNK, LANE), 1)
        prow = _iota2((8, LANE), 0)
        plane = _iota2((8, LANE), 1)
        dab = jnp.zeros((DN_CHUNK, LANE), F32)
        dpar = jnp.zeros((8, LANE), F32)
        for h in range(DN_HEADS):
            dzab_ref[:, h * d:(h + 1) * d] = dz[h]
            dact_ref[:, h * d:(h + 1) * d] = dq[h]
            dact_ref[:, DN_WIDTH + h * d:DN_WIDTH + (h + 1) * d] = dk[h]
            dact_ref[:, 2 * DN_WIDTH + h * d:2 * DN_WIDTH + (h + 1) * d] = dv[h]
            dab = dab + jnp.where(lane == h, da[h], 0.0) + jnp.where(lane == DN_HEADS + h, db[h], 0.0)
            dpar = dpar + jnp.where((prow == 0) & (plane == h), dalog[h], 0.0)
            dpar = dpar + jnp.where((prow == 1) & (plane == h), ddtb[h], 0.0)
            dpar = dpar + jnp.where(prow == 2, dwn[h], 0.0)
        dzab_ref[:, DN_WIDTH:DN_WIDTH + LANE] = dab
        dzab_ref[:, DN_WIDTH + LANE:] = jnp.zeros((DN_CHUNK, AB_PAD - LANE), F32)
        dpar_ref[...] += dpar

    rev = lambda i: (n - 1 - i, 0)
    return _call(
        body, name="dn_bwd", grid=(n,),
        in_specs=[pl.BlockSpec((DN_CHUNK, 3 * DN_WIDTH), rev),
                  pl.BlockSpec((DN_CHUNK, DN_WIDTH), lambda i: (n - 1 - i, C_Z // DN_WIDTH)),
                  pl.BlockSpec((DN_CHUNK, LANE), lambda i: (n - 1 - i, C_AB // LANE)),
                  _full((8, LANE)), _full((1, d)),
                  pl.BlockSpec((1, DN_HEADS, d, d), lambda i: (n - 1 - i, 0, 0, 0)),
                  pl.BlockSpec((DN_CHUNK, DN_WIDTH), rev)],
        out_specs=[pl.BlockSpec((DN_CHUNK, 3 * DN_WIDTH), rev),
                   pl.BlockSpec((DN_CHUNK, zab), lambda i: (n - 1 - i, C_Z // zab)), _full((8, LANE))],
        out_shape=[jax.ShapeDtypeStruct((t, 3 * DN_WIDTH), F32), jax.ShapeDtypeStruct((t, IN_PAD), F32),
                   jax.ShapeDtypeStruct((8, LANE), F32)],
        scratch_shapes=[pltpu.VMEM((DN_HEADS, d, d), F32)],
    )(act, proj, proj, scal, wn, states, dy)


_INV_SQRT2 = 0.7071067811865476
_INV_SQRT2PI = 0.3989422804014327


def _gelu(x):
    return 0.5 * x * (1.0 + lax.erf(x * _INV_SQRT2))


def _dgelu(x):
    return 0.5 * (1.0 + lax.erf(x * _INV_SQRT2)) + x * jnp.exp(-0.5 * x * x) * _INV_SQRT2PI


def _gm_core(uv, lng, lnb, ws_ref, bst):
    c = uv.shape[0]
    zz = _gelu(uv)
    u = zz[:, :GM_WIDTH]
    vv = zz[:, GM_WIDTH:]
    xc = vv - jnp.mean(vv, axis=-1, keepdims=True)
    rs = lax.rsqrt(jnp.mean(xc * xc, axis=-1, keepdims=True) + NORM_EPS)
    xh = xc * rs
    vn = xh * lng + lnb
    grp = _iota2((c, GM_WIDTH), 1) // GM_GROUP_DIM
    tril = _iota2((c, c), 0) >= _iota2((c, c), 1)
    sv = jnp.zeros((c, GM_WIDTH), F32)
    masks = []
    for g in range(GM_GROUPS):
        mk = grp == g
        masks.append(mk)
        ws = jnp.where(tril, ws_ref[g], 0.0)
        sv = sv + _bdot(ws, jnp.where(mk, vn, 0.0)) + jnp.where(mk, bst[:, g:g + 1], 0.0)
    return u, xh, rs, vn, sv, masks, tril


def gm_fwd(proj, lng, lnb, w_s, bst, ybuf):
    t = proj.shape[0]

    def body(uv_ref, g_ref, b_ref, ws_ref, bst_ref, _, y_ref):
        u, _, _, _, sv, _, _ = _gm_core(uv_ref[...], g_ref[...], b_ref[...], ws_ref, bst_ref[...])
        y_ref[...] = u * sv

    return _call(
        body, name="gm_fwd", grid=(t // GM_CHUNK,),
        in_specs=[pl.BlockSpec((GM_CHUNK, 2 * GM_WIDTH), lambda i: (i, C_UV // (2 * GM_WIDTH))),
                  _full((1, GM_WIDTH)), _full((1, GM_WIDTH)), _full((GM_GROUPS, GM_CHUNK, GM_CHUNK)),
                  _full((GM_CHUNK, LANE)), ANY],
        out_specs=pl.BlockSpec((GM_CHUNK, GM_WIDTH), lambda i: (i, DN_WIDTH // GM_WIDTH)),
        out_shape=jax.ShapeDtypeStruct(ybuf.shape, F32), semantics=("parallel",), aliases={5: 0},
    )(proj, lng, lnb, w_s, bst, ybuf)


def gm_bwd(proj, lng, lnb, w_s, bst, dy, dproj):
    t = proj.shape[0]

    def body(uv_ref, g_ref, b_ref, ws_ref, bst_ref, dy_ref, _, duv_ref, dws_ref, dbst_ref, dln_ref):
        @pl.when(pl.program_id(0) == 0)
        def _():
            dws_ref[...] = jnp.zeros_like(dws_ref)
            dbst_ref[...] = jnp.zeros_like(dbst_ref)
            dln_ref[...] = jnp.zeros_like(dln_ref)

        uv = uv_ref[...]
        lng = g_ref[...]
        u, xh, rs, vn, sv, masks, tril = _gm_core(uv, lng, b_ref[...], ws_ref, bst_ref[...])
        dyv = dy_ref[...]
        dsv = dyv * u
        lane = _iota2((GM_CHUNK, LANE), 1)
        dvn = jnp.zeros_like(dsv)
        dbst = jnp.zeros((GM_CHUNK, LANE), F32)
        for g in range(GM_GROUPS):
            ws = jnp.where(tril, ws_ref[g], 0.0)
            dsg = jnp.where(masks[g], dsv, 0.0)
            dvn = dvn + jnp.where(masks[g], _bdot(ws, dsv, 0, 0), 0.0)
            dws_ref[g] += jnp.where(tril, _bdot(dsg, vn, 1, 1), 0.0)
            dbst = dbst + jnp.where(lane == g, _rowsum(dsg), 0.0)
        dbst_ref[...] += dbst
        row = _iota2((8, GM_WIDTH), 0)
        dln_ref[...] += jnp.where(row == 0, _colsum(dvn * xh), 0.0) + jnp.where(row == 1, _colsum(dvn), 0.0)
        dxh = dvn * lng
        dvv = rs * (dxh - jnp.mean(dxh, axis=-1, keepdims=True) - xh * jnp.mean(dxh * xh, axis=-1, keepdims=True))
        dg = _dgelu(uv)
        duv_ref[:, :GM_WIDTH] = dyv * sv * dg[:, :GM_WIDTH]
        duv_ref[:, GM_WIDTH:] = dvv * dg[:, GM_WIDTH:]

    return _call(
        body, name="gm_bwd", grid=(t // GM_CHUNK,),
        in_specs=[pl.BlockSpec((GM_CHUNK, 2 * GM_WIDTH), lambda i: (i, C_UV // (2 * GM_WIDTH))),
                  _full((1, GM_WIDTH)), _full((1, GM_WIDTH)), _full((GM_GROUPS, GM_CHUNK, GM_CHUNK)),
                  _full((GM_CHUNK, LANE)),
                  pl.BlockSpec((GM_CHUNK, GM_WIDTH), lambda i: (i, DN_WIDTH // GM_WIDTH)), ANY],
        out_specs=[pl.BlockSpec((GM_CHUNK, 2 * GM_WIDTH), lambda i: (i, C_UV // (2 * GM_WIDTH))),
                   _full((GM_GROUPS, GM_CHUNK, GM_CHUNK)), _full((GM_CHUNK, LANE)), _full((8, GM_WIDTH))],
        out_shape=[jax.ShapeDtypeStruct(dproj.shape, F32),
                   jax.ShapeDtypeStruct((GM_GROUPS, GM_CHUNK, GM_CHUNK), F32),
                   jax.ShapeDtypeStruct((GM_CHUNK, LANE), F32), jax.ShapeDtypeStruct((8, GM_WIDTH), F32)],
        aliases={6: 0},
    )(proj, lng, lnb, w_s, bst, dy, dproj)


def _head_mats():
    r = _iota2((SW_WIDTH, SW_WIDTH), 0)
    c = _iota2((SW_WIDTH, SW_WIDTH), 1)
    same = (r // SW_HEAD_DIM) == (c // SW_HEAD_DIM)
    cc = c % SW_HEAD_DIM
    half = ROPE_DIM // 2
    rot = jnp.where((cc < half) & (r == c + half), -1.0, 0.0) + jnp.where((cc >= half) & (cc < ROPE_DIM) & (r == c - half), 1.0, 0.0)
    return same.astype(F32), rot


def _seg_col(s):
    return C_SW // SW_WIDTH + (s // 2) * 3 + s % 2


def _halves(x):
    return x[:, :LANE], x[:, LANE:]


def sw_prep(proj, nw2, cos_t, sin_t, *, tm=512):
    t = proj.shape[0]

    def body(x_ref, w_ref, c_ref, s_ref, o_ref):
        same, rot = _head_mats()
        x = x_ref[...]
        r = lax.rsqrt(_hdot(x * x, same) * (1.0 / SW_HEAD_DIM) + NORM_EPS)
        xn = x * r * w_ref[0]
        o_ref[0, 0], o_ref[0, 1] = _halves(xn * c_ref[...] + _hdot(xn, rot) * s_ref[...])

    return _call(
        body, name="sw_prep", grid=(6, t // tm),
        in_specs=[pl.BlockSpec((tm, SW_WIDTH), lambda s, i: (i, _seg_col(s))),
                  pl.BlockSpec((1, 1, SW_WIDTH), lambda s, i: (s % 2, 0, 0)),
                  pl.BlockSpec((tm, SW_WIDTH), lambda s, i: (i, 0)),
                  pl.BlockSpec((tm, SW_WIDTH), lambda s, i: (i, 0))],
        out_specs=pl.BlockSpec((1, 2, tm, LANE), lambda s, i: (s, 0, i, 0)),
        out_shape=jax.ShapeDtypeStruct((6, 2, t, LANE), F32), semantics=("parallel", "parallel"),
    )(proj, nw2, cos_t, sin_t)


def sw_prep_bwd(proj, nw2, cos_t, sin_t, dkvq, dproj, dnw, p, *, tm=512):
    t = proj.shape[0]
    col0 = C_SW // SW_WIDTH + 3 * p
    seg_col = lambda s: col0 + (s + 1) % 3

    def body(x_ref, w_ref, c_ref, s_ref, d_ref, _, dw0_ref, dx_ref, dw_ref):
        s = pl.program_id(0)
        dout = jnp.concatenate([d_ref[0, 0], d_ref[0, 1]], axis=1)

        @pl.when(s == 1)
        def _():
            dx_ref[...] = dout

        @pl.when((s != 1) & (pl.program_id(1) == 0))
        def _():
            dw_ref[...] = dw0_ref[...]

        @pl.when(s != 1)
        def _():
            same, rot = _head_mats()
            x = x_ref[...]
            w = w_ref[0]
            r = lax.rsqrt(_hdot(x * x, same) * (1.0 / SW_HEAD_DIM) + NORM_EPS)
            xh = x * r
            dxn = dout * c_ref[...] + _hdot(dout * s_ref[...], rot, 1, 1)
            dw_ref[0] += _colsum(dxn * xh)
            dxh = dxn * w
            dx_ref[...] = r * (dxh - xh * (_hdot(dxh * xh, same) * (1.0 / SW_HEAD_DIM)))

    return _call(
        body, name=f"sw_prep_bwd{p}", grid=(3, t // tm),
        in_specs=[pl.BlockSpec((tm, SW_WIDTH), lambda s, i: (i, seg_col(s))),
                  pl.BlockSpec((1, 1, SW_WIDTH), lambda s, i: (1 - s // 2, 0, 0)),
                  pl.BlockSpec((tm, SW_WIDTH), lambda s, i: (i, 0)),
                  pl.BlockSpec((tm, SW_WIDTH), lambda s, i: (i, 0)),
                  pl.BlockSpec((1, 2, tm, LANE), lambda s, i: (s, 0, i, 0)), ANY,
                  pl.BlockSpec((1, 1, SW_WIDTH), lambda s, i: (s // 2, 0, 0))],
        out_specs=[pl.BlockSpec((tm, SW_WIDTH), lambda s, i: (i, seg_col(s))),
                   pl.BlockSpec((1, 1, SW_WIDTH), lambda s, i: (s // 2, 0, 0))],
        out_shape=[jax.ShapeDtypeStruct(dproj.shape, F32), jax.ShapeDtypeStruct((2, 1, SW_WIDTH), F32)],
        semantics=("arbitrary", "arbitrary"), aliases={5: 0},
    )(proj, nw2, cos_t, sin_t, dkvq, dproj, dnw)


_SW_SCALE = SW_HEAD_DIM ** -0.5
_NEG = -1e30


def _sw_masks(has_other):
    ri = _iota2((SW_BLOCK, SW_BLOCK), 0)
    ci = _iota2((SW_BLOCK, SW_BLOCK), 1)
    return ri >= ci, (ci >= ri) & has_other


def _pair(x):
    first = _iota2((1, LANE), 1) < SW_HEAD_DIM
    return jnp.stack([jnp.where(first, x, 0.0), jnp.where(first, 0.0, x)])


def _both(x):
    return jnp.broadcast_to(x.astype(BF16)[None], (2,) + x.shape)


def _unpair(x2):
    first = _iota2((1, LANE), 1) < SW_HEAD_DIM
    return jnp.where(first, x2[0], x2[1])


def _head_cols(x):
    return jnp.stack([x[:, 0:1], x[:, SW_HEAD_DIM:SW_HEAD_DIM + 1]])


def _residues(dil, body):
    if dil == 1:
        body(pl.ds(0, SW_BLOCK))
    else:
        def step(r, carry):
            body(pl.ds(r, SW_BLOCK, stride=dil))
            return carry

        lax.fori_loop(0, dil, step, 0)


def _sw_geometry(t, p):
    dil = SW_DILATIONS[p]
    span = SW_BLOCK * dil
    return dil, span, t // span


def sw_attn(qk, proj, p):
    t = proj.shape[0]
    dil, span, nsp = _sw_geometry(t, p)
    vcol = (C_SW + 3 * SW_WIDTH * p + 2 * SW_WIDTH) // LANE

    def body(q_ref, kc_ref, kp_ref, vc_ref, vp_ref, o_ref, l_ref):
        mc, mp = _sw_masks(pl.program_id(1) != 0)

        def one(rows):
            q2 = _pair(q_ref.at[0, 0][rows, :])
            sc = jnp.where(mc, _bdot(q2, _both(kc_ref.at[0, 0][rows, :]), 1, 1) * _SW_SCALE, _NEG)
            sp = jnp.where(mp, _bdot(q2, _both(kp_ref.at[0, 0][rows, :]), 1, 1) * _SW_SCALE, _NEG)
            mx = jnp.maximum(jnp.max(sc, axis=-1, keepdims=True), jnp.max(sp, axis=-1, keepdims=True))
            pc = jnp.exp(sc - mx)
            pp = jnp.exp(sp - mx)
            den = _rowsum(pc) + _rowsum(pp)
            o2 = (_bdot(pc, _both(vc_ref[rows, :])) + _bdot(pp, _both(vp_ref[rows, :]))) / den
            o_ref.at[0][rows, :] = _unpair(o2)
            l_ref.at[0][rows, :] = _unpair(jnp.broadcast_to(mx + jnp.log(den), o2.shape))

        _residues(dil, one)

    prev = lambda j: jnp.maximum(j - 1, 0)
    seg = lambda s, at: pl.BlockSpec((1, 1, span, LANE), lambda h, j: (s, h, at(j), 0))
    val = lambda at: pl.BlockSpec((span, LANE), lambda h, j: (at(j), vcol + h))
    out = pl.BlockSpec((1, span, LANE), lambda h, j: (h, j, 0))
    cur = lambda j: j
    shp = jax.ShapeDtypeStruct((2, t, LANE), F32)
    return _call(
        body, name=f"sw_attn{p}", grid=(2, nsp),
        in_specs=[seg(2 * p, cur), seg(2 * p + 1, cur), seg(2 * p + 1, prev), val(cur), val(prev)],
        out_specs=[out, out], out_shape=[shp, shp], semantics=("parallel", "parallel"),
    )(qk, qk, qk, proj, proj)


def sw_attn_dkv(qk, proj, dy, lg, dm, p):
    t = proj.shape[0]
    dil, span, nsp = _sw_geometry(t, p)
    vcol = (C_SW + 3 * SW_WIDTH * p + 2 * SW_WIDTH) // LANE
    ycol = (DN_WIDTH + GM_WIDTH) // LANE

    def body(k_ref, v_ref, qc_ref, qn_ref, doc_ref, don_ref, lc_ref, ln_ref, dc_ref, dn_ref, o_ref):
        mc, mn = _sw_masks(pl.program_id(1) + 1 < nsp)

        def one(rows):
            k2 = _both(k_ref.at[0, 0][rows, :])
            v2 = _both(v_ref[rows, :])
            dk = jnp.zeros((2, SW_BLOCK, LANE), F32)
            dv = jnp.zeros((2, SW_BLOCK, LANE), F32)
            for q_ref, do_ref, l_ref, d_ref, mk in ((qc_ref, doc_ref, lc_ref, dc_ref, mc),
                                                    (qn_ref, don_ref, ln_ref, dn_ref, mn)):
                q2 = _pair(q_ref.at[0, 0][rows, :])
                do2 = _pair(do_ref[rows, :])
                pr = jnp.exp(jnp.where(mk, _bdot(q2, k2, 1, 1) * _SW_SCALE, _NEG) - _head_cols(l_ref.at[0][rows, :]))
                dv = dv + _bdot(pr, do2, 0, 0)
                ds = pr * (_bdot(do2, v2, 1, 1) - _head_cols(d_ref.at[0][rows, :]))
                dk = dk + _bdot(ds, q2, 0, 0)
            o_ref.at[0, 0][rows, :] = (dk[0] + dk[1]) * _SW_SCALE
            o_ref.at[1, 0][rows, :] = dv[0] + dv[1]

        _residues(dil, one)

    cur = lambda j: j
    nxt = lambda j: jnp.minimum(j + 1, nsp - 1)
    seg = lambda s, at: pl.BlockSpec((1, 1, span, LANE), lambda h, j: (s, h, at(j), 0))
    col = lambda c0, at: pl.BlockSpec((span, LANE), lambda h, j: (at(j), c0 + h))
    hp = lambda at: pl.BlockSpec((1, span, LANE), lambda h, j: (h, at(j), 0))
    return _call(
        body, name=f"sw_dkv{p}", grid=(2, nsp),
        in_specs=[seg(2 * p + 1, cur), col(vcol, cur), seg(2 * p, cur), seg(2 * p, nxt), col(ycol, cur),
                  col(ycol, nxt), hp(cur), hp(nxt), hp(cur), hp(nxt)],
        out_specs=pl.BlockSpec((2, 1, span, LANE), lambda h, j: (0, h, j, 0)),
        out_shape=jax.ShapeDtypeStruct((3, 2, t, LANE), F32), semantics=("parallel", "parallel"),
    )(qk, proj, qk, qk, dy, dy, lg, lg, dm, dm)


def sw_attn_dq(qk, proj, dy, lg, dm, dkvq, p):
    t = proj.shape[0]
    dil, span, nsp = _sw_geometry(t, p)
    vcol = (C_SW + 3 * SW_WIDTH * p + 2 * SW_WIDTH) // LANE
    ycol = (DN_WIDTH + GM_WIDTH) // LANE

    def body(q_ref, kc_ref, kp_ref, vc_ref, vp_ref, do_ref, l_ref, d_ref, _, dq_ref):
        mc, mp = _sw_masks(pl.program_id(1) != 0)

        def one(rows):
            q2 = _pair(q_ref.at[0, 0][rows, :])
            do2 = _pair(do_ref[rows, :])
            lse = _head_cols(l_ref.at[0][rows, :])
            dd = _head_cols(d_ref.at[0][rows, :])
            kc, kp = _both(kc_ref.at[0, 0][rows, :]), _both(kp_ref.at[0, 0][rows, :])
            pc = jnp.exp(jnp.where(mc, _bdot(q2, kc, 1, 1) * _SW_SCALE, _NEG) - lse)
            pp = jnp.exp(jnp.where(mp, _bdot(q2, kp, 1, 1) * _SW_SCALE, _NEG) - lse)
            dsc = pc * (_bdot(do2, _both(vc_ref[rows, :]), 1, 1) - dd)
            dsp = pp * (_bdot(do2, _both(vp_ref[rows, :]), 1, 1) - dd)
            dq_ref.at[0, 0][rows, :] = _unpair(_bdot(dsc, kc) + _bdot(dsp, kp)) * _SW_SCALE

        _residues(dil, one)

    cur = lambda j: j
    prev = lambda j: jnp.maximum(j - 1, 0)
    seg = lambda s, at: pl.BlockSpec((1, 1, span, LANE), lambda h, j: (s, h, at(j), 0))
    col = lambda c0, at: pl.BlockSpec((span, LANE), lambda h, j: (at(j), c0 + h))
    hp = pl.BlockSpec((1, span, LANE), lambda h, j: (h, j, 0))
    return _call(
        body, name=f"sw_dq{p}", grid=(2, nsp),
        in_specs=[seg(2 * p, cur), seg(2 * p + 1, cur), seg(2 * p + 1, prev), col(vcol, cur), col(vcol, prev),
                  col(ycol, cur), hp, hp, ANY],
        out_specs=pl.BlockSpec((1, 1, span, LANE), lambda h, j: (2, h, j, 0)),
        out_shape=jax.ShapeDtypeStruct(dkvq.shape, F32), semantics=("parallel", "parallel"), aliases={8: 0},
    )(qk, qk, qk, proj, proj, dy, lg, dm, dkvq)


def sw_merge(outs, lses, ybuf, *, tm=512):
    t = ybuf.shape[0]

    def body(o0, o1, o2, l0_ref, l1_ref, l2_ref, _, y_ref, lg_ref):
        l0, l1, l2 = l0_ref[...], l1_ref[...], l2_ref[...]
        mx = jnp.maximum(jnp.maximum(l0, l1), l2)
        lg = mx + jnp.log(jnp.exp(l0 - mx) + jnp.exp(l1 - mx) + jnp.exp(l2 - mx))
        lg_ref[...] = lg
        y = jnp.exp(l0 - lg) * o0[...] + jnp.exp(l1 - lg) * o1[...] + jnp.exp(l2 - lg) * o2[...]
        y_ref[...] = jnp.concatenate([y[0], y[1]], axis=1)

    hp = pl.BlockSpec((2, tm, LANE), lambda i: (0, i, 0))
    return _call(
        body, name="sw_merge", grid=(t // tm,), in_specs=[hp] * 6 + [ANY],
        out_specs=[pl.BlockSpec((tm, SW_WIDTH), lambda i: (i, (DN_WIDTH + GM_WIDTH) // SW_WIDTH)), hp],
        out_shape=[jax.ShapeDtypeStruct(ybuf.shape, F32), jax.ShapeDtypeStruct((2, t, LANE), F32)],
        semantics=("parallel",), aliases={6: 0},
    )(*outs, *lses, ybuf)


def sw_delta(dy, ybuf, *, tm=512):
    t = ybuf.shape[0]

    def body(dy_ref, y_ref, o_ref):
        same, _ = _head_mats()
        o_ref[0], o_ref[1] = _halves(_hdot(dy_ref[...] * y_ref[...], same))

    b1 = pl.BlockSpec((tm, SW_WIDTH), lambda i: (i, (DN_WIDTH + GM_WIDTH) // SW_WIDTH))
    return _call(body, name="sw_delta", grid=(t // tm,), in_specs=[b1, b1],
                 out_specs=pl.BlockSpec((2, tm, LANE), lambda i: (0, i, 0)),
                 out_shape=jax.ShapeDtypeStruct((2, t, LANE), F32), semantics=("parallel",))(dy, ybuf)


def _rope_tables(t):
    inv = ROPE_THETA ** (-jnp.arange(0, ROPE_DIM, 2, dtype=F32) / ROPE_DIM)
    ang = jnp.arange(t, dtype=F32)[:, None] * inv[None, :]
    pad1 = jnp.ones((t, SW_HEAD_DIM - ROPE_DIM), F32)
    pad0 = jnp.zeros((t, SW_HEAD_DIM - ROPE_DIM), F32)
    cos_h = jnp.concatenate([jnp.cos(ang), jnp.cos(ang), pad1], axis=1)
    sin_h = jnp.concatenate([jnp.sin(ang), jnp.sin(ang), pad0], axis=1)
    return jnp.tile(cos_h, (1, SW_HEADS)), jnp.tile(sin_h, (1, SW_HEADS))


def sw_forward(proj, nw2, cos_t, sin_t, ybuf):
    qk = sw_prep(proj, nw2, cos_t, sin_t)
    outs, lses = [], []
    for p in range(len(SW_DILATIONS)):
        o, lse = sw_attn(qk, proj, p)
        outs.append(o)
        lses.append(lse)
    ybuf, lg = sw_merge(outs, lses, ybuf)
    return ybuf, (qk, lg)


def sw_backward(proj, nw2, cos_t, sin_t, res, ybuf, dy, dproj):
    qk, lg = res
    dm = sw_delta(dy, ybuf)
    dnw = jnp.zeros((2, 1, SW_WIDTH), F32)
    for p in range(len(SW_DILATIONS)):
        dkvq = sw_attn_dkv(qk, proj, dy, lg, dm, p)
        dkvq = sw_attn_dq(qk, proj, dy, lg, dm, dkvq, p)
        dproj, dnw = sw_prep_bwd(proj, nw2, cos_t, sin_t, dkvq, dproj, dnw, p)
    return dproj, dnw[::-1, 0]


def _pad_rows(a, rows):
    return jnp.zeros((rows,) + a.shape[1:], a.dtype).at[:a.shape[0]].set(a)


def _consts(sp):
    d = {}
    d["mix_nw"] = sp["mix_norm_w"][:, None, :]
    d["ffn_nw"] = sp["ffn_norm_w"][:, None, :]
    d["cw8"] = jnp.pad(sp["dn_conv_w"], ((0, 0), (0, 8 - DN_CONV), (0, 0)))
    d["scal"] = jnp.pad(jnp.stack([sp["dn_a_log"], sp["dn_dt_bias"]], axis=1), ((0, 0), (0, 6), (0, LANE - DN_HEADS)))
    d["wn"] = sp["dn_out_norm_w"][:, None, :]
    d["lng"] = sp["gm_ln_g"][:, None, :]
    d["lnb"] = sp["gm_ln_b"][:, None, :]
    d["w_s"] = sp["gm_w_s"]
    d["bst"] = jnp.pad(jnp.swapaxes(sp["gm_b_s"], 1, 2), ((0, 0), (0, 0), (0, LANE - GM_GROUPS)))
    d["nw2"] = jnp.stack([jnp.tile(sp["sw_q_norm_w"], (1, SW_HEADS)),
                          jnp.tile(sp["sw_k_norm_w"], (1, SW_HEADS))], axis=1)[:, :, None, :]
    return d


def _layer_fwd(x, mod, get_w, cs, tabs):
    wb = dict(get_w("w_in", x))
    h1, proj = norm_mm(x, cs["mix_nw"], mod[1], mod[0], wb["w_in"], swiglu=False, name="in_proj")
    act = dn_conv(proj, cs["cw8"])
    y, states = dn_fwd(act, proj, cs["scal"], cs["wn"])
    y = gm_fwd(proj, cs["lng"], cs["lnb"], cs["w_s"], cs["bst"], y)
    y, swres = sw_forward(proj, cs["nw2"], *tabs, y)
    wb.update(get_w("w_out", y))
    x1, o1 = resid_mm(y, wb["w_out"], x, mod[2], name="out_proj")
    wb.update(get_w("ffn", x1))
    h2, gu, actf = norm_mm(x1, cs["ffn_nw"], mod[4], mod[3], wb["w_ffn_in"], swiglu=True, name="ffn_in")
    x2, o2 = resid_mm(actf, wb["w_ffn_out"], x1, mod[5], name="ffn_out")
    res = dict(x=x, h1=h1, proj=proj, act=act, states=states, swres=swres, y=y, x1=x1, o1=o1, h2=h2, gu=gu,
               actf=actf, o2=o2)
    return x2, res, wb


def _layer_bwd(dx2, res, mod, wb, cs, tabs, grads_done):
    dgu, gx2, dgate2 = resid_mm_bwd(dx2, mod[5], res["o2"], wb["w_ffn_out"], res["gu"], name="ffn_out_bwd")
    g_wfo = mm_tn(res["actf"], gx2, name="wg_ffn_out")
    g_wfi = mm_tn(res["h2"], dgu, name="wg_ffn_in")
    token = grads_done("ffn", dict(w_ffn_in=g_wfi, w_ffn_out=g_wfo))
    dx1, d_ffn_nw, dscale2, dshift2 = norm_mm_bwd(dgu, wb["w_ffn_in"], res["x1"], cs["ffn_nw"], mod[4] + token, dx2,
                                                  name="ffn_in_bwd")
    dy, gx1, dgate1 = resid_mm_bwd(dx1, mod[2], res["o1"], wb["w_out"], None, name="out_proj_bwd")
    g_wout = mm_tn(res["y"], gx1, name="wg_out")
    proj = res["proj"]
    dact, dproj, dpar = dn_bwd(res["act"], proj, cs["scal"], cs["wn"], res["states"], dy)
    dproj, dcw = dn_conv_bwd(proj, cs["cw8"], dact, dproj)
    dproj, dws, dbst, dln = gm_bwd(proj, cs["lng"], cs["lnb"], cs["w_s"], cs["bst"], dy, dproj)
    dproj, dnw = sw_backward(proj, cs["nw2"], *tabs, res["swres"], res["y"], dy, dproj)
    g_win = mm_tn(res["h1"], dproj, name="wg_in")
    dx, d_mix_nw, dscale1, dshift1 = norm_mm_bwd(dproj, wb["w_in"], res["x"], cs["mix_nw"], mod[1], dx1,
                                                 name="in_proj_bwd")
    dmod = jnp.concatenate([dshift1, dscale1, dgate1, dshift2, dscale2, dgate2], axis=1)
    dnw = dnw.reshape(2, SW_HEADS, SW_HEAD_DIM).sum(1)
    small = dict(mix_norm_w=d_mix_nw[0], ffn_norm_w=d_ffn_nw[0], dn_conv_w=dcw[:DN_CONV],
                 dn_a_log=dpar[0, :DN_HEADS], dn_dt_bias=dpar[1, :DN_HEADS], dn_out_norm_w=dpar[2],
                 gm_ln_g=dln[0], gm_ln_b=dln[1], gm_w_s=dws, gm_b_s=dbst[:, :GM_GROUPS].T,
                 sw_q_norm_w=dnw[0], sw_k_norm_w=dnw[1])
    token = grads_done("mix", dict(w_in=g_win, w_out=g_wout))
    return dx, small, dmod, token


def _permute_w_in(w):
    pad = jnp.zeros(w.shape[:-1] + (AB_PAD - 8,), w.dtype)
    return jnp.concatenate([w[..., 0:2056], pad, w[..., 2568:IN_WIDTH], w[..., 2056:2568]], axis=-1)


def _unpermute_w_in(g):
    return jnp.concatenate([g[..., 0:2056], g[..., C_UV:IN_PAD], g[..., C_SW:C_UV]], axis=-1)


def _local_step(x, target, mods, weights_of, grads_done, sp):
    layers = mods.shape[0]
    t, d = x.shape
    tabs = _rope_tables(t)
    consts = _consts(sp)
    saved = []
    for layer in range(layers):
        mod = mods[layer].reshape(6, 1, d)
        cs = {k: v[layer] for k, v in consts.items()}
        x, res, wb = _layer_fwd(x, mod, functools.partial(weights_of, layer), cs, tabs)
        saved.append((res, mod, wb, cs))
    dx, loss = loss_head(x, target)
    smalls, dmods = [], []
    token = jnp.zeros((1, 1), F32)
    for layer in reversed(range(layers)):
        res, mod, wb, cs = saved[layer]
        dx, small, dmod, token = _layer_bwd(dx, res, mod + token, wb, cs, tabs, functools.partial(grads_done, layer))
        smalls.append(small)
        dmods.append(dmod[0])
    smalls, dmods = smalls[::-1], dmods[::-1]
    small = {k: jnp.stack([s[k] for s in smalls]) for k in smalls[0]}
    return loss, dx, small, jnp.stack(dmods)


def mod_fwd(c_all, w_mod, b_shard):
    layers, d, n = w_mod.shape

    def body(c_ref, w_ref, b_ref, o_ref):
        ca = _silu(c_ref[...]).astype(BF16)
        o_ref[0] = _dot(ca, w_ref[0].astype(BF16), 1, 0) + b_ref[0]

    return _call(
        body, name="mod_fwd", grid=(layers,),
        in_specs=[_full((8, d)), pl.BlockSpec((1, d, n), lambda i: (i, 0, 0)),
                  pl.BlockSpec((1, 1, n), lambda i: (i, 0, 0))],
        out_specs=pl.BlockSpec((1, 8, n), lambda i: (i, 0, 0)),
        out_shape=jax.ShapeDtypeStruct((layers, 8, n), F32), semantics=("parallel",),
    )(c_all, w_mod, b_shard)


def mod_bwd(c_all, dmod):
    layers, _, n = dmod.shape
    d = c_all.shape[1]

    def body(c_ref, g_ref, o_ref):
        ca = _silu(c_ref[...]).astype(BF16)
        o_ref[0] = _dot(ca, g_ref[0].astype(BF16), 0, 0)

    return _call(
        body, name="mod_bwd", grid=(layers,),
        in_specs=[_full((8, d)), pl.BlockSpec((1, 8, n), lambda i: (i, 0, 0))],
        out_specs=pl.BlockSpec((1, d, n), lambda i: (i, 0, 0)),
        out_shape=jax.ShapeDtypeStruct((layers, d, n), F32), semantics=("parallel",),
    )(c_all, dmod)


N_DEV = 8


def _place():
    return lax.axis_index("x"), lax.axis_index("y"), lax.axis_index("c")


def _other_chips(x, y):
    return [(1 - x, y), (x, 1 - y), (1 - x, 1 - y)]


def allgather8(x_shard, *, name):
    m_per, n = x_shard.shape

    def body(x_ref, out_ref, send_sems, recv_sems, local_sem):
        x, y, c = _place()
        me, sibling = (x, y, c), (x, y, 1 - c)
        chips = _other_chips(x, y)

        def rows(px, py, pc):
            return out_ref.at[pl.ds((4 * px + 2 * py + pc) * m_per, m_per), :]

        def copy(k, block, to, src=None):
            return pltpu.make_async_remote_copy(
                src_ref=rows(*block) if src is None else src, dst_ref=rows(*block),
                send_sem=send_sems.at[k], recv_sem=recv_sems.at[k], device_id=to, device_id_type=MESH)

        mine = pltpu.make_async_copy(x_ref, rows(*me), local_sem)
        mine.start()
        first = [copy(0, me, sibling, src=x_ref)]
        first += [copy(1 + j, me, (*chip, c), src=x_ref) for j, chip in enumerate(chips)]
        for cp in first:
            cp.start()
        passed = [copy(4 + j, (*chip, c), sibling) for j, chip in enumerate(chips)]
        for j, chip in enumerate(chips):
            copy(1 + j, (*chip, c), me).wait_recv()
            passed[j].start()
        copy(0, sibling, me).wait_recv()
        for j, chip in enumerate(chips):
            copy(4 + j, (*chip, 1 - c), me).wait_recv()
        for cp in first + passed:
            cp.wait_send()
        mine.wait()

    return pl.pallas_call(
        body, name=name, out_shape=jax.ShapeDtypeStruct((N_DEV * m_per, n), x_shard.dtype),
        in_specs=[pl.BlockSpec(memory_space=pltpu.VMEM)], out_specs=pl.BlockSpec(memory_space=pltpu.VMEM),
        scratch_shapes=[pltpu.SemaphoreType.DMA((7,)), pltpu.SemaphoreType.DMA((7,)), pltpu.SemaphoreType.DMA],
    )(x_shard)


HBM = pl.BlockSpec(memory_space=pltpu.HBM)
SEM = pl.BlockSpec(memory_space=pltpu.SEMAPHORE)
_EFFECT = pltpu.SideEffectType.DATAFLOW_SIDE_EFFECTING


def _piece(ref, sliced, chip):
    return ref.at[2 * chip[0] + chip[1]] if sliced else ref


def exchange_start(srcs, after, *, sliced, name):
    n = len(srcs)
    piece = lambda s: s.shape[1:] if sliced else s.shape

    def body(*refs):
        ins, lands = refs[:n], refs[n:2 * n]
        send_sems, recv_sems = refs[2 * n + len(after):2 * n + len(after) + 2]
        token = refs[-1]
        x, y, c = _place()
        me_s = 2 * x + y
        for a in range(n):
            for j, chip in enumerate(_other_chips(x, y)):
                pltpu.make_async_remote_copy(
                    src_ref=_piece(ins[a], sliced, chip), dst_ref=lands[a].at[me_s], send_sem=send_sems.at[3 * a + j],
                    recv_sem=recv_sems.at[3 * a + j], device_id=(*chip, c), device_id_type=MESH).start()
        token[...] = jnp.zeros_like(token)

    zones = [pltpu.with_memory_space_constraint(lax.empty((4,) + piece(s), s.dtype), pltpu.HBM) for s in srcs]
    srcs = [pltpu.with_memory_space_constraint(s, pltpu.HBM) for s in srcs]
    out = pl.pallas_call(
        body, name=name,
        out_shape=(pltpu.SemaphoreType.DMA((3 * n,)), pltpu.SemaphoreType.DMA((3 * n,)),
                   *[pltpu.HBM(s.shape, s.dtype) for s in srcs], *[pltpu.HBM(z.shape, z.dtype) for z in zones],
                   jax.ShapeDtypeStruct((8, LANE), F32)),
        in_specs=[HBM] * (2 * n) + [ANY] * len(after),
        out_specs=(SEM, SEM, *[HBM] * (2 * n), pl.BlockSpec(memory_space=pltpu.VMEM)),
        input_output_aliases={i: 2 + i for i in range(2 * n)},
        compiler_params=pltpu.CompilerParams(has_side_effects=_EFFECT),
    )(*srcs, *zones, *after)
    return out[0], out[1], out[2:2 + n], out[2 + n:2 + 2 * n], out[-1]


def exchange_wait(send_sems, recv_sems, srcs, zones, after, *, which, sliced, name):
    n = len(srcs)

    def body(*refs):
        ins, lands = refs[:n], refs[n:2 * n]
        send_sems, recv_sems = refs[2 * n:2 * n + 2]
        x, y, c = _place()
        for a in range(n):
            for j, chip in enumerate(_other_chips(x, y)):
                copy = pltpu.make_async_remote_copy(
                    src_ref=_piece(ins[a], sliced, chip), dst_ref=lands[a].at[2 * chip[0] + chip[1]],
                    send_sem=send_sems.at[3 * which[a] + j], recv_sem=recv_sems.at[3 * which[a] + j],
                    device_id=(*chip, c), device_id_type=MESH)
                copy.wait_send()
                copy.wait_recv()

    out = pl.pallas_call(
        body, name=name,
        out_shape=tuple(pltpu.HBM(s.shape, s.dtype) for s in (*srcs, *zones)),
        in_specs=[HBM] * (2 * n) + [SEM, SEM, ANY], out_specs=tuple([HBM] * (2 * n)),
        input_output_aliases={i: i for i in range(2 * n)},
        compiler_params=pltpu.CompilerParams(has_side_effects=_EFFECT),
    )(*srcs, *zones, send_sems, recv_sems, after)
    return out[n:]


def sibling_swap(parts):
    n = len(parts)

    def body(*refs):
        ins, outs = refs[:n], refs[n:2 * n]
        send_sems, recv_sems = refs[2 * n:]
        x, y, c = _place()
        cps = []
        for a in range(n):
            cp = pltpu.make_async_remote_copy(
                src_ref=ins[a], dst_ref=outs[a], send_sem=send_sems.at[a], recv_sem=recv_sems.at[a],
                device_id=(x, y, 1 - c), device_id_type=MESH)
            cp.start()
            cps.append(cp)
        for cp in cps:
            cp.wait()

    return pl.pallas_call(
        body, name="sibling_swap", out_shape=[jax.ShapeDtypeStruct(p.shape, p.dtype) for p in parts],
        in_specs=[ANY] * n, out_specs=[ANY] * n,
        scratch_shapes=[pltpu.SemaphoreType.DMA((n,)), pltpu.SemaphoreType.DMA((n,))],
    )(*parts)


def _row_block(rows, cols, budget=1 << 20):
    best = rows if rows % 8 else 8
    for tr in range(8, rows + 1, 8):
        if rows % tr == 0 and tr * cols * 4 <= budget:
            best = tr
    return best


def chip_sum(own, recv, me_s, buf, layer, layers, *, name):
    _, r, n = own.shape
    tr = _row_block(r, n)
    steps = r // tr

    def body(me_ref, own_ref, recv_ref, *rest):
        o_ref = rest[-1]
        me = me_ref[0]
        acc = jnp.zeros((tr, n), F32)
        for s in range(4):
            acc = acc + jnp.where(me == s, own_ref[0], recv_ref[s].astype(F32))
        o_ref[...] = acc

    in_specs = [pl.BlockSpec((1, tr, n), lambda i, me: (me[0], i, 0)), pl.BlockSpec((4, tr, n), lambda i, me: (0, i, 0))]
    args = [me_s, own, recv]
    aliases = {}
    if buf is not None:
        in_specs.append(ANY)
        args.append(buf)
        aliases = {3: 0}
    return pl.pallas_call(
        body, name=name, out_shape=jax.ShapeDtypeStruct((layers * r, n), F32),
        grid_spec=pltpu.PrefetchScalarGridSpec(
            num_scalar_prefetch=1, grid=(steps,), in_specs=in_specs,
            out_specs=pl.BlockSpec((tr, n), lambda i, me: (layer * steps + i, 0))),
        input_output_aliases=aliases,
        compiler_params=pltpu.CompilerParams(dimension_semantics=("parallel",)),
    )(*args)


def _adam_update(w, g, m, v):
    m2 = ADAM_B1 * m + (1.0 - ADAM_B1) * g
    v2 = ADAM_B2 * v + (1.0 - ADAM_B2) * (g * g)
    m_hat = m2 / (1.0 - ADAM_B1 ** ADAM_STEP)
    v_hat = v2 / (1.0 - ADAM_B2 ** ADAM_STEP)
    delta = -ADAM_LR * (m_hat / (jnp.sqrt(v_hat) + ADAM_EPS) + ADAM_WD * w)
    return delta, m2, v2


def adamw(w, g_parts, m, v, *, name):
    r, n = w.shape
    tr = _row_block(r, n)
    k = len(g_parts)

    def body(*refs):
        w_ref, m_ref, v_ref = refs[k], refs[k + 1], refs[k + 2]
        g_ref, d_ref, m2_ref, v2_ref = refs[k + 3:]
        g = refs[0][...]
        for p in refs[1:k]:
            g = g + p[...]
        g_ref[...] = g
        d_ref[...], m2_ref[...], v2_ref[...] = _adam_update(w_ref[...], g, m_ref[...], v_ref[...])

    blk = pl.BlockSpec((tr, n), lambda i: (i, 0))
    shp = jax.ShapeDtypeStruct((r, n), F32)
    return _call(body, name=name, grid=(r // tr,), in_specs=[blk] * (k + 3), out_specs=[blk] * 4,
                 out_shape=[shp] * 4, semantics=("parallel",))(*g_parts, w, m, v)


def adamw_gathered(g_all, w, m, v, *, name):
    _, r, n = g_all.shape
    tr = _row_block(r, n * 4)

    def body(ga_ref, w_ref, m_ref, v_ref, g_ref, d_ref, m2_ref, v2_ref):
        g = ga_ref[0]
        for dev in range(1, N_DEV):
            g = g + ga_ref[dev]
        g_ref[...] = g
        d_ref[...], m2_ref[...], v2_ref[...] = _adam_update(w_ref[...], g, m_ref[...], v_ref[...])

    blk = pl.BlockSpec((tr, n), lambda i: (i, 0))
    shp = jax.ShapeDtypeStruct((r, n), F32)
    return _call(body, name=name, grid=(r // tr,),
                 in_specs=[pl.BlockSpec((N_DEV, tr, n), lambda i: (0, i, 0)), blk, blk, blk], out_specs=[blk] * 4,
                 out_shape=[shp] * 4, semantics=("parallel",))(g_all, w, m, v)


BIG = ("w_in", "w_out", "w_ffn_in", "w_ffn_out")
SMALL = ("b_mod", "mix_norm_w", "ffn_norm_w", "dn_conv_w", "dn_a_log", "dn_dt_bias", "dn_out_norm_w", "gm_ln_g",
         "gm_ln_b", "gm_w_s", "gm_b_s", "sw_q_norm_w", "sw_k_norm_w")
WEIGHTS = ("w_mod", "b_mod", "mix_norm_w", "ffn_norm_w", "w_in", "w_out", "dn_conv_w", "dn_a_log", "dn_dt_bias",
           "dn_out_norm_w", "gm_ln_g", "gm_ln_b", "gm_w_s", "gm_b_s", "sw_q_norm_w", "sw_k_norm_w", "w_ffn_in",
           "w_ffn_out")
PACK_ROWS = 8


def _pack(arrs):
    out = []
    for a in arrs:
        flat = a.reshape(-1).astype(F32)
        rows = -(-flat.shape[0] // (LANE * PACK_ROWS)) * PACK_ROWS
        out.append(jnp.pad(flat, (0, rows * LANE - flat.shape[0])).reshape(rows, LANE))
    return jnp.concatenate(out, axis=0)


def _unpack(packed, shapes):
    out, r0 = [], 0
    for shp in shapes:
        size = math.prod(shp)
        rows = -(-size // (LANE * PACK_ROWS)) * PACK_ROWS
        out.append(packed[r0:r0 + rows].reshape(-1)[:size].reshape(shp))
        r0 += rows
    return out


def kernel(x, c, w_mod, b_mod, mix_norm_w, ffn_norm_w, w_in, w_out, dn_conv_w, dn_a_log, dn_dt_bias, dn_out_norm_w, gm_ln_g, gm_ln_b, gm_w_s, gm_b_s, sw_q_norm_w, sw_k_norm_w, w_ffn_in, w_ffn_out, loss_target, m_w_mod, m_b_mod, m_mix_norm_w, m_ffn_norm_w, m_w_in, m_w_out, m_dn_conv_w, m_dn_a_log, m_dn_dt_bias, m_dn_out_norm_w, m_gm_ln_g, m_gm_ln_b, m_gm_w_s, m_gm_b_s, m_sw_q_norm_w, m_sw_k_norm_w, m_w_ffn_in, m_w_ffn_out, v_w_mod, v_b_mod, v_mix_norm_w, v_ffn_norm_w, v_w_in, v_w_out, v_dn_conv_w, v_dn_a_log, v_dn_dt_bias, v_dn_out_norm_w, v_gm_ln_g, v_gm_ln_b, v_gm_w_s, v_gm_b_s, v_sw_q_norm_w, v_sw_k_norm_w, v_w_ffn_in, v_w_ffn_out):
    w = dict(w_mod=w_mod, b_mod=b_mod, mix_norm_w=mix_norm_w, ffn_norm_w=ffn_norm_w, w_in=w_in, w_out=w_out,
             dn_conv_w=dn_conv_w, dn_a_log=dn_a_log, dn_dt_bias=dn_dt_bias, dn_out_norm_w=dn_out_norm_w,
             gm_ln_g=gm_ln_g, gm_ln_b=gm_ln_b, gm_w_s=gm_w_s, gm_b_s=gm_b_s, sw_q_norm_w=sw_q_norm_w,
             sw_k_norm_w=sw_k_norm_w, w_ffn_in=w_ffn_in, w_ffn_out=w_ffn_out)
    m = dict(w_mod=m_w_mod, b_mod=m_b_mod, mix_norm_w=m_mix_norm_w, ffn_norm_w=m_ffn_norm_w, w_in=m_w_in,
             w_out=m_w_out, dn_conv_w=m_dn_conv_w, dn_a_log=m_dn_a_log, dn_dt_bias=m_dn_dt_bias,
             dn_out_norm_w=m_dn_out_norm_w, gm_ln_g=m_gm_ln_g, gm_ln_b=m_gm_ln_b, gm_w_s=m_gm_w_s, gm_b_s=m_gm_b_s,
             sw_q_norm_w=m_sw_q_norm_w, sw_k_norm_w=m_sw_k_norm_w, w_ffn_in=m_w_ffn_in, w_ffn_out=m_w_ffn_out)
    v = dict(w_mod=v_w_mod, b_mod=v_b_mod, mix_norm_w=v_mix_norm_w, ffn_norm_w=v_ffn_norm_w, w_in=v_w_in,
             w_out=v_w_out, dn_conv_w=v_dn_conv_w, dn_a_log=v_dn_a_log, dn_dt_bias=v_dn_dt_bias,
             dn_out_norm_w=v_dn_out_norm_w, gm_ln_g=v_gm_ln_g, gm_ln_b=v_gm_ln_b, gm_w_s=v_gm_w_s, gm_b_s=v_gm_b_s,
             sw_q_norm_w=v_sw_q_norm_w, sw_k_norm_w=v_sw_k_norm_w, w_ffn_in=v_w_ffn_in, w_ffn_out=v_w_ffn_out)
    layers, d, mod_n = w_mod.shape
    mx, my, mc = _place()
    me_s = 2 * mx + my
    me_dev = 4 * mx + 2 * my + mc

    c_all = allgather8(_pad_rows(c, 8), name="gather_c").reshape(N_DEV, 8, d)[:, 0]
    b_shard = lax.dynamic_slice_in_dim(b_mod, me_s * mod_n, mod_n, axis=1)[:, None, :]
    mod_part = mod_fwd(c_all, w_mod, b_shard)
    mod_parts = allgather8(mod_part.reshape(layers * 8, mod_n), name="gather_mod")
    mod_parts = mod_parts.reshape(4, 2, layers, 8, mod_n)[:, 0]
    mod_all = mod_parts.transpose(1, 2, 0, 3).reshape(layers, 8, 4 * mod_n)
    mods = lax.dynamic_index_in_dim(mod_all, me_dev, axis=1, keepdims=False)

    cw = dn_conv_w.shape[-1]
    conv_rows = -(-layers * DN_CONV // 8) * 8
    conv_parts = allgather8(_pad_rows(dn_conv_w.reshape(layers * DN_CONV, cw), conv_rows), name="gather_conv")
    conv_parts = conv_parts.reshape(4, 2, conv_rows, cw)[:, 0, :layers * DN_CONV]
    conv_full = conv_parts.reshape(4, layers, DN_CONV, cw).transpose(1, 2, 0, 3).reshape(layers, DN_CONV, 4 * cw)

    shards = {k: w[k].astype(BF16) for k in BIG}
    groups = dict(w_in=(0,), w_out=(1,), ffn=(2, 3))
    gathers = [exchange_start([shards[k][layer] for k in BIG], [mods, conv_full], sliced=False, name=f"gather_start{layer}")
               for layer in range(layers)]
    mods = mods + sum(g[4][0, 0] for g in gathers)

    def weights_of(layer, group, after):
        send_sems, recv_sems, srcs, zones, _ = gathers[layer]
        which = groups[group]
        got = exchange_wait(send_sems, recv_sems, [srcs[a] for a in which], [zones[a] for a in which], after,
                            which=which, sliced=False, name=f"gather_wait_{group}{layer}")
        full = {BIG[a]: lax.dynamic_update_index_in_dim(z, shards[BIG[a]][layer], me_s, 0) for a, z in zip(which, got)}
        cols = lambda g: jnp.concatenate([g[s] for s in range(4)], axis=-1)
        shape = dict(w_in=lambda g: _permute_w_in(cols(g)), w_out=lambda g: g.reshape(-1, d), w_ffn_in=cols,
                     w_ffn_out=lambda g: g.reshape(-1, d))
        return {k: shape[k](g) for k, g in full.items()}

    scatters = {}
    cut = lambda g, axis: jnp.stack(jnp.split(g, 4, axis=axis))
    shard_major = dict(w_in=lambda g: cut(_unpermute_w_in(g), 1), w_out=lambda g: cut(g, 0),
                       w_ffn_in=lambda g: cut(g, 1), w_ffn_out=lambda g: cut(g, 0))

    def grads_done(layer, group, grads):
        own = {k: shard_major[k](g) for k, g in grads.items()}
        started = exchange_start([g.astype(BF16) for g in own.values()], [], sliced=True,
                                 name=f"scatter_start_{group}{layer}")
        scatters[layer, group] = (started, own)
        return started[4][:1, :1]

    sp = {k: w[k] for k in SMALL}
    sp["dn_conv_w"] = conv_full
    loss_blk, grad_x, small, dmods = _local_step(x[0], loss_target[0], mods, weights_of, grads_done, sp)
    loss = lax.psum(loss_blk[0, 0], ("x", "y", "c"))

    me_arr = jnp.reshape(me_s, (1,)).astype(jnp.int32)
    partial = {k: None for k in BIG}
    for layer in range(layers):
        for group in ("ffn", "mix"):
            (send_sems, recv_sems, srcs, zones, _), own = scatters[layer, group]
            zones = exchange_wait(send_sems, recv_sems, srcs, zones, grad_x, which=tuple(range(len(srcs))),
                                  sliced=True, name=f"scatter_wait_{group}{layer}")
            for k, z in zip(own, zones):
                partial[k] = chip_sum(own[k], z, me_arr, partial[k], layer, layers, name=f"chip_sum_{k}{layer}")
    partial = [partial[k] for k in BIG]
    theirs = sibling_swap(partial)
    outs = {}
    for k, mine, other in zip(BIG, partial, theirs):
        shp = w[k].shape
        flat = lambda a: a.reshape(-1, shp[-1])
        res = adamw(flat(w[k]), [mine, other], flat(m[k]), flat(v[k]), name="adamw_" + k)
        outs[k] = [a.reshape(shp) for a in res]

    small = dict(small, b_mod=dmods)
    packed = _pack([small[k] for k in SMALL])
    rows = packed.shape[0]
    g_all = allgather8(packed, name="gather_small").reshape(N_DEV, rows, LANE)
    conv_zero = jnp.zeros((layers, DN_CONV, 3 * DN_WIDTH), F32)
    pk = lambda src: _pack([conv_zero if k == "dn_conv_w" else src[k] for k in SMALL])
    res = adamw_gathered(g_all, pk(w), pk(m), pk(v), name="adamw_small")
    shapes = [small[k].shape for k in SMALL]
    un = [_unpack(a, shapes) for a in res]
    for i, k in enumerate(SMALL):
        outs[k] = [un[j][i] for j in range(4)]
    g_conv = lax.dynamic_slice_in_dim(outs["dn_conv_w"][0], me_s * cw, cw, axis=2)
    flat = lambda a: a.reshape(-1, cw)
    res = adamw(flat(dn_conv_w), [flat(g_conv)], flat(m["dn_conv_w"]), flat(v["dn_conv_w"]), name="adamw_conv")
    outs["dn_conv_w"] = [a.reshape(dn_conv_w.shape) for a in res]

    b_rows = layers * 6 * d // LANE
    dmod_all = g_all[:, :b_rows].reshape(N_DEV, layers, 6 * d).transpose(1, 0, 2)
    dmod_shard = lax.dynamic_slice_in_dim(dmod_all, me_s * mod_n, mod_n, axis=2)
    g_wmod = mod_bwd(c_all, dmod_shard)
    flat = lambda a: a.reshape(-1, mod_n)
    res = adamw(flat(w_mod), [flat(g_wmod)], flat(m_w_mod), flat(v_w_mod), name="adamw_w_mod")
    outs["w_mod"] = [a.reshape(w_mod.shape) for a in res]

    result = [loss, grad_x[None]]
    for j in range(4):
        result += [outs[k][j] for k in WEIGHTS]
    return tuple(result)
```

```python
import functools
import math

import jax
import jax.numpy as jnp
from jax import lax
from jax.experimental import pallas as pl
from jax.experimental.pallas import tpu as pltpu

F32 = jnp.float32
BF16 = jnp.bfloat16
HI = lax.Precision.HIGH

NORM_EPS = 1e-6
DN_HEADS = 4
DN_HEAD_DIM = 128
DN_WIDTH = 512
DN_CHUNK = 64
DN_CONV = 4
GM_WIDTH = 256
GM_GROUPS = 4
GM_GROUP_DIM = 64
GM_CHUNK = 128
SW_HEADS = 4
SW_HEAD_DIM = 64
SW_WIDTH = 256
SW_DILATIONS = (1, 4, 16)
SW_BLOCK = 128
ROPE_THETA = 500000.0
ROPE_DIM = 16
LANE = 128

C_QKV = 0
C_Z = 1536
C_AB = 2048
C_SW = 2304
C_UV = 4608
IN_WIDTH = 4872
IN_PAD = 5120
AB_PAD = C_SW - C_AB
MIX_WIDTH = 1024

ADAM_LR = 0.001
ADAM_B1 = 0.9
ADAM_B2 = 0.999
ADAM_EPS = 1e-08
ADAM_WD = 0.01
ADAM_STEP = 10

MESH = pl.DeviceIdType.MESH


BIG_VMEM = 56 << 20


def _call(body, *, name, grid, in_specs, out_specs, out_shape, scratch_shapes=(), semantics=None, aliases=None,
          vmem=None):
    if semantics is None:
        semantics = ("arbitrary",) * len(grid)
    return pl.pallas_call(
        body, name=name, grid=grid, in_specs=in_specs, out_specs=out_specs, out_shape=out_shape,
        scratch_shapes=list(scratch_shapes), input_output_aliases=aliases or {},
        compiler_params=pltpu.CompilerParams(dimension_semantics=semantics, vmem_limit_bytes=vmem),
    )


def _dot(a, b, ca, cb, prec=None):
    if a.ndim == 3:
        dims = (((ca + 1,), (cb + 1,)), ((0,), (0,)))
    else:
        dims = (((ca,), (cb,)), ((), ()))
    return lax.dot_general(a, b, dims, preferred_element_type=F32, precision=prec)


def _bdot(a, b, ca=1, cb=0):
    return _dot(a.astype(BF16), b.astype(BF16), ca, cb)


def _hdot(a, b, ca=1, cb=0):
    return _dot(a.astype(F32), b.astype(F32), ca, cb, HI)


def _sigmoid(x):
    return 1.0 / (1.0 + jnp.exp(-x))


def _silu(x):
    return x * _sigmoid(x)


def _dsilu(x):
    s = _sigmoid(x)
    return s * (1.0 + x * (1.0 - s))


def _softplus(x):
    return jnp.maximum(x, 0.0) + jnp.log(1.0 + jnp.exp(-jnp.abs(x)))


def _iota2(shape, dim):
    return lax.broadcasted_iota(jnp.int32, shape, dim)


def _rowsum(x):
    return jnp.sum(x, axis=-1, keepdims=True)


def _colsum(x):
    return jnp.sum(x, axis=-2, keepdims=True)


def _full(shape):
    return pl.BlockSpec(shape, lambda *_: (0,) * len(shape))


def _resident(shape):
    return pl.BlockSpec(shape, lambda *_: (0,) * len(shape), pipeline_mode=pl.Buffered(1))


ANY = pl.BlockSpec(memory_space=pl.ANY)


def _norm_mod(x, nw, scale, shift):
    r = lax.rsqrt(jnp.mean(x * x, axis=-1, keepdims=True) + NORM_EPS)
    xn = x * r
    return xn, r, (xn * nw) * (1.0 + scale) + shift


def norm_mm(x, nw, scale, shift, w, *, swiglu, name, tm=512):
    t, d = x.shape
    n = w.shape[1]
    half = n // 2

    def body(x_ref, nw_ref, sc_ref, sh_ref, w_ref, h_ref, y_ref, *act_ref):
        _, _, h = _norm_mod(x_ref[...], nw_ref[...], sc_ref[...], sh_ref[...])
        hb = h.astype(BF16)
        h_ref[...] = hb
        y = _dot(hb, w_ref[...], 1, 0)
        y_ref[...] = y
        if swiglu:
            act_ref[0][...] = (_silu(y[:, :half]) * y[:, half:]).astype(BF16)

    row = lambda i: (i, 0)
    out_shape = [jax.ShapeDtypeStruct((t, d), BF16), jax.ShapeDtypeStruct((t, n), F32)]
    out_specs = [pl.BlockSpec((tm, d), row), pl.BlockSpec((tm, n), row)]
    if swiglu:
        out_shape.append(jax.ShapeDtypeStruct((t, half), BF16))
        out_specs.append(pl.BlockSpec((tm, half), row))
    return _call(
        body, name=name, grid=(t // tm,),
        in_specs=[pl.BlockSpec((tm, d), row), _full((1, d)), _full((1, d)), _full((1, d)), _resident((d, n))],
        out_specs=out_specs, out_shape=out_shape, semantics=("parallel",), vmem=BIG_VMEM,
    )(x, nw, scale, shift, w)


def resid_mm(y, w, x, gate, *, name, tm=512):
    t, k = y.shape
    d = w.shape[1]

    def body(y_ref, w_ref, x_ref, g_ref, xo_ref, o_ref):
        o = _dot(y_ref[...].astype(BF16), w_ref[...], 1, 0)
        o_ref[...] = o
        xo_ref[...] = x_ref[...] + g_ref[...] * o

    row = lambda i: (i, 0)
    return _call(
        body, name=name, grid=(t // tm,),
        in_specs=[pl.BlockSpec((tm, k), row), _resident((k, d)), pl.BlockSpec((tm, d), row), _full((1, d))],
        out_specs=[pl.BlockSpec((tm, d), row), pl.BlockSpec((tm, d), row)],
        out_shape=[jax.ShapeDtypeStruct((t, d), F32), jax.ShapeDtypeStruct((t, d), F32)],
        semantics=("parallel",), vmem=BIG_VMEM,
    )(y, w, x, gate)


def resid_mm_bwd(dx, gate, o, w, gu, *, name, tm):
    t, d = dx.shape
    k = w.shape[0]
    swiglu = gu is not None

    def body(dx_ref, g_ref, o_ref, w_ref, *rest):
        if swiglu:
            gu_ref, dy_ref, gx_ref, dg_ref = rest
        else:
            dy_ref, gx_ref, dg_ref = rest
        i = pl.program_id(0)
        dxv = dx_ref[...]
        gx = (dxv * g_ref[...]).astype(BF16)
        gx_ref[...] = gx
        part = _colsum(dxv * o_ref[...])

        @pl.when(i == 0)
        def _():
            dg_ref[...] = jnp.zeros_like(dg_ref)

        dg_ref[...] += part
        da = _dot(gx, w_ref[...], 1, 1)
        if swiglu:
            g = gu_ref[:, :k]
            u = gu_ref[:, k:]
            dy_ref[:, :k] = (da * u * _dsilu(g)).astype(BF16)
            dy_ref[:, k:] = (da * _silu(g)).astype(BF16)
        else:
            dy_ref[...] = da

    row = lambda i: (i, 0)
    in_specs = [pl.BlockSpec((tm, d), row), _full((1, d)), pl.BlockSpec((tm, d), row), _resident((k, d))]
    args = [dx, gate, o, w]
    if swiglu:
        in_specs.append(pl.BlockSpec((tm, 2 * k), row))
        args.append(gu)
        dy_shape = jax.ShapeDtypeStruct((t, 2 * k), BF16)
        dy_spec = pl.BlockSpec((tm, 2 * k), row)
    else:
        dy_shape = jax.ShapeDtypeStruct((t, k), F32)
        dy_spec = pl.BlockSpec((tm, k), row)
    return _call(
        body, name=name, grid=(t // tm,), in_specs=in_specs,
        out_specs=[dy_spec, pl.BlockSpec((tm, d), row), _full((1, d))],
        out_shape=[dy_shape, jax.ShapeDtypeStruct((t, d), BF16), jax.ShapeDtypeStruct((1, d), F32)], vmem=BIG_VMEM,
    )(*args)


def norm_mm_bwd(dy, w, x, nw, scale, dres, *, name, tm=512):
    t, n = dy.shape
    d = x.shape[1]
    steps = t // tm

    def body(dy_ref, w_ref, x_ref, nw_ref, sc_ref, dres_ref, dx_ref, dnw_ref, dsc_ref, dsh_ref):
        i = pl.program_id(0)
        dh = _dot(dy_ref[...].astype(BF16), w_ref[...], 1, 1)
        x = x_ref[...]
        r = lax.rsqrt(jnp.mean(x * x, axis=-1, keepdims=True) + NORM_EPS)
        xn = x * r
        a = nw_ref[...] * (1.0 + sc_ref[...])

        @pl.when(i == 0)
        def _():
            dnw_ref[...] = jnp.zeros_like(dnw_ref)
            dsh_ref[...] = jnp.zeros_like(dsh_ref)

        dnw_ref[...] += _colsum(dh * xn)
        dsh_ref[...] += _colsum(dh)
        dxn = dh * a
        dx_ref[...] = r * (dxn - xn * jnp.mean(dxn * xn, axis=-1, keepdims=True)) + dres_ref[...]

        @pl.when(i == steps - 1)
        def _():
            da = dnw_ref[...]
            dsc_ref[...] = da * nw_ref[...]
            dnw_ref[...] = da * (1.0 + sc_ref[...])

    row = lambda i: (i, 0)
    vec = jax.ShapeDtypeStruct((1, d), F32)
    return _call(
        body, name=name, grid=(steps,),
        in_specs=[pl.BlockSpec((tm, n), row), _resident((d, n)), pl.BlockSpec((tm, d), row), _full((1, d)),
                  _full((1, d)), pl.BlockSpec((tm, d), row)],
        out_specs=[pl.BlockSpec((tm, d), row), _full((1, d)), _full((1, d)), _full((1, d))],
        out_shape=[jax.ShapeDtypeStruct((t, d), F32), vec, vec, vec], vmem=BIG_VMEM,
    )(dy, w, x, nw, scale, dres)


def _pick_tn(n, k, budget=6 << 20):
    best = LANE
    for m in range(1, n // LANE + 1):
        tn = m * LANE
        if n % tn == 0 and k * tn * 4 <= budget:
            best = tn
    return best


def mm_tn(a, g, *, name, tt=512):
    t, k = a.shape
    n = g.shape[1]
    tn = _pick_tn(n, k)

    def body(a_ref, g_ref, o_ref):
        @pl.when(pl.program_id(1) == 0)
        def _():
            o_ref[...] = jnp.zeros_like(o_ref)

        o_ref[...] += _dot(a_ref[...].astype(BF16), g_ref[...].astype(BF16), 0, 0)

    return _call(
        body, name=name, grid=(n // tn, t // tt),
        in_specs=[pl.BlockSpec((tt, k), lambda j, i: (i, 0)), pl.BlockSpec((tt, tn), lambda j, i: (i, j))],
        out_specs=pl.BlockSpec((k, tn), lambda j, i: (0, j)),
        out_shape=jax.ShapeDtypeStruct((k, n), F32), semantics=("parallel", "arbitrary"),
    )(a, g)


def loss_head(y, target, *, tm=512):
    t, d = y.shape
    steps = t // tm

    def body(y_ref, t_ref, dy_ref, l_ref, acc_ref):
        i = pl.program_id(0)

        @pl.when(i == 0)
        def _():
            acc_ref[...] = jnp.zeros_like(acc_ref)

        e = y_ref[...] - t_ref[...]
        dy_ref[...] = e * (1.0 / d)
        acc_ref[...] += _colsum(e * e)

        @pl.when(i == steps - 1)
        def _():
            tot = jnp.sum(acc_ref[...], axis=-1, keepdims=True) * (0.5 / d)
            l_ref[...] = jnp.broadcast_to(tot, l_ref.shape)

    row = lambda i: (i, 0)
    return _call(
        body, name="loss_head", grid=(steps,),
        in_specs=[pl.BlockSpec((tm, d), row), pl.BlockSpec((tm, d), row)],
        out_specs=[pl.BlockSpec((tm, d), row), _full((8, LANE))],
        out_shape=[jax.ShapeDtypeStruct((t, d), F32), jax.ShapeDtypeStruct((8, LANE), F32)],
        scratch_shapes=[pltpu.VMEM((1, d), F32)],
    )(y, target)


def _shift_rows(x, s):
    if s == 0:
        return x
    t = x.shape[0]
    ri = _iota2(x.shape, 0)
    rolled = pltpu.roll(x, s % t, axis=0)
    if s > 0:
        return jnp.where(ri >= s, rolled, 0.0)
    return jnp.where(ri < t + s, rolled, 0.0)


def _conv_pre(x, w):
    acc = x * w[DN_CONV - 1:DN_CONV, :]
    for j in range(DN_CONV - 1):
        acc = acc + _shift_rows(x, DN_CONV - 1 - j) * w[j:j + 1, :]
    return acc


def dn_conv(proj, conv_w):
    t = proj.shape[0]
    width = 3 * DN_WIDTH

    def body(x_ref, w_ref, o_ref):
        o_ref[...] = _silu(_conv_pre(x_ref[...], w_ref[...]))

    col = lambda j: (0, j)
    return _call(
        body, name="dn_conv", grid=(width // LANE,),
        in_specs=[pl.BlockSpec((t, LANE), col), pl.BlockSpec((8, LANE), col)],
        out_specs=pl.BlockSpec((t, LANE), col),
        out_shape=jax.ShapeDtypeStruct((t, width), F32), semantics=("parallel",),
    )(proj, conv_w)


def dn_conv_bwd(proj, conv_w, dact, dproj):
    t = proj.shape[0]
    width = 3 * DN_WIDTH

    def body(x_ref, w_ref, d_ref, _, dx_ref, dw_ref):
        x = x_ref[...]
        w = w_ref[...]
        dc = d_ref[...] * _dsilu(_conv_pre(x, w))
        dx = dc * w[DN_CONV - 1:DN_CONV, :]
        rows = []
        for j in range(DN_CONV - 1):
            s = DN_CONV - 1 - j
            dx = dx + _shift_rows(dc, -s) * w[j:j + 1, :]
            rows.append(_colsum(dc * _shift_rows(x, s)))
        rows.append(_colsum(dc * x))
        dx_ref[...] = dx
        ri = _iota2((8, LANE), 0)
        dw = jnp.zeros((8, LANE), F32)
        for j in range(DN_CONV):
            dw = dw + jnp.where(ri == j, rows[j], 0.0)
        dw_ref[...] = dw

    col = lambda j: (0, j)
    return _call(
        body, name="dn_conv_bwd", grid=(width // LANE,),
        in_specs=[pl.BlockSpec((t, LANE), col), pl.BlockSpec((8, LANE), col), pl.BlockSpec((t, LANE), col), ANY],
        out_specs=[pl.BlockSpec((t, LANE), col), pl.BlockSpec((8, LANE), col)],
        out_shape=[jax.ShapeDtypeStruct(dproj.shape, F32), jax.ShapeDtypeStruct((8, width), F32)],
        semantics=("parallel",), aliases={3: 0},
    )(proj, conv_w, dact, dproj)


def _t(x):
    return jnp.swapaxes(x, -1, -2)


def _inv_unit_lower(a):
    c = a.shape[-1]
    eye = (_iota2((c, c), 0) == _iota2((c, c), 1)).astype(F32)
    x = eye - a
    p = _hdot(a, a)
    steps = int(math.log2(c)) - 1
    for i in range(steps):
        x = x + _hdot(x, p)
        if i < steps - 1:
            p = _hdot(p, p)
    return x


def _dn_chunk(q, k, v, a, b, alog, dtb, s_in):
    nh, c, d = q.shape
    rq = lax.rsqrt(_rowsum(q * q) + NORM_EPS)
    rk = lax.rsqrt(_rowsum(k * k) + NORM_EPS)
    qh = q * rq
    kn = k * rk
    qs = qh * (d ** -0.5)
    g = -jnp.exp(alog) * _softplus(a + dtb)
    beta = _sigmoid(b)
    ri = _iota2((c, c), 0)
    ci = _iota2((c, c), 1)
    causal = ri >= ci
    strict = ri > ci
    gb = jnp.broadcast_to(g, (nh, c, d))
    gcb = _hdot(jnp.broadcast_to(causal.astype(F32), (nh, c, c)), gb)
    gc = gcb[..., :1]
    gl = _colsum(gb)[..., :1]
    dec = jnp.exp(jnp.where(causal, gc - _t(gcb)[:, :c, :], -1e30))
    kb = kn * beta
    amat = jnp.where(strict, _bdot(kb, kn, 1, 1) * dec, 0.0)
    tinv = _inv_unit_lower(amat)
    e = jnp.exp(gc)
    f = jnp.exp(gl - gc)
    rw = kb * e
    sol = _hdot(tinv, jnp.concatenate([v * beta, rw], axis=-1))
    u = sol[..., :d]
    w = sol[..., d:]
    pmat = jnp.where(causal, _bdot(qs, kn, 1, 1) * dec, 0.0)
    qd = qs * e
    kd = kn * f
    vnew = u - _bdot(w, s_in)
    o = _bdot(qd, s_in) + _bdot(pmat, vnew)
    s_out = s_in * jnp.exp(gl) + _bdot(kd, vnew, 0, 0)
    return dict(rq=rq, rk=rk, qh=qh, kn=kn, qs=qs, g=g, beta=beta, causal=causal, strict=strict, gl=gl,
                dec=dec, kb=kb, amat=amat, tinv=tinv, e=e, f=f, rw=rw, u=u, w=w, pmat=pmat, qd=qd, kd=kd,
                vnew=vnew, o=o, s_out=s_out)


def _dn_chunk_bwd(m, q, v, a, alog, dtb, s_in, do, ds_out):
    nh, c, d = q.shape
    kn, qs, kb, u, w, e, f = m["kn"], m["qs"], m["kb"], m["u"], m["w"], m["e"], m["f"]
    beta, dec, tinv, vnew, kd, qd = m["beta"], m["dec"], m["tinv"], m["vnew"], m["kd"], m["qd"]
    el = jnp.exp(m["gl"])
    dvnew = _bdot(m["pmat"], do, 0, 0) + _bdot(kd, ds_out)
    dp = jnp.where(m["causal"], _bdot(do, vnew, 1, 1), 0.0)
    dqd = _bdot(do, s_in, 1, 1)
    dkd = _bdot(vnew, ds_out, 1, 1)
    ds_in = _bdot(qd, do, 0, 0) + el * ds_out - _bdot(w, dvnew, 0, 0)
    dgl = el * _colsum(_rowsum(s_in * ds_out))
    dw = -_bdot(dvnew, s_in, 1, 1)
    dsol = _hdot(tinv, jnp.concatenate([dvnew, dw], axis=-1), 0, 0)
    dru = dsol[..., :d]
    drw = dsol[..., d:]
    da_m = -jnp.where(m["strict"], _bdot(dsol, jnp.concatenate([u, w], axis=-1), 1, 1), 0.0)
    db_m = da_m * dec
    dq_m = dp * dec
    dkb = _bdot(db_m, kn)
    dkn = _bdot(db_m, kb, 0, 0) + _bdot(dq_m, qs, 0, 0)
    dqs = _bdot(dq_m, kn)
    gmat = da_m * m["amat"] + dp * m["pmat"]
    ones = jnp.ones((nh, c, d), F32)
    dgam = (_hdot(gmat, ones) - _hdot(gmat, ones, 0, 0))[..., :1]
    dqs = dqs + dqd * e
    dgam = dgam + _rowsum(dqd * qd)
    dkn = dkn + dkd * f
    tk = _rowsum(dkd * kd)
    dgam = dgam - tk
    dgl = dgl + _colsum(tk)
    dkb = dkb + drw * e
    dgam = dgam + _rowsum(drw * m["rw"])
    dv = dru * beta
    dbeta = _rowsum(dru * v) + _rowsum(dkb * kn)
    dkn = dkn + dkb * beta
    last = (_iota2((c, 1), 0) == c - 1).astype(F32)
    dgam = dgam + last * dgl
    upper = (_iota2((c, c), 0) <= _iota2((c, c), 1)).astype(F32)
    dg = _hdot(jnp.broadcast_to(upper, (nh, c, c)), jnp.broadcast_to(dgam, (nh, c, d)))[..., :1]
    dqh = dqs * (d ** -0.5)
    dq = m["rq"] * (dqh - m["qh"] * _rowsum(dqh * m["qh"]))
    dk = m["rk"] * (dkn - kn * _rowsum(dkn * kn))
    sg = _sigmoid(a + dtb)
    da = dg * (-jnp.exp(alog)) * sg
    dalog = _colsum(dg * m["g"])
    ddtb = _colsum(da)
    db = dbeta * beta * (1.0 - beta)
    return dq, dk, dv, da, db, dalog, ddtb, ds_in


def _dn_gate(o, z, wn):
    ro = lax.rsqrt(jnp.mean(o * o, axis=-1, keepdims=True) + NORM_EPS)
    n = o * ro
    return n, ro, n * wn * _silu(z)


def _heads(ref, col0):
    d = DN_HEAD_DIM
    return jnp.stack([ref[:, col0 + h * d:col0 + (h + 1) * d] for h in range(DN_HEADS)])


def _dn_inputs(act_ref, ab_ref, sc_ref):
    ab = ab_ref[...]
    sc = sc_ref[...]
    q = _heads(act_ref, 0)
    k = _heads(act_ref, DN_WIDTH)
    v = _heads(act_ref, 2 * DN_WIDTH)
    a = jnp.stack([ab[:, h:h + 1] for h in range(DN_HEADS)])
    b = jnp.stack([ab[:, DN_HEADS + h:DN_HEADS + h + 1] for h in range(DN_HEADS)])
    alog = jnp.stack([sc[0:1, h:h + 1] for h in range(DN_HEADS)])
    dtb = jnp.stack([sc[1:2, h:h + 1] for h in range(DN_HEADS)])
    return q, k, v, a, b, alog, dtb


def dn_fwd(act, proj, scal, wn):
    t = act.shape[0]
    n = t // DN_CHUNK
    d = DN_HEAD_DIM

    def body(act_ref, z_ref, ab_ref, sc_ref, wn_ref, y_ref, st_ref, s_ref):
        @pl.when(pl.program_id(0) == 0)
        def _():
            s_ref[...] = jnp.zeros_like(s_ref)

        q, k, v, a, b, alog, dtb = _dn_inputs(act_ref, ab_ref, sc_ref)
        s_in = s_ref[...]
        st_ref[0] = s_in
        m = _dn_chunk(q, k, v, a, b, alog, dtb, s_in)
        s_ref[...] = m["s_out"]
        y = _dn_gate(m["o"], _heads(z_ref, 0), wn_ref[...])[2]
        for h in range(DN_HEADS):
            y_ref[:, h * d:(h + 1) * d] = y[h]

    return _call(
        body, name="dn_fwd", grid=(n,),
        in_specs=[pl.BlockSpec((DN_CHUNK, 3 * DN_WIDTH), lambda i: (i, 0)),
                  pl.BlockSpec((DN_CHUNK, DN_WIDTH), lambda i: (i, C_Z // DN_WIDTH)),
                  pl.BlockSpec((DN_CHUNK, LANE), lambda i: (i, C_AB // LANE)),
                  _full((8, LANE)), _full((1, d))],
        out_specs=[pl.BlockSpec((DN_CHUNK, DN_WIDTH), lambda i: (i, 0)),
                   pl.BlockSpec((1, DN_HEADS, d, d), lambda i: (i, 0, 0, 0))],
        out_shape=[jax.ShapeDtypeStruct((t, MIX_WIDTH), F32), jax.ShapeDtypeStruct((n, DN_HEADS, d, d), F32)],
        scratch_shapes=[pltpu.VMEM((DN_HEADS, d, d), F32)],
    )(act, proj, proj, scal, wn)


def dn_bwd(act, proj, scal, wn, states, dy):
    t = act.shape[0]
    n = t // DN_CHUNK
    d = DN_HEAD_DIM
    zab = DN_WIDTH + AB_PAD

    def body(act_ref, z_ref, ab_ref, sc_ref, wn_ref, st_ref, dy_ref, dact_ref, dzab_ref, dpar_ref, ds_ref):
        @pl.when(pl.program_id(0) == 0)
        def _():
            ds_ref[...] = jnp.zeros_like(ds_ref)
            dpar_ref[...] = jnp.zeros_like(dpar_ref)

        wnv = wn_ref[...]
        q, k, v, a, b, alog, dtb = _dn_inputs(act_ref, ab_ref, sc_ref)
        s_in = st_ref[0]
        z = _heads(z_ref, 0)
        dyh = _heads(dy_ref, 0)
        m = _dn_chunk(q, k, v, a, b, alog, dtb, s_in)
        nrm, ro, _ = _dn_gate(m["o"], z, wnv)
        sz = _silu(z)
        dz = dyh * nrm * wnv * _dsilu(z)
        dn = dyh * wnv * sz
        dwn = _colsum(dyh * nrm * sz)
        do = ro * (dn - nrm * jnp.mean(dn * nrm, axis=-1, keepdims=True))
        dq, dk, dv, da, db, dalog, ddtb, ds_in = _dn_chunk_bwd(m, q, v, a, alog, dtb, s_in, do, ds_ref[...])
        ds_ref[...] = ds_in
        lane = _iota2((DN_CHUNK, LANE), 1)
        prow = _iota2((8, LANE), 0)
        plane = _iota2((8, LANE), 1)
        dab = jnp.zeros((DN_CHUNK, LANE), F32)
        dpar = jnp.zeros((8, LANE), F32)
        for h in range(DN_HEADS):
            dzab_ref[:, h * d:(h + 1) * d] = dz[h]
            dact_ref[:, h * d:(h + 1) * d] = dq[h]
            dact_ref[:, DN_WIDTH + h * d:DN_WIDTH + (h + 1) * d] = dk[h]
            dact_ref[:, 2 * DN_WIDTH + h * d:2 * DN_WIDTH + (h + 1) * d] = dv[h]
            dab = dab + jnp.where(lane == h, da[h], 0.0) + jnp.where(lane == DN_HEADS + h, db[h], 0.0)
            dpar = dpar + jnp.where((prow == 0) & (plane == h), dalog[h], 0.0)
            dpar = dpar + jnp.where((prow == 1) & (plane == h), ddtb[h], 0.0)
            dpar = dpar + jnp.where(prow == 2, dwn[h], 0.0)
        dzab_ref[:, DN_WIDTH:DN_WIDTH + LANE] = dab
        dzab_ref[:, DN_WIDTH + LANE:] = jnp.zeros((DN_CHUNK, AB_PAD - LANE), F32)
        dpar_ref[...] += dpar

    rev = lambda i: (n - 1 - i, 0)
    return _call(
        body, name="dn_bwd", grid=(n,),
        in_specs=[pl.BlockSpec((DN_CHUNK, 3 * DN_WIDTH), rev),
                  pl.BlockSpec((DN_CHUNK, DN_WIDTH), lambda i: (n - 1 - i, C_Z // DN_WIDTH)),
                  pl.BlockSpec((DN_CHUNK, LANE), lambda i: (n - 1 - i, C_AB // LANE)),
                  _full((8, LANE)), _full((1, d)),
                  pl.BlockSpec((1, DN_HEADS, d, d), lambda i: (n - 1 - i, 0, 0, 0)),
                  pl.BlockSpec((DN_CHUNK, DN_WIDTH), rev)],
        out_specs=[pl.BlockSpec((DN_CHUNK, 3 * DN_WIDTH), rev),
                   pl.BlockSpec((DN_CHUNK, zab), lambda i: (n - 1 - i, C_Z // zab)), _full((8, LANE))],
        out_shape=[jax.ShapeDtypeStruct((t, 3 * DN_WIDTH), F32), jax.ShapeDtypeStruct((t, IN_PAD), F32),
                   jax.ShapeDtypeStruct((8, LANE), F32)],
        scratch_shapes=[pltpu.VMEM((DN_HEADS, d, d), F32)],
    )(act, proj, proj, scal, wn, states, dy)


_INV_SQRT2 = 0.7071067811865476
_INV_SQRT2PI = 0.3989422804014327


def _gelu(x):
    return 0.5 * x * (1.0 + lax.erf(x * _INV_SQRT2))


def _dgelu(x):
    return 0.5 * (1.0 + lax.erf(x * _INV_SQRT2)) + x * jnp.exp(-0.5 * x * x) * _INV_SQRT2PI


def _gm_core(uv, lng, lnb, ws_ref, bst):
    c = uv.shape[0]
    zz = _gelu(uv)
    u = zz[:, :GM_WIDTH]
    vv = zz[:, GM_WIDTH:]
    xc = vv - jnp.mean(vv, axis=-1, keepdims=True)
    rs = lax.rsqrt(jnp.mean(xc * xc, axis=-1, keepdims=True) + NORM_EPS)
    xh = xc * rs
    vn = xh * lng + lnb
    grp = _iota2((c, GM_WIDTH), 1) // GM_GROUP_DIM
    tril = _iota2((c, c), 0) >= _iota2((c, c), 1)
    sv = jnp.zeros((c, GM_WIDTH), F32)
    masks = []
    for g in range(GM_GROUPS):
        mk = grp == g
        masks.append(mk)
        ws = jnp.where(tril, ws_ref[g], 0.0)
        sv = sv + _bdot(ws, jnp.where(mk, vn, 0.0)) + jnp.where(mk, bst[:, g:g + 1], 0.0)
    return u, xh, rs, vn, sv, masks, tril


def gm_fwd(proj, lng, lnb, w_s, bst, ybuf):
    t = proj.shape[0]

    def body(uv_ref, g_ref, b_ref, ws_ref, bst_ref, _, y_ref):
        u, _, _, _, sv, _, _ = _gm_core(uv_ref[...], g_ref[...], b_ref[...], ws_ref, bst_ref[...])
        y_ref[...] = u * sv

    return _call(
        body, name="gm_fwd", grid=(t // GM_CHUNK,),
        in_specs=[pl.BlockSpec((GM_CHUNK, 2 * GM_WIDTH), lambda i: (i, C_UV // (2 * GM_WIDTH))),
                  _full((1, GM_WIDTH)), _full((1, GM_WIDTH)), _full((GM_GROUPS, GM_CHUNK, GM_CHUNK)),
                  _full((GM_CHUNK, LANE)), ANY],
        out_specs=pl.BlockSpec((GM_CHUNK, GM_WIDTH), lambda i: (i, DN_WIDTH // GM_WIDTH)),
        out_shape=jax.ShapeDtypeStruct(ybuf.shape, F32), semantics=("parallel",), aliases={5: 0},
    )(proj, lng, lnb, w_s, bst, ybuf)


def gm_bwd(proj, lng, lnb, w_s, bst, dy, dproj):
    t = proj.shape[0]

    def body(uv_ref, g_ref, b_ref, ws_ref, bst_ref, dy_ref, _, duv_ref, dws_ref, dbst_ref, dln_ref):
        @pl.when(pl.program_id(0) == 0)
        def _():
            dws_ref[...] = jnp.zeros_like(dws_ref)
            dbst_ref[...] = jnp.zeros_like(dbst_ref)
            dln_ref[...] = jnp.zeros_like(dln_ref)

        uv = uv_ref[...]
        lng = g_ref[...]
        u, xh, rs, vn, sv, masks, tril = _gm_core(uv, lng, b_ref[...], ws_ref, bst_ref[...])
        dyv = dy_ref[...]
        dsv = dyv * u
        lane = _iota2((GM_CHUNK, LANE), 1)
        dvn = jnp.zeros_like(dsv)
        dbst = jnp.zeros((GM_CHUNK, LANE), F32)
        for g in range(GM_GROUPS):
            ws = jnp.where(tril, ws_ref[g], 0.0)
            dsg = jnp.where(masks[g], dsv, 0.0)
            dvn = dvn + jnp.where(masks[g], _bdot(ws, dsv, 0, 0), 0.0)
            dws_ref[g] += jnp.where(tril, _bdot(dsg, vn, 1, 1), 0.0)
            dbst = dbst + jnp.where(lane == g, _rowsum(dsg), 0.0)
        dbst_ref[...] += dbst
        row = _iota2((8, GM_WIDTH), 0)
        dln_ref[...] += jnp.where(row == 0, _colsum(dvn * xh), 0.0) + jnp.where(row == 1, _colsum(dvn), 0.0)
        dxh = dvn * lng
        dvv = rs * (dxh - jnp.mean(dxh, axis=-1, keepdims=True) - xh * jnp.mean(dxh * xh, axis=-1, keepdims=True))
        dg = _dgelu(uv)
        duv_ref[:, :GM_WIDTH] = dyv * sv * dg[:, :GM_WIDTH]
        duv_ref[:, GM_WIDTH:] = dvv * dg[:, GM_WIDTH:]

    return _call(
        body, name="gm_bwd", grid=(t // GM_CHUNK,),
        in_specs=[pl.BlockSpec((GM_CHUNK, 2 * GM_WIDTH), lambda i: (i, C_UV // (2 * GM_WIDTH))),
                  _full((1, GM_WIDTH)), _full((1, GM_WIDTH)), _full((GM_GROUPS, GM_CHUNK, GM_CHUNK)),
                  _full((GM_CHUNK, LANE)),
                  pl.BlockSpec((GM_CHUNK, GM_WIDTH), lambda i: (i, DN_WIDTH // GM_WIDTH)), ANY],
        out_specs=[pl.BlockSpec((GM_CHUNK, 2 * GM_WIDTH), lambda i: (i, C_UV // (2 * GM_WIDTH))),
                   _full((GM_GROUPS, GM_CHUNK, GM_CHUNK)), _full((GM_CHUNK, LANE)), _full((8, GM_WIDTH))],
        out_shape=[jax.ShapeDtypeStruct(dproj.shape, F32),
                   jax.ShapeDtypeStruct((GM_GROUPS, GM_CHUNK, GM_CHUNK), F32),
                   jax.ShapeDtypeStruct((GM_CHUNK, LANE), F32), jax.ShapeDtypeStruct((8, GM_WIDTH), F32)],
        aliases={6: 0},
    )(proj, lng, lnb, w_s, bst, dy, dproj)


def _head_mats():
    r = _iota2((SW_WIDTH, SW_WIDTH), 0)
    c = _iota2((SW_WIDTH, SW_WIDTH), 1)
    same = (r // SW_HEAD_DIM) == (c // SW_HEAD_DIM)
    cc = c % SW_HEAD_DIM
    half = ROPE_DIM // 2
    rot = jnp.where((cc < half) & (r == c + half), -1.0, 0.0) + jnp.where((cc >= half) & (cc < ROPE_DIM) & (r == c - half), 1.0, 0.0)
    return same.astype(F32), rot


def _seg_col(s):
    return C_SW // SW_WIDTH + (s // 2) * 3 + s % 2


def _halves(x):
    return x[:, :LANE], x[:, LANE:]


def sw_prep(proj, nw2, cos_t, sin_t, *, tm=512):
    t = proj.shape[0]

    def body(x_ref, w_ref, c_ref, s_ref, o_ref):
        same, rot = _head_mats()
        x = x_ref[...]
        r = lax.rsqrt(_hdot(x * x, same) * (1.0 / SW_HEAD_DIM) + NORM_EPS)
        xn = x * r * w_ref[0]
        o_ref[0, 0], o_ref[0, 1] = _halves(xn * c_ref[...] + _hdot(xn, rot) * s_ref[...])

    return _call(
        body, name="sw_prep", grid=(6, t // tm),
        in_specs=[pl.BlockSpec((tm, SW_WIDTH), lambda s, i: (i, _seg_col(s))),
                  pl.BlockSpec((1, 1, SW_WIDTH), lambda s, i: (s % 2, 0, 0)),
                  pl.BlockSpec((tm, SW_WIDTH), lambda s, i: (i, 0)),
                  pl.BlockSpec((tm, SW_WIDTH), lambda s, i: (i, 0))],
        out_specs=pl.BlockSpec((1, 2, tm, LANE), lambda s, i: (s, 0, i, 0)),
        out_shape=jax.ShapeDtypeStruct((6, 2, t, LANE), F32), semantics=("parallel", "parallel"),
    )(proj, nw2, cos_t, sin_t)


def sw_prep_bwd(proj, nw2, cos_t, sin_t, dkvq, dproj, dnw, p, *, tm=512):
    t = proj.shape[0]
    col0 = C_SW // SW_WIDTH + 3 * p
    seg_col = lambda s: col0 + (s + 1) % 3

    def body(x_ref, w_ref, c_ref, s_ref, d_ref, _, dw0_ref, dx_ref, dw_ref):
        s = pl.program_id(0)
        dout = jnp.concatenate([d_ref[0, 0], d_ref[0, 1]], axis=1)

        @pl.when(s == 1)
        def _():
            dx_ref[...] = dout

        @pl.when((s != 1) & (pl.program_id(1) == 0))
        def _():
            dw_ref[...] = dw0_ref[...]

        @pl.when(s != 1)
        def _():
            same, rot = _head_mats()
            x = x_ref[...]
            w = w_ref[0]
            r = lax.rsqrt(_hdot(x * x, same) * (1.0 / SW_HEAD_DIM) + NORM_EPS)
            xh = x * r
            dxn = dout * c_ref[...] + _hdot(dout * s_ref[...], rot, 1, 1)
            dw_ref[0] += _colsum(dxn * xh)
            dxh = dxn * w
            dx_ref[...] = r * (dxh - xh * (_hdot(dxh * xh, same) * (1.0 / SW_HEAD_DIM)))

    return _call(
        body, name=f"sw_prep_bwd{p}", grid=(3, t // tm),
        in_specs=[pl.BlockSpec((tm, SW_WIDTH), lambda s, i: (i, seg_col(s))),
                  pl.BlockSpec((1, 1, SW_WIDTH), lambda s, i: (1 - s // 2, 0, 0)),
                  pl.BlockSpec((tm, SW_WIDTH), lambda s, i: (i, 0)),
                  pl.BlockSpec((tm, SW_WIDTH), lambda s, i: (i, 0)),
                  pl.BlockSpec((1, 2, tm, LANE), lambda s, i: (s, 0, i, 0)), ANY,
                  pl.BlockSpec((1, 1, SW_WIDTH), lambda s, i: (s // 2, 0, 0))],
        out_specs=[pl.BlockSpec((tm, SW_WIDTH), lambda s, i: (i, seg_col(s))),
                   pl.BlockSpec((1, 1, SW_WIDTH), lambda s, i: (s // 2, 0, 0))],
        out_shape=[jax.ShapeDtypeStruct(dproj.shape, F32), jax.ShapeDtypeStruct((2, 1, SW_WIDTH), F32)],
        semantics=("arbitrary", "arbitrary"), aliases={5: 0},
    )(proj, nw2, cos_t, sin_t, dkvq, dproj, dnw)


_SW_SCALE = SW_HEAD_DIM ** -0.5
_NEG = -1e30


def _sw_masks(has_other):
    ri = _iota2((SW_BLOCK, SW_BLOCK), 0)
    ci = _iota2((SW_BLOCK, SW_BLOCK), 1)
    return ri >= ci, (ci >= ri) & has_other


def _pair(x):
    first = _iota2((1, LANE), 1) < SW_HEAD_DIM
    return jnp.stack([jnp.where(first, x, 0.0), jnp.where(first, 0.0, x)])


def _both(x):
    return jnp.broadcast_to(x.astype(BF16)[None], (2,) + x.shape)


def _unpair(x2):
    first = _iota2((1, LANE), 1) < SW_HEAD_DIM
    return jnp.where(first, x2[0], x2[1])


def _head_cols(x):
    return jnp.stack([x[:, 0:1], x[:, SW_HEAD_DIM:SW_HEAD_DIM + 1]])


SW_GROUP = 4


def _sw_geometry(t, p):
    dil = SW_DILATIONS[p]
    unit = SW_BLOCK * dil
    nb = max(1, SW_GROUP // dil)
    return dil, unit, nb, t // (unit * nb)


def _sw_groups(dil, nb, body):
    if nb * dil == SW_GROUP:
        body([(k // dil, k % dil) for k in range(SW_GROUP)])
    else:
        def step(g, carry):
            body([(0, SW_GROUP * g + k) for k in range(SW_GROUP)])
            return carry

        lax.fori_loop(0, nb * dil // SW_GROUP, step, 0)


def _sw_rows(i, r, dil):
    start = i * SW_BLOCK * dil + r
    return pl.ds(start, SW_BLOCK) if dil == 1 else pl.ds(start, SW_BLOCK, stride=dil)


def _sw_load(refs, probs, dil, shift, wrap, fn):
    out = []
    for i, r in probs:
        if shift != 0 and i == wrap:
            out.append(fn(refs[1][_sw_rows(0, r, dil), :]))
        else:
            out.append(fn(refs[0][_sw_rows(i + shift, r, dil), :]))
    return jnp.concatenate(out, axis=0)


def _sw_other_masks(probs, wrap, edge_ok):
    _, other = _sw_masks(edge_ok)
    _, always = _sw_masks(True)
    return jnp.stack([other if i == wrap else always for i, _ in probs for _ in range(2)])


def sw_attn(qk, proj, p):
    t = proj.shape[0]
    dil, unit, nb, nsp = _sw_geometry(t, p)
    vcol = (C_SW + 3 * SW_WIDTH * p + 2 * SW_WIDTH) // LANE

    def body(q_ref, kc_ref, kp_ref, vc_ref, vp_ref, o_ref, l_ref):
        mc, _ = _sw_masks(True)
        first = pl.program_id(1) != 0
        q_r, k_r, v_r = (q_ref.at[0, 0], None), (kc_ref.at[0, 0], kp_ref.at[0, 0]), (vc_ref, vp_ref)

        def one(probs):
            mp = _sw_other_masks(probs, 0, first)
            q2 = _sw_load(q_r, probs, dil, 0, 0, _pair)
            sc = jnp.where(mc, _bdot(q2, _sw_load(k_r, probs, dil, 0, 0, _both), 1, 1) * _SW_SCALE, _NEG)
            sp = jnp.where(mp, _bdot(q2, _sw_load(k_r, probs, dil, -1, 0, _both), 1, 1) * _SW_SCALE, _NEG)
            mx = jnp.maximum(jnp.max(sc, axis=-1, keepdims=True), jnp.max(sp, axis=-1, keepdims=True))
            pc = jnp.exp(sc - mx)
            pp = jnp.exp(sp - mx)
            den = _rowsum(pc) + _rowsum(pp)
            o2 = (_bdot(pc, _sw_load(v_r, probs, dil, 0, 0, _both))
                  + _bdot(pp, _sw_load(v_r, probs, dil, -1, 0, _both))) / den
            l2 = jnp.broadcast_to(mx + jnp.log(den), o2.shape)
            for n, (i, r) in enumerate(probs):
                o_ref.at[0][_sw_rows(i, r, dil), :] = _unpair(o2[2 * n:2 * n + 2])
                l_ref.at[0][_sw_rows(i, r, dil), :] = _unpair(l2[2 * n:2 * n + 2])

        _sw_groups(dil, nb, one)

    before = lambda j: jnp.maximum(j * nb - 1, 0)
    seg = lambda s: pl.BlockSpec((1, 1, unit * nb, LANE), lambda h, j: (s, h, j, 0))
    seg_b = lambda s: pl.BlockSpec((1, 1, unit, LANE), lambda h, j: (s, h, before(j), 0))
    out = pl.BlockSpec((1, unit * nb, LANE), lambda h, j: (h, j, 0))
    shp = jax.ShapeDtypeStruct((2, t, LANE), F32)
    return _call(
        body, name=f"sw_attn{p}", grid=(2, nsp),
        in_specs=[seg(2 * p), seg(2 * p + 1), seg_b(2 * p + 1),
                  pl.BlockSpec((unit * nb, LANE), lambda h, j: (j, vcol + h)),
                  pl.BlockSpec((unit, LANE), lambda h, j: (before(j), vcol + h))],
        out_specs=[out, out], out_shape=[shp, shp], semantics=("parallel", "parallel"),
    )(qk, qk, qk, proj, proj)


def sw_attn_dkv(qk, proj, dy, lg, dm, p):
    t = proj.shape[0]
    dil, unit, nb, nsp = _sw_geometry(t, p)
    nunits = t // unit
    vcol = (C_SW + 3 * SW_WIDTH * p + 2 * SW_WIDTH) // LANE
    ycol = (DN_WIDTH + GM_WIDTH) // LANE

    def body(k_ref, v_ref, qc_ref, qn_ref, doc_ref, don_ref, lc_ref, ln_ref, dc_ref, dn_ref, o_ref):
        mc, _ = _sw_masks(True)
        more = pl.program_id(1) + 1 < nsp
        q_r, do_r = (qc_ref.at[0, 0], qn_ref.at[0, 0]), (doc_ref, don_ref)
        l_r, d_r = (lc_ref.at[0], ln_ref.at[0]), (dc_ref.at[0], dn_ref.at[0])

        def one(probs):
            k2 = _sw_load((k_ref.at[0, 0], None), probs, dil, 0, 0, _both)
            v2 = _sw_load((v_ref, None), probs, dil, 0, 0, _both)
            dk = jnp.zeros((2 * SW_GROUP, SW_BLOCK, LANE), F32)
            dv = jnp.zeros((2 * SW_GROUP, SW_BLOCK, LANE), F32)
            for shift, mk in ((0, mc), (1, _sw_other_masks(probs, nb - 1, more))):
                q2 = _sw_load(q_r, probs, dil, shift, nb - 1, _pair)
                do2 = _sw_load(do_r, probs, dil, shift, nb - 1, _pair)
                lse = _sw_load(l_r, probs, dil, shift, nb - 1, _head_cols)
                dd = _sw_load(d_r, probs, dil, shift, nb - 1, _head_cols)
                pr = jnp.exp(jnp.where(mk, _bdot(q2, k2, 1, 1) * _SW_SCALE, _NEG) - lse)
                dv = dv + _bdot(pr, do2, 0, 0)
                ds = pr * (_bdot(do2, v2, 1, 1) - dd)
                dk = dk + _bdot(ds, q2, 0, 0)
            for n, (i, r) in enumerate(probs):
                o_ref.at[0, 0][_sw_rows(i, r, dil), :] = (dk[2 * n] + dk[2 * n + 1]) * _SW_SCALE
                o_ref.at[1, 0][_sw_rows(i, r, dil), :] = dv[2 * n] + dv[2 * n + 1]

        _sw_groups(dil, nb, one)

    after = lambda j: jnp.minimum((j + 1) * nb, nunits - 1)
    seg = lambda s: pl.BlockSpec((1, 1, unit * nb, LANE), lambda h, j: (s, h, j, 0))
    seg_a = lambda s: pl.BlockSpec((1, 1, unit, LANE), lambda h, j: (s, h, after(j), 0))
    col = lambda c0: pl.BlockSpec((unit * nb, LANE), lambda h, j: (j, c0 + h))
    col_a = lambda c0: pl.BlockSpec((unit, LANE), lambda h, j: (after(j), c0 + h))
    hp = pl.BlockSpec((1, unit * nb, LANE), lambda h, j: (h, j, 0))
    hp_a = pl.BlockSpec((1, unit, LANE), lambda h, j: (h, after(j), 0))
    return _call(
        body, name=f"sw_dkv{p}", grid=(2, nsp),
        in_specs=[seg(2 * p + 1), col(vcol), seg(2 * p), seg_a(2 * p), col(ycol), col_a(ycol), hp, hp_a, hp, hp_a],
        out_specs=pl.BlockSpec((2, 1, unit * nb, LANE), lambda h, j: (0, h, j, 0)),
        out_shape=jax.ShapeDtypeStruct((3, 2, t, LANE), F32), semantics=("parallel", "parallel"),
    )(qk, proj, qk, qk, dy, dy, lg, lg, dm, dm)


def sw_attn_dq(qk, proj, dy, lg, dm, dkvq, p):
    t = proj.shape[0]
    dil, unit, nb, nsp = _sw_geometry(t, p)
    vcol = (C_SW + 3 * SW_WIDTH * p + 2 * SW_WIDTH) // LANE
    ycol = (DN_WIDTH + GM_WIDTH) // LANE

    def body(q_ref, kc_ref, kp_ref, vc_ref, vp_ref, do_ref, l_ref, d_ref, _, dq_ref):
        mc, _ = _sw_masks(True)
        first = pl.program_id(1) != 0
        k_r, v_r = (kc_ref.at[0, 0], kp_ref.at[0, 0]), (vc_ref, vp_ref)

        def one(probs):
            mp = _sw_other_masks(probs, 0, first)
            q2 = _sw_load((q_ref.at[0, 0], None), probs, dil, 0, 0, _pair)
            do2 = _sw_load((do_ref, None), probs, dil, 0, 0, _pair)
            lse = _sw_load((l_ref.at[0], None), probs, dil, 0, 0, _head_cols)
            dd = _sw_load((d_ref.at[0], None), probs, dil, 0, 0, _head_cols)
            kc = _sw_load(k_r, probs, dil, 0, 0, _both)
            kp = _sw_load(k_r, probs, dil, -1, 0, _both)
            pc = jnp.exp(jnp.where(mc, _bdot(q2, kc, 1, 1) * _SW_SCALE, _NEG) - lse)
            pp = jnp.exp(jnp.where(mp, _bdot(q2, kp, 1, 1) * _SW_SCALE, _NEG) - lse)
            dsc = pc * (_bdot(do2, _sw_load(v_r, probs, dil, 0, 0, _both), 1, 1) - dd)
            dsp = pp * (_bdot(do2, _sw_load(v_r, probs, dil, -1, 0, _both), 1, 1) - dd)
            dq2 = (_bdot(dsc, kc) + _bdot(dsp, kp)) * _SW_SCALE
            for n, (i, r) in enumerate(probs):
                dq_ref.at[0, 0][_sw_rows(i, r, dil), :] = _unpair(dq2[2 * n:2 * n + 2])

        _sw_groups(dil, nb, one)

    before = lambda j: jnp.maximum(j * nb - 1, 0)
    seg = lambda s: pl.BlockSpec((1, 1, unit * nb, LANE), lambda h, j: (s, h, j, 0))
    seg_b = lambda s: pl.BlockSpec((1, 1, unit, LANE), lambda h, j: (s, h, before(j), 0))
    col = lambda c0: pl.BlockSpec((unit * nb, LANE), lambda h, j: (j, c0 + h))
    col_b = lambda c0: pl.BlockSpec((unit, LANE), lambda h, j: (before(j), c0 + h))
    hp = pl.BlockSpec((1, unit * nb, LANE), lambda h, j: (h, j, 0))
    return _call(
        body, name=f"sw_dq{p}", grid=(2, nsp),
        in_specs=[seg(2 * p), seg(2 * p + 1), seg_b(2 * p + 1), col(vcol), col_b(vcol), col(ycol), hp, hp, ANY],
        out_specs=pl.BlockSpec((1, 1, unit * nb, LANE), lambda h, j: (2, h, j, 0)),
        out_shape=jax.ShapeDtypeStruct(dkvq.shape, F32), semantics=("parallel", "parallel"), aliases={8: 0},
    )(qk, qk, qk, proj, proj, dy, lg, dm, dkvq)


def sw_merge(outs, lses, ybuf, *, tm=512):
    t = ybuf.shape[0]

    def body(o0, o1, o2, l0_ref, l1_ref, l2_ref, _, y_ref, lg_ref):
        l0, l1, l2 = l0_ref[...], l1_ref[...], l2_ref[...]
        mx = jnp.maximum(jnp.maximum(l0, l1), l2)
        lg = mx + jnp.log(jnp.exp(l0 - mx) + jnp.exp(l1 - mx) + jnp.exp(l2 - mx))
        lg_ref[...] = lg
        y = jnp.exp(l0 - lg) * o0[...] + jnp.exp(l1 - lg) * o1[...] + jnp.exp(l2 - lg) * o2[...]
        y_ref[...] = jnp.concatenate([y[0], y[1]], axis=1)

    hp = pl.BlockSpec((2, tm, LANE), lambda i: (0, i, 0))
    return _call(
        body, name="sw_merge", grid=(t // tm,), in_specs=[hp] * 6 + [ANY],
        out_specs=[pl.BlockSpec((tm, SW_WIDTH), lambda i: (i, (DN_WIDTH + GM_WIDTH) // SW_WIDTH)), hp],
        out_shape=[jax.ShapeDtypeStruct(ybuf.shape, F32), jax.ShapeDtypeStruct((2, t, LANE), F32)],
        semantics=("parallel",), aliases={6: 0},
    )(*outs, *lses, ybuf)


def sw_delta(dy, ybuf, *, tm=512):
    t = ybuf.shape[0]

    def body(dy_ref, y_ref, o_ref):
        same, _ = _head_mats()
        o_ref[0], o_ref[1] = _halves(_hdot(dy_ref[...] * y_ref[...], same))

    b1 = pl.BlockSpec((tm, SW_WIDTH), lambda i: (i, (DN_WIDTH + GM_WIDTH) // SW_WIDTH))
    return _call(body, name="sw_delta", grid=(t // tm,), in_specs=[b1, b1],
                 out_specs=pl.BlockSpec((2, tm, LANE), lambda i: (0, i, 0)),
                 out_shape=jax.ShapeDtypeStruct((2, t, LANE), F32), semantics=("parallel",))(dy, ybuf)


def _rope_tables(t):
    inv = ROPE_THETA ** (-jnp.arange(0, ROPE_DIM, 2, dtype=F32) / ROPE_DIM)
    ang = jnp.arange(t, dtype=F32)[:, None] * inv[None, :]
    pad1 = jnp.ones((t, SW_HEAD_DIM - ROPE_DIM), F32)
    pad0 = jnp.zeros((t, SW_HEAD_DIM - ROPE_DIM), F32)
    cos_h = jnp.concatenate([jnp.cos(ang), jnp.cos(ang), pad1], axis=1)
    sin_h = jnp.concatenate([jnp.sin(ang), jnp.sin(ang), pad0], axis=1)
    return jnp.tile(cos_h, (1, SW_HEADS)), jnp.tile(sin_h, (1, SW_HEADS))


def sw_forward(proj, nw2, cos_t, sin_t, ybuf):
    qk = sw_prep(proj, nw2, cos_t, sin_t)
    outs, lses = [], []
    for p in range(len(SW_DILATIONS)):
        o, lse = sw_attn(qk, proj, p)
        outs.append(o)
        lses.append(lse)
    ybuf, lg = sw_merge(outs, lses, ybuf)
    return ybuf, (qk, lg)


def sw_backward(proj, nw2, cos_t, sin_t, res, ybuf, dy, dproj):
    qk, lg = res
    dm = sw_delta(dy, ybuf)
    dnw = jnp.zeros((2, 1, SW_WIDTH), F32)
    for p in range(len(SW_DILATIONS)):
        dkvq = sw_attn_dkv(qk, proj, dy, lg, dm, p)
        dkvq = sw_attn_dq(qk, proj, dy, lg, dm, dkvq, p)
        dproj, dnw = sw_prep_bwd(proj, nw2, cos_t, sin_t, dkvq, dproj, dnw, p)
    return dproj, dnw[::-1, 0]


def _pad_rows(a, rows):
    return jnp.zeros((rows,) + a.shape[1:], a.dtype).at[:a.shape[0]].set(a)


def _consts(sp):
    d = {}
    d["mix_nw"] = sp["mix_norm_w"][:, None, :]
    d["ffn_nw"] = sp["ffn_norm_w"][:, None, :]
    d["cw8"] = jnp.pad(sp["dn_conv_w"], ((0, 0), (0, 8 - DN_CONV), (0, 0)))
    d["scal"] = jnp.pad(jnp.stack([sp["dn_a_log"], sp["dn_dt_bias"]], axis=1), ((0, 0), (0, 6), (0, LANE - DN_HEADS)))
    d["wn"] = sp["dn_out_norm_w"][:, None, :]
    d["lng"] = sp["gm_ln_g"][:, None, :]
    d["lnb"] = sp["gm_ln_b"][:, None, :]
    d["w_s"] = sp["gm_w_s"]
    d["bst"] = jnp.pad(jnp.swapaxes(sp["gm_b_s"], 1, 2), ((0, 0), (0, 0), (0, LANE - GM_GROUPS)))
    d["nw2"] = jnp.stack([jnp.tile(sp["sw_q_norm_w"], (1, SW_HEADS)),
                          jnp.tile(sp["sw_k_norm_w"], (1, SW_HEADS))], axis=1)[:, :, None, :]
    return d


def _layer_fwd(x, mod, get_w, cs, tabs):
    wb = dict(get_w("w_in", x))
    h1, proj = norm_mm(x, cs["mix_nw"], mod[1], mod[0], wb["w_in"], swiglu=False, name="in_proj")
    act = dn_conv(proj, cs["cw8"])
    y, states = dn_fwd(act, proj, cs["scal"], cs["wn"])
    y = gm_fwd(proj, cs["lng"], cs["lnb"], cs["w_s"], cs["bst"], y)
    y, swres = sw_forward(proj, cs["nw2"], *tabs, y)
    wb.update(get_w("w_out", y))
    x1, o1 = resid_mm(y, wb["w_out"], x, mod[2], name="out_proj")
    wb.update(get_w("ffn", x1))
    h2, gu, actf = norm_mm(x1, cs["ffn_nw"], mod[4], mod[3], wb["w_ffn_in"], swiglu=True, name="ffn_in")
    x2, o2 = resid_mm(actf, wb["w_ffn_out"], x1, mod[5], name="ffn_out")
    res = dict(x=x, h1=h1, proj=proj, act=act, states=states, swres=swres, y=y, x1=x1, o1=o1, h2=h2, gu=gu,
               actf=actf, o2=o2)
    return x2, res, wb


def _layer_bwd(dx2, res, mod, wb, cs, tabs, grads_done):
    dgu, gx2, dgate2 = resid_mm_bwd(dx2, mod[5], res["o2"], wb["w_ffn_out"], res["gu"], name="ffn_out_bwd", tm=256)
    g_wfo = mm_tn(res["actf"], gx2, name="wg_ffn_out")
    g_wfi = mm_tn(res["h2"], dgu, name="wg_ffn_in")
    token = grads_done("ffn", dict(w_ffn_in=g_wfi, w_ffn_out=g_wfo))
    dx1, d_ffn_nw, dscale2, dshift2 = norm_mm_bwd(dgu, wb["w_ffn_in"], res["x1"], cs["ffn_nw"], mod[4] + token, dx2,
                                                  name="ffn_in_bwd")
    dy, gx1, dgate1 = resid_mm_bwd(dx1, mod[2], res["o1"], wb["w_out"], None, name="out_proj_bwd", tm=512)
    g_wout = mm_tn(res["y"], gx1, name="wg_out")
    proj = res["proj"]
    dact, dproj, dpar = dn_bwd(res["act"], proj, cs["scal"], cs["wn"], res["states"], dy)
    dproj, dcw = dn_conv_bwd(proj, cs["cw8"], dact, dproj)
    dproj, dws, dbst, dln = gm_bwd(proj, cs["lng"], cs["lnb"], cs["w_s"], cs["bst"], dy, dproj)
    dproj, dnw = sw_backward(proj, cs["nw2"], *tabs, res["swres"], res["y"], dy, dproj)
    g_win = mm_tn(res["h1"], dproj, name="wg_in")
    dx, d_mix_nw, dscale1, dshift1 = norm_mm_bwd(dproj, wb["w_in"], res["x"], cs["mix_nw"], mod[1], dx1,
                                                 name="in_proj_bwd")
    dmod = jnp.concatenate([dshift1, dscale1, dgate1, dshift2, dscale2, dgate2], axis=1)
    dnw = dnw.reshape(2, SW_HEADS, SW_HEAD_DIM).sum(1)
    small = dict(mix_norm_w=d_mix_nw[0], ffn_norm_w=d_ffn_nw[0], dn_conv_w=dcw[:DN_CONV],
                 dn_a_log=dpar[0, :DN_HEADS], dn_dt_bias=dpar[1, :DN_HEADS], dn_out_norm_w=dpar[2],
                 gm_ln_g=dln[0], gm_ln_b=dln[1], gm_w_s=dws, gm_b_s=dbst[:, :GM_GROUPS].T,
                 sw_q_norm_w=dnw[0], sw_k_norm_w=dnw[1])
    token = grads_done("mix", dict(w_in=g_win, w_out=g_wout))
    return dx, small, dmod, token


def _permute_w_in(w):
    pad = jnp.zeros(w.shape[:-1] + (AB_PAD - 8,), w.dtype)
    return jnp.concatenate([w[..., 0:2056], pad, w[..., 2568:IN_WIDTH], w[..., 2056:2568]], axis=-1)


def _unpermute_w_in(g):
    return jnp.concatenate([g[..., 0:2056], g[..., C_UV:IN_PAD], g[..., C_SW:C_UV]], axis=-1)


def _local_step(x, target, mods, weights_of, grads_done, sp):
    layers = mods.shape[0]
    t, d = x.shape
    tabs = _rope_tables(t)
    consts = _consts(sp)
    saved = []
    for layer in range(layers):
        mod = mods[layer].reshape(6, 1, d)
        cs = {k: v[layer] for k, v in consts.items()}
        x, res, wb = _layer_fwd(x, mod, functools.partial(weights_of, layer), cs, tabs)
        saved.append((res, mod, wb, cs))
    dx, loss = loss_head(x, target)
    smalls, dmods = [], []
    token = jnp.zeros((1, 1), F32)
    for layer in reversed(range(layers)):
        res, mod, wb, cs = saved[layer]
        dx, small, dmod, token = _layer_bwd(dx, res, mod + token, wb, cs, tabs, functools.partial(grads_done, layer))
        smalls.append(small)
        dmods.append(dmod[0])
    smalls, dmods = smalls[::-1], dmods[::-1]
    small = {k: jnp.stack([s[k] for s in smalls]) for k in smalls[0]}
    return loss, dx, small, jnp.stack(dmods)


def mod_fwd(c_all, w_mod, b_shard):
    layers, d, n = w_mod.shape

    def body(c_ref, w_ref, b_ref, o_ref):
        ca = _silu(c_ref[...]).astype(BF16)
        o_ref[0] = _dot(ca, w_ref[0].astype(BF16), 1, 0) + b_ref[0]

    return _call(
        body, name="mod_fwd", grid=(layers,),
        in_specs=[_full((8, d)), pl.BlockSpec((1, d, n), lambda i: (i, 0, 0)),
                  pl.BlockSpec((1, 1, n), lambda i: (i, 0, 0))],
        out_specs=pl.BlockSpec((1, 8, n), lambda i: (i, 0, 0)),
        out_shape=jax.ShapeDtypeStruct((layers, 8, n), F32), semantics=("parallel",),
    )(c_all, w_mod, b_shard)


def mod_bwd(c_all, dmod):
    layers, _, n = dmod.shape
    d = c_all.shape[1]

    def body(c_ref, g_ref, o_ref):
        ca = _silu(c_ref[...]).astype(BF16)
        o_ref[0] = _dot(ca, g_ref[0].astype(BF16), 0, 0)

    return _call(
        body, name="mod_bwd", grid=(layers,),
        in_specs=[_full((8, d)), pl.BlockSpec((1, 8, n), lambda i: (i, 0, 0))],
        out_specs=pl.BlockSpec((1, d, n), lambda i: (i, 0, 0)),
        out_shape=jax.ShapeDtypeStruct((layers, d, n), F32), semantics=("parallel",),
    )(c_all, dmod)


N_DEV = 8


def _place():
    return lax.axis_index("x"), lax.axis_index("y"), lax.axis_index("c")


def _other_chips(x, y):
    return [(1 - x, y), (x, 1 - y), (1 - x, 1 - y)]


def allgather8(x_shard, *, name):
    m_per, n = x_shard.shape

    def body(x_ref, out_ref, send_sems, recv_sems, local_sem):
        x, y, c = _place()
        me, sibling = (x, y, c), (x, y, 1 - c)
        chips = _other_chips(x, y)

        def rows(px, py, pc):
            return out_ref.at[pl.ds((4 * px + 2 * py + pc) * m_per, m_per), :]

        def copy(k, block, to, src=None):
            return pltpu.make_async_remote_copy(
                src_ref=rows(*block) if src is None else src, dst_ref=rows(*block),
                send_sem=send_sems.at[k], recv_sem=recv_sems.at[k], device_id=to, device_id_type=MESH)

        mine = pltpu.make_async_copy(x_ref, rows(*me), local_sem)
        mine.start()
        first = [copy(0, me, sibling, src=x_ref)]
        first += [copy(1 + j, me, (*chip, c), src=x_ref) for j, chip in enumerate(chips)]
        for cp in first:
            cp.start()
        passed = [copy(4 + j, (*chip, c), sibling) for j, chip in enumerate(chips)]
        for j, chip in enumerate(chips):
            copy(1 + j, (*chip, c), me).wait_recv()
            passed[j].start()
        copy(0, sibling, me).wait_recv()
        for j, chip in enumerate(chips):
            copy(4 + j, (*chip, 1 - c), me).wait_recv()
        for cp in first + passed:
            cp.wait_send()
        mine.wait()

    return pl.pallas_call(
        body, name=name, out_shape=jax.ShapeDtypeStruct((N_DEV * m_per, n), x_shard.dtype),
        in_specs=[pl.BlockSpec(memory_space=pltpu.VMEM)], out_specs=pl.BlockSpec(memory_space=pltpu.VMEM),
        scratch_shapes=[pltpu.SemaphoreType.DMA((7,)), pltpu.SemaphoreType.DMA((7,)), pltpu.SemaphoreType.DMA],
    )(x_shard)


HBM = pl.BlockSpec(memory_space=pltpu.HBM)
SEM = pl.BlockSpec(memory_space=pltpu.SEMAPHORE)
_EFFECT = pltpu.SideEffectType.DATAFLOW_SIDE_EFFECTING


def _piece(ref, sliced, chip):
    return ref.at[2 * chip[0] + chip[1]] if sliced else ref


def exchange_start(srcs, after, *, sliced, name):
    n = len(srcs)
    piece = lambda s: s.shape[1:] if sliced else s.shape

    def body(*refs):
        ins, lands = refs[:n], refs[n:2 * n]
        send_sems, recv_sems = refs[2 * n + len(after):2 * n + len(after) + 2]
        token = refs[-1]
        x, y, c = _place()
        me_s = 2 * x + y
        for a in range(n):
            for j, chip in enumerate(_other_chips(x, y)):
                pltpu.make_async_remote_copy(
                    src_ref=_piece(ins[a], sliced, chip), dst_ref=lands[a].at[me_s], send_sem=send_sems.at[3 * a + j],
                    recv_sem=recv_sems.at[3 * a + j], device_id=(*chip, c), device_id_type=MESH).start()
        token[...] = jnp.zeros_like(token)

    zones = [pltpu.with_memory_space_constraint(lax.empty((4,) + piece(s), s.dtype), pltpu.HBM) for s in srcs]
    srcs = [pltpu.with_memory_space_constraint(s, pltpu.HBM) for s in srcs]
    out = pl.pallas_call(
        body, name=name,
        out_shape=(pltpu.SemaphoreType.DMA((3 * n,)), pltpu.SemaphoreType.DMA((3 * n,)),
                   *[pltpu.HBM(s.shape, s.dtype) for s in srcs], *[pltpu.HBM(z.shape, z.dtype) for z in zones],
                   jax.ShapeDtypeStruct((8, LANE), F32)),
        in_specs=[HBM] * (2 * n) + [ANY] * len(after),
        out_specs=(SEM, SEM, *[HBM] * (2 * n), pl.BlockSpec(memory_space=pltpu.VMEM)),
        input_output_aliases={i: 2 + i for i in range(2 * n)},
        compiler_params=pltpu.CompilerParams(has_side_effects=_EFFECT),
    )(*srcs, *zones, *after)
    return out[0], out[1], out[2:2 + n], out[2 + n:2 + 2 * n], out[-1]


def exchange_wait(send_sems, recv_sems, srcs, zones, after, *, which, sliced, name):
    n = len(srcs)

    def body(*refs):
        ins, lands = refs[:n], refs[n:2 * n]
        send_sems, recv_sems = refs[2 * n:2 * n + 2]
        x, y, c = _place()
        for a in range(n):
            for j, chip in enumerate(_other_chips(x, y)):
                copy = pltpu.make_async_remote_copy(
                    src_ref=_piece(ins[a], sliced, chip), dst_ref=lands[a].at[2 * chip[0] + chip[1]],
                    send_sem=send_sems.at[3 * which[a] + j], recv_sem=recv_sems.at[3 * which[a] + j],
                    device_id=(*chip, c), device_id_type=MESH)
                copy.wait_send()
                copy.wait_recv()

    out = pl.pallas_call(
        body, name=name,
        out_shape=tuple(pltpu.HBM(s.shape, s.dtype) for s in (*srcs, *zones)),
        in_specs=[HBM] * (2 * n) + [SEM, SEM, ANY], out_specs=tuple([HBM] * (2 * n)),
        input_output_aliases={i: i for i in range(2 * n)},
        compiler_params=pltpu.CompilerParams(has_side_effects=_EFFECT),
    )(*srcs, *zones, send_sems, recv_sems, after)
    return out[n:]


def sibling_swap(parts):
    n = len(parts)

    def body(*refs):
        ins, outs = refs[:n], refs[n:2 * n]
        send_sems, recv_sems = refs[2 * n:]
        x, y, c = _place()
        cps = []
        for a in range(n):
            cp = pltpu.make_async_remote_copy(
                src_ref=ins[a], dst_ref=outs[a], send_sem=send_sems.at[a], recv_sem=recv_sems.at[a],
                device_id=(x, y, 1 - c), device_id_type=MESH)
            cp.start()
            cps.append(cp)
        for cp in cps:
            cp.wait()

    return pl.pallas_call(
        body, name="sibling_swap", out_shape=[jax.ShapeDtypeStruct(p.shape, p.dtype) for p in parts],
        in_specs=[ANY] * n, out_specs=[ANY] * n,
        scratch_shapes=[pltpu.SemaphoreType.DMA((n,)), pltpu.SemaphoreType.DMA((n,))],
    )(*parts)


def _row_block(rows, cols, budget=1 << 20):
    best = rows if rows % 8 else 8
    for tr in range(8, rows + 1, 8):
        if rows % tr == 0 and tr * cols * 4 <= budget:
            best = tr
    return best


def chip_sum(own, recv, me_s, buf, layer, layers, *, name):
    _, r, n = own.shape
    tr = _row_block(r, n)
    steps = r // tr

    def body(me_ref, own_ref, recv_ref, *rest):
        o_ref = rest[-1]
        me = me_ref[0]
        acc = jnp.zeros((tr, n), F32)
        for s in range(4):
            acc = acc + jnp.where(me == s, own_ref[0], recv_ref[s].astype(F32))
        o_ref[...] = acc

    in_specs = [pl.BlockSpec((1, tr, n), lambda i, me: (me[0], i, 0)), pl.BlockSpec((4, tr, n), lambda i, me: (0, i, 0))]
    args = [me_s, own, recv]
    aliases = {}
    if buf is not None:
        in_specs.append(ANY)
        args.append(buf)
        aliases = {3: 0}
    return pl.pallas_call(
        body, name=name, out_shape=jax.ShapeDtypeStruct((layers * r, n), F32),
        grid_spec=pltpu.PrefetchScalarGridSpec(
            num_scalar_prefetch=1, grid=(steps,), in_specs=in_specs,
            out_specs=pl.BlockSpec((tr, n), lambda i, me: (layer * steps + i, 0))),
        input_output_aliases=aliases,
        compiler_params=pltpu.CompilerParams(dimension_semantics=("parallel",)),
    )(*args)


def _adam_update(w, g, m, v):
    m2 = ADAM_B1 * m + (1.0 - ADAM_B1) * g
    v2 = ADAM_B2 * v + (1.0 - ADAM_B2) * (g * g)
    m_hat = m2 / (1.0 - ADAM_B1 ** ADAM_STEP)
    v_hat = v2 / (1.0 - ADAM_B2 ** ADAM_STEP)
    delta = -ADAM_LR * (m_hat / (jnp.sqrt(v_hat) + ADAM_EPS) + ADAM_WD * w)
    return delta, m2, v2


def adamw(w, g_parts, m, v, *, name):
    r, n = w.shape
    tr = _row_block(r, n)
    k = len(g_parts)

    def body(*refs):
        w_ref, m_ref, v_ref = refs[k], refs[k + 1], refs[k + 2]
        g_ref, d_ref, m2_ref, v2_ref = refs[k + 3:]
        g = refs[0][...]
        for p in refs[1:k]:
            g = g + p[...]
        g_ref[...] = g
        d_ref[...], m2_ref[...], v2_ref[...] = _adam_update(w_ref[...], g, m_ref[...], v_ref[...])

    blk = pl.BlockSpec((tr, n), lambda i: (i, 0))
    shp = jax.ShapeDtypeStruct((r, n), F32)
    return _call(body, name=name, grid=(r // tr,), in_specs=[blk] * (k + 3), out_specs=[blk] * 4,
                 out_shape=[shp] * 4, semantics=("parallel",))(*g_parts, w, m, v)


def adamw_gathered(g_all, w, m, v, *, name):
    _, r, n = g_all.shape
    tr = _row_block(r, n * 4)

    def body(ga_ref, w_ref, m_ref, v_ref, g_ref, d_ref, m2_ref, v2_ref):
        g = ga_ref[0]
        for dev in range(1, N_DEV):
            g = g + ga_ref[dev]
        g_ref[...] = g
        d_ref[...], m2_ref[...], v2_ref[...] = _adam_update(w_ref[...], g, m_ref[...], v_ref[...])

    blk = pl.BlockSpec((tr, n), lambda i: (i, 0))
    shp = jax.ShapeDtypeStruct((r, n), F32)
    return _call(body, name=name, grid=(r // tr,),
                 in_specs=[pl.BlockSpec((N_DEV, tr, n), lambda i: (0, i, 0)), blk, blk, blk], out_specs=[blk] * 4,
                 out_shape=[shp] * 4, semantics=("parallel",))(g_all, w, m, v)


BIG = ("w_in", "w_out", "w_ffn_in", "w_ffn_out")
SMALL = ("b_mod", "mix_norm_w", "ffn_norm_w", "dn_conv_w", "dn_a_log", "dn_dt_bias", "dn_out_norm_w", "gm_ln_g",
         "gm_ln_b", "gm_w_s", "gm_b_s", "sw_q_norm_w", "sw_k_norm_w")
WEIGHTS = ("w_mod", "b_mod", "mix_norm_w", "ffn_norm_w", "w_in", "w_out", "dn_conv_w", "dn_a_log", "dn_dt_bias",
           "dn_out_norm_w", "gm_ln_g", "gm_ln_b", "gm_w_s", "gm_b_s", "sw_q_norm_w", "sw_k_norm_w", "w_ffn_in",
           "w_ffn_out")
PACK_ROWS = 8


def _pack(arrs):
    out = []
    for a in arrs:
        flat = a.reshape(-1).astype(F32)
        rows = -(-flat.shape[0] // (LANE * PACK_ROWS)) * PACK_ROWS
        out.append(jnp.pad(flat, (0, rows * LANE - flat.shape[0])).reshape(rows, LANE))
    return jnp.concatenate(out, axis=0)


def _unpack(packed, shapes):
    out, r0 = [], 0
    for shp in shapes:
        size = math.prod(shp)
        rows = -(-size // (LANE * PACK_ROWS)) * PACK_ROWS
        out.append(packed[r0:r0 + rows].reshape(-1)[:size].reshape(shp))
        r0 += rows
    return out


def kernel(x, c, w_mod, b_mod, mix_norm_w, ffn_norm_w, w_in, w_out, dn_conv_w, dn_a_log, dn_dt_bias, dn_out_norm_w, gm_ln_g, gm_ln_b, gm_w_s, gm_b_s, sw_q_norm_w, sw_k_norm_w, w_ffn_in, w_ffn_out, loss_target, m_w_mod, m_b_mod, m_mix_norm_w, m_ffn_norm_w, m_w_in, m_w_out, m_dn_conv_w, m_dn_a_log, m_dn_dt_bias, m_dn_out_norm_w, m_gm_ln_g, m_gm_ln_b, m_gm_w_s, m_gm_b_s, m_sw_q_norm_w, m_sw_k_norm_w, m_w_ffn_in, m_w_ffn_out, v_w_mod, v_b_mod, v_mix_norm_w, v_ffn_norm_w, v_w_in, v_w_out, v_dn_conv_w, v_dn_a_log, v_dn_dt_bias, v_dn_out_norm_w, v_gm_ln_g, v_gm_ln_b, v_gm_w_s, v_gm_b_s, v_sw_q_norm_w, v_sw_k_norm_w, v_w_ffn_in, v_w_ffn_out):
    w = dict(w_mod=w_mod, b_mod=b_mod, mix_norm_w=mix_norm_w, ffn_norm_w=ffn_norm_w, w_in=w_in, w_out=w_out,
             dn_conv_w=dn_conv_w, dn_a_log=dn_a_log, dn_dt_bias=dn_dt_bias, dn_out_norm_w=dn_out_norm_w,
             gm_ln_g=gm_ln_g, gm_ln_b=gm_ln_b, gm_w_s=gm_w_s, gm_b_s=gm_b_s, sw_q_norm_w=sw_q_norm_w,
             sw_k_norm_w=sw_k_norm_w, w_ffn_in=w_ffn_in, w_ffn_out=w_ffn_out)
    m = dict(w_mod=m_w_mod, b_mod=m_b_mod, mix_norm_w=m_mix_norm_w, ffn_norm_w=m_ffn_norm_w, w_in=m_w_in,
             w_out=m_w_out, dn_conv_w=m_dn_conv_w, dn_a_log=m_dn_a_log, dn_dt_bias=m_dn_dt_bias,
             dn_out_norm_w=m_dn_out_norm_w, gm_ln_g=m_gm_ln_g, gm_ln_b=m_gm_ln_b, gm_w_s=m_gm_w_s, gm_b_s=m_gm_b_s,
             sw_q_norm_w=m_sw_q_norm_w, sw_k_norm_w=m_sw_k_norm_w, w_ffn_in=m_w_ffn_in, w_ffn_out=m_w_ffn_out)
    v = dict(w_mod=v_w_mod, b_mod=v_b_mod, mix_norm_w=v_mix_norm_w, ffn_norm_w=v_ffn_norm_w, w_in=v_w_in,
             w_out=v_w_out, dn_conv_w=v_dn_conv_w, dn_a_log=v_dn_a_log, dn_dt_bias=v_dn_dt_bias,
             dn_out_norm_w=v_dn_out_norm_w, gm_ln_g=v_gm_ln_g, gm_ln_b=v_gm_ln_b, gm_w_s=v_gm_w_s, gm_b_s=v_gm_b_s,
             sw_q_norm_w=v_sw_q_norm_w, sw_k_norm_w=v_sw_k_norm_w, w_ffn_in=v_w_ffn_in, w_ffn_out=v_w_ffn_out)
    layers, d, mod_n = w_mod.shape
    mx, my, mc = _place()
    me_s = 2 * mx + my
    me_dev = 4 * mx + 2 * my + mc

    c_all = allgather8(_pad_rows(c, 8), name="gather_c").reshape(N_DEV, 8, d)[:, 0]
    b_shard = lax.dynamic_slice_in_dim(b_mod, me_s * mod_n, mod_n, axis=1)[:, None, :]
    mod_part = mod_fwd(c_all, w_mod, b_shard)
    mod_parts = allgather8(mod_part.reshape(layers * 8, mod_n), name="gather_mod")
    mod_parts = mod_parts.reshape(4, 2, layers, 8, mod_n)[:, 0]
    mod_all = mod_parts.transpose(1, 2, 0, 3).reshape(layers, 8, 4 * mod_n)
    mods = lax.dynamic_index_in_dim(mod_all, me_dev, axis=1, keepdims=False)

    cw = dn_conv_w.shape[-1]
    conv_rows = -(-layers * DN_CONV // 8) * 8
    conv_parts = allgather8(_pad_rows(dn_conv_w.reshape(layers * DN_CONV, cw), conv_rows), name="gather_conv")
    conv_parts = conv_parts.reshape(4, 2, conv_rows, cw)[:, 0, :layers * DN_CONV]
    conv_full = conv_parts.reshape(4, layers, DN_CONV, cw).transpose(1, 2, 0, 3).reshape(layers, DN_CONV, 4 * cw)

    shards = {k: w[k].astype(BF16) for k in BIG}
    groups = dict(w_in=(0,), w_out=(1,), ffn=(2, 3))
    gathers = [exchange_start([shards[k][layer] for k in BIG], [mods, conv_full], sliced=False, name=f"gather_start{layer}")
               for layer in range(layers)]
    mods = mods + sum(g[4][0, 0] for g in gathers)

    def weights_of(layer, group, after):
        send_sems, recv_sems, srcs, zones, _ = gathers[layer]
        which = groups[group]
        got = exchange_wait(send_sems, recv_sems, [srcs[a] for a in which], [zones[a] for a in which], after,
                            which=which, sliced=False, name=f"gather_wait_{group}{layer}")
        full = {BIG[a]: lax.dynamic_update_index_in_dim(z, shards[BIG[a]][layer], me_s, 0) for a, z in zip(which, got)}
        cols = lambda g: jnp.concatenate([g[s] for s in range(4)], axis=-1)
        shape = dict(w_in=lambda g: _permute_w_in(cols(g)), w_out=lambda g: g.reshape(-1, d), w_ffn_in=cols,
                     w_ffn_out=lambda g: g.reshape(-1, d))
        return {k: shape[k](g) for k, g in full.items()}

    scatters = {}
    cut = lambda g, axis: jnp.stack(jnp.split(g, 4, axis=axis))
    shard_major = dict(w_in=lambda g: cut(_unpermute_w_in(g), 1), w_out=lambda g: cut(g, 0),
                       w_ffn_in=lambda g: cut(g, 1), w_ffn_out=lambda g: cut(g, 0))

    def grads_done(layer, group, grads):
        own = {k: shard_major[k](g) for k, g in grads.items()}
        started = exchange_start([g.astype(BF16) for g in own.values()], [], sliced=True,
                                 name=f"scatter_start_{group}{layer}")
        scatters[layer, group] = (started, own)
        return started[4][:1, :1]

    sp = {k: w[k] for k in SMALL}
    sp["dn_conv_w"] = conv_full
    loss_blk, grad_x, small, dmods = _local_step(x[0], loss_target[0], mods, weights_of, grads_done, sp)
    loss = lax.psum(loss_blk[0, 0], ("x", "y", "c"))

    me_arr = jnp.reshape(me_s, (1,)).astype(jnp.int32)
    partial = {k: None for k in BIG}
    for layer in range(layers):
        for group in ("ffn", "mix"):
            (send_sems, recv_sems, srcs, zones, _), own = scatters[layer, group]
            zones = exchange_wait(send_sems, recv_sems, srcs, zones, grad_x, which=tuple(range(len(srcs))),
                                  sliced=True, name=f"scatter_wait_{group}{layer}")
            for k, z in zip(own, zones):
                partial[k] = chip_sum(own[k], z, me_arr, partial[k], layer, layers, name=f"chip_sum_{k}{layer}")
    partial = [partial[k] for k in BIG]
    theirs = sibling_swap(partial)
    outs = {}
    for k, mine, other in zip(BIG, partial, theirs):
        shp = w[k].shape
        flat = lambda a: a.reshape(-1, shp[-1])
        res = adamw(flat(w[k]), [mine, other], flat(m[k]), flat(v[k]), name="adamw_" + k)
        outs[k] = [a.reshape(shp) for a in res]

    small = dict(small, b_mod=dmods)
    packed = _pack([small[k] for k in SMALL])
    rows = packed.shape[0]
    g_all = allgather8(packed, name="gather_small").reshape(N_DEV, rows, LANE)
    conv_zero = jnp.zeros((layers, DN_CONV, 3 * DN_WIDTH), F32)
    pk = lambda src: _pack([conv_zero if k == "dn_conv_w" else src[k] for k in SMALL])
    res = adamw_gathered(g_all, pk(w), pk(m), pk(v), name="adamw_small")
    shapes = [small[k].shape for k in SMALL]
    un = [_unpack(a, shapes) for a in res]
    for i, k in enumerate(SMALL):
        outs[k] = [un[j][i] for j in range(4)]
    g_conv = lax.dynamic_slice_in_dim(outs["dn_conv_w"][0], me_s * cw, cw, axis=2)
    flat = lambda a: a.reshape(-1, cw)
    res = adamw(flat(dn_conv_w), [flat(g_conv)], flat(m["dn_conv_w"]), flat(v["dn_conv_w"]), name="adamw_conv")
    outs["dn_conv_w"] = [a.reshape(dn_conv_w.shape) for a in res]

    b_rows = layers * 6 * d // LANE
    dmod_all = g_all[:, :b_rows].reshape(N_DEV, layers, 6 * d).transpose(1, 0, 2)
    dmod_shard = lax.dynamic_slice_in_dim(dmod_all, me_s * mod_n, mod_n, axis=2)
    g_wmod = mod_bwd(c_all, dmod_shard)
    flat = lambda a: a.reshape(-1, mod_n)
    res = adamw(flat(w_mod), [flat(g_wmod)], flat(m_w_mod), flat(v_w_mod), name="adamw_w_mod")
    outs["w_mod"] = [a.reshape(w_mod.shape) for a in res]

    result = [loss, grad_x[None]]
    for j in range(4):
        result += [outs[k][j] for k in WEIGHTS]
    return tuple(result)
```

```python
import functools
import math

import jax
import jax.numpy as jnp
from jax import lax
from jax.experimental import pallas as pl
from jax.experimental.pallas import tpu as pltpu

F32 = jnp.float32
BF16 = jnp.bfloat16
HI = lax.Precision.HIGH

NORM_EPS = 1e-6
DN_HEADS = 4
DN_HEAD_DIM = 128
DN_WIDTH = 512
DN_CHUNK = 64
DN_CONV = 4
GM_WIDTH = 256
GM_GROUPS = 4
GM_GROUP_DIM = 64
GM_CHUNK = 128
SW_HEADS = 4
SW_HEAD_DIM = 64
SW_WIDTH = 256
SW_DILATIONS = (1, 4, 16)
SW_BLOCK = 128
ROPE_THETA = 500000.0
ROPE_DIM = 16
LANE = 128

C_QKV = 0
C_Z = 1536
C_AB = 2048
C_SW = 2304
C_UV = 4608
IN_WIDTH = 4872
IN_PAD = 5120
AB_PAD = C_SW - C_AB
MIX_WIDTH = 1024

ADAM_LR = 0.001
ADAM_B1 = 0.9
ADAM_B2 = 0.999
ADAM_EPS = 1e-08
ADAM_WD = 0.01
ADAM_STEP = 10

MESH = pl.DeviceIdType.MESH


BIG_VMEM = 56 << 20


def _call(body, *, name, grid, in_specs, out_specs, out_shape, scratch_shapes=(), semantics=None, aliases=None,
          vmem=None):
    if semantics is None:
        semantics = ("arbitrary",) * len(grid)
    return pl.pallas_call(
        body, name=name, grid=grid, in_specs=in_specs, out_specs=out_specs, out_shape=out_shape,
        scratch_shapes=list(scratch_shapes), input_output_aliases=aliases or {},
        compiler_params=pltpu.CompilerParams(dimension_semantics=semantics, vmem_limit_bytes=vmem),
    )


def _dot(a, b, ca, cb, prec=None):
    if a.ndim == 3:
        dims = (((ca + 1,), (cb + 1,)), ((0,), (0,)))
    else:
        dims = (((ca,), (cb,)), ((), ()))
    return lax.dot_general(a, b, dims, preferred_element_type=F32, precision=prec)


def _bdot(a, b, ca=1, cb=0):
    return _dot(a.astype(BF16), b.astype(BF16), ca, cb)


def _hdot(a, b, ca=1, cb=0):
    return _dot(a.astype(F32), b.astype(F32), ca, cb, HI)


def _split(x):
    hi = x.astype(BF16)
    return hi, (x - hi.astype(F32)).astype(BF16)


def _xdot(a, b, ca=1, cb=0, exact=1):
    if exact == 1:
        hi, lo = _split(a)
        e = b.astype(BF16)
        return _dot(hi, e, ca, cb) + _dot(lo, e, ca, cb)
    hi, lo = _split(b)
    e = a.astype(BF16)
    return _dot(e, hi, ca, cb) + _dot(e, lo, ca, cb)


def _sigmoid(x):
    return 1.0 / (1.0 + jnp.exp(-x))


def _silu(x):
    return x * _sigmoid(x)


def _dsilu(x):
    s = _sigmoid(x)
    return s * (1.0 + x * (1.0 - s))


def _softplus(x):
    return jnp.maximum(x, 0.0) + jnp.log(1.0 + jnp.exp(-jnp.abs(x)))


def _iota2(shape, dim):
    return lax.broadcasted_iota(jnp.int32, shape, dim)


def _rowsum(x):
    return jnp.sum(x, axis=-1, keepdims=True)


def _colsum(x):
    return jnp.sum(x, axis=-2, keepdims=True)


def _full(shape):
    return pl.BlockSpec(shape, lambda *_: (0,) * len(shape))


def _resident(shape):
    return pl.BlockSpec(shape, lambda *_: (0,) * len(shape), pipeline_mode=pl.Buffered(1))


ANY = pl.BlockSpec(memory_space=pl.ANY)


def _norm_mod(x, nw, scale, shift):
    r = lax.rsqrt(jnp.mean(x * x, axis=-1, keepdims=True) + NORM_EPS)
    xn = x * r
    return xn, r, (xn * nw) * (1.0 + scale) + shift


def norm_mm(x, nw, scale, shift, w, *, swiglu, name, tm=512):
    t, d = x.shape
    n = w.shape[1]
    half = n // 2

    def body(x_ref, nw_ref, sc_ref, sh_ref, w_ref, h_ref, y_ref, *act_ref):
        _, _, h = _norm_mod(x_ref[...], nw_ref[...], sc_ref[...], sh_ref[...])
        hb = h.astype(BF16)
        h_ref[...] = hb
        y = _dot(hb, w_ref[...], 1, 0)
        y_ref[...] = y
        if swiglu:
            act_ref[0][...] = (_silu(y[:, :half]) * y[:, half:]).astype(BF16)

    row = lambda i: (i, 0)
    out_shape = [jax.ShapeDtypeStruct((t, d), BF16), jax.ShapeDtypeStruct((t, n), F32)]
    out_specs = [pl.BlockSpec((tm, d), row), pl.BlockSpec((tm, n), row)]
    if swiglu:
        out_shape.append(jax.ShapeDtypeStruct((t, half), BF16))
        out_specs.append(pl.BlockSpec((tm, half), row))
    return _call(
        body, name=name, grid=(t // tm,),
        in_specs=[pl.BlockSpec((tm, d), row), _full((1, d)), _full((1, d)), _full((1, d)), _resident((d, n))],
        out_specs=out_specs, out_shape=out_shape, semantics=("parallel",), vmem=BIG_VMEM,
    )(x, nw, scale, shift, w)


def resid_mm(y, w, x, gate, *, name, tm=512):
    t, k = y.shape
    d = w.shape[1]

    def body(y_ref, w_ref, x_ref, g_ref, xo_ref, o_ref):
        o = _dot(y_ref[...].astype(BF16), w_ref[...], 1, 0)
        o_ref[...] = o
        xo_ref[...] = x_ref[...] + g_ref[...] * o

    row = lambda i: (i, 0)
    return _call(
        body, name=name, grid=(t // tm,),
        in_specs=[pl.BlockSpec((tm, k), row), _resident((k, d)), pl.BlockSpec((tm, d), row), _full((1, d))],
        out_specs=[pl.BlockSpec((tm, d), row), pl.BlockSpec((tm, d), row)],
        out_shape=[jax.ShapeDtypeStruct((t, d), F32), jax.ShapeDtypeStruct((t, d), F32)],
        semantics=("parallel",), vmem=BIG_VMEM,
    )(y, w, x, gate)


def resid_mm_bwd(dx, gate, o, w, gu, *, name, tm):
    t, d = dx.shape
    k = w.shape[0]
    swiglu = gu is not None

    def body(dx_ref, g_ref, o_ref, w_ref, *rest):
        if swiglu:
            gu_ref, dy_ref, gx_ref, dg_ref = rest
        else:
            dy_ref, gx_ref, dg_ref = rest
        i = pl.program_id(0)
        dxv = dx_ref[...]
        gx = (dxv * g_ref[...]).astype(BF16)
        gx_ref[...] = gx
        part = _colsum(dxv * o_ref[...])

        @pl.when(i == 0)
        def _():
            dg_ref[...] = jnp.zeros_like(dg_ref)

        dg_ref[...] += part
        da = _dot(gx, w_ref[...], 1, 1)
        if swiglu:
            g = gu_ref[:, :k]
            u = gu_ref[:, k:]
            dy_ref[:, :k] = (da * u * _dsilu(g)).astype(BF16)
            dy_ref[:, k:] = (da * _silu(g)).astype(BF16)
        else:
            dy_ref[...] = da

    row = lambda i: (i, 0)
    in_specs = [pl.BlockSpec((tm, d), row), _full((1, d)), pl.BlockSpec((tm, d), row), _resident((k, d))]
    args = [dx, gate, o, w]
    if swiglu:
        in_specs.append(pl.BlockSpec((tm, 2 * k), row))
        args.append(gu)
        dy_shape = jax.ShapeDtypeStruct((t, 2 * k), BF16)
        dy_spec = pl.BlockSpec((tm, 2 * k), row)
    else:
        dy_shape = jax.ShapeDtypeStruct((t, k), F32)
        dy_spec = pl.BlockSpec((tm, k), row)
    return _call(
        body, name=name, grid=(t // tm,), in_specs=in_specs,
        out_specs=[dy_spec, pl.BlockSpec((tm, d), row), _full((1, d))],
        out_shape=[dy_shape, jax.ShapeDtypeStruct((t, d), BF16), jax.ShapeDtypeStruct((1, d), F32)], vmem=BIG_VMEM,
    )(*args)


def norm_mm_bwd(dy, w, x, nw, scale, dres, *, name, tm=512):
    t, n = dy.shape
    d = x.shape[1]
    steps = t // tm

    def body(dy_ref, w_ref, x_ref, nw_ref, sc_ref, dres_ref, dx_ref, dnw_ref, dsc_ref, dsh_ref):
        i = pl.program_id(0)
        dh = _dot(dy_ref[...].astype(BF16), w_ref[...], 1, 1)
        x = x_ref[...]
        r = lax.rsqrt(jnp.mean(x * x, axis=-1, keepdims=True) + NORM_EPS)
        xn = x * r
        a = nw_ref[...] * (1.0 + sc_ref[...])

        @pl.when(i == 0)
        def _():
            dnw_ref[...] = jnp.zeros_like(dnw_ref)
            dsh_ref[...] = jnp.zeros_like(dsh_ref)

        dnw_ref[...] += _colsum(dh * xn)
        dsh_ref[...] += _colsum(dh)
        dxn = dh * a
        dx_ref[...] = r * (dxn - xn * jnp.mean(dxn * xn, axis=-1, keepdims=True)) + dres_ref[...]

        @pl.when(i == steps - 1)
        def _():
            da = dnw_ref[...]
            dsc_ref[...] = da * nw_ref[...]
            dnw_ref[...] = da * (1.0 + sc_ref[...])

    row = lambda i: (i, 0)
    vec = jax.ShapeDtypeStruct((1, d), F32)
    return _call(
        body, name=name, grid=(steps,),
        in_specs=[pl.BlockSpec((tm, n), row), _resident((d, n)), pl.BlockSpec((tm, d), row), _full((1, d)),
                  _full((1, d)), pl.BlockSpec((tm, d), row)],
        out_specs=[pl.BlockSpec((tm, d), row), _full((1, d)), _full((1, d)), _full((1, d))],
        out_shape=[jax.ShapeDtypeStruct((t, d), F32), vec, vec, vec], vmem=BIG_VMEM,
    )(dy, w, x, nw, scale, dres)


def _pick_tn(n, k, budget=6 << 20):
    best = LANE
    for m in range(1, n // LANE + 1):
        tn = m * LANE
        if n % tn == 0 and k * tn * 4 <= budget:
            best = tn
    return best


def mm_tn(a, g, *, name, tt=512):
    t, k = a.shape
    n = g.shape[1]
    tn = _pick_tn(n, k)

    def body(a_ref, g_ref, o_ref):
        @pl.when(pl.program_id(1) == 0)
        def _():
            o_ref[...] = jnp.zeros_like(o_ref)

        o_ref[...] += _dot(a_ref[...].astype(BF16), g_ref[...].astype(BF16), 0, 0)

    return _call(
        body, name=name, grid=(n // tn, t // tt),
        in_specs=[pl.BlockSpec((tt, k), lambda j, i: (i, 0)), pl.BlockSpec((tt, tn), lambda j, i: (i, j))],
        out_specs=pl.BlockSpec((k, tn), lambda j, i: (0, j)),
        out_shape=jax.ShapeDtypeStruct((k, n), F32), semantics=("parallel", "arbitrary"),
    )(a, g)


def loss_head(y, target, *, tm=512):
    t, d = y.shape
    steps = t // tm

    def body(y_ref, t_ref, dy_ref, l_ref, acc_ref):
        i = pl.program_id(0)

        @pl.when(i == 0)
        def _():
            acc_ref[...] = jnp.zeros_like(acc_ref)

        e = y_ref[...] - t_ref[...]
        dy_ref[...] = e * (1.0 / d)
        acc_ref[...] += _colsum(e * e)

        @pl.when(i == steps - 1)
        def _():
            tot = jnp.sum(acc_ref[...], axis=-1, keepdims=True) * (0.5 / d)
            l_ref[...] = jnp.broadcast_to(tot, l_ref.shape)

    row = lambda i: (i, 0)
    return _call(
        body, name="loss_head", grid=(steps,),
        in_specs=[pl.BlockSpec((tm, d), row), pl.BlockSpec((tm, d), row)],
        out_specs=[pl.BlockSpec((tm, d), row), _full((8, LANE))],
        out_shape=[jax.ShapeDtypeStruct((t, d), F32), jax.ShapeDtypeStruct((8, LANE), F32)],
        scratch_shapes=[pltpu.VMEM((1, d), F32)],
    )(y, target)


def _shift_rows(x, s):
    if s == 0:
        return x
    t = x.shape[0]
    ri = _iota2(x.shape, 0)
    rolled = pltpu.roll(x, s % t, axis=0)
    if s > 0:
        return jnp.where(ri >= s, rolled, 0.0)
    return jnp.where(ri < t + s, rolled, 0.0)


def _conv_pre(x, w):
    acc = x * w[DN_CONV - 1:DN_CONV, :]
    for j in range(DN_CONV - 1):
        acc = acc + _shift_rows(x, DN_CONV - 1 - j) * w[j:j + 1, :]
    return acc


def dn_conv(proj, conv_w):
    t = proj.shape[0]
    width = 3 * DN_WIDTH

    def body(x_ref, w_ref, o_ref):
        o_ref[...] = _silu(_conv_pre(x_ref[...], w_ref[...]))

    col = lambda j: (0, j)
    return _call(
        body, name="dn_conv", grid=(width // LANE,),
        in_specs=[pl.BlockSpec((t, LANE), col), pl.BlockSpec((8, LANE), col)],
        out_specs=pl.BlockSpec((t, LANE), col),
        out_shape=jax.ShapeDtypeStruct((t, width), F32), semantics=("parallel",),
    )(proj, conv_w)


def dn_conv_bwd(proj, conv_w, dact, dproj):
    t = proj.shape[0]
    width = 3 * DN_WIDTH

    def body(x_ref, w_ref, d_ref, _, dx_ref, dw_ref):
        x = x_ref[...]
        w = w_ref[...]
        dc = d_ref[...] * _dsilu(_conv_pre(x, w))
        dx = dc * w[DN_CONV - 1:DN_CONV, :]
        rows = []
        for j in range(DN_CONV - 1):
            s = DN_CONV - 1 - j
            dx = dx + _shift_rows(dc, -s) * w[j:j + 1, :]
            rows.append(_colsum(dc * _shift_rows(x, s)))
        rows.append(_colsum(dc * x))
        dx_ref[...] = dx
        ri = _iota2((8, LANE), 0)
        dw = jnp.zeros((8, LANE), F32)
        for j in range(DN_CONV):
            dw = dw + jnp.where(ri == j, rows[j], 0.0)
        dw_ref[...] = dw

    col = lambda j: (0, j)
    return _call(
        body, name="dn_conv_bwd", grid=(width // LANE,),
        in_specs=[pl.BlockSpec((t, LANE), col), pl.BlockSpec((8, LANE), col), pl.BlockSpec((t, LANE), col), ANY],
        out_specs=[pl.BlockSpec((t, LANE), col), pl.BlockSpec((8, LANE), col)],
        out_shape=[jax.ShapeDtypeStruct(dproj.shape, F32), jax.ShapeDtypeStruct((8, width), F32)],
        semantics=("parallel",), aliases={3: 0},
    )(proj, conv_w, dact, dproj)


def _t(x):
    return jnp.swapaxes(x, -1, -2)


def _inv_unit_lower(a):
    c = a.shape[-1]
    eye = (_iota2((c, c), 0) == _iota2((c, c), 1)).astype(F32)
    x = eye - a
    p = _hdot(a, a)
    steps = int(math.log2(c)) - 1
    for i in range(steps):
        x = x + _hdot(x, p)
        if i < steps - 1:
            p = _hdot(p, p)
    return x


def _dn_chunk(q, k, v, a, b, alog, dtb, s_in, tinv=None):
    nh, c, d = q.shape
    rq = lax.rsqrt(_rowsum(q * q) + NORM_EPS)
    rk = lax.rsqrt(_rowsum(k * k) + NORM_EPS)
    qh = q * rq
    kn = k * rk
    qs = qh * (d ** -0.5)
    g = -jnp.exp(alog) * _softplus(a + dtb)
    beta = _sigmoid(b)
    ri = _iota2((c, c), 0)
    ci = _iota2((c, c), 1)
    causal = ri >= ci
    strict = ri > ci
    gb = jnp.broadcast_to(g, (nh, c, d))
    gcb = _xdot(jnp.broadcast_to(causal.astype(F32), (nh, c, c)), gb, exact=0)
    gc = gcb[..., :1]
    gl = _colsum(gb)[..., :1]
    dec = jnp.exp(jnp.where(causal, gc - _t(gcb)[:, :c, :], -1e30))
    kb = kn * beta
    amat = jnp.where(strict, _bdot(kb, kn, 1, 1) * dec, 0.0)
    if tinv is None:
        tinv = _inv_unit_lower(amat)
    e = jnp.exp(gc)
    f = jnp.exp(gl - gc)
    rw = kb * e
    sol = _hdot(tinv, jnp.concatenate([v * beta, rw], axis=-1))
    u = sol[..., :d]
    w = sol[..., d:]
    pmat = jnp.where(causal, _bdot(qs, kn, 1, 1) * dec, 0.0)
    qd = qs * e
    kd = kn * f
    vnew = u - _bdot(w, s_in)
    o = _bdot(qd, s_in) + _bdot(pmat, vnew)
    s_out = s_in * jnp.exp(gl) + _bdot(kd, vnew, 0, 0)
    return dict(rq=rq, rk=rk, qh=qh, kn=kn, qs=qs, g=g, beta=beta, causal=causal, strict=strict, gl=gl,
                dec=dec, kb=kb, amat=amat, tinv=tinv, e=e, f=f, rw=rw, u=u, w=w, pmat=pmat, qd=qd, kd=kd,
                vnew=vnew, o=o, s_out=s_out)


def _dn_chunk_bwd(m, q, v, a, alog, dtb, s_in, do, ds_out):
    nh, c, d = q.shape
    kn, qs, kb, u, w, e, f = m["kn"], m["qs"], m["kb"], m["u"], m["w"], m["e"], m["f"]
    beta, dec, tinv, vnew, kd, qd = m["beta"], m["dec"], m["tinv"], m["vnew"], m["kd"], m["qd"]
    el = jnp.exp(m["gl"])
    dvnew = _bdot(m["pmat"], do, 0, 0) + _bdot(kd, ds_out)
    dp = jnp.where(m["causal"], _bdot(do, vnew, 1, 1), 0.0)
    dqd = _bdot(do, s_in, 1, 1)
    dkd = _bdot(vnew, ds_out, 1, 1)
    ds_in = _bdot(qd, do, 0, 0) + el * ds_out - _bdot(w, dvnew, 0, 0)
    dgl = el * _colsum(_rowsum(s_in * ds_out))
    dw = -_bdot(dvnew, s_in, 1, 1)
    dsol = _hdot(tinv, jnp.concatenate([dvnew, dw], axis=-1), 0, 0)
    dru = dsol[..., :d]
    drw = dsol[..., d:]
    da_m = -jnp.where(m["strict"], _bdot(dsol, jnp.concatenate([u, w], axis=-1), 1, 1), 0.0)
    db_m = da_m * dec
    dq_m = dp * dec
    dkb = _bdot(db_m, kn)
    dkn = _bdot(db_m, kb, 0, 0) + _bdot(dq_m, qs, 0, 0)
    dqs = _bdot(dq_m, kn)
    gmat = da_m * m["amat"] + dp * m["pmat"]
    ones = jnp.ones((nh, c, d), F32)
    dgam = (_xdot(gmat, ones) - _xdot(gmat, ones, 0, 0))[..., :1]
    dqs = dqs + dqd * e
    dgam = dgam + _rowsum(dqd * qd)
    dkn = dkn + dkd * f
    tk = _rowsum(dkd * kd)
    dgam = dgam - tk
    dgl = dgl + _colsum(tk)
    dkb = dkb + drw * e
    dgam = dgam + _rowsum(drw * m["rw"])
    dv = dru * beta
    dbeta = _rowsum(dru * v) + _rowsum(dkb * kn)
    dkn = dkn + dkb * beta
    last = (_iota2((c, 1), 0) == c - 1).astype(F32)
    dgam = dgam + last * dgl
    upper = (_iota2((c, c), 0) <= _iota2((c, c), 1)).astype(F32)
    dg = _xdot(jnp.broadcast_to(upper, (nh, c, c)), jnp.broadcast_to(dgam, (nh, c, d)), exact=0)[..., :1]
    dqh = dqs * (d ** -0.5)
    dq = m["rq"] * (dqh - m["qh"] * _rowsum(dqh * m["qh"]))
    dk = m["rk"] * (dkn - kn * _rowsum(dkn * kn))
    sg = _sigmoid(a + dtb)
    da = dg * (-jnp.exp(alog)) * sg
    dalog = _colsum(dg * m["g"])
    ddtb = _colsum(da)
    db = dbeta * beta * (1.0 - beta)
    return dq, dk, dv, da, db, dalog, ddtb, ds_in


def _dn_gate(o, z, wn):
    ro = lax.rsqrt(jnp.mean(o * o, axis=-1, keepdims=True) + NORM_EPS)
    n = o * ro
    return n, ro, n * wn * _silu(z)


def _heads(ref, col0):
    d = DN_HEAD_DIM
    return jnp.stack([ref[:, col0 + h * d:col0 + (h + 1) * d] for h in range(DN_HEADS)])


def _dn_inputs(act_ref, ab_ref, sc_ref):
    ab = ab_ref[...]
    sc = sc_ref[...]
    q = _heads(act_ref, 0)
    k = _heads(act_ref, DN_WIDTH)
    v = _heads(act_ref, 2 * DN_WIDTH)
    a = jnp.stack([ab[:, h:h + 1] for h in range(DN_HEADS)])
    b = jnp.stack([ab[:, DN_HEADS + h:DN_HEADS + h + 1] for h in range(DN_HEADS)])
    alog = jnp.stack([sc[0:1, h:h + 1] for h in range(DN_HEADS)])
    dtb = jnp.stack([sc[1:2, h:h + 1] for h in range(DN_HEADS)])
    return q, k, v, a, b, alog, dtb


def dn_fwd(act, proj, scal, wn):
    t = act.shape[0]
    n = t // DN_CHUNK
    d = DN_HEAD_DIM

    def body(act_ref, z_ref, ab_ref, sc_ref, wn_ref, y_ref, st_ref, ti_ref, s_ref):
        @pl.when(pl.program_id(0) == 0)
        def _():
            s_ref[...] = jnp.zeros_like(s_ref)

        q, k, v, a, b, alog, dtb = _dn_inputs(act_ref, ab_ref, sc_ref)
        s_in = s_ref[...]
        st_ref[0] = s_in
        m = _dn_chunk(q, k, v, a, b, alog, dtb, s_in)
        ti_ref[0] = m["tinv"]
        s_ref[...] = m["s_out"]
        y = _dn_gate(m["o"], _heads(z_ref, 0), wn_ref[...])[2]
        for h in range(DN_HEADS):
            y_ref[:, h * d:(h + 1) * d] = y[h]

    return _call(
        body, name="dn_fwd", grid=(n,),
        in_specs=[pl.BlockSpec((DN_CHUNK, 3 * DN_WIDTH), lambda i: (i, 0)),
                  pl.BlockSpec((DN_CHUNK, DN_WIDTH), lambda i: (i, C_Z // DN_WIDTH)),
                  pl.BlockSpec((DN_CHUNK, LANE), lambda i: (i, C_AB // LANE)),
                  _full((8, LANE)), _full((1, d))],
        out_specs=[pl.BlockSpec((DN_CHUNK, DN_WIDTH), lambda i: (i, 0)),
                   pl.BlockSpec((1, DN_HEADS, d, d), lambda i: (i, 0, 0, 0)),
                   pl.BlockSpec((1, DN_HEADS, DN_CHUNK, DN_CHUNK), lambda i: (i, 0, 0, 0))],
        out_shape=[jax.ShapeDtypeStruct((t, MIX_WIDTH), F32), jax.ShapeDtypeStruct((n, DN_HEADS, d, d), F32),
                   jax.ShapeDtypeStruct((n, DN_HEADS, DN_CHUNK, DN_CHUNK), F32)],
        scratch_shapes=[pltpu.VMEM((DN_HEADS, d, d), F32)],
    )(act, proj, proj, scal, wn)


def dn_bwd(act, proj, scal, wn, states, tinvs, dy):
    t = act.shape[0]
    n = t // DN_CHUNK
    d = DN_HEAD_DIM
    zab = DN_WIDTH + AB_PAD

    def body(act_ref, z_ref, ab_ref, sc_ref, wn_ref, st_ref, ti_ref, dy_ref, dact_ref, dzab_ref, dpar_ref, ds_ref):
        @pl.when(pl.program_id(0) == 0)
        def _():
            ds_ref[...] = jnp.zeros_like(ds_ref)
            dpar_ref[...] = jnp.zeros_like(dpar_ref)

        wnv = wn_ref[...]
        q, k, v, a, b, alog, dtb = _dn_inputs(act_ref, ab_ref, sc_ref)
        s_in = st_ref[0]
        z = _heads(z_ref, 0)
        dyh = _heads(dy_ref, 0)
        m = _dn_chunk(q, k, v, a, b, alog, dtb, s_in, ti_ref[0])
        nrm, ro, _ = _dn_gate(m["o"], z, wnv)
        sz = _silu(z)
        dz = dyh * nrm * wnv * _dsilu(z)
        dn = dyh * wnv * sz
        dwn = _colsum(dyh * nrm * sz)
        do = ro * (dn - nrm * jnp.mean(dn * nrm, axis=-1, keepdims=True))
        dq, dk, dv, da, db, dalog, ddtb, ds_in = _dn_chunk_bwd(m, q, v, a, alog, dtb, s_in, do, ds_ref[...])
        ds_ref[...] = ds_in
        lane = _iota2((DN_CHUNK, LANE), 1)
        prow = _iota2((8, LANE), 0)
        plane = _iota2((8, LANE), 1)
        dab = jnp.zeros((DN_CHUNK, LANE), F32)
        dpar = jnp.zeros((8, LANE), F32)
        for h in range(DN_HEADS):
            dzab_ref[:, h * d:(h + 1) * d] = dz[h]
            dact_ref[:, h * d:(h + 1) * d] = dq[h]
            dact_ref[:, DN_WIDTH + h * d:DN_WIDTH + (h + 1) * d] = dk[h]
            dact_ref[:, 2 * DN_WIDTH + h * d:2 * DN_WIDTH + (h + 1) * d] = dv[h]
            dab = dab + jnp.where(lane == h, da[h], 0.0) + jnp.where(lane == DN_HEADS + h, db[h], 0.0)
            dpar = dpar + jnp.where((prow == 0) & (plane == h), dalog[h], 0.0)
            dpar = dpar + jnp.where((prow == 1) & (plane == h), ddtb[h], 0.0)
            dpar = dpar + jnp.where(prow == 2, dwn[h], 0.0)
        dzab_ref[:, DN_WIDTH:DN_WIDTH + LANE] = dab
        dzab_ref[:, DN_WIDTH + LANE:] = jnp.zeros((DN_CHUNK, AB_PAD - LANE), F32)
        dpar_ref[...] += dpar

    rev = lambda i: (n - 1 - i, 0)
    return _call(
        body, name="dn_bwd", grid=(n,),
        in_specs=[pl.BlockSpec((DN_CHUNK, 3 * DN_WIDTH), rev),
                  pl.BlockSpec((DN_CHUNK, DN_WIDTH), lambda i: (n - 1 - i, C_Z // DN_WIDTH)),
                  pl.BlockSpec((DN_CHUNK, LANE), lambda i: (n - 1 - i, C_AB // LANE)),
                  _full((8, LANE)), _full((1, d)),
                  pl.BlockSpec((1, DN_HEADS, d, d), lambda i: (n - 1 - i, 0, 0, 0)),
                  pl.BlockSpec((1, DN_HEADS, DN_CHUNK, DN_CHUNK), lambda i: (n - 1 - i, 0, 0, 0)),
                  pl.BlockSpec((DN_CHUNK, DN_WIDTH), rev)],
        out_specs=[pl.BlockSpec((DN_CHUNK, 3 * DN_WIDTH), rev),
                   pl.BlockSpec((DN_CHUNK, zab), lambda i: (n - 1 - i, C_Z // zab)), _full((8, LANE))],
        out_shape=[jax.ShapeDtypeStruct((t, 3 * DN_WIDTH), F32), jax.ShapeDtypeStruct((t, IN_PAD), F32),
                   jax.ShapeDtypeStruct((8, LANE), F32)],
        scratch_shapes=[pltpu.VMEM((DN_HEADS, d, d), F32)],
    )(act, proj, proj, scal, wn, states, tinvs, dy)


_INV_SQRT2 = 0.7071067811865476
_INV_SQRT2PI = 0.3989422804014327


def _gelu(x):
    return 0.5 * x * (1.0 + lax.erf(x * _INV_SQRT2))


def _dgelu(x):
    return 0.5 * (1.0 + lax.erf(x * _INV_SQRT2)) + x * jnp.exp(-0.5 * x * x) * _INV_SQRT2PI


def _gm_core(uv, lng, lnb, ws_ref, bst):
    c = uv.shape[0]
    zz = _gelu(uv)
    u = zz[:, :GM_WIDTH]
    vv = zz[:, GM_WIDTH:]
    xc = vv - jnp.mean(vv, axis=-1, keepdims=True)
    rs = lax.rsqrt(jnp.mean(xc * xc, axis=-1, keepdims=True) + NORM_EPS)
    xh = xc * rs
    vn = xh * lng + lnb
    grp = _iota2((c, GM_WIDTH), 1) // GM_GROUP_DIM
    tril = _iota2((c, c), 0) >= _iota2((c, c), 1)
    sv = jnp.zeros((c, GM_WIDTH), F32)
    masks = []
    for g in range(GM_GROUPS):
        mk = grp == g
        masks.append(mk)
        ws = jnp.where(tril, ws_ref[g], 0.0)
        sv = sv + _bdot(ws, jnp.where(mk, vn, 0.0)) + jnp.where(mk, bst[:, g:g + 1], 0.0)
    return u, xh, rs, vn, sv, masks, tril


def gm_fwd(proj, lng, lnb, w_s, bst, ybuf):
    t = proj.shape[0]

    def body(uv_ref, g_ref, b_ref, ws_ref, bst_ref, _, y_ref):
        u, _, _, _, sv, _, _ = _gm_core(uv_ref[...], g_ref[...], b_ref[...], ws_ref, bst_ref[...])
        y_ref[...] = u * sv

    return _call(
        body, name="gm_fwd", grid=(t // GM_CHUNK,),
        in_specs=[pl.BlockSpec((GM_CHUNK, 2 * GM_WIDTH), lambda i: (i, C_UV // (2 * GM_WIDTH))),
                  _full((1, GM_WIDTH)), _full((1, GM_WIDTH)), _full((GM_GROUPS, GM_CHUNK, GM_CHUNK)),
                  _full((GM_CHUNK, LANE)), ANY],
        out_specs=pl.BlockSpec((GM_CHUNK, GM_WIDTH), lambda i: (i, DN_WIDTH // GM_WIDTH)),
        out_shape=jax.ShapeDtypeStruct(ybuf.shape, F32), semantics=("parallel",), aliases={5: 0},
    )(proj, lng, lnb, w_s, bst, ybuf)


def gm_bwd(proj, lng, lnb, w_s, bst, dy, dproj):
    t = proj.shape[0]

    def body(uv_ref, g_ref, b_ref, ws_ref, bst_ref, dy_ref, _, duv_ref, dws_ref, dbst_ref, dln_ref):
        @pl.when(pl.program_id(0) == 0)
        def _():
            dws_ref[...] = jnp.zeros_like(dws_ref)
            dbst_ref[...] = jnp.zeros_like(dbst_ref)
            dln_ref[...] = jnp.zeros_like(dln_ref)

        uv = uv_ref[...]
        lng = g_ref[...]
        u, xh, rs, vn, sv, masks, tril = _gm_core(uv, lng, b_ref[...], ws_ref, bst_ref[...])
        dyv = dy_ref[...]
        dsv = dyv * u
        lane = _iota2((GM_CHUNK, LANE), 1)
        dvn = jnp.zeros_like(dsv)
        dbst = jnp.zeros((GM_CHUNK, LANE), F32)
        for g in range(GM_GROUPS):
            ws = jnp.where(tril, ws_ref[g], 0.0)
            dsg = jnp.where(masks[g], dsv, 0.0)
            dvn = dvn + jnp.where(masks[g], _bdot(ws, dsv, 0, 0), 0.0)
            dws_ref[g] += jnp.where(tril, _bdot(dsg, vn, 1, 1), 0.0)
            dbst = dbst + jnp.where(lane == g, _rowsum(dsg), 0.0)
        dbst_ref[...] += dbst
        row = _iota2((8, GM_WIDTH), 0)
        dln_ref[...] += jnp.where(row == 0, _colsum(dvn * xh), 0.0) + jnp.where(row == 1, _colsum(dvn), 0.0)
        dxh = dvn * lng
        dvv = rs * (dxh - jnp.mean(dxh, axis=-1, keepdims=True) - xh * jnp.mean(dxh * xh, axis=-1, keepdims=True))
        dg = _dgelu(uv)
        duv_ref[:, :GM_WIDTH] = dyv * sv * dg[:, :GM_WIDTH]
        duv_ref[:, GM_WIDTH:] = dvv * dg[:, GM_WIDTH:]

    return _call(
        body, name="gm_bwd", grid=(t // GM_CHUNK,),
        in_specs=[pl.BlockSpec((GM_CHUNK, 2 * GM_WIDTH), lambda i: (i, C_UV // (2 * GM_WIDTH))),
                  _full((1, GM_WIDTH)), _full((1, GM_WIDTH)), _full((GM_GROUPS, GM_CHUNK, GM_CHUNK)),
                  _full((GM_CHUNK, LANE)),
                  pl.BlockSpec((GM_CHUNK, GM_WIDTH), lambda i: (i, DN_WIDTH // GM_WIDTH)), ANY],
        out_specs=[pl.BlockSpec((GM_CHUNK, 2 * GM_WIDTH), lambda i: (i, C_UV // (2 * GM_WIDTH))),
                   _full((GM_GROUPS, GM_CHUNK, GM_CHUNK)), _full((GM_CHUNK, LANE)), _full((8, GM_WIDTH))],
        out_shape=[jax.ShapeDtypeStruct(dproj.shape, F32),
                   jax.ShapeDtypeStruct((GM_GROUPS, GM_CHUNK, GM_CHUNK), F32),
                   jax.ShapeDtypeStruct((GM_CHUNK, LANE), F32), jax.ShapeDtypeStruct((8, GM_WIDTH), F32)],
        aliases={6: 0},
    )(proj, lng, lnb, w_s, bst, dy, dproj)


def _head_mats():
    r = _iota2((SW_WIDTH, SW_WIDTH), 0)
    c = _iota2((SW_WIDTH, SW_WIDTH), 1)
    same = (r // SW_HEAD_DIM) == (c // SW_HEAD_DIM)
    cc = c % SW_HEAD_DIM
    half = ROPE_DIM // 2
    rot = jnp.where((cc < half) & (r == c + half), -1.0, 0.0) + jnp.where((cc >= half) & (cc < ROPE_DIM) & (r == c - half), 1.0, 0.0)
    return same.astype(F32), rot


def _seg_col(s):
    return C_SW // SW_WIDTH + (s // 2) * 3 + s % 2


def _halves(x):
    return x[:, :LANE], x[:, LANE:]


def sw_prep(proj, nw2, cos_t, sin_t, *, tm=512):
    t = proj.shape[0]

    def body(x_ref, w_ref, c_ref, s_ref, o_ref):
        same, rot = _head_mats()
        x = x_ref[...]
        r = lax.rsqrt(_xdot(x * x, same) * (1.0 / SW_HEAD_DIM) + NORM_EPS)
        xn = x * r * w_ref[0]
        o_ref[0, 0], o_ref[0, 1] = _halves(xn * c_ref[...] + _xdot(xn, rot) * s_ref[...])

    return _call(
        body, name="sw_prep", grid=(6, t // tm),
        in_specs=[pl.BlockSpec((tm, SW_WIDTH), lambda s, i: (i, _seg_col(s))),
                  pl.BlockSpec((1, 1, SW_WIDTH), lambda s, i: (s % 2, 0, 0)),
                  pl.BlockSpec((tm, SW_WIDTH), lambda s, i: (i, 0)),
                  pl.BlockSpec((tm, SW_WIDTH), lambda s, i: (i, 0))],
        out_specs=pl.BlockSpec((1, 2, tm, LANE), lambda s, i: (s, 0, i, 0)),
        out_shape=jax.ShapeDtypeStruct((6, 2, t, LANE), F32), semantics=("parallel", "parallel"),
    )(proj, nw2, cos_t, sin_t)


def sw_prep_bwd(proj, nw2, cos_t, sin_t, dkvq, dproj, dnw, p, *, tm=512):
    t = proj.shape[0]
    col0 = C_SW // SW_WIDTH + 3 * p
    seg_col = lambda s: col0 + (s + 1) % 3

    def body(x_ref, w_ref, c_ref, s_ref, d_ref, _, dw0_ref, dx_ref, dw_ref):
        s = pl.program_id(0)
        dout = jnp.concatenate([d_ref[0, 0], d_ref[0, 1]], axis=1)

        @pl.when(s == 1)
        def _():
            dx_ref[...] = dout

        @pl.when((s != 1) & (pl.program_id(1) == 0))
        def _():
            dw_ref[...] = dw0_ref[...]

        @pl.when(s != 1)
        def _():
            same, rot = _head_mats()
            x = x_ref[...]
            w = w_ref[0]
            r = lax.rsqrt(_xdot(x * x, same) * (1.0 / SW_HEAD_DIM) + NORM_EPS)
            xh = x * r
            dxn = dout * c_ref[...] + _xdot(dout * s_ref[...], rot, 1, 1)
            dw_ref[0] += _colsum(dxn * xh)
            dxh = dxn * w
            dx_ref[...] = r * (dxh - xh * (_xdot(dxh * xh, same) * (1.0 / SW_HEAD_DIM)))

    return _call(
        body, name=f"sw_prep_bwd{p}", grid=(3, t // tm),
        in_specs=[pl.BlockSpec((tm, SW_WIDTH), lambda s, i: (i, seg_col(s))),
                  pl.BlockSpec((1, 1, SW_WIDTH), lambda s, i: (1 - s // 2, 0, 0)),
                  pl.BlockSpec((tm, SW_WIDTH), lambda s, i: (i, 0)),
                  pl.BlockSpec((tm, SW_WIDTH), lambda s, i: (i, 0)),
                  pl.BlockSpec((1, 2, tm, LANE), lambda s, i: (s, 0, i, 0)), ANY,
                  pl.BlockSpec((1, 1, SW_WIDTH), lambda s, i: (s // 2, 0, 0))],
        out_specs=[pl.BlockSpec((tm, SW_WIDTH), lambda s, i: (i, seg_col(s))),
                   pl.BlockSpec((1, 1, SW_WIDTH), lambda s, i: (s // 2, 0, 0))],
        out_shape=[jax.ShapeDtypeStruct(dproj.shape, F32), jax.ShapeDtypeStruct((2, 1, SW_WIDTH), F32)],
        semantics=("arbitrary", "arbitrary"), aliases={5: 0},
    )(proj, nw2, cos_t, sin_t, dkvq, dproj, dnw)


_SW_SCALE = SW_HEAD_DIM ** -0.5
_NEG = -1e30


def _sw_masks(has_other):
    ri = _iota2((SW_BLOCK, SW_BLOCK), 0)
    ci = _iota2((SW_BLOCK, SW_BLOCK), 1)
    return ri >= ci, (ci >= ri) & has_other


def _pair(x):
    first = _iota2((1, LANE), 1) < SW_HEAD_DIM
    return jnp.stack([jnp.where(first, x, 0.0), jnp.where(first, 0.0, x)])


def _both(x):
    return jnp.broadcast_to(x.astype(BF16)[None], (2,) + x.shape)


def _unpair(x2):
    first = _iota2((1, LANE), 1) < SW_HEAD_DIM
    return jnp.where(first, x2[0], x2[1])


def _head_cols(x):
    return jnp.stack([x[:, 0:1], x[:, SW_HEAD_DIM:SW_HEAD_DIM + 1]])


SW_GROUP = 4


def _sw_geometry(t, p):
    dil = SW_DILATIONS[p]
    unit = SW_BLOCK * dil
    nb = max(1, SW_GROUP // dil)
    return dil, unit, nb, t // (unit * nb)


def _sw_groups(dil, nb, body):
    if nb * dil == SW_GROUP:
        body([(k // dil, k % dil) for k in range(SW_GROUP)])
    else:
        def step(g, carry):
            body([(0, SW_GROUP * g + k) for k in range(SW_GROUP)])
            return carry

        lax.fori_loop(0, nb * dil // SW_GROUP, step, 0)


def _sw_rows(i, r, dil):
    start = i * SW_BLOCK * dil + r
    return pl.ds(start, SW_BLOCK) if dil == 1 else pl.ds(start, SW_BLOCK, stride=dil)


def _sw_load(refs, probs, dil, shift, wrap, fn):
    out = []
    for i, r in probs:
        if shift != 0 and i == wrap:
            out.append(fn(refs[1][_sw_rows(0, r, dil), :]))
        else:
            out.append(fn(refs[0][_sw_rows(i + shift, r, dil), :]))
    return jnp.concatenate(out, axis=0)


def _sw_other_masks(probs, wrap, edge_ok):
    _, other = _sw_masks(edge_ok)
    _, always = _sw_masks(True)
    return jnp.stack([other if i == wrap else always for i, _ in probs for _ in range(2)])


def sw_attn(qk, proj, p):
    t = proj.shape[0]
    dil, unit, nb, nsp = _sw_geometry(t, p)
    vcol = (C_SW + 3 * SW_WIDTH * p + 2 * SW_WIDTH) // LANE

    def body(q_ref, kc_ref, kp_ref, vc_ref, vp_ref, o_ref, l_ref):
        mc, _ = _sw_masks(True)
        first = pl.program_id(1) != 0
        q_r, k_r, v_r = (q_ref.at[0, 0], None), (kc_ref.at[0, 0], kp_ref.at[0, 0]), (vc_ref, vp_ref)

        def one(probs):
            mp = _sw_other_masks(probs, 0, first)
            q2 = _sw_load(q_r, probs, dil, 0, 0, _pair)
            sc = jnp.where(mc, _bdot(q2, _sw_load(k_r, probs, dil, 0, 0, _both), 1, 1) * _SW_SCALE, _NEG)
            sp = jnp.where(mp, _bdot(q2, _sw_load(k_r, probs, dil, -1, 0, _both), 1, 1) * _SW_SCALE, _NEG)
            mx = jnp.maximum(jnp.max(sc, axis=-1, keepdims=True), jnp.max(sp, axis=-1, keepdims=True))
            pc = jnp.exp(sc - mx)
            pp = jnp.exp(sp - mx)
            den = _rowsum(pc) + _rowsum(pp)
            o2 = (_bdot(pc, _sw_load(v_r, probs, dil, 0, 0, _both))
                  + _bdot(pp, _sw_load(v_r, probs, dil, -1, 0, _both))) / den
            l2 = jnp.broadcast_to(mx + jnp.log(den), o2.shape)
            for n, (i, r) in enumerate(probs):
                o_ref.at[0][_sw_rows(i, r, dil), :] = _unpair(o2[2 * n:2 * n + 2])
                l_ref.at[0][_sw_rows(i, r, dil), :] = _unpair(l2[2 * n:2 * n + 2])

        _sw_groups(dil, nb, one)

    before = lambda j: jnp.maximum(j * nb - 1, 0)
    seg = lambda s: pl.BlockSpec((1, 1, unit * nb, LANE), lambda h, j: (s, h, j, 0))
    seg_b = lambda s: pl.BlockSpec((1, 1, unit, LANE), lambda h, j: (s, h, before(j), 0))
    out = pl.BlockSpec((1, unit * nb, LANE), lambda h, j: (h, j, 0))
    shp = jax.ShapeDtypeStruct((2, t, LANE), F32)
    return _call(
        body, name=f"sw_attn{p}", grid=(2, nsp),
        in_specs=[seg(2 * p), seg(2 * p + 1), seg_b(2 * p + 1),
                  pl.BlockSpec((unit * nb, LANE), lambda h, j: (j, vcol + h)),
                  pl.BlockSpec((unit, LANE), lambda h, j: (before(j), vcol + h))],
        out_specs=[out, out], out_shape=[shp, shp], semantics=("parallel", "parallel"),
    )(qk, qk, qk, proj, proj)


def sw_attn_dkv(qk, proj, dy, lg, dm, p):
    t = proj.shape[0]
    dil, unit, nb, nsp = _sw_geometry(t, p)
    nunits = t // unit
    vcol = (C_SW + 3 * SW_WIDTH * p + 2 * SW_WIDTH) // LANE
    ycol = (DN_WIDTH + GM_WIDTH) // LANE

    def body(k_ref, v_ref, qc_ref, qn_ref, doc_ref, don_ref, lc_ref, ln_ref, dc_ref, dn_ref, o_ref):
        mc, _ = _sw_masks(True)
        more = pl.program_id(1) + 1 < nsp
        q_r, do_r = (qc_ref.at[0, 0], qn_ref.at[0, 0]), (doc_ref, don_ref)
        l_r, d_r = (lc_ref.at[0], ln_ref.at[0]), (dc_ref.at[0], dn_ref.at[0])

        def one(probs):
            k2 = _sw_load((k_ref.at[0, 0], None), probs, dil, 0, 0, _both)
            v2 = _sw_load((v_ref, None), probs, dil, 0, 0, _both)
            dk = jnp.zeros((2 * SW_GROUP, SW_BLOCK, LANE), F32)
            dv = jnp.zeros((2 * SW_GROUP, SW_BLOCK, LANE), F32)
            for shift, mk in ((0, mc), (1, _sw_other_masks(probs, nb - 1, more))):
                q2 = _sw_load(q_r, probs, dil, shift, nb - 1, _pair)
                do2 = _sw_load(do_r, probs, dil, shift, nb - 1, _pair)
                lse = _sw_load(l_r, probs, dil, shift, nb - 1, _head_cols)
                dd = _sw_load(d_r, probs, dil, shift, nb - 1, _head_cols)
                pr = jnp.exp(jnp.where(mk, _bdot(q2, k2, 1, 1) * _SW_SCALE, _NEG) - lse)
                dv = dv + _bdot(pr, do2, 0, 0)
                ds = pr * (_bdot(do2, v2, 1, 1) - dd)
                dk = dk + _bdot(ds, q2, 0, 0)
            for n, (i, r) in enumerate(probs):
                o_ref.at[0, 0][_sw_rows(i, r, dil), :] = (dk[2 * n] + dk[2 * n + 1]) * _SW_SCALE
                o_ref.at[1, 0][_sw_rows(i, r, dil), :] = dv[2 * n] + dv[2 * n + 1]

        _sw_groups(dil, nb, one)

    after = lambda j: jnp.minimum((j + 1) * nb, nunits - 1)
    seg = lambda s: pl.BlockSpec((1, 1, unit * nb, LANE), lambda h, j: (s, h, j, 0))
    seg_a = lambda s: pl.BlockSpec((1, 1, unit, LANE), lambda h, j: (s, h, after(j), 0))
    col = lambda c0: pl.BlockSpec((unit * nb, LANE), lambda h, j: (j, c0 + h))
    col_a = lambda c0: pl.BlockSpec((unit, LANE), lambda h, j: (after(j), c0 + h))
    hp = pl.BlockSpec((1, unit * nb, LANE), lambda h, j: (h, j, 0))
    hp_a = pl.BlockSpec((1, unit, LANE), lambda h, j: (h, after(j), 0))
    return _call(
        body, name=f"sw_dkv{p}", grid=(2, nsp),
        in_specs=[seg(2 * p + 1), col(vcol), seg(2 * p), seg_a(2 * p), col(ycol), col_a(ycol), hp, hp_a, hp, hp_a],
        out_specs=pl.BlockSpec((2, 1, unit * nb, LANE), lambda h, j: (0, h, j, 0)),
        out_shape=jax.ShapeDtypeStruct((3, 2, t, LANE), F32), semantics=("parallel", "parallel"),
    )(qk, proj, qk, qk, dy, dy, lg, lg, dm, dm)


def sw_attn_dq(qk, proj, dy, lg, dm, dkvq, p):
    t = proj.shape[0]
    dil, unit, nb, nsp = _sw_geometry(t, p)
    vcol = (C_SW + 3 * SW_WIDTH * p + 2 * SW_WIDTH) // LANE
    ycol = (DN_WIDTH + GM_WIDTH) // LANE

    def body(q_ref, kc_ref, kp_ref, vc_ref, vp_ref, do_ref, l_ref, d_ref, _, dq_ref):
        mc, _ = _sw_masks(True)
        first = pl.program_id(1) != 0
        k_r, v_r = (kc_ref.at[0, 0], kp_ref.at[0, 0]), (vc_ref, vp_ref)

        def one(probs):
            mp = _sw_other_masks(probs, 0, first)
            q2 = _sw_load((q_ref.at[0, 0], None), probs, dil, 0, 0, _pair)
            do2 = _sw_load((do_ref, None), probs, dil, 0, 0, _pair)
            lse = _sw_load((l_ref.at[0], None), probs, dil, 0, 0, _head_cols)
            dd = _sw_load((d_ref.at[0], None), probs, dil, 0, 0, _head_cols)
            kc = _sw_load(k_r, probs, dil, 0, 0, _both)
            kp = _sw_load(k_r, probs, dil, -1, 0, _both)
            pc = jnp.exp(jnp.where(mc, _bdot(q2, kc, 1, 1) * _SW_SCALE, _NEG) - lse)
            pp = jnp.exp(jnp.where(mp, _bdot(q2, kp, 1, 1) * _SW_SCALE, _NEG) - lse)
            dsc = pc * (_bdot(do2, _sw_load(v_r, probs, dil, 0, 0, _both), 1, 1) - dd)
            dsp = pp * (_bdot(do2, _sw_load(v_r, probs, dil, -1, 0, _both), 1, 1) - dd)
            dq2 = (_bdot(dsc, kc) + _bdot(dsp, kp)) * _SW_SCALE
            for n, (i, r) in enumerate(probs):
                dq_ref.at[0, 0][_sw_rows(i, r, dil), :] = _unpair(dq2[2 * n:2 * n + 2])

        _sw_groups(dil, nb, one)

    before = lambda j: jnp.maximum(j * nb - 1, 0)
    seg = lambda s: pl.BlockSpec((1, 1, unit * nb, LANE), lambda h, j: (s, h, j, 0))
    seg_b = lambda s: pl.BlockSpec((1, 1, unit, LANE), lambda h, j: (s, h, before(j), 0))
    col = lambda c0: pl.BlockSpec((unit * nb, LANE), lambda h, j: (j, c0 + h))
    col_b = lambda c0: pl.BlockSpec((unit, LANE), lambda h, j: (before(j), c0 + h))
    hp = pl.BlockSpec((1, unit * nb, LANE), lambda h, j: (h, j, 0))
    return _call(
        body, name=f"sw_dq{p}", grid=(2, nsp),
        in_specs=[seg(2 * p), seg(2 * p + 1), seg_b(2 * p + 1), col(vcol), col_b(vcol), col(ycol), hp, hp, ANY],
        out_specs=pl.BlockSpec((1, 1, unit * nb, LANE), lambda h, j: (2, h, j, 0)),
        out_shape=jax.ShapeDtypeStruct(dkvq.shape, F32), semantics=("parallel", "parallel"), aliases={8: 0},
    )(qk, qk, qk, proj, proj, dy, lg, dm, dkvq)


def sw_merge(outs, lses, ybuf, *, tm=512):
    t = ybuf.shape[0]

    def body(o0, o1, o2, l0_ref, l1_ref, l2_ref, _, y_ref, lg_ref):
        l0, l1, l2 = l0_ref[...], l1_ref[...], l2_ref[...]
        mx = jnp.maximum(jnp.maximum(l0, l1), l2)
        lg = mx + jnp.log(jnp.exp(l0 - mx) + jnp.exp(l1 - mx) + jnp.exp(l2 - mx))
        lg_ref[...] = lg
        y = jnp.exp(l0 - lg) * o0[...] + jnp.exp(l1 - lg) * o1[...] + jnp.exp(l2 - lg) * o2[...]
        y_ref[...] = jnp.concatenate([y[0], y[1]], axis=1)

    hp = pl.BlockSpec((2, tm, LANE), lambda i: (0, i, 0))
    return _call(
        body, name="sw_merge", grid=(t // tm,), in_specs=[hp] * 6 + [ANY],
        out_specs=[pl.BlockSpec((tm, SW_WIDTH), lambda i: (i, (DN_WIDTH + GM_WIDTH) // SW_WIDTH)), hp],
        out_shape=[jax.ShapeDtypeStruct(ybuf.shape, F32), jax.ShapeDtypeStruct((2, t, LANE), F32)],
        semantics=("parallel",), aliases={6: 0},
    )(*outs, *lses, ybuf)


def sw_delta(dy, ybuf, *, tm=512):
    t = ybuf.shape[0]

    def body(dy_ref, y_ref, o_ref):
        same, _ = _head_mats()
        o_ref[0], o_ref[1] = _halves(_xdot(dy_ref[...] * y_ref[...], same))

    b1 = pl.BlockSpec((tm, SW_WIDTH), lambda i: (i, (DN_WIDTH + GM_WIDTH) // SW_WIDTH))
    return _call(body, name="sw_delta", grid=(t // tm,), in_specs=[b1, b1],
                 out_specs=pl.BlockSpec((2, tm, LANE), lambda i: (0, i, 0)),
                 out_shape=jax.ShapeDtypeStruct((2, t, LANE), F32), semantics=("parallel",))(dy, ybuf)


def _rope_tables(t):
    inv = ROPE_THETA ** (-jnp.arange(0, ROPE_DIM, 2, dtype=F32) / ROPE_DIM)
    ang = jnp.arange(t, dtype=F32)[:, None] * inv[None, :]
    pad1 = jnp.ones((t, SW_HEAD_DIM - ROPE_DIM), F32)
    pad0 = jnp.zeros((t, SW_HEAD_DIM - ROPE_DIM), F32)
    cos_h = jnp.concatenate([jnp.cos(ang), jnp.cos(ang), pad1], axis=1)
    sin_h = jnp.concatenate([jnp.sin(ang), jnp.sin(ang), pad0], axis=1)
    return jnp.tile(cos_h, (1, SW_HEADS)), jnp.tile(sin_h, (1, SW_HEADS))


def sw_forward(proj, nw2, cos_t, sin_t, ybuf):
    qk = sw_prep(proj, nw2, cos_t, sin_t)
    outs, lses = [], []
    for p in range(len(SW_DILATIONS)):
        o, lse = sw_attn(qk, proj, p)
        outs.append(o)
        lses.append(lse)
    ybuf, lg = sw_merge(outs, lses, ybuf)
    return ybuf, (qk, lg)


def sw_backward(proj, nw2, cos_t, sin_t, res, ybuf, dy, dproj):
    qk, lg = res
    dm = sw_delta(dy, ybuf)
    dnw = jnp.zeros((2, 1, SW_WIDTH), F32)
    for p in range(len(SW_DILATIONS)):
        dkvq = sw_attn_dkv(qk, proj, dy, lg, dm, p)
        dkvq = sw_attn_dq(qk, proj, dy, lg, dm, dkvq, p)
        dproj, dnw = sw_prep_bwd(proj, nw2, cos_t, sin_t, dkvq, dproj, dnw, p)
    return dproj, dnw[::-1, 0]


def _pad_rows(a, rows):
    return jnp.zeros((rows,) + a.shape[1:], a.dtype).at[:a.shape[0]].set(a)


def _consts(sp):
    d = {}
    d["mix_nw"] = sp["mix_norm_w"][:, None, :]
    d["ffn_nw"] = sp["ffn_norm_w"][:, None, :]
    d["cw8"] = jnp.pad(sp["dn_conv_w"], ((0, 0), (0, 8 - DN_CONV), (0, 0)))
    d["scal"] = jnp.pad(jnp.stack([sp["dn_a_log"], sp["dn_dt_bias"]], axis=1), ((0, 0), (0, 6), (0, LANE - DN_HEADS)))
    d["wn"] = sp["dn_out_norm_w"][:, None, :]
    d["lng"] = sp["gm_ln_g"][:, None, :]
    d["lnb"] = sp["gm_ln_b"][:, None, :]
    d["w_s"] = sp["gm_w_s"]
    d["bst"] = jnp.pad(jnp.swapaxes(sp["gm_b_s"], 1, 2), ((0, 0), (0, 0), (0, LANE - GM_GROUPS)))
    d["nw2"] = jnp.stack([jnp.tile(sp["sw_q_norm_w"], (1, SW_HEADS)),
                          jnp.tile(sp["sw_k_norm_w"], (1, SW_HEADS))], axis=1)[:, :, None, :]
    return d


def _layer_fwd(x, mod, get_w, cs, tabs):
    wb = dict(get_w("w_in", x))
    h1, proj = norm_mm(x, cs["mix_nw"], mod[1], mod[0], wb["w_in"], swiglu=False, name="in_proj")
    act = dn_conv(proj, cs["cw8"])
    y, states, tinvs = dn_fwd(act, proj, cs["scal"], cs["wn"])
    y = gm_fwd(proj, cs["lng"], cs["lnb"], cs["w_s"], cs["bst"], y)
    y, swres = sw_forward(proj, cs["nw2"], *tabs, y)
    wb.update(get_w("w_out", y))
    x1, o1 = resid_mm(y, wb["w_out"], x, mod[2], name="out_proj")
    wb.update(get_w("ffn", x1))
    h2, gu, actf = norm_mm(x1, cs["ffn_nw"], mod[4], mod[3], wb["w_ffn_in"], swiglu=True, name="ffn_in")
    x2, o2 = resid_mm(actf, wb["w_ffn_out"], x1, mod[5], name="ffn_out")
    res = dict(x=x, h1=h1, proj=proj, act=act, states=states, tinvs=tinvs, swres=swres, y=y, x1=x1, o1=o1, h2=h2, gu=gu,
               actf=actf, o2=o2)
    return x2, res, wb


def _layer_bwd(dx2, res, mod, wb, cs, tabs, grads_done):
    dgu, gx2, dgate2 = resid_mm_bwd(dx2, mod[5], res["o2"], wb["w_ffn_out"], res["gu"], name="ffn_out_bwd", tm=256)
    g_wfo = mm_tn(res["actf"], gx2, name="wg_ffn_out")
    g_wfi = mm_tn(res["h2"], dgu, name="wg_ffn_in")
    token = grads_done("ffn", dict(w_ffn_in=g_wfi, w_ffn_out=g_wfo))
    dx1, d_ffn_nw, dscale2, dshift2 = norm_mm_bwd(dgu, wb["w_ffn_in"], res["x1"], cs["ffn_nw"], mod[4] + token, dx2,
                                                  name="ffn_in_bwd")
    dy, gx1, dgate1 = resid_mm_bwd(dx1, mod[2], res["o1"], wb["w_out"], None, name="out_proj_bwd", tm=512)
    g_wout = mm_tn(res["y"], gx1, name="wg_out")
    proj = res["proj"]
    dact, dproj, dpar = dn_bwd(res["act"], proj, cs["scal"], cs["wn"], res["states"], res["tinvs"], dy)
    dproj, dcw = dn_conv_bwd(proj, cs["cw8"], dact, dproj)
    dproj, dws, dbst, dln = gm_bwd(proj, cs["lng"], cs["lnb"], cs["w_s"], cs["bst"], dy, dproj)
    dproj, dnw = sw_backward(proj, cs["nw2"], *tabs, res["swres"], res["y"], dy, dproj)
    g_win = mm_tn(res["h1"], dproj, name="wg_in")
    dx, d_mix_nw, dscale1, dshift1 = norm_mm_bwd(dproj, wb["w_in"], res["x"], cs["mix_nw"], mod[1], dx1,
                                                 name="in_proj_bwd")
    dmod = jnp.concatenate([dshift1, dscale1, dgate1, dshift2, dscale2, dgate2], axis=1)
    dnw = dnw.reshape(2, SW_HEADS, SW_HEAD_DIM).sum(1)
    small = dict(mix_norm_w=d_mix_nw[0], ffn_norm_w=d_ffn_nw[0], dn_conv_w=dcw[:DN_CONV],
                 dn_a_log=dpar[0, :DN_HEADS], dn_dt_bias=dpar[1, :DN_HEADS], dn_out_norm_w=dpar[2],
                 gm_ln_g=dln[0], gm_ln_b=dln[1], gm_w_s=dws, gm_b_s=dbst[:, :GM_GROUPS].T,
                 sw_q_norm_w=dnw[0], sw_k_norm_w=dnw[1])
    token = grads_done("mix", dict(w_in=g_win, w_out=g_wout))
    return dx, small, dmod, token


def _permute_w_in(w):
    pad = jnp.zeros(w.shape[:-1] + (AB_PAD - 8,), w.dtype)
    return jnp.concatenate([w[..., 0:2056], pad, w[..., 2568:IN_WIDTH], w[..., 2056:2568]], axis=-1)


def _unpermute_w_in(g):
    return jnp.concatenate([g[..., 0:2056], g[..., C_UV:IN_PAD], g[..., C_SW:C_UV]], axis=-1)


def _local_step(x, target, mods, weights_of, grads_done, sp):
    layers = mods.shape[0]
    t, d = x.shape
    tabs = _rope_tables(t)
    consts = _consts(sp)
    saved = []
    for layer in range(layers):
        mod = mods[layer].reshape(6, 1, d)
        cs = {k: v[layer] for k, v in consts.items()}
        x, res, wb = _layer_fwd(x, mod, functools.partial(weights_of, layer), cs, tabs)
        saved.append((res, mod, wb, cs))
    dx, loss = loss_head(x, target)
    smalls, dmods = [], []
    token = jnp.zeros((1, 1), F32)
    for layer in reversed(range(layers)):
        res, mod, wb, cs = saved[layer]
        dx, small, dmod, token = _layer_bwd(dx, res, mod + token, wb, cs, tabs, functools.partial(grads_done, layer))
        smalls.append(small)
        dmods.append(dmod[0])
    smalls, dmods = smalls[::-1], dmods[::-1]
    small = {k: jnp.stack([s[k] for s in smalls]) for k in smalls[0]}
    return loss, dx, small, jnp.stack(dmods)


def mod_fwd(c_all, w_mod, b_shard):
    layers, d, n = w_mod.shape

    def body(c_ref, w_ref, b_ref, o_ref):
        ca = _silu(c_ref[...]).astype(BF16)
        o_ref[0] = _dot(ca, w_ref[0].astype(BF16), 1, 0) + b_ref[0]

    return _call(
        body, name="mod_fwd", grid=(layers,),
        in_specs=[_full((8, d)), pl.BlockSpec((1, d, n), lambda i: (i, 0, 0)),
                  pl.BlockSpec((1, 1, n), lambda i: (i, 0, 0))],
        out_specs=pl.BlockSpec((1, 8, n), lambda i: (i, 0, 0)),
        out_shape=jax.ShapeDtypeStruct((layers, 8, n), F32), semantics=("parallel",),
    )(c_all, w_mod, b_shard)


def mod_bwd(c_all, dmod):
    layers, _, n = dmod.shape
    d = c_all.shape[1]

    def body(c_ref, g_ref, o_ref):
        ca = _silu(c_ref[...]).astype(BF16)
        o_ref[0] = _dot(ca, g_ref[0].astype(BF16), 0, 0)

    return _call(
        body, name="mod_bwd", grid=(layers,),
        in_specs=[_full((8, d)), pl.BlockSpec((1, 8, n), lambda i: (i, 0, 0))],
        out_specs=pl.BlockSpec((1, d, n), lambda i: (i, 0, 0)),
        out_shape=jax.ShapeDtypeStruct((layers, d, n), F32), semantics=("parallel",),
    )(c_all, dmod)


N_DEV = 8


def _place():
    return lax.axis_index("x"), lax.axis_index("y"), lax.axis_index("c")


def _other_chips(x, y):
    return [(1 - x, y), (x, 1 - y), (1 - x, 1 - y)]


def allgather8(x_shard, *, name):
    m_per, n = x_shard.shape

    def body(x_ref, out_ref, send_sems, recv_sems, local_sem):
        x, y, c = _place()
        me, sibling = (x, y, c), (x, y, 1 - c)
        chips = _other_chips(x, y)

        def rows(px, py, pc):
            return out_ref.at[pl.ds((4 * px + 2 * py + pc) * m_per, m_per), :]

        def copy(k, block, to, src=None):
            return pltpu.make_async_remote_copy(
                src_ref=rows(*block) if src is None else src, dst_ref=rows(*block),
                send_sem=send_sems.at[k], recv_sem=recv_sems.at[k], device_id=to, device_id_type=MESH)

        mine = pltpu.make_async_copy(x_ref, rows(*me), local_sem)
        mine.start()
        first = [copy(0, me, sibling, src=x_ref)]
        first += [copy(1 + j, me, (*chip, c), src=x_ref) for j, chip in enumerate(chips)]
        for cp in first:
            cp.start()
        passed = [copy(4 + j, (*chip, c), sibling) for j, chip in enumerate(chips)]
        for j, chip in enumerate(chips):
            copy(1 + j, (*chip, c), me).wait_recv()
            passed[j].start()
        copy(0, sibling, me).wait_recv()
        for j, chip in enumerate(chips):
            copy(4 + j, (*chip, 1 - c), me).wait_recv()
        for cp in first + passed:
            cp.wait_send()
        mine.wait()

    return pl.pallas_call(
        body, name=name, out_shape=jax.ShapeDtypeStruct((N_DEV * m_per, n), x_shard.dtype),
        in_specs=[pl.BlockSpec(memory_space=pltpu.VMEM)], out_specs=pl.BlockSpec(memory_space=pltpu.VMEM),
        scratch_shapes=[pltpu.SemaphoreType.DMA((7,)), pltpu.SemaphoreType.DMA((7,)), pltpu.SemaphoreType.DMA],
    )(x_shard)


HBM = pl.BlockSpec(memory_space=pltpu.HBM)
SEM = pl.BlockSpec(memory_space=pltpu.SEMAPHORE)
_EFFECT = pltpu.SideEffectType.DATAFLOW_SIDE_EFFECTING


def _piece(ref, sliced, chip):
    return ref.at[2 * chip[0] + chip[1]] if sliced else ref


def exchange_start(srcs, after, *, sliced, name):
    n = len(srcs)
    piece = lambda s: s.shape[1:] if sliced else s.shape

    def body(*refs):
        ins, lands = refs[:n], refs[n:2 * n]
        send_sems, recv_sems = refs[2 * n + len(after):2 * n + len(after) + 2]
        token = refs[-1]
        x, y, c = _place()
        me_s = 2 * x + y
        for a in range(n):
            for j, chip in enumerate(_other_chips(x, y)):
                pltpu.make_async_remote_copy(
                    src_ref=_piece(ins[a], sliced, chip), dst_ref=lands[a].at[me_s], send_sem=send_sems.at[3 * a + j],
                    recv_sem=recv_sems.at[3 * a + j], device_id=(*chip, c), device_id_type=MESH).start()
        token[...] = jnp.zeros_like(token)

    zones = [pltpu.with_memory_space_constraint(lax.empty((4,) + piece(s), s.dtype), pltpu.HBM) for s in srcs]
    srcs = [pltpu.with_memory_space_constraint(s, pltpu.HBM) for s in srcs]
    out = pl.pallas_call(
        body, name=name,
        out_shape=(pltpu.SemaphoreType.DMA((3 * n,)), pltpu.SemaphoreType.DMA((3 * n,)),
                   *[pltpu.HBM(s.shape, s.dtype) for s in srcs], *[pltpu.HBM(z.shape, z.dtype) for z in zones],
                   jax.ShapeDtypeStruct((8, LANE), F32)),
        in_specs=[HBM] * (2 * n) + [ANY] * len(after),
        out_specs=(SEM, SEM, *[HBM] * (2 * n), pl.BlockSpec(memory_space=pltpu.VMEM)),
        input_output_aliases={i: 2 + i for i in range(2 * n)},
        compiler_params=pltpu.CompilerParams(has_side_effects=_EFFECT),
    )(*srcs, *zones, *after)
    return out[0], out[1], out[2:2 + n], out[2 + n:2 + 2 * n], out[-1]


def exchange_wait(send_sems, recv_sems, srcs, zones, after, *, which, sliced, name):
    n = len(srcs)

    def body(*refs):
        ins, lands = refs[:n], refs[n:2 * n]
        send_sems, recv_sems = refs[2 * n:2 * n + 2]
        x, y, c = _place()
        for a in range(n):
            for j, chip in enumerate(_other_chips(x, y)):
                copy = pltpu.make_async_remote_copy(
                    src_ref=_piece(ins[a], sliced, chip), dst_ref=lands[a].at[2 * chip[0] + chip[1]],
                    send_sem=send_sems.at[3 * which[a] + j], recv_sem=recv_sems.at[3 * which[a] + j],
                    device_id=(*chip, c), device_id_type=MESH)
                copy.wait_send()
                copy.wait_recv()

    out = pl.pallas_call(
        body, name=name,
        out_shape=tuple(pltpu.HBM(s.shape, s.dtype) for s in (*srcs, *zones)),
        in_specs=[HBM] * (2 * n) + [SEM, SEM, ANY], out_specs=tuple([HBM] * (2 * n)),
        input_output_aliases={i: i for i in range(2 * n)},
        compiler_params=pltpu.CompilerParams(has_side_effects=_EFFECT),
    )(*srcs, *zones, send_sems, recv_sems, after)
    return out[n:]


def sibling_swap(parts):
    n = len(parts)

    def body(*refs):
        ins, outs = refs[:n], refs[n:2 * n]
        send_sems, recv_sems = refs[2 * n:]
        x, y, c = _place()
        cps = []
        for a in range(n):
            cp = pltpu.make_async_remote_copy(
                src_ref=ins[a], dst_ref=outs[a], send_sem=send_sems.at[a], recv_sem=recv_sems.at[a],
                device_id=(x, y, 1 - c), device_id_type=MESH)
            cp.start()
            cps.append(cp)
        for cp in cps:
            cp.wait()

    return pl.pallas_call(
        body, name="sibling_swap", out_shape=[jax.ShapeDtypeStruct(p.shape, p.dtype) for p in parts],
        in_specs=[ANY] * n, out_specs=[ANY] * n,
        scratch_shapes=[pltpu.SemaphoreType.DMA((n,)), pltpu.SemaphoreType.DMA((n,))],
    )(*parts)


def _row_block(rows, cols, budget=1 << 20):
    best = rows if rows % 8 else 8
    for tr in range(8, rows + 1, 8):
        if rows % tr == 0 and tr * cols * 4 <= budget:
            best = tr
    return best


def chip_sum(own, recv, me_s, buf, layer, layers, *, name):
    _, r, n = own.shape
    tr = _row_block(r, n)
    steps = r // tr

    def body(me_ref, own_ref, recv_ref, *rest):
        o_ref = rest[-1]
        me = me_ref[0]
        acc = jnp.zeros((tr, n), F32)
        for s in range(4):
            acc = acc + jnp.where(me == s, own_ref[0], recv_ref[s].astype(F32))
        o_ref[...] = acc

    in_specs = [pl.BlockSpec((1, tr, n), lambda i, me: (me[0], i, 0)), pl.BlockSpec((4, tr, n), lambda i, me: (0, i, 0))]
    args = [me_s, own, recv]
    aliases = {}
    if buf is not None:
        in_specs.append(ANY)
        args.append(buf)
        aliases = {3: 0}
    return pl.pallas_call(
        body, name=name, out_shape=jax.ShapeDtypeStruct((layers * r, n), F32),
        grid_spec=pltpu.PrefetchScalarGridSpec(
            num_scalar_prefetch=1, grid=(steps,), in_specs=in_specs,
            out_specs=pl.BlockSpec((tr, n), lambda i, me: (layer * steps + i, 0))),
        input_output_aliases=aliases,
        compiler_params=pltpu.CompilerParams(dimension_semantics=("parallel",)),
    )(*args)


def _adam_update(w, g, m, v):
    m2 = ADAM_B1 * m + (1.0 - ADAM_B1) * g
    v2 = ADAM_B2 * v + (1.0 - ADAM_B2) * (g * g)
    m_hat = m2 / (1.0 - ADAM_B1 ** ADAM_STEP)
    v_hat = v2 / (1.0 - ADAM_B2 ** ADAM_STEP)
    delta = -ADAM_LR * (m_hat / (jnp.sqrt(v_hat) + ADAM_EPS) + ADAM_WD * w)
    return delta, m2, v2


def adamw(w, g_parts, m, v, *, name):
    r, n = w.shape
    tr = _row_block(r, n)
    k = len(g_parts)

    def body(*refs):
        w_ref, m_ref, v_ref = refs[k], refs[k + 1], refs[k + 2]
        g_ref, d_ref, m2_ref, v2_ref = refs[k + 3:]
        g = refs[0][...]
        for p in refs[1:k]:
            g = g + p[...]
        g_ref[...] = g
        d_ref[...], m2_ref[...], v2_ref[...] = _adam_update(w_ref[...], g, m_ref[...], v_ref[...])

    blk = pl.BlockSpec((tr, n), lambda i: (i, 0))
    shp = jax.ShapeDtypeStruct((r, n), F32)
    return _call(body, name=name, grid=(r // tr,), in_specs=[blk] * (k + 3), out_specs=[blk] * 4,
                 out_shape=[shp] * 4, semantics=("parallel",))(*g_parts, w, m, v)


def adamw_gathered(g_all, w, m, v, *, name):
    _, r, n = g_all.shape
    tr = _row_block(r, n * 4)

    def body(ga_ref, w_ref, m_ref, v_ref, g_ref, d_ref, m2_ref, v2_ref):
        g = ga_ref[0]
        for dev in range(1, N_DEV):
            g = g + ga_ref[dev]
        g_ref[...] = g
        d_ref[...], m2_ref[...], v2_ref[...] = _adam_update(w_ref[...], g, m_ref[...], v_ref[...])

    blk = pl.BlockSpec((tr, n), lambda i: (i, 0))
    shp = jax.ShapeDtypeStruct((r, n), F32)
    return _call(body, name=name, grid=(r // tr,),
                 in_specs=[pl.BlockSpec((N_DEV, tr, n), lambda i: (0, i, 0)), blk, blk, blk], out_specs=[blk] * 4,
                 out_shape=[shp] * 4, semantics=("parallel",))(g_all, w, m, v)


BIG = ("w_in", "w_out", "w_ffn_in", "w_ffn_out")
SMALL = ("b_mod", "mix_norm_w", "ffn_norm_w", "dn_conv_w", "dn_a_log", "dn_dt_bias", "dn_out_norm_w", "gm_ln_g",
         "gm_ln_b", "gm_w_s", "gm_b_s", "sw_q_norm_w", "sw_k_norm_w")
WEIGHTS = ("w_mod", "b_mod", "mix_norm_w", "ffn_norm_w", "w_in", "w_out", "dn_conv_w", "dn_a_log", "dn_dt_bias",
           "dn_out_norm_w", "gm_ln_g", "gm_ln_b", "gm_w_s", "gm_b_s", "sw_q_norm_w", "sw_k_norm_w", "w_ffn_in",
           "w_ffn_out")
PACK_ROWS = 8


def _pack(arrs):
    out = []
    for a in arrs:
        flat = a.reshape(-1).astype(F32)
        rows = -(-flat.shape[0] // (LANE * PACK_ROWS)) * PACK_ROWS
        out.append(jnp.pad(flat, (0, rows * LANE - flat.shape[0])).reshape(rows, LANE))
    return jnp.concatenate(out, axis=0)


def _unpack(packed, shapes):
    out, r0 = [], 0
    for shp in shapes:
        size = math.prod(shp)
        rows = -(-size // (LANE * PACK_ROWS)) * PACK_ROWS
        out.append(packed[r0:r0 + rows].reshape(-1)[:size].reshape(shp))
        r0 += rows
    return out


def kernel(x, c, w_mod, b_mod, mix_norm_w, ffn_norm_w, w_in, w_out, dn_conv_w, dn_a_log, dn_dt_bias, dn_out_norm_w, gm_ln_g, gm_ln_b, gm_w_s, gm_b_s, sw_q_norm_w, sw_k_norm_w, w_ffn_in, w_ffn_out, loss_target, m_w_mod, m_b_mod, m_mix_norm_w, m_ffn_norm_w, m_w_in, m_w_out, m_dn_conv_w, m_dn_a_log, m_dn_dt_bias, m_dn_out_norm_w, m_gm_ln_g, m_gm_ln_b, m_gm_w_s, m_gm_b_s, m_sw_q_norm_w, m_sw_k_norm_w, m_w_ffn_in, m_w_ffn_out, v_w_mod, v_b_mod, v_mix_norm_w, v_ffn_norm_w, v_w_in, v_w_out, v_dn_conv_w, v_dn_a_log, v_dn_dt_bias, v_dn_out_norm_w, v_gm_ln_g, v_gm_ln_b, v_gm_w_s, v_gm_b_s, v_sw_q_norm_w, v_sw_k_norm_w, v_w_ffn_in, v_w_ffn_out):
    w = dict(w_mod=w_mod, b_mod=b_mod, mix_norm_w=mix_norm_w, ffn_norm_w=ffn_norm_w, w_in=w_in, w_out=w_out,
             dn_conv_w=dn_conv_w, dn_a_log=dn_a_log, dn_dt_bias=dn_dt_bias, dn_out_norm_w=dn_out_norm_w,
             gm_ln_g=gm_ln_g, gm_ln_b=gm_ln_b, gm_w_s=gm_w_s, gm_b_s=gm_b_s, sw_q_norm_w=sw_q_norm_w,
             sw_k_norm_w=sw_k_norm_w, w_ffn_in=w_ffn_in, w_ffn_out=w_ffn_out)
    m = dict(w_mod=m_w_mod, b_mod=m_b_mod, mix_norm_w=m_mix_norm_w, ffn_norm_w=m_ffn_norm_w, w_in=m_w_in,
             w_out=m_w_out, dn_conv_w=m_dn_conv_w, dn_a_log=m_dn_a_log, dn_dt_bias=m_dn_dt_bias,
             dn_out_norm_w=m_dn_out_norm_w, gm_ln_g=m_gm_ln_g, gm_ln_b=m_gm_ln_b, gm_w_s=m_gm_w_s, gm_b_s=m_gm_b_s,
             sw_q_norm_w=m_sw_q_norm_w, sw_k_norm_w=m_sw_k_norm_w, w_ffn_in=m_w_ffn_in, w_ffn_out=m_w_ffn_out)
    v = dict(w_mod=v_w_mod, b_mod=v_b_mod, mix_norm_w=v_mix_norm_w, ffn_norm_w=v_ffn_norm_w, w_in=v_w_in,
             w_out=v_w_out, dn_conv_w=v_dn_conv_w, dn_a_log=v_dn_a_log, dn_dt_bias=v_dn_dt_bias,
             dn_out_norm_w=v_dn_out_norm_w, gm_ln_g=v_gm_ln_g, gm_ln_b=v_gm_ln_b, gm_w_s=v_gm_w_s, gm_b_s=v_gm_b_s,
             sw_q_norm_w=v_sw_q_norm_w, sw_k_norm_w=v_sw_k_norm_w, w_ffn_in=v_w_ffn_in, w_ffn_out=v_w_ffn_out)
    layers, d, mod_n = w_mod.shape
    mx, my, mc = _place()
    me_s = 2 * mx + my
    me_dev = 4 * mx + 2 * my + mc

    c_all = allgather8(_pad_rows(c, 8), name="gather_c").reshape(N_DEV, 8, d)[:, 0]
    b_shard = lax.dynamic_slice_in_dim(b_mod, me_s * mod_n, mod_n, axis=1)[:, None, :]
    mod_part = mod_fwd(c_all, w_mod, b_shard)
    mod_parts = allgather8(mod_part.reshape(layers * 8, mod_n), name="gather_mod")
    mod_parts = mod_parts.reshape(4, 2, layers, 8, mod_n)[:, 0]
    mod_all = mod_parts.transpose(1, 2, 0, 3).reshape(layers, 8, 4 * mod_n)
    mods = lax.dynamic_index_in_dim(mod_all, me_dev, axis=1, keepdims=False)

    cw = dn_conv_w.shape[-1]
    conv_rows = -(-layers * DN_CONV // 8) * 8
    conv_parts = allgather8(_pad_rows(dn_conv_w.reshape(layers * DN_CONV, cw), conv_rows), name="gather_conv")
    conv_parts = conv_parts.reshape(4, 2, conv_rows, cw)[:, 0, :layers * DN_CONV]
    conv_full = conv_parts.reshape(4, layers, DN_CONV, cw).transpose(1, 2, 0, 3).reshape(layers, DN_CONV, 4 * cw)

    shards = {k: w[k].astype(BF16) for k in BIG}
    groups = dict(w_in=(0,), w_out=(1,), ffn=(2, 3))
    gathers = [exchange_start([shards[k][layer] for k in BIG], [mods, conv_full], sliced=False, name=f"gather_start{layer}")
               for layer in range(layers)]
    mods = mods + sum(g[4][0, 0] for g in gathers)

    def weights_of(layer, group, after):
        send_sems, recv_sems, srcs, zones, _ = gathers[layer]
        which = groups[group]
        got = exchange_wait(send_sems, recv_sems, [srcs[a] for a in which], [zones[a] for a in which], after,
                            which=which, sliced=False, name=f"gather_wait_{group}{layer}")
        full = {BIG[a]: lax.dynamic_update_index_in_dim(z, shards[BIG[a]][layer], me_s, 0) for a, z in zip(which, got)}
        cols = lambda g: jnp.concatenate([g[s] for s in range(4)], axis=-1)
        shape = dict(w_in=lambda g: _permute_w_in(cols(g)), w_out=lambda g: g.reshape(-1, d), w_ffn_in=cols,
                     w_ffn_out=lambda g: g.reshape(-1, d))
        return {k: shape[k](g) for k, g in full.items()}

    scatters = {}
    cut = lambda g, axis: jnp.stack(jnp.split(g, 4, axis=axis))
    shard_major = dict(w_in=lambda g: cut(_unpermute_w_in(g), 1), w_out=lambda g: cut(g, 0),
                       w_ffn_in=lambda g: cut(g, 1), w_ffn_out=lambda g: cut(g, 0))

    def grads_done(layer, group, grads):
        own = {k: shard_major[k](g) for k, g in grads.items()}
        started = exchange_start([g.astype(BF16) for g in own.values()], [], sliced=True,
                                 name=f"scatter_start_{group}{layer}")
        scatters[layer, group] = (started, own)
        return started[4][:1, :1]

    sp = {k: w[k] for k in SMALL}
    sp["dn_conv_w"] = conv_full
    loss_blk, grad_x, small, dmods = _local_step(x[0], loss_target[0], mods, weights_of, grads_done, sp)
    loss = lax.psum(loss_blk[0, 0], ("x", "y", "c"))

    me_arr = jnp.reshape(me_s, (1,)).astype(jnp.int32)
    partial = {k: None for k in BIG}
    for layer in range(layers):
        for group in ("ffn", "mix"):
            (send_sems, recv_sems, srcs, zones, _), own = scatters[layer, group]
            zones = exchange_wait(send_sems, recv_sems, srcs, zones, grad_x, which=tuple(range(len(srcs))),
                                  sliced=True, name=f"scatter_wait_{group}{layer}")
            for k, z in zip(own, zones):
                partial[k] = chip_sum(own[k], z, me_arr, partial[k], layer, layers, name=f"chip_sum_{k}{layer}")
    partial = [partial[k] for k in BIG]
    theirs = sibling_swap(partial)
    outs = {}
    for k, mine, other in zip(BIG, partial, theirs):
        shp = w[k].shape
        flat = lambda a: a.reshape(-1, shp[-1])
        res = adamw(flat(w[k]), [mine, other], flat(m[k]), flat(v[k]), name="adamw_" + k)
        outs[k] = [a.reshape(shp) for a in res]

    small = dict(small, b_mod=dmods)
    packed = _pack([small[k] for k in SMALL])
    rows = packed.shape[0]
    g_all = allgather8(packed, name="gather_small").reshape(N_DEV, rows, LANE)
    conv_zero = jnp.zeros((layers, DN_CONV, 3 * DN_WIDTH), F32)
    pk = lambda src: _pack([conv_zero if k == "dn_conv_w" else src[k] for k in SMALL])
    res = adamw_gathered(g_all, pk(w), pk(m), pk(v), name="adamw_small")
    shapes = [small[k].shape for k in SMALL]
    un = [_unpack(a, shapes) for a in res]
    for i, k in enumerate(SMALL):
        outs[k] = [un[j][i] for j in range(4)]
    g_conv = lax.dynamic_slice_in_dim(outs["dn_conv_w"][0], me_s * cw, cw, axis=2)
    flat = lambda a: a.reshape(-1, cw)
    res = adamw(flat(dn_conv_w), [flat(g_conv)], flat(m["dn_conv_w"]), flat(v["dn_conv_w"]), name="adamw_conv")
    outs["dn_conv_w"] = [a.reshape(dn_conv_w.shape) for a in res]

    b_rows = layers * 6 * d // LANE
    dmod_all = g_all[:, :b_rows].reshape(N_DEV, layers, 6 * d).transpose(1, 0, 2)
    dmod_shard = lax.dynamic_slice_in_dim(dmod_all, me_s * mod_n, mod_n, axis=2)
    g_wmod = mod_bwd(c_all, dmod_shard)
    flat = lambda a: a.reshape(-1, mod_n)
    res = adamw(flat(w_mod), [flat(g_wmod)], flat(m_w_mod), flat(v_w_mod), name="adamw_w_mod")
    outs["w_mod"] = [a.reshape(w_mod.shape) for a in res]

    result = [loss, grad_x[None]]
    for j in range(4):
        result += [outs[k][j] for k in WEIGHTS]
    return tuple(result)
```

```python
import functools
import math

import jax
import jax.numpy as jnp
from jax import lax
from jax.experimental import pallas as pl
from jax.experimental.pallas import tpu as pltpu

F32 = jnp.float32
BF16 = jnp.bfloat16
HI = lax.Precision.HIGH

NORM_EPS = 1e-6
DN_HEADS = 4
DN_HEAD_DIM = 128
DN_WIDTH = 512
DN_CHUNK = 64
DN_CONV = 4
GM_WIDTH = 256
GM_GROUPS = 4
GM_GROUP_DIM = 64
GM_CHUNK = 128
SW_HEADS = 4
SW_HEAD_DIM = 64
SW_WIDTH = 256
SW_DILATIONS = (1, 4, 16)
SW_BLOCK = 128
ROPE_THETA = 500000.0
ROPE_DIM = 16
LANE = 128

C_QKV = 0
C_Z = 1536
C_AB = 2048
C_SW = 2304
C_UV = 4608
IN_WIDTH = 4872
IN_PAD = 5120
AB_PAD = C_SW - C_AB
MIX_WIDTH = 1024

ADAM_LR = 0.001
ADAM_B1 = 0.9
ADAM_B2 = 0.999
ADAM_EPS = 1e-08
ADAM_WD = 0.01
ADAM_STEP = 10

MESH = pl.DeviceIdType.MESH


BIG_VMEM = 56 << 20


def _call(body, *, name, grid, in_specs, out_specs, out_shape, scratch_shapes=(), semantics=None, aliases=None,
          vmem=None):
    if semantics is None:
        semantics = ("arbitrary",) * len(grid)
    return pl.pallas_call(
        body, name=name, grid=grid, in_specs=in_specs, out_specs=out_specs, out_shape=out_shape,
        scratch_shapes=list(scratch_shapes), input_output_aliases=aliases or {},
        compiler_params=pltpu.CompilerParams(dimension_semantics=semantics, vmem_limit_bytes=vmem),
    )


def _dot(a, b, ca, cb, prec=None):
    if a.ndim == 3:
        dims = (((ca + 1,), (cb + 1,)), ((0,), (0,)))
    else:
        dims = (((ca,), (cb,)), ((), ()))
    return lax.dot_general(a, b, dims, preferred_element_type=F32, precision=prec)


def _bdot(a, b, ca=1, cb=0):
    return _dot(a.astype(BF16), b.astype(BF16), ca, cb)


def _hdot(a, b, ca=1, cb=0):
    return _dot(a.astype(F32), b.astype(F32), ca, cb, HI)


def _split(x):
    hi = x.astype(BF16)
    return hi, (x - hi.astype(F32)).astype(BF16)


def _xdot(a, b, ca=1, cb=0, exact=1):
    if exact == 1:
        hi, lo = _split(a)
        e = b.astype(BF16)
        return _dot(hi, e, ca, cb) + _dot(lo, e, ca, cb)
    hi, lo = _split(b)
    e = a.astype(BF16)
    return _dot(e, hi, ca, cb) + _dot(e, lo, ca, cb)


def _sigmoid(x):
    return 0.5 * jnp.tanh(0.5 * x) + 0.5


def _silu(x):
    return x * _sigmoid(x)


def _dsilu(x):
    s = _sigmoid(x)
    return s * (1.0 + x * (1.0 - s))


def _softplus(x):
    return jnp.maximum(x, 0.0) + jnp.log(1.0 + jnp.exp(-jnp.abs(x)))


def _iota2(shape, dim):
    return lax.broadcasted_iota(jnp.int32, shape, dim)


def _rowsum(x):
    return jnp.sum(x, axis=-1, keepdims=True)


def _colsum(x):
    return jnp.sum(x, axis=-2, keepdims=True)


def _full(shape):
    return pl.BlockSpec(shape, lambda *_: (0,) * len(shape))


def _resident(shape):
    return pl.BlockSpec(shape, lambda *_: (0,) * len(shape), pipeline_mode=pl.Buffered(1))


ANY = pl.BlockSpec(memory_space=pl.ANY)


def _norm_mod(x, nw, scale, shift):
    r = lax.rsqrt(jnp.mean(x * x, axis=-1, keepdims=True) + NORM_EPS)
    xn = x * r
    return xn, r, (xn * nw) * (1.0 + scale) + shift


def norm_mm(x, nw, scale, shift, w, *, swiglu, name, tm=512):
    t, d = x.shape
    n = w.shape[1]
    half = n // 2

    def body(x_ref, nw_ref, sc_ref, sh_ref, w_ref, h_ref, y_ref, *act_ref):
        _, _, h = _norm_mod(x_ref[...], nw_ref[...], sc_ref[...], sh_ref[...])
        hb = h.astype(BF16)
        h_ref[...] = hb
        y = _dot(hb, w_ref[...], 1, 0)
        y_ref[...] = y
        if swiglu:
            act_ref[0][...] = (_silu(y[:, :half]) * y[:, half:]).astype(BF16)

    row = lambda i: (i, 0)
    out_shape = [jax.ShapeDtypeStruct((t, d), BF16), jax.ShapeDtypeStruct((t, n), F32)]
    out_specs = [pl.BlockSpec((tm, d), row), pl.BlockSpec((tm, n), row)]
    if swiglu:
        out_shape.append(jax.ShapeDtypeStruct((t, half), BF16))
        out_specs.append(pl.BlockSpec((tm, half), row))
    return _call(
        body, name=name, grid=(t // tm,),
        in_specs=[pl.BlockSpec((tm, d), row), _full((1, d)), _full((1, d)), _full((1, d)), _resident((d, n))],
        out_specs=out_specs, out_shape=out_shape, semantics=("parallel",), vmem=BIG_VMEM,
    )(x, nw, scale, shift, w)


def resid_mm(y, w, x, gate, *, name, tm=512):
    t, k = y.shape
    d = w.shape[1]

    def body(y_ref, w_ref, x_ref, g_ref, xo_ref, o_ref):
        o = _dot(y_ref[...].astype(BF16), w_ref[...], 1, 0)
        o_ref[...] = o
        xo_ref[...] = x_ref[...] + g_ref[...] * o

    row = lambda i: (i, 0)
    return _call(
        body, name=name, grid=(t // tm,),
        in_specs=[pl.BlockSpec((tm, k), row), _resident((k, d)), pl.BlockSpec((tm, d), row), _full((1, d))],
        out_specs=[pl.BlockSpec((tm, d), row), pl.BlockSpec((tm, d), row)],
        out_shape=[jax.ShapeDtypeStruct((t, d), F32), jax.ShapeDtypeStruct((t, d), F32)],
        semantics=("parallel",), vmem=BIG_VMEM,
    )(y, w, x, gate)


def resid_mm_bwd(dx, gate, o, w, gu, *, name, tm):
    t, d = dx.shape
    k = w.shape[0]
    swiglu = gu is not None

    def body(dx_ref, g_ref, o_ref, w_ref, *rest):
        if swiglu:
            gu_ref, dy_ref, gx_ref, dg_ref = rest
        else:
            dy_ref, gx_ref, dg_ref = rest
        i = pl.program_id(0)
        dxv = dx_ref[...]
        gx = (dxv * g_ref[...]).astype(BF16)
        gx_ref[...] = gx
        part = _colsum(dxv * o_ref[...])

        @pl.when(i == 0)
        def _():
            dg_ref[...] = jnp.zeros_like(dg_ref)

        dg_ref[...] += part
        da = _dot(gx, w_ref[...], 1, 1)
        if swiglu:
            g = gu_ref[:, :k]
            u = gu_ref[:, k:]
            dy_ref[:, :k] = (da * u * _dsilu(g)).astype(BF16)
            dy_ref[:, k:] = (da * _silu(g)).astype(BF16)
        else:
            dy_ref[...] = da

    row = lambda i: (i, 0)
    in_specs = [pl.BlockSpec((tm, d), row), _full((1, d)), pl.BlockSpec((tm, d), row), _resident((k, d))]
    args = [dx, gate, o, w]
    if swiglu:
        in_specs.append(pl.BlockSpec((tm, 2 * k), row))
        args.append(gu)
        dy_shape = jax.ShapeDtypeStruct((t, 2 * k), BF16)
        dy_spec = pl.BlockSpec((tm, 2 * k), row)
    else:
        dy_shape = jax.ShapeDtypeStruct((t, k), F32)
        dy_spec = pl.BlockSpec((tm, k), row)
    return _call(
        body, name=name, grid=(t // tm,), in_specs=in_specs,
        out_specs=[dy_spec, pl.BlockSpec((tm, d), row), _full((1, d))],
        out_shape=[dy_shape, jax.ShapeDtypeStruct((t, d), BF16), jax.ShapeDtypeStruct((1, d), F32)], vmem=BIG_VMEM,
    )(*args)


def norm_mm_bwd(dy, w, x, nw, scale, dres, *, name, tm=512):
    t, n = dy.shape
    d = x.shape[1]
    steps = t // tm

    def body(dy_ref, w_ref, x_ref, nw_ref, sc_ref, dres_ref, dx_ref, dnw_ref, dsc_ref, dsh_ref):
        i = pl.program_id(0)
        dh = _dot(dy_ref[...].astype(BF16), w_ref[...], 1, 1)
        x = x_ref[...]
        r = lax.rsqrt(jnp.mean(x * x, axis=-1, keepdims=True) + NORM_EPS)
        xn = x * r
        a = nw_ref[...] * (1.0 + sc_ref[...])

        @pl.when(i == 0)
        def _():
            dnw_ref[...] = jnp.zeros_like(dnw_ref)
            dsh_ref[...] = jnp.zeros_like(dsh_ref)

        dnw_ref[...] += _colsum(dh * xn)
        dsh_ref[...] += _colsum(dh)
        dxn = dh * a
        dx_ref[...] = r * (dxn - xn * jnp.mean(dxn * xn, axis=-1, keepdims=True)) + dres_ref[...]

        @pl.when(i == steps - 1)
        def _():
            da = dnw_ref[...]
            dsc_ref[...] = da * nw_ref[...]
            dnw_ref[...] = da * (1.0 + sc_ref[...])

    row = lambda i: (i, 0)
    vec = jax.ShapeDtypeStruct((1, d), F32)
    return _call(
        body, name=name, grid=(steps,),
        in_specs=[pl.BlockSpec((tm, n), row), _resident((d, n)), pl.BlockSpec((tm, d), row), _full((1, d)),
                  _full((1, d)), pl.BlockSpec((tm, d), row)],
        out_specs=[pl.BlockSpec((tm, d), row), _full((1, d)), _full((1, d)), _full((1, d))],
        out_shape=[jax.ShapeDtypeStruct((t, d), F32), vec, vec, vec], vmem=BIG_VMEM,
    )(dy, w, x, nw, scale, dres)


def _pick_tn(n, k, budget=6 << 20):
    best = LANE
    for m in range(1, n // LANE + 1):
        tn = m * LANE
        if n % tn == 0 and k * tn * 4 <= budget:
            best = tn
    return best


def mm_tn(a, g, *, name, tt=512):
    t, k = a.shape
    n = g.shape[1]
    tn = _pick_tn(n, k)

    def body(a_ref, g_ref, o_ref):
        @pl.when(pl.program_id(1) == 0)
        def _():
            o_ref[...] = jnp.zeros_like(o_ref)

        o_ref[...] += _dot(a_ref[...].astype(BF16), g_ref[...].astype(BF16), 0, 0)

    return _call(
        body, name=name, grid=(n // tn, t // tt),
        in_specs=[pl.BlockSpec((tt, k), lambda j, i: (i, 0)), pl.BlockSpec((tt, tn), lambda j, i: (i, j))],
        out_specs=pl.BlockSpec((k, tn), lambda j, i: (0, j)),
        out_shape=jax.ShapeDtypeStruct((k, n), F32), semantics=("parallel", "arbitrary"),
    )(a, g)


def loss_head(y, target, *, tm=512):
    t, d = y.shape
    steps = t // tm

    def body(y_ref, t_ref, dy_ref, l_ref, acc_ref):
        i = pl.program_id(0)

        @pl.when(i == 0)
        def _():
            acc_ref[...] = jnp.zeros_like(acc_ref)

        e = y_ref[...] - t_ref[...]
        dy_ref[...] = e * (1.0 / d)
        acc_ref[...] += _colsum(e * e)

        @pl.when(i == steps - 1)
        def _():
            tot = jnp.sum(acc_ref[...], axis=-1, keepdims=True) * (0.5 / d)
            l_ref[...] = jnp.broadcast_to(tot, l_ref.shape)

    row = lambda i: (i, 0)
    return _call(
        body, name="loss_head", grid=(steps,),
        in_specs=[pl.BlockSpec((tm, d), row), pl.BlockSpec((tm, d), row)],
        out_specs=[pl.BlockSpec((tm, d), row), _full((8, LANE))],
        out_shape=[jax.ShapeDtypeStruct((t, d), F32), jax.ShapeDtypeStruct((8, LANE), F32)],
        scratch_shapes=[pltpu.VMEM((1, d), F32)],
    )(y, target)


def _shift_rows(x, s):
    if s == 0:
        return x
    t = x.shape[0]
    ri = _iota2(x.shape, 0)
    rolled = pltpu.roll(x, s % t, axis=0)
    if s > 0:
        return jnp.where(ri >= s, rolled, 0.0)
    return jnp.where(ri < t + s, rolled, 0.0)


def _conv_pre(x, w):
    acc = x * w[DN_CONV - 1:DN_CONV, :]
    for j in range(DN_CONV - 1):
        acc = acc + _shift_rows(x, DN_CONV - 1 - j) * w[j:j + 1, :]
    return acc


def dn_conv(proj, conv_w):
    t = proj.shape[0]
    width = 3 * DN_WIDTH

    def body(x_ref, w_ref, o_ref):
        o_ref[...] = _silu(_conv_pre(x_ref[...], w_ref[...]))

    col = lambda j: (0, j)
    return _call(
        body, name="dn_conv", grid=(width // LANE,),
        in_specs=[pl.BlockSpec((t, LANE), col), pl.BlockSpec((8, LANE), col)],
        out_specs=pl.BlockSpec((t, LANE), col),
        out_shape=jax.ShapeDtypeStruct((t, width), F32), semantics=("parallel",),
    )(proj, conv_w)


def dn_conv_bwd(proj, conv_w, dact, dproj):
    t = proj.shape[0]
    width = 3 * DN_WIDTH

    def body(x_ref, w_ref, d_ref, _, dx_ref, dw_ref):
        x = x_ref[...]
        w = w_ref[...]
        dc = d_ref[...] * _dsilu(_conv_pre(x, w))
        dx = dc * w[DN_CONV - 1:DN_CONV, :]
        rows = []
        for j in range(DN_CONV - 1):
            s = DN_CONV - 1 - j
            dx = dx + _shift_rows(dc, -s) * w[j:j + 1, :]
            rows.append(_colsum(dc * _shift_rows(x, s)))
        rows.append(_colsum(dc * x))
        dx_ref[...] = dx
        ri = _iota2((8, LANE), 0)
        dw = jnp.zeros((8, LANE), F32)
        for j in range(DN_CONV):
            dw = dw + jnp.where(ri == j, rows[j], 0.0)
        dw_ref[...] = dw

    col = lambda j: (0, j)
    return _call(
        body, name="dn_conv_bwd", grid=(width // LANE,),
        in_specs=[pl.BlockSpec((t, LANE), col), pl.BlockSpec((8, LANE), col), pl.BlockSpec((t, LANE), col), ANY],
        out_specs=[pl.BlockSpec((t, LANE), col), pl.BlockSpec((8, LANE), col)],
        out_shape=[jax.ShapeDtypeStruct(dproj.shape, F32), jax.ShapeDtypeStruct((8, width), F32)],
        semantics=("parallel",), aliases={3: 0},
    )(proj, conv_w, dact, dproj)


def _t(x):
    return jnp.swapaxes(x, -1, -2)


def _inv_unit_lower(a):
    c = a.shape[-1]
    eye = (_iota2((c, c), 0) == _iota2((c, c), 1)).astype(F32)
    x = eye - a
    p = _hdot(a, a)
    steps = int(math.log2(c)) - 1
    for i in range(steps):
        x = x + _hdot(x, p)
        if i < steps - 1:
            p = _hdot(p, p)
    return x


def _dn_chunk(q, k, v, a, b, alog, dtb, s_in, tinv=None):
    nh, c, d = q.shape
    rq = lax.rsqrt(_rowsum(q * q) + NORM_EPS)
    rk = lax.rsqrt(_rowsum(k * k) + NORM_EPS)
    qh = q * rq
    kn = k * rk
    qs = qh * (d ** -0.5)
    g = -jnp.exp(alog) * _softplus(a + dtb)
    beta = _sigmoid(b)
    ri = _iota2((c, c), 0)
    ci = _iota2((c, c), 1)
    causal = ri >= ci
    strict = ri > ci
    gb = jnp.broadcast_to(g, (nh, c, d))
    gcb = _xdot(jnp.broadcast_to(causal.astype(F32), (nh, c, c)), gb, exact=0)
    gc = gcb[..., :1]
    gl = _colsum(gb)[..., :1]
    dec = jnp.exp(jnp.where(causal, gc - _t(gcb)[:, :c, :], -1e30))
    kb = kn * beta
    amat = jnp.where(strict, _bdot(kb, kn, 1, 1) * dec, 0.0)
    if tinv is None:
        tinv = _inv_unit_lower(amat)
    e = jnp.exp(gc)
    f = jnp.exp(gl - gc)
    rw = kb * e
    sol = _hdot(tinv, jnp.concatenate([v * beta, rw], axis=-1))
    u = sol[..., :d]
    w = sol[..., d:]
    pmat = jnp.where(causal, _bdot(qs, kn, 1, 1) * dec, 0.0)
    qd = qs * e
    kd = kn * f
    vnew = u - _bdot(w, s_in)
    o = _bdot(qd, s_in) + _bdot(pmat, vnew)
    s_out = s_in * jnp.exp(gl) + _bdot(kd, vnew, 0, 0)
    return dict(rq=rq, rk=rk, qh=qh, kn=kn, qs=qs, g=g, beta=beta, causal=causal, strict=strict, gl=gl,
                dec=dec, kb=kb, amat=amat, tinv=tinv, e=e, f=f, rw=rw, u=u, w=w, pmat=pmat, qd=qd, kd=kd,
                vnew=vnew, o=o, s_out=s_out)


def _dn_chunk_bwd(m, q, v, a, alog, dtb, s_in, do, ds_out):
    nh, c, d = q.shape
    kn, qs, kb, u, w, e, f = m["kn"], m["qs"], m["kb"], m["u"], m["w"], m["e"], m["f"]
    beta, dec, tinv, vnew, kd, qd = m["beta"], m["dec"], m["tinv"], m["vnew"], m["kd"], m["qd"]
    el = jnp.exp(m["gl"])
    dvnew = _bdot(m["pmat"], do, 0, 0) + _bdot(kd, ds_out)
    dp = jnp.where(m["causal"], _bdot(do, vnew, 1, 1), 0.0)
    dqd = _bdot(do, s_in, 1, 1)
    dkd = _bdot(vnew, ds_out, 1, 1)
    ds_in = _bdot(qd, do, 0, 0) + el * ds_out - _bdot(w, dvnew, 0, 0)
    dgl = el * _colsum(_rowsum(s_in * ds_out))
    dw = -_bdot(dvnew, s_in, 1, 1)
    dsol = _hdot(tinv, jnp.concatenate([dvnew, dw], axis=-1), 0, 0)
    dru = dsol[..., :d]
    drw = dsol[..., d:]
    da_m = -jnp.where(m["strict"], _bdot(dsol, jnp.concatenate([u, w], axis=-1), 1, 1), 0.0)
    db_m = da_m * dec
    dq_m = dp * dec
    dkb = _bdot(db_m, kn)
    dkn = _bdot(db_m, kb, 0, 0) + _bdot(dq_m, qs, 0, 0)
    dqs = _bdot(dq_m, kn)
    gmat = da_m * m["amat"] + dp * m["pmat"]
    ones = jnp.ones((nh, c, d), F32)
    dgam = (_xdot(gmat, ones) - _xdot(gmat, ones, 0, 0))[..., :1]
    dqs = dqs + dqd * e
    dgam = dgam + _rowsum(dqd * qd)
    dkn = dkn + dkd * f
    tk = _rowsum(dkd * kd)
    dgam = dgam - tk
    dgl = dgl + _colsum(tk)
    dkb = dkb + drw * e
    dgam = dgam + _rowsum(drw * m["rw"])
    dv = dru * beta
    dbeta = _rowsum(dru * v) + _rowsum(dkb * kn)
    dkn = dkn + dkb * beta
    last = (_iota2((c, 1), 0) == c - 1).astype(F32)
    dgam = dgam + last * dgl
    upper = (_iota2((c, c), 0) <= _iota2((c, c), 1)).astype(F32)
    dg = _xdot(jnp.broadcast_to(upper, (nh, c, c)), jnp.broadcast_to(dgam, (nh, c, d)), exact=0)[..., :1]
    dqh = dqs * (d ** -0.5)
    dq = m["rq"] * (dqh - m["qh"] * _rowsum(dqh * m["qh"]))
    dk = m["rk"] * (dkn - kn * _rowsum(dkn * kn))
    sg = _sigmoid(a + dtb)
    da = dg * (-jnp.exp(alog)) * sg
    dalog = _colsum(dg * m["g"])
    ddtb = _colsum(da)
    db = dbeta * beta * (1.0 - beta)
    return dq, dk, dv, da, db, dalog, ddtb, ds_in


def _dn_gate(o, z, wn):
    ro = lax.rsqrt(jnp.mean(o * o, axis=-1, keepdims=True) + NORM_EPS)
    n = o * ro
    return n, ro, n * wn * _silu(z)


def _heads(ref, col0):
    d = DN_HEAD_DIM
    return jnp.stack([ref[:, col0 + h * d:col0 + (h + 1) * d] for h in range(DN_HEADS)])


def _dn_inputs(act_ref, ab_ref, sc_ref):
    ab = ab_ref[...]
    sc = sc_ref[...]
    q = _heads(act_ref, 0)
    k = _heads(act_ref, DN_WIDTH)
    v = _heads(act_ref, 2 * DN_WIDTH)
    a = jnp.stack([ab[:, h:h + 1] for h in range(DN_HEADS)])
    b = jnp.stack([ab[:, DN_HEADS + h:DN_HEADS + h + 1] for h in range(DN_HEADS)])
    alog = jnp.stack([sc[0:1, h:h + 1] for h in range(DN_HEADS)])
    dtb = jnp.stack([sc[1:2, h:h + 1] for h in range(DN_HEADS)])
    return q, k, v, a, b, alog, dtb


def dn_fwd(act, proj, scal, wn):
    t = act.shape[0]
    n = t // DN_CHUNK
    d = DN_HEAD_DIM

    def body(act_ref, z_ref, ab_ref, sc_ref, wn_ref, y_ref, st_ref, ti_ref, s_ref):
        @pl.when(pl.program_id(0) == 0)
        def _():
            s_ref[...] = jnp.zeros_like(s_ref)

        q, k, v, a, b, alog, dtb = _dn_inputs(act_ref, ab_ref, sc_ref)
        s_in = s_ref[...]
        st_ref[0] = s_in
        m = _dn_chunk(q, k, v, a, b, alog, dtb, s_in)
        ti_ref[0] = m["tinv"]
        s_ref[...] = m["s_out"]
        y = _dn_gate(m["o"], _heads(z_ref, 0), wn_ref[...])[2]
        for h in range(DN_HEADS):
            y_ref[:, h * d:(h + 1) * d] = y[h]

    return _call(
        body, name="dn_fwd", grid=(n,),
        in_specs=[pl.BlockSpec((DN_CHUNK, 3 * DN_WIDTH), lambda i: (i, 0)),
                  pl.BlockSpec((DN_CHUNK, DN_WIDTH), lambda i: (i, C_Z // DN_WIDTH)),
                  pl.BlockSpec((DN_CHUNK, LANE), lambda i: (i, C_AB // LANE)),
                  _full((8, LANE)), _full((1, d))],
        out_specs=[pl.BlockSpec((DN_CHUNK, DN_WIDTH), lambda i: (i, 0)),
                   pl.BlockSpec((1, DN_HEADS, d, d), lambda i: (i, 0, 0, 0)),
                   pl.BlockSpec((1, DN_HEADS, DN_CHUNK, DN_CHUNK), lambda i: (i, 0, 0, 0))],
        out_shape=[jax.ShapeDtypeStruct((t, MIX_WIDTH), F32), jax.ShapeDtypeStruct((n, DN_HEADS, d, d), F32),
                   jax.ShapeDtypeStruct((n, DN_HEADS, DN_CHUNK, DN_CHUNK), F32)],
        scratch_shapes=[pltpu.VMEM((DN_HEADS, d, d), F32)],
    )(act, proj, proj, scal, wn)


def dn_bwd(act, proj, scal, wn, states, tinvs, dy):
    t = act.shape[0]
    n = t // DN_CHUNK
    d = DN_HEAD_DIM
    zab = DN_WIDTH + AB_PAD

    def body(act_ref, z_ref, ab_ref, sc_ref, wn_ref, st_ref, ti_ref, dy_ref, dact_ref, dzab_ref, dpar_ref, ds_ref):
        @pl.when(pl.program_id(0) == 0)
        def _():
            ds_ref[...] = jnp.zeros_like(ds_ref)
            dpar_ref[...] = jnp.zeros_like(dpar_ref)

        wnv = wn_ref[...]
        q, k, v, a, b, alog, dtb = _dn_inputs(act_ref, ab_ref, sc_ref)
        s_in = st_ref[0]
        z = _heads(z_ref, 0)
        dyh = _heads(dy_ref, 0)
        m = _dn_chunk(q, k, v, a, b, alog, dtb, s_in, ti_ref[0])
        nrm, ro, _ = _dn_gate(m["o"], z, wnv)
        sz = _silu(z)
        dz = dyh * nrm * wnv * _dsilu(z)
        dn = dyh * wnv * sz
        dwn = _colsum(dyh * nrm * sz)
        do = ro * (dn - nrm * jnp.mean(dn * nrm, axis=-1, keepdims=True))
        dq, dk, dv, da, db, dalog, ddtb, ds_in = _dn_chunk_bwd(m, q, v, a, alog, dtb, s_in, do, ds_ref[...])
        ds_ref[...] = ds_in
        lane = _iota2((DN_CHUNK, LANE), 1)
        prow = _iota2((8, LANE), 0)
        plane = _iota2((8, LANE), 1)
        dab = jnp.zeros((DN_CHUNK, LANE), F32)
        dpar = jnp.zeros((8, LANE), F32)
        for h in range(DN_HEADS):
            dzab_ref[:, h * d:(h + 1) * d] = dz[h]
            dact_ref[:, h * d:(h + 1) * d] = dq[h]
            dact_ref[:, DN_WIDTH + h * d:DN_WIDTH + (h + 1) * d] = dk[h]
            dact_ref[:, 2 * DN_WIDTH + h * d:2 * DN_WIDTH + (h + 1) * d] = dv[h]
            dab = dab + jnp.where(lane == h, da[h], 0.0) + jnp.where(lane == DN_HEADS + h, db[h], 0.0)
            dpar = dpar + jnp.where((prow == 0) & (plane == h), dalog[h], 0.0)
            dpar = dpar + jnp.where((prow == 1) & (plane == h), ddtb[h], 0.0)
            dpar = dpar + jnp.where(prow == 2, dwn[h], 0.0)
        dzab_ref[:, DN_WIDTH:DN_WIDTH + LANE] = dab
        dzab_ref[:, DN_WIDTH + LANE:] = jnp.zeros((DN_CHUNK, AB_PAD - LANE), F32)
        dpar_ref[...] += dpar

    rev = lambda i: (n - 1 - i, 0)
    return _call(
        body, name="dn_bwd", grid=(n,),
        in_specs=[pl.BlockSpec((DN_CHUNK, 3 * DN_WIDTH), rev),
                  pl.BlockSpec((DN_CHUNK, DN_WIDTH), lambda i: (n - 1 - i, C_Z // DN_WIDTH)),
                  pl.BlockSpec((DN_CHUNK, LANE), lambda i: (n - 1 - i, C_AB // LANE)),
                  _full((8, LANE)), _full((1, d)),
                  pl.BlockSpec((1, DN_HEADS, d, d), lambda i: (n - 1 - i, 0, 0, 0)),
                  pl.BlockSpec((1, DN_HEADS, DN_CHUNK, DN_CHUNK), lambda i: (n - 1 - i, 0, 0, 0)),
                  pl.BlockSpec((DN_CHUNK, DN_WIDTH), rev)],
        out_specs=[pl.BlockSpec((DN_CHUNK, 3 * DN_WIDTH), rev),
                   pl.BlockSpec((DN_CHUNK, zab), lambda i: (n - 1 - i, C_Z // zab)), _full((8, LANE))],
        out_shape=[jax.ShapeDtypeStruct((t, 3 * DN_WIDTH), F32), jax.ShapeDtypeStruct((t, IN_PAD), F32),
                   jax.ShapeDtypeStruct((8, LANE), F32)],
        scratch_shapes=[pltpu.VMEM((DN_HEADS, d, d), F32)],
    )(act, proj, proj, scal, wn, states, tinvs, dy)


_INV_SQRT2 = 0.7071067811865476
_INV_SQRT2PI = 0.3989422804014327


def _gelu(x):
    return 0.5 * x * (1.0 + lax.erf(x * _INV_SQRT2))


def _dgelu(x):
    return 0.5 * (1.0 + lax.erf(x * _INV_SQRT2)) + x * jnp.exp(-0.5 * x * x) * _INV_SQRT2PI


def _gm_core(uv, lng, lnb, ws_ref, bst):
    c = uv.shape[0]
    zz = _gelu(uv)
    u = zz[:, :GM_WIDTH]
    vv = zz[:, GM_WIDTH:]
    xc = vv - jnp.mean(vv, axis=-1, keepdims=True)
    rs = lax.rsqrt(jnp.mean(xc * xc, axis=-1, keepdims=True) + NORM_EPS)
    xh = xc * rs
    vn = xh * lng + lnb
    grp = _iota2((c, GM_WIDTH), 1) // GM_GROUP_DIM
    tril = _iota2((c, c), 0) >= _iota2((c, c), 1)
    sv = jnp.zeros((c, GM_WIDTH), F32)
    masks = []
    for g in range(GM_GROUPS):
        mk = grp == g
        masks.append(mk)
        ws = jnp.where(tril, ws_ref[g], 0.0)
        sv = sv + _bdot(ws, jnp.where(mk, vn, 0.0)) + jnp.where(mk, bst[:, g:g + 1], 0.0)
    return u, xh, rs, vn, sv, masks, tril


def gm_fwd(proj, lng, lnb, w_s, bst, ybuf):
    t = proj.shape[0]

    def body(uv_ref, g_ref, b_ref, ws_ref, bst_ref, _, y_ref):
        u, _, _, _, sv, _, _ = _gm_core(uv_ref[...], g_ref[...], b_ref[...], ws_ref, bst_ref[...])
        y_ref[...] = u * sv

    return _call(
        body, name="gm_fwd", grid=(t // GM_CHUNK,),
        in_specs=[pl.BlockSpec((GM_CHUNK, 2 * GM_WIDTH), lambda i: (i, C_UV // (2 * GM_WIDTH))),
                  _full((1, GM_WIDTH)), _full((1, GM_WIDTH)), _full((GM_GROUPS, GM_CHUNK, GM_CHUNK)),
                  _full((GM_CHUNK, LANE)), ANY],
        out_specs=pl.BlockSpec((GM_CHUNK, GM_WIDTH), lambda i: (i, DN_WIDTH // GM_WIDTH)),
        out_shape=jax.ShapeDtypeStruct(ybuf.shape, F32), semantics=("parallel",), aliases={5: 0},
    )(proj, lng, lnb, w_s, bst, ybuf)


def gm_bwd(proj, lng, lnb, w_s, bst, dy, dproj):
    t = proj.shape[0]

    def body(uv_ref, g_ref, b_ref, ws_ref, bst_ref, dy_ref, _, duv_ref, dws_ref, dbst_ref, dln_ref):
        @pl.when(pl.program_id(0) == 0)
        def _():
            dws_ref[...] = jnp.zeros_like(dws_ref)
            dbst_ref[...] = jnp.zeros_like(dbst_ref)
            dln_ref[...] = jnp.zeros_like(dln_ref)

        uv = uv_ref[...]
        lng = g_ref[...]
        u, xh, rs, vn, sv, masks, tril = _gm_core(uv, lng, b_ref[...], ws_ref, bst_ref[...])
        dyv = dy_ref[...]
        dsv = dyv * u
        lane = _iota2((GM_CHUNK, LANE), 1)
        dvn = jnp.zeros_like(dsv)
        dbst = jnp.zeros((GM_CHUNK, LANE), F32)
        for g in range(GM_GROUPS):
            ws = jnp.where(tril, ws_ref[g], 0.0)
            dsg = jnp.where(masks[g], dsv, 0.0)
            dvn = dvn + jnp.where(masks[g], _bdot(ws, dsv, 0, 0), 0.0)
            dws_ref[g] += jnp.where(tril, _bdot(dsg, vn, 1, 1), 0.0)
            dbst = dbst + jnp.where(lane == g, _rowsum(dsg), 0.0)
        dbst_ref[...] += dbst
        row = _iota2((8, GM_WIDTH), 0)
        dln_ref[...] += jnp.where(row == 0, _colsum(dvn * xh), 0.0) + jnp.where(row == 1, _colsum(dvn), 0.0)
        dxh = dvn * lng
        dvv = rs * (dxh - jnp.mean(dxh, axis=-1, keepdims=True) - xh * jnp.mean(dxh * xh, axis=-1, keepdims=True))
        dg = _dgelu(uv)
        duv_ref[:, :GM_WIDTH] = dyv * sv * dg[:, :GM_WIDTH]
        duv_ref[:, GM_WIDTH:] = dvv * dg[:, GM_WIDTH:]

    return _call(
        body, name="gm_bwd", grid=(t // GM_CHUNK,),
        in_specs=[pl.BlockSpec((GM_CHUNK, 2 * GM_WIDTH), lambda i: (i, C_UV // (2 * GM_WIDTH))),
                  _full((1, GM_WIDTH)), _full((1, GM_WIDTH)), _full((GM_GROUPS, GM_CHUNK, GM_CHUNK)),
                  _full((GM_CHUNK, LANE)),
                  pl.BlockSpec((GM_CHUNK, GM_WIDTH), lambda i: (i, DN_WIDTH // GM_WIDTH)), ANY],
        out_specs=[pl.BlockSpec((GM_CHUNK, 2 * GM_WIDTH), lambda i: (i, C_UV // (2 * GM_WIDTH))),
                   _full((GM_GROUPS, GM_CHUNK, GM_CHUNK)), _full((GM_CHUNK, LANE)), _full((8, GM_WIDTH))],
        out_shape=[jax.ShapeDtypeStruct(dproj.shape, F32),
                   jax.ShapeDtypeStruct((GM_GROUPS, GM_CHUNK, GM_CHUNK), F32),
                   jax.ShapeDtypeStruct((GM_CHUNK, LANE), F32), jax.ShapeDtypeStruct((8, GM_WIDTH), F32)],
        aliases={6: 0},
    )(proj, lng, lnb, w_s, bst, dy, dproj)


def _head_mats():
    r = _iota2((SW_WIDTH, SW_WIDTH), 0)
    c = _iota2((SW_WIDTH, SW_WIDTH), 1)
    same = (r // SW_HEAD_DIM) == (c // SW_HEAD_DIM)
    cc = c % SW_HEAD_DIM
    half = ROPE_DIM // 2
    rot = jnp.where((cc < half) & (r == c + half), -1.0, 0.0) + jnp.where((cc >= half) & (cc < ROPE_DIM) & (r == c - half), 1.0, 0.0)
    return same.astype(F32), rot


def _seg_col(s):
    return C_SW // SW_WIDTH + (s // 2) * 3 + s % 2


def _halves(x):
    return x[:, :LANE], x[:, LANE:]


def sw_prep(proj, nw2, cos_t, sin_t, *, tm=512):
    t = proj.shape[0]

    def body(x_ref, w_ref, c_ref, s_ref, o_ref):
        same, rot = _head_mats()
        x = x_ref[...]
        r = lax.rsqrt(_xdot(x * x, same) * (1.0 / SW_HEAD_DIM) + NORM_EPS)
        xn = x * r * w_ref[0]
        o_ref[0, 0], o_ref[0, 1] = _halves(xn * c_ref[...] + _xdot(xn, rot) * s_ref[...])

    return _call(
        body, name="sw_prep", grid=(6, t // tm),
        in_specs=[pl.BlockSpec((tm, SW_WIDTH), lambda s, i: (i, _seg_col(s))),
                  pl.BlockSpec((1, 1, SW_WIDTH), lambda s, i: (s % 2, 0, 0)),
                  pl.BlockSpec((tm, SW_WIDTH), lambda s, i: (i, 0)),
                  pl.BlockSpec((tm, SW_WIDTH), lambda s, i: (i, 0))],
        out_specs=pl.BlockSpec((1, 2, tm, LANE), lambda s, i: (s, 0, i, 0)),
        out_shape=jax.ShapeDtypeStruct((6, 2, t, LANE), F32), semantics=("parallel", "parallel"),
    )(proj, nw2, cos_t, sin_t)


def sw_prep_bwd(proj, nw2, cos_t, sin_t, dkvq, dproj, dnw, p, *, tm=512):
    t = proj.shape[0]
    col0 = C_SW // SW_WIDTH + 3 * p
    seg_col = lambda s: col0 + (s + 1) % 3

    def body(x_ref, w_ref, c_ref, s_ref, d_ref, _, dw0_ref, dx_ref, dw_ref):
        s = pl.program_id(0)
        dout = jnp.concatenate([d_ref[0, 0], d_ref[0, 1]], axis=1)

        @pl.when(s == 1)
        def _():
            dx_ref[...] = dout

        @pl.when((s != 1) & (pl.program_id(1) == 0))
        def _():
            dw_ref[...] = dw0_ref[...]

        @pl.when(s != 1)
        def _():
            same, rot = _head_mats()
            x = x_ref[...]
            w = w_ref[0]
            r = lax.rsqrt(_xdot(x * x, same) * (1.0 / SW_HEAD_DIM) + NORM_EPS)
            xh = x * r
            dxn = dout * c_ref[...] + _xdot(dout * s_ref[...], rot, 1, 1)
            dw_ref[0] += _colsum(dxn * xh)
            dxh = dxn * w
            dx_ref[...] = r * (dxh - xh * (_xdot(dxh * xh, same) * (1.0 / SW_HEAD_DIM)))

    return _call(
        body, name=f"sw_prep_bwd{p}", grid=(3, t // tm),
        in_specs=[pl.BlockSpec((tm, SW_WIDTH), lambda s, i: (i, seg_col(s))),
                  pl.BlockSpec((1, 1, SW_WIDTH), lambda s, i: (1 - s // 2, 0, 0)),
                  pl.BlockSpec((tm, SW_WIDTH), lambda s, i: (i, 0)),
                  pl.BlockSpec((tm, SW_WIDTH), lambda s, i: (i, 0)),
                  pl.BlockSpec((1, 2, tm, LANE), lambda s, i: (s, 0, i, 0)), ANY,
                  pl.BlockSpec((1, 1, SW_WIDTH), lambda s, i: (s // 2, 0, 0))],
        out_specs=[pl.BlockSpec((tm, SW_WIDTH), lambda s, i: (i, seg_col(s))),
                   pl.BlockSpec((1, 1, SW_WIDTH), lambda s, i: (s // 2, 0, 0))],
        out_shape=[jax.ShapeDtypeStruct(dproj.shape, F32), jax.ShapeDtypeStruct((2, 1, SW_WIDTH), F32)],
        semantics=("arbitrary", "arbitrary"), aliases={5: 0},
    )(proj, nw2, cos_t, sin_t, dkvq, dproj, dnw)


_SW_SCALE = SW_HEAD_DIM ** -0.5
_NEG = -1e30


def _sw_masks(has_other):
    ri = _iota2((SW_BLOCK, SW_BLOCK), 0)
    ci = _iota2((SW_BLOCK, SW_BLOCK), 1)
    return ri >= ci, (ci >= ri) & has_other


def _pair(x):
    first = _iota2((1, LANE), 1) < SW_HEAD_DIM
    return jnp.stack([jnp.where(first, x, 0.0), jnp.where(first, 0.0, x)])


def _both(x):
    return jnp.broadcast_to(x.astype(BF16)[None], (2,) + x.shape)


def _unpair(x2):
    first = _iota2((1, LANE), 1) < SW_HEAD_DIM
    return jnp.where(first, x2[0], x2[1])


def _head_cols(x):
    return jnp.stack([x[:, 0:1], x[:, SW_HEAD_DIM:SW_HEAD_DIM + 1]])


SW_GROUP = 4


def _sw_geometry(t, p):
    dil = SW_DILATIONS[p]
    unit = SW_BLOCK * dil
    nb = max(1, SW_GROUP // dil)
    return dil, unit, nb, t // (unit * nb)


def _sw_groups(dil, nb, body):
    if nb * dil == SW_GROUP:
        body([(k // dil, k % dil) for k in range(SW_GROUP)])
    else:
        def step(g, carry):
            body([(0, SW_GROUP * g + k) for k in range(SW_GROUP)])
            return carry

        lax.fori_loop(0, nb * dil // SW_GROUP, step, 0)


def _sw_rows(i, r, dil):
    start = i * SW_BLOCK * dil + r
    return pl.ds(start, SW_BLOCK) if dil == 1 else pl.ds(start, SW_BLOCK, stride=dil)


def _sw_load(refs, probs, dil, shift, wrap, fn):
    out = []
    for i, r in probs:
        if shift != 0 and i == wrap:
            out.append(fn(refs[1][_sw_rows(0, r, dil), :]))
        else:
            out.append(fn(refs[0][_sw_rows(i + shift, r, dil), :]))
    return jnp.concatenate(out, axis=0)


def _sw_other_masks(probs, wrap, edge_ok):
    _, other = _sw_masks(edge_ok)
    _, always = _sw_masks(True)
    return jnp.stack([other if i == wrap else always for i, _ in probs for _ in range(2)])


def sw_attn(qk, proj, p):
    t = proj.shape[0]
    dil, unit, nb, nsp = _sw_geometry(t, p)
    vcol = (C_SW + 3 * SW_WIDTH * p + 2 * SW_WIDTH) // LANE

    def body(q_ref, kc_ref, kp_ref, vc_ref, vp_ref, o_ref, l_ref):
        mc, _ = _sw_masks(True)
        first = pl.program_id(1) != 0
        q_r, k_r, v_r = (q_ref.at[0, 0], None), (kc_ref.at[0, 0], kp_ref.at[0, 0]), (vc_ref, vp_ref)

        def one(probs):
            mp = _sw_other_masks(probs, 0, first)
            q2 = _sw_load(q_r, probs, dil, 0, 0, _pair)
            sc = jnp.where(mc, _bdot(q2, _sw_load(k_r, probs, dil, 0, 0, _both), 1, 1) * _SW_SCALE, _NEG)
            sp = jnp.where(mp, _bdot(q2, _sw_load(k_r, probs, dil, -1, 0, _both), 1, 1) * _SW_SCALE, _NEG)
            mx = jnp.maximum(jnp.max(sc, axis=-1, keepdims=True), jnp.max(sp, axis=-1, keepdims=True))
            pc = jnp.exp(sc - mx)
            pp = jnp.exp(sp - mx)
            den = _rowsum(pc) + _rowsum(pp)
            o2 = (_bdot(pc, _sw_load(v_r, probs, dil, 0, 0, _both))
                  + _bdot(pp, _sw_load(v_r, probs, dil, -1, 0, _both))) * (1.0 / den)
            l2 = jnp.broadcast_to(mx + jnp.log(den), o2.shape)
            for n, (i, r) in enumerate(probs):
                o_ref.at[0][_sw_rows(i, r, dil), :] = _unpair(o2[2 * n:2 * n + 2])
                l_ref.at[0][_sw_rows(i, r, dil), :] = _unpair(l2[2 * n:2 * n + 2])

        _sw_groups(dil, nb, one)

    before = lambda j: jnp.maximum(j * nb - 1, 0)
    seg = lambda s: pl.BlockSpec((1, 1, unit * nb, LANE), lambda h, j: (s, h, j, 0))
    seg_b = lambda s: pl.BlockSpec((1, 1, unit, LANE), lambda h, j: (s, h, before(j), 0))
    out = pl.BlockSpec((1, unit * nb, LANE), lambda h, j: (h, j, 0))
    shp = jax.ShapeDtypeStruct((2, t, LANE), F32)
    return _call(
        body, name=f"sw_attn{p}", grid=(2, nsp),
        in_specs=[seg(2 * p), seg(2 * p + 1), seg_b(2 * p + 1),
                  pl.BlockSpec((unit * nb, LANE), lambda h, j: (j, vcol + h)),
                  pl.BlockSpec((unit, LANE), lambda h, j: (before(j), vcol + h))],
        out_specs=[out, out], out_shape=[shp, shp], semantics=("parallel", "parallel"),
    )(qk, qk, qk, proj, proj)


def sw_attn_dkv(qk, proj, dy, lg, dm, p):
    t = proj.shape[0]
    dil, unit, nb, nsp = _sw_geometry(t, p)
    nunits = t // unit
    vcol = (C_SW + 3 * SW_WIDTH * p + 2 * SW_WIDTH) // LANE
    ycol = (DN_WIDTH + GM_WIDTH) // LANE

    def body(k_ref, v_ref, qc_ref, qn_ref, doc_ref, don_ref, lc_ref, ln_ref, dc_ref, dn_ref, o_ref):
        mc, _ = _sw_masks(True)
        more = pl.program_id(1) + 1 < nsp
        q_r, do_r = (qc_ref.at[0, 0], qn_ref.at[0, 0]), (doc_ref, don_ref)
        l_r, d_r = (lc_ref.at[0], ln_ref.at[0]), (dc_ref.at[0], dn_ref.at[0])

        def one(probs):
            k2 = _sw_load((k_ref.at[0, 0], None), probs, dil, 0, 0, _both)
            v2 = _sw_load((v_ref, None), probs, dil, 0, 0, _both)
            dk = jnp.zeros((2 * SW_GROUP, SW_BLOCK, LANE), F32)
            dv = jnp.zeros((2 * SW_GROUP, SW_BLOCK, LANE), F32)
            for shift, mk in ((0, mc), (1, _sw_other_masks(probs, nb - 1, more))):
                q2 = _sw_load(q_r, probs, dil, shift, nb - 1, _pair)
                do2 = _sw_load(do_r, probs, dil, shift, nb - 1, _pair)
                lse = _sw_load(l_r, probs, dil, shift, nb - 1, _head_cols)
                dd = _sw_load(d_r, probs, dil, shift, nb - 1, _head_cols)
                pr = jnp.exp(jnp.where(mk, _bdot(q2, k2, 1, 1) * _SW_SCALE, _NEG) - lse)
                dv = dv + _bdot(pr, do2, 0, 0)
                ds = pr * (_bdot(do2, v2, 1, 1) - dd)
                dk = dk + _bdot(ds, q2, 0, 0)
            for n, (i, r) in enumerate(probs):
                o_ref.at[0, 0][_sw_rows(i, r, dil), :] = (dk[2 * n] + dk[2 * n + 1]) * _SW_SCALE
                o_ref.at[1, 0][_sw_rows(i, r, dil), :] = dv[2 * n] + dv[2 * n + 1]

        _sw_groups(dil, nb, one)

    after = lambda j: jnp.minimum((j + 1) * nb, nunits - 1)
    seg = lambda s: pl.BlockSpec((1, 1, unit * nb, LANE), lambda h, j: (s, h, j, 0))
    seg_a = lambda s: pl.BlockSpec((1, 1, unit, LANE), lambda h, j: (s, h, after(j), 0))
    col = lambda c0: pl.BlockSpec((unit * nb, LANE), lambda h, j: (j, c0 + h))
    col_a = lambda c0: pl.BlockSpec((unit, LANE), lambda h, j: (after(j), c0 + h))
    hp = pl.BlockSpec((1, unit * nb, LANE), lambda h, j: (h, j, 0))
    hp_a = pl.BlockSpec((1, unit, LANE), lambda h, j: (h, after(j), 0))
    return _call(
        body, name=f"sw_dkv{p}", grid=(2, nsp),
        in_specs=[seg(2 * p + 1), col(vcol), seg(2 * p), seg_a(2 * p), col(ycol), col_a(ycol), hp, hp_a, hp, hp_a],
        out_specs=pl.BlockSpec((2, 1, unit * nb, LANE), lambda h, j: (0, h, j, 0)),
        out_shape=jax.ShapeDtypeStruct((3, 2, t, LANE), F32), semantics=("parallel", "parallel"),
    )(qk, proj, qk, qk, dy, dy, lg, lg, dm, dm)


def sw_attn_dq(qk, proj, dy, lg, dm, dkvq, p):
    t = proj.shape[0]
    dil, unit, nb, nsp = _sw_geometry(t, p)
    vcol = (C_SW + 3 * SW_WIDTH * p + 2 * SW_WIDTH) // LANE
    ycol = (DN_WIDTH + GM_WIDTH) // LANE

    def body(q_ref, kc_ref, kp_ref, vc_ref, vp_ref, do_ref, l_ref, d_ref, _, dq_ref):
        mc, _ = _sw_masks(True)
        first = pl.program_id(1) != 0
        k_r, v_r = (kc_ref.at[0, 0], kp_ref.at[0, 0]), (vc_ref, vp_ref)

        def one(probs):
            mp = _sw_other_masks(probs, 0, first)
            q2 = _sw_load((q_ref.at[0, 0], None), probs, dil, 0, 0, _pair)
            do2 = _sw_load((do_ref, None), probs, dil, 0, 0, _pair)
            lse = _sw_load((l_ref.at[0], None), probs, dil, 0, 0, _head_cols)
            dd = _sw_load((d_ref.at[0], None), probs, dil, 0, 0, _head_cols)
            kc = _sw_load(k_r, probs, dil, 0, 0, _both)
            kp = _sw_load(k_r, probs, dil, -1, 0, _both)
            pc = jnp.exp(jnp.where(mc, _bdot(q2, kc, 1, 1) * _SW_SCALE, _NEG) - lse)
            pp = jnp.exp(jnp.where(mp, _bdot(q2, kp, 1, 1) * _SW_SCALE, _NEG) - lse)
            dsc = pc * (_bdot(do2, _sw_load(v_r, probs, dil, 0, 0, _both), 1, 1) - dd)
            dsp = pp * (_bdot(do2, _sw_load(v_r, probs, dil, -1, 0, _both), 1, 1) - dd)
            dq2 = (_bdot(dsc, kc) + _bdot(dsp, kp)) * _SW_SCALE
            for n, (i, r) in enumerate(probs):
                dq_ref.at[0, 0][_sw_rows(i, r, dil), :] = _unpair(dq2[2 * n:2 * n + 2])

        _sw_groups(dil, nb, one)

    before = lambda j: jnp.maximum(j * nb - 1, 0)
    seg = lambda s: pl.BlockSpec((1, 1, unit * nb, LANE), lambda h, j: (s, h, j, 0))
    seg_b = lambda s: pl.BlockSpec((1, 1, unit, LANE), lambda h, j: (s, h, before(j), 0))
    col = lambda c0: pl.BlockSpec((unit * nb, LANE), lambda h, j: (j, c0 + h))
    col_b = lambda c0: pl.BlockSpec((unit, LANE), lambda h, j: (before(j), c0 + h))
    hp = pl.BlockSpec((1, unit * nb, LANE), lambda h, j: (h, j, 0))
    return _call(
        body, name=f"sw_dq{p}", grid=(2, nsp),
        in_specs=[seg(2 * p), seg(2 * p + 1), seg_b(2 * p + 1), col(vcol), col_b(vcol), col(ycol), hp, hp, ANY],
        out_specs=pl.BlockSpec((1, 1, unit * nb, LANE), lambda h, j: (2, h, j, 0)),
        out_shape=jax.ShapeDtypeStruct(dkvq.shape, F32), semantics=("parallel", "parallel"), aliases={8: 0},
    )(qk, qk, qk, proj, proj, dy, lg, dm, dkvq)


def sw_merge(outs, lses, ybuf, *, tm=512):
    t = ybuf.shape[0]

    def body(o0, o1, o2, l0_ref, l1_ref, l2_ref, _, y_ref, lg_ref):
        l0, l1, l2 = l0_ref[...], l1_ref[...], l2_ref[...]
        mx = jnp.maximum(jnp.maximum(l0, l1), l2)
        lg = mx + jnp.log(jnp.exp(l0 - mx) + jnp.exp(l1 - mx) + jnp.exp(l2 - mx))
        lg_ref[...] = lg
        y = jnp.exp(l0 - lg) * o0[...] + jnp.exp(l1 - lg) * o1[...] + jnp.exp(l2 - lg) * o2[...]
        y_ref[...] = jnp.concatenate([y[0], y[1]], axis=1)

    hp = pl.BlockSpec((2, tm, LANE), lambda i: (0, i, 0))
    return _call(
        body, name="sw_merge", grid=(t // tm,), in_specs=[hp] * 6 + [ANY],
        out_specs=[pl.BlockSpec((tm, SW_WIDTH), lambda i: (i, (DN_WIDTH + GM_WIDTH) // SW_WIDTH)), hp],
        out_shape=[jax.ShapeDtypeStruct(ybuf.shape, F32), jax.ShapeDtypeStruct((2, t, LANE), F32)],
        semantics=("parallel",), aliases={6: 0},
    )(*outs, *lses, ybuf)


def sw_delta(dy, ybuf, *, tm=512):
    t = ybuf.shape[0]

    def body(dy_ref, y_ref, o_ref):
        same, _ = _head_mats()
        o_ref[0], o_ref[1] = _halves(_xdot(dy_ref[...] * y_ref[...], same))

    b1 = pl.BlockSpec((tm, SW_WIDTH), lambda i: (i, (DN_WIDTH + GM_WIDTH) // SW_WIDTH))
    return _call(body, name="sw_delta", grid=(t // tm,), in_specs=[b1, b1],
                 out_specs=pl.BlockSpec((2, tm, LANE), lambda i: (0, i, 0)),
                 out_shape=jax.ShapeDtypeStruct((2, t, LANE), F32), semantics=("parallel",))(dy, ybuf)


def _rope_tables(t):
    inv = ROPE_THETA ** (-jnp.arange(0, ROPE_DIM, 2, dtype=F32) / ROPE_DIM)
    ang = jnp.arange(t, dtype=F32)[:, None] * inv[None, :]
    pad1 = jnp.ones((t, SW_HEAD_DIM - ROPE_DIM), F32)
    pad0 = jnp.zeros((t, SW_HEAD_DIM - ROPE_DIM), F32)
    cos_h = jnp.concatenate([jnp.cos(ang), jnp.cos(ang), pad1], axis=1)
    sin_h = jnp.concatenate([jnp.sin(ang), jnp.sin(ang), pad0], axis=1)
    return jnp.tile(cos_h, (1, SW_HEADS)), jnp.tile(sin_h, (1, SW_HEADS))


def sw_forward(proj, nw2, cos_t, sin_t, ybuf):
    qk = sw_prep(proj, nw2, cos_t, sin_t)
    outs, lses = [], []
    for p in range(len(SW_DILATIONS)):
        o, lse = sw_attn(qk, proj, p)
        outs.append(o)
        lses.append(lse)
    ybuf, lg = sw_merge(outs, lses, ybuf)
    return ybuf, (qk, lg)


def sw_backward(proj, nw2, cos_t, sin_t, res, ybuf, dy, dproj):
    qk, lg = res
    dm = sw_delta(dy, ybuf)
    dnw = jnp.zeros((2, 1, SW_WIDTH), F32)
    for p in range(len(SW_DILATIONS)):
        dkvq = sw_attn_dkv(qk, proj, dy, lg, dm, p)
        dkvq = sw_attn_dq(qk, proj, dy, lg, dm, dkvq, p)
        dproj, dnw = sw_prep_bwd(proj, nw2, cos_t, sin_t, dkvq, dproj, dnw, p)
    return dproj, dnw[::-1, 0]


def _pad_rows(a, rows):
    return jnp.zeros((rows,) + a.shape[1:], a.dtype).at[:a.shape[0]].set(a)


def _consts(sp):
    d = {}
    d["mix_nw"] = sp["mix_norm_w"][:, None, :]
    d["ffn_nw"] = sp["ffn_norm_w"][:, None, :]
    d["cw8"] = jnp.pad(sp["dn_conv_w"], ((0, 0), (0, 8 - DN_CONV), (0, 0)))
    d["scal"] = jnp.pad(jnp.stack([sp["dn_a_log"], sp["dn_dt_bias"]], axis=1), ((0, 0), (0, 6), (0, LANE - DN_HEADS)))
    d["wn"] = sp["dn_out_norm_w"][:, None, :]
    d["lng"] = sp["gm_ln_g"][:, None, :]
    d["lnb"] = sp["gm_ln_b"][:, None, :]
    d["w_s"] = sp["gm_w_s"]
    d["bst"] = jnp.pad(jnp.swapaxes(sp["gm_b_s"], 1, 2), ((0, 0), (0, 0), (0, LANE - GM_GROUPS)))
    d["nw2"] = jnp.stack([jnp.tile(sp["sw_q_norm_w"], (1, SW_HEADS)),
                          jnp.tile(sp["sw_k_norm_w"], (1, SW_HEADS))], axis=1)[:, :, None, :]
    return d


def _layer_fwd(x, mod, get_w, cs, tabs):
    wb = dict(get_w("w_in", x))
    h1, proj = norm_mm(x, cs["mix_nw"], mod[1], mod[0], wb["w_in"], swiglu=False, name="in_proj")
    act = dn_conv(proj, cs["cw8"])
    y, states, tinvs = dn_fwd(act, proj, cs["scal"], cs["wn"])
    y = gm_fwd(proj, cs["lng"], cs["lnb"], cs["w_s"], cs["bst"], y)
    y, swres = sw_forward(proj, cs["nw2"], *tabs, y)
    wb.update(get_w("w_out", y))
    x1, o1 = resid_mm(y, wb["w_out"], x, mod[2], name="out_proj")
    wb.update(get_w("ffn", x1))
    h2, gu, actf = norm_mm(x1, cs["ffn_nw"], mod[4], mod[3], wb["w_ffn_in"], swiglu=True, name="ffn_in")
    x2, o2 = resid_mm(actf, wb["w_ffn_out"], x1, mod[5], name="ffn_out")
    res = dict(x=x, h1=h1, proj=proj, act=act, states=states, tinvs=tinvs, swres=swres, y=y, x1=x1, o1=o1, h2=h2, gu=gu,
               actf=actf, o2=o2)
    return x2, res, wb


def _layer_bwd(dx2, res, mod, wb, cs, tabs, grads_done):
    dgu, gx2, dgate2 = resid_mm_bwd(dx2, mod[5], res["o2"], wb["w_ffn_out"], res["gu"], name="ffn_out_bwd", tm=256)
    g_wfo = mm_tn(res["actf"], gx2, name="wg_ffn_out")
    g_wfi = mm_tn(res["h2"], dgu, name="wg_ffn_in")
    token = grads_done("ffn", dict(w_ffn_in=g_wfi, w_ffn_out=g_wfo))
    dx1, d_ffn_nw, dscale2, dshift2 = norm_mm_bwd(dgu, wb["w_ffn_in"], res["x1"], cs["ffn_nw"], mod[4] + token, dx2,
                                                  name="ffn_in_bwd")
    dy, gx1, dgate1 = resid_mm_bwd(dx1, mod[2], res["o1"], wb["w_out"], None, name="out_proj_bwd", tm=512)
    g_wout = mm_tn(res["y"], gx1, name="wg_out")
    proj = res["proj"]
    dact, dproj, dpar = dn_bwd(res["act"], proj, cs["scal"], cs["wn"], res["states"], res["tinvs"], dy)
    dproj, dcw = dn_conv_bwd(proj, cs["cw8"], dact, dproj)
    dproj, dws, dbst, dln = gm_bwd(proj, cs["lng"], cs["lnb"], cs["w_s"], cs["bst"], dy, dproj)
    dproj, dnw = sw_backward(proj, cs["nw2"], *tabs, res["swres"], res["y"], dy, dproj)
    g_win = mm_tn(res["h1"], dproj, name="wg_in")
    dx, d_mix_nw, dscale1, dshift1 = norm_mm_bwd(dproj, wb["w_in"], res["x"], cs["mix_nw"], mod[1], dx1,
                                                 name="in_proj_bwd")
    dmod = jnp.concatenate([dshift1, dscale1, dgate1, dshift2, dscale2, dgate2], axis=1)
    dnw = dnw.reshape(2, SW_HEADS, SW_HEAD_DIM).sum(1)
    small = dict(mix_norm_w=d_mix_nw[0], ffn_norm_w=d_ffn_nw[0], dn_conv_w=dcw[:DN_CONV],
                 dn_a_log=dpar[0, :DN_HEADS], dn_dt_bias=dpar[1, :DN_HEADS], dn_out_norm_w=dpar[2],
                 gm_ln_g=dln[0], gm_ln_b=dln[1], gm_w_s=dws, gm_b_s=dbst[:, :GM_GROUPS].T,
                 sw_q_norm_w=dnw[0], sw_k_norm_w=dnw[1])
    token = grads_done("mix", dict(w_in=g_win, w_out=g_wout))
    return dx, small, dmod, token


def _permute_w_in(w):
    pad = jnp.zeros(w.shape[:-1] + (AB_PAD - 8,), w.dtype)
    return jnp.concatenate([w[..., 0:2056], pad, w[..., 2568:IN_WIDTH], w[..., 2056:2568]], axis=-1)


def _unpermute_w_in(g):
    return jnp.concatenate([g[..., 0:2056], g[..., C_UV:IN_PAD], g[..., C_SW:C_UV]], axis=-1)


def _local_step(x, target, mods, weights_of, grads_done, sp):
    layers = mods.shape[0]
    t, d = x.shape
    tabs = _rope_tables(t)
    consts = _consts(sp)
    saved = []
    for layer in range(layers):
        mod = mods[layer].reshape(6, 1, d)
        cs = {k: v[layer] for k, v in consts.items()}
        x, res, wb = _layer_fwd(x, mod, functools.partial(weights_of, layer), cs, tabs)
        saved.append((res, mod, wb, cs))
    dx, loss = loss_head(x, target)
    smalls, dmods = [], []
    token = jnp.zeros((1, 1), F32)
    for layer in reversed(range(layers)):
        res, mod, wb, cs = saved[layer]
        dx, small, dmod, token = _layer_bwd(dx, res, mod + token, wb, cs, tabs, functools.partial(grads_done, layer))
        smalls.append(small)
        dmods.append(dmod[0])
    smalls, dmods = smalls[::-1], dmods[::-1]
    small = {k: jnp.stack([s[k] for s in smalls]) for k in smalls[0]}
    return loss, dx, small, jnp.stack(dmods)


def mod_fwd(c_all, w_mod, b_shard):
    layers, d, n = w_mod.shape

    def body(c_ref, w_ref, b_ref, o_ref):
        ca = _silu(c_ref[...]).astype(BF16)
        o_ref[0] = _dot(ca, w_ref[0].astype(BF16), 1, 0) + b_ref[0]

    return _call(
        body, name="mod_fwd", grid=(layers,),
        in_specs=[_full((8, d)), pl.BlockSpec((1, d, n), lambda i: (i, 0, 0)),
                  pl.BlockSpec((1, 1, n), lambda i: (i, 0, 0))],
        out_specs=pl.BlockSpec((1, 8, n), lambda i: (i, 0, 0)),
        out_shape=jax.ShapeDtypeStruct((layers, 8, n), F32), semantics=("parallel",),
    )(c_all, w_mod, b_shard)


def mod_bwd(c_all, dmod):
    layers, _, n = dmod.shape
    d = c_all.shape[1]

    def body(c_ref, g_ref, o_ref):
        ca = _silu(c_ref[...]).astype(BF16)
        o_ref[0] = _dot(ca, g_ref[0].astype(BF16), 0, 0)

    return _call(
        body, name="mod_bwd", grid=(layers,),
        in_specs=[_full((8, d)), pl.BlockSpec((1, 8, n), lambda i: (i, 0, 0))],
        out_specs=pl.BlockSpec((1, d, n), lambda i: (i, 0, 0)),
        out_shape=jax.ShapeDtypeStruct((layers, d, n), F32), semantics=("parallel",),
    )(c_all, dmod)


N_DEV = 8


def _place():
    return lax.axis_index("x"), lax.axis_index("y"), lax.axis_index("c")


def _other_chips(x, y):
    return [(1 - x, y), (x, 1 - y), (1 - x, 1 - y)]


def allgather8(x_shard, *, name):
    m_per, n = x_shard.shape

    def body(x_ref, out_ref, send_sems, recv_sems, local_sem):
        x, y, c = _place()
        me, sibling = (x, y, c), (x, y, 1 - c)
        chips = _other_chips(x, y)

        def rows(px, py, pc):
            return out_ref.at[pl.ds((4 * px + 2 * py + pc) * m_per, m_per), :]

        def copy(k, block, to, src=None):
            return pltpu.make_async_remote_copy(
                src_ref=rows(*block) if src is None else src, dst_ref=rows(*block),
                send_sem=send_sems.at[k], recv_sem=recv_sems.at[k], device_id=to, device_id_type=MESH)

        mine = pltpu.make_async_copy(x_ref, rows(*me), local_sem)
        mine.start()
        first = [copy(0, me, sibling, src=x_ref)]
        first += [copy(1 + j, me, (*chip, c), src=x_ref) for j, chip in enumerate(chips)]
        for cp in first:
            cp.start()
        passed = [copy(4 + j, (*chip, c), sibling) for j, chip in enumerate(chips)]
        for j, chip in enumerate(chips):
            copy(1 + j, (*chip, c), me).wait_recv()
            passed[j].start()
        copy(0, sibling, me).wait_recv()
        for j, chip in enumerate(chips):
            copy(4 + j, (*chip, 1 - c), me).wait_recv()
        for cp in first + passed:
            cp.wait_send()
        mine.wait()

    return pl.pallas_call(
        body, name=name, out_shape=jax.ShapeDtypeStruct((N_DEV * m_per, n), x_shard.dtype),
        in_specs=[pl.BlockSpec(memory_space=pltpu.VMEM)], out_specs=pl.BlockSpec(memory_space=pltpu.VMEM),
        scratch_shapes=[pltpu.SemaphoreType.DMA((7,)), pltpu.SemaphoreType.DMA((7,)), pltpu.SemaphoreType.DMA],
    )(x_shard)


HBM = pl.BlockSpec(memory_space=pltpu.HBM)
SEM = pl.BlockSpec(memory_space=pltpu.SEMAPHORE)
_EFFECT = pltpu.SideEffectType.DATAFLOW_SIDE_EFFECTING


def _piece(ref, sliced, chip):
    return ref.at[2 * chip[0] + chip[1]] if sliced else ref


def exchange_start(srcs, after, *, sliced, name):
    n = len(srcs)
    piece = lambda s: s.shape[1:] if sliced else s.shape

    def body(*refs):
        ins, lands = refs[:n], refs[n:2 * n]
        send_sems, recv_sems = refs[2 * n + len(after):2 * n + len(after) + 2]
        token = refs[-1]
        x, y, c = _place()
        me_s = 2 * x + y
        for a in range(n):
            for j, chip in enumerate(_other_chips(x, y)):
                pltpu.make_async_remote_copy(
                    src_ref=_piece(ins[a], sliced, chip), dst_ref=lands[a].at[me_s], send_sem=send_sems.at[3 * a + j],
                    recv_sem=recv_sems.at[3 * a + j], device_id=(*chip, c), device_id_type=MESH).start()
        token[...] = jnp.zeros_like(token)

    zones = [pltpu.with_memory_space_constraint(lax.empty((4,) + piece(s), s.dtype), pltpu.HBM) for s in srcs]
    srcs = [pltpu.with_memory_space_constraint(s, pltpu.HBM) for s in srcs]
    out = pl.pallas_call(
        body, name=name,
        out_shape=(pltpu.SemaphoreType.DMA((3 * n,)), pltpu.SemaphoreType.DMA((3 * n,)),
                   *[pltpu.HBM(s.shape, s.dtype) for s in srcs], *[pltpu.HBM(z.shape, z.dtype) for z in zones],
                   jax.ShapeDtypeStruct((8, LANE), F32)),
        in_specs=[HBM] * (2 * n) + [ANY] * len(after),
        out_specs=(SEM, SEM, *[HBM] * (2 * n), pl.BlockSpec(memory_space=pltpu.VMEM)),
        input_output_aliases={i: 2 + i for i in range(2 * n)},
        compiler_params=pltpu.CompilerParams(has_side_effects=_EFFECT),
    )(*srcs, *zones, *after)
    return out[0], out[1], out[2:2 + n], out[2 + n:2 + 2 * n], out[-1]


def exchange_wait(send_sems, recv_sems, srcs, zones, after, *, which, sliced, name):
    n = len(srcs)

    def body(*refs):
        ins, lands = refs[:n], refs[n:2 * n]
        send_sems, recv_sems = refs[2 * n:2 * n + 2]
        x, y, c = _place()
        for a in range(n):
            for j, chip in enumerate(_other_chips(x, y)):
                copy = pltpu.make_async_remote_copy(
                    src_ref=_piece(ins[a], sliced, chip), dst_ref=lands[a].at[2 * chip[0] + chip[1]],
                    send_sem=send_sems.at[3 * which[a] + j], recv_sem=recv_sems.at[3 * which[a] + j],
                    device_id=(*chip, c), device_id_type=MESH)
                copy.wait_send()
                copy.wait_recv()

    out = pl.pallas_call(
        body, name=name,
        out_shape=tuple(pltpu.HBM(s.shape, s.dtype) for s in (*srcs, *zones)),
        in_specs=[HBM] * (2 * n) + [SEM, SEM, ANY], out_specs=tuple([HBM] * (2 * n)),
        input_output_aliases={i: i for i in range(2 * n)},
        compiler_params=pltpu.CompilerParams(has_side_effects=_EFFECT),
    )(*srcs, *zones, send_sems, recv_sems, after)
    return out[n:]


def sibling_swap(parts):
    n = len(parts)

    def body(*refs):
        ins, outs = refs[:n], refs[n:2 * n]
        send_sems, recv_sems = refs[2 * n:]
        x, y, c = _place()
        cps = []
        for a in range(n):
            cp = pltpu.make_async_remote_copy(
                src_ref=ins[a], dst_ref=outs[a], send_sem=send_sems.at[a], recv_sem=recv_sems.at[a],
                device_id=(x, y, 1 - c), device_id_type=MESH)
            cp.start()
            cps.append(cp)
        for cp in cps:
            cp.wait()

    return pl.pallas_call(
        body, name="sibling_swap", out_shape=[jax.ShapeDtypeStruct(p.shape, p.dtype) for p in parts],
        in_specs=[ANY] * n, out_specs=[ANY] * n,
        scratch_shapes=[pltpu.SemaphoreType.DMA((n,)), pltpu.SemaphoreType.DMA((n,))],
    )(*parts)


def _row_block(rows, cols, budget=1 << 20):
    best = rows if rows % 8 else 8
    for tr in range(8, rows + 1, 8):
        if rows % tr == 0 and tr * cols * 4 <= budget:
            best = tr
    return best


def chip_sum(own, recv, me_s, buf, layer, layers, *, name):
    _, r, n = own.shape
    tr = _row_block(r, n)
    steps = r // tr

    def body(me_ref, own_ref, recv_ref, *rest):
        o_ref = rest[-1]
        me = me_ref[0]
        acc = jnp.zeros((tr, n), F32)
        for s in range(4):
            acc = acc + jnp.where(me == s, own_ref[0], recv_ref[s].astype(F32))
        o_ref[...] = acc

    in_specs = [pl.BlockSpec((1, tr, n), lambda i, me: (me[0], i, 0)), pl.BlockSpec((4, tr, n), lambda i, me: (0, i, 0))]
    args = [me_s, own, recv]
    aliases = {}
    if buf is not None:
        in_specs.append(ANY)
        args.append(buf)
        aliases = {3: 0}
    return pl.pallas_call(
        body, name=name, out_shape=jax.ShapeDtypeStruct((layers * r, n), F32),
        grid_spec=pltpu.PrefetchScalarGridSpec(
            num_scalar_prefetch=1, grid=(steps,), in_specs=in_specs,
            out_specs=pl.BlockSpec((tr, n), lambda i, me: (layer * steps + i, 0))),
        input_output_aliases=aliases,
        compiler_params=pltpu.CompilerParams(dimension_semantics=("parallel",)),
    )(*args)


def _adam_update(w, g, m, v):
    m2 = ADAM_B1 * m + (1.0 - ADAM_B1) * g
    v2 = ADAM_B2 * v + (1.0 - ADAM_B2) * (g * g)
    m_hat = m2 / (1.0 - ADAM_B1 ** ADAM_STEP)
    v_hat = v2 / (1.0 - ADAM_B2 ** ADAM_STEP)
    delta = -ADAM_LR * (m_hat / (jnp.sqrt(v_hat) + ADAM_EPS) + ADAM_WD * w)
    return delta, m2, v2


def adamw(w, g_parts, m, v, *, name):
    r, n = w.shape
    tr = _row_block(r, n)
    k = len(g_parts)

    def body(*refs):
        w_ref, m_ref, v_ref = refs[k], refs[k + 1], refs[k + 2]
        g_ref, d_ref, m2_ref, v2_ref = refs[k + 3:]
        g = refs[0][...]
        for p in refs[1:k]:
            g = g + p[...]
        g_ref[...] = g
        d_ref[...], m2_ref[...], v2_ref[...] = _adam_update(w_ref[...], g, m_ref[...], v_ref[...])

    blk = pl.BlockSpec((tr, n), lambda i: (i, 0))
    shp = jax.ShapeDtypeStruct((r, n), F32)
    return _call(body, name=name, grid=(r // tr,), in_specs=[blk] * (k + 3), out_specs=[blk] * 4,
                 out_shape=[shp] * 4, semantics=("parallel",))(*g_parts, w, m, v)


def adamw_gathered(g_all, w, m, v, *, name):
    _, r, n = g_all.shape
    tr = _row_block(r, n * 4)

    def body(ga_ref, w_ref, m_ref, v_ref, g_ref, d_ref, m2_ref, v2_ref):
        g = ga_ref[0]
        for dev in range(1, N_DEV):
            g = g + ga_ref[dev]
        g_ref[...] = g
        d_ref[...], m2_ref[...], v2_ref[...] = _adam_update(w_ref[...], g, m_ref[...], v_ref[...])

    blk = pl.BlockSpec((tr, n), lambda i: (i, 0))
    shp = jax.ShapeDtypeStruct((r, n), F32)
    return _call(body, name=name, grid=(r // tr,),
                 in_specs=[pl.BlockSpec((N_DEV, tr, n), lambda i: (0, i, 0)), blk, blk, blk], out_specs=[blk] * 4,
                 out_shape=[shp] * 4, semantics=("parallel",))(g_all, w, m, v)


BIG = ("w_in", "w_out", "w_ffn_in", "w_ffn_out")
SMALL = ("b_mod", "mix_norm_w", "ffn_norm_w", "dn_conv_w", "dn_a_log", "dn_dt_bias", "dn_out_norm_w", "gm_ln_g",
         "gm_ln_b", "gm_w_s", "gm_b_s", "sw_q_norm_w", "sw_k_norm_w")
WEIGHTS = ("w_mod", "b_mod", "mix_norm_w", "ffn_norm_w", "w_in", "w_out", "dn_conv_w", "dn_a_log", "dn_dt_bias",
           "dn_out_norm_w", "gm_ln_g", "gm_ln_b", "gm_w_s", "gm_b_s", "sw_q_norm_w", "sw_k_norm_w", "w_ffn_in",
           "w_ffn_out")
PACK_ROWS = 8


def _pack(arrs):
    out = []
    for a in arrs:
        flat = a.reshape(-1).astype(F32)
        rows = -(-flat.shape[0] // (LANE * PACK_ROWS)) * PACK_ROWS
        out.append(jnp.pad(flat, (0, rows * LANE - flat.shape[0])).reshape(rows, LANE))
    return jnp.concatenate(out, axis=0)


def _unpack(packed, shapes):
    out, r0 = [], 0
    for shp in shapes:
        size = math.prod(shp)
        rows = -(-size // (LANE * PACK_ROWS)) * PACK_ROWS
        out.append(packed[r0:r0 + rows].reshape(-1)[:size].reshape(shp))
        r0 += rows
    return out


def kernel(x, c, w_mod, b_mod, mix_norm_w, ffn_norm_w, w_in, w_out, dn_conv_w, dn_a_log, dn_dt_bias, dn_out_norm_w, gm_ln_g, gm_ln_b, gm_w_s, gm_b_s, sw_q_norm_w, sw_k_norm_w, w_ffn_in, w_ffn_out, loss_target, m_w_mod, m_b_mod, m_mix_norm_w, m_ffn_norm_w, m_w_in, m_w_out, m_dn_conv_w, m_dn_a_log, m_dn_dt_bias, m_dn_out_norm_w, m_gm_ln_g, m_gm_ln_b, m_gm_w_s, m_gm_b_s, m_sw_q_norm_w, m_sw_k_norm_w, m_w_ffn_in, m_w_ffn_out, v_w_mod, v_b_mod, v_mix_norm_w, v_ffn_norm_w, v_w_in, v_w_out, v_dn_conv_w, v_dn_a_log, v_dn_dt_bias, v_dn_out_norm_w, v_gm_ln_g, v_gm_ln_b, v_gm_w_s, v_gm_b_s, v_sw_q_norm_w, v_sw_k_norm_w, v_w_ffn_in, v_w_ffn_out):
    w = dict(w_mod=w_mod, b_mod=b_mod, mix_norm_w=mix_norm_w, ffn_norm_w=ffn_norm_w, w_in=w_in, w_out=w_out,
             dn_conv_w=dn_conv_w, dn_a_log=dn_a_log, dn_dt_bias=dn_dt_bias, dn_out_norm_w=dn_out_norm_w,
             gm_ln_g=gm_ln_g, gm_ln_b=gm_ln_b, gm_w_s=gm_w_s, gm_b_s=gm_b_s, sw_q_norm_w=sw_q_norm_w,
             sw_k_norm_w=sw_k_norm_w, w_ffn_in=w_ffn_in, w_ffn_out=w_ffn_out)
    m = dict(w_mod=m_w_mod, b_mod=m_b_mod, mix_norm_w=m_mix_norm_w, ffn_norm_w=m_ffn_norm_w, w_in=m_w_in,
             w_out=m_w_out, dn_conv_w=m_dn_conv_w, dn_a_log=m_dn_a_log, dn_dt_bias=m_dn_dt_bias,
             dn_out_norm_w=m_dn_out_norm_w, gm_ln_g=m_gm_ln_g, gm_ln_b=m_gm_ln_b, gm_w_s=m_gm_w_s, gm_b_s=m_gm_b_s,
             sw_q_norm_w=m_sw_q_norm_w, sw_k_norm_w=m_sw_k_norm_w, w_ffn_in=m_w_ffn_in, w_ffn_out=m_w_ffn_out)
    v = dict(w_mod=v_w_mod, b_mod=v_b_mod, mix_norm_w=v_mix_norm_w, ffn_norm_w=v_ffn_norm_w, w_in=v_w_in,
             w_out=v_w_out, dn_conv_w=v_dn_conv_w, dn_a_log=v_dn_a_log, dn_dt_bias=v_dn_dt_bias,
             dn_out_norm_w=v_dn_out_norm_w, gm_ln_g=v_gm_ln_g, gm_ln_b=v_gm_ln_b, gm_w_s=v_gm_w_s, gm_b_s=v_gm_b_s,
             sw_q_norm_w=v_sw_q_norm_w, sw_k_norm_w=v_sw_k_norm_w, w_ffn_in=v_w_ffn_in, w_ffn_out=v_w_ffn_out)
    layers, d, mod_n = w_mod.shape
    mx, my, mc = _place()
    me_s = 2 * mx + my
    me_dev = 4 * mx + 2 * my + mc

    c_all = allgather8(_pad_rows(c, 8), name="gather_c").reshape(N_DEV, 8, d)[:, 0]
    b_shard = lax.dynamic_slice_in_dim(b_mod, me_s * mod_n, mod_n, axis=1)[:, None, :]
    mod_part = mod_fwd(c_all, w_mod, b_shard)
    mod_parts = allgather8(mod_part.reshape(layers * 8, mod_n), name="gather_mod")
    mod_parts = mod_parts.reshape(4, 2, layers, 8, mod_n)[:, 0]
    mod_all = mod_parts.transpose(1, 2, 0, 3).reshape(layers, 8, 4 * mod_n)
    mods = lax.dynamic_index_in_dim(mod_all, me_dev, axis=1, keepdims=False)

    cw = dn_conv_w.shape[-1]
    conv_rows = -(-layers * DN_CONV // 8) * 8
    conv_parts = allgather8(_pad_rows(dn_conv_w.reshape(layers * DN_CONV, cw), conv_rows), name="gather_conv")
    conv_parts = conv_parts.reshape(4, 2, conv_rows, cw)[:, 0, :layers * DN_CONV]
    conv_full = conv_parts.reshape(4, layers, DN_CONV, cw).transpose(1, 2, 0, 3).reshape(layers, DN_CONV, 4 * cw)

    shards = {k: w[k].astype(BF16) for k in BIG}
    groups = dict(w_in=(0,), w_out=(1,), ffn=(2, 3))
    gathers = [exchange_start([shards[k][layer] for k in BIG], [mods, conv_full], sliced=False, name=f"gather_start{layer}")
               for layer in range(layers)]
    mods = mods + sum(g[4][0, 0] for g in gathers)

    def weights_of(layer, group, after):
        send_sems, recv_sems, srcs, zones, _ = gathers[layer]
        which = groups[group]
        got = exchange_wait(send_sems, recv_sems, [srcs[a] for a in which], [zones[a] for a in which], after,
                            which=which, sliced=False, name=f"gather_wait_{group}{layer}")
        full = {BIG[a]: lax.dynamic_update_index_in_dim(z, shards[BIG[a]][layer], me_s, 0) for a, z in zip(which, got)}
        cols = lambda g: jnp.concatenate([g[s] for s in range(4)], axis=-1)
        shape = dict(w_in=lambda g: _permute_w_in(cols(g)), w_out=lambda g: g.reshape(-1, d), w_ffn_in=cols,
                     w_ffn_out=lambda g: g.reshape(-1, d))
        return {k: shape[k](g) for k, g in full.items()}

    scatters = {}
    cut = lambda g, axis: jnp.stack(jnp.split(g, 4, axis=axis))
    shard_major = dict(w_in=lambda g: cut(_unpermute_w_in(g), 1), w_out=lambda g: cut(g, 0),
                       w_ffn_in=lambda g: cut(g, 1), w_ffn_out=lambda g: cut(g, 0))

    def grads_done(layer, group, grads):
        own = {k: shard_major[k](g) for k, g in grads.items()}
        started = exchange_start([g.astype(BF16) for g in own.values()], [], sliced=True,
                                 name=f"scatter_start_{group}{layer}")
        scatters[layer, group] = (started, own)
        return started[4][:1, :1]

    sp = {k: w[k] for k in SMALL}
    sp["dn_conv_w"] = conv_full
    loss_blk, grad_x, small, dmods = _local_step(x[0], loss_target[0], mods, weights_of, grads_done, sp)
    loss = lax.psum(loss_blk[0, 0], ("x", "y", "c"))

    outs = {}
    small = dict(small, b_mod=dmods)
    packed = _pack([small[k] for k in SMALL])
    rows = packed.shape[0]
    g_all = allgather8(packed, name="gather_small").reshape(N_DEV, rows, LANE)
    conv_zero = jnp.zeros((layers, DN_CONV, 3 * DN_WIDTH), F32)
    pk = lambda src: _pack([conv_zero if k == "dn_conv_w" else src[k] for k in SMALL])
    res = adamw_gathered(g_all, pk(w), pk(m), pk(v), name="adamw_small")
    shapes = [small[k].shape for k in SMALL]
    un = [_unpack(a, shapes) for a in res]
    for i, k in enumerate(SMALL):
        outs[k] = [un[j][i] for j in range(4)]
    g_conv = lax.dynamic_slice_in_dim(outs["dn_conv_w"][0], me_s * cw, cw, axis=2)
    flat = lambda a: a.reshape(-1, cw)
    res = adamw(flat(dn_conv_w), [flat(g_conv)], flat(m["dn_conv_w"]), flat(v["dn_conv_w"]), name="adamw_conv")
    outs["dn_conv_w"] = [a.reshape(dn_conv_w.shape) for a in res]

    b_rows = layers * 6 * d // LANE
    dmod_all = g_all[:, :b_rows].reshape(N_DEV, layers, 6 * d).transpose(1, 0, 2)
    dmod_shard = lax.dynamic_slice_in_dim(dmod_all, me_s * mod_n, mod_n, axis=2)
    g_wmod = mod_bwd(c_all, dmod_shard)
    flat = lambda a: a.reshape(-1, mod_n)
    res = adamw(flat(w_mod), [flat(g_wmod)], flat(m_w_mod), flat(v_w_mod), name="adamw_w_mod")
    outs["w_mod"] = [a.reshape(w_mod.shape) for a in res]

    me_arr = jnp.reshape(me_s, (1,)).astype(jnp.int32)
    partial = {k: None for k in BIG}
    for layer in range(layers):
        for group in ("ffn", "mix"):
            (send_sems, recv_sems, srcs, zones, _), own = scatters[layer, group]
            zones = exchange_wait(send_sems, recv_sems, srcs, zones, res[0], which=tuple(range(len(srcs))),
                                  sliced=True, name=f"scatter_wait_{group}{layer}")
            for k, z in zip(own, zones):
                partial[k] = chip_sum(own[k], z, me_arr, partial[k], layer, layers, name=f"chip_sum_{k}{layer}")
    partial = [partial[k] for k in BIG]
    theirs = sibling_swap(partial)
    for k, mine, other in zip(BIG, partial, theirs):
        shp = w[k].shape
        flat = lambda a: a.reshape(-1, shp[-1])
        res = adamw(flat(w[k]), [mine, other], flat(m[k]), flat(v[k]), name="adamw_" + k)
        outs[k] = [a.reshape(shp) for a in res]

    result = [loss, grad_x[None]]
    for j in range(4):
        result += [outs[k][j] for k in WEIGHTS]
    return tuple(result)
```

```python
import functools
import math

import jax
import jax.numpy as jnp
from jax import lax
from jax.experimental import pallas as pl
from jax.experimental.pallas import tpu as pltpu

F32 = jnp.float32
BF16 = jnp.bfloat16
HI = lax.Precision.HIGH

NORM_EPS = 1e-6
DN_HEADS = 4
DN_HEAD_DIM = 128
DN_WIDTH = 512
DN_CHUNK = 64
DN_CONV = 4
GM_WIDTH = 256
GM_GROUPS = 4
GM_GROUP_DIM = 64
GM_CHUNK = 128
SW_HEADS = 4
SW_HEAD_DIM = 64
SW_WIDTH = 256
SW_DILATIONS = (1, 4, 16)
SW_BLOCK = 128
ROPE_THETA = 500000.0
ROPE_DIM = 16
LANE = 128

C_QKV = 0
C_Z = 1536
C_AB = 2048
C_SW = 2304
C_UV = 4608
IN_WIDTH = 4872
IN_PAD = 5120
AB_PAD = C_SW - C_AB
MIX_WIDTH = 1024

ADAM_LR = 0.001
ADAM_B1 = 0.9
ADAM_B2 = 0.999
ADAM_EPS = 1e-08
ADAM_WD = 0.01
ADAM_STEP = 10

MESH = pl.DeviceIdType.MESH


BIG_VMEM = 56 << 20


def _call(body, *, name, grid, in_specs, out_specs, out_shape, scratch_shapes=(), semantics=None, aliases=None,
          vmem=None):
    if semantics is None:
        semantics = ("arbitrary",) * len(grid)
    return pl.pallas_call(
        body, name=name, grid=grid, in_specs=in_specs, out_specs=out_specs, out_shape=out_shape,
        scratch_shapes=list(scratch_shapes), input_output_aliases=aliases or {},
        compiler_params=pltpu.CompilerParams(dimension_semantics=semantics, vmem_limit_bytes=vmem),
    )


def _dot(a, b, ca, cb, prec=None):
    if a.ndim == 3:
        dims = (((ca + 1,), (cb + 1,)), ((0,), (0,)))
    else:
        dims = (((ca,), (cb,)), ((), ()))
    return lax.dot_general(a, b, dims, preferred_element_type=F32, precision=prec)


def _bdot(a, b, ca=1, cb=0):
    return _dot(a.astype(BF16), b.astype(BF16), ca, cb)


def _hdot(a, b, ca=1, cb=0):
    return _dot(a.astype(F32), b.astype(F32), ca, cb, HI)


def _split(x):
    hi = x.astype(BF16)
    return hi, (x - hi.astype(F32)).astype(BF16)


def _xdot(a, b, ca=1, cb=0, exact=1):
    if exact == 1:
        hi, lo = _split(a)
        e = b.astype(BF16)
        return _dot(hi, e, ca, cb) + _dot(lo, e, ca, cb)
    hi, lo = _split(b)
    e = a.astype(BF16)
    return _dot(e, hi, ca, cb) + _dot(e, lo, ca, cb)


def _sigmoid(x):
    return 0.5 * jnp.tanh(0.5 * x) + 0.5


def _silu(x):
    return x * _sigmoid(x)


def _dsilu(x):
    s = _sigmoid(x)
    return s * (1.0 + x * (1.0 - s))


def _softplus(x):
    return jnp.maximum(x, 0.0) + jnp.log(1.0 + jnp.exp(-jnp.abs(x)))


def _iota2(shape, dim):
    return lax.broadcasted_iota(jnp.int32, shape, dim)


def _rowsum(x):
    return jnp.sum(x, axis=-1, keepdims=True)


def _colsum(x):
    return jnp.sum(x, axis=-2, keepdims=True)


def _full(shape):
    return pl.BlockSpec(shape, lambda *_: (0,) * len(shape))


def _resident(shape):
    return pl.BlockSpec(shape, lambda *_: (0,) * len(shape), pipeline_mode=pl.Buffered(1))


ANY = pl.BlockSpec(memory_space=pl.ANY)


def _norm_mod(x, nw, scale, shift):
    r = lax.rsqrt(jnp.mean(x * x, axis=-1, keepdims=True) + NORM_EPS)
    xn = x * r
    return xn, r, (xn * nw) * (1.0 + scale) + shift


def norm_mm(x, nw, scale, shift, w, *, swiglu, name, tm=512):
    t, d = x.shape
    n = w.shape[1]
    half = n // 2

    def body(x_ref, nw_ref, sc_ref, sh_ref, w_ref, h_ref, y_ref, *act_ref):
        _, _, h = _norm_mod(x_ref[...], nw_ref[...], sc_ref[...], sh_ref[...])
        hb = h.astype(BF16)
        h_ref[...] = hb
        y = _dot(hb, w_ref[...], 1, 0)
        y_ref[...] = y.astype(y_ref.dtype)
        if swiglu:
            act_ref[0][...] = (_silu(y[:, :half]) * y[:, half:]).astype(BF16)

    row = lambda i: (i, 0)
    out_shape = [jax.ShapeDtypeStruct((t, d), BF16), jax.ShapeDtypeStruct((t, n), BF16 if swiglu else F32)]
    out_specs = [pl.BlockSpec((tm, d), row), pl.BlockSpec((tm, n), row)]
    if swiglu:
        out_shape.append(jax.ShapeDtypeStruct((t, half), BF16))
        out_specs.append(pl.BlockSpec((tm, half), row))
    return _call(
        body, name=name, grid=(t // tm,),
        in_specs=[pl.BlockSpec((tm, d), row), _full((1, d)), _full((1, d)), _full((1, d)), _resident((d, n))],
        out_specs=out_specs, out_shape=out_shape, semantics=("parallel",), vmem=BIG_VMEM,
    )(x, nw, scale, shift, w)


def resid_mm(y, w, x, gate, *, name, tm=512):
    t, k = y.shape
    d = w.shape[1]

    def body(y_ref, w_ref, x_ref, g_ref, xo_ref, o_ref):
        o = _dot(y_ref[...].astype(BF16), w_ref[...], 1, 0)
        o_ref[...] = o
        xo_ref[...] = x_ref[...] + g_ref[...] * o

    row = lambda i: (i, 0)
    return _call(
        body, name=name, grid=(t // tm,),
        in_specs=[pl.BlockSpec((tm, k), row), _resident((k, d)), pl.BlockSpec((tm, d), row), _full((1, d))],
        out_specs=[pl.BlockSpec((tm, d), row), pl.BlockSpec((tm, d), row)],
        out_shape=[jax.ShapeDtypeStruct((t, d), F32), jax.ShapeDtypeStruct((t, d), F32)],
        semantics=("parallel",), vmem=BIG_VMEM,
    )(y, w, x, gate)


def resid_mm_bwd(dx, gate, o, w, gu, *, name, tm):
    t, d = dx.shape
    k = w.shape[0]
    swiglu = gu is not None

    def body(dx_ref, g_ref, o_ref, w_ref, *rest):
        if swiglu:
            gu_ref, dy_ref, gx_ref, dg_ref = rest
        else:
            dy_ref, gx_ref, dg_ref = rest
        i = pl.program_id(0)
        dxv = dx_ref[...]
        gx = (dxv * g_ref[...]).astype(BF16)
        gx_ref[...] = gx
        part = _colsum(dxv * o_ref[...])

        @pl.when(i == 0)
        def _():
            dg_ref[...] = jnp.zeros_like(dg_ref)

        dg_ref[...] += part
        da = _dot(gx, w_ref[...], 1, 1)
        if swiglu:
            g = gu_ref[:, :k].astype(F32)
            u = gu_ref[:, k:].astype(F32)
            dy_ref[:, :k] = (da * u * _dsilu(g)).astype(BF16)
            dy_ref[:, k:] = (da * _silu(g)).astype(BF16)
        else:
            dy_ref[...] = da

    row = lambda i: (i, 0)
    in_specs = [pl.BlockSpec((tm, d), row), _full((1, d)), pl.BlockSpec((tm, d), row), _resident((k, d))]
    args = [dx, gate, o, w]
    if swiglu:
        in_specs.append(pl.BlockSpec((tm, 2 * k), row))
        args.append(gu)
        dy_shape = jax.ShapeDtypeStruct((t, 2 * k), BF16)
        dy_spec = pl.BlockSpec((tm, 2 * k), row)
    else:
        dy_shape = jax.ShapeDtypeStruct((t, k), F32)
        dy_spec = pl.BlockSpec((tm, k), row)
    return _call(
        body, name=name, grid=(t // tm,), in_specs=in_specs,
        out_specs=[dy_spec, pl.BlockSpec((tm, d), row), _full((1, d))],
        out_shape=[dy_shape, jax.ShapeDtypeStruct((t, d), BF16), jax.ShapeDtypeStruct((1, d), F32)], vmem=BIG_VMEM,
    )(*args)


def norm_mm_bwd(dy, w, x, nw, scale, dres, *, name, tm=512):
    t, n = dy.shape
    d = x.shape[1]
    steps = t // tm

    def body(dy_ref, w_ref, x_ref, nw_ref, sc_ref, dres_ref, dx_ref, dnw_ref, dsc_ref, dsh_ref):
        i = pl.program_id(0)
        dh = _dot(dy_ref[...].astype(BF16), w_ref[...], 1, 1)
        x = x_ref[...]
        r = lax.rsqrt(jnp.mean(x * x, axis=-1, keepdims=True) + NORM_EPS)
        xn = x * r
        a = nw_ref[...] * (1.0 + sc_ref[...])

        @pl.when(i == 0)
        def _():
            dnw_ref[...] = jnp.zeros_like(dnw_ref)
            dsh_ref[...] = jnp.zeros_like(dsh_ref)

        dnw_ref[...] += _colsum(dh * xn)
        dsh_ref[...] += _colsum(dh)
        dxn = dh * a
        dx_ref[...] = r * (dxn - xn * jnp.mean(dxn * xn, axis=-1, keepdims=True)) + dres_ref[...]

        @pl.when(i == steps - 1)
        def _():
            da = dnw_ref[...]
            dsc_ref[...] = da * nw_ref[...]
            dnw_ref[...] = da * (1.0 + sc_ref[...])

    row = lambda i: (i, 0)
    vec = jax.ShapeDtypeStruct((1, d), F32)
    return _call(
        body, name=name, grid=(steps,),
        in_specs=[pl.BlockSpec((tm, n), row), _resident((d, n)), pl.BlockSpec((tm, d), row), _full((1, d)),
                  _full((1, d)), pl.BlockSpec((tm, d), row)],
        out_specs=[pl.BlockSpec((tm, d), row), _full((1, d)), _full((1, d)), _full((1, d))],
        out_shape=[jax.ShapeDtypeStruct((t, d), F32), vec, vec, vec], vmem=BIG_VMEM,
    )(dy, w, x, nw, scale, dres)


def _pick_tn(n, k, budget=6 << 20):
    best = LANE
    for m in range(1, n // LANE + 1):
        tn = m * LANE
        if n % tn == 0 and k * tn * 4 <= budget:
            best = tn
    return best


def mm_tn(a, g, *, name, tt=512):
    t, k = a.shape
    n = g.shape[1]
    tn = _pick_tn(n, k)

    def body(a_ref, g_ref, o_ref):
        @pl.when(pl.program_id(1) == 0)
        def _():
            o_ref[...] = jnp.zeros_like(o_ref)

        o_ref[...] += _dot(a_ref[...].astype(BF16), g_ref[...].astype(BF16), 0, 0)

    return _call(
        body, name=name, grid=(n // tn, t // tt),
        in_specs=[pl.BlockSpec((tt, k), lambda j, i: (i, 0)), pl.BlockSpec((tt, tn), lambda j, i: (i, j))],
        out_specs=pl.BlockSpec((k, tn), lambda j, i: (0, j)),
        out_shape=jax.ShapeDtypeStruct((k, n), F32), semantics=("parallel", "arbitrary"),
    )(a, g)


def loss_head(y, target, *, tm=512):
    t, d = y.shape
    steps = t // tm

    def body(y_ref, t_ref, dy_ref, l_ref, acc_ref):
        i = pl.program_id(0)

        @pl.when(i == 0)
        def _():
            acc_ref[...] = jnp.zeros_like(acc_ref)

        e = y_ref[...] - t_ref[...]
        dy_ref[...] = e * (1.0 / d)
        acc_ref[...] += _colsum(e * e)

        @pl.when(i == steps - 1)
        def _():
            tot = jnp.sum(acc_ref[...], axis=-1, keepdims=True) * (0.5 / d)
            l_ref[...] = jnp.broadcast_to(tot, l_ref.shape)

    row = lambda i: (i, 0)
    return _call(
        body, name="loss_head", grid=(steps,),
        in_specs=[pl.BlockSpec((tm, d), row), pl.BlockSpec((tm, d), row)],
        out_specs=[pl.BlockSpec((tm, d), row), _full((8, LANE))],
        out_shape=[jax.ShapeDtypeStruct((t, d), F32), jax.ShapeDtypeStruct((8, LANE), F32)],
        scratch_shapes=[pltpu.VMEM((1, d), F32)],
    )(y, target)


def _shift_rows(x, s):
    if s == 0:
        return x
    t = x.shape[0]
    ri = _iota2(x.shape, 0)
    rolled = pltpu.roll(x, s % t, axis=0)
    if s > 0:
        return jnp.where(ri >= s, rolled, 0.0)
    return jnp.where(ri < t + s, rolled, 0.0)


def _conv_pre(x, w):
    acc = x * w[DN_CONV - 1:DN_CONV, :]
    for j in range(DN_CONV - 1):
        acc = acc + _shift_rows(x, DN_CONV - 1 - j) * w[j:j + 1, :]
    return acc


def dn_conv(proj, conv_w):
    t = proj.shape[0]
    width = 3 * DN_WIDTH

    def body(x_ref, w_ref, o_ref):
        o_ref[...] = _silu(_conv_pre(x_ref[...], w_ref[...]))

    col = lambda j: (0, j)
    return _call(
        body, name="dn_conv", grid=(width // LANE,),
        in_specs=[pl.BlockSpec((t, LANE), col), pl.BlockSpec((8, LANE), col)],
        out_specs=pl.BlockSpec((t, LANE), col),
        out_shape=jax.ShapeDtypeStruct((t, width), F32), semantics=("parallel",),
    )(proj, conv_w)


def dn_conv_bwd(proj, conv_w, dact, dproj):
    t = proj.shape[0]
    width = 3 * DN_WIDTH

    def body(x_ref, w_ref, d_ref, _, dx_ref, dw_ref):
        x = x_ref[...]
        w = w_ref[...]
        dc = d_ref[...] * _dsilu(_conv_pre(x, w))
        dx = dc * w[DN_CONV - 1:DN_CONV, :]
        rows = []
        for j in range(DN_CONV - 1):
            s = DN_CONV - 1 - j
            dx = dx + _shift_rows(dc, -s) * w[j:j + 1, :]
            rows.append(_colsum(dc * _shift_rows(x, s)))
        rows.append(_colsum(dc * x))
        dx_ref[...] = dx.astype(BF16)
        ri = _iota2((8, LANE), 0)
        dw = jnp.zeros((8, LANE), F32)
        for j in range(DN_CONV):
            dw = dw + jnp.where(ri == j, rows[j], 0.0)
        dw_ref[...] = dw

    col = lambda j: (0, j)
    return _call(
        body, name="dn_conv_bwd", grid=(width // LANE,),
        in_specs=[pl.BlockSpec((t, LANE), col), pl.BlockSpec((8, LANE), col), pl.BlockSpec((t, LANE), col), ANY],
        out_specs=[pl.BlockSpec((t, LANE), col), pl.BlockSpec((8, LANE), col)],
        out_shape=[jax.ShapeDtypeStruct(dproj.shape, dproj.dtype), jax.ShapeDtypeStruct((8, width), F32)],
        semantics=("parallel",), aliases={3: 0},
    )(proj, conv_w, dact, dproj)


def _t(x):
    return jnp.swapaxes(x, -1, -2)


def _inv_unit_lower(a):
    c = a.shape[-1]
    eye = (_iota2((c, c), 0) == _iota2((c, c), 1)).astype(F32)
    x = eye - a
    p = _hdot(a, a)
    steps = int(math.log2(c)) - 1
    for i in range(steps):
        x = x + _hdot(x, p)
        if i < steps - 1:
            p = _hdot(p, p)
    return x


def _dn_chunk(q, k, v, a, b, alog, dtb, s_in, tinv=None):
    nh, c, d = q.shape
    rq = lax.rsqrt(_rowsum(q * q) + NORM_EPS)
    rk = lax.rsqrt(_rowsum(k * k) + NORM_EPS)
    qh = q * rq
    kn = k * rk
    qs = qh * (d ** -0.5)
    g = -jnp.exp(alog) * _softplus(a + dtb)
    beta = _sigmoid(b)
    ri = _iota2((c, c), 0)
    ci = _iota2((c, c), 1)
    causal = ri >= ci
    strict = ri > ci
    gb = jnp.broadcast_to(g, (nh, c, d))
    gcb = _xdot(jnp.broadcast_to(causal.astype(F32), (nh, c, c)), gb, exact=0)
    gc = gcb[..., :1]
    gl = _colsum(gb)[..., :1]
    dec = jnp.exp(jnp.where(causal, gc - _t(gcb)[:, :c, :], -1e30))
    kb = kn * beta
    amat = jnp.where(strict, _bdot(kb, kn, 1, 1) * dec, 0.0)
    if tinv is None:
        tinv = _inv_unit_lower(amat)
    e = jnp.exp(gc)
    f = jnp.exp(gl - gc)
    rw = kb * e
    sol = _hdot(tinv, jnp.concatenate([v * beta, rw], axis=-1))
    u = sol[..., :d]
    w = sol[..., d:]
    pmat = jnp.where(causal, _bdot(qs, kn, 1, 1) * dec, 0.0)
    qd = qs * e
    kd = kn * f
    vnew = u - _bdot(w, s_in)
    o = _bdot(qd, s_in) + _bdot(pmat, vnew)
    s_out = s_in * jnp.exp(gl) + _bdot(kd, vnew, 0, 0)
    return dict(rq=rq, rk=rk, qh=qh, kn=kn, qs=qs, g=g, beta=beta, causal=causal, strict=strict, gl=gl,
                dec=dec, kb=kb, amat=amat, tinv=tinv, e=e, f=f, rw=rw, u=u, w=w, pmat=pmat, qd=qd, kd=kd,
                vnew=vnew, o=o, s_out=s_out)


def _dn_chunk_bwd(m, q, v, a, alog, dtb, s_in, do, ds_out):
    nh, c, d = q.shape
    kn, qs, kb, u, w, e, f = m["kn"], m["qs"], m["kb"], m["u"], m["w"], m["e"], m["f"]
    beta, dec, tinv, vnew, kd, qd = m["beta"], m["dec"], m["tinv"], m["vnew"], m["kd"], m["qd"]
    el = jnp.exp(m["gl"])
    dvnew = _bdot(m["pmat"], do, 0, 0) + _bdot(kd, ds_out)
    dp = jnp.where(m["causal"], _bdot(do, vnew, 1, 1), 0.0)
    dqd = _bdot(do, s_in, 1, 1)
    dkd = _bdot(vnew, ds_out, 1, 1)
    ds_in = _bdot(qd, do, 0, 0) + el * ds_out - _bdot(w, dvnew, 0, 0)
    dgl = el * _colsum(_rowsum(s_in * ds_out))
    dw = -_bdot(dvnew, s_in, 1, 1)
    dsol = _hdot(tinv, jnp.concatenate([dvnew, dw], axis=-1), 0, 0)
    dru = dsol[..., :d]
    drw = dsol[..., d:]
    da_m = -jnp.where(m["strict"], _bdot(dsol, jnp.concatenate([u, w], axis=-1), 1, 1), 0.0)
    db_m = da_m * dec
    dq_m = dp * dec
    dkb = _bdot(db_m, kn)
    dkn = _bdot(db_m, kb, 0, 0) + _bdot(dq_m, qs, 0, 0)
    dqs = _bdot(dq_m, kn)
    gmat = da_m * m["amat"] + dp * m["pmat"]
    ones = jnp.ones((nh, c, d), F32)
    dgam = (_xdot(gmat, ones) - _xdot(gmat, ones, 0, 0))[..., :1]
    dqs = dqs + dqd * e
    dgam = dgam + _rowsum(dqd * qd)
    dkn = dkn + dkd * f
    tk = _rowsum(dkd * kd)
    dgam = dgam - tk
    dgl = dgl + _colsum(tk)
    dkb = dkb + drw * e
    dgam = dgam + _rowsum(drw * m["rw"])
    dv = dru * beta
    dbeta = _rowsum(dru * v) + _rowsum(dkb * kn)
    dkn = dkn + dkb * beta
    last = (_iota2((c, 1), 0) == c - 1).astype(F32)
    dgam = dgam + last * dgl
    upper = (_iota2((c, c), 0) <= _iota2((c, c), 1)).astype(F32)
    dg = _xdot(jnp.broadcast_to(upper, (nh, c, c)), jnp.broadcast_to(dgam, (nh, c, d)), exact=0)[..., :1]
    dqh = dqs * (d ** -0.5)
    dq = m["rq"] * (dqh - m["qh"] * _rowsum(dqh * m["qh"]))
    dk = m["rk"] * (dkn - kn * _rowsum(dkn * kn))
    sg = _sigmoid(a + dtb)
    da = dg * (-jnp.exp(alog)) * sg
    dalog = _colsum(dg * m["g"])
    ddtb = _colsum(da)
    db = dbeta * beta * (1.0 - beta)
    return dq, dk, dv, da, db, dalog, ddtb, ds_in


def _dn_gate(o, z, wn):
    ro = lax.rsqrt(jnp.mean(o * o, axis=-1, keepdims=True) + NORM_EPS)
    n = o * ro
    return n, ro, n * wn * _silu(z)


def _heads(ref, col0):
    d = DN_HEAD_DIM
    return jnp.stack([ref[:, col0 + h * d:col0 + (h + 1) * d] for h in range(DN_HEADS)])


def _dn_inputs(act_ref, ab_ref, sc_ref):
    ab = ab_ref[...]
    sc = sc_ref[...]
    q = _heads(act_ref, 0)
    k = _heads(act_ref, DN_WIDTH)
    v = _heads(act_ref, 2 * DN_WIDTH)
    a = jnp.stack([ab[:, h:h + 1] for h in range(DN_HEADS)])
    b = jnp.stack([ab[:, DN_HEADS + h:DN_HEADS + h + 1] for h in range(DN_HEADS)])
    alog = jnp.stack([sc[0:1, h:h + 1] for h in range(DN_HEADS)])
    dtb = jnp.stack([sc[1:2, h:h + 1] for h in range(DN_HEADS)])
    return q, k, v, a, b, alog, dtb


def dn_fwd(act, proj, scal, wn):
    t = act.shape[0]
    n = t // DN_CHUNK
    d = DN_HEAD_DIM

    def body(act_ref, z_ref, ab_ref, sc_ref, wn_ref, y_ref, st_ref, ti_ref, s_ref):
        @pl.when(pl.program_id(0) == 0)
        def _():
            s_ref[...] = jnp.zeros_like(s_ref)

        q, k, v, a, b, alog, dtb = _dn_inputs(act_ref, ab_ref, sc_ref)
        s_in = s_ref[...]
        st_ref[0] = s_in
        m = _dn_chunk(q, k, v, a, b, alog, dtb, s_in)
        ti_ref[0] = m["tinv"]
        s_ref[...] = m["s_out"]
        y = _dn_gate(m["o"], _heads(z_ref, 0), wn_ref[...])[2]
        for h in range(DN_HEADS):
            y_ref[:, h * d:(h + 1) * d] = y[h]

    return _call(
        body, name="dn_fwd", grid=(n,),
        in_specs=[pl.BlockSpec((DN_CHUNK, 3 * DN_WIDTH), lambda i: (i, 0)),
                  pl.BlockSpec((DN_CHUNK, DN_WIDTH), lambda i: (i, C_Z // DN_WIDTH)),
                  pl.BlockSpec((DN_CHUNK, LANE), lambda i: (i, C_AB // LANE)),
                  _full((8, LANE)), _full((1, d))],
        out_specs=[pl.BlockSpec((DN_CHUNK, DN_WIDTH), lambda i: (i, 0)),
                   pl.BlockSpec((1, DN_HEADS, d, d), lambda i: (i, 0, 0, 0)),
                   pl.BlockSpec((1, DN_HEADS, DN_CHUNK, DN_CHUNK), lambda i: (i, 0, 0, 0))],
        out_shape=[jax.ShapeDtypeStruct((t, MIX_WIDTH), F32), jax.ShapeDtypeStruct((n, DN_HEADS, d, d), F32),
                   jax.ShapeDtypeStruct((n, DN_HEADS, DN_CHUNK, DN_CHUNK), F32)],
        scratch_shapes=[pltpu.VMEM((DN_HEADS, d, d), F32)],
    )(act, proj, proj, scal, wn)


def dn_bwd(act, proj, scal, wn, states, tinvs, dy):
    t = act.shape[0]
    n = t // DN_CHUNK
    d = DN_HEAD_DIM
    zab = DN_WIDTH + AB_PAD

    def body(act_ref, z_ref, ab_ref, sc_ref, wn_ref, st_ref, ti_ref, dy_ref, dact_ref, dzab_ref, dpar_ref, ds_ref):
        @pl.when(pl.program_id(0) == 0)
        def _():
            ds_ref[...] = jnp.zeros_like(ds_ref)
            dpar_ref[...] = jnp.zeros_like(dpar_ref)

        wnv = wn_ref[...]
        q, k, v, a, b, alog, dtb = _dn_inputs(act_ref, ab_ref, sc_ref)
        s_in = st_ref[0]
        z = _heads(z_ref, 0)
        dyh = _heads(dy_ref, 0)
        m = _dn_chunk(q, k, v, a, b, alog, dtb, s_in, ti_ref[0])
        nrm, ro, _ = _dn_gate(m["o"], z, wnv)
        sz = _silu(z)
        dz = dyh * nrm * wnv * _dsilu(z)
        dn = dyh * wnv * sz
        dwn = _colsum(dyh * nrm * sz)
        do = ro * (dn - nrm * jnp.mean(dn * nrm, axis=-1, keepdims=True))
        dq, dk, dv, da, db, dalog, ddtb, ds_in = _dn_chunk_bwd(m, q, v, a, alog, dtb, s_in, do, ds_ref[...])
        ds_ref[...] = ds_in
        lane = _iota2((DN_CHUNK, LANE), 1)
        prow = _iota2((8, LANE), 0)
        plane = _iota2((8, LANE), 1)
        dab = jnp.zeros((DN_CHUNK, LANE), F32)
        dpar = jnp.zeros((8, LANE), F32)
        for h in range(DN_HEADS):
            dzab_ref[:, h * d:(h + 1) * d] = dz[h].astype(BF16)
            dact_ref[:, h * d:(h + 1) * d] = dq[h]
            dact_ref[:, DN_WIDTH + h * d:DN_WIDTH + (h + 1) * d] = dk[h]
            dact_ref[:, 2 * DN_WIDTH + h * d:2 * DN_WIDTH + (h + 1) * d] = dv[h]
            dab = dab + jnp.where(lane == h, da[h], 0.0) + jnp.where(lane == DN_HEADS + h, db[h], 0.0)
            dpar = dpar + jnp.where((prow == 0) & (plane == h), dalog[h], 0.0)
            dpar = dpar + jnp.where((prow == 1) & (plane == h), ddtb[h], 0.0)
            dpar = dpar + jnp.where(prow == 2, dwn[h], 0.0)
        dzab_ref[:, DN_WIDTH:DN_WIDTH + LANE] = dab.astype(BF16)
        dzab_ref[:, DN_WIDTH + LANE:] = jnp.zeros((DN_CHUNK, AB_PAD - LANE), BF16)
        dpar_ref[...] += dpar

    rev = lambda i: (n - 1 - i, 0)
    return _call(
        body, name="dn_bwd", grid=(n,),
        in_specs=[pl.BlockSpec((DN_CHUNK, 3 * DN_WIDTH), rev),
                  pl.BlockSpec((DN_CHUNK, DN_WIDTH), lambda i: (n - 1 - i, C_Z // DN_WIDTH)),
                  pl.BlockSpec((DN_CHUNK, LANE), lambda i: (n - 1 - i, C_AB // LANE)),
                  _full((8, LANE)), _full((1, d)),
                  pl.BlockSpec((1, DN_HEADS, d, d), lambda i: (n - 1 - i, 0, 0, 0)),
                  pl.BlockSpec((1, DN_HEADS, DN_CHUNK, DN_CHUNK), lambda i: (n - 1 - i, 0, 0, 0)),
                  pl.BlockSpec((DN_CHUNK, DN_WIDTH), rev)],
        out_specs=[pl.BlockSpec((DN_CHUNK, 3 * DN_WIDTH), rev),
                   pl.BlockSpec((DN_CHUNK, zab), lambda i: (n - 1 - i, C_Z // zab)), _full((8, LANE))],
        out_shape=[jax.ShapeDtypeStruct((t, 3 * DN_WIDTH), F32), jax.ShapeDtypeStruct((t, IN_PAD), BF16),
                   jax.ShapeDtypeStruct((8, LANE), F32)],
        scratch_shapes=[pltpu.VMEM((DN_HEADS, d, d), F32)],
    )(act, proj, proj, scal, wn, states, tinvs, dy)


_INV_SQRT2 = 0.7071067811865476
_INV_SQRT2PI = 0.3989422804014327


def _gelu(x):
    return 0.5 * x * (1.0 + lax.erf(x * _INV_SQRT2))


def _dgelu(x):
    return 0.5 * (1.0 + lax.erf(x * _INV_SQRT2)) + x * jnp.exp(-0.5 * x * x) * _INV_SQRT2PI


def _gm_core(uv, lng, lnb, ws_ref, bst):
    c = uv.shape[0]
    zz = _gelu(uv)
    u = zz[:, :GM_WIDTH]
    vv = zz[:, GM_WIDTH:]
    xc = vv - jnp.mean(vv, axis=-1, keepdims=True)
    rs = lax.rsqrt(jnp.mean(xc * xc, axis=-1, keepdims=True) + NORM_EPS)
    xh = xc * rs
    vn = xh * lng + lnb
    grp = _iota2((c, GM_WIDTH), 1) // GM_GROUP_DIM
    tril = _iota2((c, c), 0) >= _iota2((c, c), 1)
    sv = jnp.zeros((c, GM_WIDTH), F32)
    masks = []
    for g in range(GM_GROUPS):
        mk = grp == g
        masks.append(mk)
        ws = jnp.where(tril, ws_ref[g], 0.0)
        sv = sv + _bdot(ws, jnp.where(mk, vn, 0.0)) + jnp.where(mk, bst[:, g:g + 1], 0.0)
    return u, xh, rs, vn, sv, masks, tril


def gm_fwd(proj, lng, lnb, w_s, bst, ybuf):
    t = proj.shape[0]

    def body(uv_ref, g_ref, b_ref, ws_ref, bst_ref, _, y_ref):
        u, _, _, _, sv, _, _ = _gm_core(uv_ref[...], g_ref[...], b_ref[...], ws_ref, bst_ref[...])
        y_ref[...] = u * sv

    return _call(
        body, name="gm_fwd", grid=(t // GM_CHUNK,),
        in_specs=[pl.BlockSpec((GM_CHUNK, 2 * GM_WIDTH), lambda i: (i, C_UV // (2 * GM_WIDTH))),
                  _full((1, GM_WIDTH)), _full((1, GM_WIDTH)), _full((GM_GROUPS, GM_CHUNK, GM_CHUNK)),
                  _full((GM_CHUNK, LANE)), ANY],
        out_specs=pl.BlockSpec((GM_CHUNK, GM_WIDTH), lambda i: (i, DN_WIDTH // GM_WIDTH)),
        out_shape=jax.ShapeDtypeStruct(ybuf.shape, F32), semantics=("parallel",), aliases={5: 0},
    )(proj, lng, lnb, w_s, bst, ybuf)


def gm_bwd(proj, lng, lnb, w_s, bst, dy, dproj):
    t = proj.shape[0]

    def body(uv_ref, g_ref, b_ref, ws_ref, bst_ref, dy_ref, _, duv_ref, dws_ref, dbst_ref, dln_ref):
        @pl.when(pl.program_id(0) == 0)
        def _():
            dws_ref[...] = jnp.zeros_like(dws_ref)
            dbst_ref[...] = jnp.zeros_like(dbst_ref)
            dln_ref[...] = jnp.zeros_like(dln_ref)

        uv = uv_ref[...]
        lng = g_ref[...]
        u, xh, rs, vn, sv, masks, tril = _gm_core(uv, lng, b_ref[...], ws_ref, bst_ref[...])
        dyv = dy_ref[...]
        dsv = dyv * u
        lane = _iota2((GM_CHUNK, LANE), 1)
        dvn = jnp.zeros_like(dsv)
        dbst = jnp.zeros((GM_CHUNK, LANE), F32)
        for g in range(GM_GROUPS):
            ws = jnp.where(tril, ws_ref[g], 0.0)
            dsg = jnp.where(masks[g], dsv, 0.0)
            dvn = dvn + jnp.where(masks[g], _bdot(ws, dsv, 0, 0), 0.0)
            dws_ref[g] += jnp.where(tril, _bdot(dsg, vn, 1, 1), 0.0)
            dbst = dbst + jnp.where(lane == g, _rowsum(dsg), 0.0)
        dbst_ref[...] += dbst
        row = _iota2((8, GM_WIDTH), 0)
        dln_ref[...] += jnp.where(row == 0, _colsum(dvn * xh), 0.0) + jnp.where(row == 1, _colsum(dvn), 0.0)
        dxh = dvn * lng
        dvv = rs * (dxh - jnp.mean(dxh, axis=-1, keepdims=True) - xh * jnp.mean(dxh * xh, axis=-1, keepdims=True))
        dg = _dgelu(uv)
        duv_ref[:, :GM_WIDTH] = (dyv * sv * dg[:, :GM_WIDTH]).astype(BF16)
        duv_ref[:, GM_WIDTH:] = (dvv * dg[:, GM_WIDTH:]).astype(BF16)

    return _call(
        body, name="gm_bwd", grid=(t // GM_CHUNK,),
        in_specs=[pl.BlockSpec((GM_CHUNK, 2 * GM_WIDTH), lambda i: (i, C_UV // (2 * GM_WIDTH))),
                  _full((1, GM_WIDTH)), _full((1, GM_WIDTH)), _full((GM_GROUPS, GM_CHUNK, GM_CHUNK)),
                  _full((GM_CHUNK, LANE)),
                  pl.BlockSpec((GM_CHUNK, GM_WIDTH), lambda i: (i, DN_WIDTH // GM_WIDTH)), ANY],
        out_specs=[pl.BlockSpec((GM_CHUNK, 2 * GM_WIDTH), lambda i: (i, C_UV // (2 * GM_WIDTH))),
                   _full((GM_GROUPS, GM_CHUNK, GM_CHUNK)), _full((GM_CHUNK, LANE)), _full((8, GM_WIDTH))],
        out_shape=[jax.ShapeDtypeStruct(dproj.shape, dproj.dtype),
                   jax.ShapeDtypeStruct((GM_GROUPS, GM_CHUNK, GM_CHUNK), F32),
                   jax.ShapeDtypeStruct((GM_CHUNK, LANE), F32), jax.ShapeDtypeStruct((8, GM_WIDTH), F32)],
        aliases={6: 0},
    )(proj, lng, lnb, w_s, bst, dy, dproj)


def _head_mats():
    r = _iota2((SW_WIDTH, SW_WIDTH), 0)
    c = _iota2((SW_WIDTH, SW_WIDTH), 1)
    same = (r // SW_HEAD_DIM) == (c // SW_HEAD_DIM)
    cc = c % SW_HEAD_DIM
    half = ROPE_DIM // 2
    rot = jnp.where((cc < half) & (r == c + half), -1.0, 0.0) + jnp.where((cc >= half) & (cc < ROPE_DIM) & (r == c - half), 1.0, 0.0)
    return same.astype(F32), rot


def _seg_col(s):
    return C_SW // SW_WIDTH + (s // 2) * 3 + s % 2


def _halves(x):
    return x[:, :LANE], x[:, LANE:]


def sw_prep(proj, nw2, cos_t, sin_t, *, tm=512):
    t = proj.shape[0]

    def body(x_ref, w_ref, c_ref, s_ref, o_ref):
        same, rot = _head_mats()
        x = x_ref[...]
        r = lax.rsqrt(_xdot(x * x, same) * (1.0 / SW_HEAD_DIM) + NORM_EPS)
        xn = x * r * w_ref[0]
        o_ref[0, 0], o_ref[0, 1] = _halves(xn * c_ref[...] + _xdot(xn, rot) * s_ref[...])

    return _call(
        body, name="sw_prep", grid=(6, t // tm),
        in_specs=[pl.BlockSpec((tm, SW_WIDTH), lambda s, i: (i, _seg_col(s))),
                  pl.BlockSpec((1, 1, SW_WIDTH), lambda s, i: (s % 2, 0, 0)),
                  pl.BlockSpec((tm, SW_WIDTH), lambda s, i: (i, 0)),
                  pl.BlockSpec((tm, SW_WIDTH), lambda s, i: (i, 0))],
        out_specs=pl.BlockSpec((1, 2, tm, LANE), lambda s, i: (s, 0, i, 0)),
        out_shape=jax.ShapeDtypeStruct((6, 2, t, LANE), F32), semantics=("parallel", "parallel"),
    )(proj, nw2, cos_t, sin_t)


def sw_prep_bwd(proj, nw2, cos_t, sin_t, dkvq, dproj, dnw, p, *, tm=512):
    t = proj.shape[0]
    col0 = C_SW // SW_WIDTH + 3 * p
    seg_col = lambda s: col0 + (s + 1) % 3

    def body(x_ref, w_ref, c_ref, s_ref, d_ref, _, dw0_ref, dx_ref, dw_ref):
        s = pl.program_id(0)
        dout = jnp.concatenate([d_ref[0, 0], d_ref[0, 1]], axis=1)

        @pl.when(s == 1)
        def _():
            dx_ref[...] = dout.astype(BF16)

        @pl.when((s != 1) & (pl.program_id(1) == 0))
        def _():
            dw_ref[...] = dw0_ref[...]

        @pl.when(s != 1)
        def _():
            same, rot = _head_mats()
            x = x_ref[...]
            w = w_ref[0]
            r = lax.rsqrt(_xdot(x * x, same) * (1.0 / SW_HEAD_DIM) + NORM_EPS)
            xh = x * r
            dxn = dout * c_ref[...] + _xdot(dout * s_ref[...], rot, 1, 1)
            dw_ref[0] += _colsum(dxn * xh)
            dxh = dxn * w
            dx_ref[...] = (r * (dxh - xh * (_xdot(dxh * xh, same) * (1.0 / SW_HEAD_DIM)))).astype(BF16)

    return _call(
        body, name=f"sw_prep_bwd{p}", grid=(3, t // tm),
        in_specs=[pl.BlockSpec((tm, SW_WIDTH), lambda s, i: (i, seg_col(s))),
                  pl.BlockSpec((1, 1, SW_WIDTH), lambda s, i: (1 - s // 2, 0, 0)),
                  pl.BlockSpec((tm, SW_WIDTH), lambda s, i: (i, 0)),
                  pl.BlockSpec((tm, SW_WIDTH), lambda s, i: (i, 0)),
                  pl.BlockSpec((1, 2, tm, LANE), lambda s, i: (s, 0, i, 0)), ANY,
                  pl.BlockSpec((1, 1, SW_WIDTH), lambda s, i: (s // 2, 0, 0))],
        out_specs=[pl.BlockSpec((tm, SW_WIDTH), lambda s, i: (i, seg_col(s))),
                   pl.BlockSpec((1, 1, SW_WIDTH), lambda s, i: (s // 2, 0, 0))],
        out_shape=[jax.ShapeDtypeStruct(dproj.shape, dproj.dtype), jax.ShapeDtypeStruct((2, 1, SW_WIDTH), F32)],
        semantics=("arbitrary", "arbitrary"), aliases={5: 0},
    )(proj, nw2, cos_t, sin_t, dkvq, dproj, dnw)


_SW_SCALE = SW_HEAD_DIM ** -0.5
_NEG = -1e30


def _sw_masks(has_other):
    ri = _iota2((SW_BLOCK, SW_BLOCK), 0)
    ci = _iota2((SW_BLOCK, SW_BLOCK), 1)
    return ri >= ci, (ci >= ri) & has_other


def _pair(x):
    first = _iota2((1, LANE), 1) < SW_HEAD_DIM
    return jnp.stack([jnp.where(first, x, 0.0), jnp.where(first, 0.0, x)])


def _both(x):
    return jnp.broadcast_to(x.astype(BF16)[None], (2,) + x.shape)


def _unpair(x2):
    first = _iota2((1, LANE), 1) < SW_HEAD_DIM
    return jnp.where(first, x2[0], x2[1])


def _head_cols(x):
    return jnp.stack([x[:, 0:1], x[:, SW_HEAD_DIM:SW_HEAD_DIM + 1]])


SW_GROUP = 4


def _sw_geometry(t, p):
    dil = SW_DILATIONS[p]
    unit = SW_BLOCK * dil
    nb = max(1, SW_GROUP // dil)
    return dil, unit, nb, t // (unit * nb)


def _sw_groups(dil, nb, body):
    if nb * dil == SW_GROUP:
        body([(k // dil, k % dil) for k in range(SW_GROUP)])
    else:
        def step(g, carry):
            body([(0, SW_GROUP * g + k) for k in range(SW_GROUP)])
            return carry

        lax.fori_loop(0, nb * dil // SW_GROUP, step, 0)


def _sw_rows(i, r, dil):
    start = i * SW_BLOCK * dil + r
    return pl.ds(start, SW_BLOCK) if dil == 1 else pl.ds(start, SW_BLOCK, stride=dil)


def _sw_load(refs, probs, dil, shift, wrap, fn):
    out = []
    for i, r in probs:
        if shift != 0 and i == wrap:
            out.append(fn(refs[1][_sw_rows(0, r, dil), :]))
        else:
            out.append(fn(refs[0][_sw_rows(i + shift, r, dil), :]))
    return jnp.concatenate(out, axis=0)


def _sw_other_masks(probs, wrap, edge_ok):
    _, other = _sw_masks(edge_ok)
    _, always = _sw_masks(True)
    return jnp.stack([other if i == wrap else always for i, _ in probs for _ in range(2)])


def sw_attn(qk, proj, p):
    t = proj.shape[0]
    dil, unit, nb, nsp = _sw_geometry(t, p)
    vcol = (C_SW + 3 * SW_WIDTH * p + 2 * SW_WIDTH) // LANE

    def body(q_ref, kc_ref, kp_ref, vc_ref, vp_ref, o_ref, l_ref):
        mc, _ = _sw_masks(True)
        first = pl.program_id(1) != 0
        q_r, k_r, v_r = (q_ref.at[0, 0], None), (kc_ref.at[0, 0], kp_ref.at[0, 0]), (vc_ref, vp_ref)

        def one(probs):
            mp = _sw_other_masks(probs, 0, first)
            q2 = _sw_load(q_r, probs, dil, 0, 0, _pair)
            sc = jnp.where(mc, _bdot(q2, _sw_load(k_r, probs, dil, 0, 0, _both), 1, 1) * _SW_SCALE, _NEG)
            sp = jnp.where(mp, _bdot(q2, _sw_load(k_r, probs, dil, -1, 0, _both), 1, 1) * _SW_SCALE, _NEG)
            mx = jnp.maximum(jnp.max(sc, axis=-1, keepdims=True), jnp.max(sp, axis=-1, keepdims=True))
            pc = jnp.exp(sc - mx)
            pp = jnp.exp(sp - mx)
            den = _rowsum(pc) + _rowsum(pp)
            o2 = (_bdot(pc, _sw_load(v_r, probs, dil, 0, 0, _both))
                  + _bdot(pp, _sw_load(v_r, probs, dil, -1, 0, _both))) * (1.0 / den)
            l2 = jnp.broadcast_to(mx + jnp.log(den), o2.shape)
            for n, (i, r) in enumerate(probs):
                o_ref.at[0][_sw_rows(i, r, dil), :] = _unpair(o2[2 * n:2 * n + 2])
                l_ref.at[0][_sw_rows(i, r, dil), :] = _unpair(l2[2 * n:2 * n + 2])

        _sw_groups(dil, nb, one)

    before = lambda j: jnp.maximum(j * nb - 1, 0)
    seg = lambda s: pl.BlockSpec((1, 1, unit * nb, LANE), lambda h, j: (s, h, j, 0))
    seg_b = lambda s: pl.BlockSpec((1, 1, unit, LANE), lambda h, j: (s, h, before(j), 0))
    out = pl.BlockSpec((1, unit * nb, LANE), lambda h, j: (h, j, 0))
    shp = jax.ShapeDtypeStruct((2, t, LANE), F32)
    return _call(
        body, name=f"sw_attn{p}", grid=(2, nsp),
        in_specs=[seg(2 * p), seg(2 * p + 1), seg_b(2 * p + 1),
                  pl.BlockSpec((unit * nb, LANE), lambda h, j: (j, vcol + h)),
                  pl.BlockSpec((unit, LANE), lambda h, j: (before(j), vcol + h))],
        out_specs=[out, out], out_shape=[shp, shp], semantics=("parallel", "parallel"),
    )(qk, qk, qk, proj, proj)


def sw_attn_dkv(qk, proj, dy, lg, dm, p):
    t = proj.shape[0]
    dil, unit, nb, nsp = _sw_geometry(t, p)
    nunits = t // unit
    vcol = (C_SW + 3 * SW_WIDTH * p + 2 * SW_WIDTH) // LANE
    ycol = (DN_WIDTH + GM_WIDTH) // LANE

    def body(k_ref, v_ref, qc_ref, qn_ref, doc_ref, don_ref, lc_ref, ln_ref, dc_ref, dn_ref, o_ref):
        mc, _ = _sw_masks(True)
        more = pl.program_id(1) + 1 < nsp
        q_r, do_r = (qc_ref.at[0, 0], qn_ref.at[0, 0]), (doc_ref, don_ref)
        l_r, d_r = (lc_ref.at[0], ln_ref.at[0]), (dc_ref.at[0], dn_ref.at[0])

        def one(probs):
            k2 = _sw_load((k_ref.at[0, 0], None), probs, dil, 0, 0, _both)
            v2 = _sw_load((v_ref, None), probs, dil, 0, 0, _both)
            dk = jnp.zeros((2 * SW_GROUP, SW_BLOCK, LANE), F32)
            dv = jnp.zeros((2 * SW_GROUP, SW_BLOCK, LANE), F32)
            for shift, mk in ((0, mc), (1, _sw_other_masks(probs, nb - 1, more))):
                q2 = _sw_load(q_r, probs, dil, shift, nb - 1, _pair)
                do2 = _sw_load(do_r, probs, dil, shift, nb - 1, _pair)
                lse = _sw_load(l_r, probs, dil, shift, nb - 1, _head_cols)
                dd = _sw_load(d_r, probs, dil, shift, nb - 1, _head_cols)
                pr = jnp.exp(jnp.where(mk, _bdot(q2, k2, 1, 1) * _SW_SCALE, _NEG) - lse)
                dv = dv + _bdot(pr, do2, 0, 0)
                ds = pr * (_bdot(do2, v2, 1, 1) - dd)
                dk = dk + _bdot(ds, q2, 0, 0)
            for n, (i, r) in enumerate(probs):
                o_ref.at[0, 0][_sw_rows(i, r, dil), :] = (dk[2 * n] + dk[2 * n + 1]) * _SW_SCALE
                o_ref.at[1, 0][_sw_rows(i, r, dil), :] = dv[2 * n] + dv[2 * n + 1]

        _sw_groups(dil, nb, one)

    after = lambda j: jnp.minimum((j + 1) * nb, nunits - 1)
    seg = lambda s: pl.BlockSpec((1, 1, unit * nb, LANE), lambda h, j: (s, h, j, 0))
    seg_a = lambda s: pl.BlockSpec((1, 1, unit, LANE), lambda h, j: (s, h, after(j), 0))
    col = lambda c0: pl.BlockSpec((unit * nb, LANE), lambda h, j: (j, c0 + h))
    col_a = lambda c0: pl.BlockSpec((unit, LANE), lambda h, j: (after(j), c0 + h))
    hp = pl.BlockSpec((1, unit * nb, LANE), lambda h, j: (h, j, 0))
    hp_a = pl.BlockSpec((1, unit, LANE), lambda h, j: (h, after(j), 0))
    return _call(
        body, name=f"sw_dkv{p}", grid=(2, nsp),
        in_specs=[seg(2 * p + 1), col(vcol), seg(2 * p), seg_a(2 * p), col(ycol), col_a(ycol), hp, hp_a, hp, hp_a],
        out_specs=pl.BlockSpec((2, 1, unit * nb, LANE), lambda h, j: (0, h, j, 0)),
        out_shape=jax.ShapeDtypeStruct((3, 2, t, LANE), F32), semantics=("parallel", "parallel"),
    )(qk, proj, qk, qk, dy, dy, lg, lg, dm, dm)


def sw_attn_dq(qk, proj, dy, lg, dm, dkvq, p):
    t = proj.shape[0]
    dil, unit, nb, nsp = _sw_geometry(t, p)
    vcol = (C_SW + 3 * SW_WIDTH * p + 2 * SW_WIDTH) // LANE
    ycol = (DN_WIDTH + GM_WIDTH) // LANE

    def body(q_ref, kc_ref, kp_ref, vc_ref, vp_ref, do_ref, l_ref, d_ref, _, dq_ref):
        mc, _ = _sw_masks(True)
        first = pl.program_id(1) != 0
        k_r, v_r = (kc_ref.at[0, 0], kp_ref.at[0, 0]), (vc_ref, vp_ref)

        def one(probs):
            mp = _sw_other_masks(probs, 0, first)
            q2 = _sw_load((q_ref.at[0, 0], None), probs, dil, 0, 0, _pair)
            do2 = _sw_load((do_ref, None), probs, dil, 0, 0, _pair)
            lse = _sw_load((l_ref.at[0], None), probs, dil, 0, 0, _head_cols)
            dd = _sw_load((d_ref.at[0], None), probs, dil, 0, 0, _head_cols)
            kc = _sw_load(k_r, probs, dil, 0, 0, _both)
            kp = _sw_load(k_r, probs, dil, -1, 0, _both)
            pc = jnp.exp(jnp.where(mc, _bdot(q2, kc, 1, 1) * _SW_SCALE, _NEG) - lse)
            pp = jnp.exp(jnp.where(mp, _bdot(q2, kp, 1, 1) * _SW_SCALE, _NEG) - lse)
            dsc = pc * (_bdot(do2, _sw_load(v_r, probs, dil, 0, 0, _both), 1, 1) - dd)
            dsp = pp * (_bdot(do2, _sw_load(v_r, probs, dil, -1, 0, _both), 1, 1) - dd)
            dq2 = (_bdot(dsc, kc) + _bdot(dsp, kp)) * _SW_SCALE
            for n, (i, r) in enumerate(probs):
                dq_ref.at[0, 0][_sw_rows(i, r, dil), :] = _unpair(dq2[2 * n:2 * n + 2])

        _sw_groups(dil, nb, one)

    before = lambda j: jnp.maximum(j * nb - 1, 0)
    seg = lambda s: pl.BlockSpec((1, 1, unit * nb, LANE), lambda h, j: (s, h, j, 0))
    seg_b = lambda s: pl.BlockSpec((1, 1, unit, LANE), lambda h, j: (s, h, before(j), 0))
    col = lambda c0: pl.BlockSpec((unit * nb, LANE), lambda h, j: (j, c0 + h))
    col_b = lambda c0: pl.BlockSpec((unit, LANE), lambda h, j: (before(j), c0 + h))
    hp = pl.BlockSpec((1, unit * nb, LANE), lambda h, j: (h, j, 0))
    return _call(
        body, name=f"sw_dq{p}", grid=(2, nsp),
        in_specs=[seg(2 * p), seg(2 * p + 1), seg_b(2 * p + 1), col(vcol), col_b(vcol), col(ycol), hp, hp, ANY],
        out_specs=pl.BlockSpec((1, 1, unit * nb, LANE), lambda h, j: (2, h, j, 0)),
        out_shape=jax.ShapeDtypeStruct(dkvq.shape, F32), semantics=("parallel", "parallel"), aliases={8: 0},
    )(qk, qk, qk, proj, proj, dy, lg, dm, dkvq)


def sw_merge(outs, lses, ybuf, *, tm=512):
    t = ybuf.shape[0]

    def body(o0, o1, o2, l0_ref, l1_ref, l2_ref, _, y_ref, lg_ref):
        l0, l1, l2 = l0_ref[...], l1_ref[...], l2_ref[...]
        mx = jnp.maximum(jnp.maximum(l0, l1), l2)
        lg = mx + jnp.log(jnp.exp(l0 - mx) + jnp.exp(l1 - mx) + jnp.exp(l2 - mx))
        lg_ref[...] = lg
        y = jnp.exp(l0 - lg) * o0[...] + jnp.exp(l1 - lg) * o1[...] + jnp.exp(l2 - lg) * o2[...]
        y_ref[...] = jnp.concatenate([y[0], y[1]], axis=1)

    hp = pl.BlockSpec((2, tm, LANE), lambda i: (0, i, 0))
    return _call(
        body, name="sw_merge", grid=(t // tm,), in_specs=[hp] * 6 + [ANY],
        out_specs=[pl.BlockSpec((tm, SW_WIDTH), lambda i: (i, (DN_WIDTH + GM_WIDTH) // SW_WIDTH)), hp],
        out_shape=[jax.ShapeDtypeStruct(ybuf.shape, F32), jax.ShapeDtypeStruct((2, t, LANE), F32)],
        semantics=("parallel",), aliases={6: 0},
    )(*outs, *lses, ybuf)


def sw_delta(dy, ybuf, *, tm=512):
    t = ybuf.shape[0]

    def body(dy_ref, y_ref, o_ref):
        same, _ = _head_mats()
        o_ref[0], o_ref[1] = _halves(_xdot(dy_ref[...] * y_ref[...], same))

    b1 = pl.BlockSpec((tm, SW_WIDTH), lambda i: (i, (DN_WIDTH + GM_WIDTH) // SW_WIDTH))
    return _call(body, name="sw_delta", grid=(t // tm,), in_specs=[b1, b1],
                 out_specs=pl.BlockSpec((2, tm, LANE), lambda i: (0, i, 0)),
                 out_shape=jax.ShapeDtypeStruct((2, t, LANE), F32), semantics=("parallel",))(dy, ybuf)


def _rope_tables(t):
    inv = ROPE_THETA ** (-jnp.arange(0, ROPE_DIM, 2, dtype=F32) / ROPE_DIM)
    ang = jnp.arange(t, dtype=F32)[:, None] * inv[None, :]
    pad1 = jnp.ones((t, SW_HEAD_DIM - ROPE_DIM), F32)
    pad0 = jnp.zeros((t, SW_HEAD_DIM - ROPE_DIM), F32)
    cos_h = jnp.concatenate([jnp.cos(ang), jnp.cos(ang), pad1], axis=1)
    sin_h = jnp.concatenate([jnp.sin(ang), jnp.sin(ang), pad0], axis=1)
    return jnp.tile(cos_h, (1, SW_HEADS)), jnp.tile(sin_h, (1, SW_HEADS))


def sw_forward(proj, nw2, cos_t, sin_t, ybuf):
    qk = sw_prep(proj, nw2, cos_t, sin_t)
    outs, lses = [], []
    for p in range(len(SW_DILATIONS)):
        o, lse = sw_attn(qk, proj, p)
        outs.append(o)
        lses.append(lse)
    ybuf, lg = sw_merge(outs, lses, ybuf)
    return ybuf, (qk, lg)


def sw_backward(proj, nw2, cos_t, sin_t, res, ybuf, dy, dproj):
    qk, lg = res
    dm = sw_delta(dy, ybuf)
    dnw = jnp.zeros((2, 1, SW_WIDTH), F32)
    for p in range(len(SW_DILATIONS)):
        dkvq = sw_attn_dkv(qk, proj, dy, lg, dm, p)
        dkvq = sw_attn_dq(qk, proj, dy, lg, dm, dkvq, p)
        dproj, dnw = sw_prep_bwd(proj, nw2, cos_t, sin_t, dkvq, dproj, dnw, p)
    return dproj, dnw[::-1, 0]


def _pad_rows(a, rows):
    return jnp.zeros((rows,) + a.shape[1:], a.dtype).at[:a.shape[0]].set(a)


def _consts(sp):
    d = {}
    d["mix_nw"] = sp["mix_norm_w"][:, None, :]
    d["ffn_nw"] = sp["ffn_norm_w"][:, None, :]
    d["cw8"] = jnp.pad(sp["dn_conv_w"], ((0, 0), (0, 8 - DN_CONV), (0, 0)))
    d["scal"] = jnp.pad(jnp.stack([sp["dn_a_log"], sp["dn_dt_bias"]], axis=1), ((0, 0), (0, 6), (0, LANE - DN_HEADS)))
    d["wn"] = sp["dn_out_norm_w"][:, None, :]
    d["lng"] = sp["gm_ln_g"][:, None, :]
    d["lnb"] = sp["gm_ln_b"][:, None, :]
    d["w_s"] = sp["gm_w_s"]
    d["bst"] = jnp.pad(jnp.swapaxes(sp["gm_b_s"], 1, 2), ((0, 0), (0, 0), (0, LANE - GM_GROUPS)))
    d["nw2"] = jnp.stack([jnp.tile(sp["sw_q_norm_w"], (1, SW_HEADS)),
                          jnp.tile(sp["sw_k_norm_w"], (1, SW_HEADS))], axis=1)[:, :, None, :]
    return d


def _layer_fwd(x, mod, get_w, cs, tabs):
    wb = dict(get_w("w_in", x))
    h1, proj = norm_mm(x, cs["mix_nw"], mod[1], mod[0], wb["w_in"], swiglu=False, name="in_proj")
    act = dn_conv(proj, cs["cw8"])
    y, states, tinvs = dn_fwd(act, proj, cs["scal"], cs["wn"])
    y = gm_fwd(proj, cs["lng"], cs["lnb"], cs["w_s"], cs["bst"], y)
    y, swres = sw_forward(proj, cs["nw2"], *tabs, y)
    wb.update(get_w("w_out", y))
    x1, o1 = resid_mm(y, wb["w_out"], x, mod[2], name="out_proj")
    wb.update(get_w("ffn", x1))
    h2, gu, actf = norm_mm(x1, cs["ffn_nw"], mod[4], mod[3], wb["w_ffn_in"], swiglu=True, name="ffn_in")
    x2, o2 = resid_mm(actf, wb["w_ffn_out"], x1, mod[5], name="ffn_out")
    res = dict(x=x, h1=h1, proj=proj, act=act, states=states, tinvs=tinvs, swres=swres, y=y, x1=x1, o1=o1, h2=h2, gu=gu,
               actf=actf, o2=o2)
    return x2, res, wb


def _layer_bwd(dx2, res, mod, wb, cs, tabs, grads_done):
    dgu, gx2, dgate2 = resid_mm_bwd(dx2, mod[5], res["o2"], wb["w_ffn_out"], res["gu"], name="ffn_out_bwd", tm=512)
    g_wfo = mm_tn(res["actf"], gx2, name="wg_ffn_out")
    g_wfi = mm_tn(res["h2"], dgu, name="wg_ffn_in")
    token = grads_done("ffn", dict(w_ffn_in=g_wfi, w_ffn_out=g_wfo))
    dx1, d_ffn_nw, dscale2, dshift2 = norm_mm_bwd(dgu, wb["w_ffn_in"], res["x1"], cs["ffn_nw"], mod[4] + token, dx2,
                                                  name="ffn_in_bwd")
    dy, gx1, dgate1 = resid_mm_bwd(dx1, mod[2], res["o1"], wb["w_out"], None, name="out_proj_bwd", tm=512)
    g_wout = mm_tn(res["y"], gx1, name="wg_out")
    proj = res["proj"]
    dact, dproj, dpar = dn_bwd(res["act"], proj, cs["scal"], cs["wn"], res["states"], res["tinvs"], dy)
    dproj, dcw = dn_conv_bwd(proj, cs["cw8"], dact, dproj)
    dproj, dws, dbst, dln = gm_bwd(proj, cs["lng"], cs["lnb"], cs["w_s"], cs["bst"], dy, dproj)
    dproj, dnw = sw_backward(proj, cs["nw2"], *tabs, res["swres"], res["y"], dy, dproj)
    g_win = mm_tn(res["h1"], dproj, name="wg_in")
    dx, d_mix_nw, dscale1, dshift1 = norm_mm_bwd(dproj, wb["w_in"], res["x"], cs["mix_nw"], mod[1], dx1,
                                                 name="in_proj_bwd")
    dmod = jnp.concatenate([dshift1, dscale1, dgate1, dshift2, dscale2, dgate2], axis=1)
    dnw = dnw.reshape(2, SW_HEADS, SW_HEAD_DIM).sum(1)
    small = dict(mix_norm_w=d_mix_nw[0], ffn_norm_w=d_ffn_nw[0], dn_conv_w=dcw[:DN_CONV],
                 dn_a_log=dpar[0, :DN_HEADS], dn_dt_bias=dpar[1, :DN_HEADS], dn_out_norm_w=dpar[2],
                 gm_ln_g=dln[0], gm_ln_b=dln[1], gm_w_s=dws, gm_b_s=dbst[:, :GM_GROUPS].T,
                 sw_q_norm_w=dnw[0], sw_k_norm_w=dnw[1])
    token = grads_done("mix", dict(w_in=g_win, w_out=g_wout))
    return dx, small, dmod, token


def _permute_w_in(w):
    pad = jnp.zeros(w.shape[:-1] + (AB_PAD - 8,), w.dtype)
    return jnp.concatenate([w[..., 0:2056], pad, w[..., 2568:IN_WIDTH], w[..., 2056:2568]], axis=-1)


def _unpermute_w_in(g):
    return jnp.concatenate([g[..., 0:2056], g[..., C_UV:IN_PAD], g[..., C_SW:C_UV]], axis=-1)


def _local_step(x, target, mods, weights_of, grads_done, sp):
    layers = mods.shape[0]
    t, d = x.shape
    tabs = _rope_tables(t)
    consts = _consts(sp)
    saved = []
    for layer in range(layers):
        mod = mods[layer].reshape(6, 1, d)
        cs = {k: v[layer] for k, v in consts.items()}
        x, res, wb = _layer_fwd(x, mod, functools.partial(weights_of, layer), cs, tabs)
        saved.append((res, mod, wb, cs))
    dx, loss = loss_head(x, target)
    smalls, dmods = [], []
    token = jnp.zeros((1, 1), F32)
    for layer in reversed(range(layers)):
        res, mod, wb, cs = saved[layer]
        dx, small, dmod, token = _layer_bwd(dx, res, mod + token, wb, cs, tabs, functools.partial(grads_done, layer))
        smalls.append(small)
        dmods.append(dmod[0])
    smalls, dmods = smalls[::-1], dmods[::-1]
    small = {k: jnp.stack([s[k] for s in smalls]) for k in smalls[0]}
    return loss, dx, small, jnp.stack(dmods)


def mod_fwd(c_all, w_mod, b_shard):
    layers, d, n = w_mod.shape

    def body(c_ref, w_ref, b_ref, o_ref):
        ca = _silu(c_ref[...]).astype(BF16)
        o_ref[0] = _dot(ca, w_ref[0].astype(BF16), 1, 0) + b_ref[0]

    return _call(
        body, name="mod_fwd", grid=(layers,),
        in_specs=[_full((8, d)), pl.BlockSpec((1, d, n), lambda i: (i, 0, 0)),
                  pl.BlockSpec((1, 1, n), lambda i: (i, 0, 0))],
        out_specs=pl.BlockSpec((1, 8, n), lambda i: (i, 0, 0)),
        out_shape=jax.ShapeDtypeStruct((layers, 8, n), F32), semantics=("parallel",),
    )(c_all, w_mod, b_shard)


def mod_bwd(c_all, dmod):
    layers, _, n = dmod.shape
    d = c_all.shape[1]

    def body(c_ref, g_ref, o_ref):
        ca = _silu(c_ref[...]).astype(BF16)
        o_ref[0] = _dot(ca, g_ref[0].astype(BF16), 0, 0)

    return _call(
        body, name="mod_bwd", grid=(layers,),
        in_specs=[_full((8, d)), pl.BlockSpec((1, 8, n), lambda i: (i, 0, 0))],
        out_specs=pl.BlockSpec((1, d, n), lambda i: (i, 0, 0)),
        out_shape=jax.ShapeDtypeStruct((layers, d, n), F32), semantics=("parallel",),
    )(c_all, dmod)


N_DEV = 8


def _place():
    return lax.axis_index("x"), lax.axis_index("y"), lax.axis_index("c")


def _other_chips(x, y):
    return [(1 - x, y), (x, 1 - y), (1 - x, 1 - y)]


def allgather8(x_shard, *, name):
    m_per, n = x_shard.shape

    def body(x_ref, out_ref, send_sems, recv_sems, local_sem):
        x, y, c = _place()
        me, sibling = (x, y, c), (x, y, 1 - c)
        chips = _other_chips(x, y)

        def rows(px, py, pc):
            return out_ref.at[pl.ds((4 * px + 2 * py + pc) * m_per, m_per), :]

        def copy(k, block, to, src=None):
            return pltpu.make_async_remote_copy(
                src_ref=rows(*block) if src is None else src, dst_ref=rows(*block),
                send_sem=send_sems.at[k], recv_sem=recv_sems.at[k], device_id=to, device_id_type=MESH)

        mine = pltpu.make_async_copy(x_ref, rows(*me), local_sem)
        mine.start()
        first = [copy(0, me, sibling, src=x_ref)]
        first += [copy(1 + j, me, (*chip, c), src=x_ref) for j, chip in enumerate(chips)]
        for cp in first:
            cp.start()
        passed = [copy(4 + j, (*chip, c), sibling) for j, chip in enumerate(chips)]
        for j, chip in enumerate(chips):
            copy(1 + j, (*chip, c), me).wait_recv()
            passed[j].start()
        copy(0, sibling, me).wait_recv()
        for j, chip in enumerate(chips):
            copy(4 + j, (*chip, 1 - c), me).wait_recv()
        for cp in first + passed:
            cp.wait_send()
        mine.wait()

    return pl.pallas_call(
        body, name=name, out_shape=jax.ShapeDtypeStruct((N_DEV * m_per, n), x_shard.dtype),
        in_specs=[pl.BlockSpec(memory_space=pltpu.VMEM)], out_specs=pl.BlockSpec(memory_space=pltpu.VMEM),
        scratch_shapes=[pltpu.SemaphoreType.DMA((7,)), pltpu.SemaphoreType.DMA((7,)), pltpu.SemaphoreType.DMA],
    )(x_shard)


HBM = pl.BlockSpec(memory_space=pltpu.HBM)
SEM = pl.BlockSpec(memory_space=pltpu.SEMAPHORE)
_EFFECT = pltpu.SideEffectType.DATAFLOW_SIDE_EFFECTING


def _piece(ref, sliced, chip):
    return ref.at[2 * chip[0] + chip[1]] if sliced else ref


def exchange_start(srcs, after, *, sliced, name):
    n = len(srcs)
    piece = lambda s: s.shape[1:] if sliced else s.shape

    def body(*refs):
        ins, lands = refs[:n], refs[n:2 * n]
        send_sems, recv_sems = refs[2 * n + len(after):2 * n + len(after) + 2]
        token = refs[-1]
        x, y, c = _place()
        me_s = 2 * x + y
        for a in range(n):
            for j, chip in enumerate(_other_chips(x, y)):
                pltpu.make_async_remote_copy(
                    src_ref=_piece(ins[a], sliced, chip), dst_ref=lands[a].at[me_s], send_sem=send_sems.at[3 * a + j],
                    recv_sem=recv_sems.at[3 * a + j], device_id=(*chip, c), device_id_type=MESH).start()
        token[...] = jnp.zeros_like(token)

    zones = [pltpu.with_memory_space_constraint(lax.empty((4,) + piece(s), s.dtype), pltpu.HBM) for s in srcs]
    srcs = [pltpu.with_memory_space_constraint(s, pltpu.HBM) for s in srcs]
    out = pl.pallas_call(
        body, name=name,
        out_shape=(pltpu.SemaphoreType.DMA((3 * n,)), pltpu.SemaphoreType.DMA((3 * n,)),
                   *[pltpu.HBM(s.shape, s.dtype) for s in srcs], *[pltpu.HBM(z.shape, z.dtype) for z in zones],
                   jax.ShapeDtypeStruct((8, LANE), F32)),
        in_specs=[HBM] * (2 * n) + [ANY] * len(after),
        out_specs=(SEM, SEM, *[HBM] * (2 * n), pl.BlockSpec(memory_space=pltpu.VMEM)),
        input_output_aliases={i: 2 + i for i in range(2 * n)},
        compiler_params=pltpu.CompilerParams(has_side_effects=_EFFECT),
    )(*srcs, *zones, *after)
    return out[0], out[1], out[2:2 + n], out[2 + n:2 + 2 * n], out[-1]


def exchange_wait(send_sems, recv_sems, srcs, zones, after, *, which, sliced, name):
    n = len(srcs)

    def body(*refs):
        ins, lands = refs[:n], refs[n:2 * n]
        send_sems, recv_sems = refs[2 * n:2 * n + 2]
        x, y, c = _place()
        for a in range(n):
            for j, chip in enumerate(_other_chips(x, y)):
                copy = pltpu.make_async_remote_copy(
                    src_ref=_piece(ins[a], sliced, chip), dst_ref=lands[a].at[2 * chip[0] + chip[1]],
                    send_sem=send_sems.at[3 * which[a] + j], recv_sem=recv_sems.at[3 * which[a] + j],
                    device_id=(*chip, c), device_id_type=MESH)
                copy.wait_send()
                copy.wait_recv()

    out = pl.pallas_call(
        body, name=name,
        out_shape=tuple(pltpu.HBM(s.shape, s.dtype) for s in (*srcs, *zones)),
        in_specs=[HBM] * (2 * n) + [SEM, SEM, ANY], out_specs=tuple([HBM] * (2 * n)),
        input_output_aliases={i: i for i in range(2 * n)},
        compiler_params=pltpu.CompilerParams(has_side_effects=_EFFECT),
    )(*srcs, *zones, send_sems, recv_sems, after)
    return out[n:]


def sibling_swap(parts):
    n = len(parts)

    def body(*refs):
        ins, outs = refs[:n], refs[n:2 * n]
        send_sems, recv_sems = refs[2 * n:]
        x, y, c = _place()
        cps = []
        for a in range(n):
            cp = pltpu.make_async_remote_copy(
                src_ref=ins[a], dst_ref=outs[a], send_sem=send_sems.at[a], recv_sem=recv_sems.at[a],
                device_id=(x, y, 1 - c), device_id_type=MESH)
            cp.start()
            cps.append(cp)
        for cp in cps:
            cp.wait()

    return pl.pallas_call(
        body, name="sibling_swap", out_shape=[jax.ShapeDtypeStruct(p.shape, p.dtype) for p in parts],
        in_specs=[ANY] * n, out_specs=[ANY] * n,
        scratch_shapes=[pltpu.SemaphoreType.DMA((n,)), pltpu.SemaphoreType.DMA((n,))],
    )(*parts)


def _row_block(rows, cols, budget=1 << 20):
    best = rows if rows % 8 else 8
    for tr in range(8, rows + 1, 8):
        if rows % tr == 0 and tr * cols * 4 <= budget:
            best = tr
    return best


def chip_sum(own, recv, me_s, buf, layer, layers, *, name):
    _, r, n = own.shape
    tr = _row_block(r, n)
    steps = r // tr

    def body(me_ref, own_ref, recv_ref, *rest):
        o_ref = rest[-1]
        me = me_ref[0]
        acc = jnp.zeros((tr, n), F32)
        for s in range(4):
            acc = acc + jnp.where(me == s, own_ref[0], recv_ref[s].astype(F32))
        o_ref[...] = acc

    in_specs = [pl.BlockSpec((1, tr, n), lambda i, me: (me[0], i, 0)), pl.BlockSpec((4, tr, n), lambda i, me: (0, i, 0))]
    args = [me_s, own, recv]
    aliases = {}
    if buf is not None:
        in_specs.append(ANY)
        args.append(buf)
        aliases = {3: 0}
    return pl.pallas_call(
        body, name=name, out_shape=jax.ShapeDtypeStruct((layers * r, n), F32),
        grid_spec=pltpu.PrefetchScalarGridSpec(
            num_scalar_prefetch=1, grid=(steps,), in_specs=in_specs,
            out_specs=pl.BlockSpec((tr, n), lambda i, me: (layer * steps + i, 0))),
        input_output_aliases=aliases,
        compiler_params=pltpu.CompilerParams(dimension_semantics=("parallel",)),
    )(*args)


def _adam_update(w, g, m, v):
    m2 = ADAM_B1 * m + (1.0 - ADAM_B1) * g
    v2 = ADAM_B2 * v + (1.0 - ADAM_B2) * (g * g)
    m_hat = m2 / (1.0 - ADAM_B1 ** ADAM_STEP)
    v_hat = v2 / (1.0 - ADAM_B2 ** ADAM_STEP)
    delta = -ADAM_LR * (m_hat / (jnp.sqrt(v_hat) + ADAM_EPS) + ADAM_WD * w)
    return delta, m2, v2


def adamw(w, g_parts, m, v, *, name):
    r, n = w.shape
    tr = _row_block(r, n)
    k = len(g_parts)

    def body(*refs):
        w_ref, m_ref, v_ref = refs[k], refs[k + 1], refs[k + 2]
        g_ref, d_ref, m2_ref, v2_ref = refs[k + 3:]
        g = refs[0][...]
        for p in refs[1:k]:
            g = g + p[...]
        g_ref[...] = g
        d_ref[...], m2_ref[...], v2_ref[...] = _adam_update(w_ref[...], g, m_ref[...], v_ref[...])

    blk = pl.BlockSpec((tr, n), lambda i: (i, 0))
    shp = jax.ShapeDtypeStruct((r, n), F32)
    return _call(body, name=name, grid=(r // tr,), in_specs=[blk] * (k + 3), out_specs=[blk] * 4,
                 out_shape=[shp] * 4, semantics=("parallel",))(*g_parts, w, m, v)


def adamw_gathered(g_all, w, m, v, *, name):
    _, r, n = g_all.shape
    tr = _row_block(r, n * 4)

    def body(ga_ref, w_ref, m_ref, v_ref, g_ref, d_ref, m2_ref, v2_ref):
        g = ga_ref[0]
        for dev in range(1, N_DEV):
            g = g + ga_ref[dev]
        g_ref[...] = g
        d_ref[...], m2_ref[...], v2_ref[...] = _adam_update(w_ref[...], g, m_ref[...], v_ref[...])

    blk = pl.BlockSpec((tr, n), lambda i: (i, 0))
    shp = jax.ShapeDtypeStruct((r, n), F32)
    return _call(body, name=name, grid=(r // tr,),
                 in_specs=[pl.BlockSpec((N_DEV, tr, n), lambda i: (0, i, 0)), blk, blk, blk], out_specs=[blk] * 4,
                 out_shape=[shp] * 4, semantics=("parallel",))(g_all, w, m, v)


BIG = ("w_in", "w_out", "w_ffn_in", "w_ffn_out")
SMALL = ("b_mod", "mix_norm_w", "ffn_norm_w", "dn_conv_w", "dn_a_log", "dn_dt_bias", "dn_out_norm_w", "gm_ln_g",
         "gm_ln_b", "gm_w_s", "gm_b_s", "sw_q_norm_w", "sw_k_norm_w")
WEIGHTS = ("w_mod", "b_mod", "mix_norm_w", "ffn_norm_w", "w_in", "w_out", "dn_conv_w", "dn_a_log", "dn_dt_bias",
           "dn_out_norm_w", "gm_ln_g", "gm_ln_b", "gm_w_s", "gm_b_s", "sw_q_norm_w", "sw_k_norm_w", "w_ffn_in",
           "w_ffn_out")
PACK_ROWS = 8


def _pack(arrs):
    out = []
    for a in arrs:
        flat = a.reshape(-1).astype(F32)
        rows = -(-flat.shape[0] // (LANE * PACK_ROWS)) * PACK_ROWS
        out.append(jnp.pad(flat, (0, rows * LANE - flat.shape[0])).reshape(rows, LANE))
    return jnp.concatenate(out, axis=0)


def _unpack(packed, shapes):
    out, r0 = [], 0
    for shp in shapes:
        size = math.prod(shp)
        rows = -(-size // (LANE * PACK_ROWS)) * PACK_ROWS
        out.append(packed[r0:r0 + rows].reshape(-1)[:size].reshape(shp))
        r0 += rows
    return out


def kernel(x, c, w_mod, b_mod, mix_norm_w, ffn_norm_w, w_in, w_out, dn_conv_w, dn_a_log, dn_dt_bias, dn_out_norm_w, gm_ln_g, gm_ln_b, gm_w_s, gm_b_s, sw_q_norm_w, sw_k_norm_w, w_ffn_in, w_ffn_out, loss_target, m_w_mod, m_b_mod, m_mix_norm_w, m_ffn_norm_w, m_w_in, m_w_out, m_dn_conv_w, m_dn_a_log, m_dn_dt_bias, m_dn_out_norm_w, m_gm_ln_g, m_gm_ln_b, m_gm_w_s, m_gm_b_s, m_sw_q_norm_w, m_sw_k_norm_w, m_w_ffn_in, m_w_ffn_out, v_w_mod, v_b_mod, v_mix_norm_w, v_ffn_norm_w, v_w_in, v_w_out, v_dn_conv_w, v_dn_a_log, v_dn_dt_bias, v_dn_out_norm_w, v_gm_ln_g, v_gm_ln_b, v_gm_w_s, v_gm_b_s, v_sw_q_norm_w, v_sw_k_norm_w, v_w_ffn_in, v_w_ffn_out):
    w = dict(w_mod=w_mod, b_mod=b_mod, mix_norm_w=mix_norm_w, ffn_norm_w=ffn_norm_w, w_in=w_in, w_out=w_out,
             dn_conv_w=dn_conv_w, dn_a_log=dn_a_log, dn_dt_bias=dn_dt_bias, dn_out_norm_w=dn_out_norm_w,
             gm_ln_g=gm_ln_g, gm_ln_b=gm_ln_b, gm_w_s=gm_w_s, gm_b_s=gm_b_s, sw_q_norm_w=sw_q_norm_w,
             sw_k_norm_w=sw_k_norm_w, w_ffn_in=w_ffn_in, w_ffn_out=w_ffn_out)
    m = dict(w_mod=m_w_mod, b_mod=m_b_mod, mix_norm_w=m_mix_norm_w, ffn_norm_w=m_ffn_norm_w, w_in=m_w_in,
             w_out=m_w_out, dn_conv_w=m_dn_conv_w, dn_a_log=m_dn_a_log, dn_dt_bias=m_dn_dt_bias,
             dn_out_norm_w=m_dn_out_norm_w, gm_ln_g=m_gm_ln_g, gm_ln_b=m_gm_ln_b, gm_w_s=m_gm_w_s, gm_b_s=m_gm_b_s,
             sw_q_norm_w=m_sw_q_norm_w, sw_k_norm_w=m_sw_k_norm_w, w_ffn_in=m_w_ffn_in, w_ffn_out=m_w_ffn_out)
    v = dict(w_mod=v_w_mod, b_mod=v_b_mod, mix_norm_w=v_mix_norm_w, ffn_norm_w=v_ffn_norm_w, w_in=v_w_in,
             w_out=v_w_out, dn_conv_w=v_dn_conv_w, dn_a_log=v_dn_a_log, dn_dt_bias=v_dn_dt_bias,
             dn_out_norm_w=v_dn_out_norm_w, gm_ln_g=v_gm_ln_g, gm_ln_b=v_gm_ln_b, gm_w_s=v_gm_w_s, gm_b_s=v_gm_b_s,
             sw_q_norm_w=v_sw_q_norm_w, sw_k_norm_w=v_sw_k_norm_w, w_ffn_in=v_w_ffn_in, w_ffn_out=v_w_ffn_out)
    layers, d, mod_n = w_mod.shape
    mx, my, mc = _place()
    me_s = 2 * mx + my
    me_dev = 4 * mx + 2 * my + mc

    c_all = allgather8(_pad_rows(c, 8), name="gather_c").reshape(N_DEV, 8, d)[:, 0]
    b_shard = lax.dynamic_slice_in_dim(b_mod, me_s * mod_n, mod_n, axis=1)[:, None, :]
    mod_part = mod_fwd(c_all, w_mod, b_shard)
    mod_parts = allgather8(mod_part.reshape(layers * 8, mod_n), name="gather_mod")
    mod_parts = mod_parts.reshape(4, 2, layers, 8, mod_n)[:, 0]
    mod_all = mod_parts.transpose(1, 2, 0, 3).reshape(layers, 8, 4 * mod_n)
    mods = lax.dynamic_index_in_dim(mod_all, me_dev, axis=1, keepdims=False)

    cw = dn_conv_w.shape[-1]
    conv_rows = -(-layers * DN_CONV // 8) * 8
    conv_parts = allgather8(_pad_rows(dn_conv_w.reshape(layers * DN_CONV, cw), conv_rows), name="gather_conv")
    conv_parts = conv_parts.reshape(4, 2, conv_rows, cw)[:, 0, :layers * DN_CONV]
    conv_full = conv_parts.reshape(4, layers, DN_CONV, cw).transpose(1, 2, 0, 3).reshape(layers, DN_CONV, 4 * cw)

    shards = {k: w[k].astype(BF16) for k in BIG}
    groups = dict(w_in=(0,), w_out=(1,), ffn=(2, 3))
    gathers = [exchange_start([shards[k][layer] for k in BIG], [mods, conv_full], sliced=False, name=f"gather_start{layer}")
               for layer in range(layers)]
    mods = mods + sum(g[4][0, 0] for g in gathers)

    def weights_of(layer, group, after):
        send_sems, recv_sems, srcs, zones, _ = gathers[layer]
        which = groups[group]
        got = exchange_wait(send_sems, recv_sems, [srcs[a] for a in which], [zones[a] for a in which], after,
                            which=which, sliced=False, name=f"gather_wait_{group}{layer}")
        full = {BIG[a]: lax.dynamic_update_index_in_dim(z, shards[BIG[a]][layer], me_s, 0) for a, z in zip(which, got)}
        cols = lambda g: jnp.concatenate([g[s] for s in range(4)], axis=-1)
        shape = dict(w_in=lambda g: _permute_w_in(cols(g)), w_out=lambda g: g.reshape(-1, d), w_ffn_in=cols,
                     w_ffn_out=lambda g: g.reshape(-1, d))
        return {k: shape[k](g) for k, g in full.items()}

    scatters = {}
    cut = lambda g, axis: jnp.stack(jnp.split(g, 4, axis=axis))
    shard_major = dict(w_in=lambda g: cut(_unpermute_w_in(g), 1), w_out=lambda g: cut(g, 0),
                       w_ffn_in=lambda g: cut(g, 1), w_ffn_out=lambda g: cut(g, 0))

    def grads_done(layer, group, grads):
        own = {k: shard_major[k](g) for k, g in grads.items()}
        started = exchange_start([g.astype(BF16) for g in own.values()], [], sliced=True,
                                 name=f"scatter_start_{group}{layer}")
        scatters[layer, group] = (started, own)
        return started[4][:1, :1]

    sp = {k: w[k] for k in SMALL}
    sp["dn_conv_w"] = conv_full
    loss_blk, grad_x, small, dmods = _local_step(x[0], loss_target[0], mods, weights_of, grads_done, sp)
    loss = lax.psum(loss_blk[0, 0], ("x", "y", "c"))

    outs = {}
    small = dict(small, b_mod=dmods)
    packed = _pack([small[k] for k in SMALL])
    rows = packed.shape[0]
    g_all = allgather8(packed, name="gather_small").reshape(N_DEV, rows, LANE)
    conv_zero = jnp.zeros((layers, DN_CONV, 3 * DN_WIDTH), F32)
    pk = lambda src: _pack([conv_zero if k == "dn_conv_w" else src[k] for k in SMALL])
    res = adamw_gathered(g_all, pk(w), pk(m), pk(v), name="adamw_small")
    shapes = [small[k].shape for k in SMALL]
    un = [_unpack(a, shapes) for a in res]
    for i, k in enumerate(SMALL):
        outs[k] = [un[j][i] for j in range(4)]
    g_conv = lax.dynamic_slice_in_dim(outs["dn_conv_w"][0], me_s * cw, cw, axis=2)
    flat = lambda a: a.reshape(-1, cw)
    res = adamw(flat(dn_conv_w), [flat(g_conv)], flat(m["dn_conv_w"]), flat(v["dn_conv_w"]), name="adamw_conv")
    outs["dn_conv_w"] = [a.reshape(dn_conv_w.shape) for a in res]

    b_rows = layers * 6 * d // LANE
    dmod_all = g_all[:, :b_rows].reshape(N_DEV, layers, 6 * d).transpose(1, 0, 2)
    dmod_shard = lax.dynamic_slice_in_dim(dmod_all, me_s * mod_n, mod_n, axis=2)
    g_wmod = mod_bwd(c_all, dmod_shard)
    flat = lambda a: a.reshape(-1, mod_n)
    res = adamw(flat(w_mod), [flat(g_wmod)], flat(m_w_mod), flat(v_w_mod), name="adamw_w_mod")
    outs["w_mod"] = [a.reshape(w_mod.shape) for a in res]

    me_arr = jnp.reshape(me_s, (1,)).astype(jnp.int32)
    partial = {k: None for k in BIG}
    for layer in range(layers):
        for group in ("ffn", "mix"):
            (send_sems, recv_sems, srcs, zones, _), own = scatters[layer, group]
            zones = exchange_wait(send_sems, recv_sems, srcs, zones, res[0], which=tuple(range(len(srcs))),
                                  sliced=True, name=f"scatter_wait_{group}{layer}")
            for k, z in zip(own, zones):
                partial[k] = chip_sum(own[k], z, me_arr, partial[k], layer, layers, name=f"chip_sum_{k}{layer}")
    partial = [partial[k] for k in BIG]
    theirs = sibling_swap(partial)
    for k, mine, other in zip(BIG, partial, theirs):
        shp = w[k].shape
        flat = lambda a: a.reshape(-1, shp[-1])
        res = adamw(flat(w[k]), [mine, other], flat(m[k]), flat(v[k]), name="adamw_" + k)
        outs[k] = [a.reshape(shp) for a in res]

    result = [loss, grad_x[None]]
    for j in range(4):
        result += [outs[k][j] for k in WEIGHTS]
    return tuple(result)
```

```python
import functools
import math

import jax
import jax.numpy as jnp
from jax import lax
from jax.experimental import pallas as pl
from jax.experimental.pallas import tpu as pltpu

F32 = jnp.float32
BF16 = jnp.bfloat16
HI = lax.Precision.HIGH

NORM_EPS = 1e-6
DN_HEADS = 4
DN_HEAD_DIM = 128
DN_WIDTH = 512
DN_CHUNK = 64
DN_CONV = 4
GM_WIDTH = 256
GM_GROUPS = 4
GM_GROUP_DIM = 64
GM_CHUNK = 128
SW_HEADS = 4
SW_HEAD_DIM = 64
SW_WIDTH = 256
SW_DILATIONS = (1, 4, 16)
SW_BLOCK = 128
ROPE_THETA = 500000.0
ROPE_DIM = 16
LANE = 128

C_QKV = 0
C_Z = 1536
C_AB = 2048
C_SW = 2304
C_UV = 4608
IN_WIDTH = 4872
IN_PAD = 5120
AB_PAD = C_SW - C_AB
MIX_WIDTH = 1024

ADAM_LR = 0.001
ADAM_B1 = 0.9
ADAM_B2 = 0.999
ADAM_EPS = 1e-08
ADAM_WD = 0.01
ADAM_STEP = 10

MESH = pl.DeviceIdType.MESH


BIG_VMEM = 56 << 20


def _call(body, *, name, grid, in_specs, out_specs, out_shape, scratch_shapes=(), semantics=None, aliases=None,
          vmem=None):
    if semantics is None:
        semantics = ("arbitrary",) * len(grid)
    return pl.pallas_call(
        body, name=name, grid=grid, in_specs=in_specs, out_specs=out_specs, out_shape=out_shape,
        scratch_shapes=list(scratch_shapes), input_output_aliases=aliases or {},
        compiler_params=pltpu.CompilerParams(dimension_semantics=semantics, vmem_limit_bytes=vmem),
    )


def _dot(a, b, ca, cb, prec=None):
    if a.ndim == 3:
        dims = (((ca + 1,), (cb + 1,)), ((0,), (0,)))
    else:
        dims = (((ca,), (cb,)), ((), ()))
    return lax.dot_general(a, b, dims, preferred_element_type=F32, precision=prec)


def _bdot(a, b, ca=1, cb=0):
    return _dot(a.astype(BF16), b.astype(BF16), ca, cb)


def _hdot(a, b, ca=1, cb=0):
    return _dot(a.astype(F32), b.astype(F32), ca, cb, HI)


def _split(x):
    hi = x.astype(BF16)
    return hi, (x - hi.astype(F32)).astype(BF16)


def _xdot(a, b, ca=1, cb=0, exact=1):
    if exact == 1:
        hi, lo = _split(a)
        e = b.astype(BF16)
        return _dot(hi, e, ca, cb) + _dot(lo, e, ca, cb)
    hi, lo = _split(b)
    e = a.astype(BF16)
    return _dot(e, hi, ca, cb) + _dot(e, lo, ca, cb)


def _sigmoid(x):
    return 0.5 * jnp.tanh(0.5 * x) + 0.5


def _silu(x):
    return x * _sigmoid(x)


def _dsilu(x):
    s = _sigmoid(x)
    return s * (1.0 + x * (1.0 - s))


def _softplus(x):
    return jnp.maximum(x, 0.0) + jnp.log(1.0 + jnp.exp(-jnp.abs(x)))


def _iota2(shape, dim):
    return lax.broadcasted_iota(jnp.int32, shape, dim)


def _rowsum(x):
    return jnp.sum(x, axis=-1, keepdims=True)


def _colsum(x):
    return jnp.sum(x, axis=-2, keepdims=True)


def _full(shape):
    return pl.BlockSpec(shape, lambda *_: (0,) * len(shape))


def _resident(shape):
    return pl.BlockSpec(shape, lambda *_: (0,) * len(shape), pipeline_mode=pl.Buffered(1))


ANY = pl.BlockSpec(memory_space=pl.ANY)


def _norm_mod(x, nw, scale, shift):
    r = lax.rsqrt(jnp.mean(x * x, axis=-1, keepdims=True) + NORM_EPS)
    xn = x * r
    return xn, r, (xn * nw) * (1.0 + scale) + shift


def norm_mm(x, nw, scale, shift, w, *, swiglu, name, tm=512):
    t, d = x.shape
    n = w.shape[1]
    half = n // 2

    def body(x_ref, nw_ref, sc_ref, sh_ref, w_ref, h_ref, y_ref, *act_ref):
        _, _, h = _norm_mod(x_ref[...], nw_ref[...], sc_ref[...], sh_ref[...])
        hb = h.astype(BF16)
        h_ref[...] = hb
        y = _dot(hb, w_ref[...], 1, 0)
        y_ref[...] = y.astype(y_ref.dtype)
        if swiglu:
            act_ref[0][...] = (_silu(y[:, :half]) * y[:, half:]).astype(BF16)

    row = lambda i: (i, 0)
    out_shape = [jax.ShapeDtypeStruct((t, d), BF16), jax.ShapeDtypeStruct((t, n), BF16 if swiglu else F32)]
    out_specs = [pl.BlockSpec((tm, d), row), pl.BlockSpec((tm, n), row)]
    if swiglu:
        out_shape.append(jax.ShapeDtypeStruct((t, half), BF16))
        out_specs.append(pl.BlockSpec((tm, half), row))
    return _call(
        body, name=name, grid=(t // tm,),
        in_specs=[pl.BlockSpec((tm, d), row), _full((1, d)), _full((1, d)), _full((1, d)), _resident((d, n))],
        out_specs=out_specs, out_shape=out_shape, semantics=("parallel",), vmem=BIG_VMEM,
    )(x, nw, scale, shift, w)


def resid_mm(y, w, x, gate, *, name, tm=512):
    t, k = y.shape
    d = w.shape[1]

    def body(y_ref, w_ref, x_ref, g_ref, xo_ref, o_ref):
        o = _dot(y_ref[...].astype(BF16), w_ref[...], 1, 0)
        o_ref[...] = o
        xo_ref[...] = x_ref[...] + g_ref[...] * o

    row = lambda i: (i, 0)
    return _call(
        body, name=name, grid=(t // tm,),
        in_specs=[pl.BlockSpec((tm, k), row), _resident((k, d)), pl.BlockSpec((tm, d), row), _full((1, d))],
        out_specs=[pl.BlockSpec((tm, d), row), pl.BlockSpec((tm, d), row)],
        out_shape=[jax.ShapeDtypeStruct((t, d), F32), jax.ShapeDtypeStruct((t, d), F32)],
        semantics=("parallel",), vmem=BIG_VMEM,
    )(y, w, x, gate)


def resid_mm_bwd(dx, gate, o, w, gu, *, name, tm):
    t, d = dx.shape
    k = w.shape[0]
    swiglu = gu is not None

    def body(dx_ref, g_ref, o_ref, w_ref, *rest):
        if swiglu:
            gu_ref, dy_ref, gx_ref, dg_ref = rest
        else:
            dy_ref, gx_ref, dg_ref = rest
        i = pl.program_id(0)
        dxv = dx_ref[...]
        gx = (dxv * g_ref[...]).astype(BF16)
        gx_ref[...] = gx
        part = _colsum(dxv * o_ref[...])

        @pl.when(i == 0)
        def _():
            dg_ref[...] = jnp.zeros_like(dg_ref)

        dg_ref[...] += part
        da = _dot(gx, w_ref[...], 1, 1)
        if swiglu:
            g = gu_ref[:, :k].astype(F32)
            u = gu_ref[:, k:].astype(F32)
            dy_ref[:, :k] = (da * u * _dsilu(g)).astype(BF16)
            dy_ref[:, k:] = (da * _silu(g)).astype(BF16)
        else:
            dy_ref[...] = da

    row = lambda i: (i, 0)
    in_specs = [pl.BlockSpec((tm, d), row), _full((1, d)), pl.BlockSpec((tm, d), row), _resident((k, d))]
    args = [dx, gate, o, w]
    if swiglu:
        in_specs.append(pl.BlockSpec((tm, 2 * k), row))
        args.append(gu)
        dy_shape = jax.ShapeDtypeStruct((t, 2 * k), BF16)
        dy_spec = pl.BlockSpec((tm, 2 * k), row)
    else:
        dy_shape = jax.ShapeDtypeStruct((t, k), F32)
        dy_spec = pl.BlockSpec((tm, k), row)
    return _call(
        body, name=name, grid=(t // tm,), in_specs=in_specs,
        out_specs=[dy_spec, pl.BlockSpec((tm, d), row), _full((1, d))],
        out_shape=[dy_shape, jax.ShapeDtypeStruct((t, d), BF16), jax.ShapeDtypeStruct((1, d), F32)], vmem=BIG_VMEM,
    )(*args)


def norm_mm_bwd(dy, w, x, nw, scale, dres, *, name, tm=512):
    t, n = dy.shape
    d = x.shape[1]
    steps = t // tm

    def body(dy_ref, w_ref, x_ref, nw_ref, sc_ref, dres_ref, dx_ref, dnw_ref, dsc_ref, dsh_ref):
        i = pl.program_id(0)
        dh = _dot(dy_ref[...].astype(BF16), w_ref[...], 1, 1)
        x = x_ref[...]
        r = lax.rsqrt(jnp.mean(x * x, axis=-1, keepdims=True) + NORM_EPS)
        xn = x * r
        a = nw_ref[...] * (1.0 + sc_ref[...])

        @pl.when(i == 0)
        def _():
            dnw_ref[...] = jnp.zeros_like(dnw_ref)
            dsh_ref[...] = jnp.zeros_like(dsh_ref)

        dnw_ref[...] += _colsum(dh * xn)
        dsh_ref[...] += _colsum(dh)
        dxn = dh * a
        dx_ref[...] = r * (dxn - xn * jnp.mean(dxn * xn, axis=-1, keepdims=True)) + dres_ref[...]

        @pl.when(i == steps - 1)
        def _():
            da = dnw_ref[...]
            dsc_ref[...] = da * nw_ref[...]
            dnw_ref[...] = da * (1.0 + sc_ref[...])

    row = lambda i: (i, 0)
    vec = jax.ShapeDtypeStruct((1, d), F32)
    return _call(
        body, name=name, grid=(steps,),
        in_specs=[pl.BlockSpec((tm, n), row), _resident((d, n)), pl.BlockSpec((tm, d), row), _full((1, d)),
                  _full((1, d)), pl.BlockSpec((tm, d), row)],
        out_specs=[pl.BlockSpec((tm, d), row), _full((1, d)), _full((1, d)), _full((1, d))],
        out_shape=[jax.ShapeDtypeStruct((t, d), F32), vec, vec, vec], vmem=BIG_VMEM,
    )(dy, w, x, nw, scale, dres)


def _pick_tn(n, k, budget=6 << 20):
    best = LANE
    for m in range(1, n // LANE + 1):
        tn = m * LANE
        if n % tn == 0 and k * tn * 4 <= budget:
            best = tn
    return best


def mm_tn(a, g, *, name, tt=512):
    t, k = a.shape
    n = g.shape[1]
    tn = _pick_tn(n, k)

    def body(a_ref, g_ref, o_ref):
        @pl.when(pl.program_id(1) == 0)
        def _():
            o_ref[...] = jnp.zeros_like(o_ref)

        o_ref[...] += _dot(a_ref[...].astype(BF16), g_ref[...].astype(BF16), 0, 0)

    return _call(
        body, name=name, grid=(n // tn, t // tt),
        in_specs=[pl.BlockSpec((tt, k), lambda j, i: (i, 0)), pl.BlockSpec((tt, tn), lambda j, i: (i, j))],
        out_specs=pl.BlockSpec((k, tn), lambda j, i: (0, j)),
        out_shape=jax.ShapeDtypeStruct((k, n), F32), semantics=("parallel", "arbitrary"),
    )(a, g)


def loss_head(y, target, *, tm=512):
    t, d = y.shape
    steps = t // tm

    def body(y_ref, t_ref, dy_ref, l_ref, acc_ref):
        i = pl.program_id(0)

        @pl.when(i == 0)
        def _():
            acc_ref[...] = jnp.zeros_like(acc_ref)

        e = y_ref[...] - t_ref[...]
        dy_ref[...] = e * (1.0 / d)
        acc_ref[...] += _colsum(e * e)

        @pl.when(i == steps - 1)
        def _():
            tot = jnp.sum(acc_ref[...], axis=-1, keepdims=True) * (0.5 / d)
            l_ref[...] = jnp.broadcast_to(tot, l_ref.shape)

    row = lambda i: (i, 0)
    return _call(
        body, name="loss_head", grid=(steps,),
        in_specs=[pl.BlockSpec((tm, d), row), pl.BlockSpec((tm, d), row)],
        out_specs=[pl.BlockSpec((tm, d), row), _full((8, LANE))],
        out_shape=[jax.ShapeDtypeStruct((t, d), F32), jax.ShapeDtypeStruct((8, LANE), F32)],
        scratch_shapes=[pltpu.VMEM((1, d), F32)],
    )(y, target)


def _shift_rows(x, s):
    if s == 0:
        return x
    t = x.shape[0]
    ri = _iota2(x.shape, 0)
    rolled = pltpu.roll(x, s % t, axis=0)
    if s > 0:
        return jnp.where(ri >= s, rolled, 0.0)
    return jnp.where(ri < t + s, rolled, 0.0)


def _conv_pre(x, w):
    acc = x * w[DN_CONV - 1:DN_CONV, :]
    for j in range(DN_CONV - 1):
        acc = acc + _shift_rows(x, DN_CONV - 1 - j) * w[j:j + 1, :]
    return acc


def dn_conv(proj, conv_w):
    t = proj.shape[0]
    width = 3 * DN_WIDTH

    def body(x_ref, w_ref, o_ref):
        o_ref[...] = _silu(_conv_pre(x_ref[...], w_ref[...]))

    col = lambda j: (0, j)
    return _call(
        body, name="dn_conv", grid=(width // LANE,),
        in_specs=[pl.BlockSpec((t, LANE), col), pl.BlockSpec((8, LANE), col)],
        out_specs=pl.BlockSpec((t, LANE), col),
        out_shape=jax.ShapeDtypeStruct((t, width), F32), semantics=("parallel",),
    )(proj, conv_w)


def dn_conv_bwd(proj, conv_w, dact, dproj):
    t = proj.shape[0]
    width = 3 * DN_WIDTH

    def body(x_ref, w_ref, d_ref, _, dx_ref, dw_ref):
        x = x_ref[...]
        w = w_ref[...]
        dc = d_ref[...] * _dsilu(_conv_pre(x, w))
        dx = dc * w[DN_CONV - 1:DN_CONV, :]
        rows = []
        for j in range(DN_CONV - 1):
            s = DN_CONV - 1 - j
            dx = dx + _shift_rows(dc, -s) * w[j:j + 1, :]
            rows.append(_colsum(dc * _shift_rows(x, s)))
        rows.append(_colsum(dc * x))
        dx_ref[...] = dx.astype(BF16)
        ri = _iota2((8, LANE), 0)
        dw = jnp.zeros((8, LANE), F32)
        for j in range(DN_CONV):
            dw = dw + jnp.where(ri == j, rows[j], 0.0)
        dw_ref[...] = dw

    col = lambda j: (0, j)
    return _call(
        body, name="dn_conv_bwd", grid=(width // LANE,),
        in_specs=[pl.BlockSpec((t, LANE), col), pl.BlockSpec((8, LANE), col), pl.BlockSpec((t, LANE), col), ANY],
        out_specs=[pl.BlockSpec((t, LANE), col), pl.BlockSpec((8, LANE), col)],
        out_shape=[jax.ShapeDtypeStruct(dproj.shape, dproj.dtype), jax.ShapeDtypeStruct((8, width), F32)],
        semantics=("parallel",), aliases={3: 0},
    )(proj, conv_w, dact, dproj)


def _t(x):
    return jnp.swapaxes(x, -1, -2)


def _inv_unit_lower(a):
    c = a.shape[-1]
    eye = (_iota2((c, c), 0) == _iota2((c, c), 1)).astype(F32)
    x = eye - a
    p = _hdot(a, a)
    steps = int(math.log2(c)) - 1
    for i in range(steps):
        x = x + _hdot(x, p)
        if i < steps - 1:
            p = _hdot(p, p)
    return x


def _dn_chunk(q, k, v, a, b, alog, dtb, s_in, tinv=None):
    nh, c, d = q.shape
    rq = lax.rsqrt(_rowsum(q * q) + NORM_EPS)
    rk = lax.rsqrt(_rowsum(k * k) + NORM_EPS)
    qh = q * rq
    kn = k * rk
    qs = qh * (d ** -0.5)
    g = -jnp.exp(alog) * _softplus(a + dtb)
    beta = _sigmoid(b)
    ri = _iota2((c, c), 0)
    ci = _iota2((c, c), 1)
    causal = ri >= ci
    strict = ri > ci
    gb = jnp.broadcast_to(g, (nh, c, d))
    gcb = _xdot(jnp.broadcast_to(causal.astype(F32), (nh, c, c)), gb, exact=0)
    gc = gcb[..., :1]
    gl = _colsum(gb)[..., :1]
    dec = jnp.exp(jnp.where(causal, gc - _t(gcb)[:, :c, :], -1e30))
    kb = kn * beta
    amat = jnp.where(strict, _bdot(kb, kn, 1, 1) * dec, 0.0)
    if tinv is None:
        tinv = _inv_unit_lower(amat)
    e = jnp.exp(gc)
    f = jnp.exp(gl - gc)
    rw = kb * e
    sol = _hdot(tinv, jnp.concatenate([v * beta, rw], axis=-1))
    u = sol[..., :d]
    w = sol[..., d:]
    pmat = jnp.where(causal, _bdot(qs, kn, 1, 1) * dec, 0.0)
    qd = qs * e
    kd = kn * f
    vnew = u - _bdot(w, s_in)
    o = _bdot(qd, s_in) + _bdot(pmat, vnew)
    s_out = s_in * jnp.exp(gl) + _bdot(kd, vnew, 0, 0)
    return dict(rq=rq, rk=rk, qh=qh, kn=kn, qs=qs, g=g, beta=beta, causal=causal, strict=strict, gl=gl,
                dec=dec, kb=kb, amat=amat, tinv=tinv, e=e, f=f, rw=rw, u=u, w=w, pmat=pmat, qd=qd, kd=kd,
                vnew=vnew, o=o, s_out=s_out)


def _dn_chunk_bwd(m, q, v, a, alog, dtb, s_in, do, ds_out):
    nh, c, d = q.shape
    kn, qs, kb, u, w, e, f = m["kn"], m["qs"], m["kb"], m["u"], m["w"], m["e"], m["f"]
    beta, dec, tinv, vnew, kd, qd = m["beta"], m["dec"], m["tinv"], m["vnew"], m["kd"], m["qd"]
    el = jnp.exp(m["gl"])
    dvnew = _bdot(m["pmat"], do, 0, 0) + _bdot(kd, ds_out)
    dp = jnp.where(m["causal"], _bdot(do, vnew, 1, 1), 0.0)
    dqd = _bdot(do, s_in, 1, 1)
    dkd = _bdot(vnew, ds_out, 1, 1)
    ds_in = _bdot(qd, do, 0, 0) + el * ds_out - _bdot(w, dvnew, 0, 0)
    dgl = el * _colsum(_rowsum(s_in * ds_out))
    dw = -_bdot(dvnew, s_in, 1, 1)
    dsol = _hdot(tinv, jnp.concatenate([dvnew, dw], axis=-1), 0, 0)
    dru = dsol[..., :d]
    drw = dsol[..., d:]
    da_m = -jnp.where(m["strict"], _bdot(dsol, jnp.concatenate([u, w], axis=-1), 1, 1), 0.0)
    db_m = da_m * dec
    dq_m = dp * dec
    dkb = _bdot(db_m, kn)
    dkn = _bdot(db_m, kb, 0, 0) + _bdot(dq_m, qs, 0, 0)
    dqs = _bdot(dq_m, kn)
    gmat = da_m * m["amat"] + dp * m["pmat"]
    ones = jnp.ones((nh, c, d), F32)
    dgam = (_xdot(gmat, ones) - _xdot(gmat, ones, 0, 0))[..., :1]
    dqs = dqs + dqd * e
    dgam = dgam + _rowsum(dqd * qd)
    dkn = dkn + dkd * f
    tk = _rowsum(dkd * kd)
    dgam = dgam - tk
    dgl = dgl + _colsum(tk)
    dkb = dkb + drw * e
    dgam = dgam + _rowsum(drw * m["rw"])
    dv = dru * beta
    dbeta = _rowsum(dru * v) + _rowsum(dkb * kn)
    dkn = dkn + dkb * beta
    last = (_iota2((c, 1), 0) == c - 1).astype(F32)
    dgam = dgam + last * dgl
    upper = (_iota2((c, c), 0) <= _iota2((c, c), 1)).astype(F32)
    dg = _xdot(jnp.broadcast_to(upper, (nh, c, c)), jnp.broadcast_to(dgam, (nh, c, d)), exact=0)[..., :1]
    dqh = dqs * (d ** -0.5)
    dq = m["rq"] * (dqh - m["qh"] * _rowsum(dqh * m["qh"]))
    dk = m["rk"] * (dkn - kn * _rowsum(dkn * kn))
    sg = _sigmoid(a + dtb)
    da = dg * (-jnp.exp(alog)) * sg
    dalog = _colsum(dg * m["g"])
    ddtb = _colsum(da)
    db = dbeta * beta * (1.0 - beta)
    return dq, dk, dv, da, db, dalog, ddtb, ds_in


def _dn_gate(o, z, wn):
    ro = lax.rsqrt(jnp.mean(o * o, axis=-1, keepdims=True) + NORM_EPS)
    n = o * ro
    return n, ro, n * wn * _silu(z)


def _heads(ref, col0):
    d = DN_HEAD_DIM
    return jnp.stack([ref[:, col0 + h * d:col0 + (h + 1) * d] for h in range(DN_HEADS)])


def _dn_inputs(act_ref, ab_ref, sc_ref):
    ab = ab_ref[...]
    sc = sc_ref[...]
    q = _heads(act_ref, 0)
    k = _heads(act_ref, DN_WIDTH)
    v = _heads(act_ref, 2 * DN_WIDTH)
    a = jnp.stack([ab[:, h:h + 1] for h in range(DN_HEADS)])
    b = jnp.stack([ab[:, DN_HEADS + h:DN_HEADS + h + 1] for h in range(DN_HEADS)])
    alog = jnp.stack([sc[0:1, h:h + 1] for h in range(DN_HEADS)])
    dtb = jnp.stack([sc[1:2, h:h + 1] for h in range(DN_HEADS)])
    return q, k, v, a, b, alog, dtb


def dn_fwd(act, proj, scal, wn):
    t = act.shape[0]
    n = t // DN_CHUNK
    d = DN_HEAD_DIM

    def body(act_ref, z_ref, ab_ref, sc_ref, wn_ref, y_ref, st_ref, ti_ref, s_ref):
        @pl.when(pl.program_id(0) == 0)
        def _():
            s_ref[...] = jnp.zeros_like(s_ref)

        q, k, v, a, b, alog, dtb = _dn_inputs(act_ref, ab_ref, sc_ref)
        s_in = s_ref[...]
        st_ref[0] = s_in
        m = _dn_chunk(q, k, v, a, b, alog, dtb, s_in)
        ti_ref[0] = m["tinv"]
        s_ref[...] = m["s_out"]
        y = _dn_gate(m["o"], _heads(z_ref, 0), wn_ref[...])[2]
        for h in range(DN_HEADS):
            y_ref[:, h * d:(h + 1) * d] = y[h]

    return _call(
        body, name="dn_fwd", grid=(n,),
        in_specs=[pl.BlockSpec((DN_CHUNK, 3 * DN_WIDTH), lambda i: (i, 0)),
                  pl.BlockSpec((DN_CHUNK, DN_WIDTH), lambda i: (i, C_Z // DN_WIDTH)),
                  pl.BlockSpec((DN_CHUNK, LANE), lambda i: (i, C_AB // LANE)),
                  _full((8, LANE)), _full((1, d))],
        out_specs=[pl.BlockSpec((DN_CHUNK, DN_WIDTH), lambda i: (i, 0)),
                   pl.BlockSpec((1, DN_HEADS, d, d), lambda i: (i, 0, 0, 0)),
                   pl.BlockSpec((1, DN_HEADS, DN_CHUNK, DN_CHUNK), lambda i: (i, 0, 0, 0))],
        out_shape=[jax.ShapeDtypeStruct((t, MIX_WIDTH), F32), jax.ShapeDtypeStruct((n, DN_HEADS, d, d), F32),
                   jax.ShapeDtypeStruct((n, DN_HEADS, DN_CHUNK, DN_CHUNK), F32)],
        scratch_shapes=[pltpu.VMEM((DN_HEADS, d, d), F32)],
    )(act, proj, proj, scal, wn)


def dn_bwd(act, proj, scal, wn, states, tinvs, dy):
    t = act.shape[0]
    n = t // DN_CHUNK
    d = DN_HEAD_DIM
    zab = DN_WIDTH + AB_PAD

    def body(act_ref, z_ref, ab_ref, sc_ref, wn_ref, st_ref, ti_ref, dy_ref, dact_ref, dzab_ref, dpar_ref, ds_ref):
        @pl.when(pl.program_id(0) == 0)
        def _():
            ds_ref[...] = jnp.zeros_like(ds_ref)
            dpar_ref[...] = jnp.zeros_like(dpar_ref)

        wnv = wn_ref[...]
        q, k, v, a, b, alog, dtb = _dn_inputs(act_ref, ab_ref, sc_ref)
        s_in = st_ref[0]
        z = _heads(z_ref, 0)
        dyh = _heads(dy_ref, 0)
        m = _dn_chunk(q, k, v, a, b, alog, dtb, s_in, ti_ref[0])
        nrm, ro, _ = _dn_gate(m["o"], z, wnv)
        sz = _silu(z)
        dz = dyh * nrm * wnv * _dsilu(z)
        dn = dyh * wnv * sz
        dwn = _colsum(dyh * nrm * sz)
        do = ro * (dn - nrm * jnp.mean(dn * nrm, axis=-1, keepdims=True))
        dq, dk, dv, da, db, dalog, ddtb, ds_in = _dn_chunk_bwd(m, q, v, a, alog, dtb, s_in, do, ds_ref[...])
        ds_ref[...] = ds_in
        lane = _iota2((DN_CHUNK, LANE), 1)
        prow = _iota2((8, LANE), 0)
        plane = _iota2((8, LANE), 1)
        dab = jnp.zeros((DN_CHUNK, LANE), F32)
        dpar = jnp.zeros((8, LANE), F32)
        for h in range(DN_HEADS):
            dzab_ref[:, h * d:(h + 1) * d] = dz[h].astype(BF16)
            dact_ref[:, h * d:(h + 1) * d] = dq[h]
            dact_ref[:, DN_WIDTH + h * d:DN_WIDTH + (h + 1) * d] = dk[h]
            dact_ref[:, 2 * DN_WIDTH + h * d:2 * DN_WIDTH + (h + 1) * d] = dv[h]
            dab = dab + jnp.where(lane == h, da[h], 0.0) + jnp.where(lane == DN_HEADS + h, db[h], 0.0)
            dpar = dpar + jnp.where((prow == 0) & (plane == h), dalog[h], 0.0)
            dpar = dpar + jnp.where((prow == 1) & (plane == h), ddtb[h], 0.0)
            dpar = dpar + jnp.where(prow == 2, dwn[h], 0.0)
        dzab_ref[:, DN_WIDTH:DN_WIDTH + LANE] = dab.astype(BF16)
        dzab_ref[:, DN_WIDTH + LANE:] = jnp.zeros((DN_CHUNK, AB_PAD - LANE), BF16)
        dpar_ref[...] += dpar

    rev = lambda i: (n - 1 - i, 0)
    return _call(
        body, name="dn_bwd", grid=(n,),
        in_specs=[pl.BlockSpec((DN_CHUNK, 3 * DN_WIDTH), rev),
                  pl.BlockSpec((DN_CHUNK, DN_WIDTH), lambda i: (n - 1 - i, C_Z // DN_WIDTH)),
                  pl.BlockSpec((DN_CHUNK, LANE), lambda i: (n - 1 - i, C_AB // LANE)),
                  _full((8, LANE)), _full((1, d)),
                  pl.BlockSpec((1, DN_HEADS, d, d), lambda i: (n - 1 - i, 0, 0, 0)),
                  pl.BlockSpec((1, DN_HEADS, DN_CHUNK, DN_CHUNK), lambda i: (n - 1 - i, 0, 0, 0)),
                  pl.BlockSpec((DN_CHUNK, DN_WIDTH), rev)],
        out_specs=[pl.BlockSpec((DN_CHUNK, 3 * DN_WIDTH), rev),
                   pl.BlockSpec((DN_CHUNK, zab), lambda i: (n - 1 - i, C_Z // zab)), _full((8, LANE))],
        out_shape=[jax.ShapeDtypeStruct((t, 3 * DN_WIDTH), F32), jax.ShapeDtypeStruct((t, IN_PAD), BF16),
                   jax.ShapeDtypeStruct((8, LANE), F32)],
        scratch_shapes=[pltpu.VMEM((DN_HEADS, d, d), F32)],
    )(act, proj, proj, scal, wn, states, tinvs, dy)


_INV_SQRT2 = 0.7071067811865476
_INV_SQRT2PI = 0.3989422804014327


def _gelu(x):
    return 0.5 * x * (1.0 + lax.erf(x * _INV_SQRT2))


def _dgelu(x):
    return 0.5 * (1.0 + lax.erf(x * _INV_SQRT2)) + x * jnp.exp(-0.5 * x * x) * _INV_SQRT2PI


def _gm_core(uv, lng, lnb, ws_ref, bst):
    c = uv.shape[0]
    zz = _gelu(uv)
    u = zz[:, :GM_WIDTH]
    vv = zz[:, GM_WIDTH:]
    xc = vv - jnp.mean(vv, axis=-1, keepdims=True)
    rs = lax.rsqrt(jnp.mean(xc * xc, axis=-1, keepdims=True) + NORM_EPS)
    xh = xc * rs
    vn = xh * lng + lnb
    grp = _iota2((c, GM_WIDTH), 1) // GM_GROUP_DIM
    tril = _iota2((c, c), 0) >= _iota2((c, c), 1)
    sv = jnp.zeros((c, GM_WIDTH), F32)
    masks = []
    for g in range(GM_GROUPS):
        mk = grp == g
        masks.append(mk)
        ws = jnp.where(tril, ws_ref[g], 0.0)
        sv = sv + _bdot(ws, jnp.where(mk, vn, 0.0)) + jnp.where(mk, bst[:, g:g + 1], 0.0)
    return u, xh, rs, vn, sv, masks, tril


def gm_fwd(proj, lng, lnb, w_s, bst, ybuf):
    t = proj.shape[0]

    def body(uv_ref, g_ref, b_ref, ws_ref, bst_ref, _, y_ref):
        u, _, _, _, sv, _, _ = _gm_core(uv_ref[...], g_ref[...], b_ref[...], ws_ref, bst_ref[...])
        y_ref[...] = u * sv

    return _call(
        body, name="gm_fwd", grid=(t // GM_CHUNK,),
        in_specs=[pl.BlockSpec((GM_CHUNK, 2 * GM_WIDTH), lambda i: (i, C_UV // (2 * GM_WIDTH))),
                  _full((1, GM_WIDTH)), _full((1, GM_WIDTH)), _full((GM_GROUPS, GM_CHUNK, GM_CHUNK)),
                  _full((GM_CHUNK, LANE)), ANY],
        out_specs=pl.BlockSpec((GM_CHUNK, GM_WIDTH), lambda i: (i, DN_WIDTH // GM_WIDTH)),
        out_shape=jax.ShapeDtypeStruct(ybuf.shape, F32), semantics=("parallel",), aliases={5: 0},
    )(proj, lng, lnb, w_s, bst, ybuf)


def gm_bwd(proj, lng, lnb, w_s, bst, dy, dproj):
    t = proj.shape[0]

    def body(uv_ref, g_ref, b_ref, ws_ref, bst_ref, dy_ref, _, duv_ref, dws_ref, dbst_ref, dln_ref):
        @pl.when(pl.program_id(0) == 0)
        def _():
            dws_ref[...] = jnp.zeros_like(dws_ref)
            dbst_ref[...] = jnp.zeros_like(dbst_ref)
            dln_ref[...] = jnp.zeros_like(dln_ref)

        uv = uv_ref[...]
        lng = g_ref[...]
        u, xh, rs, vn, sv, masks, tril = _gm_core(uv, lng, b_ref[...], ws_ref, bst_ref[...])
        dyv = dy_ref[...]
        dsv = dyv * u
        lane = _iota2((GM_CHUNK, LANE), 1)
        dvn = jnp.zeros_like(dsv)
        dbst = jnp.zeros((GM_CHUNK, LANE), F32)
        for g in range(GM_GROUPS):
            ws = jnp.where(tril, ws_ref[g], 0.0)
            dsg = jnp.where(masks[g], dsv, 0.0)
            dvn = dvn + jnp.where(masks[g], _bdot(ws, dsv, 0, 0), 0.0)
            dws_ref[g] += jnp.where(tril, _bdot(dsg, vn, 1, 1), 0.0)
            dbst = dbst + jnp.where(lane == g, _rowsum(dsg), 0.0)
        dbst_ref[...] += dbst
        row = _iota2((8, GM_WIDTH), 0)
        dln_ref[...] += jnp.where(row == 0, _colsum(dvn * xh), 0.0) + jnp.where(row == 1, _colsum(dvn), 0.0)
        dxh = dvn * lng
        dvv = rs * (dxh - jnp.mean(dxh, axis=-1, keepdims=True) - xh * jnp.mean(dxh * xh, axis=-1, keepdims=True))
        dg = _dgelu(uv)
        duv_ref[:, :GM_WIDTH] = (dyv * sv * dg[:, :GM_WIDTH]).astype(BF16)
        duv_ref[:, GM_WIDTH:] = (dvv * dg[:, GM_WIDTH:]).astype(BF16)

    return _call(
        body, name="gm_bwd", grid=(t // GM_CHUNK,),
        in_specs=[pl.BlockSpec((GM_CHUNK, 2 * GM_WIDTH), lambda i: (i, C_UV // (2 * GM_WIDTH))),
                  _full((1, GM_WIDTH)), _full((1, GM_WIDTH)), _full((GM_GROUPS, GM_CHUNK, GM_CHUNK)),
                  _full((GM_CHUNK, LANE)),
                  pl.BlockSpec((GM_CHUNK, GM_WIDTH), lambda i: (i, DN_WIDTH // GM_WIDTH)), ANY],
        out_specs=[pl.BlockSpec((GM_CHUNK, 2 * GM_WIDTH), lambda i: (i, C_UV // (2 * GM_WIDTH))),
                   _full((GM_GROUPS, GM_CHUNK, GM_CHUNK)), _full((GM_CHUNK, LANE)), _full((8, GM_WIDTH))],
        out_shape=[jax.ShapeDtypeStruct(dproj.shape, dproj.dtype),
                   jax.ShapeDtypeStruct((GM_GROUPS, GM_CHUNK, GM_CHUNK), F32),
                   jax.ShapeDtypeStruct((GM_CHUNK, LANE), F32), jax.ShapeDtypeStruct((8, GM_WIDTH), F32)],
        aliases={6: 0},
    )(proj, lng, lnb, w_s, bst, dy, dproj)


def _head_mats():
    r = _iota2((SW_WIDTH, SW_WIDTH), 0)
    c = _iota2((SW_WIDTH, SW_WIDTH), 1)
    same = (r // SW_HEAD_DIM) == (c // SW_HEAD_DIM)
    cc = c % SW_HEAD_DIM
    half = ROPE_DIM // 2
    rot = jnp.where((cc < half) & (r == c + half), -1.0, 0.0) + jnp.where((cc >= half) & (cc < ROPE_DIM) & (r == c - half), 1.0, 0.0)
    return same.astype(F32), rot


def _seg_col(s):
    return C_SW // SW_WIDTH + (s // 2) * 3 + s % 2


def _halves(x):
    return x[:, :LANE], x[:, LANE:]


def sw_prep(proj, nw2, cos_t, sin_t, *, tm=512):
    t = proj.shape[0]

    def body(x_ref, w_ref, c_ref, s_ref, o_ref):
        same, rot = _head_mats()
        x = x_ref[...]
        r = lax.rsqrt(_xdot(x * x, same) * (1.0 / SW_HEAD_DIM) + NORM_EPS)
        xn = x * r * w_ref[0]
        o_ref[0, 0], o_ref[0, 1] = _halves(xn * c_ref[...] + _xdot(xn, rot) * s_ref[...])

    return _call(
        body, name="sw_prep", grid=(6, t // tm),
        in_specs=[pl.BlockSpec((tm, SW_WIDTH), lambda s, i: (i, _seg_col(s))),
                  pl.BlockSpec((1, 1, SW_WIDTH), lambda s, i: (s % 2, 0, 0)),
                  pl.BlockSpec((tm, SW_WIDTH), lambda s, i: (i, 0)),
                  pl.BlockSpec((tm, SW_WIDTH), lambda s, i: (i, 0))],
        out_specs=pl.BlockSpec((1, 2, tm, LANE), lambda s, i: (s, 0, i, 0)),
        out_shape=jax.ShapeDtypeStruct((6, 2, t, LANE), F32), semantics=("parallel", "parallel"),
    )(proj, nw2, cos_t, sin_t)


def sw_prep_bwd(proj, nw2, cos_t, sin_t, dkvq, dproj, dnw, p, *, tm=512):
    t = proj.shape[0]
    col0 = C_SW // SW_WIDTH + 3 * p
    seg_col = lambda s: col0 + (s + 1) % 3

    def body(x_ref, w_ref, c_ref, s_ref, d_ref, _, dw0_ref, dx_ref, dw_ref):
        s = pl.program_id(0)
        dout = jnp.concatenate([d_ref[0, 0], d_ref[0, 1]], axis=1)

        @pl.when(s == 1)
        def _():
            dx_ref[...] = dout.astype(BF16)

        @pl.when((s != 1) & (pl.program_id(1) == 0))
        def _():
            dw_ref[...] = dw0_ref[...]

        @pl.when(s != 1)
        def _():
            same, rot = _head_mats()
            x = x_ref[...]
            w = w_ref[0]
            r = lax.rsqrt(_xdot(x * x, same) * (1.0 / SW_HEAD_DIM) + NORM_EPS)
            xh = x * r
            dxn = dout * c_ref[...] + _xdot(dout * s_ref[...], rot, 1, 1)
            dw_ref[0] += _colsum(dxn * xh)
            dxh = dxn * w
            dx_ref[...] = (r * (dxh - xh * (_xdot(dxh * xh, same) * (1.0 / SW_HEAD_DIM)))).astype(BF16)

    return _call(
        body, name=f"sw_prep_bwd{p}", grid=(3, t // tm),
        in_specs=[pl.BlockSpec((tm, SW_WIDTH), lambda s, i: (i, seg_col(s))),
                  pl.BlockSpec((1, 1, SW_WIDTH), lambda s, i: (1 - s // 2, 0, 0)),
                  pl.BlockSpec((tm, SW_WIDTH), lambda s, i: (i, 0)),
                  pl.BlockSpec((tm, SW_WIDTH), lambda s, i: (i, 0)),
                  pl.BlockSpec((1, 2, tm, LANE), lambda s, i: (s, 0, i, 0)), ANY,
                  pl.BlockSpec((1, 1, SW_WIDTH), lambda s, i: (s // 2, 0, 0))],
        out_specs=[pl.BlockSpec((tm, SW_WIDTH), lambda s, i: (i, seg_col(s))),
                   pl.BlockSpec((1, 1, SW_WIDTH), lambda s, i: (s // 2, 0, 0))],
        out_shape=[jax.ShapeDtypeStruct(dproj.shape, dproj.dtype), jax.ShapeDtypeStruct((2, 1, SW_WIDTH), F32)],
        semantics=("arbitrary", "arbitrary"), aliases={5: 0},
    )(proj, nw2, cos_t, sin_t, dkvq, dproj, dnw)


_SW_SCALE = SW_HEAD_DIM ** -0.5
_NEG = -1e30


def _sw_masks(has_other):
    ri = _iota2((SW_BLOCK, SW_BLOCK), 0)
    ci = _iota2((SW_BLOCK, SW_BLOCK), 1)
    return ri >= ci, (ci >= ri) & has_other


def _pair(x):
    first = _iota2((1, LANE), 1) < SW_HEAD_DIM
    return jnp.stack([jnp.where(first, x, 0.0), jnp.where(first, 0.0, x)])


def _both(x):
    return jnp.broadcast_to(x.astype(BF16)[None], (2,) + x.shape)


def _unpair(x2):
    first = _iota2((1, LANE), 1) < SW_HEAD_DIM
    return jnp.where(first, x2[0], x2[1])


def _head_cols(x):
    return jnp.stack([x[:, 0:1], x[:, SW_HEAD_DIM:SW_HEAD_DIM + 1]])


SW_GROUP = 4


def _sw_geometry(t, p):
    dil = SW_DILATIONS[p]
    unit = SW_BLOCK * dil
    nb = max(1, SW_GROUP // dil)
    return dil, unit, nb, t // (unit * nb)


def _sw_groups(dil, nb, body):
    if nb * dil == SW_GROUP:
        body([(k // dil, k % dil) for k in range(SW_GROUP)])
    else:
        def step(g, carry):
            body([(0, SW_GROUP * g + k) for k in range(SW_GROUP)])
            return carry

        lax.fori_loop(0, nb * dil // SW_GROUP, step, 0)


def _sw_rows(i, r, dil):
    start = i * SW_BLOCK * dil + r
    return pl.ds(start, SW_BLOCK) if dil == 1 else pl.ds(start, SW_BLOCK, stride=dil)


def _sw_load(refs, probs, dil, shift, wrap, fn):
    out = []
    for i, r in probs:
        if shift != 0 and i == wrap:
            out.append(fn(refs[1][_sw_rows(0, r, dil), :]))
        else:
            out.append(fn(refs[0][_sw_rows(i + shift, r, dil), :]))
    return jnp.concatenate(out, axis=0)


def _sw_other_masks(probs, wrap, edge_ok):
    _, other = _sw_masks(edge_ok)
    _, always = _sw_masks(True)
    return jnp.stack([other if i == wrap else always for i, _ in probs for _ in range(2)])


def sw_attn(qk, proj, p):
    t = proj.shape[0]
    dil, unit, nb, nsp = _sw_geometry(t, p)
    vcol = (C_SW + 3 * SW_WIDTH * p + 2 * SW_WIDTH) // LANE

    def body(q_ref, kc_ref, kp_ref, vc_ref, vp_ref, o_ref, l_ref):
        mc, _ = _sw_masks(True)
        first = pl.program_id(1) != 0
        q_r, k_r, v_r = (q_ref.at[0, 0], None), (kc_ref.at[0, 0], kp_ref.at[0, 0]), (vc_ref, vp_ref)

        def one(probs):
            mp = _sw_other_masks(probs, 0, first)
            q2 = _sw_load(q_r, probs, dil, 0, 0, _pair)
            sc = jnp.where(mc, _bdot(q2, _sw_load(k_r, probs, dil, 0, 0, _both), 1, 1) * _SW_SCALE, _NEG)
            sp = jnp.where(mp, _bdot(q2, _sw_load(k_r, probs, dil, -1, 0, _both), 1, 1) * _SW_SCALE, _NEG)
            mx = jnp.maximum(jnp.max(sc, axis=-1, keepdims=True), jnp.max(sp, axis=-1, keepdims=True))
            pc = jnp.exp(sc - mx)
            pp = jnp.exp(sp - mx)
            den = _rowsum(pc) + _rowsum(pp)
            o2 = (_bdot(pc, _sw_load(v_r, probs, dil, 0, 0, _both))
                  + _bdot(pp, _sw_load(v_r, probs, dil, -1, 0, _both))) * (1.0 / den)
            l2 = jnp.broadcast_to(mx + jnp.log(den), o2.shape)
            for n, (i, r) in enumerate(probs):
                o_ref.at[0][_sw_rows(i, r, dil), :] = _unpair(o2[2 * n:2 * n + 2])
                l_ref.at[0][_sw_rows(i, r, dil), :] = _unpair(l2[2 * n:2 * n + 2])

        _sw_groups(dil, nb, one)

    before = lambda j: jnp.maximum(j * nb - 1, 0)
    seg = lambda s: pl.BlockSpec((1, 1, unit * nb, LANE), lambda h, j: (s, h, j, 0))
    seg_b = lambda s: pl.BlockSpec((1, 1, unit, LANE), lambda h, j: (s, h, before(j), 0))
    out = pl.BlockSpec((1, unit * nb, LANE), lambda h, j: (h, j, 0))
    shp = jax.ShapeDtypeStruct((2, t, LANE), F32)
    return _call(
        body, name=f"sw_attn{p}", grid=(2, nsp),
        in_specs=[seg(2 * p), seg(2 * p + 1), seg_b(2 * p + 1),
                  pl.BlockSpec((unit * nb, LANE), lambda h, j: (j, vcol + h)),
                  pl.BlockSpec((unit, LANE), lambda h, j: (before(j), vcol + h))],
        out_specs=[out, out], out_shape=[shp, shp], semantics=("parallel", "parallel"),
    )(qk, qk, qk, proj, proj)


def sw_attn_dkv(qk, proj, dy, lg, dm, p):
    t = proj.shape[0]
    dil, unit, nb, nsp = _sw_geometry(t, p)
    nunits = t // unit
    vcol = (C_SW + 3 * SW_WIDTH * p + 2 * SW_WIDTH) // LANE
    ycol = (DN_WIDTH + GM_WIDTH) // LANE

    def body(k_ref, v_ref, qc_ref, qn_ref, doc_ref, don_ref, lc_ref, ln_ref, dc_ref, dn_ref, o_ref):
        mc, _ = _sw_masks(True)
        more = pl.program_id(1) + 1 < nsp
        q_r, do_r = (qc_ref.at[0, 0], qn_ref.at[0, 0]), (doc_ref, don_ref)
        l_r, d_r = (lc_ref.at[0], ln_ref.at[0]), (dc_ref.at[0], dn_ref.at[0])

        def one(probs):
            k2 = _sw_load((k_ref.at[0, 0], None), probs, dil, 0, 0, _both)
            v2 = _sw_load((v_ref, None), probs, dil, 0, 0, _both)
            dk = jnp.zeros((2 * SW_GROUP, SW_BLOCK, LANE), F32)
            dv = jnp.zeros((2 * SW_GROUP, SW_BLOCK, LANE), F32)
            for shift, mk in ((0, mc), (1, _sw_other_masks(probs, nb - 1, more))):
                q2 = _sw_load(q_r, probs, dil, shift, nb - 1, _pair)
                do2 = _sw_load(do_r, probs, dil, shift, nb - 1, _pair)
                lse = _sw_load(l_r, probs, dil, shift, nb - 1, _head_cols)
                dd = _sw_load(d_r, probs, dil, shift, nb - 1, _head_cols)
                pr = jnp.exp(jnp.where(mk, _bdot(q2, k2, 1, 1) * _SW_SCALE, _NEG) - lse)
                dv = dv + _bdot(pr, do2, 0, 0)
                ds = pr * (_bdot(do2, v2, 1, 1) - dd)
                dk = dk + _bdot(ds, q2, 0, 0)
            for n, (i, r) in enumerate(probs):
                o_ref.at[0, 0][_sw_rows(i, r, dil), :] = (dk[2 * n] + dk[2 * n + 1]) * _SW_SCALE
                o_ref.at[1, 0][_sw_rows(i, r, dil), :] = dv[2 * n] + dv[2 * n + 1]

        _sw_groups(dil, nb, one)

    after = lambda j: jnp.minimum((j + 1) * nb, nunits - 1)
    seg = lambda s: pl.BlockSpec((1, 1, unit * nb, LANE), lambda h, j: (s, h, j, 0))
    seg_a = lambda s: pl.BlockSpec((1, 1, unit, LANE), lambda h, j: (s, h, after(j), 0))
    col = lambda c0: pl.BlockSpec((unit * nb, LANE), lambda h, j: (j, c0 + h))
    col_a = lambda c0: pl.BlockSpec((unit, LANE), lambda h, j: (after(j), c0 + h))
    hp = pl.BlockSpec((1, unit * nb, LANE), lambda h, j: (h, j, 0))
    hp_a = pl.BlockSpec((1, unit, LANE), lambda h, j: (h, after(j), 0))
    return _call(
        body, name=f"sw_dkv{p}", grid=(2, nsp),
        in_specs=[seg(2 * p + 1), col(vcol), seg(2 * p), seg_a(2 * p), col(ycol), col_a(ycol), hp, hp_a, hp, hp_a],
        out_specs=pl.BlockSpec((2, 1, unit * nb, LANE), lambda h, j: (0, h, j, 0)),
        out_shape=jax.ShapeDtypeStruct((3, 2, t, LANE), F32), semantics=("parallel", "parallel"),
    )(qk, proj, qk, qk, dy, dy, lg, lg, dm, dm)


def sw_attn_dq(qk, proj, dy, lg, dm, dkvq, p):
    t = proj.shape[0]
    dil, unit, nb, nsp = _sw_geometry(t, p)
    vcol = (C_SW + 3 * SW_WIDTH * p + 2 * SW_WIDTH) // LANE
    ycol = (DN_WIDTH + GM_WIDTH) // LANE

    def body(q_ref, kc_ref, kp_ref, vc_ref, vp_ref, do_ref, l_ref, d_ref, _, dq_ref):
        mc, _ = _sw_masks(True)
        first = pl.program_id(1) != 0
        k_r, v_r = (kc_ref.at[0, 0], kp_ref.at[0, 0]), (vc_ref, vp_ref)

        def one(probs):
            mp = _sw_other_masks(probs, 0, first)
            q2 = _sw_load((q_ref.at[0, 0], None), probs, dil, 0, 0, _pair)
            do2 = _sw_load((do_ref, None), probs, dil, 0, 0, _pair)
            lse = _sw_load((l_ref.at[0], None), probs, dil, 0, 0, _head_cols)
            dd = _sw_load((d_ref.at[0], None), probs, dil, 0, 0, _head_cols)
            kc = _sw_load(k_r, probs, dil, 0, 0, _both)
            kp = _sw_load(k_r, probs, dil, -1, 0, _both)
            pc = jnp.exp(jnp.where(mc, _bdot(q2, kc, 1, 1) * _SW_SCALE, _NEG) - lse)
            pp = jnp.exp(jnp.where(mp, _bdot(q2, kp, 1, 1) * _SW_SCALE, _NEG) - lse)
            dsc = pc * (_bdot(do2, _sw_load(v_r, probs, dil, 0, 0, _both), 1, 1) - dd)
            dsp = pp * (_bdot(do2, _sw_load(v_r, probs, dil, -1, 0, _both), 1, 1) - dd)
            dq2 = (_bdot(dsc, kc) + _bdot(dsp, kp)) * _SW_SCALE
            for n, (i, r) in enumerate(probs):
                dq_ref.at[0, 0][_sw_rows(i, r, dil), :] = _unpair(dq2[2 * n:2 * n + 2])

        _sw_groups(dil, nb, one)

    before = lambda j: jnp.maximum(j * nb - 1, 0)
    seg = lambda s: pl.BlockSpec((1, 1, unit * nb, LANE), lambda h, j: (s, h, j, 0))
    seg_b = lambda s: pl.BlockSpec((1, 1, unit, LANE), lambda h, j: (s, h, before(j), 0))
    col = lambda c0: pl.BlockSpec((unit * nb, LANE), lambda h, j: (j, c0 + h))
    col_b = lambda c0: pl.BlockSpec((unit, LANE), lambda h, j: (before(j), c0 + h))
    hp = pl.BlockSpec((1, unit * nb, LANE), lambda h, j: (h, j, 0))
    return _call(
        body, name=f"sw_dq{p}", grid=(2, nsp),
        in_specs=[seg(2 * p), seg(2 * p + 1), seg_b(2 * p + 1), col(vcol), col_b(vcol), col(ycol), hp, hp, ANY],
        out_specs=pl.BlockSpec((1, 1, unit * nb, LANE), lambda h, j: (2, h, j, 0)),
        out_shape=jax.ShapeDtypeStruct(dkvq.shape, F32), semantics=("parallel", "parallel"), aliases={8: 0},
    )(qk, qk, qk, proj, proj, dy, lg, dm, dkvq)


def sw_merge(outs, lses, ybuf, *, tm=512):
    t = ybuf.shape[0]

    def body(o0, o1, o2, l0_ref, l1_ref, l2_ref, _, y_ref, lg_ref):
        l0, l1, l2 = l0_ref[...], l1_ref[...], l2_ref[...]
        mx = jnp.maximum(jnp.maximum(l0, l1), l2)
        lg = mx + jnp.log(jnp.exp(l0 - mx) + jnp.exp(l1 - mx) + jnp.exp(l2 - mx))
        lg_ref[...] = lg
        y = jnp.exp(l0 - lg) * o0[...] + jnp.exp(l1 - lg) * o1[...] + jnp.exp(l2 - lg) * o2[...]
        y_ref[...] = jnp.concatenate([y[0], y[1]], axis=1)

    hp = pl.BlockSpec((2, tm, LANE), lambda i: (0, i, 0))
    return _call(
        body, name="sw_merge", grid=(t // tm,), in_specs=[hp] * 6 + [ANY],
        out_specs=[pl.BlockSpec((tm, SW_WIDTH), lambda i: (i, (DN_WIDTH + GM_WIDTH) // SW_WIDTH)), hp],
        out_shape=[jax.ShapeDtypeStruct(ybuf.shape, F32), jax.ShapeDtypeStruct((2, t, LANE), F32)],
        semantics=("parallel",), aliases={6: 0},
    )(*outs, *lses, ybuf)


def sw_delta(dy, ybuf, *, tm=512):
    t = ybuf.shape[0]

    def body(dy_ref, y_ref, o_ref):
        same, _ = _head_mats()
        o_ref[0], o_ref[1] = _halves(_xdot(dy_ref[...] * y_ref[...], same))

    b1 = pl.BlockSpec((tm, SW_WIDTH), lambda i: (i, (DN_WIDTH + GM_WIDTH) // SW_WIDTH))
    return _call(body, name="sw_delta", grid=(t // tm,), in_specs=[b1, b1],
                 out_specs=pl.BlockSpec((2, tm, LANE), lambda i: (0, i, 0)),
                 out_shape=jax.ShapeDtypeStruct((2, t, LANE), F32), semantics=("parallel",))(dy, ybuf)


def _rope_tables(t):
    inv = ROPE_THETA ** (-jnp.arange(0, ROPE_DIM, 2, dtype=F32) / ROPE_DIM)
    ang = jnp.arange(t, dtype=F32)[:, None] * inv[None, :]
    pad1 = jnp.ones((t, SW_HEAD_DIM - ROPE_DIM), F32)
    pad0 = jnp.zeros((t, SW_HEAD_DIM - ROPE_DIM), F32)
    cos_h = jnp.concatenate([jnp.cos(ang), jnp.cos(ang), pad1], axis=1)
    sin_h = jnp.concatenate([jnp.sin(ang), jnp.sin(ang), pad0], axis=1)
    return jnp.tile(cos_h, (1, SW_HEADS)), jnp.tile(sin_h, (1, SW_HEADS))


def sw_forward(proj, nw2, cos_t, sin_t, ybuf):
    qk = sw_prep(proj, nw2, cos_t, sin_t)
    outs, lses = [], []
    for p in range(len(SW_DILATIONS)):
        o, lse = sw_attn(qk, proj, p)
        outs.append(o)
        lses.append(lse)
    ybuf, lg = sw_merge(outs, lses, ybuf)
    return ybuf, (qk, lg)


def sw_backward(proj, nw2, cos_t, sin_t, res, ybuf, dy, dproj):
    qk, lg = res
    dm = sw_delta(dy, ybuf)
    dnw = jnp.zeros((2, 1, SW_WIDTH), F32)
    for p in range(len(SW_DILATIONS)):
        dkvq = sw_attn_dkv(qk, proj, dy, lg, dm, p)
        dkvq = sw_attn_dq(qk, proj, dy, lg, dm, dkvq, p)
        dproj, dnw = sw_prep_bwd(proj, nw2, cos_t, sin_t, dkvq, dproj, dnw, p)
    return dproj, dnw[::-1, 0]


def _pad_rows(a, rows):
    return jnp.zeros((rows,) + a.shape[1:], a.dtype).at[:a.shape[0]].set(a)


def _consts(sp):
    d = {}
    d["mix_nw"] = sp["mix_norm_w"][:, None, :]
    d["ffn_nw"] = sp["ffn_norm_w"][:, None, :]
    d["cw8"] = jnp.pad(sp["dn_conv_w"], ((0, 0), (0, 8 - DN_CONV), (0, 0)))
    d["scal"] = jnp.pad(jnp.stack([sp["dn_a_log"], sp["dn_dt_bias"]], axis=1), ((0, 0), (0, 6), (0, LANE - DN_HEADS)))
    d["wn"] = sp["dn_out_norm_w"][:, None, :]
    d["lng"] = sp["gm_ln_g"][:, None, :]
    d["lnb"] = sp["gm_ln_b"][:, None, :]
    d["w_s"] = sp["gm_w_s"]
    d["bst"] = jnp.pad(jnp.swapaxes(sp["gm_b_s"], 1, 2), ((0, 0), (0, 0), (0, LANE - GM_GROUPS)))
    d["nw2"] = jnp.stack([jnp.tile(sp["sw_q_norm_w"], (1, SW_HEADS)),
                          jnp.tile(sp["sw_k_norm_w"], (1, SW_HEADS))], axis=1)[:, :, None, :]
    return d


def _layer_fwd(x, mod, get_w, cs, tabs):
    wb = dict(get_w("w_in", x))
    h1, proj = norm_mm(x, cs["mix_nw"], mod[1], mod[0], wb["w_in"], swiglu=False, name="in_proj")
    act = dn_conv(proj, cs["cw8"])
    y, states, tinvs = dn_fwd(act, proj, cs["scal"], cs["wn"])
    y = gm_fwd(proj, cs["lng"], cs["lnb"], cs["w_s"], cs["bst"], y)
    y, swres = sw_forward(proj, cs["nw2"], *tabs, y)
    wb.update(get_w("w_out", y))
    x1, o1 = resid_mm(y, wb["w_out"], x, mod[2], name="out_proj")
    wb.update(get_w("ffn", x1))
    h2, gu, actf = norm_mm(x1, cs["ffn_nw"], mod[4], mod[3], wb["w_ffn_in"], swiglu=True, name="ffn_in")
    x2, o2 = resid_mm(actf, wb["w_ffn_out"], x1, mod[5], name="ffn_out")
    res = dict(x=x, h1=h1, proj=proj, act=act, states=states, tinvs=tinvs, swres=swres, y=y, x1=x1, o1=o1, h2=h2, gu=gu,
               actf=actf, o2=o2)
    return x2, res, wb


def _layer_bwd(dx2, res, mod, wb, cs, tabs, grads_done):
    dgu, gx2, dgate2 = resid_mm_bwd(dx2, mod[5], res["o2"], wb["w_ffn_out"], res["gu"], name="ffn_out_bwd", tm=512)
    g_wfo = mm_tn(res["actf"], gx2, name="wg_ffn_out")
    g_wfi = mm_tn(res["h2"], dgu, name="wg_ffn_in")
    token = grads_done("ffn", dict(w_ffn_in=g_wfi, w_ffn_out=g_wfo))
    dx1, d_ffn_nw, dscale2, dshift2 = norm_mm_bwd(dgu, wb["w_ffn_in"], res["x1"], cs["ffn_nw"], mod[4] + token, dx2,
                                                  name="ffn_in_bwd")
    dy, gx1, dgate1 = resid_mm_bwd(dx1, mod[2], res["o1"], wb["w_out"], None, name="out_proj_bwd", tm=512)
    g_wout = mm_tn(res["y"], gx1, name="wg_out")
    proj = res["proj"]
    dact, dproj, dpar = dn_bwd(res["act"], proj, cs["scal"], cs["wn"], res["states"], res["tinvs"], dy)
    dproj, dcw = dn_conv_bwd(proj, cs["cw8"], dact, dproj)
    dproj, dws, dbst, dln = gm_bwd(proj, cs["lng"], cs["lnb"], cs["w_s"], cs["bst"], dy, dproj)
    dproj, dnw = sw_backward(proj, cs["nw2"], *tabs, res["swres"], res["y"], dy, dproj)
    g_win = mm_tn(res["h1"], dproj, name="wg_in")
    dx, d_mix_nw, dscale1, dshift1 = norm_mm_bwd(dproj, wb["w_in"], res["x"], cs["mix_nw"], mod[1], dx1,
                                                 name="in_proj_bwd")
    dmod = jnp.concatenate([dshift1, dscale1, dgate1, dshift2, dscale2, dgate2], axis=1)
    dnw = dnw.reshape(2, SW_HEADS, SW_HEAD_DIM).sum(1)
    small = dict(mix_norm_w=d_mix_nw[0], ffn_norm_w=d_ffn_nw[0], dn_conv_w=dcw[:DN_CONV],
                 dn_a_log=dpar[0, :DN_HEADS], dn_dt_bias=dpar[1, :DN_HEADS], dn_out_norm_w=dpar[2],
                 gm_ln_g=dln[0], gm_ln_b=dln[1], gm_w_s=dws, gm_b_s=dbst[:, :GM_GROUPS].T,
                 sw_q_norm_w=dnw[0], sw_k_norm_w=dnw[1])
    token = grads_done("mix", dict(w_in=g_win, w_out=g_wout))
    return dx, small, dmod, token


def _permute_w_in(w):
    pad = jnp.zeros(w.shape[:-1] + (AB_PAD - 8,), w.dtype)
    return jnp.concatenate([w[..., 0:2056], pad, w[..., 2568:IN_WIDTH], w[..., 2056:2568]], axis=-1)


def _unpermute_w_in(g):
    return jnp.concatenate([g[..., 0:2056], g[..., C_UV:IN_PAD], g[..., C_SW:C_UV]], axis=-1)


def _local_step(x, target, mods, weights_of, grads_done, sp):
    layers = mods.shape[0]
    t, d = x.shape
    tabs = _rope_tables(t)
    consts = _consts(sp)
    saved = []
    for layer in range(layers):
        mod = mods[layer].reshape(6, 1, d)
        cs = {k: v[layer] for k, v in consts.items()}
        x, res, wb = _layer_fwd(x, mod, functools.partial(weights_of, layer), cs, tabs)
        saved.append((res, mod, wb, cs))
    dx, loss = loss_head(x, target)
    smalls, dmods = [], []
    token = jnp.zeros((1, 1), F32)
    for layer in reversed(range(layers)):
        res, mod, wb, cs = saved[layer]
        dx, small, dmod, token = _layer_bwd(dx, res, mod + token, wb, cs, tabs, functools.partial(grads_done, layer))
        smalls.append(small)
        dmods.append(dmod[0])
    smalls, dmods = smalls[::-1], dmods[::-1]
    small = {k: jnp.stack([s[k] for s in smalls]) for k in smalls[0]}
    return loss, dx, small, jnp.stack(dmods) + token


def mod_fwd(c_all, w_mod, b_shard):
    layers, d, n = w_mod.shape

    def body(c_ref, w_ref, b_ref, o_ref):
        ca = _silu(c_ref[...]).astype(BF16)
        o_ref[0] = _dot(ca, w_ref[0].astype(BF16), 1, 0) + b_ref[0]

    return _call(
        body, name="mod_fwd", grid=(layers,),
        in_specs=[_full((8, d)), pl.BlockSpec((1, d, n), lambda i: (i, 0, 0)),
                  pl.BlockSpec((1, 1, n), lambda i: (i, 0, 0))],
        out_specs=pl.BlockSpec((1, 8, n), lambda i: (i, 0, 0)),
        out_shape=jax.ShapeDtypeStruct((layers, 8, n), F32), semantics=("parallel",),
    )(c_all, w_mod, b_shard)


def mod_bwd(c_all, dmod):
    layers, _, n = dmod.shape
    d = c_all.shape[1]

    def body(c_ref, g_ref, o_ref):
        ca = _silu(c_ref[...]).astype(BF16)
        o_ref[0] = _dot(ca, g_ref[0].astype(BF16), 0, 0)

    return _call(
        body, name="mod_bwd", grid=(layers,),
        in_specs=[_full((8, d)), pl.BlockSpec((1, 8, n), lambda i: (i, 0, 0))],
        out_specs=pl.BlockSpec((1, d, n), lambda i: (i, 0, 0)),
        out_shape=jax.ShapeDtypeStruct((layers, d, n), F32), semantics=("parallel",),
    )(c_all, dmod)


N_DEV = 8


def _place():
    return lax.axis_index("x"), lax.axis_index("y"), lax.axis_index("c")


def _other_chips(x, y):
    return [(1 - x, y), (x, 1 - y), (1 - x, 1 - y)]


def allgather8(x_shard, *, name):
    m_per, n = x_shard.shape

    def body(x_ref, out_ref, send_sems, recv_sems, local_sem):
        x, y, c = _place()
        me, sibling = (x, y, c), (x, y, 1 - c)
        chips = _other_chips(x, y)

        def rows(px, py, pc):
            return out_ref.at[pl.ds((4 * px + 2 * py + pc) * m_per, m_per), :]

        def copy(k, block, to, src=None):
            return pltpu.make_async_remote_copy(
                src_ref=rows(*block) if src is None else src, dst_ref=rows(*block),
                send_sem=send_sems.at[k], recv_sem=recv_sems.at[k], device_id=to, device_id_type=MESH)

        mine = pltpu.make_async_copy(x_ref, rows(*me), local_sem)
        mine.start()
        first = [copy(0, me, sibling, src=x_ref)]
        first += [copy(1 + j, me, (*chip, c), src=x_ref) for j, chip in enumerate(chips)]
        for cp in first:
            cp.start()
        passed = [copy(4 + j, (*chip, c), sibling) for j, chip in enumerate(chips)]
        for j, chip in enumerate(chips):
            copy(1 + j, (*chip, c), me).wait_recv()
            passed[j].start()
        copy(0, sibling, me).wait_recv()
        for j, chip in enumerate(chips):
            copy(4 + j, (*chip, 1 - c), me).wait_recv()
        for cp in first + passed:
            cp.wait_send()
        mine.wait()

    return pl.pallas_call(
        body, name=name, out_shape=jax.ShapeDtypeStruct((N_DEV * m_per, n), x_shard.dtype),
        in_specs=[pl.BlockSpec(memory_space=pltpu.VMEM)], out_specs=pl.BlockSpec(memory_space=pltpu.VMEM),
        scratch_shapes=[pltpu.SemaphoreType.DMA((7,)), pltpu.SemaphoreType.DMA((7,)), pltpu.SemaphoreType.DMA],
    )(x_shard)


HBM = pl.BlockSpec(memory_space=pltpu.HBM)
SEM = pl.BlockSpec(memory_space=pltpu.SEMAPHORE)
_EFFECT = pltpu.SideEffectType.DATAFLOW_SIDE_EFFECTING


def _piece(ref, sliced, chip):
    return ref.at[2 * chip[0] + chip[1]] if sliced else ref


def exchange_start(srcs, after, *, sliced, name):
    n = len(srcs)
    piece = lambda s: s.shape[1:] if sliced else s.shape

    def body(*refs):
        ins, lands = refs[:n], refs[n:2 * n]
        send_sems, recv_sems = refs[2 * n + len(after):2 * n + len(after) + 2]
        token = refs[-1]
        x, y, c = _place()
        me_s = 2 * x + y
        for a in range(n):
            for j, chip in enumerate(_other_chips(x, y)):
                pltpu.make_async_remote_copy(
                    src_ref=_piece(ins[a], sliced, chip), dst_ref=lands[a].at[me_s], send_sem=send_sems.at[3 * a + j],
                    recv_sem=recv_sems.at[3 * a + j], device_id=(*chip, c), device_id_type=MESH).start()
        token[...] = jnp.zeros_like(token)

    zones = [pltpu.with_memory_space_constraint(lax.empty((4,) + piece(s), s.dtype), pltpu.HBM) for s in srcs]
    srcs = [pltpu.with_memory_space_constraint(s, pltpu.HBM) for s in srcs]
    out = pl.pallas_call(
        body, name=name,
        out_shape=(pltpu.SemaphoreType.DMA((3 * n,)), pltpu.SemaphoreType.DMA((3 * n,)),
                   *[pltpu.HBM(s.shape, s.dtype) for s in srcs], *[pltpu.HBM(z.shape, z.dtype) for z in zones],
                   jax.ShapeDtypeStruct((8, LANE), F32)),
        in_specs=[HBM] * (2 * n) + [ANY] * len(after),
        out_specs=(SEM, SEM, *[HBM] * (2 * n), pl.BlockSpec(memory_space=pltpu.VMEM)),
        input_output_aliases={i: 2 + i for i in range(2 * n)},
        compiler_params=pltpu.CompilerParams(has_side_effects=_EFFECT),
    )(*srcs, *zones, *after)
    return out[0], out[1], out[2:2 + n], out[2 + n:2 + 2 * n], out[-1]


def exchange_wait(send_sems, recv_sems, srcs, zones, after, *, which, sliced, name):
    n = len(srcs)

    def body(*refs):
        ins, lands = refs[:n], refs[n:2 * n]
        send_sems, recv_sems = refs[2 * n:2 * n + 2]
        x, y, c = _place()
        for a in range(n):
            for j, chip in enumerate(_other_chips(x, y)):
                copy = pltpu.make_async_remote_copy(
                    src_ref=_piece(ins[a], sliced, chip), dst_ref=lands[a].at[2 * chip[0] + chip[1]],
                    send_sem=send_sems.at[3 * which[a] + j], recv_sem=recv_sems.at[3 * which[a] + j],
                    device_id=(*chip, c), device_id_type=MESH)
                copy.wait_send()
                copy.wait_recv()

    out = pl.pallas_call(
        body, name=name,
        out_shape=tuple(pltpu.HBM(s.shape, s.dtype) for s in (*srcs, *zones)),
        in_specs=[HBM] * (2 * n) + [SEM, SEM, ANY], out_specs=tuple([HBM] * (2 * n)),
        input_output_aliases={i: i for i in range(2 * n)},
        compiler_params=pltpu.CompilerParams(has_side_effects=_EFFECT),
    )(*srcs, *zones, send_sems, recv_sems, after)
    return out[n:]


def sibling_swap(parts):
    n = len(parts)

    def body(*refs):
        ins, outs = refs[:n], refs[n:2 * n]
        send_sems, recv_sems = refs[2 * n:]
        x, y, c = _place()
        cps = []
        for a in range(n):
            cp = pltpu.make_async_remote_copy(
                src_ref=ins[a], dst_ref=outs[a], send_sem=send_sems.at[a], recv_sem=recv_sems.at[a],
                device_id=(x, y, 1 - c), device_id_type=MESH)
            cp.start()
            cps.append(cp)
        for cp in cps:
            cp.wait()

    return pl.pallas_call(
        body, name="sibling_swap", out_shape=[jax.ShapeDtypeStruct(p.shape, p.dtype) for p in parts],
        in_specs=[ANY] * n, out_specs=[ANY] * n,
        scratch_shapes=[pltpu.SemaphoreType.DMA((n,)), pltpu.SemaphoreType.DMA((n,))],
    )(*parts)


def _row_block(rows, cols, budget=1 << 20):
    best = rows if rows % 8 else 8
    for tr in range(8, rows + 1, 8):
        if rows % tr == 0 and tr * cols * 4 <= budget:
            best = tr
    return best


def chip_sum(own, recv, me_s, buf, layer, layers, *, name):
    r, n = own.shape
    tr = _row_block(r, n)
    steps = r // tr

    def body(me_ref, own_ref, recv_ref, *rest):
        o_ref = rest[-1]
        me = me_ref[0]
        acc = jnp.zeros((tr, n), F32)
        for s in range(4):
            acc = acc + jnp.where(me == s, own_ref[...], recv_ref[s].astype(F32))
        o_ref[...] = acc

    in_specs = [pl.BlockSpec((tr, n), lambda i, me: (i, 0)), pl.BlockSpec((4, tr, n), lambda i, me: (0, i, 0))]
    args = [me_s, own, recv]
    aliases = {}
    if buf is not None:
        in_specs.append(ANY)
        args.append(buf)
        aliases = {3: 0}
    return pl.pallas_call(
        body, name=name, out_shape=jax.ShapeDtypeStruct((layers * r, n), F32),
        grid_spec=pltpu.PrefetchScalarGridSpec(
            num_scalar_prefetch=1, grid=(steps,), in_specs=in_specs,
            out_specs=pl.BlockSpec((tr, n), lambda i, me: (layer * steps + i, 0))),
        input_output_aliases=aliases,
        compiler_params=pltpu.CompilerParams(dimension_semantics=("parallel",)),
    )(*args)


def _adam_update(w, g, m, v):
    m2 = ADAM_B1 * m + (1.0 - ADAM_B1) * g
    v2 = ADAM_B2 * v + (1.0 - ADAM_B2) * (g * g)
    m_hat = m2 / (1.0 - ADAM_B1 ** ADAM_STEP)
    v_hat = v2 / (1.0 - ADAM_B2 ** ADAM_STEP)
    delta = -ADAM_LR * (m_hat / (jnp.sqrt(v_hat) + ADAM_EPS) + ADAM_WD * w)
    return delta, m2, v2


def adamw(w, g_parts, m, v, *, name):
    r, n = w.shape
    tr = _row_block(r, n)
    k = len(g_parts)

    def body(*refs):
        w_ref, m_ref, v_ref = refs[k], refs[k + 1], refs[k + 2]
        g_ref, d_ref, m2_ref, v2_ref = refs[k + 3:]
        g = refs[0][...]
        for p in refs[1:k]:
            g = g + p[...]
        g_ref[...] = g
        d_ref[...], m2_ref[...], v2_ref[...] = _adam_update(w_ref[...], g, m_ref[...], v_ref[...])

    blk = pl.BlockSpec((tr, n), lambda i: (i, 0))
    shp = jax.ShapeDtypeStruct((r, n), F32)
    return _call(body, name=name, grid=(r // tr,), in_specs=[blk] * (k + 3), out_specs=[blk] * 4,
                 out_shape=[shp] * 4, semantics=("parallel",))(*g_parts, w, m, v)


def adamw_gathered(g_all, w, m, v, *, name):
    _, r, n = g_all.shape
    tr = _row_block(r, n * 4)

    def body(ga_ref, w_ref, m_ref, v_ref, g_ref, d_ref, m2_ref, v2_ref):
        g = ga_ref[0]
        for dev in range(1, N_DEV):
            g = g + ga_ref[dev]
        g_ref[...] = g
        d_ref[...], m2_ref[...], v2_ref[...] = _adam_update(w_ref[...], g, m_ref[...], v_ref[...])

    blk = pl.BlockSpec((tr, n), lambda i: (i, 0))
    shp = jax.ShapeDtypeStruct((r, n), F32)
    return _call(body, name=name, grid=(r // tr,),
                 in_specs=[pl.BlockSpec((N_DEV, tr, n), lambda i: (0, i, 0)), blk, blk, blk], out_specs=[blk] * 4,
                 out_shape=[shp] * 4, semantics=("parallel",))(g_all, w, m, v)


BIG = ("w_in", "w_out", "w_ffn_in", "w_ffn_out")
SMALL = ("b_mod", "mix_norm_w", "ffn_norm_w", "dn_conv_w", "dn_a_log", "dn_dt_bias", "dn_out_norm_w", "gm_ln_g",
         "gm_ln_b", "gm_w_s", "gm_b_s", "sw_q_norm_w", "sw_k_norm_w")
WEIGHTS = ("w_mod", "b_mod", "mix_norm_w", "ffn_norm_w", "w_in", "w_out", "dn_conv_w", "dn_a_log", "dn_dt_bias",
           "dn_out_norm_w", "gm_ln_g", "gm_ln_b", "gm_w_s", "gm_b_s", "sw_q_norm_w", "sw_k_norm_w", "w_ffn_in",
           "w_ffn_out")
PACK_ROWS = 8


def _pack(arrs):
    out = []
    for a in arrs:
        flat = a.reshape(-1).astype(F32)
        rows = -(-flat.shape[0] // (LANE * PACK_ROWS)) * PACK_ROWS
        out.append(jnp.pad(flat, (0, rows * LANE - flat.shape[0])).reshape(rows, LANE))
    return jnp.concatenate(out, axis=0)


def _unpack(packed, shapes):
    out, r0 = [], 0
    for shp in shapes:
        size = math.prod(shp)
        rows = -(-size // (LANE * PACK_ROWS)) * PACK_ROWS
        out.append(packed[r0:r0 + rows].reshape(-1)[:size].reshape(shp))
        r0 += rows
    return out


def kernel(x, c, w_mod, b_mod, mix_norm_w, ffn_norm_w, w_in, w_out, dn_conv_w, dn_a_log, dn_dt_bias, dn_out_norm_w, gm_ln_g, gm_ln_b, gm_w_s, gm_b_s, sw_q_norm_w, sw_k_norm_w, w_ffn_in, w_ffn_out, loss_target, m_w_mod, m_b_mod, m_mix_norm_w, m_ffn_norm_w, m_w_in, m_w_out, m_dn_conv_w, m_dn_a_log, m_dn_dt_bias, m_dn_out_norm_w, m_gm_ln_g, m_gm_ln_b, m_gm_w_s, m_gm_b_s, m_sw_q_norm_w, m_sw_k_norm_w, m_w_ffn_in, m_w_ffn_out, v_w_mod, v_b_mod, v_mix_norm_w, v_ffn_norm_w, v_w_in, v_w_out, v_dn_conv_w, v_dn_a_log, v_dn_dt_bias, v_dn_out_norm_w, v_gm_ln_g, v_gm_ln_b, v_gm_w_s, v_gm_b_s, v_sw_q_norm_w, v_sw_k_norm_w, v_w_ffn_in, v_w_ffn_out):
    w = dict(w_mod=w_mod, b_mod=b_mod, mix_norm_w=mix_norm_w, ffn_norm_w=ffn_norm_w, w_in=w_in, w_out=w_out,
             dn_conv_w=dn_conv_w, dn_a_log=dn_a_log, dn_dt_bias=dn_dt_bias, dn_out_norm_w=dn_out_norm_w,
             gm_ln_g=gm_ln_g, gm_ln_b=gm_ln_b, gm_w_s=gm_w_s, gm_b_s=gm_b_s, sw_q_norm_w=sw_q_norm_w,
             sw_k_norm_w=sw_k_norm_w, w_ffn_in=w_ffn_in, w_ffn_out=w_ffn_out)
    m = dict(w_mod=m_w_mod, b_mod=m_b_mod, mix_norm_w=m_mix_norm_w, ffn_norm_w=m_ffn_norm_w, w_in=m_w_in,
             w_out=m_w_out, dn_conv_w=m_dn_conv_w, dn_a_log=m_dn_a_log, dn_dt_bias=m_dn_dt_bias,
             dn_out_norm_w=m_dn_out_norm_w, gm_ln_g=m_gm_ln_g, gm_ln_b=m_gm_ln_b, gm_w_s=m_gm_w_s, gm_b_s=m_gm_b_s,
             sw_q_norm_w=m_sw_q_norm_w, sw_k_norm_w=m_sw_k_norm_w, w_ffn_in=m_w_ffn_in, w_ffn_out=m_w_ffn_out)
    v = dict(w_mod=v_w_mod, b_mod=v_b_mod, mix_norm_w=v_mix_norm_w, ffn_norm_w=v_ffn_norm_w, w_in=v_w_in,
             w_out=v_w_out, dn_conv_w=v_dn_conv_w, dn_a_log=v_dn_a_log, dn_dt_bias=v_dn_dt_bias,
             dn_out_norm_w=v_dn_out_norm_w, gm_ln_g=v_gm_ln_g, gm_ln_b=v_gm_ln_b, gm_w_s=v_gm_w_s, gm_b_s=v_gm_b_s,
             sw_q_norm_w=v_sw_q_norm_w, sw_k_norm_w=v_sw_k_norm_w, w_ffn_in=v_w_ffn_in, w_ffn_out=v_w_ffn_out)
    layers, d, mod_n = w_mod.shape
    mx, my, mc = _place()
    me_s = 2 * mx + my
    me_dev = 4 * mx + 2 * my + mc

    c_all = allgather8(_pad_rows(c, 8), name="gather_c").reshape(N_DEV, 8, d)[:, 0]
    b_shard = lax.dynamic_slice_in_dim(b_mod, me_s * mod_n, mod_n, axis=1)[:, None, :]
    mod_part = mod_fwd(c_all, w_mod, b_shard)
    mod_parts = allgather8(mod_part.reshape(layers * 8, mod_n), name="gather_mod")
    mod_parts = mod_parts.reshape(4, 2, layers, 8, mod_n)[:, 0]
    mod_all = mod_parts.transpose(1, 2, 0, 3).reshape(layers, 8, 4 * mod_n)
    mods = lax.dynamic_index_in_dim(mod_all, me_dev, axis=1, keepdims=False)

    cw = dn_conv_w.shape[-1]
    conv_rows = -(-layers * DN_CONV // 8) * 8
    conv_parts = allgather8(_pad_rows(dn_conv_w.reshape(layers * DN_CONV, cw), conv_rows), name="gather_conv")
    conv_parts = conv_parts.reshape(4, 2, conv_rows, cw)[:, 0, :layers * DN_CONV]
    conv_full = conv_parts.reshape(4, layers, DN_CONV, cw).transpose(1, 2, 0, 3).reshape(layers, DN_CONV, 4 * cw)

    shards = {k: w[k].astype(BF16) for k in BIG}
    groups = dict(w_in=(0,), w_out=(1,), ffn=(2, 3))
    gathers = [exchange_start([shards[k][layer] for k in BIG], [mods, conv_full], sliced=False, name=f"gather_start{layer}")
               for layer in range(layers)]
    mods = mods + sum(g[4][0, 0] for g in gathers)

    def weights_of(layer, group, after):
        send_sems, recv_sems, srcs, zones, _ = gathers[layer]
        which = groups[group]
        got = exchange_wait(send_sems, recv_sems, [srcs[a] for a in which], [zones[a] for a in which], after,
                            which=which, sliced=False, name=f"gather_wait_{group}{layer}")
        full = {BIG[a]: lax.dynamic_update_index_in_dim(z, shards[BIG[a]][layer], me_s, 0) for a, z in zip(which, got)}
        cols = lambda g: jnp.concatenate([g[s] for s in range(4)], axis=-1)
        shape = dict(w_in=lambda g: _permute_w_in(cols(g)), w_out=lambda g: g.reshape(-1, d), w_ffn_in=cols,
                     w_ffn_out=lambda g: g.reshape(-1, d))
        return {k: shape[k](g) for k, g in full.items()}

    scatters = {}
    shard_axis = dict(w_in=1, w_out=0, w_ffn_in=1, w_ffn_out=0)

    def grads_done(layer, group, grads):
        grads = {k: _unpermute_w_in(g) if k == "w_in" else g for k, g in grads.items()}
        send = [jnp.stack(jnp.split(g.astype(BF16), 4, axis=shard_axis[k])) for k, g in grads.items()]
        own = {}
        for k, g in grads.items():
            size = g.shape[shard_axis[k]] // 4
            own[k] = lax.dynamic_slice_in_dim(g, me_s * size, size, axis=shard_axis[k])
        started = exchange_start(send, [], sliced=True, name=f"scatter_start_{group}{layer}")
        scatters[layer, group] = (started, own)
        return started[4][:1, :1]

    sp = {k: w[k] for k in SMALL}
    sp["dn_conv_w"] = conv_full
    loss_blk, grad_x, small, dmods = _local_step(x[0], loss_target[0], mods, weights_of, grads_done, sp)
    loss = lax.psum(loss_blk[0, 0], ("x", "y", "c"))

    outs = {}
    small = dict(small, b_mod=dmods)
    packed = _pack([small[k] for k in SMALL])
    rows = packed.shape[0]
    g_all = allgather8(packed, name="gather_small").reshape(N_DEV, rows, LANE)
    conv_zero = jnp.zeros((layers, DN_CONV, 3 * DN_WIDTH), F32)
    pk = lambda src: _pack([conv_zero if k == "dn_conv_w" else src[k] for k in SMALL])
    res = adamw_gathered(g_all, pk(w), pk(m), pk(v), name="adamw_small")
    shapes = [small[k].shape for k in SMALL]
    un = [_unpack(a, shapes) for a in res]
    for i, k in enumerate(SMALL):
        outs[k] = [un[j][i] for j in range(4)]
    g_conv = lax.dynamic_slice_in_dim(outs["dn_conv_w"][0], me_s * cw, cw, axis=2)
    flat = lambda a: a.reshape(-1, cw)
    res = adamw(flat(dn_conv_w), [flat(g_conv)], flat(m["dn_conv_w"]), flat(v["dn_conv_w"]), name="adamw_conv")
    outs["dn_conv_w"] = [a.reshape(dn_conv_w.shape) for a in res]

    b_rows = layers * 6 * d // LANE
    dmod_all = g_all[:, :b_rows].reshape(N_DEV, layers, 6 * d).transpose(1, 0, 2)
    dmod_shard = lax.dynamic_slice_in_dim(dmod_all, me_s * mod_n, mod_n, axis=2)
    g_wmod = mod_bwd(c_all, dmod_shard)
    flat = lambda a: a.reshape(-1, mod_n)
    res = adamw(flat(w_mod), [flat(g_wmod)], flat(m_w_mod), flat(v_w_mod), name="adamw_w_mod")
    outs["w_mod"] = [a.reshape(w_mod.shape) for a in res]

    me_arr = jnp.reshape(me_s, (1,)).astype(jnp.int32)
    partial = {k: None for k in BIG}
    for layer in range(layers):
        for group in ("ffn", "mix"):
            (send_sems, recv_sems, srcs, zones, _), own = scatters[layer, group]
            zones = exchange_wait(send_sems, recv_sems, srcs, zones, res[0], which=tuple(range(len(srcs))),
                                  sliced=True, name=f"scatter_wait_{group}{layer}")
            for k, z in zip(own, zones):
                partial[k] = chip_sum(own[k], z, me_arr, partial[k], layer, layers, name=f"chip_sum_{k}{layer}")
    partial = [partial[k] for k in BIG]
    theirs = sibling_swap(partial)
    for k, mine, other in zip(BIG, partial, theirs):
        shp = w[k].shape
        flat = lambda a: a.reshape(-1, shp[-1])
        res = adamw(flat(w[k]), [mine, other], flat(m[k]), flat(v[k]), name="adamw_" + k)
        outs[k] = [a.reshape(shp) for a in res]

    result = [loss, grad_x[None]]
    for j in range(4):
        result += [outs[k][j] for k in WEIGHTS]
    return tuple(result)
```

```python
import functools
import math

import jax
import jax.numpy as jnp
from jax import lax
from jax.experimental import pallas as pl
from jax.experimental.pallas import tpu as pltpu

F32 = jnp.float32
BF16 = jnp.bfloat16
HI = lax.Precision.HIGH

NORM_EPS = 1e-6
DN_HEADS = 4
DN_HEAD_DIM = 128
DN_WIDTH = 512
DN_CHUNK = 64
DN_CONV = 4
GM_WIDTH = 256
GM_GROUPS = 4
GM_GROUP_DIM = 64
GM_CHUNK = 128
SW_HEADS = 4
SW_HEAD_DIM = 64
SW_WIDTH = 256
SW_DILATIONS = (1, 4, 16)
SW_BLOCK = 128
ROPE_THETA = 500000.0
ROPE_DIM = 16
LANE = 128

C_QKV = 0
C_Z = 1536
C_AB = 2048
C_SW = 2304
C_UV = 4608
IN_WIDTH = 4872
IN_PAD = 5120
AB_PAD = C_SW - C_AB
MIX_WIDTH = 1024

ADAM_LR = 0.001
ADAM_B1 = 0.9
ADAM_B2 = 0.999
ADAM_EPS = 1e-08
ADAM_WD = 0.01
ADAM_STEP = 10

MESH = pl.DeviceIdType.MESH


BIG_VMEM = 56 << 20


def _call(body, *, name, grid, in_specs, out_specs, out_shape, scratch_shapes=(), semantics=None, aliases=None,
          vmem=None):
    if semantics is None:
        semantics = ("arbitrary",) * len(grid)
    return pl.pallas_call(
        body, name=name, grid=grid, in_specs=in_specs, out_specs=out_specs, out_shape=out_shape,
        scratch_shapes=list(scratch_shapes), input_output_aliases=aliases or {},
        compiler_params=pltpu.CompilerParams(dimension_semantics=semantics, vmem_limit_bytes=vmem),
    )


def _dot(a, b, ca, cb, prec=None):
    if a.ndim == 3:
        dims = (((ca + 1,), (cb + 1,)), ((0,), (0,)))
    else:
        dims = (((ca,), (cb,)), ((), ()))
    return lax.dot_general(a, b, dims, preferred_element_type=F32, precision=prec)


def _bdot(a, b, ca=1, cb=0):
    return _dot(a.astype(BF16), b.astype(BF16), ca, cb)


def _hdot(a, b, ca=1, cb=0):
    return _dot(a.astype(F32), b.astype(F32), ca, cb, HI)


def _split(x):
    hi = x.astype(BF16)
    return hi, (x - hi.astype(F32)).astype(BF16)


def _xdot(a, b, ca=1, cb=0, exact=1):
    if exact == 1:
        hi, lo = _split(a)
        e = b.astype(BF16)
        return _dot(hi, e, ca, cb) + _dot(lo, e, ca, cb)
    hi, lo = _split(b)
    e = a.astype(BF16)
    return _dot(e, hi, ca, cb) + _dot(e, lo, ca, cb)


def _sigmoid(x):
    return 0.5 * jnp.tanh(0.5 * x) + 0.5


def _silu(x):
    return x * _sigmoid(x)


def _dsilu(x):
    s = _sigmoid(x)
    return s * (1.0 + x * (1.0 - s))


def _softplus(x):
    return jnp.maximum(x, 0.0) + jnp.log(1.0 + jnp.exp(-jnp.abs(x)))


def _iota2(shape, dim):
    return lax.broadcasted_iota(jnp.int32, shape, dim)


def _rowsum(x):
    return jnp.sum(x, axis=-1, keepdims=True)


def _colsum(x):
    return jnp.sum(x, axis=-2, keepdims=True)


def _full(shape):
    return pl.BlockSpec(shape, lambda *_: (0,) * len(shape))


def _resident(shape):
    return pl.BlockSpec(shape, lambda *_: (0,) * len(shape), pipeline_mode=pl.Buffered(1))


ANY = pl.BlockSpec(memory_space=pl.ANY)


def _norm_mod(x, nw, scale, shift):
    r = lax.rsqrt(jnp.mean(x * x, axis=-1, keepdims=True) + NORM_EPS)
    xn = x * r
    return xn, r, (xn * nw) * (1.0 + scale) + shift


def norm_mm(x, nw, scale, shift, w, *, swiglu, name, tm=512):
    t, d = x.shape
    n = w.shape[1]
    half = n // 2

    def body(x_ref, nw_ref, sc_ref, sh_ref, w_ref, h_ref, y_ref, *act_ref):
        _, _, h = _norm_mod(x_ref[...], nw_ref[...], sc_ref[...], sh_ref[...])
        hb = h.astype(BF16)
        h_ref[...] = hb
        y = _dot(hb, w_ref[...], 1, 0)
        y_ref[...] = y.astype(y_ref.dtype)
        if swiglu:
            act_ref[0][...] = (_silu(y[:, :half]) * y[:, half:]).astype(BF16)

    row = lambda i: (i, 0)
    out_shape = [jax.ShapeDtypeStruct((t, d), BF16), jax.ShapeDtypeStruct((t, n), BF16 if swiglu else F32)]
    out_specs = [pl.BlockSpec((tm, d), row), pl.BlockSpec((tm, n), row)]
    if swiglu:
        out_shape.append(jax.ShapeDtypeStruct((t, half), BF16))
        out_specs.append(pl.BlockSpec((tm, half), row))
    return _call(
        body, name=name, grid=(t // tm,),
        in_specs=[pl.BlockSpec((tm, d), row), _full((1, d)), _full((1, d)), _full((1, d)), _resident((d, n))],
        out_specs=out_specs, out_shape=out_shape, semantics=("parallel",), vmem=BIG_VMEM,
    )(x, nw, scale, shift, w)


def resid_mm(y, w, x, gate, *, name, tm=512):
    t, k = y.shape
    d = w.shape[1]

    def body(y_ref, w_ref, x_ref, g_ref, xo_ref, o_ref):
        o = _dot(y_ref[...].astype(BF16), w_ref[...], 1, 0)
        o_ref[...] = o
        xo_ref[...] = x_ref[...] + g_ref[...] * o

    row = lambda i: (i, 0)
    return _call(
        body, name=name, grid=(t // tm,),
        in_specs=[pl.BlockSpec((tm, k), row), _resident((k, d)), pl.BlockSpec((tm, d), row), _full((1, d))],
        out_specs=[pl.BlockSpec((tm, d), row), pl.BlockSpec((tm, d), row)],
        out_shape=[jax.ShapeDtypeStruct((t, d), F32), jax.ShapeDtypeStruct((t, d), F32)],
        semantics=("parallel",), vmem=BIG_VMEM,
    )(y, w, x, gate)


def resid_mm_bwd(dx, gate, o, w, gu, *, name, tm):
    t, d = dx.shape
    k = w.shape[0]
    swiglu = gu is not None

    def body(dx_ref, g_ref, o_ref, w_ref, *rest):
        if swiglu:
            gu_ref, dy_ref, gx_ref, dg_ref = rest
        else:
            dy_ref, gx_ref, dg_ref = rest
        i = pl.program_id(0)
        dxv = dx_ref[...]
        gx = (dxv * g_ref[...]).astype(BF16)
        gx_ref[...] = gx
        part = _colsum(dxv * o_ref[...])

        @pl.when(i == 0)
        def _():
            dg_ref[...] = jnp.zeros_like(dg_ref)

        dg_ref[...] += part
        da = _dot(gx, w_ref[...], 1, 1)
        if swiglu:
            g = gu_ref[:, :k].astype(F32)
            u = gu_ref[:, k:].astype(F32)
            dy_ref[:, :k] = (da * u * _dsilu(g)).astype(BF16)
            dy_ref[:, k:] = (da * _silu(g)).astype(BF16)
        else:
            dy_ref[...] = da

    row = lambda i: (i, 0)
    in_specs = [pl.BlockSpec((tm, d), row), _full((1, d)), pl.BlockSpec((tm, d), row), _resident((k, d))]
    args = [dx, gate, o, w]
    if swiglu:
        in_specs.append(pl.BlockSpec((tm, 2 * k), row))
        args.append(gu)
        dy_shape = jax.ShapeDtypeStruct((t, 2 * k), BF16)
        dy_spec = pl.BlockSpec((tm, 2 * k), row)
    else:
        dy_shape = jax.ShapeDtypeStruct((t, k), F32)
        dy_spec = pl.BlockSpec((tm, k), row)
    return _call(
        body, name=name, grid=(t // tm,), in_specs=in_specs,
        out_specs=[dy_spec, pl.BlockSpec((tm, d), row), _full((1, d))],
        out_shape=[dy_shape, jax.ShapeDtypeStruct((t, d), BF16), jax.ShapeDtypeStruct((1, d), F32)], vmem=BIG_VMEM,
    )(*args)


def norm_mm_bwd(dy, w, x, nw, scale, dres, *, name, tm=512):
    t, n = dy.shape
    d = x.shape[1]
    steps = t // tm

    def body(dy_ref, w_ref, x_ref, nw_ref, sc_ref, dres_ref, dx_ref, dnw_ref, dsc_ref, dsh_ref):
        i = pl.program_id(0)
        dh = _dot(dy_ref[...].astype(BF16), w_ref[...], 1, 1)
        x = x_ref[...]
        r = lax.rsqrt(jnp.mean(x * x, axis=-1, keepdims=True) + NORM_EPS)
        xn = x * r
        a = nw_ref[...] * (1.0 + sc_ref[...])

        @pl.when(i == 0)
        def _():
            dnw_ref[...] = jnp.zeros_like(dnw_ref)
            dsh_ref[...] = jnp.zeros_like(dsh_ref)

        dnw_ref[...] += _colsum(dh * xn)
        dsh_ref[...] += _colsum(dh)
        dxn = dh * a
        dx_ref[...] = r * (dxn - xn * jnp.mean(dxn * xn, axis=-1, keepdims=True)) + dres_ref[...]

        @pl.when(i == steps - 1)
        def _():
            da = dnw_ref[...]
            dsc_ref[...] = da * nw_ref[...]
            dnw_ref[...] = da * (1.0 + sc_ref[...])

    row = lambda i: (i, 0)
    vec = jax.ShapeDtypeStruct((1, d), F32)
    return _call(
        body, name=name, grid=(steps,),
        in_specs=[pl.BlockSpec((tm, n), row), _resident((d, n)), pl.BlockSpec((tm, d), row), _full((1, d)),
                  _full((1, d)), pl.BlockSpec((tm, d), row)],
        out_specs=[pl.BlockSpec((tm, d), row), _full((1, d)), _full((1, d)), _full((1, d))],
        out_shape=[jax.ShapeDtypeStruct((t, d), F32), vec, vec, vec], vmem=BIG_VMEM,
    )(dy, w, x, nw, scale, dres)


def _pick_tn(n, k, budget=6 << 20):
    best = LANE
    for m in range(1, n // LANE + 1):
        tn = m * LANE
        if n % tn == 0 and k * tn * 4 <= budget:
            best = tn
    return best


def mm_tn(a, g, *, name, tt=512):
    t, k = a.shape
    n = g.shape[1]
    tn = _pick_tn(n, k)

    def body(a_ref, g_ref, o_ref):
        @pl.when(pl.program_id(1) == 0)
        def _():
            o_ref[...] = jnp.zeros_like(o_ref)

        o_ref[...] += _dot(a_ref[...].astype(BF16), g_ref[...].astype(BF16), 0, 0)

    return _call(
        body, name=name, grid=(n // tn, t // tt),
        in_specs=[pl.BlockSpec((tt, k), lambda j, i: (i, 0)), pl.BlockSpec((tt, tn), lambda j, i: (i, j))],
        out_specs=pl.BlockSpec((k, tn), lambda j, i: (0, j)),
        out_shape=jax.ShapeDtypeStruct((k, n), F32), semantics=("parallel", "arbitrary"),
    )(a, g)


def loss_head(y, target, *, tm=512):
    t, d = y.shape
    steps = t // tm

    def body(y_ref, t_ref, dy_ref, l_ref, acc_ref):
        i = pl.program_id(0)

        @pl.when(i == 0)
        def _():
            acc_ref[...] = jnp.zeros_like(acc_ref)

        e = y_ref[...] - t_ref[...]
        dy_ref[...] = e * (1.0 / d)
        acc_ref[...] += _colsum(e * e)

        @pl.when(i == steps - 1)
        def _():
            tot = jnp.sum(acc_ref[...], axis=-1, keepdims=True) * (0.5 / d)
            l_ref[...] = jnp.broadcast_to(tot, l_ref.shape)

    row = lambda i: (i, 0)
    return _call(
        body, name="loss_head", grid=(steps,),
        in_specs=[pl.BlockSpec((tm, d), row), pl.BlockSpec((tm, d), row)],
        out_specs=[pl.BlockSpec((tm, d), row), _full((8, LANE))],
        out_shape=[jax.ShapeDtypeStruct((t, d), F32), jax.ShapeDtypeStruct((8, LANE), F32)],
        scratch_shapes=[pltpu.VMEM((1, d), F32)],
    )(y, target)


def _shift_rows(x, s):
    if s == 0:
        return x
    t = x.shape[0]
    ri = _iota2(x.shape, 0)
    rolled = pltpu.roll(x, s % t, axis=0)
    if s > 0:
        return jnp.where(ri >= s, rolled, 0.0)
    return jnp.where(ri < t + s, rolled, 0.0)


def _conv_pre(x, w):
    acc = x * w[DN_CONV - 1:DN_CONV, :]
    for j in range(DN_CONV - 1):
        acc = acc + _shift_rows(x, DN_CONV - 1 - j) * w[j:j + 1, :]
    return acc


def dn_conv(proj, conv_w):
    t = proj.shape[0]
    width = 3 * DN_WIDTH

    def body(x_ref, w_ref, o_ref):
        o_ref[...] = _silu(_conv_pre(x_ref[...], w_ref[...]))

    col = lambda j: (0, j)
    return _call(
        body, name="dn_conv", grid=(width // LANE,),
        in_specs=[pl.BlockSpec((t, LANE), col), pl.BlockSpec((8, LANE), col)],
        out_specs=pl.BlockSpec((t, LANE), col),
        out_shape=jax.ShapeDtypeStruct((t, width), F32), semantics=("parallel",),
    )(proj, conv_w)


def dn_conv_bwd(proj, conv_w, dact, dproj):
    t = proj.shape[0]
    width = 3 * DN_WIDTH

    def body(x_ref, w_ref, d_ref, _, dx_ref, dw_ref):
        x = x_ref[...]
        w = w_ref[...]
        dc = d_ref[...] * _dsilu(_conv_pre(x, w))
        dx = dc * w[DN_CONV - 1:DN_CONV, :]
        rows = []
        for j in range(DN_CONV - 1):
            s = DN_CONV - 1 - j
            dx = dx + _shift_rows(dc, -s) * w[j:j + 1, :]
            rows.append(_colsum(dc * _shift_rows(x, s)))
        rows.append(_colsum(dc * x))
        dx_ref[...] = dx.astype(BF16)
        ri = _iota2((8, LANE), 0)
        dw = jnp.zeros((8, LANE), F32)
        for j in range(DN_CONV):
            dw = dw + jnp.where(ri == j, rows[j], 0.0)
        dw_ref[...] = dw

    col = lambda j: (0, j)
    return _call(
        body, name="dn_conv_bwd", grid=(width // LANE,),
        in_specs=[pl.BlockSpec((t, LANE), col), pl.BlockSpec((8, LANE), col), pl.BlockSpec((t, LANE), col), ANY],
        out_specs=[pl.BlockSpec((t, LANE), col), pl.BlockSpec((8, LANE), col)],
        out_shape=[jax.ShapeDtypeStruct(dproj.shape, dproj.dtype), jax.ShapeDtypeStruct((8, width), F32)],
        semantics=("parallel",), aliases={3: 0},
    )(proj, conv_w, dact, dproj)


def _t(x):
    return jnp.swapaxes(x, -1, -2)


def _inv_unit_lower(a):
    c = a.shape[-1]
    eye = (_iota2((c, c), 0) == _iota2((c, c), 1)).astype(F32)
    x = eye - a
    p = _hdot(a, a)
    steps = int(math.log2(c)) - 1
    for i in range(steps):
        x = x + _hdot(x, p)
        if i < steps - 1:
            p = _hdot(p, p)
    return x


def _dn_local(q, k, v, a, b, alog, dtb, tinv=None):
    nh, c, d = q.shape
    rq = lax.rsqrt(_rowsum(q * q) + NORM_EPS)
    rk = lax.rsqrt(_rowsum(k * k) + NORM_EPS)
    qh = q * rq
    kn = k * rk
    qs = qh * (d ** -0.5)
    g = -jnp.exp(alog) * _softplus(a + dtb)
    beta = _sigmoid(b)
    ri = _iota2((c, c), 0)
    ci = _iota2((c, c), 1)
    causal = ri >= ci
    strict = ri > ci
    gb = jnp.broadcast_to(g, (nh, c, d))
    gcb = _xdot(jnp.broadcast_to(causal.astype(F32), (nh, c, c)), gb, exact=0)
    gc = gcb[..., :1]
    gl = _colsum(gb)[..., :1]
    dec = jnp.exp(jnp.where(causal, gc - _t(gcb)[:, :c, :], -1e30))
    kb = kn * beta
    amat = jnp.where(strict, _bdot(kb, kn, 1, 1) * dec, 0.0)
    if tinv is None:
        tinv = _inv_unit_lower(amat)
    e = jnp.exp(gc)
    f = jnp.exp(gl - gc)
    rw = kb * e
    sol = _hdot(tinv, jnp.concatenate([v * beta, rw], axis=-1))
    pmat = jnp.where(causal, _bdot(qs, kn, 1, 1) * dec, 0.0)
    return dict(rq=rq, rk=rk, qh=qh, kn=kn, qs=qs, g=g, beta=beta, causal=causal, strict=strict, gl=gl,
                dec=dec, kb=kb, amat=amat, tinv=tinv, e=e, f=f, rw=rw, u=sol[..., :d], w=sol[..., d:], pmat=pmat,
                qd=qs * e, kd=kn * f)


_DN_FIELDS = ("u", "w", "qd", "kd", "pmat", "gl")


def _dn_state(m, s_in):
    vnew = m["u"] - _bdot(m["w"], s_in)
    o = _bdot(m["qd"], s_in) + _bdot(m["pmat"], vnew)
    return vnew, o, s_in * jnp.exp(m["gl"]) + _bdot(m["kd"], vnew, 0, 0)


def _dn_state_bwd(m, s_in, do, ds_out):
    el = jnp.exp(m["gl"])
    dvnew = _bdot(m["pmat"], do, 0, 0) + _bdot(m["kd"], ds_out)
    dkd = _bdot(m["vnew"], ds_out, 1, 1)
    ds_in = _bdot(m["qd"], do, 0, 0) + el * ds_out - _bdot(m["w"], dvnew, 0, 0)
    dgl = el * _colsum(_rowsum(s_in * ds_out))
    return dvnew, dkd, dgl, ds_in


def _dn_local_bwd(m, q, v, a, alog, dtb, s_in, vnew, do, dvnew, dkd, dgl):
    nh, c, d = q.shape
    kn, qs, kb, u, w, e, f = m["kn"], m["qs"], m["kb"], m["u"], m["w"], m["e"], m["f"]
    beta, dec, tinv, kd, qd = m["beta"], m["dec"], m["tinv"], m["kd"], m["qd"]
    dp = jnp.where(m["causal"], _bdot(do, vnew, 1, 1), 0.0)
    dqd = _bdot(do, s_in, 1, 1)
    dw = -_bdot(dvnew, s_in, 1, 1)
    dsol = _hdot(tinv, jnp.concatenate([dvnew, dw], axis=-1), 0, 0)
    dru = dsol[..., :d]
    drw = dsol[..., d:]
    da_m = -jnp.where(m["strict"], _bdot(dsol, jnp.concatenate([u, w], axis=-1), 1, 1), 0.0)
    db_m = da_m * dec
    dq_m = dp * dec
    dkb = _bdot(db_m, kn)
    dkn = _bdot(db_m, kb, 0, 0) + _bdot(dq_m, qs, 0, 0)
    dqs = _bdot(dq_m, kn)
    gmat = da_m * m["amat"] + dp * m["pmat"]
    ones = jnp.ones((nh, c, d), F32)
    dgam = (_xdot(gmat, ones) - _xdot(gmat, ones, 0, 0))[..., :1]
    dqs = dqs + dqd * e
    dgam = dgam + _rowsum(dqd * qd)
    dkn = dkn + dkd * f
    tk = _rowsum(dkd * kd)
    dgam = dgam - tk
    dgl = dgl + _colsum(tk)
    dkb = dkb + drw * e
    dgam = dgam + _rowsum(drw * m["rw"])
    dv = dru * beta
    dbeta = _rowsum(dru * v) + _rowsum(dkb * kn)
    dkn = dkn + dkb * beta
    last = (_iota2((c, 1), 0) == c - 1).astype(F32)
    dgam = dgam + last * dgl
    upper = (_iota2((c, c), 0) <= _iota2((c, c), 1)).astype(F32)
    dg = _xdot(jnp.broadcast_to(upper, (nh, c, c)), jnp.broadcast_to(dgam, (nh, c, d)), exact=0)[..., :1]
    dqh = dqs * (d ** -0.5)
    dq = m["rq"] * (dqh - m["qh"] * _rowsum(dqh * m["qh"]))
    dk = m["rk"] * (dkn - kn * _rowsum(dkn * kn))
    sg = _sigmoid(a + dtb)
    da = dg * (-jnp.exp(alog)) * sg
    dalog = _colsum(dg * m["g"])
    ddtb = _colsum(da)
    db = dbeta * beta * (1.0 - beta)
    return dq, dk, dv, da, db, dalog, ddtb


def _dn_gate(o, z, wn):
    ro = lax.rsqrt(jnp.mean(o * o, axis=-1, keepdims=True) + NORM_EPS)
    n = o * ro
    return n, ro, n * wn * _silu(z)


DN_PAIR = 2


def _heads(ref, col0):
    d = DN_HEAD_DIM
    return jnp.stack([ref[j * DN_CHUNK:(j + 1) * DN_CHUNK, col0 + h * d:col0 + (h + 1) * d]
                      for j in range(DN_PAIR) for h in range(DN_HEADS)])


def _dn_inputs(act_ref, ab_ref, sc_ref):
    ab = ab_ref[...]
    sc = sc_ref[...]
    rows = lambda j: slice(j * DN_CHUNK, (j + 1) * DN_CHUNK)
    both = [(j, h) for j in range(DN_PAIR) for h in range(DN_HEADS)]
    q = _heads(act_ref, 0)
    k = _heads(act_ref, DN_WIDTH)
    v = _heads(act_ref, 2 * DN_WIDTH)
    a = jnp.stack([ab[rows(j), h:h + 1] for j, h in both])
    b = jnp.stack([ab[rows(j), DN_HEADS + h:DN_HEADS + h + 1] for j, h in both])
    alog = jnp.stack([sc[0:1, h:h + 1] for _, h in both])
    dtb = jnp.stack([sc[1:2, h:h + 1] for _, h in both])
    return q, k, v, a, b, alog, dtb


def _chunk_of(m, j, fields):
    return {f: m[f][j * DN_HEADS:(j + 1) * DN_HEADS] for f in fields}


def dn_fwd(act, proj, scal, wn):
    t = act.shape[0]
    n = t // DN_CHUNK
    d = DN_HEAD_DIM
    rows = DN_PAIR * DN_CHUNK

    def body(act_ref, z_ref, ab_ref, sc_ref, wn_ref, y_ref, st_ref, ti_ref, s_ref):
        @pl.when(pl.program_id(0) == 0)
        def _():
            s_ref[...] = jnp.zeros_like(s_ref)

        m = _dn_local(*_dn_inputs(act_ref, ab_ref, sc_ref))
        s = s_ref[...]
        outs = []
        for j in range(DN_PAIR):
            st_ref[j] = s
            ti_ref[j] = m["tinv"][j * DN_HEADS:(j + 1) * DN_HEADS]
            _, o, s = _dn_state(_chunk_of(m, j, _DN_FIELDS), s)
            outs.append(o)
        s_ref[...] = s
        y = _dn_gate(jnp.concatenate(outs, axis=0), _heads(z_ref, 0), wn_ref[...])[2]
        for j in range(DN_PAIR):
            for h in range(DN_HEADS):
                y_ref[j * DN_CHUNK:(j + 1) * DN_CHUNK, h * d:(h + 1) * d] = y[j * DN_HEADS + h]

    return _call(
        body, name="dn_fwd", grid=(n // DN_PAIR,),
        in_specs=[pl.BlockSpec((rows, 3 * DN_WIDTH), lambda i: (i, 0)),
                  pl.BlockSpec((rows, DN_WIDTH), lambda i: (i, C_Z // DN_WIDTH)),
                  pl.BlockSpec((rows, LANE), lambda i: (i, C_AB // LANE)),
                  _full((8, LANE)), _full((1, d))],
        out_specs=[pl.BlockSpec((rows, DN_WIDTH), lambda i: (i, 0)),
                   pl.BlockSpec((DN_PAIR, DN_HEADS, d, d), lambda i: (i, 0, 0, 0)),
                   pl.BlockSpec((DN_PAIR, DN_HEADS, DN_CHUNK, DN_CHUNK), lambda i: (i, 0, 0, 0))],
        out_shape=[jax.ShapeDtypeStruct((t, MIX_WIDTH), F32), jax.ShapeDtypeStruct((n, DN_HEADS, d, d), F32),
                   jax.ShapeDtypeStruct((n, DN_HEADS, DN_CHUNK, DN_CHUNK), F32)],
        scratch_shapes=[pltpu.VMEM((DN_HEADS, d, d), F32)],
    )(act, proj, proj, scal, wn)


def dn_bwd(act, proj, scal, wn, states, tinvs, dy):
    t = act.shape[0]
    n = t // DN_CHUNK
    steps = n // DN_PAIR
    d = DN_HEAD_DIM
    zab = DN_WIDTH + AB_PAD
    rows = DN_PAIR * DN_CHUNK

    def body(act_ref, z_ref, ab_ref, sc_ref, wn_ref, st_ref, ti_ref, dy_ref, dact_ref, dzab_ref, dpar_ref, ds_ref):
        @pl.when(pl.program_id(0) == 0)
        def _():
            ds_ref[...] = jnp.zeros_like(ds_ref)
            dpar_ref[...] = jnp.zeros_like(dpar_ref)

        wnv = wn_ref[...]
        q, k, v, a, b, alog, dtb = _dn_inputs(act_ref, ab_ref, sc_ref)
        batch = (DN_PAIR * DN_HEADS,)
        s_in = st_ref[...].reshape(batch + (d, d))
        m = _dn_local(q, k, v, a, b, alog, dtb, ti_ref[...].reshape(batch + (DN_CHUNK, DN_CHUNK)))
        vnew, o, _ = _dn_state(m, s_in)
        z = _heads(z_ref, 0)
        dyh = _heads(dy_ref, 0)
        nrm, ro, _ = _dn_gate(o, z, wnv)
        sz = _silu(z)
        dz = dyh * nrm * wnv * _dsilu(z)
        dn = dyh * wnv * sz
        dwn = _colsum(dyh * nrm * sz)
        do = ro * (dn - nrm * jnp.mean(dn * nrm, axis=-1, keepdims=True))
        ds = ds_ref[...]
        parts = [None] * DN_PAIR
        for j in reversed(range(DN_PAIR)):
            mj = dict(_chunk_of(m, j, _DN_FIELDS), vnew=vnew[j * DN_HEADS:(j + 1) * DN_HEADS])
            dvnew, dkd, dgl, ds = _dn_state_bwd(mj, s_in[j * DN_HEADS:(j + 1) * DN_HEADS],
                                                do[j * DN_HEADS:(j + 1) * DN_HEADS], ds)
            parts[j] = (dvnew, dkd, dgl)
        ds_ref[...] = ds
        dvnew, dkd, dgl = (jnp.concatenate([p[i] for p in parts], axis=0) for i in range(3))
        dq, dk, dv, da, db, dalog, ddtb = _dn_local_bwd(m, q, v, a, alog, dtb, s_in, vnew, do, dvnew, dkd, dgl)
        lane = _iota2((DN_CHUNK, LANE), 1)
        prow = _iota2((8, LANE), 0)
        plane = _iota2((8, LANE), 1)
        dpar = jnp.zeros((8, LANE), F32)
        for j in range(DN_PAIR):
            rs = slice(j * DN_CHUNK, (j + 1) * DN_CHUNK)
            dab = jnp.zeros((DN_CHUNK, LANE), F32)
            for h in range(DN_HEADS):
                n_ = j * DN_HEADS + h
                dzab_ref[rs, h * d:(h + 1) * d] = dz[n_].astype(BF16)
                dact_ref[rs, h * d:(h + 1) * d] = dq[n_]
                dact_ref[rs, DN_WIDTH + h * d:DN_WIDTH + (h + 1) * d] = dk[n_]
                dact_ref[rs, 2 * DN_WIDTH + h * d:2 * DN_WIDTH + (h + 1) * d] = dv[n_]
                dab = dab + jnp.where(lane == h, da[n_], 0.0) + jnp.where(lane == DN_HEADS + h, db[n_], 0.0)
                dpar = dpar + jnp.where((prow == 0) & (plane == h), dalog[n_], 0.0)
                dpar = dpar + jnp.where((prow == 1) & (plane == h), ddtb[n_], 0.0)
                dpar = dpar + jnp.where(prow == 2, dwn[n_], 0.0)
            dzab_ref[rs, DN_WIDTH:DN_WIDTH + LANE] = dab.astype(BF16)
            dzab_ref[rs, DN_WIDTH + LANE:] = jnp.zeros((DN_CHUNK, AB_PAD - LANE), BF16)
        dpar_ref[...] += dpar

    rev = lambda i: (steps - 1 - i, 0)
    rev4 = lambda i: (steps - 1 - i, 0, 0, 0)
    return _call(
        body, name="dn_bwd", grid=(steps,),
        in_specs=[pl.BlockSpec((rows, 3 * DN_WIDTH), rev),
                  pl.BlockSpec((rows, DN_WIDTH), lambda i: (steps - 1 - i, C_Z // DN_WIDTH)),
                  pl.BlockSpec((rows, LANE), lambda i: (steps - 1 - i, C_AB // LANE)),
                  _full((8, LANE)), _full((1, d)),
                  pl.BlockSpec((DN_PAIR, DN_HEADS, d, d), rev4),
                  pl.BlockSpec((DN_PAIR, DN_HEADS, DN_CHUNK, DN_CHUNK), rev4),
                  pl.BlockSpec((rows, DN_WIDTH), rev)],
        out_specs=[pl.BlockSpec((rows, 3 * DN_WIDTH), rev),
                   pl.BlockSpec((rows, zab), lambda i: (steps - 1 - i, C_Z // zab)), _full((8, LANE))],
        out_shape=[jax.ShapeDtypeStruct((t, 3 * DN_WIDTH), F32), jax.ShapeDtypeStruct((t, IN_PAD), BF16),
                   jax.ShapeDtypeStruct((8, LANE), F32)],
        scratch_shapes=[pltpu.VMEM((DN_HEADS, d, d), F32)],
    )(act, proj, proj, scal, wn, states, tinvs, dy)


_INV_SQRT2 = 0.7071067811865476
_INV_SQRT2PI = 0.3989422804014327


def _gelu(x):
    return 0.5 * x * (1.0 + lax.erf(x * _INV_SQRT2))


def _dgelu(x):
    return 0.5 * (1.0 + lax.erf(x * _INV_SQRT2)) + x * jnp.exp(-0.5 * x * x) * _INV_SQRT2PI


def _gm_core(uv, lng, lnb, ws_ref, bst):
    c = uv.shape[0]
    zz = _gelu(uv)
    u = zz[:, :GM_WIDTH]
    vv = zz[:, GM_WIDTH:]
    xc = vv - jnp.mean(vv, axis=-1, keepdims=True)
    rs = lax.rsqrt(jnp.mean(xc * xc, axis=-1, keepdims=True) + NORM_EPS)
    xh = xc * rs
    vn = xh * lng + lnb
    grp = _iota2((c, GM_WIDTH), 1) // GM_GROUP_DIM
    tril = _iota2((c, c), 0) >= _iota2((c, c), 1)
    sv = jnp.zeros((c, GM_WIDTH), F32)
    masks = []
    for g in range(GM_GROUPS):
        mk = grp == g
        masks.append(mk)
        ws = jnp.where(tril, ws_ref[g], 0.0)
        sv = sv + _bdot(ws, jnp.where(mk, vn, 0.0)) + jnp.where(mk, bst[:, g:g + 1], 0.0)
    return u, xh, rs, vn, sv, masks, tril


def gm_fwd(proj, lng, lnb, w_s, bst, ybuf):
    t = proj.shape[0]

    def body(uv_ref, g_ref, b_ref, ws_ref, bst_ref, _, y_ref):
        u, _, _, _, sv, _, _ = _gm_core(uv_ref[...], g_ref[...], b_ref[...], ws_ref, bst_ref[...])
        y_ref[...] = u * sv

    return _call(
        body, name="gm_fwd", grid=(t // GM_CHUNK,),
        in_specs=[pl.BlockSpec((GM_CHUNK, 2 * GM_WIDTH), lambda i: (i, C_UV // (2 * GM_WIDTH))),
                  _full((1, GM_WIDTH)), _full((1, GM_WIDTH)), _full((GM_GROUPS, GM_CHUNK, GM_CHUNK)),
                  _full((GM_CHUNK, LANE)), ANY],
        out_specs=pl.BlockSpec((GM_CHUNK, GM_WIDTH), lambda i: (i, DN_WIDTH // GM_WIDTH)),
        out_shape=jax.ShapeDtypeStruct(ybuf.shape, F32), semantics=("parallel",), aliases={5: 0},
    )(proj, lng, lnb, w_s, bst, ybuf)


def gm_bwd(proj, lng, lnb, w_s, bst, dy, dproj):
    t = proj.shape[0]

    def body(uv_ref, g_ref, b_ref, ws_ref, bst_ref, dy_ref, _, duv_ref, dws_ref, dbst_ref, dln_ref):
        @pl.when(pl.program_id(0) == 0)
        def _():
            dws_ref[...] = jnp.zeros_like(dws_ref)
            dbst_ref[...] = jnp.zeros_like(dbst_ref)
            dln_ref[...] = jnp.zeros_like(dln_ref)

        uv = uv_ref[...]
        lng = g_ref[...]
        u, xh, rs, vn, sv, masks, tril = _gm_core(uv, lng, b_ref[...], ws_ref, bst_ref[...])
        dyv = dy_ref[...]
        dsv = dyv * u
        lane = _iota2((GM_CHUNK, LANE), 1)
        dvn = jnp.zeros_like(dsv)
        dbst = jnp.zeros((GM_CHUNK, LANE), F32)
        for g in range(GM_GROUPS):
            ws = jnp.where(tril, ws_ref[g], 0.0)
            dsg = jnp.where(masks[g], dsv, 0.0)
            dvn = dvn + jnp.where(masks[g], _bdot(ws, dsv, 0, 0), 0.0)
            dws_ref[g] += jnp.where(tril, _bdot(dsg, vn, 1, 1), 0.0)
            dbst = dbst + jnp.where(lane == g, _rowsum(dsg), 0.0)
        dbst_ref[...] += dbst
        row = _iota2((8, GM_WIDTH), 0)
        dln_ref[...] += jnp.where(row == 0, _colsum(dvn * xh), 0.0) + jnp.where(row == 1, _colsum(dvn), 0.0)
        dxh = dvn * lng
        dvv = rs * (dxh - jnp.mean(dxh, axis=-1, keepdims=True) - xh * jnp.mean(dxh * xh, axis=-1, keepdims=True))
        dg = _dgelu(uv)
        duv_ref[:, :GM_WIDTH] = (dyv * sv * dg[:, :GM_WIDTH]).astype(BF16)
        duv_ref[:, GM_WIDTH:] = (dvv * dg[:, GM_WIDTH:]).astype(BF16)

    return _call(
        body, name="gm_bwd", grid=(t // GM_CHUNK,),
        in_specs=[pl.BlockSpec((GM_CHUNK, 2 * GM_WIDTH), lambda i: (i, C_UV // (2 * GM_WIDTH))),
                  _full((1, GM_WIDTH)), _full((1, GM_WIDTH)), _full((GM_GROUPS, GM_CHUNK, GM_CHUNK)),
                  _full((GM_CHUNK, LANE)),
                  pl.BlockSpec((GM_CHUNK, GM_WIDTH), lambda i: (i, DN_WIDTH // GM_WIDTH)), ANY],
        out_specs=[pl.BlockSpec((GM_CHUNK, 2 * GM_WIDTH), lambda i: (i, C_UV // (2 * GM_WIDTH))),
                   _full((GM_GROUPS, GM_CHUNK, GM_CHUNK)), _full((GM_CHUNK, LANE)), _full((8, GM_WIDTH))],
        out_shape=[jax.ShapeDtypeStruct(dproj.shape, dproj.dtype),
                   jax.ShapeDtypeStruct((GM_GROUPS, GM_CHUNK, GM_CHUNK), F32),
                   jax.ShapeDtypeStruct((GM_CHUNK, LANE), F32), jax.ShapeDtypeStruct((8, GM_WIDTH), F32)],
        aliases={6: 0},
    )(proj, lng, lnb, w_s, bst, dy, dproj)


def _head_mats():
    r = _iota2((SW_WIDTH, SW_WIDTH), 0)
    c = _iota2((SW_WIDTH, SW_WIDTH), 1)
    same = (r // SW_HEAD_DIM) == (c // SW_HEAD_DIM)
    cc = c % SW_HEAD_DIM
    half = ROPE_DIM // 2
    rot = jnp.where((cc < half) & (r == c + half), -1.0, 0.0) + jnp.where((cc >= half) & (cc < ROPE_DIM) & (r == c - half), 1.0, 0.0)
    return same.astype(F32), rot


def _seg_col(s):
    return C_SW // SW_WIDTH + (s // 2) * 3 + s % 2


def _halves(x):
    return x[:, :LANE], x[:, LANE:]


def sw_prep(proj, nw2, cos_t, sin_t, *, tm=512):
    t = proj.shape[0]

    def body(x_ref, w_ref, c_ref, s_ref, o_ref):
        same, rot = _head_mats()
        x = x_ref[...]
        r = lax.rsqrt(_xdot(x * x, same) * (1.0 / SW_HEAD_DIM) + NORM_EPS)
        xn = x * r * w_ref[0]
        o_ref[0, 0], o_ref[0, 1] = _halves(xn * c_ref[...] + _xdot(xn, rot) * s_ref[...])

    return _call(
        body, name="sw_prep", grid=(6, t // tm),
        in_specs=[pl.BlockSpec((tm, SW_WIDTH), lambda s, i: (i, _seg_col(s))),
                  pl.BlockSpec((1, 1, SW_WIDTH), lambda s, i: (s % 2, 0, 0)),
                  pl.BlockSpec((tm, SW_WIDTH), lambda s, i: (i, 0)),
                  pl.BlockSpec((tm, SW_WIDTH), lambda s, i: (i, 0))],
        out_specs=pl.BlockSpec((1, 2, tm, LANE), lambda s, i: (s, 0, i, 0)),
        out_shape=jax.ShapeDtypeStruct((6, 2, t, LANE), F32), semantics=("parallel", "parallel"),
    )(proj, nw2, cos_t, sin_t)


def sw_prep_bwd(proj, nw2, cos_t, sin_t, dkvq, dproj, dnw, p, *, tm=512):
    t = proj.shape[0]
    col0 = C_SW // SW_WIDTH + 3 * p
    seg_col = lambda s: col0 + (s + 1) % 3

    def body(x_ref, w_ref, c_ref, s_ref, d_ref, _, dw0_ref, dx_ref, dw_ref):
        s = pl.program_id(0)
        dout = jnp.concatenate([d_ref[0, 0], d_ref[0, 1]], axis=1)

        @pl.when(s == 1)
        def _():
            dx_ref[...] = dout.astype(BF16)

        @pl.when((s != 1) & (pl.program_id(1) == 0))
        def _():
            dw_ref[...] = dw0_ref[...]

        @pl.when(s != 1)
        def _():
            same, rot = _head_mats()
            x = x_ref[...]
            w = w_ref[0]
            r = lax.rsqrt(_xdot(x * x, same) * (1.0 / SW_HEAD_DIM) + NORM_EPS)
            xh = x * r
            dxn = dout * c_ref[...] + _xdot(dout * s_ref[...], rot, 1, 1)
            dw_ref[0] += _colsum(dxn * xh)
            dxh = dxn * w
            dx_ref[...] = (r * (dxh - xh * (_xdot(dxh * xh, same) * (1.0 / SW_HEAD_DIM)))).astype(BF16)

    return _call(
        body, name=f"sw_prep_bwd{p}", grid=(3, t // tm),
        in_specs=[pl.BlockSpec((tm, SW_WIDTH), lambda s, i: (i, seg_col(s))),
                  pl.BlockSpec((1, 1, SW_WIDTH), lambda s, i: (1 - s // 2, 0, 0)),
                  pl.BlockSpec((tm, SW_WIDTH), lambda s, i: (i, 0)),
                  pl.BlockSpec((tm, SW_WIDTH), lambda s, i: (i, 0)),
                  pl.BlockSpec((1, 2, tm, LANE), lambda s, i: (s, 0, i, 0)), ANY,
                  pl.BlockSpec((1, 1, SW_WIDTH), lambda s, i: (s // 2, 0, 0))],
        out_specs=[pl.BlockSpec((tm, SW_WIDTH), lambda s, i: (i, seg_col(s))),
                   pl.BlockSpec((1, 1, SW_WIDTH), lambda s, i: (s // 2, 0, 0))],
        out_shape=[jax.ShapeDtypeStruct(dproj.shape, dproj.dtype), jax.ShapeDtypeStruct((2, 1, SW_WIDTH), F32)],
        semantics=("arbitrary", "arbitrary"), aliases={5: 0},
    )(proj, nw2, cos_t, sin_t, dkvq, dproj, dnw)


_SW_SCALE = SW_HEAD_DIM ** -0.5
_NEG = -1e30


def _sw_masks(has_other):
    ri = _iota2((SW_BLOCK, SW_BLOCK), 0)
    ci = _iota2((SW_BLOCK, SW_BLOCK), 1)
    return ri >= ci, (ci >= ri) & has_other


def _pair(x):
    first = _iota2((1, LANE), 1) < SW_HEAD_DIM
    return jnp.stack([jnp.where(first, x, 0.0), jnp.where(first, 0.0, x)])


def _both(x):
    return jnp.broadcast_to(x.astype(BF16)[None], (2,) + x.shape)


def _unpair(x2):
    first = _iota2((1, LANE), 1) < SW_HEAD_DIM
    return jnp.where(first, x2[0], x2[1])


def _head_cols(x):
    return jnp.stack([x[:, 0:1], x[:, SW_HEAD_DIM:SW_HEAD_DIM + 1]])


SW_GROUP = 4


def _sw_geometry(t, p):
    dil = SW_DILATIONS[p]
    unit = SW_BLOCK * dil
    nb = max(1, SW_GROUP // dil)
    return dil, unit, nb, t // (unit * nb)


def _sw_groups(dil, nb, body):
    if nb * dil == SW_GROUP:
        body([(k // dil, k % dil) for k in range(SW_GROUP)])
    else:
        def step(g, carry):
            body([(0, SW_GROUP * g + k) for k in range(SW_GROUP)])
            return carry

        lax.fori_loop(0, nb * dil // SW_GROUP, step, 0)


def _sw_rows(i, r, dil):
    start = i * SW_BLOCK * dil + r
    return pl.ds(start, SW_BLOCK) if dil == 1 else pl.ds(start, SW_BLOCK, stride=dil)


def _sw_load(refs, probs, dil, shift, wrap, fn):
    out = []
    for i, r in probs:
        if shift != 0 and i == wrap:
            out.append(fn(refs[1][_sw_rows(0, r, dil), :]))
        else:
            out.append(fn(refs[0][_sw_rows(i + shift, r, dil), :]))
    return jnp.concatenate(out, axis=0)


def _sw_other_masks(probs, wrap, edge_ok):
    _, other = _sw_masks(edge_ok)
    _, always = _sw_masks(True)
    return jnp.stack([other if i == wrap else always for i, _ in probs for _ in range(2)])


def sw_attn(qk, proj, p):
    t = proj.shape[0]
    dil, unit, nb, nsp = _sw_geometry(t, p)
    vcol = (C_SW + 3 * SW_WIDTH * p + 2 * SW_WIDTH) // LANE

    def body(q_ref, kc_ref, kp_ref, vc_ref, vp_ref, o_ref, l_ref):
        mc, _ = _sw_masks(True)
        first = pl.program_id(1) != 0
        q_r, k_r, v_r = (q_ref.at[0, 0], None), (kc_ref.at[0, 0], kp_ref.at[0, 0]), (vc_ref, vp_ref)

        def one(probs):
            mp = _sw_other_masks(probs, 0, first)
            q2 = _sw_load(q_r, probs, dil, 0, 0, _pair)
            sc = jnp.where(mc, _bdot(q2, _sw_load(k_r, probs, dil, 0, 0, _both), 1, 1) * _SW_SCALE, _NEG)
            sp = jnp.where(mp, _bdot(q2, _sw_load(k_r, probs, dil, -1, 0, _both), 1, 1) * _SW_SCALE, _NEG)
            mx = jnp.maximum(jnp.max(sc, axis=-1, keepdims=True), jnp.max(sp, axis=-1, keepdims=True))
            pc = jnp.exp(sc - mx)
            pp = jnp.exp(sp - mx)
            den = _rowsum(pc) + _rowsum(pp)
            o2 = (_bdot(pc, _sw_load(v_r, probs, dil, 0, 0, _both))
                  + _bdot(pp, _sw_load(v_r, probs, dil, -1, 0, _both))) * (1.0 / den)
            l2 = jnp.broadcast_to(mx + jnp.log(den), o2.shape)
            for n, (i, r) in enumerate(probs):
                o_ref.at[0][_sw_rows(i, r, dil), :] = _unpair(o2[2 * n:2 * n + 2])
                l_ref.at[0][_sw_rows(i, r, dil), :] = _unpair(l2[2 * n:2 * n + 2])

        _sw_groups(dil, nb, one)

    before = lambda j: jnp.maximum(j * nb - 1, 0)
    seg = lambda s: pl.BlockSpec((1, 1, unit * nb, LANE), lambda h, j: (s, h, j, 0))
    seg_b = lambda s: pl.BlockSpec((1, 1, unit, LANE), lambda h, j: (s, h, before(j), 0))
    out = pl.BlockSpec((1, unit * nb, LANE), lambda h, j: (h, j, 0))
    shp = jax.ShapeDtypeStruct((2, t, LANE), F32)
    return _call(
        body, name=f"sw_attn{p}", grid=(2, nsp),
        in_specs=[seg(2 * p), seg(2 * p + 1), seg_b(2 * p + 1),
                  pl.BlockSpec((unit * nb, LANE), lambda h, j: (j, vcol + h)),
                  pl.BlockSpec((unit, LANE), lambda h, j: (before(j), vcol + h))],
        out_specs=[out, out], out_shape=[shp, shp], semantics=("parallel", "parallel"),
    )(qk, qk, qk, proj, proj)


def sw_attn_dkv(qk, proj, dy, lg, dm, p):
    t = proj.shape[0]
    dil, unit, nb, nsp = _sw_geometry(t, p)
    nunits = t // unit
    vcol = (C_SW + 3 * SW_WIDTH * p + 2 * SW_WIDTH) // LANE
    ycol = (DN_WIDTH + GM_WIDTH) // LANE

    def body(k_ref, v_ref, qc_ref, qn_ref, doc_ref, don_ref, lc_ref, ln_ref, dc_ref, dn_ref, o_ref):
        mc, _ = _sw_masks(True)
        more = pl.program_id(1) + 1 < nsp
        q_r, do_r = (qc_ref.at[0, 0], qn_ref.at[0, 0]), (doc_ref, don_ref)
        l_r, d_r = (lc_ref.at[0], ln_ref.at[0]), (dc_ref.at[0], dn_ref.at[0])

        def one(probs):
            k2 = _sw_load((k_ref.at[0, 0], None), probs, dil, 0, 0, _both)
            v2 = _sw_load((v_ref, None), probs, dil, 0, 0, _both)
            dk = jnp.zeros((2 * SW_GROUP, SW_BLOCK, LANE), F32)
            dv = jnp.zeros((2 * SW_GROUP, SW_BLOCK, LANE), F32)
            for shift, mk in ((0, mc), (1, _sw_other_masks(probs, nb - 1, more))):
                q2 = _sw_load(q_r, probs, dil, shift, nb - 1, _pair)
                do2 = _sw_load(do_r, probs, dil, shift, nb - 1, _pair)
                lse = _sw_load(l_r, probs, dil, shift, nb - 1, _head_cols)
                dd = _sw_load(d_r, probs, dil, shift, nb - 1, _head_cols)
                pr = jnp.exp(jnp.where(mk, _bdot(q2, k2, 1, 1) * _SW_SCALE, _NEG) - lse)
                dv = dv + _bdot(pr, do2, 0, 0)
                ds = pr * (_bdot(do2, v2, 1, 1) - dd)
                dk = dk + _bdot(ds, q2, 0, 0)
            for n, (i, r) in enumerate(probs):
                o_ref.at[0, 0][_sw_rows(i, r, dil), :] = (dk[2 * n] + dk[2 * n + 1]) * _SW_SCALE
                o_ref.at[1, 0][_sw_rows(i, r, dil), :] = dv[2 * n] + dv[2 * n + 1]

        _sw_groups(dil, nb, one)

    after = lambda j: jnp.minimum((j + 1) * nb, nunits - 1)
    seg = lambda s: pl.BlockSpec((1, 1, unit * nb, LANE), lambda h, j: (s, h, j, 0))
    seg_a = lambda s: pl.BlockSpec((1, 1, unit, LANE), lambda h, j: (s, h, after(j), 0))
    col = lambda c0: pl.BlockSpec((unit * nb, LANE), lambda h, j: (j, c0 + h))
    col_a = lambda c0: pl.BlockSpec((unit, LANE), lambda h, j: (after(j), c0 + h))
    hp = pl.BlockSpec((1, unit * nb, LANE), lambda h, j: (h, j, 0))
    hp_a = pl.BlockSpec((1, unit, LANE), lambda h, j: (h, after(j), 0))
    return _call(
        body, name=f"sw_dkv{p}", grid=(2, nsp),
        in_specs=[seg(2 * p + 1), col(vcol), seg(2 * p), seg_a(2 * p), col(ycol), col_a(ycol), hp, hp_a, hp, hp_a],
        out_specs=pl.BlockSpec((2, 1, unit * nb, LANE), lambda h, j: (0, h, j, 0)),
        out_shape=jax.ShapeDtypeStruct((3, 2, t, LANE), F32), semantics=("parallel", "parallel"),
    )(qk, proj, qk, qk, dy, dy, lg, lg, dm, dm)


def sw_attn_dq(qk, proj, dy, lg, dm, dkvq, p):
    t = proj.shape[0]
    dil, unit, nb, nsp = _sw_geometry(t, p)
    vcol = (C_SW + 3 * SW_WIDTH * p + 2 * SW_WIDTH) // LANE
    ycol = (DN_WIDTH + GM_WIDTH) // LANE

    def body(q_ref, kc_ref, kp_ref, vc_ref, vp_ref, do_ref, l_ref, d_ref, _, dq_ref):
        mc, _ = _sw_masks(True)
        first = pl.program_id(1) != 0
        k_r, v_r = (kc_ref.at[0, 0], kp_ref.at[0, 0]), (vc_ref, vp_ref)

        def one(probs):
            mp = _sw_other_masks(probs, 0, first)
            q2 = _sw_load((q_ref.at[0, 0], None), probs, dil, 0, 0, _pair)
            do2 = _sw_load((do_ref, None), probs, dil, 0, 0, _pair)
            lse = _sw_load((l_ref.at[0], None), probs, dil, 0, 0, _head_cols)
            dd = _sw_load((d_ref.at[0], None), probs, dil, 0, 0, _head_cols)
            kc = _sw_load(k_r, probs, dil, 0, 0, _both)
            kp = _sw_load(k_r, probs, dil, -1, 0, _both)
            pc = jnp.exp(jnp.where(mc, _bdot(q2, kc, 1, 1) * _SW_SCALE, _NEG) - lse)
            pp = jnp.exp(jnp.where(mp, _bdot(q2, kp, 1, 1) * _SW_SCALE, _NEG) - lse)
            dsc = pc * (_bdot(do2, _sw_load(v_r, probs, dil, 0, 0, _both), 1, 1) - dd)
            dsp = pp * (_bdot(do2, _sw_load(v_r, probs, dil, -1, 0, _both), 1, 1) - dd)
            dq2 = (_bdot(dsc, kc) + _bdot(dsp, kp)) * _SW_SCALE
            for n, (i, r) in enumerate(probs):
                dq_ref.at[0, 0][_sw_rows(i, r, dil), :] = _unpair(dq2[2 * n:2 * n + 2])

        _sw_groups(dil, nb, one)

    before = lambda j: jnp.maximum(j * nb - 1, 0)
    seg = lambda s: pl.BlockSpec((1, 1, unit * nb, LANE), lambda h, j: (s, h, j, 0))
    seg_b = lambda s: pl.BlockSpec((1, 1, unit, LANE), lambda h, j: (s, h, before(j), 0))
    col = lambda c0: pl.BlockSpec((unit * nb, LANE), lambda h, j: (j, c0 + h))
    col_b = lambda c0: pl.BlockSpec((unit, LANE), lambda h, j: (before(j), c0 + h))
    hp = pl.BlockSpec((1, unit * nb, LANE), lambda h, j: (h, j, 0))
    return _call(
        body, name=f"sw_dq{p}", grid=(2, nsp),
        in_specs=[seg(2 * p), seg(2 * p + 1), seg_b(2 * p + 1), col(vcol), col_b(vcol), col(ycol), hp, hp, ANY],
        out_specs=pl.BlockSpec((1, 1, unit * nb, LANE), lambda h, j: (2, h, j, 0)),
        out_shape=jax.ShapeDtypeStruct(dkvq.shape, F32), semantics=("parallel", "parallel"), aliases={8: 0},
    )(qk, qk, qk, proj, proj, dy, lg, dm, dkvq)


def sw_merge(outs, lses, ybuf, *, tm=512):
    t = ybuf.shape[0]

    def body(o0, o1, o2, l0_ref, l1_ref, l2_ref, _, y_ref, lg_ref):
        l0, l1, l2 = l0_ref[...], l1_ref[...], l2_ref[...]
        mx = jnp.maximum(jnp.maximum(l0, l1), l2)
        lg = mx + jnp.log(jnp.exp(l0 - mx) + jnp.exp(l1 - mx) + jnp.exp(l2 - mx))
        lg_ref[...] = lg
        y = jnp.exp(l0 - lg) * o0[...] + jnp.exp(l1 - lg) * o1[...] + jnp.exp(l2 - lg) * o2[...]
        y_ref[...] = jnp.concatenate([y[0], y[1]], axis=1)

    hp = pl.BlockSpec((2, tm, LANE), lambda i: (0, i, 0))
    return _call(
        body, name="sw_merge", grid=(t // tm,), in_specs=[hp] * 6 + [ANY],
        out_specs=[pl.BlockSpec((tm, SW_WIDTH), lambda i: (i, (DN_WIDTH + GM_WIDTH) // SW_WIDTH)), hp],
        out_shape=[jax.ShapeDtypeStruct(ybuf.shape, F32), jax.ShapeDtypeStruct((2, t, LANE), F32)],
        semantics=("parallel",), aliases={6: 0},
    )(*outs, *lses, ybuf)


def sw_delta(dy, ybuf, *, tm=512):
    t = ybuf.shape[0]

    def body(dy_ref, y_ref, o_ref):
        same, _ = _head_mats()
        o_ref[0], o_ref[1] = _halves(_xdot(dy_ref[...] * y_ref[...], same))

    b1 = pl.BlockSpec((tm, SW_WIDTH), lambda i: (i, (DN_WIDTH + GM_WIDTH) // SW_WIDTH))
    return _call(body, name="sw_delta", grid=(t // tm,), in_specs=[b1, b1],
                 out_specs=pl.BlockSpec((2, tm, LANE), lambda i: (0, i, 0)),
                 out_shape=jax.ShapeDtypeStruct((2, t, LANE), F32), semantics=("parallel",))(dy, ybuf)


def _rope_tables(t):
    inv = ROPE_THETA ** (-jnp.arange(0, ROPE_DIM, 2, dtype=F32) / ROPE_DIM)
    ang = jnp.arange(t, dtype=F32)[:, None] * inv[None, :]
    pad1 = jnp.ones((t, SW_HEAD_DIM - ROPE_DIM), F32)
    pad0 = jnp.zeros((t, SW_HEAD_DIM - ROPE_DIM), F32)
    cos_h = jnp.concatenate([jnp.cos(ang), jnp.cos(ang), pad1], axis=1)
    sin_h = jnp.concatenate([jnp.sin(ang), jnp.sin(ang), pad0], axis=1)
    return jnp.tile(cos_h, (1, SW_HEADS)), jnp.tile(sin_h, (1, SW_HEADS))


def sw_forward(proj, nw2, cos_t, sin_t, ybuf):
    qk = sw_prep(proj, nw2, cos_t, sin_t)
    outs, lses = [], []
    for p in range(len(SW_DILATIONS)):
        o, lse = sw_attn(qk, proj, p)
        outs.append(o)
        lses.append(lse)
    ybuf, lg = sw_merge(outs, lses, ybuf)
    return ybuf, (qk, lg)


def sw_backward(proj, nw2, cos_t, sin_t, res, ybuf, dy, dproj):
    qk, lg = res
    dm = sw_delta(dy, ybuf)
    dnw = jnp.zeros((2, 1, SW_WIDTH), F32)
    for p in range(len(SW_DILATIONS)):
        dkvq = sw_attn_dkv(qk, proj, dy, lg, dm, p)
        dkvq = sw_attn_dq(qk, proj, dy, lg, dm, dkvq, p)
        dproj, dnw = sw_prep_bwd(proj, nw2, cos_t, sin_t, dkvq, dproj, dnw, p)
    return dproj, dnw[::-1, 0]


def _pad_rows(a, rows):
    return jnp.zeros((rows,) + a.shape[1:], a.dtype).at[:a.shape[0]].set(a)


def _consts(sp):
    d = {}
    d["mix_nw"] = sp["mix_norm_w"][:, None, :]
    d["ffn_nw"] = sp["ffn_norm_w"][:, None, :]
    d["cw8"] = jnp.pad(sp["dn_conv_w"], ((0, 0), (0, 8 - DN_CONV), (0, 0)))
    d["scal"] = jnp.pad(jnp.stack([sp["dn_a_log"], sp["dn_dt_bias"]], axis=1), ((0, 0), (0, 6), (0, LANE - DN_HEADS)))
    d["wn"] = sp["dn_out_norm_w"][:, None, :]
    d["lng"] = sp["gm_ln_g"][:, None, :]
    d["lnb"] = sp["gm_ln_b"][:, None, :]
    d["w_s"] = sp["gm_w_s"]
    d["bst"] = jnp.pad(jnp.swapaxes(sp["gm_b_s"], 1, 2), ((0, 0), (0, 0), (0, LANE - GM_GROUPS)))
    d["nw2"] = jnp.stack([jnp.tile(sp["sw_q_norm_w"], (1, SW_HEADS)),
                          jnp.tile(sp["sw_k_norm_w"], (1, SW_HEADS))], axis=1)[:, :, None, :]
    return d


def _layer_fwd(x, mod, get_w, cs, tabs):
    wb = dict(get_w("w_in", x))
    h1, proj = norm_mm(x, cs["mix_nw"], mod[1], mod[0], wb["w_in"], swiglu=False, name="in_proj")
    act = dn_conv(proj, cs["cw8"])
    y, states, tinvs = dn_fwd(act, proj, cs["scal"], cs["wn"])
    y = gm_fwd(proj, cs["lng"], cs["lnb"], cs["w_s"], cs["bst"], y)
    y, swres = sw_forward(proj, cs["nw2"], *tabs, y)
    wb.update(get_w("w_out", y))
    x1, o1 = resid_mm(y, wb["w_out"], x, mod[2], name="out_proj")
    wb.update(get_w("ffn", x1))
    h2, gu, actf = norm_mm(x1, cs["ffn_nw"], mod[4], mod[3], wb["w_ffn_in"], swiglu=True, name="ffn_in")
    x2, o2 = resid_mm(actf, wb["w_ffn_out"], x1, mod[5], name="ffn_out")
    res = dict(x=x, h1=h1, proj=proj, act=act, states=states, tinvs=tinvs, swres=swres, y=y, x1=x1, o1=o1, h2=h2, gu=gu,
               actf=actf, o2=o2)
    return x2, res, wb


def _layer_bwd(dx2, res, mod, wb, cs, tabs, grads_done):
    dgu, gx2, dgate2 = resid_mm_bwd(dx2, mod[5], res["o2"], wb["w_ffn_out"], res["gu"], name="ffn_out_bwd", tm=512)
    g_wfo = mm_tn(res["actf"], gx2, name="wg_ffn_out")
    g_wfi = mm_tn(res["h2"], dgu, name="wg_ffn_in")
    token = grads_done("ffn", dict(w_ffn_in=g_wfi, w_ffn_out=g_wfo))
    dx1, d_ffn_nw, dscale2, dshift2 = norm_mm_bwd(dgu, wb["w_ffn_in"], res["x1"], cs["ffn_nw"], mod[4] + token, dx2,
                                                  name="ffn_in_bwd")
    dy, gx1, dgate1 = resid_mm_bwd(dx1, mod[2], res["o1"], wb["w_out"], None, name="out_proj_bwd", tm=512)
    g_wout = mm_tn(res["y"], gx1, name="wg_out")
    proj = res["proj"]
    dact, dproj, dpar = dn_bwd(res["act"], proj, cs["scal"], cs["wn"], res["states"], res["tinvs"], dy)
    dproj, dcw = dn_conv_bwd(proj, cs["cw8"], dact, dproj)
    dproj, dws, dbst, dln = gm_bwd(proj, cs["lng"], cs["lnb"], cs["w_s"], cs["bst"], dy, dproj)
    dproj, dnw = sw_backward(proj, cs["nw2"], *tabs, res["swres"], res["y"], dy, dproj)
    g_win = mm_tn(res["h1"], dproj, name="wg_in")
    dx, d_mix_nw, dscale1, dshift1 = norm_mm_bwd(dproj, wb["w_in"], res["x"], cs["mix_nw"], mod[1], dx1,
                                                 name="in_proj_bwd")
    dmod = jnp.concatenate([dshift1, dscale1, dgate1, dshift2, dscale2, dgate2], axis=1)
    dnw = dnw.reshape(2, SW_HEADS, SW_HEAD_DIM).sum(1)
    small = dict(mix_norm_w=d_mix_nw[0], ffn_norm_w=d_ffn_nw[0], dn_conv_w=dcw[:DN_CONV],
                 dn_a_log=dpar[0, :DN_HEADS], dn_dt_bias=dpar[1, :DN_HEADS], dn_out_norm_w=dpar[2],
                 gm_ln_g=dln[0], gm_ln_b=dln[1], gm_w_s=dws, gm_b_s=dbst[:, :GM_GROUPS].T,
                 sw_q_norm_w=dnw[0], sw_k_norm_w=dnw[1])
    token = grads_done("mix", dict(w_in=g_win, w_out=g_wout))
    return dx, small, dmod, token


def _permute_w_in(w):
    pad = jnp.zeros(w.shape[:-1] + (AB_PAD - 8,), w.dtype)
    return jnp.concatenate([w[..., 0:2056], pad, w[..., 2568:IN_WIDTH], w[..., 2056:2568]], axis=-1)


def _unpermute_w_in(g):
    return jnp.concatenate([g[..., 0:2056], g[..., C_UV:IN_PAD], g[..., C_SW:C_UV]], axis=-1)


def _local_step(x, target, mods, weights_of, grads_done, sp):
    layers = mods.shape[0]
    t, d = x.shape
    tabs = _rope_tables(t)
    consts = _consts(sp)
    saved = []
    for layer in range(layers):
        mod = mods[layer].reshape(6, 1, d)
        cs = {k: v[layer] for k, v in consts.items()}
        x, res, wb = _layer_fwd(x, mod, functools.partial(weights_of, layer), cs, tabs)
        saved.append((res, mod, wb, cs))
    dx, loss = loss_head(x, target)
    smalls, dmods = [], []
    token = jnp.zeros((1, 1), F32)
    for layer in reversed(range(layers)):
        res, mod, wb, cs = saved[layer]
        dx, small, dmod, token = _layer_bwd(dx, res, mod + token, wb, cs, tabs, functools.partial(grads_done, layer))
        smalls.append(small)
        dmods.append(dmod[0])
    smalls, dmods = smalls[::-1], dmods[::-1]
    small = {k: jnp.stack([s[k] for s in smalls]) for k in smalls[0]}
    return loss, dx, small, jnp.stack(dmods) + token


def mod_fwd(c_all, w_mod, b_shard):
    layers, d, n = w_mod.shape

    def body(c_ref, w_ref, b_ref, o_ref):
        ca = _silu(c_ref[...]).astype(BF16)
        o_ref[0] = _dot(ca, w_ref[0].astype(BF16), 1, 0) + b_ref[0]

    return _call(
        body, name="mod_fwd", grid=(layers,),
        in_specs=[_full((8, d)), pl.BlockSpec((1, d, n), lambda i: (i, 0, 0)),
                  pl.BlockSpec((1, 1, n), lambda i: (i, 0, 0))],
        out_specs=pl.BlockSpec((1, 8, n), lambda i: (i, 0, 0)),
        out_shape=jax.ShapeDtypeStruct((layers, 8, n), F32), semantics=("parallel",),
    )(c_all, w_mod, b_shard)


def mod_bwd(c_all, dmod):
    layers, _, n = dmod.shape
    d = c_all.shape[1]

    def body(c_ref, g_ref, o_ref):
        ca = _silu(c_ref[...]).astype(BF16)
        o_ref[0] = _dot(ca, g_ref[0].astype(BF16), 0, 0)

    return _call(
        body, name="mod_bwd", grid=(layers,),
        in_specs=[_full((8, d)), pl.BlockSpec((1, 8, n), lambda i: (i, 0, 0))],
        out_specs=pl.BlockSpec((1, d, n), lambda i: (i, 0, 0)),
        out_shape=jax.ShapeDtypeStruct((layers, d, n), F32), semantics=("parallel",),
    )(c_all, dmod)


N_DEV = 8


def _place():
    return lax.axis_index("x"), lax.axis_index("y"), lax.axis_index("c")


def _other_chips(x, y):
    return [(1 - x, y), (x, 1 - y), (1 - x, 1 - y)]


def allgather8(x_shard, *, name):
    m_per, n = x_shard.shape

    def body(x_ref, out_ref, send_sems, recv_sems, local_sem):
        x, y, c = _place()
        me, sibling = (x, y, c), (x, y, 1 - c)
        chips = _other_chips(x, y)

        def rows(px, py, pc):
            return out_ref.at[pl.ds((4 * px + 2 * py + pc) * m_per, m_per), :]

        def copy(k, block, to, src=None):
            return pltpu.make_async_remote_copy(
                src_ref=rows(*block) if src is None else src, dst_ref=rows(*block),
                send_sem=send_sems.at[k], recv_sem=recv_sems.at[k], device_id=to, device_id_type=MESH)

        mine = pltpu.make_async_copy(x_ref, rows(*me), local_sem)
        mine.start()
        first = [copy(0, me, sibling, src=x_ref)]
        first += [copy(1 + j, me, (*chip, c), src=x_ref) for j, chip in enumerate(chips)]
        for cp in first:
            cp.start()
        passed = [copy(4 + j, (*chip, c), sibling) for j, chip in enumerate(chips)]
        for j, chip in enumerate(chips):
            copy(1 + j, (*chip, c), me).wait_recv()
            passed[j].start()
        copy(0, sibling, me).wait_recv()
        for j, chip in enumerate(chips):
            copy(4 + j, (*chip, 1 - c), me).wait_recv()
        for cp in first + passed:
            cp.wait_send()
        mine.wait()

    return pl.pallas_call(
        body, name=name, out_shape=jax.ShapeDtypeStruct((N_DEV * m_per, n), x_shard.dtype),
        in_specs=[pl.BlockSpec(memory_space=pltpu.VMEM)], out_specs=pl.BlockSpec(memory_space=pltpu.VMEM),
        scratch_shapes=[pltpu.SemaphoreType.DMA((7,)), pltpu.SemaphoreType.DMA((7,)), pltpu.SemaphoreType.DMA],
    )(x_shard)


HBM = pl.BlockSpec(memory_space=pltpu.HBM)
SEM = pl.BlockSpec(memory_space=pltpu.SEMAPHORE)
_EFFECT = pltpu.SideEffectType.DATAFLOW_SIDE_EFFECTING


def _piece(ref, sliced, chip):
    return ref.at[2 * chip[0] + chip[1]] if sliced else ref


def exchange_start(srcs, after, *, sliced, name):
    n = len(srcs)
    piece = lambda s: s.shape[1:] if sliced else s.shape

    def body(*refs):
        ins, lands = refs[:n], refs[n:2 * n]
        send_sems, recv_sems = refs[2 * n + len(after):2 * n + len(after) + 2]
        token = refs[-1]
        x, y, c = _place()
        me_s = 2 * x + y
        for a in range(n):
            for j, chip in enumerate(_other_chips(x, y)):
                pltpu.make_async_remote_copy(
                    src_ref=_piece(ins[a], sliced, chip), dst_ref=lands[a].at[me_s], send_sem=send_sems.at[3 * a + j],
                    recv_sem=recv_sems.at[3 * a + j], device_id=(*chip, c), device_id_type=MESH).start()
        token[...] = jnp.zeros_like(token)

    zones = [pltpu.with_memory_space_constraint(lax.empty((4,) + piece(s), s.dtype), pltpu.HBM) for s in srcs]
    srcs = [pltpu.with_memory_space_constraint(s, pltpu.HBM) for s in srcs]
    out = pl.pallas_call(
        body, name=name,
        out_shape=(pltpu.SemaphoreType.DMA((3 * n,)), pltpu.SemaphoreType.DMA((3 * n,)),
                   *[pltpu.HBM(s.shape, s.dtype) for s in srcs], *[pltpu.HBM(z.shape, z.dtype) for z in zones],
                   jax.ShapeDtypeStruct((8, LANE), F32)),
        in_specs=[HBM] * (2 * n) + [ANY] * len(after),
        out_specs=(SEM, SEM, *[HBM] * (2 * n), pl.BlockSpec(memory_space=pltpu.VMEM)),
        input_output_aliases={i: 2 + i for i in range(2 * n)},
        compiler_params=pltpu.CompilerParams(has_side_effects=_EFFECT),
    )(*srcs, *zones, *after)
    return out[0], out[1], out[2:2 + n], out[2 + n:2 + 2 * n], out[-1]


def exchange_wait(send_sems, recv_sems, srcs, zones, after, *, which, sliced, name):
    n = len(srcs)

    def body(*refs):
        ins, lands = refs[:n], refs[n:2 * n]
        send_sems, recv_sems = refs[2 * n:2 * n + 2]
        x, y, c = _place()
        for a in range(n):
            for j, chip in enumerate(_other_chips(x, y)):
                copy = pltpu.make_async_remote_copy(
                    src_ref=_piece(ins[a], sliced, chip), dst_ref=lands[a].at[2 * chip[0] + chip[1]],
                    send_sem=send_sems.at[3 * which[a] + j], recv_sem=recv_sems.at[3 * which[a] + j],
                    device_id=(*chip, c), device_id_type=MESH)
                copy.wait_send()
                copy.wait_recv()

    out = pl.pallas_call(
        body, name=name,
        out_shape=tuple(pltpu.HBM(s.shape, s.dtype) for s in (*srcs, *zones)),
        in_specs=[HBM] * (2 * n) + [SEM, SEM, ANY], out_specs=tuple([HBM] * (2 * n)),
        input_output_aliases={i: i for i in range(2 * n)},
        compiler_params=pltpu.CompilerParams(has_side_effects=_EFFECT),
    )(*srcs, *zones, send_sems, recv_sems, after)
    return out[n:]


def sibling_swap(parts):
    n = len(parts)

    def body(*refs):
        ins, outs = refs[:n], refs[n:2 * n]
        send_sems, recv_sems = refs[2 * n:]
        x, y, c = _place()
        cps = []
        for a in range(n):
            cp = pltpu.make_async_remote_copy(
                src_ref=ins[a], dst_ref=outs[a], send_sem=send_sems.at[a], recv_sem=recv_sems.at[a],
                device_id=(x, y, 1 - c), device_id_type=MESH)
            cp.start()
            cps.append(cp)
        for cp in cps:
            cp.wait()

    return pl.pallas_call(
        body, name="sibling_swap", out_shape=[jax.ShapeDtypeStruct(p.shape, p.dtype) for p in parts],
        in_specs=[ANY] * n, out_specs=[ANY] * n,
        scratch_shapes=[pltpu.SemaphoreType.DMA((n,)), pltpu.SemaphoreType.DMA((n,))],
    )(*parts)


def _row_block(rows, cols, budget=1 << 20):
    best = rows if rows % 8 else 8
    for tr in range(8, rows + 1, 8):
        if rows % tr == 0 and tr * cols * 4 <= budget:
            best = tr
    return best


def chip_sum(own, recv, me_s, buf, layer, layers, *, name):
    r, n = own.shape
    tr = _row_block(r, n)
    steps = r // tr

    def body(me_ref, own_ref, recv_ref, *rest):
        o_ref = rest[-1]
        me = me_ref[0]
        acc = jnp.zeros((tr, n), F32)
        for s in range(4):
            acc = acc + jnp.where(me == s, own_ref[...], recv_ref[s].astype(F32))
        o_ref[...] = acc

    in_specs = [pl.BlockSpec((tr, n), lambda i, me: (i, 0)), pl.BlockSpec((4, tr, n), lambda i, me: (0, i, 0))]
    args = [me_s, own, recv]
    aliases = {}
    if buf is not None:
        in_specs.append(ANY)
        args.append(buf)
        aliases = {3: 0}
    return pl.pallas_call(
        body, name=name, out_shape=jax.ShapeDtypeStruct((layers * r, n), F32),
        grid_spec=pltpu.PrefetchScalarGridSpec(
            num_scalar_prefetch=1, grid=(steps,), in_specs=in_specs,
            out_specs=pl.BlockSpec((tr, n), lambda i, me: (layer * steps + i, 0))),
        input_output_aliases=aliases,
        compiler_params=pltpu.CompilerParams(dimension_semantics=("parallel",)),
    )(*args)


def _adam_update(w, g, m, v):
    m2 = ADAM_B1 * m + (1.0 - ADAM_B1) * g
    v2 = ADAM_B2 * v + (1.0 - ADAM_B2) * (g * g)
    m_hat = m2 / (1.0 - ADAM_B1 ** ADAM_STEP)
    v_hat = v2 / (1.0 - ADAM_B2 ** ADAM_STEP)
    delta = -ADAM_LR * (m_hat / (jnp.sqrt(v_hat) + ADAM_EPS) + ADAM_WD * w)
    return delta, m2, v2


def adamw(w, g_parts, m, v, *, name):
    r, n = w.shape
    tr = _row_block(r, n)
    k = len(g_parts)

    def body(*refs):
        w_ref, m_ref, v_ref = refs[k], refs[k + 1], refs[k + 2]
        g_ref, d_ref, m2_ref, v2_ref = refs[k + 3:]
        g = refs[0][...]
        for p in refs[1:k]:
            g = g + p[...]
        g_ref[...] = g
        d_ref[...], m2_ref[...], v2_ref[...] = _adam_update(w_ref[...], g, m_ref[...], v_ref[...])

    blk = pl.BlockSpec((tr, n), lambda i: (i, 0))
    shp = jax.ShapeDtypeStruct((r, n), F32)
    return _call(body, name=name, grid=(r // tr,), in_specs=[blk] * (k + 3), out_specs=[blk] * 4,
                 out_shape=[shp] * 4, semantics=("parallel",))(*g_parts, w, m, v)


def adamw_gathered(g_all, w, m, v, *, name):
    _, r, n = g_all.shape
    tr = _row_block(r, n * 4)

    def body(ga_ref, w_ref, m_ref, v_ref, g_ref, d_ref, m2_ref, v2_ref):
        g = ga_ref[0]
        for dev in range(1, N_DEV):
            g = g + ga_ref[dev]
        g_ref[...] = g
        d_ref[...], m2_ref[...], v2_ref[...] = _adam_update(w_ref[...], g, m_ref[...], v_ref[...])

    blk = pl.BlockSpec((tr, n), lambda i: (i, 0))
    shp = jax.ShapeDtypeStruct((r, n), F32)
    return _call(body, name=name, grid=(r // tr,),
                 in_specs=[pl.BlockSpec((N_DEV, tr, n), lambda i: (0, i, 0)), blk, blk, blk], out_specs=[blk] * 4,
                 out_shape=[shp] * 4, semantics=("parallel",))(g_all, w, m, v)


BIG = ("w_in", "w_out", "w_ffn_in", "w_ffn_out")
SMALL = ("b_mod", "mix_norm_w", "ffn_norm_w", "dn_conv_w", "dn_a_log", "dn_dt_bias", "dn_out_norm_w", "gm_ln_g",
         "gm_ln_b", "gm_w_s", "gm_b_s", "sw_q_norm_w", "sw_k_norm_w")
WEIGHTS = ("w_mod", "b_mod", "mix_norm_w", "ffn_norm_w", "w_in", "w_out", "dn_conv_w", "dn_a_log", "dn_dt_bias",
           "dn_out_norm_w", "gm_ln_g", "gm_ln_b", "gm_w_s", "gm_b_s", "sw_q_norm_w", "sw_k_norm_w", "w_ffn_in",
           "w_ffn_out")
PACK_ROWS = 8


def _pack(arrs):
    out = []
    for a in arrs:
        flat = a.reshape(-1).astype(F32)
        rows = -(-flat.shape[0] // (LANE * PACK_ROWS)) * PACK_ROWS
        out.append(jnp.pad(flat, (0, rows * LANE - flat.shape[0])).reshape(rows, LANE))
    return jnp.concatenate(out, axis=0)


def _unpack(packed, shapes):
    out, r0 = [], 0
    for shp in shapes:
        size = math.prod(shp)
        rows = -(-size // (LANE * PACK_ROWS)) * PACK_ROWS
        out.append(packed[r0:r0 + rows].reshape(-1)[:size].reshape(shp))
        r0 += rows
    return out


def kernel(x, c, w_mod, b_mod, mix_norm_w, ffn_norm_w, w_in, w_out, dn_conv_w, dn_a_log, dn_dt_bias, dn_out_norm_w, gm_ln_g, gm_ln_b, gm_w_s, gm_b_s, sw_q_norm_w, sw_k_norm_w, w_ffn_in, w_ffn_out, loss_target, m_w_mod, m_b_mod, m_mix_norm_w, m_ffn_norm_w, m_w_in, m_w_out, m_dn_conv_w, m_dn_a_log, m_dn_dt_bias, m_dn_out_norm_w, m_gm_ln_g, m_gm_ln_b, m_gm_w_s, m_gm_b_s, m_sw_q_norm_w, m_sw_k_norm_w, m_w_ffn_in, m_w_ffn_out, v_w_mod, v_b_mod, v_mix_norm_w, v_ffn_norm_w, v_w_in, v_w_out, v_dn_conv_w, v_dn_a_log, v_dn_dt_bias, v_dn_out_norm_w, v_gm_ln_g, v_gm_ln_b, v_gm_w_s, v_gm_b_s, v_sw_q_norm_w, v_sw_k_norm_w, v_w_ffn_in, v_w_ffn_out):
    w = dict(w_mod=w_mod, b_mod=b_mod, mix_norm_w=mix_norm_w, ffn_norm_w=ffn_norm_w, w_in=w_in, w_out=w_out,
             dn_conv_w=dn_conv_w, dn_a_log=dn_a_log, dn_dt_bias=dn_dt_bias, dn_out_norm_w=dn_out_norm_w,
             gm_ln_g=gm_ln_g, gm_ln_b=gm_ln_b, gm_w_s=gm_w_s, gm_b_s=gm_b_s, sw_q_norm_w=sw_q_norm_w,
             sw_k_norm_w=sw_k_norm_w, w_ffn_in=w_ffn_in, w_ffn_out=w_ffn_out)
    m = dict(w_mod=m_w_mod, b_mod=m_b_mod, mix_norm_w=m_mix_norm_w, ffn_norm_w=m_ffn_norm_w, w_in=m_w_in,
             w_out=m_w_out, dn_conv_w=m_dn_conv_w, dn_a_log=m_dn_a_log, dn_dt_bias=m_dn_dt_bias,
             dn_out_norm_w=m_dn_out_norm_w, gm_ln_g=m_gm_ln_g, gm_ln_b=m_gm_ln_b, gm_w_s=m_gm_w_s, gm_b_s=m_gm_b_s,
             sw_q_norm_w=m_sw_q_norm_w, sw_k_norm_w=m_sw_k_norm_w, w_ffn_in=m_w_ffn_in, w_ffn_out=m_w_ffn_out)
    v = dict(w_mod=v_w_mod, b_mod=v_b_mod, mix_norm_w=v_mix_norm_w, ffn_norm_w=v_ffn_norm_w, w_in=v_w_in,
             w_out=v_w_out, dn_conv_w=v_dn_conv_w, dn_a_log=v_dn_a_log, dn_dt_bias=v_dn_dt_bias,
             dn_out_norm_w=v_dn_out_norm_w, gm_ln_g=v_gm_ln_g, gm_ln_b=v_gm_ln_b, gm_w_s=v_gm_w_s, gm_b_s=v_gm_b_s,
             sw_q_norm_w=v_sw_q_norm_w, sw_k_norm_w=v_sw_k_norm_w, w_ffn_in=v_w_ffn_in, w_ffn_out=v_w_ffn_out)
    layers, d, mod_n = w_mod.shape
    mx, my, mc = _place()
    me_s = 2 * mx + my
    me_dev = 4 * mx + 2 * my + mc

    c_all = allgather8(_pad_rows(c, 8), name="gather_c").reshape(N_DEV, 8, d)[:, 0]
    b_shard = lax.dynamic_slice_in_dim(b_mod, me_s * mod_n, mod_n, axis=1)[:, None, :]
    mod_part = mod_fwd(c_all, w_mod, b_shard)
    mod_parts = allgather8(mod_part.reshape(layers * 8, mod_n), name="gather_mod")
    mod_parts = mod_parts.reshape(4, 2, layers, 8, mod_n)[:, 0]
    mod_all = mod_parts.transpose(1, 2, 0, 3).reshape(layers, 8, 4 * mod_n)
    mods = lax.dynamic_index_in_dim(mod_all, me_dev, axis=1, keepdims=False)

    cw = dn_conv_w.shape[-1]
    conv_rows = -(-layers * DN_CONV // 8) * 8
    conv_parts = allgather8(_pad_rows(dn_conv_w.reshape(layers * DN_CONV, cw), conv_rows), name="gather_conv")
    conv_parts = conv_parts.reshape(4, 2, conv_rows, cw)[:, 0, :layers * DN_CONV]
    conv_full = conv_parts.reshape(4, layers, DN_CONV, cw).transpose(1, 2, 0, 3).reshape(layers, DN_CONV, 4 * cw)

    shards = {k: w[k].astype(BF16) for k in BIG}
    groups = dict(w_in=(0,), w_out=(1,), ffn=(2, 3))
    gathers = [exchange_start([shards[k][layer] for k in BIG], [mods, conv_full], sliced=False, name=f"gather_start{layer}")
               for layer in range(layers)]
    mods = mods + sum(g[4][0, 0] for g in gathers)

    def weights_of(layer, group, after):
        send_sems, recv_sems, srcs, zones, _ = gathers[layer]
        which = groups[group]
        got = exchange_wait(send_sems, recv_sems, [srcs[a] for a in which], [zones[a] for a in which], after,
                            which=which, sliced=False, name=f"gather_wait_{group}{layer}")
        full = {BIG[a]: lax.dynamic_update_index_in_dim(z, shards[BIG[a]][layer], me_s, 0) for a, z in zip(which, got)}
        cols = lambda g: jnp.concatenate([g[s] for s in range(4)], axis=-1)
        shape = dict(w_in=lambda g: _permute_w_in(cols(g)), w_out=lambda g: g.reshape(-1, d), w_ffn_in=cols,
                     w_ffn_out=lambda g: g.reshape(-1, d))
        return {k: shape[k](g) for k, g in full.items()}

    scatters = {}
    last_scatter = []
    shard_axis = dict(w_in=1, w_out=0, w_ffn_in=1, w_ffn_out=0)

    def grads_done(layer, group, grads):
        grads = {k: _unpermute_w_in(g) if k == "w_in" else g for k, g in grads.items()}
        send = [jnp.stack(jnp.split(g.astype(BF16), 4, axis=shard_axis[k])) for k, g in grads.items()]
        own = {}
        for k, g in grads.items():
            size = g.shape[shard_axis[k]] // 4
            own[k] = lax.dynamic_slice_in_dim(g, me_s * size, size, axis=shard_axis[k])
        if (layer, group) == (0, "mix"):
            last_scatter.append((send, own))
            return jnp.zeros((1, 1), F32)
        started = exchange_start(send, [], sliced=True, name=f"scatter_start_{group}{layer}")
        scatters[layer, group] = (started, own)
        return started[4][:1, :1]

    sp = {k: w[k] for k in SMALL}
    sp["dn_conv_w"] = conv_full
    loss_blk, grad_x, small, dmods = _local_step(x[0], loss_target[0], mods, weights_of, grads_done, sp)
    loss = lax.psum(loss_blk[0, 0], ("x", "y", "c"))

    outs = {}
    small = dict(small, b_mod=dmods)
    packed = _pack([small[k] for k in SMALL])
    rows = packed.shape[0]
    g_all = allgather8(packed, name="gather_small").reshape(N_DEV, rows, LANE)
    send, own = last_scatter[0]
    scatters[0, "mix"] = (exchange_start(send, [g_all], sliced=True, name="scatter_start_mix0"), own)
    g_all = g_all + scatters[0, "mix"][0][4][0, 0]
    conv_zero = jnp.zeros((layers, DN_CONV, 3 * DN_WIDTH), F32)
    pk = lambda src: _pack([conv_zero if k == "dn_conv_w" else src[k] for k in SMALL])
    res = adamw_gathered(g_all, pk(w), pk(m), pk(v), name="adamw_small")
    shapes = [small[k].shape for k in SMALL]
    un = [_unpack(a, shapes) for a in res]
    for i, k in enumerate(SMALL):
        outs[k] = [un[j][i] for j in range(4)]
    g_conv = lax.dynamic_slice_in_dim(outs["dn_conv_w"][0], me_s * cw, cw, axis=2)
    flat = lambda a: a.reshape(-1, cw)
    res = adamw(flat(dn_conv_w), [flat(g_conv)], flat(m["dn_conv_w"]), flat(v["dn_conv_w"]), name="adamw_conv")
    outs["dn_conv_w"] = [a.reshape(dn_conv_w.shape) for a in res]

    b_rows = layers * 6 * d // LANE
    dmod_all = g_all[:, :b_rows].reshape(N_DEV, layers, 6 * d).transpose(1, 0, 2)
    dmod_shard = lax.dynamic_slice_in_dim(dmod_all, me_s * mod_n, mod_n, axis=2)
    g_wmod = mod_bwd(c_all, dmod_shard)
    flat = lambda a: a.reshape(-1, mod_n)
    res = adamw(flat(w_mod), [flat(g_wmod)], flat(m_w_mod), flat(v_w_mod), name="adamw_w_mod")
    outs["w_mod"] = [a.reshape(w_mod.shape) for a in res]

    me_arr = jnp.reshape(me_s, (1,)).astype(jnp.int32)
    partial = {k: None for k in BIG}
    for layer in range(layers):
        for group in ("ffn", "mix"):
            (send_sems, recv_sems, srcs, zones, _), own = scatters[layer, group]
            zones = exchange_wait(send_sems, recv_sems, srcs, zones, res[0], which=tuple(range(len(srcs))),
                                  sliced=True, name=f"scatter_wait_{group}{layer}")
            for k, z in zip(own, zones):
                partial[k] = chip_sum(own[k], z, me_arr, partial[k], layer, layers, name=f"chip_sum_{k}{layer}")
    partial = [partial[k] for k in BIG]
    theirs = sibling_swap(partial)
    for k, mine, other in zip(BIG, partial, theirs):
        shp = w[k].shape
        flat = lambda a: a.reshape(-1, shp[-1])
        res = adamw(flat(w[k]), [mine, other], flat(m[k]), flat(v[k]), name="adamw_" + k)
        outs[k] = [a.reshape(shp) for a in res]

    result = [loss, grad_x[None]]
    for j in range(4):
        result += [outs[k][j] for k in WEIGHTS]
    return tuple(result)
```

```python
import functools
import math

import jax
import jax.numpy as jnp
from jax import lax
from jax.experimental import pallas as pl
from jax.experimental.pallas import tpu as pltpu

F32 = jnp.float32
BF16 = jnp.bfloat16
HI = lax.Precision.HIGH

NORM_EPS = 1e-6
DN_HEADS = 4
DN_HEAD_DIM = 128
DN_WIDTH = 512
DN_CHUNK = 64
DN_CONV = 4
GM_WIDTH = 256
GM_GROUPS = 4
GM_GROUP_DIM = 64
GM_CHUNK = 128
SW_HEADS = 4
SW_HEAD_DIM = 64
SW_WIDTH = 256
SW_DILATIONS = (1, 4, 16)
SW_BLOCK = 128
ROPE_THETA = 500000.0
ROPE_DIM = 16
LANE = 128

C_QKV = 0
C_Z = 1536
C_AB = 2048
C_SW = 2304
C_UV = 4608
IN_WIDTH = 4872
IN_PAD = 5120
AB_PAD = C_SW - C_AB
MIX_WIDTH = 1024

ADAM_LR = 0.001
ADAM_B1 = 0.9
ADAM_B2 = 0.999
ADAM_EPS = 1e-08
ADAM_WD = 0.01
ADAM_STEP = 10

MESH = pl.DeviceIdType.MESH


BIG_VMEM = 56 << 20


def _call(body, *, name, grid, in_specs, out_specs, out_shape, scratch_shapes=(), semantics=None, aliases=None,
          vmem=None):
    if semantics is None:
        semantics = ("arbitrary",) * len(grid)
    return pl.pallas_call(
        body, name=name, grid=grid, in_specs=in_specs, out_specs=out_specs, out_shape=out_shape,
        scratch_shapes=list(scratch_shapes), input_output_aliases=aliases or {},
        compiler_params=pltpu.CompilerParams(dimension_semantics=semantics, vmem_limit_bytes=vmem),
    )


def _dot(a, b, ca, cb, prec=None):
    if a.ndim == 3:
        dims = (((ca + 1,), (cb + 1,)), ((0,), (0,)))
    else:
        dims = (((ca,), (cb,)), ((), ()))
    return lax.dot_general(a, b, dims, preferred_element_type=F32, precision=prec)


def _bdot(a, b, ca=1, cb=0):
    return _dot(a.astype(BF16), b.astype(BF16), ca, cb)


def _hdot(a, b, ca=1, cb=0):
    return _dot(a.astype(F32), b.astype(F32), ca, cb, HI)


def _split(x):
    hi = x.astype(BF16)
    return hi, (x - hi.astype(F32)).astype(BF16)


def _xdot(a, b, ca=1, cb=0, exact=1):
    if exact == 1:
        hi, lo = _split(a)
        e = b.astype(BF16)
        return _dot(hi, e, ca, cb) + _dot(lo, e, ca, cb)
    hi, lo = _split(b)
    e = a.astype(BF16)
    return _dot(e, hi, ca, cb) + _dot(e, lo, ca, cb)


def _sigmoid(x):
    return 0.5 * jnp.tanh(0.5 * x) + 0.5


def _silu(x):
    return x * _sigmoid(x)


def _dsilu(x):
    s = _sigmoid(x)
    return s * (1.0 + x * (1.0 - s))


def _softplus(x):
    return jnp.maximum(x, 0.0) + jnp.log(1.0 + jnp.exp(-jnp.abs(x)))


def _iota2(shape, dim):
    return lax.broadcasted_iota(jnp.int32, shape, dim)


def _rowsum(x):
    return jnp.sum(x, axis=-1, keepdims=True)


def _colsum(x):
    return jnp.sum(x, axis=-2, keepdims=True)


def _full(shape):
    return pl.BlockSpec(shape, lambda *_: (0,) * len(shape))


def _resident(shape):
    return pl.BlockSpec(shape, lambda *_: (0,) * len(shape), pipeline_mode=pl.Buffered(1))


ANY = pl.BlockSpec(memory_space=pl.ANY)


def _norm_mod(x, nw, scale, shift):
    r = lax.rsqrt(jnp.mean(x * x, axis=-1, keepdims=True) + NORM_EPS)
    xn = x * r
    return xn, r, (xn * nw) * (1.0 + scale) + shift


def norm_mm(x, nw, scale, shift, w, *, swiglu, name, tm=512):
    t, d = x.shape
    n = w.shape[1]
    half = n // 2

    def body(x_ref, nw_ref, sc_ref, sh_ref, w_ref, h_ref, y_ref, *act_ref):
        _, _, h = _norm_mod(x_ref[...], nw_ref[...], sc_ref[...], sh_ref[...])
        hb = h.astype(BF16)
        h_ref[...] = hb
        y = _dot(hb, w_ref[...], 1, 0)
        y_ref[...] = y.astype(y_ref.dtype)
        if swiglu:
            act_ref[0][...] = (_silu(y[:, :half]) * y[:, half:]).astype(BF16)

    row = lambda i: (i, 0)
    out_shape = [jax.ShapeDtypeStruct((t, d), BF16), jax.ShapeDtypeStruct((t, n), BF16 if swiglu else F32)]
    out_specs = [pl.BlockSpec((tm, d), row), pl.BlockSpec((tm, n), row)]
    if swiglu:
        out_shape.append(jax.ShapeDtypeStruct((t, half), BF16))
        out_specs.append(pl.BlockSpec((tm, half), row))
    return _call(
        body, name=name, grid=(t // tm,),
        in_specs=[pl.BlockSpec((tm, d), row), _full((1, d)), _full((1, d)), _full((1, d)), _resident((d, n))],
        out_specs=out_specs, out_shape=out_shape, semantics=("parallel",), vmem=BIG_VMEM,
    )(x, nw, scale, shift, w)


def resid_mm(y, w, x, gate, *, name, tm=512):
    t, k = y.shape
    d = w.shape[1]

    def body(y_ref, w_ref, x_ref, g_ref, xo_ref, o_ref):
        o = _dot(y_ref[...].astype(BF16), w_ref[...], 1, 0)
        o_ref[...] = o
        xo_ref[...] = x_ref[...] + g_ref[...] * o

    row = lambda i: (i, 0)
    return _call(
        body, name=name, grid=(t // tm,),
        in_specs=[pl.BlockSpec((tm, k), row), _resident((k, d)), pl.BlockSpec((tm, d), row), _full((1, d))],
        out_specs=[pl.BlockSpec((tm, d), row), pl.BlockSpec((tm, d), row)],
        out_shape=[jax.ShapeDtypeStruct((t, d), F32), jax.ShapeDtypeStruct((t, d), F32)],
        semantics=("parallel",), vmem=BIG_VMEM,
    )(y, w, x, gate)


def resid_mm_bwd(dx, gate, o, w, gu, *, name, tm):
    t, d = dx.shape
    k = w.shape[0]
    swiglu = gu is not None

    def body(dx_ref, g_ref, o_ref, w_ref, *rest):
        if swiglu:
            gu_ref, dy_ref, gx_ref, dg_ref = rest
        else:
            dy_ref, gx_ref, dg_ref = rest
        i = pl.program_id(0)
        dxv = dx_ref[...]
        gx = (dxv * g_ref[...]).astype(BF16)
        gx_ref[...] = gx
        part = _colsum(dxv * o_ref[...])

        @pl.when(i == 0)
        def _():
            dg_ref[...] = jnp.zeros_like(dg_ref)

        dg_ref[...] += part
        da = _dot(gx, w_ref[...], 1, 1)
        if swiglu:
            g = gu_ref[:, :k].astype(F32)
            u = gu_ref[:, k:].astype(F32)
            dy_ref[:, :k] = (da * u * _dsilu(g)).astype(BF16)
            dy_ref[:, k:] = (da * _silu(g)).astype(BF16)
        else:
            dy_ref[...] = da

    row = lambda i: (i, 0)
    in_specs = [pl.BlockSpec((tm, d), row), _full((1, d)), pl.BlockSpec((tm, d), row), _resident((k, d))]
    args = [dx, gate, o, w]
    if swiglu:
        in_specs.append(pl.BlockSpec((tm, 2 * k), row))
        args.append(gu)
        dy_shape = jax.ShapeDtypeStruct((t, 2 * k), BF16)
        dy_spec = pl.BlockSpec((tm, 2 * k), row)
    else:
        dy_shape = jax.ShapeDtypeStruct((t, k), F32)
        dy_spec = pl.BlockSpec((tm, k), row)
    return _call(
        body, name=name, grid=(t // tm,), in_specs=in_specs,
        out_specs=[dy_spec, pl.BlockSpec((tm, d), row), _full((1, d))],
        out_shape=[dy_shape, jax.ShapeDtypeStruct((t, d), BF16), jax.ShapeDtypeStruct((1, d), F32)], vmem=BIG_VMEM,
    )(*args)


def norm_mm_bwd(dy, w, x, nw, scale, dres, *, name, tm=512):
    t, n = dy.shape
    d = x.shape[1]
    steps = t // tm

    def body(dy_ref, w_ref, x_ref, nw_ref, sc_ref, dres_ref, dx_ref, dnw_ref, dsc_ref, dsh_ref):
        i = pl.program_id(0)
        dh = _dot(dy_ref[...].astype(BF16), w_ref[...], 1, 1)
        x = x_ref[...]
        r = lax.rsqrt(jnp.mean(x * x, axis=-1, keepdims=True) + NORM_EPS)
        xn = x * r
        a = nw_ref[...] * (1.0 + sc_ref[...])

        @pl.when(i == 0)
        def _():
            dnw_ref[...] = jnp.zeros_like(dnw_ref)
            dsh_ref[...] = jnp.zeros_like(dsh_ref)

        dnw_ref[...] += _colsum(dh * xn)
        dsh_ref[...] += _colsum(dh)
        dxn = dh * a
        dx_ref[...] = r * (dxn - xn * jnp.mean(dxn * xn, axis=-1, keepdims=True)) + dres_ref[...]

        @pl.when(i == steps - 1)
        def _():
            da = dnw_ref[...]
            dsc_ref[...] = da * nw_ref[...]
            dnw_ref[...] = da * (1.0 + sc_ref[...])

    row = lambda i: (i, 0)
    vec = jax.ShapeDtypeStruct((1, d), F32)
    return _call(
        body, name=name, grid=(steps,),
        in_specs=[pl.BlockSpec((tm, n), row), _resident((d, n)), pl.BlockSpec((tm, d), row), _full((1, d)),
                  _full((1, d)), pl.BlockSpec((tm, d), row)],
        out_specs=[pl.BlockSpec((tm, d), row), _full((1, d)), _full((1, d)), _full((1, d))],
        out_shape=[jax.ShapeDtypeStruct((t, d), F32), vec, vec, vec], vmem=BIG_VMEM,
    )(dy, w, x, nw, scale, dres)


def _pick_tn(n, k, budget=6 << 20):
    best = LANE
    for m in range(1, n // LANE + 1):
        tn = m * LANE
        if n % tn == 0 and k * tn * 4 <= budget:
            best = tn
    return best


def mm_tn(a, g, *, name, tt=512):
    t, k = a.shape
    n = g.shape[1]
    tn = _pick_tn(n, k)

    def body(a_ref, g_ref, o_ref):
        @pl.when(pl.program_id(1) == 0)
        def _():
            o_ref[...] = jnp.zeros_like(o_ref)

        o_ref[...] += _dot(a_ref[...].astype(BF16), g_ref[...].astype(BF16), 0, 0)

    return _call(
        body, name=name, grid=(n // tn, t // tt),
        in_specs=[pl.BlockSpec((tt, k), lambda j, i: (i, 0)), pl.BlockSpec((tt, tn), lambda j, i: (i, j))],
        out_specs=pl.BlockSpec((k, tn), lambda j, i: (0, j)),
        out_shape=jax.ShapeDtypeStruct((k, n), F32), semantics=("parallel", "arbitrary"),
    )(a, g)


def loss_head(y, target, *, tm=512):
    t, d = y.shape
    steps = t // tm

    def body(y_ref, t_ref, dy_ref, l_ref, acc_ref):
        i = pl.program_id(0)

        @pl.when(i == 0)
        def _():
            acc_ref[...] = jnp.zeros_like(acc_ref)

        e = y_ref[...] - t_ref[...]
        dy_ref[...] = e * (1.0 / d)
        acc_ref[...] += _colsum(e * e)

        @pl.when(i == steps - 1)
        def _():
            tot = jnp.sum(acc_ref[...], axis=-1, keepdims=True) * (0.5 / d)
            l_ref[...] = jnp.broadcast_to(tot, l_ref.shape)

    row = lambda i: (i, 0)
    return _call(
        body, name="loss_head", grid=(steps,),
        in_specs=[pl.BlockSpec((tm, d), row), pl.BlockSpec((tm, d), row)],
        out_specs=[pl.BlockSpec((tm, d), row), _full((8, LANE))],
        out_shape=[jax.ShapeDtypeStruct((t, d), F32), jax.ShapeDtypeStruct((8, LANE), F32)],
        scratch_shapes=[pltpu.VMEM((1, d), F32)],
    )(y, target)


def _shift_rows(x, s):
    if s == 0:
        return x
    t = x.shape[0]
    ri = _iota2(x.shape, 0)
    rolled = pltpu.roll(x, s % t, axis=0)
    if s > 0:
        return jnp.where(ri >= s, rolled, 0.0)
    return jnp.where(ri < t + s, rolled, 0.0)


def _conv_pre(x, w):
    acc = x * w[DN_CONV - 1:DN_CONV, :]
    for j in range(DN_CONV - 1):
        acc = acc + _shift_rows(x, DN_CONV - 1 - j) * w[j:j + 1, :]
    return acc


def dn_conv(proj, conv_w):
    t = proj.shape[0]
    width = 3 * DN_WIDTH

    def body(x_ref, w_ref, o_ref):
        o_ref[...] = _silu(_conv_pre(x_ref[...], w_ref[...]))

    col = lambda j: (0, j)
    return _call(
        body, name="dn_conv", grid=(width // LANE,),
        in_specs=[pl.BlockSpec((t, LANE), col), pl.BlockSpec((8, LANE), col)],
        out_specs=pl.BlockSpec((t, LANE), col),
        out_shape=jax.ShapeDtypeStruct((t, width), F32), semantics=("parallel",),
    )(proj, conv_w)


def dn_conv_bwd(proj, conv_w, dact, dproj):
    t = proj.shape[0]
    width = 3 * DN_WIDTH

    def body(x_ref, w_ref, d_ref, _, dx_ref, dw_ref):
        x = x_ref[...]
        w = w_ref[...]
        dc = d_ref[...] * _dsilu(_conv_pre(x, w))
        dx = dc * w[DN_CONV - 1:DN_CONV, :]
        rows = []
        for j in range(DN_CONV - 1):
            s = DN_CONV - 1 - j
            dx = dx + _shift_rows(dc, -s) * w[j:j + 1, :]
            rows.append(_colsum(dc * _shift_rows(x, s)))
        rows.append(_colsum(dc * x))
        dx_ref[...] = dx.astype(BF16)
        ri = _iota2((8, LANE), 0)
        dw = jnp.zeros((8, LANE), F32)
        for j in range(DN_CONV):
            dw = dw + jnp.where(ri == j, rows[j], 0.0)
        dw_ref[...] = dw

    col = lambda j: (0, j)
    return _call(
        body, name="dn_conv_bwd", grid=(width // LANE,),
        in_specs=[pl.BlockSpec((t, LANE), col), pl.BlockSpec((8, LANE), col), pl.BlockSpec((t, LANE), col), ANY],
        out_specs=[pl.BlockSpec((t, LANE), col), pl.BlockSpec((8, LANE), col)],
        out_shape=[jax.ShapeDtypeStruct(dproj.shape, dproj.dtype), jax.ShapeDtypeStruct((8, width), F32)],
        semantics=("parallel",), aliases={3: 0},
    )(proj, conv_w, dact, dproj)


def _t(x):
    return jnp.swapaxes(x, -1, -2)


def _inv_unit_lower(a):
    c = a.shape[-1]
    eye = (_iota2((c, c), 0) == _iota2((c, c), 1)).astype(F32)
    x = eye - a
    p = _hdot(a, a)
    steps = int(math.log2(c)) - 1
    for i in range(steps):
        x = x + _hdot(x, p)
        if i < steps - 1:
            p = _hdot(p, p)
    return x


def _dn_local(q, k, v, a, b, alog, dtb, tinv=None):
    nh, c, d = q.shape
    rq = lax.rsqrt(_rowsum(q * q) + NORM_EPS)
    rk = lax.rsqrt(_rowsum(k * k) + NORM_EPS)
    qh = q * rq
    kn = k * rk
    qs = qh * (d ** -0.5)
    g = -jnp.exp(alog) * _softplus(a + dtb)
    beta = _sigmoid(b)
    ri = _iota2((c, c), 0)
    ci = _iota2((c, c), 1)
    causal = ri >= ci
    strict = ri > ci
    gb = jnp.broadcast_to(g, (nh, c, d))
    gcb = _xdot(jnp.broadcast_to(causal.astype(F32), (nh, c, c)), gb, exact=0)
    gc = gcb[..., :1]
    gl = _colsum(gb)[..., :1]
    dec = jnp.exp(jnp.where(causal, gc - _t(gcb)[:, :c, :], -1e30))
    kb = kn * beta
    amat = jnp.where(strict, _bdot(kb, kn, 1, 1) * dec, 0.0)
    if tinv is None:
        tinv = _inv_unit_lower(amat)
    e = jnp.exp(gc)
    f = jnp.exp(gl - gc)
    rw = kb * e
    sol = _hdot(tinv, jnp.concatenate([v * beta, rw], axis=-1))
    pmat = jnp.where(causal, _bdot(qs, kn, 1, 1) * dec, 0.0)
    return dict(rq=rq, rk=rk, qh=qh, kn=kn, qs=qs, g=g, beta=beta, causal=causal, strict=strict, gl=gl,
                dec=dec, kb=kb, amat=amat, tinv=tinv, e=e, f=f, rw=rw, u=sol[..., :d], w=sol[..., d:], pmat=pmat,
                qd=qs * e, kd=kn * f)


_DN_FIELDS = ("u", "w", "qd", "kd", "pmat", "gl")


def _dn_state(m, s_in):
    vnew = m["u"] - _bdot(m["w"], s_in)
    o = _bdot(m["qd"], s_in) + _bdot(m["pmat"], vnew)
    return vnew, o, s_in * jnp.exp(m["gl"]) + _bdot(m["kd"], vnew, 0, 0)


def _dn_state_bwd(m, s_in, do, ds_out):
    el = jnp.exp(m["gl"])
    dvnew = _bdot(m["pmat"], do, 0, 0) + _bdot(m["kd"], ds_out)
    dkd = _bdot(m["vnew"], ds_out, 1, 1)
    ds_in = _bdot(m["qd"], do, 0, 0) + el * ds_out - _bdot(m["w"], dvnew, 0, 0)
    dgl = el * _colsum(_rowsum(s_in * ds_out))
    return dvnew, dkd, dgl, ds_in


def _dn_local_bwd(m, q, v, a, alog, dtb, s_in, vnew, do, dvnew, dkd, dgl):
    nh, c, d = q.shape
    kn, qs, kb, u, w, e, f = m["kn"], m["qs"], m["kb"], m["u"], m["w"], m["e"], m["f"]
    beta, dec, tinv, kd, qd = m["beta"], m["dec"], m["tinv"], m["kd"], m["qd"]
    dp = jnp.where(m["causal"], _bdot(do, vnew, 1, 1), 0.0)
    dqd = _bdot(do, s_in, 1, 1)
    dw = -_bdot(dvnew, s_in, 1, 1)
    dsol = _hdot(tinv, jnp.concatenate([dvnew, dw], axis=-1), 0, 0)
    dru = dsol[..., :d]
    drw = dsol[..., d:]
    da_m = -jnp.where(m["strict"], _bdot(dsol, jnp.concatenate([u, w], axis=-1), 1, 1), 0.0)
    db_m = da_m * dec
    dq_m = dp * dec
    dkb = _bdot(db_m, kn)
    dkn = _bdot(db_m, kb, 0, 0) + _bdot(dq_m, qs, 0, 0)
    dqs = _bdot(dq_m, kn)
    gmat = da_m * m["amat"] + dp * m["pmat"]
    ones = jnp.ones((nh, c, d), F32)
    dgam = (_xdot(gmat, ones) - _xdot(gmat, ones, 0, 0))[..., :1]
    dqs = dqs + dqd * e
    dgam = dgam + _rowsum(dqd * qd)
    dkn = dkn + dkd * f
    tk = _rowsum(dkd * kd)
    dgam = dgam - tk
    dgl = dgl + _colsum(tk)
    dkb = dkb + drw * e
    dgam = dgam + _rowsum(drw * m["rw"])
    dv = dru * beta
    dbeta = _rowsum(dru * v) + _rowsum(dkb * kn)
    dkn = dkn + dkb * beta
    last = (_iota2((c, 1), 0) == c - 1).astype(F32)
    dgam = dgam + last * dgl
    upper = (_iota2((c, c), 0) <= _iota2((c, c), 1)).astype(F32)
    dg = _xdot(jnp.broadcast_to(upper, (nh, c, c)), jnp.broadcast_to(dgam, (nh, c, d)), exact=0)[..., :1]
    dqh = dqs * (d ** -0.5)
    dq = m["rq"] * (dqh - m["qh"] * _rowsum(dqh * m["qh"]))
    dk = m["rk"] * (dkn - kn * _rowsum(dkn * kn))
    sg = _sigmoid(a + dtb)
    da = dg * (-jnp.exp(alog)) * sg
    dalog = _colsum(dg * m["g"])
    ddtb = _colsum(da)
    db = dbeta * beta * (1.0 - beta)
    return dq, dk, dv, da, db, dalog, ddtb


def _dn_gate(o, z, wn):
    ro = lax.rsqrt(jnp.mean(o * o, axis=-1, keepdims=True) + NORM_EPS)
    n = o * ro
    return n, ro, n * wn * _silu(z)


DN_PAIR = 4


def _heads(ref, col0):
    d = DN_HEAD_DIM
    return jnp.stack([ref[j * DN_CHUNK:(j + 1) * DN_CHUNK, col0 + h * d:col0 + (h + 1) * d]
                      for j in range(DN_PAIR) for h in range(DN_HEADS)])


def _dn_inputs(act_ref, ab_ref, sc_ref):
    ab = ab_ref[...]
    sc = sc_ref[...]
    rows = lambda j: slice(j * DN_CHUNK, (j + 1) * DN_CHUNK)
    both = [(j, h) for j in range(DN_PAIR) for h in range(DN_HEADS)]
    q = _heads(act_ref, 0)
    k = _heads(act_ref, DN_WIDTH)
    v = _heads(act_ref, 2 * DN_WIDTH)
    a = jnp.stack([ab[rows(j), h:h + 1] for j, h in both])
    b = jnp.stack([ab[rows(j), DN_HEADS + h:DN_HEADS + h + 1] for j, h in both])
    alog = jnp.stack([sc[0:1, h:h + 1] for _, h in both])
    dtb = jnp.stack([sc[1:2, h:h + 1] for _, h in both])
    return q, k, v, a, b, alog, dtb


def _chunk_of(m, j, fields):
    return {f: m[f][j * DN_HEADS:(j + 1) * DN_HEADS] for f in fields}


def dn_fwd(act, proj, scal, wn):
    t = act.shape[0]
    n = t // DN_CHUNK
    d = DN_HEAD_DIM
    rows = DN_PAIR * DN_CHUNK

    def body(act_ref, z_ref, ab_ref, sc_ref, wn_ref, y_ref, st_ref, ti_ref, s_ref):
        @pl.when(pl.program_id(0) == 0)
        def _():
            s_ref[...] = jnp.zeros_like(s_ref)

        m = _dn_local(*_dn_inputs(act_ref, ab_ref, sc_ref))
        s = s_ref[...]
        outs = []
        for j in range(DN_PAIR):
            st_ref[j] = s
            ti_ref[j] = m["tinv"][j * DN_HEADS:(j + 1) * DN_HEADS]
            _, o, s = _dn_state(_chunk_of(m, j, _DN_FIELDS), s)
            outs.append(o)
        s_ref[...] = s
        y = _dn_gate(jnp.concatenate(outs, axis=0), _heads(z_ref, 0), wn_ref[...])[2]
        for j in range(DN_PAIR):
            for h in range(DN_HEADS):
                y_ref[j * DN_CHUNK:(j + 1) * DN_CHUNK, h * d:(h + 1) * d] = y[j * DN_HEADS + h]

    return _call(
        body, name="dn_fwd", grid=(n // DN_PAIR,),
        in_specs=[pl.BlockSpec((rows, 3 * DN_WIDTH), lambda i: (i, 0)),
                  pl.BlockSpec((rows, DN_WIDTH), lambda i: (i, C_Z // DN_WIDTH)),
                  pl.BlockSpec((rows, LANE), lambda i: (i, C_AB // LANE)),
                  _full((8, LANE)), _full((1, d))],
        out_specs=[pl.BlockSpec((rows, DN_WIDTH), lambda i: (i, 0)),
                   pl.BlockSpec((DN_PAIR, DN_HEADS, d, d), lambda i: (i, 0, 0, 0)),
                   pl.BlockSpec((DN_PAIR, DN_HEADS, DN_CHUNK, DN_CHUNK), lambda i: (i, 0, 0, 0))],
        out_shape=[jax.ShapeDtypeStruct((t, MIX_WIDTH), F32), jax.ShapeDtypeStruct((n, DN_HEADS, d, d), F32),
                   jax.ShapeDtypeStruct((n, DN_HEADS, DN_CHUNK, DN_CHUNK), F32)],
        scratch_shapes=[pltpu.VMEM((DN_HEADS, d, d), F32)],
    )(act, proj, proj, scal, wn)


def dn_bwd(act, proj, scal, wn, states, tinvs, dy):
    t = act.shape[0]
    n = t // DN_CHUNK
    steps = n // DN_PAIR
    d = DN_HEAD_DIM
    zab = DN_WIDTH + AB_PAD
    rows = DN_PAIR * DN_CHUNK

    def body(act_ref, z_ref, ab_ref, sc_ref, wn_ref, st_ref, ti_ref, dy_ref, dact_ref, dzab_ref, dpar_ref, ds_ref):
        @pl.when(pl.program_id(0) == 0)
        def _():
            ds_ref[...] = jnp.zeros_like(ds_ref)
            dpar_ref[...] = jnp.zeros_like(dpar_ref)

        wnv = wn_ref[...]
        q, k, v, a, b, alog, dtb = _dn_inputs(act_ref, ab_ref, sc_ref)
        batch = (DN_PAIR * DN_HEADS,)
        s_in = st_ref[...].reshape(batch + (d, d))
        m = _dn_local(q, k, v, a, b, alog, dtb, ti_ref[...].reshape(batch + (DN_CHUNK, DN_CHUNK)))
        vnew, o, _ = _dn_state(m, s_in)
        z = _heads(z_ref, 0)
        dyh = _heads(dy_ref, 0)
        nrm, ro, _ = _dn_gate(o, z, wnv)
        sz = _silu(z)
        dz = dyh * nrm * wnv * _dsilu(z)
        dn = dyh * wnv * sz
        dwn = _colsum(dyh * nrm * sz)
        do = ro * (dn - nrm * jnp.mean(dn * nrm, axis=-1, keepdims=True))
        ds = ds_ref[...]
        parts = [None] * DN_PAIR
        for j in reversed(range(DN_PAIR)):
            mj = dict(_chunk_of(m, j, _DN_FIELDS), vnew=vnew[j * DN_HEADS:(j + 1) * DN_HEADS])
            dvnew, dkd, dgl, ds = _dn_state_bwd(mj, s_in[j * DN_HEADS:(j + 1) * DN_HEADS],
                                                do[j * DN_HEADS:(j + 1) * DN_HEADS], ds)
            parts[j] = (dvnew, dkd, dgl)
        ds_ref[...] = ds
        dvnew, dkd, dgl = (jnp.concatenate([p[i] for p in parts], axis=0) for i in range(3))
        dq, dk, dv, da, db, dalog, ddtb = _dn_local_bwd(m, q, v, a, alog, dtb, s_in, vnew, do, dvnew, dkd, dgl)
        lane = _iota2((DN_CHUNK, LANE), 1)
        prow = _iota2((8, LANE), 0)
        plane = _iota2((8, LANE), 1)
        dpar = jnp.zeros((8, LANE), F32)
        for j in range(DN_PAIR):
            rs = slice(j * DN_CHUNK, (j + 1) * DN_CHUNK)
            dab = jnp.zeros((DN_CHUNK, LANE), F32)
            for h in range(DN_HEADS):
                n_ = j * DN_HEADS + h
                dzab_ref[rs, h * d:(h + 1) * d] = dz[n_].astype(BF16)
                dact_ref[rs, h * d:(h + 1) * d] = dq[n_]
                dact_ref[rs, DN_WIDTH + h * d:DN_WIDTH + (h + 1) * d] = dk[n_]
                dact_ref[rs, 2 * DN_WIDTH + h * d:2 * DN_WIDTH + (h + 1) * d] = dv[n_]
                dab = dab + jnp.where(lane == h, da[n_], 0.0) + jnp.where(lane == DN_HEADS + h, db[n_], 0.0)
                dpar = dpar + jnp.where((prow == 0) & (plane == h), dalog[n_], 0.0)
                dpar = dpar + jnp.where((prow == 1) & (plane == h), ddtb[n_], 0.0)
                dpar = dpar + jnp.where(prow == 2, dwn[n_], 0.0)
            dzab_ref[rs, DN_WIDTH:DN_WIDTH + LANE] = dab.astype(BF16)
            dzab_ref[rs, DN_WIDTH + LANE:] = jnp.zeros((DN_CHUNK, AB_PAD - LANE), BF16)
        dpar_ref[...] += dpar

    rev = lambda i: (steps - 1 - i, 0)
    rev4 = lambda i: (steps - 1 - i, 0, 0, 0)
    return _call(
        body, name="dn_bwd", grid=(steps,),
        in_specs=[pl.BlockSpec((rows, 3 * DN_WIDTH), rev),
                  pl.BlockSpec((rows, DN_WIDTH), lambda i: (steps - 1 - i, C_Z // DN_WIDTH)),
                  pl.BlockSpec((rows, LANE), lambda i: (steps - 1 - i, C_AB // LANE)),
                  _full((8, LANE)), _full((1, d)),
                  pl.BlockSpec((DN_PAIR, DN_HEADS, d, d), rev4),
                  pl.BlockSpec((DN_PAIR, DN_HEADS, DN_CHUNK, DN_CHUNK), rev4),
                  pl.BlockSpec((rows, DN_WIDTH), rev)],
        out_specs=[pl.BlockSpec((rows, 3 * DN_WIDTH), rev),
                   pl.BlockSpec((rows, zab), lambda i: (steps - 1 - i, C_Z // zab)), _full((8, LANE))],
        out_shape=[jax.ShapeDtypeStruct((t, 3 * DN_WIDTH), F32), jax.ShapeDtypeStruct((t, IN_PAD), BF16),
                   jax.ShapeDtypeStruct((8, LANE), F32)],
        scratch_shapes=[pltpu.VMEM((DN_HEADS, d, d), F32)],
    )(act, proj, proj, scal, wn, states, tinvs, dy)


_INV_SQRT2 = 0.7071067811865476
_INV_SQRT2PI = 0.3989422804014327


def _gelu(x):
    return 0.5 * x * (1.0 + lax.erf(x * _INV_SQRT2))


def _dgelu(x):
    return 0.5 * (1.0 + lax.erf(x * _INV_SQRT2)) + x * jnp.exp(-0.5 * x * x) * _INV_SQRT2PI


def _gm_core(uv, lng, lnb, ws_ref, bst):
    c = uv.shape[0]
    zz = _gelu(uv)
    u = zz[:, :GM_WIDTH]
    vv = zz[:, GM_WIDTH:]
    xc = vv - jnp.mean(vv, axis=-1, keepdims=True)
    rs = lax.rsqrt(jnp.mean(xc * xc, axis=-1, keepdims=True) + NORM_EPS)
    xh = xc * rs
    vn = xh * lng + lnb
    grp = _iota2((c, GM_WIDTH), 1) // GM_GROUP_DIM
    tril = _iota2((c, c), 0) >= _iota2((c, c), 1)
    sv = jnp.zeros((c, GM_WIDTH), F32)
    masks = []
    for g in range(GM_GROUPS):
        mk = grp == g
        masks.append(mk)
        ws = jnp.where(tril, ws_ref[g], 0.0)
        sv = sv + _bdot(ws, jnp.where(mk, vn, 0.0)) + jnp.where(mk, bst[:, g:g + 1], 0.0)
    return u, xh, rs, vn, sv, masks, tril


def gm_fwd(proj, lng, lnb, w_s, bst, ybuf):
    t = proj.shape[0]

    def body(uv_ref, g_ref, b_ref, ws_ref, bst_ref, _, y_ref):
        u, _, _, _, sv, _, _ = _gm_core(uv_ref[...], g_ref[...], b_ref[...], ws_ref, bst_ref[...])
        y_ref[...] = u * sv

    return _call(
        body, name="gm_fwd", grid=(t // GM_CHUNK,),
        in_specs=[pl.BlockSpec((GM_CHUNK, 2 * GM_WIDTH), lambda i: (i, C_UV // (2 * GM_WIDTH))),
                  _full((1, GM_WIDTH)), _full((1, GM_WIDTH)), _full((GM_GROUPS, GM_CHUNK, GM_CHUNK)),
                  _full((GM_CHUNK, LANE)), ANY],
        out_specs=pl.BlockSpec((GM_CHUNK, GM_WIDTH), lambda i: (i, DN_WIDTH // GM_WIDTH)),
        out_shape=jax.ShapeDtypeStruct(ybuf.shape, F32), semantics=("parallel",), aliases={5: 0},
    )(proj, lng, lnb, w_s, bst, ybuf)


def gm_bwd(proj, lng, lnb, w_s, bst, dy, dproj):
    t = proj.shape[0]

    def body(uv_ref, g_ref, b_ref, ws_ref, bst_ref, dy_ref, _, duv_ref, dws_ref, dbst_ref, dln_ref):
        @pl.when(pl.program_id(0) == 0)
        def _():
            dws_ref[...] = jnp.zeros_like(dws_ref)
            dbst_ref[...] = jnp.zeros_like(dbst_ref)
            dln_ref[...] = jnp.zeros_like(dln_ref)

        uv = uv_ref[...]
        lng = g_ref[...]
        u, xh, rs, vn, sv, masks, tril = _gm_core(uv, lng, b_ref[...], ws_ref, bst_ref[...])
        dyv = dy_ref[...]
        dsv = dyv * u
        lane = _iota2((GM_CHUNK, LANE), 1)
        dvn = jnp.zeros_like(dsv)
        dbst = jnp.zeros((GM_CHUNK, LANE), F32)
        for g in range(GM_GROUPS):
            ws = jnp.where(tril, ws_ref[g], 0.0)
            dsg = jnp.where(masks[g], dsv, 0.0)
            dvn = dvn + jnp.where(masks[g], _bdot(ws, dsv, 0, 0), 0.0)
            dws_ref[g] += jnp.where(tril, _bdot(dsg, vn, 1, 1), 0.0)
            dbst = dbst + jnp.where(lane == g, _rowsum(dsg), 0.0)
        dbst_ref[...] += dbst
        row = _iota2((8, GM_WIDTH), 0)
        dln_ref[...] += jnp.where(row == 0, _colsum(dvn * xh), 0.0) + jnp.where(row == 1, _colsum(dvn), 0.0)
        dxh = dvn * lng
        dvv = rs * (dxh - jnp.mean(dxh, axis=-1, keepdims=True) - xh * jnp.mean(dxh * xh, axis=-1, keepdims=True))
        dg = _dgelu(uv)
        duv_ref[:, :GM_WIDTH] = (dyv * sv * dg[:, :GM_WIDTH]).astype(BF16)
        duv_ref[:, GM_WIDTH:] = (dvv * dg[:, GM_WIDTH:]).astype(BF16)

    return _call(
        body, name="gm_bwd", grid=(t // GM_CHUNK,),
        in_specs=[pl.BlockSpec((GM_CHUNK, 2 * GM_WIDTH), lambda i: (i, C_UV // (2 * GM_WIDTH))),
                  _full((1, GM_WIDTH)), _full((1, GM_WIDTH)), _full((GM_GROUPS, GM_CHUNK, GM_CHUNK)),
                  _full((GM_CHUNK, LANE)),
                  pl.BlockSpec((GM_CHUNK, GM_WIDTH), lambda i: (i, DN_WIDTH // GM_WIDTH)), ANY],
        out_specs=[pl.BlockSpec((GM_CHUNK, 2 * GM_WIDTH), lambda i: (i, C_UV // (2 * GM_WIDTH))),
                   _full((GM_GROUPS, GM_CHUNK, GM_CHUNK)), _full((GM_CHUNK, LANE)), _full((8, GM_WIDTH))],
        out_shape=[jax.ShapeDtypeStruct(dproj.shape, dproj.dtype),
                   jax.ShapeDtypeStruct((GM_GROUPS, GM_CHUNK, GM_CHUNK), F32),
                   jax.ShapeDtypeStruct((GM_CHUNK, LANE), F32), jax.ShapeDtypeStruct((8, GM_WIDTH), F32)],
        aliases={6: 0},
    )(proj, lng, lnb, w_s, bst, dy, dproj)


def _head_mats():
    r = _iota2((SW_WIDTH, SW_WIDTH), 0)
    c = _iota2((SW_WIDTH, SW_WIDTH), 1)
    same = (r // SW_HEAD_DIM) == (c // SW_HEAD_DIM)
    cc = c % SW_HEAD_DIM
    half = ROPE_DIM // 2
    rot = jnp.where((cc < half) & (r == c + half), -1.0, 0.0) + jnp.where((cc >= half) & (cc < ROPE_DIM) & (r == c - half), 1.0, 0.0)
    return same.astype(F32), rot


def _seg_col(s):
    return C_SW // SW_WIDTH + (s // 2) * 3 + s % 2


def _halves(x):
    return x[:, :LANE], x[:, LANE:]


def sw_prep(proj, nw2, cos_t, sin_t, *, tm=512):
    t = proj.shape[0]

    def body(x_ref, w_ref, c_ref, s_ref, o_ref):
        same, rot = _head_mats()
        x = x_ref[...]
        r = lax.rsqrt(_xdot(x * x, same) * (1.0 / SW_HEAD_DIM) + NORM_EPS)
        xn = x * r * w_ref[0]
        o_ref[0, 0], o_ref[0, 1] = _halves(xn * c_ref[...] + _xdot(xn, rot) * s_ref[...])

    return _call(
        body, name="sw_prep", grid=(6, t // tm),
        in_specs=[pl.BlockSpec((tm, SW_WIDTH), lambda s, i: (i, _seg_col(s))),
                  pl.BlockSpec((1, 1, SW_WIDTH), lambda s, i: (s % 2, 0, 0)),
                  pl.BlockSpec((tm, SW_WIDTH), lambda s, i: (i, 0)),
                  pl.BlockSpec((tm, SW_WIDTH), lambda s, i: (i, 0))],
        out_specs=pl.BlockSpec((1, 2, tm, LANE), lambda s, i: (s, 0, i, 0)),
        out_shape=jax.ShapeDtypeStruct((6, 2, t, LANE), F32), semantics=("parallel", "parallel"),
    )(proj, nw2, cos_t, sin_t)


def sw_prep_bwd(proj, nw2, cos_t, sin_t, dkvq, dproj, dnw, p, *, tm=512):
    t = proj.shape[0]
    col0 = C_SW // SW_WIDTH + 3 * p
    seg_col = lambda s: col0 + (s + 1) % 3

    def body(x_ref, w_ref, c_ref, s_ref, d_ref, _, dw0_ref, dx_ref, dw_ref):
        s = pl.program_id(0)
        dout = jnp.concatenate([d_ref[0, 0], d_ref[0, 1]], axis=1)

        @pl.when(s == 1)
        def _():
            dx_ref[...] = dout.astype(BF16)

        @pl.when((s != 1) & (pl.program_id(1) == 0))
        def _():
            dw_ref[...] = dw0_ref[...]

        @pl.when(s != 1)
        def _():
            same, rot = _head_mats()
            x = x_ref[...]
            w = w_ref[0]
            r = lax.rsqrt(_xdot(x * x, same) * (1.0 / SW_HEAD_DIM) + NORM_EPS)
            xh = x * r
            dxn = dout * c_ref[...] + _xdot(dout * s_ref[...], rot, 1, 1)
            dw_ref[0] += _colsum(dxn * xh)
            dxh = dxn * w
            dx_ref[...] = (r * (dxh - xh * (_xdot(dxh * xh, same) * (1.0 / SW_HEAD_DIM)))).astype(BF16)

    return _call(
        body, name=f"sw_prep_bwd{p}", grid=(3, t // tm),
        in_specs=[pl.BlockSpec((tm, SW_WIDTH), lambda s, i: (i, seg_col(s))),
                  pl.BlockSpec((1, 1, SW_WIDTH), lambda s, i: (1 - s // 2, 0, 0)),
                  pl.BlockSpec((tm, SW_WIDTH), lambda s, i: (i, 0)),
                  pl.BlockSpec((tm, SW_WIDTH), lambda s, i: (i, 0)),
                  pl.BlockSpec((1, 2, tm, LANE), lambda s, i: (s, 0, i, 0)), ANY,
                  pl.BlockSpec((1, 1, SW_WIDTH), lambda s, i: (s // 2, 0, 0))],
        out_specs=[pl.BlockSpec((tm, SW_WIDTH), lambda s, i: (i, seg_col(s))),
                   pl.BlockSpec((1, 1, SW_WIDTH), lambda s, i: (s // 2, 0, 0))],
        out_shape=[jax.ShapeDtypeStruct(dproj.shape, dproj.dtype), jax.ShapeDtypeStruct((2, 1, SW_WIDTH), F32)],
        semantics=("arbitrary", "arbitrary"), aliases={5: 0},
    )(proj, nw2, cos_t, sin_t, dkvq, dproj, dnw)


_SW_SCALE = SW_HEAD_DIM ** -0.5
_NEG = -1e30


def _sw_masks(has_other):
    ri = _iota2((SW_BLOCK, SW_BLOCK), 0)
    ci = _iota2((SW_BLOCK, SW_BLOCK), 1)
    return ri >= ci, (ci >= ri) & has_other


def _pair(x):
    first = _iota2((1, LANE), 1) < SW_HEAD_DIM
    return jnp.stack([jnp.where(first, x, 0.0), jnp.where(first, 0.0, x)])


def _both(x):
    return jnp.broadcast_to(x.astype(BF16)[None], (2,) + x.shape)


def _unpair(x2):
    first = _iota2((1, LANE), 1) < SW_HEAD_DIM
    return jnp.where(first, x2[0], x2[1])


def _head_cols(x):
    return jnp.stack([x[:, 0:1], x[:, SW_HEAD_DIM:SW_HEAD_DIM + 1]])


SW_GROUP = 8


def _sw_geometry(t, p):
    dil = SW_DILATIONS[p]
    unit = SW_BLOCK * dil
    nb = max(1, SW_GROUP // dil)
    return dil, unit, nb, t // (unit * nb)


def _sw_groups(dil, nb, body):
    if nb * dil == SW_GROUP:
        body([(k // dil, k % dil) for k in range(SW_GROUP)])
    else:
        def step(g, carry):
            body([(0, SW_GROUP * g + k) for k in range(SW_GROUP)])
            return carry

        lax.fori_loop(0, nb * dil // SW_GROUP, step, 0)


def _sw_rows(i, r, dil):
    start = i * SW_BLOCK * dil + r
    return pl.ds(start, SW_BLOCK) if dil == 1 else pl.ds(start, SW_BLOCK, stride=dil)


def _sw_load(refs, probs, dil, shift, wrap, fn):
    out = []
    for i, r in probs:
        if shift != 0 and i == wrap:
            out.append(fn(refs[1][_sw_rows(0, r, dil), :]))
        else:
            out.append(fn(refs[0][_sw_rows(i + shift, r, dil), :]))
    return jnp.concatenate(out, axis=0)


def _sw_other_masks(probs, wrap, edge_ok):
    _, other = _sw_masks(edge_ok)
    _, always = _sw_masks(True)
    return jnp.stack([other if i == wrap else always for i, _ in probs for _ in range(2)])


def sw_attn(qk, proj, p):
    t = proj.shape[0]
    dil, unit, nb, nsp = _sw_geometry(t, p)
    vcol = (C_SW + 3 * SW_WIDTH * p + 2 * SW_WIDTH) // LANE

    def body(q_ref, kc_ref, kp_ref, vc_ref, vp_ref, o_ref, l_ref):
        mc, _ = _sw_masks(True)
        first = pl.program_id(1) != 0
        q_r, k_r, v_r = (q_ref.at[0, 0], None), (kc_ref.at[0, 0], kp_ref.at[0, 0]), (vc_ref, vp_ref)

        def one(probs):
            mp = _sw_other_masks(probs, 0, first)
            q2 = _sw_load(q_r, probs, dil, 0, 0, _pair)
            sc = jnp.where(mc, _bdot(q2, _sw_load(k_r, probs, dil, 0, 0, _both), 1, 1) * _SW_SCALE, _NEG)
            sp = jnp.where(mp, _bdot(q2, _sw_load(k_r, probs, dil, -1, 0, _both), 1, 1) * _SW_SCALE, _NEG)
            mx = jnp.maximum(jnp.max(sc, axis=-1, keepdims=True), jnp.max(sp, axis=-1, keepdims=True))
            pc = jnp.exp(sc - mx)
            pp = jnp.exp(sp - mx)
            den = _rowsum(pc) + _rowsum(pp)
            o2 = (_bdot(pc, _sw_load(v_r, probs, dil, 0, 0, _both))
                  + _bdot(pp, _sw_load(v_r, probs, dil, -1, 0, _both))) * (1.0 / den)
            l2 = jnp.broadcast_to(mx + jnp.log(den), o2.shape)
            for n, (i, r) in enumerate(probs):
                o_ref.at[0][_sw_rows(i, r, dil), :] = _unpair(o2[2 * n:2 * n + 2])
                l_ref.at[0][_sw_rows(i, r, dil), :] = _unpair(l2[2 * n:2 * n + 2])

        _sw_groups(dil, nb, one)

    before = lambda j: jnp.maximum(j * nb - 1, 0)
    seg = lambda s: pl.BlockSpec((1, 1, unit * nb, LANE), lambda h, j: (s, h, j, 0))
    seg_b = lambda s: pl.BlockSpec((1, 1, unit, LANE), lambda h, j: (s, h, before(j), 0))
    out = pl.BlockSpec((1, unit * nb, LANE), lambda h, j: (h, j, 0))
    shp = jax.ShapeDtypeStruct((2, t, LANE), F32)
    return _call(
        body, name=f"sw_attn{p}", grid=(2, nsp),
        in_specs=[seg(2 * p), seg(2 * p + 1), seg_b(2 * p + 1),
                  pl.BlockSpec((unit * nb, LANE), lambda h, j: (j, vcol + h)),
                  pl.BlockSpec((unit, LANE), lambda h, j: (before(j), vcol + h))],
        out_specs=[out, out], out_shape=[shp, shp], semantics=("parallel", "parallel"),
    )(qk, qk, qk, proj, proj)


def sw_attn_dkv(qk, proj, dy, lg, dm, p):
    t = proj.shape[0]
    dil, unit, nb, nsp = _sw_geometry(t, p)
    nunits = t // unit
    vcol = (C_SW + 3 * SW_WIDTH * p + 2 * SW_WIDTH) // LANE
    ycol = (DN_WIDTH + GM_WIDTH) // LANE

    def body(k_ref, v_ref, qc_ref, qn_ref, doc_ref, don_ref, lc_ref, ln_ref, dc_ref, dn_ref, o_ref):
        mc, _ = _sw_masks(True)
        more = pl.program_id(1) + 1 < nsp
        q_r, do_r = (qc_ref.at[0, 0], qn_ref.at[0, 0]), (doc_ref, don_ref)
        l_r, d_r = (lc_ref.at[0], ln_ref.at[0]), (dc_ref.at[0], dn_ref.at[0])

        def one(probs):
            k2 = _sw_load((k_ref.at[0, 0], None), probs, dil, 0, 0, _both)
            v2 = _sw_load((v_ref, None), probs, dil, 0, 0, _both)
            dk = jnp.zeros((2 * SW_GROUP, SW_BLOCK, LANE), F32)
            dv = jnp.zeros((2 * SW_GROUP, SW_BLOCK, LANE), F32)
            for shift, mk in ((0, mc), (1, _sw_other_masks(probs, nb - 1, more))):
                q2 = _sw_load(q_r, probs, dil, shift, nb - 1, _pair)
                do2 = _sw_load(do_r, probs, dil, shift, nb - 1, _pair)
                lse = _sw_load(l_r, probs, dil, shift, nb - 1, _head_cols)
                dd = _sw_load(d_r, probs, dil, shift, nb - 1, _head_cols)
                pr = jnp.exp(jnp.where(mk, _bdot(q2, k2, 1, 1) * _SW_SCALE, _NEG) - lse)
                dv = dv + _bdot(pr, do2, 0, 0)
                ds = pr * (_bdot(do2, v2, 1, 1) - dd)
                dk = dk + _bdot(ds, q2, 0, 0)
            for n, (i, r) in enumerate(probs):
                o_ref.at[0, 0][_sw_rows(i, r, dil), :] = (dk[2 * n] + dk[2 * n + 1]) * _SW_SCALE
                o_ref.at[1, 0][_sw_rows(i, r, dil), :] = dv[2 * n] + dv[2 * n + 1]

        _sw_groups(dil, nb, one)

    after = lambda j: jnp.minimum((j + 1) * nb, nunits - 1)
    seg = lambda s: pl.BlockSpec((1, 1, unit * nb, LANE), lambda h, j: (s, h, j, 0))
    seg_a = lambda s: pl.BlockSpec((1, 1, unit, LANE), lambda h, j: (s, h, after(j), 0))
    col = lambda c0: pl.BlockSpec((unit * nb, LANE), lambda h, j: (j, c0 + h))
    col_a = lambda c0: pl.BlockSpec((unit, LANE), lambda h, j: (after(j), c0 + h))
    hp = pl.BlockSpec((1, unit * nb, LANE), lambda h, j: (h, j, 0))
    hp_a = pl.BlockSpec((1, unit, LANE), lambda h, j: (h, after(j), 0))
    return _call(
        body, name=f"sw_dkv{p}", grid=(2, nsp),
        in_specs=[seg(2 * p + 1), col(vcol), seg(2 * p), seg_a(2 * p), col(ycol), col_a(ycol), hp, hp_a, hp, hp_a],
        out_specs=pl.BlockSpec((2, 1, unit * nb, LANE), lambda h, j: (0, h, j, 0)),
        out_shape=jax.ShapeDtypeStruct((3, 2, t, LANE), F32), semantics=("parallel", "parallel"),
    )(qk, proj, qk, qk, dy, dy, lg, lg, dm, dm)


def sw_attn_dq(qk, proj, dy, lg, dm, dkvq, p):
    t = proj.shape[0]
    dil, unit, nb, nsp = _sw_geometry(t, p)
    vcol = (C_SW + 3 * SW_WIDTH * p + 2 * SW_WIDTH) // LANE
    ycol = (DN_WIDTH + GM_WIDTH) // LANE

    def body(q_ref, kc_ref, kp_ref, vc_ref, vp_ref, do_ref, l_ref, d_ref, _, dq_ref):
        mc, _ = _sw_masks(True)
        first = pl.program_id(1) != 0
        k_r, v_r = (kc_ref.at[0, 0], kp_ref.at[0, 0]), (vc_ref, vp_ref)

        def one(probs):
            mp = _sw_other_masks(probs, 0, first)
            q2 = _sw_load((q_ref.at[0, 0], None), probs, dil, 0, 0, _pair)
            do2 = _sw_load((do_ref, None), probs, dil, 0, 0, _pair)
            lse = _sw_load((l_ref.at[0], None), probs, dil, 0, 0, _head_cols)
            dd = _sw_load((d_ref.at[0], None), probs, dil, 0, 0, _head_cols)
            kc = _sw_load(k_r, probs, dil, 0, 0, _both)
            kp = _sw_load(k_r, probs, dil, -1, 0, _both)
            pc = jnp.exp(jnp.where(mc, _bdot(q2, kc, 1, 1) * _SW_SCALE, _NEG) - lse)
            pp = jnp.exp(jnp.where(mp, _bdot(q2, kp, 1, 1) * _SW_SCALE, _NEG) - lse)
            dsc = pc * (_bdot(do2, _sw_load(v_r, probs, dil, 0, 0, _both), 1, 1) - dd)
            dsp = pp * (_bdot(do2, _sw_load(v_r, probs, dil, -1, 0, _both), 1, 1) - dd)
            dq2 = (_bdot(dsc, kc) + _bdot(dsp, kp)) * _SW_SCALE
            for n, (i, r) in enumerate(probs):
                dq_ref.at[0, 0][_sw_rows(i, r, dil), :] = _unpair(dq2[2 * n:2 * n + 2])

        _sw_groups(dil, nb, one)

    before = lambda j: jnp.maximum(j * nb - 1, 0)
    seg = lambda s: pl.BlockSpec((1, 1, unit * nb, LANE), lambda h, j: (s, h, j, 0))
    seg_b = lambda s: pl.BlockSpec((1, 1, unit, LANE), lambda h, j: (s, h, before(j), 0))
    col = lambda c0: pl.BlockSpec((unit * nb, LANE), lambda h, j: (j, c0 + h))
    col_b = lambda c0: pl.BlockSpec((unit, LANE), lambda h, j: (before(j), c0 + h))
    hp = pl.BlockSpec((1, unit * nb, LANE), lambda h, j: (h, j, 0))
    return _call(
        body, name=f"sw_dq{p}", grid=(2, nsp),
        in_specs=[seg(2 * p), seg(2 * p + 1), seg_b(2 * p + 1), col(vcol), col_b(vcol), col(ycol), hp, hp, ANY],
        out_specs=pl.BlockSpec((1, 1, unit * nb, LANE), lambda h, j: (2, h, j, 0)),
        out_shape=jax.ShapeDtypeStruct(dkvq.shape, F32), semantics=("parallel", "parallel"), aliases={8: 0},
    )(qk, qk, qk, proj, proj, dy, lg, dm, dkvq)


def sw_merge(outs, lses, ybuf, *, tm=512):
    t = ybuf.shape[0]

    def body(o0, o1, o2, l0_ref, l1_ref, l2_ref, _, y_ref, lg_ref):
        l0, l1, l2 = l0_ref[...], l1_ref[...], l2_ref[...]
        mx = jnp.maximum(jnp.maximum(l0, l1), l2)
        lg = mx + jnp.log(jnp.exp(l0 - mx) + jnp.exp(l1 - mx) + jnp.exp(l2 - mx))
        lg_ref[...] = lg
        y = jnp.exp(l0 - lg) * o0[...] + jnp.exp(l1 - lg) * o1[...] + jnp.exp(l2 - lg) * o2[...]
        y_ref[...] = jnp.concatenate([y[0], y[1]], axis=1)

    hp = pl.BlockSpec((2, tm, LANE), lambda i: (0, i, 0))
    return _call(
        body, name="sw_merge", grid=(t // tm,), in_specs=[hp] * 6 + [ANY],
        out_specs=[pl.BlockSpec((tm, SW_WIDTH), lambda i: (i, (DN_WIDTH + GM_WIDTH) // SW_WIDTH)), hp],
        out_shape=[jax.ShapeDtypeStruct(ybuf.shape, F32), jax.ShapeDtypeStruct((2, t, LANE), F32)],
        semantics=("parallel",), aliases={6: 0},
    )(*outs, *lses, ybuf)


def sw_delta(dy, ybuf, *, tm=512):
    t = ybuf.shape[0]

    def body(dy_ref, y_ref, o_ref):
        same, _ = _head_mats()
        o_ref[0], o_ref[1] = _halves(_xdot(dy_ref[...] * y_ref[...], same))

    b1 = pl.BlockSpec((tm, SW_WIDTH), lambda i: (i, (DN_WIDTH + GM_WIDTH) // SW_WIDTH))
    return _call(body, name="sw_delta", grid=(t // tm,), in_specs=[b1, b1],
                 out_specs=pl.BlockSpec((2, tm, LANE), lambda i: (0, i, 0)),
                 out_shape=jax.ShapeDtypeStruct((2, t, LANE), F32), semantics=("parallel",))(dy, ybuf)


def _rope_tables(t):
    inv = ROPE_THETA ** (-jnp.arange(0, ROPE_DIM, 2, dtype=F32) / ROPE_DIM)
    ang = jnp.arange(t, dtype=F32)[:, None] * inv[None, :]
    pad1 = jnp.ones((t, SW_HEAD_DIM - ROPE_DIM), F32)
    pad0 = jnp.zeros((t, SW_HEAD_DIM - ROPE_DIM), F32)
    cos_h = jnp.concatenate([jnp.cos(ang), jnp.cos(ang), pad1], axis=1)
    sin_h = jnp.concatenate([jnp.sin(ang), jnp.sin(ang), pad0], axis=1)
    return jnp.tile(cos_h, (1, SW_HEADS)), jnp.tile(sin_h, (1, SW_HEADS))


def sw_forward(proj, nw2, cos_t, sin_t, ybuf):
    qk = sw_prep(proj, nw2, cos_t, sin_t)
    outs, lses = [], []
    for p in range(len(SW_DILATIONS)):
        o, lse = sw_attn(qk, proj, p)
        outs.append(o)
        lses.append(lse)
    ybuf, lg = sw_merge(outs, lses, ybuf)
    return ybuf, (qk, lg)


def sw_backward(proj, nw2, cos_t, sin_t, res, ybuf, dy, dproj):
    qk, lg = res
    dm = sw_delta(dy, ybuf)
    dnw = jnp.zeros((2, 1, SW_WIDTH), F32)
    for p in range(len(SW_DILATIONS)):
        dkvq = sw_attn_dkv(qk, proj, dy, lg, dm, p)
        dkvq = sw_attn_dq(qk, proj, dy, lg, dm, dkvq, p)
        dproj, dnw = sw_prep_bwd(proj, nw2, cos_t, sin_t, dkvq, dproj, dnw, p)
    return dproj, dnw[::-1, 0]


def _pad_rows(a, rows):
    return jnp.zeros((rows,) + a.shape[1:], a.dtype).at[:a.shape[0]].set(a)


def _consts(sp):
    d = {}
    d["mix_nw"] = sp["mix_norm_w"][:, None, :]
    d["ffn_nw"] = sp["ffn_norm_w"][:, None, :]
    d["cw8"] = jnp.pad(sp["dn_conv_w"], ((0, 0), (0, 8 - DN_CONV), (0, 0)))
    d["scal"] = jnp.pad(jnp.stack([sp["dn_a_log"], sp["dn_dt_bias"]], axis=1), ((0, 0), (0, 6), (0, LANE - DN_HEADS)))
    d["wn"] = sp["dn_out_norm_w"][:, None, :]
    d["lng"] = sp["gm_ln_g"][:, None, :]
    d["lnb"] = sp["gm_ln_b"][:, None, :]
    d["w_s"] = sp["gm_w_s"]
    d["bst"] = jnp.pad(jnp.swapaxes(sp["gm_b_s"], 1, 2), ((0, 0), (0, 0), (0, LANE - GM_GROUPS)))
    d["nw2"] = jnp.stack([jnp.tile(sp["sw_q_norm_w"], (1, SW_HEADS)),
                          jnp.tile(sp["sw_k_norm_w"], (1, SW_HEADS))], axis=1)[:, :, None, :]
    return d


def _layer_fwd(x, mod, get_w, cs, tabs):
    wb = dict(get_w("w_in", x))
    h1, proj = norm_mm(x, cs["mix_nw"], mod[1], mod[0], wb["w_in"], swiglu=False, name="in_proj")
    act = dn_conv(proj, cs["cw8"])
    y, states, tinvs = dn_fwd(act, proj, cs["scal"], cs["wn"])
    y = gm_fwd(proj, cs["lng"], cs["lnb"], cs["w_s"], cs["bst"], y)
    y, swres = sw_forward(proj, cs["nw2"], *tabs, y)
    wb.update(get_w("w_out", y))
    x1, o1 = resid_mm(y, wb["w_out"], x, mod[2], name="out_proj")
    wb.update(get_w("ffn", x1))
    h2, gu, actf = norm_mm(x1, cs["ffn_nw"], mod[4], mod[3], wb["w_ffn_in"], swiglu=True, name="ffn_in")
    x2, o2 = resid_mm(actf, wb["w_ffn_out"], x1, mod[5], name="ffn_out")
    res = dict(x=x, h1=h1, proj=proj, act=act, states=states, tinvs=tinvs, swres=swres, y=y, x1=x1, o1=o1, h2=h2, gu=gu,
               actf=actf, o2=o2)
    return x2, res, wb


def _layer_bwd(dx2, res, mod, wb, cs, tabs, grads_done):
    dgu, gx2, dgate2 = resid_mm_bwd(dx2, mod[5], res["o2"], wb["w_ffn_out"], res["gu"], name="ffn_out_bwd", tm=512)
    g_wfo = mm_tn(res["actf"], gx2, name="wg_ffn_out")
    g_wfi = mm_tn(res["h2"], dgu, name="wg_ffn_in")
    token = grads_done("ffn", dict(w_ffn_in=g_wfi, w_ffn_out=g_wfo))
    dx1, d_ffn_nw, dscale2, dshift2 = norm_mm_bwd(dgu, wb["w_ffn_in"], res["x1"], cs["ffn_nw"], mod[4] + token, dx2,
                                                  name="ffn_in_bwd")
    dy, gx1, dgate1 = resid_mm_bwd(dx1, mod[2], res["o1"], wb["w_out"], None, name="out_proj_bwd", tm=512)
    g_wout = mm_tn(res["y"], gx1, name="wg_out")
    proj = res["proj"]
    dact, dproj, dpar = dn_bwd(res["act"], proj, cs["scal"], cs["wn"], res["states"], res["tinvs"], dy)
    dproj, dcw = dn_conv_bwd(proj, cs["cw8"], dact, dproj)
    dproj, dws, dbst, dln = gm_bwd(proj, cs["lng"], cs["lnb"], cs["w_s"], cs["bst"], dy, dproj)
    dproj, dnw = sw_backward(proj, cs["nw2"], *tabs, res["swres"], res["y"], dy, dproj)
    g_win = mm_tn(res["h1"], dproj, name="wg_in")
    dx, d_mix_nw, dscale1, dshift1 = norm_mm_bwd(dproj, wb["w_in"], res["x"], cs["mix_nw"], mod[1], dx1,
                                                 name="in_proj_bwd")
    dmod = jnp.concatenate([dshift1, dscale1, dgate1, dshift2, dscale2, dgate2], axis=1)
    dnw = dnw.reshape(2, SW_HEADS, SW_HEAD_DIM).sum(1)
    small = dict(mix_norm_w=d_mix_nw[0], ffn_norm_w=d_ffn_nw[0], dn_conv_w=dcw[:DN_CONV],
                 dn_a_log=dpar[0, :DN_HEADS], dn_dt_bias=dpar[1, :DN_HEADS], dn_out_norm_w=dpar[2],
                 gm_ln_g=dln[0], gm_ln_b=dln[1], gm_w_s=dws, gm_b_s=dbst[:, :GM_GROUPS].T,
                 sw_q_norm_w=dnw[0], sw_k_norm_w=dnw[1])
    token = grads_done("mix", dict(w_in=g_win, w_out=g_wout))
    return dx, small, dmod, token


def _permute_w_in(w):
    pad = jnp.zeros(w.shape[:-1] + (AB_PAD - 8,), w.dtype)
    return jnp.concatenate([w[..., 0:2056], pad, w[..., 2568:IN_WIDTH], w[..., 2056:2568]], axis=-1)


def _unpermute_w_in(g):
    return jnp.concatenate([g[..., 0:2056], g[..., C_UV:IN_PAD], g[..., C_SW:C_UV]], axis=-1)


def _local_step(x, target, mods, weights_of, grads_done, sp):
    layers = mods.shape[0]
    t, d = x.shape
    tabs = _rope_tables(t)
    consts = _consts(sp)
    saved = []
    for layer in range(layers):
        mod = mods[layer].reshape(6, 1, d)
        cs = {k: v[layer] for k, v in consts.items()}
        x, res, wb = _layer_fwd(x, mod, functools.partial(weights_of, layer), cs, tabs)
        saved.append((res, mod, wb, cs))
    dx, loss = loss_head(x, target)
    smalls, dmods = [], []
    token = jnp.zeros((1, 1), F32)
    for layer in reversed(range(layers)):
        res, mod, wb, cs = saved[layer]
        dx, small, dmod, token = _layer_bwd(dx, res, mod + token, wb, cs, tabs, functools.partial(grads_done, layer))
        smalls.append(small)
        dmods.append(dmod[0])
    smalls, dmods = smalls[::-1], dmods[::-1]
    small = {k: jnp.stack([s[k] for s in smalls]) for k in smalls[0]}
    return loss, dx, small, jnp.stack(dmods) + token


def mod_fwd(c_all, w_mod, b_shard):
    layers, d, n = w_mod.shape

    def body(c_ref, w_ref, b_ref, o_ref):
        ca = _silu(c_ref[...]).astype(BF16)
        o_ref[0] = _dot(ca, w_ref[0].astype(BF16), 1, 0) + b_ref[0]

    return _call(
        body, name="mod_fwd", grid=(layers,),
        in_specs=[_full((8, d)), pl.BlockSpec((1, d, n), lambda i: (i, 0, 0)),
                  pl.BlockSpec((1, 1, n), lambda i: (i, 0, 0))],
        out_specs=pl.BlockSpec((1, 8, n), lambda i: (i, 0, 0)),
        out_shape=jax.ShapeDtypeStruct((layers, 8, n), F32), semantics=("parallel",),
    )(c_all, w_mod, b_shard)


def mod_bwd(c_all, dmod):
    layers, _, n = dmod.shape
    d = c_all.shape[1]

    def body(c_ref, g_ref, o_ref):
        ca = _silu(c_ref[...]).astype(BF16)
        o_ref[0] = _dot(ca, g_ref[0].astype(BF16), 0, 0)

    return _call(
        body, name="mod_bwd", grid=(layers,),
        in_specs=[_full((8, d)), pl.BlockSpec((1, 8, n), lambda i: (i, 0, 0))],
        out_specs=pl.BlockSpec((1, d, n), lambda i: (i, 0, 0)),
        out_shape=jax.ShapeDtypeStruct((layers, d, n), F32), semantics=("parallel",),
    )(c_all, dmod)


N_DEV = 8


def _place():
    return lax.axis_index("x"), lax.axis_index("y"), lax.axis_index("c")


def _other_chips(x, y):
    return [(1 - x, y), (x, 1 - y), (1 - x, 1 - y)]


def allgather8(x_shard, *, name):
    m_per, n = x_shard.shape

    def body(x_ref, out_ref, send_sems, recv_sems, local_sem):
        x, y, c = _place()
        me, sibling = (x, y, c), (x, y, 1 - c)
        chips = _other_chips(x, y)

        def rows(px, py, pc):
            return out_ref.at[pl.ds((4 * px + 2 * py + pc) * m_per, m_per), :]

        def copy(k, block, to, src=None):
            return pltpu.make_async_remote_copy(
                src_ref=rows(*block) if src is None else src, dst_ref=rows(*block),
                send_sem=send_sems.at[k], recv_sem=recv_sems.at[k], device_id=to, device_id_type=MESH)

        mine = pltpu.make_async_copy(x_ref, rows(*me), local_sem)
        mine.start()
        first = [copy(0, me, sibling, src=x_ref)]
        first += [copy(1 + j, me, (*chip, c), src=x_ref) for j, chip in enumerate(chips)]
        for cp in first:
            cp.start()
        passed = [copy(4 + j, (*chip, c), sibling) for j, chip in enumerate(chips)]
        for j, chip in enumerate(chips):
            copy(1 + j, (*chip, c), me).wait_recv()
            passed[j].start()
        copy(0, sibling, me).wait_recv()
        for j, chip in enumerate(chips):
            copy(4 + j, (*chip, 1 - c), me).wait_recv()
        for cp in first + passed:
            cp.wait_send()
        mine.wait()

    return pl.pallas_call(
        body, name=name, out_shape=jax.ShapeDtypeStruct((N_DEV * m_per, n), x_shard.dtype),
        in_specs=[pl.BlockSpec(memory_space=pltpu.VMEM)], out_specs=pl.BlockSpec(memory_space=pltpu.VMEM),
        scratch_shapes=[pltpu.SemaphoreType.DMA((7,)), pltpu.SemaphoreType.DMA((7,)), pltpu.SemaphoreType.DMA],
    )(x_shard)


HBM = pl.BlockSpec(memory_space=pltpu.HBM)
SEM = pl.BlockSpec(memory_space=pltpu.SEMAPHORE)
_EFFECT = pltpu.SideEffectType.DATAFLOW_SIDE_EFFECTING


def _piece(ref, sliced, chip):
    return ref.at[2 * chip[0] + chip[1]] if sliced else ref


def exchange_start(srcs, after, *, sliced, name):
    n = len(srcs)
    piece = lambda s: s.shape[1:] if sliced else s.shape

    def body(*refs):
        ins, lands = refs[:n], refs[n:2 * n]
        send_sems, recv_sems = refs[2 * n + len(after):2 * n + len(after) + 2]
        token = refs[-1]
        x, y, c = _place()
        me_s = 2 * x + y
        for a in range(n):
            for j, chip in enumerate(_other_chips(x, y)):
                pltpu.make_async_remote_copy(
                    src_ref=_piece(ins[a], sliced, chip), dst_ref=lands[a].at[me_s], send_sem=send_sems.at[3 * a + j],
                    recv_sem=recv_sems.at[3 * a + j], device_id=(*chip, c), device_id_type=MESH).start()
        token[...] = jnp.zeros_like(token)

    zones = [pltpu.with_memory_space_constraint(lax.empty((4,) + piece(s), s.dtype), pltpu.HBM) for s in srcs]
    srcs = [pltpu.with_memory_space_constraint(s, pltpu.HBM) for s in srcs]
    out = pl.pallas_call(
        body, name=name,
        out_shape=(pltpu.SemaphoreType.DMA((3 * n,)), pltpu.SemaphoreType.DMA((3 * n,)),
                   *[pltpu.HBM(s.shape, s.dtype) for s in srcs], *[pltpu.HBM(z.shape, z.dtype) for z in zones],
                   jax.ShapeDtypeStruct((8, LANE), F32)),
        in_specs=[HBM] * (2 * n) + [ANY] * len(after),
        out_specs=(SEM, SEM, *[HBM] * (2 * n), pl.BlockSpec(memory_space=pltpu.VMEM)),
        input_output_aliases={i: 2 + i for i in range(2 * n)},
        compiler_params=pltpu.CompilerParams(has_side_effects=_EFFECT),
    )(*srcs, *zones, *after)
    return out[0], out[1], out[2:2 + n], out[2 + n:2 + 2 * n], out[-1]


def exchange_wait(send_sems, recv_sems, srcs, zones, after, *, which, sliced, name):
    n = len(srcs)

    def body(*refs):
        ins, lands = refs[:n], refs[n:2 * n]
        send_sems, recv_sems = refs[2 * n:2 * n + 2]
        x, y, c = _place()
        for a in range(n):
            for j, chip in enumerate(_other_chips(x, y)):
                copy = pltpu.make_async_remote_copy(
                    src_ref=_piece(ins[a], sliced, chip), dst_ref=lands[a].at[2 * chip[0] + chip[1]],
                    send_sem=send_sems.at[3 * which[a] + j], recv_sem=recv_sems.at[3 * which[a] + j],
                    device_id=(*chip, c), device_id_type=MESH)
                copy.wait_send()
                copy.wait_recv()

    out = pl.pallas_call(
        body, name=name,
        out_shape=tuple(pltpu.HBM(s.shape, s.dtype) for s in (*srcs, *zones)),
        in_specs=[HBM] * (2 * n) + [SEM, SEM, ANY], out_specs=tuple([HBM] * (2 * n)),
        input_output_aliases={i: i for i in range(2 * n)},
        compiler_params=pltpu.CompilerParams(has_side_effects=_EFFECT),
    )(*srcs, *zones, send_sems, recv_sems, after)
    return out[n:]


def sibling_swap(parts):
    n = len(parts)

    def body(*refs):
        ins, outs = refs[:n], refs[n:2 * n]
        send_sems, recv_sems = refs[2 * n:]
        x, y, c = _place()
        cps = []
        for a in range(n):
            cp = pltpu.make_async_remote_copy(
                src_ref=ins[a], dst_ref=outs[a], send_sem=send_sems.at[a], recv_sem=recv_sems.at[a],
                device_id=(x, y, 1 - c), device_id_type=MESH)
            cp.start()
            cps.append(cp)
        for cp in cps:
            cp.wait()

    return pl.pallas_call(
        body, name="sibling_swap", out_shape=[jax.ShapeDtypeStruct(p.shape, p.dtype) for p in parts],
        in_specs=[ANY] * n, out_specs=[ANY] * n,
        scratch_shapes=[pltpu.SemaphoreType.DMA((n,)), pltpu.SemaphoreType.DMA((n,))],
    )(*parts)


def _row_block(rows, cols, budget=1 << 20):
    best = rows if rows % 8 else 8
    for tr in range(8, rows + 1, 8):
        if rows % tr == 0 and tr * cols * 4 <= budget:
            best = tr
    return best


def chip_sum(own, recv, me_s, buf, layer, layers, *, name):
    r, n = own.shape
    tr = _row_block(r, n)
    steps = r // tr

    def body(me_ref, own_ref, recv_ref, *rest):
        o_ref = rest[-1]
        me = me_ref[0]
        acc = jnp.zeros((tr, n), F32)
        for s in range(4):
            acc = acc + jnp.where(me == s, own_ref[...], recv_ref[s].astype(F32))
        o_ref[...] = acc

    in_specs = [pl.BlockSpec((tr, n), lambda i, me: (i, 0)), pl.BlockSpec((4, tr, n), lambda i, me: (0, i, 0))]
    args = [me_s, own, recv]
    aliases = {}
    if buf is not None:
        in_specs.append(ANY)
        args.append(buf)
        aliases = {3: 0}
    return pl.pallas_call(
        body, name=name, out_shape=jax.ShapeDtypeStruct((layers * r, n), F32),
        grid_spec=pltpu.PrefetchScalarGridSpec(
            num_scalar_prefetch=1, grid=(steps,), in_specs=in_specs,
            out_specs=pl.BlockSpec((tr, n), lambda i, me: (layer * steps + i, 0))),
        input_output_aliases=aliases,
        compiler_params=pltpu.CompilerParams(dimension_semantics=("parallel",)),
    )(*args)


def _adam_update(w, g, m, v):
    m2 = ADAM_B1 * m + (1.0 - ADAM_B1) * g
    v2 = ADAM_B2 * v + (1.0 - ADAM_B2) * (g * g)
    m_hat = m2 / (1.0 - ADAM_B1 ** ADAM_STEP)
    v_hat = v2 / (1.0 - ADAM_B2 ** ADAM_STEP)
    delta = -ADAM_LR * (m_hat / (jnp.sqrt(v_hat) + ADAM_EPS) + ADAM_WD * w)
    return delta, m2, v2


def adamw(w, g_parts, m, v, *, name):
    r, n = w.shape
    tr = _row_block(r, n)
    k = len(g_parts)

    def body(*refs):
        w_ref, m_ref, v_ref = refs[k], refs[k + 1], refs[k + 2]
        g_ref, d_ref, m2_ref, v2_ref = refs[k + 3:]
        g = refs[0][...]
        for p in refs[1:k]:
            g = g + p[...]
        g_ref[...] = g
        d_ref[...], m2_ref[...], v2_ref[...] = _adam_update(w_ref[...], g, m_ref[...], v_ref[...])

    blk = pl.BlockSpec((tr, n), lambda i: (i, 0))
    shp = jax.ShapeDtypeStruct((r, n), F32)
    return _call(body, name=name, grid=(r // tr,), in_specs=[blk] * (k + 3), out_specs=[blk] * 4,
                 out_shape=[shp] * 4, semantics=("parallel",))(*g_parts, w, m, v)


def adamw_gathered(g_all, w, m, v, *, name):
    _, r, n = g_all.shape
    tr = _row_block(r, n * 4)

    def body(ga_ref, w_ref, m_ref, v_ref, g_ref, d_ref, m2_ref, v2_ref):
        g = ga_ref[0]
        for dev in range(1, N_DEV):
            g = g + ga_ref[dev]
        g_ref[...] = g
        d_ref[...], m2_ref[...], v2_ref[...] = _adam_update(w_ref[...], g, m_ref[...], v_ref[...])

    blk = pl.BlockSpec((tr, n), lambda i: (i, 0))
    shp = jax.ShapeDtypeStruct((r, n), F32)
    return _call(body, name=name, grid=(r // tr,),
                 in_specs=[pl.BlockSpec((N_DEV, tr, n), lambda i: (0, i, 0)), blk, blk, blk], out_specs=[blk] * 4,
                 out_shape=[shp] * 4, semantics=("parallel",))(g_all, w, m, v)


BIG = ("w_in", "w_out", "w_ffn_in", "w_ffn_out")
SMALL = ("b_mod", "mix_norm_w", "ffn_norm_w", "dn_conv_w", "dn_a_log", "dn_dt_bias", "dn_out_norm_w", "gm_ln_g",
         "gm_ln_b", "gm_w_s", "gm_b_s", "sw_q_norm_w", "sw_k_norm_w")
WEIGHTS = ("w_mod", "b_mod", "mix_norm_w", "ffn_norm_w", "w_in", "w_out", "dn_conv_w", "dn_a_log", "dn_dt_bias",
           "dn_out_norm_w", "gm_ln_g", "gm_ln_b", "gm_w_s", "gm_b_s", "sw_q_norm_w", "sw_k_norm_w", "w_ffn_in",
           "w_ffn_out")
PACK_ROWS = 8


def _pack(arrs):
    out = []
    for a in arrs:
        flat = a.reshape(-1).astype(F32)
        rows = -(-flat.shape[0] // (LANE * PACK_ROWS)) * PACK_ROWS
        out.append(jnp.pad(flat, (0, rows * LANE - flat.shape[0])).reshape(rows, LANE))
    return jnp.concatenate(out, axis=0)


def _unpack(packed, shapes):
    out, r0 = [], 0
    for shp in shapes:
        size = math.prod(shp)
        rows = -(-size // (LANE * PACK_ROWS)) * PACK_ROWS
        out.append(packed[r0:r0 + rows].reshape(-1)[:size].reshape(shp))
        r0 += rows
    return out


def kernel(x, c, w_mod, b_mod, mix_norm_w, ffn_norm_w, w_in, w_out, dn_conv_w, dn_a_log, dn_dt_bias, dn_out_norm_w, gm_ln_g, gm_ln_b, gm_w_s, gm_b_s, sw_q_norm_w, sw_k_norm_w, w_ffn_in, w_ffn_out, loss_target, m_w_mod, m_b_mod, m_mix_norm_w, m_ffn_norm_w, m_w_in, m_w_out, m_dn_conv_w, m_dn_a_log, m_dn_dt_bias, m_dn_out_norm_w, m_gm_ln_g, m_gm_ln_b, m_gm_w_s, m_gm_b_s, m_sw_q_norm_w, m_sw_k_norm_w, m_w_ffn_in, m_w_ffn_out, v_w_mod, v_b_mod, v_mix_norm_w, v_ffn_norm_w, v_w_in, v_w_out, v_dn_conv_w, v_dn_a_log, v_dn_dt_bias, v_dn_out_norm_w, v_gm_ln_g, v_gm_ln_b, v_gm_w_s, v_gm_b_s, v_sw_q_norm_w, v_sw_k_norm_w, v_w_ffn_in, v_w_ffn_out):
    w = dict(w_mod=w_mod, b_mod=b_mod, mix_norm_w=mix_norm_w, ffn_norm_w=ffn_norm_w, w_in=w_in, w_out=w_out,
             dn_conv_w=dn_conv_w, dn_a_log=dn_a_log, dn_dt_bias=dn_dt_bias, dn_out_norm_w=dn_out_norm_w,
             gm_ln_g=gm_ln_g, gm_ln_b=gm_ln_b, gm_w_s=gm_w_s, gm_b_s=gm_b_s, sw_q_norm_w=sw_q_norm_w,
             sw_k_norm_w=sw_k_norm_w, w_ffn_in=w_ffn_in, w_ffn_out=w_ffn_out)
    m = dict(w_mod=m_w_mod, b_mod=m_b_mod, mix_norm_w=m_mix_norm_w, ffn_norm_w=m_ffn_norm_w, w_in=m_w_in,
             w_out=m_w_out, dn_conv_w=m_dn_conv_w, dn_a_log=m_dn_a_log, dn_dt_bias=m_dn_dt_bias,
             dn_out_norm_w=m_dn_out_norm_w, gm_ln_g=m_gm_ln_g, gm_ln_b=m_gm_ln_b, gm_w_s=m_gm_w_s, gm_b_s=m_gm_b_s,
             sw_q_norm_w=m_sw_q_norm_w, sw_k_norm_w=m_sw_k_norm_w, w_ffn_in=m_w_ffn_in, w_ffn_out=m_w_ffn_out)
    v = dict(w_mod=v_w_mod, b_mod=v_b_mod, mix_norm_w=v_mix_norm_w, ffn_norm_w=v_ffn_norm_w, w_in=v_w_in,
             w_out=v_w_out, dn_conv_w=v_dn_conv_w, dn_a_log=v_dn_a_log, dn_dt_bias=v_dn_dt_bias,
             dn_out_norm_w=v_dn_out_norm_w, gm_ln_g=v_gm_ln_g, gm_ln_b=v_gm_ln_b, gm_w_s=v_gm_w_s, gm_b_s=v_gm_b_s,
             sw_q_norm_w=v_sw_q_norm_w, sw_k_norm_w=v_sw_k_norm_w, w_ffn_in=v_w_ffn_in, w_ffn_out=v_w_ffn_out)
    layers, d, mod_n = w_mod.shape
    mx, my, mc = _place()
    me_s = 2 * mx + my
    me_dev = 4 * mx + 2 * my + mc

    c_all = allgather8(_pad_rows(c, 8), name="gather_c").reshape(N_DEV, 8, d)[:, 0]
    b_shard = lax.dynamic_slice_in_dim(b_mod, me_s * mod_n, mod_n, axis=1)[:, None, :]
    mod_part = mod_fwd(c_all, w_mod, b_shard)
    mod_parts = allgather8(mod_part.reshape(layers * 8, mod_n), name="gather_mod")
    mod_parts = mod_parts.reshape(4, 2, layers, 8, mod_n)[:, 0]
    mod_all = mod_parts.transpose(1, 2, 0, 3).reshape(layers, 8, 4 * mod_n)
    mods = lax.dynamic_index_in_dim(mod_all, me_dev, axis=1, keepdims=False)

    cw = dn_conv_w.shape[-1]
    conv_rows = -(-layers * DN_CONV // 8) * 8
    conv_parts = allgather8(_pad_rows(dn_conv_w.reshape(layers * DN_CONV, cw), conv_rows), name="gather_conv")
    conv_parts = conv_parts.reshape(4, 2, conv_rows, cw)[:, 0, :layers * DN_CONV]
    conv_full = conv_parts.reshape(4, layers, DN_CONV, cw).transpose(1, 2, 0, 3).reshape(layers, DN_CONV, 4 * cw)

    shards = {k: w[k].astype(BF16) for k in BIG}
    groups = dict(w_in=(0,), w_out=(1,), ffn=(2, 3))
    gathers = [exchange_start([shards[k][layer] for k in BIG], [mods, conv_full], sliced=False, name=f"gather_start{layer}")
               for layer in range(layers)]
    mods = mods + sum(g[4][0, 0] for g in gathers)

    def weights_of(layer, group, after):
        send_sems, recv_sems, srcs, zones, _ = gathers[layer]
        which = groups[group]
        got = exchange_wait(send_sems, recv_sems, [srcs[a] for a in which], [zones[a] for a in which], after,
                            which=which, sliced=False, name=f"gather_wait_{group}{layer}")
        full = {BIG[a]: lax.dynamic_update_index_in_dim(z, shards[BIG[a]][layer], me_s, 0) for a, z in zip(which, got)}
        cols = lambda g: jnp.concatenate([g[s] for s in range(4)], axis=-1)
        shape = dict(w_in=lambda g: _permute_w_in(cols(g)), w_out=lambda g: g.reshape(-1, d), w_ffn_in=cols,
                     w_ffn_out=lambda g: g.reshape(-1, d))
        return {k: shape[k](g) for k, g in full.items()}

    scatters = {}
    last_scatter = []
    shard_axis = dict(w_in=1, w_out=0, w_ffn_in=1, w_ffn_out=0)

    def grads_done(layer, group, grads):
        grads = {k: _unpermute_w_in(g) if k == "w_in" else g for k, g in grads.items()}
        send = [jnp.stack(jnp.split(g.astype(BF16), 4, axis=shard_axis[k])) for k, g in grads.items()]
        own = {}
        for k, g in grads.items():
            size = g.shape[shard_axis[k]] // 4
            own[k] = lax.dynamic_slice_in_dim(g, me_s * size, size, axis=shard_axis[k])
        if (layer, group) == (0, "mix"):
            last_scatter.append((send, own))
            return jnp.zeros((1, 1), F32)
        started = exchange_start(send, [], sliced=True, name=f"scatter_start_{group}{layer}")
        scatters[layer, group] = (started, own)
        return started[4][:1, :1]

    sp = {k: w[k] for k in SMALL}
    sp["dn_conv_w"] = conv_full
    loss_blk, grad_x, small, dmods = _local_step(x[0], loss_target[0], mods, weights_of, grads_done, sp)
    loss = lax.psum(loss_blk[0, 0], ("x", "y", "c"))

    outs = {}
    small = dict(small, b_mod=dmods)
    packed = _pack([small[k] for k in SMALL])
    rows = packed.shape[0]
    g_all = allgather8(packed, name="gather_small").reshape(N_DEV, rows, LANE)
    send, own = last_scatter[0]
    scatters[0, "mix"] = (exchange_start(send, [g_all], sliced=True, name="scatter_start_mix0"), own)
    g_all = g_all + scatters[0, "mix"][0][4][0, 0]
    conv_zero = jnp.zeros((layers, DN_CONV, 3 * DN_WIDTH), F32)
    pk = lambda src: _pack([conv_zero if k == "dn_conv_w" else src[k] for k in SMALL])
    res = adamw_gathered(g_all, pk(w), pk(m), pk(v), name="adamw_small")
    shapes = [small[k].shape for k in SMALL]
    un = [_unpack(a, shapes) for a in res]
    for i, k in enumerate(SMALL):
        outs[k] = [un[j][i] for j in range(4)]
    g_conv = lax.dynamic_slice_in_dim(outs["dn_conv_w"][0], me_s * cw, cw, axis=2)
    flat = lambda a: a.reshape(-1, cw)
    res = adamw(flat(dn_conv_w), [flat(g_conv)], flat(m["dn_conv_w"]), flat(v["dn_conv_w"]), name="adamw_conv")
    outs["dn_conv_w"] = [a.reshape(dn_conv_w.shape) for a in res]

    b_rows = layers * 6 * d // LANE
    dmod_all = g_all[:, :b_rows].reshape(N_DEV, layers, 6 * d).transpose(1, 0, 2)
    dmod_shard = lax.dynamic_slice_in_dim(dmod_all, me_s * mod_n, mod_n, axis=2)
    g_wmod = mod_bwd(c_all, dmod_shard)
    flat = lambda a: a.reshape(-1, mod_n)
    res = adamw(flat(w_mod), [flat(g_wmod)], flat(m_w_mod), flat(v_w_mod), name="adamw_w_mod")
    outs["w_mod"] = [a.reshape(w_mod.shape) for a in res]

    me_arr = jnp.reshape(me_s, (1,)).astype(jnp.int32)
    partial = {k: None for k in BIG}
    for layer in range(layers):
        for group in ("ffn", "mix"):
            (send_sems, recv_sems, srcs, zones, _), own = scatters[layer, group]
            zones = exchange_wait(send_sems, recv_sems, srcs, zones, res[0], which=tuple(range(len(srcs))),
                                  sliced=True, name=f"scatter_wait_{group}{layer}")
            for k, z in zip(own, zones):
                partial[k] = chip_sum(own[k], z, me_arr, partial[k], layer, layers, name=f"chip_sum_{k}{layer}")
    partial = [partial[k] for k in BIG]
    theirs = sibling_swap(partial)
    for k, mine, other in zip(BIG, partial, theirs):
        shp = w[k].shape
        flat = lambda a: a.reshape(-1, shp[-1])
        res = adamw(flat(w[k]), [mine, other], flat(m[k]), flat(v[k]), name="adamw_" + k)
        outs[k] = [a.reshape(shp) for a in res]

    result = [loss, grad_x[None]]
    for j in range(4):
        result += [outs[k][j] for k in WEIGHTS]
    return tuple(result)
```

```python
import functools
import math

import jax
import jax.numpy as jnp
from jax import lax
from jax.experimental import pallas as pl
from jax.experimental.pallas import tpu as pltpu

F32 = jnp.float32
BF16 = jnp.bfloat16
HI = lax.Precision.HIGH

NORM_EPS = 1e-6
DN_HEADS = 4
DN_HEAD_DIM = 128
DN_WIDTH = 512
DN_CHUNK = 64
DN_CONV = 4
GM_WIDTH = 256
GM_GROUPS = 4
GM_GROUP_DIM = 64
GM_CHUNK = 128
SW_HEADS = 4
SW_HEAD_DIM = 64
SW_WIDTH = 256
SW_DILATIONS = (1, 4, 16)
SW_BLOCK = 128
ROPE_THETA = 500000.0
ROPE_DIM = 16
LANE = 128

C_QKV = 0
C_Z = 1536
C_AB = 2048
C_SW = 2304
C_UV = 4608
IN_WIDTH = 4872
IN_PAD = 5120
AB_PAD = C_SW - C_AB
MIX_WIDTH = 1024

ADAM_LR = 0.001
ADAM_B1 = 0.9
ADAM_B2 = 0.999
ADAM_EPS = 1e-08
ADAM_WD = 0.01
ADAM_STEP = 10

MESH = pl.DeviceIdType.MESH


BIG_VMEM = 56 << 20


def _call(body, *, name, grid, in_specs, out_specs, out_shape, scratch_shapes=(), semantics=None, aliases=None,
          vmem=None):
    if semantics is None:
        semantics = ("arbitrary",) * len(grid)
    return pl.pallas_call(
        body, name=name, grid=grid, in_specs=in_specs, out_specs=out_specs, out_shape=out_shape,
        scratch_shapes=list(scratch_shapes), input_output_aliases=aliases or {},
        compiler_params=pltpu.CompilerParams(dimension_semantics=semantics, vmem_limit_bytes=vmem),
    )


def _dot(a, b, ca, cb, prec=None):
    if a.ndim == 3:
        dims = (((ca + 1,), (cb + 1,)), ((0,), (0,)))
    else:
        dims = (((ca,), (cb,)), ((), ()))
    return lax.dot_general(a, b, dims, preferred_element_type=F32, precision=prec)


def _bdot(a, b, ca=1, cb=0):
    return _dot(a.astype(BF16), b.astype(BF16), ca, cb)


def _hdot(a, b, ca=1, cb=0):
    return _dot(a.astype(F32), b.astype(F32), ca, cb, HI)


def _split(x):
    hi = x.astype(BF16)
    return hi, (x - hi.astype(F32)).astype(BF16)


def _xdot(a, b, ca=1, cb=0, exact=1):
    if exact == 1:
        hi, lo = _split(a)
        e = b.astype(BF16)
        return _dot(hi, e, ca, cb) + _dot(lo, e, ca, cb)
    hi, lo = _split(b)
    e = a.astype(BF16)
    return _dot(e, hi, ca, cb) + _dot(e, lo, ca, cb)


def _sigmoid(x):
    return 0.5 * jnp.tanh(0.5 * x) + 0.5


def _silu(x):
    return x * _sigmoid(x)


def _dsilu(x):
    s = _sigmoid(x)
    return s * (1.0 + x * (1.0 - s))


def _softplus(x):
    return jnp.maximum(x, 0.0) + jnp.log(1.0 + jnp.exp(-jnp.abs(x)))


def _iota2(shape, dim):
    return lax.broadcasted_iota(jnp.int32, shape, dim)


def _rowsum(x):
    return jnp.sum(x, axis=-1, keepdims=True)


def _colsum(x):
    return jnp.sum(x, axis=-2, keepdims=True)


def _full(shape):
    return pl.BlockSpec(shape, lambda *_: (0,) * len(shape))


def _resident(shape):
    return pl.BlockSpec(shape, lambda *_: (0,) * len(shape), pipeline_mode=pl.Buffered(1))


ANY = pl.BlockSpec(memory_space=pl.ANY)


def _norm_mod(x, nw, scale, shift):
    r = lax.rsqrt(jnp.mean(x * x, axis=-1, keepdims=True) + NORM_EPS)
    xn = x * r
    return xn, r, (xn * nw) * (1.0 + scale) + shift


def norm_mm(x, nw, scale, shift, w, *, swiglu, name, tm=512):
    t, d = x.shape
    n = w.shape[1]
    half = n // 2

    def body(x_ref, nw_ref, sc_ref, sh_ref, w_ref, h_ref, y_ref, *act_ref):
        _, _, h = _norm_mod(x_ref[...], nw_ref[...], sc_ref[...], sh_ref[...])
        hb = h.astype(BF16)
        h_ref[...] = hb
        y = _dot(hb, w_ref[...], 1, 0)
        y_ref[...] = y.astype(y_ref.dtype)
        if swiglu:
            act_ref[0][...] = (_silu(y[:, :half]) * y[:, half:]).astype(BF16)

    row = lambda i: (i, 0)
    out_shape = [jax.ShapeDtypeStruct((t, d), BF16), jax.ShapeDtypeStruct((t, n), BF16 if swiglu else F32)]
    out_specs = [pl.BlockSpec((tm, d), row), pl.BlockSpec((tm, n), row)]
    if swiglu:
        out_shape.append(jax.ShapeDtypeStruct((t, half), BF16))
        out_specs.append(pl.BlockSpec((tm, half), row))
    return _call(
        body, name=name, grid=(t // tm,),
        in_specs=[pl.BlockSpec((tm, d), row), _full((1, d)), _full((1, d)), _full((1, d)), _resident((d, n))],
        out_specs=out_specs, out_shape=out_shape, semantics=("parallel",), vmem=BIG_VMEM,
    )(x, nw, scale, shift, w)


def resid_mm(y, w, x, gate, *, name, tm=512):
    t, k = y.shape
    d = w.shape[1]

    def body(y_ref, w_ref, x_ref, g_ref, xo_ref, o_ref):
        o = _dot(y_ref[...].astype(BF16), w_ref[...], 1, 0)
        o_ref[...] = o
        xo_ref[...] = x_ref[...] + g_ref[...] * o

    row = lambda i: (i, 0)
    return _call(
        body, name=name, grid=(t // tm,),
        in_specs=[pl.BlockSpec((tm, k), row), _resident((k, d)), pl.BlockSpec((tm, d), row), _full((1, d))],
        out_specs=[pl.BlockSpec((tm, d), row), pl.BlockSpec((tm, d), row)],
        out_shape=[jax.ShapeDtypeStruct((t, d), F32), jax.ShapeDtypeStruct((t, d), F32)],
        semantics=("parallel",), vmem=BIG_VMEM,
    )(y, w, x, gate)


def resid_mm_bwd(dx, gate, o, w, gu, *, name, tm):
    t, d = dx.shape
    k = w.shape[0]
    swiglu = gu is not None

    def body(dx_ref, g_ref, o_ref, w_ref, *rest):
        if swiglu:
            gu_ref, dy_ref, gx_ref, dg_ref = rest
        else:
            dy_ref, gx_ref, dg_ref = rest
        i = pl.program_id(0)
        dxv = dx_ref[...]
        gx = (dxv * g_ref[...]).astype(BF16)
        gx_ref[...] = gx
        part = _colsum(dxv * o_ref[...])

        @pl.when(i == 0)
        def _():
            dg_ref[...] = jnp.zeros_like(dg_ref)

        dg_ref[...] += part
        da = _dot(gx, w_ref[...], 1, 1)
        if swiglu:
            g = gu_ref[:, :k].astype(F32)
            u = gu_ref[:, k:].astype(F32)
            dy_ref[:, :k] = (da * u * _dsilu(g)).astype(BF16)
            dy_ref[:, k:] = (da * _silu(g)).astype(BF16)
        else:
            dy_ref[...] = da

    row = lambda i: (i, 0)
    in_specs = [pl.BlockSpec((tm, d), row), _full((1, d)), pl.BlockSpec((tm, d), row), _resident((k, d))]
    args = [dx, gate, o, w]
    if swiglu:
        in_specs.append(pl.BlockSpec((tm, 2 * k), row))
        args.append(gu)
        dy_shape = jax.ShapeDtypeStruct((t, 2 * k), BF16)
        dy_spec = pl.BlockSpec((tm, 2 * k), row)
    else:
        dy_shape = jax.ShapeDtypeStruct((t, k), F32)
        dy_spec = pl.BlockSpec((tm, k), row)
    return _call(
        body, name=name, grid=(t // tm,), in_specs=in_specs,
        out_specs=[dy_spec, pl.BlockSpec((tm, d), row), _full((1, d))],
        out_shape=[dy_shape, jax.ShapeDtypeStruct((t, d), BF16), jax.ShapeDtypeStruct((1, d), F32)], vmem=BIG_VMEM,
    )(*args)


def norm_mm_bwd(dy, w, x, nw, scale, dres, *, name, tm=512):
    t, n = dy.shape
    d = x.shape[1]
    steps = t // tm

    def body(dy_ref, w_ref, x_ref, nw_ref, sc_ref, dres_ref, dx_ref, dnw_ref, dsc_ref, dsh_ref):
        i = pl.program_id(0)
        dh = _dot(dy_ref[...].astype(BF16), w_ref[...], 1, 1)
        x = x_ref[...]
        r = lax.rsqrt(jnp.mean(x * x, axis=-1, keepdims=True) + NORM_EPS)
        xn = x * r
        a = nw_ref[...] * (1.0 + sc_ref[...])

        @pl.when(i == 0)
        def _():
            dnw_ref[...] = jnp.zeros_like(dnw_ref)
            dsh_ref[...] = jnp.zeros_like(dsh_ref)

        dnw_ref[...] += _colsum(dh * xn)
        dsh_ref[...] += _colsum(dh)
        dxn = dh * a
        dx_ref[...] = r * (dxn - xn * jnp.mean(dxn * xn, axis=-1, keepdims=True)) + dres_ref[...]

        @pl.when(i == steps - 1)
        def _():
            da = dnw_ref[...]
            dsc_ref[...] = da * nw_ref[...]
            dnw_ref[...] = da * (1.0 + sc_ref[...])

    row = lambda i: (i, 0)
    vec = jax.ShapeDtypeStruct((1, d), F32)
    return _call(
        body, name=name, grid=(steps,),
        in_specs=[pl.BlockSpec((tm, n), row), _resident((d, n)), pl.BlockSpec((tm, d), row), _full((1, d)),
                  _full((1, d)), pl.BlockSpec((tm, d), row)],
        out_specs=[pl.BlockSpec((tm, d), row), _full((1, d)), _full((1, d)), _full((1, d))],
        out_shape=[jax.ShapeDtypeStruct((t, d), F32), vec, vec, vec], vmem=BIG_VMEM,
    )(dy, w, x, nw, scale, dres)


def _pick_tn(n, k, budget=6 << 20):
    best = LANE
    for m in range(1, n // LANE + 1):
        tn = m * LANE
        if n % tn == 0 and k * tn * 4 <= budget:
            best = tn
    return best


def mm_tn(a, g, *, name, tt=1024):
    t, k = a.shape
    n = g.shape[1]
    tn = _pick_tn(n, k)

    def body(a_ref, g_ref, o_ref):
        @pl.when(pl.program_id(1) == 0)
        def _():
            o_ref[...] = jnp.zeros_like(o_ref)

        o_ref[...] += _dot(a_ref[...].astype(BF16), g_ref[...].astype(BF16), 0, 0)

    return _call(
        body, name=name, grid=(n // tn, t // tt),
        in_specs=[pl.BlockSpec((tt, k), lambda j, i: (i, 0)), pl.BlockSpec((tt, tn), lambda j, i: (i, j))],
        out_specs=pl.BlockSpec((k, tn), lambda j, i: (0, j)),
        out_shape=jax.ShapeDtypeStruct((k, n), F32), semantics=("parallel", "arbitrary"),
    )(a, g)


def loss_head(y, target, *, tm=512):
    t, d = y.shape
    steps = t // tm

    def body(y_ref, t_ref, dy_ref, l_ref, acc_ref):
        i = pl.program_id(0)

        @pl.when(i == 0)
        def _():
            acc_ref[...] = jnp.zeros_like(acc_ref)

        e = y_ref[...] - t_ref[...]
        dy_ref[...] = e * (1.0 / d)
        acc_ref[...] += _colsum(e * e)

        @pl.when(i == steps - 1)
        def _():
            tot = jnp.sum(acc_ref[...], axis=-1, keepdims=True) * (0.5 / d)
            l_ref[...] = jnp.broadcast_to(tot, l_ref.shape)

    row = lambda i: (i, 0)
    return _call(
        body, name="loss_head", grid=(steps,),
        in_specs=[pl.BlockSpec((tm, d), row), pl.BlockSpec((tm, d), row)],
        out_specs=[pl.BlockSpec((tm, d), row), _full((8, LANE))],
        out_shape=[jax.ShapeDtypeStruct((t, d), F32), jax.ShapeDtypeStruct((8, LANE), F32)],
        scratch_shapes=[pltpu.VMEM((1, d), F32)],
    )(y, target)


def _shift_rows(x, s):
    if s == 0:
        return x
    t = x.shape[0]
    ri = _iota2(x.shape, 0)
    rolled = pltpu.roll(x, s % t, axis=0)
    if s > 0:
        return jnp.where(ri >= s, rolled, 0.0)
    return jnp.where(ri < t + s, rolled, 0.0)


def _conv_pre(x, w):
    acc = x * w[DN_CONV - 1:DN_CONV, :]
    for j in range(DN_CONV - 1):
        acc = acc + _shift_rows(x, DN_CONV - 1 - j) * w[j:j + 1, :]
    return acc


def dn_conv(proj, conv_w):
    t = proj.shape[0]
    width = 3 * DN_WIDTH

    def body(x_ref, w_ref, o_ref):
        o_ref[...] = _silu(_conv_pre(x_ref[...], w_ref[...]))

    col = lambda j: (0, j)
    return _call(
        body, name="dn_conv", grid=(width // LANE,),
        in_specs=[pl.BlockSpec((t, LANE), col), pl.BlockSpec((8, LANE), col)],
        out_specs=pl.BlockSpec((t, LANE), col),
        out_shape=jax.ShapeDtypeStruct((t, width), F32), semantics=("parallel",),
    )(proj, conv_w)


def dn_conv_bwd(proj, conv_w, dact, dproj):
    t = proj.shape[0]
    width = 3 * DN_WIDTH

    def body(x_ref, w_ref, d_ref, _, dx_ref, dw_ref):
        x = x_ref[...]
        w = w_ref[...]
        dc = d_ref[...] * _dsilu(_conv_pre(x, w))
        dx = dc * w[DN_CONV - 1:DN_CONV, :]
        rows = []
        for j in range(DN_CONV - 1):
            s = DN_CONV - 1 - j
            dx = dx + _shift_rows(dc, -s) * w[j:j + 1, :]
            rows.append(_colsum(dc * _shift_rows(x, s)))
        rows.append(_colsum(dc * x))
        dx_ref[...] = dx.astype(BF16)
        ri = _iota2((8, LANE), 0)
        dw = jnp.zeros((8, LANE), F32)
        for j in range(DN_CONV):
            dw = dw + jnp.where(ri == j, rows[j], 0.0)
        dw_ref[...] = dw

    col = lambda j: (0, j)
    return _call(
        body, name="dn_conv_bwd", grid=(width // LANE,),
        in_specs=[pl.BlockSpec((t, LANE), col), pl.BlockSpec((8, LANE), col), pl.BlockSpec((t, LANE), col), ANY],
        out_specs=[pl.BlockSpec((t, LANE), col), pl.BlockSpec((8, LANE), col)],
        out_shape=[jax.ShapeDtypeStruct(dproj.shape, dproj.dtype), jax.ShapeDtypeStruct((8, width), F32)],
        semantics=("parallel",), aliases={3: 0},
    )(proj, conv_w, dact, dproj)


def _t(x):
    return jnp.swapaxes(x, -1, -2)


def _inv_unit_lower(a):
    c = a.shape[-1]
    eye = (_iota2((c, c), 0) == _iota2((c, c), 1)).astype(F32)
    x = eye - a
    p = _hdot(a, a)
    steps = int(math.log2(c)) - 1
    for i in range(steps):
        x = x + _hdot(x, p)
        if i < steps - 1:
            p = _hdot(p, p)
    return x


def _dn_local(q, k, v, a, b, alog, dtb, tinv=None):
    nh, c, d = q.shape
    rq = lax.rsqrt(_rowsum(q * q) + NORM_EPS)
    rk = lax.rsqrt(_rowsum(k * k) + NORM_EPS)
    qh = q * rq
    kn = k * rk
    qs = qh * (d ** -0.5)
    g = -jnp.exp(alog) * _softplus(a + dtb)
    beta = _sigmoid(b)
    ri = _iota2((c, c), 0)
    ci = _iota2((c, c), 1)
    causal = ri >= ci
    strict = ri > ci
    gb = jnp.broadcast_to(g, (nh, c, d))
    gcb = _xdot(jnp.broadcast_to(causal.astype(F32), (nh, c, c)), gb, exact=0)
    gc = gcb[..., :1]
    gl = _colsum(gb)[..., :1]
    dec = jnp.exp(jnp.where(causal, gc - _t(gcb)[:, :c, :], -1e30))
    kb = kn * beta
    amat = jnp.where(strict, _bdot(kb, kn, 1, 1) * dec, 0.0)
    if tinv is None:
        tinv = _inv_unit_lower(amat)
    e = jnp.exp(gc)
    f = jnp.exp(gl - gc)
    rw = kb * e
    sol = _hdot(tinv, jnp.concatenate([v * beta, rw], axis=-1))
    pmat = jnp.where(causal, _bdot(qs, kn, 1, 1) * dec, 0.0)
    return dict(rq=rq, rk=rk, qh=qh, kn=kn, qs=qs, g=g, beta=beta, causal=causal, strict=strict, gl=gl,
                dec=dec, kb=kb, amat=amat, tinv=tinv, e=e, f=f, rw=rw, u=sol[..., :d], w=sol[..., d:], pmat=pmat,
                qd=qs * e, kd=kn * f)


_DN_FIELDS = ("u", "w", "qd", "kd", "pmat", "gl")


def _dn_state(m, s_in):
    vnew = m["u"] - _bdot(m["w"], s_in)
    o = _bdot(m["qd"], s_in) + _bdot(m["pmat"], vnew)
    return vnew, o, s_in * jnp.exp(m["gl"]) + _bdot(m["kd"], vnew, 0, 0)


def _dn_state_bwd(m, s_in, do, ds_out):
    el = jnp.exp(m["gl"])
    dvnew = _bdot(m["pmat"], do, 0, 0) + _bdot(m["kd"], ds_out)
    dkd = _bdot(m["vnew"], ds_out, 1, 1)
    ds_in = _bdot(m["qd"], do, 0, 0) + el * ds_out - _bdot(m["w"], dvnew, 0, 0)
    dgl = el * _colsum(_rowsum(s_in * ds_out))
    return dvnew, dkd, dgl, ds_in


def _dn_local_bwd(m, q, v, a, alog, dtb, s_in, vnew, do, dvnew, dkd, dgl):
    nh, c, d = q.shape
    kn, qs, kb, u, w, e, f = m["kn"], m["qs"], m["kb"], m["u"], m["w"], m["e"], m["f"]
    beta, dec, tinv, kd, qd = m["beta"], m["dec"], m["tinv"], m["kd"], m["qd"]
    dp = jnp.where(m["causal"], _bdot(do, vnew, 1, 1), 0.0)
    dqd = _bdot(do, s_in, 1, 1)
    dw = -_bdot(dvnew, s_in, 1, 1)
    dsol = _hdot(tinv, jnp.concatenate([dvnew, dw], axis=-1), 0, 0)
    dru = dsol[..., :d]
    drw = dsol[..., d:]
    da_m = -jnp.where(m["strict"], _bdot(dsol, jnp.concatenate([u, w], axis=-1), 1, 1), 0.0)
    db_m = da_m * dec
    dq_m = dp * dec
    dkb = _bdot(db_m, kn)
    dkn = _bdot(db_m, kb, 0, 0) + _bdot(dq_m, qs, 0, 0)
    dqs = _bdot(dq_m, kn)
    gmat = da_m * m["amat"] + dp * m["pmat"]
    ones = jnp.ones((nh, c, d), F32)
    dgam = (_xdot(gmat, ones) - _xdot(gmat, ones, 0, 0))[..., :1]
    dqs = dqs + dqd * e
    dgam = dgam + _rowsum(dqd * qd)
    dkn = dkn + dkd * f
    tk = _rowsum(dkd * kd)
    dgam = dgam - tk
    dgl = dgl + _colsum(tk)
    dkb = dkb + drw * e
    dgam = dgam + _rowsum(drw * m["rw"])
    dv = dru * beta
    dbeta = _rowsum(dru * v) + _rowsum(dkb * kn)
    dkn = dkn + dkb * beta
    last = (_iota2((c, 1), 0) == c - 1).astype(F32)
    dgam = dgam + last * dgl
    upper = (_iota2((c, c), 0) <= _iota2((c, c), 1)).astype(F32)
    dg = _xdot(jnp.broadcast_to(upper, (nh, c, c)), jnp.broadcast_to(dgam, (nh, c, d)), exact=0)[..., :1]
    dqh = dqs * (d ** -0.5)
    dq = m["rq"] * (dqh - m["qh"] * _rowsum(dqh * m["qh"]))
    dk = m["rk"] * (dkn - kn * _rowsum(dkn * kn))
    sg = _sigmoid(a + dtb)
    da = dg * (-jnp.exp(alog)) * sg
    dalog = _colsum(dg * m["g"])
    ddtb = _colsum(da)
    db = dbeta * beta * (1.0 - beta)
    return dq, dk, dv, da, db, dalog, ddtb


def _dn_gate(o, z, wn):
    ro = lax.rsqrt(jnp.mean(o * o, axis=-1, keepdims=True) + NORM_EPS)
    n = o * ro
    return n, ro, n * wn * _silu(z)


DN_PAIR = 4


def _heads(ref, col0):
    d = DN_HEAD_DIM
    return jnp.stack([ref[j * DN_CHUNK:(j + 1) * DN_CHUNK, col0 + h * d:col0 + (h + 1) * d]
                      for j in range(DN_PAIR) for h in range(DN_HEADS)])


def _dn_inputs(act_ref, ab_ref, sc_ref):
    ab = ab_ref[...]
    sc = sc_ref[...]
    rows = lambda j: slice(j * DN_CHUNK, (j + 1) * DN_CHUNK)
    both = [(j, h) for j in range(DN_PAIR) for h in range(DN_HEADS)]
    q = _heads(act_ref, 0)
    k = _heads(act_ref, DN_WIDTH)
    v = _heads(act_ref, 2 * DN_WIDTH)
    a = jnp.stack([ab[rows(j), h:h + 1] for j, h in both])
    b = jnp.stack([ab[rows(j), DN_HEADS + h:DN_HEADS + h + 1] for j, h in both])
    alog = jnp.stack([sc[0:1, h:h + 1] for _, h in both])
    dtb = jnp.stack([sc[1:2, h:h + 1] for _, h in both])
    return q, k, v, a, b, alog, dtb


def _chunk_of(m, j, fields):
    return {f: m[f][j * DN_HEADS:(j + 1) * DN_HEADS] for f in fields}


def dn_fwd(act, proj, scal, wn):
    t = act.shape[0]
    n = t // DN_CHUNK
    d = DN_HEAD_DIM
    rows = DN_PAIR * DN_CHUNK

    def body(act_ref, z_ref, ab_ref, sc_ref, wn_ref, y_ref, st_ref, ti_ref, s_ref):
        @pl.when(pl.program_id(0) == 0)
        def _():
            s_ref[...] = jnp.zeros_like(s_ref)

        m = _dn_local(*_dn_inputs(act_ref, ab_ref, sc_ref))
        s = s_ref[...]
        outs = []
        for j in range(DN_PAIR):
            st_ref[j] = s
            ti_ref[j] = m["tinv"][j * DN_HEADS:(j + 1) * DN_HEADS]
            _, o, s = _dn_state(_chunk_of(m, j, _DN_FIELDS), s)
            outs.append(o)
        s_ref[...] = s
        y = _dn_gate(jnp.concatenate(outs, axis=0), _heads(z_ref, 0), wn_ref[...])[2]
        for j in range(DN_PAIR):
            for h in range(DN_HEADS):
                y_ref[j * DN_CHUNK:(j + 1) * DN_CHUNK, h * d:(h + 1) * d] = y[j * DN_HEADS + h]

    return _call(
        body, name="dn_fwd", grid=(n // DN_PAIR,),
        in_specs=[pl.BlockSpec((rows, 3 * DN_WIDTH), lambda i: (i, 0)),
                  pl.BlockSpec((rows, DN_WIDTH), lambda i: (i, C_Z // DN_WIDTH)),
                  pl.BlockSpec((rows, LANE), lambda i: (i, C_AB // LANE)),
                  _full((8, LANE)), _full((1, d))],
        out_specs=[pl.BlockSpec((rows, DN_WIDTH), lambda i: (i, 0)),
                   pl.BlockSpec((DN_PAIR, DN_HEADS, d, d), lambda i: (i, 0, 0, 0)),
                   pl.BlockSpec((DN_PAIR, DN_HEADS, DN_CHUNK, DN_CHUNK), lambda i: (i, 0, 0, 0))],
        out_shape=[jax.ShapeDtypeStruct((t, MIX_WIDTH), F32), jax.ShapeDtypeStruct((n, DN_HEADS, d, d), F32),
                   jax.ShapeDtypeStruct((n, DN_HEADS, DN_CHUNK, DN_CHUNK), F32)],
        scratch_shapes=[pltpu.VMEM((DN_HEADS, d, d), F32)],
    )(act, proj, proj, scal, wn)


def dn_bwd(act, proj, scal, wn, states, tinvs, dy):
    t = act.shape[0]
    n = t // DN_CHUNK
    steps = n // DN_PAIR
    d = DN_HEAD_DIM
    zab = DN_WIDTH + AB_PAD
    rows = DN_PAIR * DN_CHUNK

    def body(act_ref, z_ref, ab_ref, sc_ref, wn_ref, st_ref, ti_ref, dy_ref, dact_ref, dzab_ref, dpar_ref, ds_ref):
        @pl.when(pl.program_id(0) == 0)
        def _():
            ds_ref[...] = jnp.zeros_like(ds_ref)
            dpar_ref[...] = jnp.zeros_like(dpar_ref)

        wnv = wn_ref[...]
        q, k, v, a, b, alog, dtb = _dn_inputs(act_ref, ab_ref, sc_ref)
        batch = (DN_PAIR * DN_HEADS,)
        s_in = st_ref[...].reshape(batch + (d, d))
        m = _dn_local(q, k, v, a, b, alog, dtb, ti_ref[...].reshape(batch + (DN_CHUNK, DN_CHUNK)))
        vnew, o, _ = _dn_state(m, s_in)
        z = _heads(z_ref, 0)
        dyh = _heads(dy_ref, 0)
        nrm, ro, _ = _dn_gate(o, z, wnv)
        sz = _silu(z)
        dz = dyh * nrm * wnv * _dsilu(z)
        dn = dyh * wnv * sz
        dwn = _colsum(dyh * nrm * sz)
        do = ro * (dn - nrm * jnp.mean(dn * nrm, axis=-1, keepdims=True))
        ds = ds_ref[...]
        parts = [None] * DN_PAIR
        for j in reversed(range(DN_PAIR)):
            mj = dict(_chunk_of(m, j, _DN_FIELDS), vnew=vnew[j * DN_HEADS:(j + 1) * DN_HEADS])
            dvnew, dkd, dgl, ds = _dn_state_bwd(mj, s_in[j * DN_HEADS:(j + 1) * DN_HEADS],
                                                do[j * DN_HEADS:(j + 1) * DN_HEADS], ds)
            parts[j] = (dvnew, dkd, dgl)
        ds_ref[...] = ds
        dvnew, dkd, dgl = (jnp.concatenate([p[i] for p in parts], axis=0) for i in range(3))
        dq, dk, dv, da, db, dalog, ddtb = _dn_local_bwd(m, q, v, a, alog, dtb, s_in, vnew, do, dvnew, dkd, dgl)
        lane = _iota2((DN_CHUNK, LANE), 1)
        prow = _iota2((8, LANE), 0)
        plane = _iota2((8, LANE), 1)
        dpar = jnp.zeros((8, LANE), F32)
        for j in range(DN_PAIR):
            rs = slice(j * DN_CHUNK, (j + 1) * DN_CHUNK)
            dab = jnp.zeros((DN_CHUNK, LANE), F32)
            for h in range(DN_HEADS):
                n_ = j * DN_HEADS + h
                dzab_ref[rs, h * d:(h + 1) * d] = dz[n_].astype(BF16)
                dact_ref[rs, h * d:(h + 1) * d] = dq[n_]
                dact_ref[rs, DN_WIDTH + h * d:DN_WIDTH + (h + 1) * d] = dk[n_]
                dact_ref[rs, 2 * DN_WIDTH + h * d:2 * DN_WIDTH + (h + 1) * d] = dv[n_]
                dab = dab + jnp.where(lane == h, da[n_], 0.0) + jnp.where(lane == DN_HEADS + h, db[n_], 0.0)
                dpar = dpar + jnp.where((prow == 0) & (plane == h), dalog[n_], 0.0)
                dpar = dpar + jnp.where((prow == 1) & (plane == h), ddtb[n_], 0.0)
                dpar = dpar + jnp.where(prow == 2, dwn[n_], 0.0)
            dzab_ref[rs, DN_WIDTH:DN_WIDTH + LANE] = dab.astype(BF16)
            dzab_ref[rs, DN_WIDTH + LANE:] = jnp.zeros((DN_CHUNK, AB_PAD - LANE), BF16)
        dpar_ref[...] += dpar

    rev = lambda i: (steps - 1 - i, 0)
    rev4 = lambda i: (steps - 1 - i, 0, 0, 0)
    return _call(
        body, name="dn_bwd", grid=(steps,),
        in_specs=[pl.BlockSpec((rows, 3 * DN_WIDTH), rev),
                  pl.BlockSpec((rows, DN_WIDTH), lambda i: (steps - 1 - i, C_Z // DN_WIDTH)),
                  pl.BlockSpec((rows, LANE), lambda i: (steps - 1 - i, C_AB // LANE)),
                  _full((8, LANE)), _full((1, d)),
                  pl.BlockSpec((DN_PAIR, DN_HEADS, d, d), rev4),
                  pl.BlockSpec((DN_PAIR, DN_HEADS, DN_CHUNK, DN_CHUNK), rev4),
                  pl.BlockSpec((rows, DN_WIDTH), rev)],
        out_specs=[pl.BlockSpec((rows, 3 * DN_WIDTH), rev),
                   pl.BlockSpec((rows, zab), lambda i: (steps - 1 - i, C_Z // zab)), _full((8, LANE))],
        out_shape=[jax.ShapeDtypeStruct((t, 3 * DN_WIDTH), F32), jax.ShapeDtypeStruct((t, IN_PAD), BF16),
                   jax.ShapeDtypeStruct((8, LANE), F32)],
        scratch_shapes=[pltpu.VMEM((DN_HEADS, d, d), F32)],
    )(act, proj, proj, scal, wn, states, tinvs, dy)


_INV_SQRT2 = 0.7071067811865476
_INV_SQRT2PI = 0.3989422804014327


def _gelu(x):
    return 0.5 * x * (1.0 + lax.erf(x * _INV_SQRT2))


def _dgelu(x):
    return 0.5 * (1.0 + lax.erf(x * _INV_SQRT2)) + x * jnp.exp(-0.5 * x * x) * _INV_SQRT2PI


def _gm_core(uv, lng, lnb, ws_ref, bst):
    c = uv.shape[0]
    zz = _gelu(uv)
    u = zz[:, :GM_WIDTH]
    vv = zz[:, GM_WIDTH:]
    xc = vv - jnp.mean(vv, axis=-1, keepdims=True)
    rs = lax.rsqrt(jnp.mean(xc * xc, axis=-1, keepdims=True) + NORM_EPS)
    xh = xc * rs
    vn = xh * lng + lnb
    grp = _iota2((c, GM_WIDTH), 1) // GM_GROUP_DIM
    tril = _iota2((c, c), 0) >= _iota2((c, c), 1)
    sv = jnp.zeros((c, GM_WIDTH), F32)
    masks = []
    for g in range(GM_GROUPS):
        mk = grp == g
        masks.append(mk)
        ws = jnp.where(tril, ws_ref[g], 0.0)
        sv = sv + _bdot(ws, jnp.where(mk, vn, 0.0)) + jnp.where(mk, bst[:, g:g + 1], 0.0)
    return u, xh, rs, vn, sv, masks, tril


def gm_fwd(proj, lng, lnb, w_s, bst, ybuf):
    t = proj.shape[0]

    def body(uv_ref, g_ref, b_ref, ws_ref, bst_ref, _, y_ref):
        u, _, _, _, sv, _, _ = _gm_core(uv_ref[...], g_ref[...], b_ref[...], ws_ref, bst_ref[...])
        y_ref[...] = u * sv

    return _call(
        body, name="gm_fwd", grid=(t // GM_CHUNK,),
        in_specs=[pl.BlockSpec((GM_CHUNK, 2 * GM_WIDTH), lambda i: (i, C_UV // (2 * GM_WIDTH))),
                  _full((1, GM_WIDTH)), _full((1, GM_WIDTH)), _full((GM_GROUPS, GM_CHUNK, GM_CHUNK)),
                  _full((GM_CHUNK, LANE)), ANY],
        out_specs=pl.BlockSpec((GM_CHUNK, GM_WIDTH), lambda i: (i, DN_WIDTH // GM_WIDTH)),
        out_shape=jax.ShapeDtypeStruct(ybuf.shape, F32), semantics=("parallel",), aliases={5: 0},
    )(proj, lng, lnb, w_s, bst, ybuf)


def gm_bwd(proj, lng, lnb, w_s, bst, dy, dproj):
    t = proj.shape[0]

    def body(uv_ref, g_ref, b_ref, ws_ref, bst_ref, dy_ref, _, duv_ref, dws_ref, dbst_ref, dln_ref):
        @pl.when(pl.program_id(0) == 0)
        def _():
            dws_ref[...] = jnp.zeros_like(dws_ref)
            dbst_ref[...] = jnp.zeros_like(dbst_ref)
            dln_ref[...] = jnp.zeros_like(dln_ref)

        uv = uv_ref[...]
        lng = g_ref[...]
        u, xh, rs, vn, sv, masks, tril = _gm_core(uv, lng, b_ref[...], ws_ref, bst_ref[...])
        dyv = dy_ref[...]
        dsv = dyv * u
        lane = _iota2((GM_CHUNK, LANE), 1)
        dvn = jnp.zeros_like(dsv)
        dbst = jnp.zeros((GM_CHUNK, LANE), F32)
        for g in range(GM_GROUPS):
            ws = jnp.where(tril, ws_ref[g], 0.0)
            dsg = jnp.where(masks[g], dsv, 0.0)
            dvn = dvn + jnp.where(masks[g], _bdot(ws, dsv, 0, 0), 0.0)
            dws_ref[g] += jnp.where(tril, _bdot(dsg, vn, 1, 1), 0.0)
            dbst = dbst + jnp.where(lane == g, _rowsum(dsg), 0.0)
        dbst_ref[...] += dbst
        row = _iota2((8, GM_WIDTH), 0)
        dln_ref[...] += jnp.where(row == 0, _colsum(dvn * xh), 0.0) + jnp.where(row == 1, _colsum(dvn), 0.0)
        dxh = dvn * lng
        dvv = rs * (dxh - jnp.mean(dxh, axis=-1, keepdims=True) - xh * jnp.mean(dxh * xh, axis=-1, keepdims=True))
        dg = _dgelu(uv)
        duv_ref[:, :GM_WIDTH] = (dyv * sv * dg[:, :GM_WIDTH]).astype(BF16)
        duv_ref[:, GM_WIDTH:] = (dvv * dg[:, GM_WIDTH:]).astype(BF16)

    return _call(
        body, name="gm_bwd", grid=(t // GM_CHUNK,),
        in_specs=[pl.BlockSpec((GM_CHUNK, 2 * GM_WIDTH), lambda i: (i, C_UV // (2 * GM_WIDTH))),
                  _full((1, GM_WIDTH)), _full((1, GM_WIDTH)), _full((GM_GROUPS, GM_CHUNK, GM_CHUNK)),
                  _full((GM_CHUNK, LANE)),
                  pl.BlockSpec((GM_CHUNK, GM_WIDTH), lambda i: (i, DN_WIDTH // GM_WIDTH)), ANY],
        out_specs=[pl.BlockSpec((GM_CHUNK, 2 * GM_WIDTH), lambda i: (i, C_UV // (2 * GM_WIDTH))),
                   _full((GM_GROUPS, GM_CHUNK, GM_CHUNK)), _full((GM_CHUNK, LANE)), _full((8, GM_WIDTH))],
        out_shape=[jax.ShapeDtypeStruct(dproj.shape, dproj.dtype),
                   jax.ShapeDtypeStruct((GM_GROUPS, GM_CHUNK, GM_CHUNK), F32),
                   jax.ShapeDtypeStruct((GM_CHUNK, LANE), F32), jax.ShapeDtypeStruct((8, GM_WIDTH), F32)],
        aliases={6: 0},
    )(proj, lng, lnb, w_s, bst, dy, dproj)


def _head_mats():
    r = _iota2((SW_WIDTH, SW_WIDTH), 0)
    c = _iota2((SW_WIDTH, SW_WIDTH), 1)
    same = (r // SW_HEAD_DIM) == (c // SW_HEAD_DIM)
    cc = c % SW_HEAD_DIM
    half = ROPE_DIM // 2
    rot = jnp.where((cc < half) & (r == c + half), -1.0, 0.0) + jnp.where((cc >= half) & (cc < ROPE_DIM) & (r == c - half), 1.0, 0.0)
    return same.astype(F32), rot


def _seg_col(s):
    return C_SW // SW_WIDTH + (s // 2) * 3 + s % 2


def _halves(x):
    return x[:, :LANE], x[:, LANE:]


def sw_prep(proj, nw2, cos_t, sin_t, *, tm=512):
    t = proj.shape[0]

    def body(x_ref, w_ref, c_ref, s_ref, o_ref):
        same, rot = _head_mats()
        x = x_ref[...]
        r = lax.rsqrt(_xdot(x * x, same) * (1.0 / SW_HEAD_DIM) + NORM_EPS)
        xn = x * r * w_ref[0]
        o_ref[0, 0], o_ref[0, 1] = _halves(xn * c_ref[...] + _xdot(xn, rot) * s_ref[...])

    return _call(
        body, name="sw_prep", grid=(6, t // tm),
        in_specs=[pl.BlockSpec((tm, SW_WIDTH), lambda s, i: (i, _seg_col(s))),
                  pl.BlockSpec((1, 1, SW_WIDTH), lambda s, i: (s % 2, 0, 0)),
                  pl.BlockSpec((tm, SW_WIDTH), lambda s, i: (i, 0)),
                  pl.BlockSpec((tm, SW_WIDTH), lambda s, i: (i, 0))],
        out_specs=pl.BlockSpec((1, 2, tm, LANE), lambda s, i: (s, 0, i, 0)),
        out_shape=jax.ShapeDtypeStruct((6, 2, t, LANE), F32), semantics=("parallel", "parallel"),
    )(proj, nw2, cos_t, sin_t)


def sw_prep_bwd(proj, nw2, cos_t, sin_t, dkvq, dproj, dnw, p, *, tm=512):
    t = proj.shape[0]
    col0 = C_SW // SW_WIDTH + 3 * p
    seg_col = lambda s: col0 + (s + 1) % 3

    def body(x_ref, w_ref, c_ref, s_ref, d_ref, _, dw0_ref, dx_ref, dw_ref):
        s = pl.program_id(0)
        dout = jnp.concatenate([d_ref[0, 0], d_ref[0, 1]], axis=1)

        @pl.when(s == 1)
        def _():
            dx_ref[...] = dout.astype(BF16)

        @pl.when((s != 1) & (pl.program_id(1) == 0))
        def _():
            dw_ref[...] = dw0_ref[...]

        @pl.when(s != 1)
        def _():
            same, rot = _head_mats()
            x = x_ref[...]
            w = w_ref[0]
            r = lax.rsqrt(_xdot(x * x, same) * (1.0 / SW_HEAD_DIM) + NORM_EPS)
            xh = x * r
            dxn = dout * c_ref[...] + _xdot(dout * s_ref[...], rot, 1, 1)
            dw_ref[0] += _colsum(dxn * xh)
            dxh = dxn * w
            dx_ref[...] = (r * (dxh - xh * (_xdot(dxh * xh, same) * (1.0 / SW_HEAD_DIM)))).astype(BF16)

    return _call(
        body, name=f"sw_prep_bwd{p}", grid=(3, t // tm),
        in_specs=[pl.BlockSpec((tm, SW_WIDTH), lambda s, i: (i, seg_col(s))),
                  pl.BlockSpec((1, 1, SW_WIDTH), lambda s, i: (1 - s // 2, 0, 0)),
                  pl.BlockSpec((tm, SW_WIDTH), lambda s, i: (i, 0)),
                  pl.BlockSpec((tm, SW_WIDTH), lambda s, i: (i, 0)),
                  pl.BlockSpec((1, 2, tm, LANE), lambda s, i: (s, 0, i, 0)), ANY,
                  pl.BlockSpec((1, 1, SW_WIDTH), lambda s, i: (s // 2, 0, 0))],
        out_specs=[pl.BlockSpec((tm, SW_WIDTH), lambda s, i: (i, seg_col(s))),
                   pl.BlockSpec((1, 1, SW_WIDTH), lambda s, i: (s // 2, 0, 0))],
        out_shape=[jax.ShapeDtypeStruct(dproj.shape, dproj.dtype), jax.ShapeDtypeStruct((2, 1, SW_WIDTH), F32)],
        semantics=("arbitrary", "arbitrary"), aliases={5: 0},
    )(proj, nw2, cos_t, sin_t, dkvq, dproj, dnw)


_SW_SCALE = SW_HEAD_DIM ** -0.5
_NEG = -1e30


def _sw_masks(has_other):
    ri = _iota2((SW_BLOCK, SW_BLOCK), 0)
    ci = _iota2((SW_BLOCK, SW_BLOCK), 1)
    return ri >= ci, (ci >= ri) & has_other


def _pair(x):
    first = _iota2((1, LANE), 1) < SW_HEAD_DIM
    return jnp.stack([jnp.where(first, x, 0.0), jnp.where(first, 0.0, x)])


def _both(x):
    return jnp.broadcast_to(x.astype(BF16)[None], (2,) + x.shape)


def _unpair(x2):
    first = _iota2((1, LANE), 1) < SW_HEAD_DIM
    return jnp.where(first, x2[0], x2[1])


def _head_cols(x):
    return jnp.stack([x[:, 0:1], x[:, SW_HEAD_DIM:SW_HEAD_DIM + 1]])


SW_GROUP = 8


def _sw_geometry(t, p):
    dil = SW_DILATIONS[p]
    unit = SW_BLOCK * dil
    nb = max(1, SW_GROUP // dil)
    return dil, unit, nb, t // (unit * nb)


def _sw_groups(dil, nb, body):
    if nb * dil == SW_GROUP:
        body([(k // dil, k % dil) for k in range(SW_GROUP)])
    else:
        for g in range(nb * dil // SW_GROUP):
            body([(0, SW_GROUP * g + k) for k in range(SW_GROUP)])


def _sw_rows(i, r, dil):
    start = i * SW_BLOCK * dil + r
    return pl.ds(start, SW_BLOCK) if dil == 1 else pl.ds(start, SW_BLOCK, stride=dil)


def _sw_load(refs, probs, dil, shift, wrap, fn):
    out = []
    for i, r in probs:
        if shift != 0 and i == wrap:
            out.append(fn(refs[1][_sw_rows(0, r, dil), :]))
        else:
            out.append(fn(refs[0][_sw_rows(i + shift, r, dil), :]))
    return jnp.concatenate(out, axis=0)


def _sw_other_masks(probs, wrap, edge_ok):
    _, other = _sw_masks(edge_ok)
    _, always = _sw_masks(True)
    return jnp.stack([other if i == wrap else always for i, _ in probs for _ in range(2)])


def sw_attn(qk, proj, p):
    t = proj.shape[0]
    dil, unit, nb, nsp = _sw_geometry(t, p)
    vcol = (C_SW + 3 * SW_WIDTH * p + 2 * SW_WIDTH) // LANE

    def body(q_ref, kc_ref, kp_ref, vc_ref, vp_ref, o_ref, l_ref):
        mc, _ = _sw_masks(True)
        first = pl.program_id(1) != 0
        q_r, k_r, v_r = (q_ref.at[0, 0], None), (kc_ref.at[0, 0], kp_ref.at[0, 0]), (vc_ref, vp_ref)

        def one(probs):
            mp = _sw_other_masks(probs, 0, first)
            q2 = _sw_load(q_r, probs, dil, 0, 0, _pair)
            sc = jnp.where(mc, _bdot(q2, _sw_load(k_r, probs, dil, 0, 0, _both), 1, 1) * _SW_SCALE, _NEG)
            sp = jnp.where(mp, _bdot(q2, _sw_load(k_r, probs, dil, -1, 0, _both), 1, 1) * _SW_SCALE, _NEG)
            mx = jnp.maximum(jnp.max(sc, axis=-1, keepdims=True), jnp.max(sp, axis=-1, keepdims=True))
            pc = jnp.exp(sc - mx)
            pp = jnp.exp(sp - mx)
            den = _rowsum(pc) + _rowsum(pp)
            o2 = (_bdot(pc, _sw_load(v_r, probs, dil, 0, 0, _both))
                  + _bdot(pp, _sw_load(v_r, probs, dil, -1, 0, _both))) * (1.0 / den)
            l2 = jnp.broadcast_to(mx + jnp.log(den), o2.shape)
            for n, (i, r) in enumerate(probs):
                o_ref.at[0][_sw_rows(i, r, dil), :] = _unpair(o2[2 * n:2 * n + 2])
                l_ref.at[0][_sw_rows(i, r, dil), :] = _unpair(l2[2 * n:2 * n + 2])

        _sw_groups(dil, nb, one)

    before = lambda j: jnp.maximum(j * nb - 1, 0)
    seg = lambda s: pl.BlockSpec((1, 1, unit * nb, LANE), lambda h, j: (s, h, j, 0))
    seg_b = lambda s: pl.BlockSpec((1, 1, unit, LANE), lambda h, j: (s, h, before(j), 0))
    out = pl.BlockSpec((1, unit * nb, LANE), lambda h, j: (h, j, 0))
    shp = jax.ShapeDtypeStruct((2, t, LANE), F32)
    return _call(
        body, name=f"sw_attn{p}", grid=(2, nsp),
        in_specs=[seg(2 * p), seg(2 * p + 1), seg_b(2 * p + 1),
                  pl.BlockSpec((unit * nb, LANE), lambda h, j: (j, vcol + h)),
                  pl.BlockSpec((unit, LANE), lambda h, j: (before(j), vcol + h))],
        out_specs=[out, out], out_shape=[shp, shp], semantics=("parallel", "parallel"),
    )(qk, qk, qk, proj, proj)


def sw_attn_dkv(qk, proj, dy, lg, dm, p):
    t = proj.shape[0]
    dil, unit, nb, nsp = _sw_geometry(t, p)
    nunits = t // unit
    vcol = (C_SW + 3 * SW_WIDTH * p + 2 * SW_WIDTH) // LANE
    ycol = (DN_WIDTH + GM_WIDTH) // LANE

    def body(k_ref, v_ref, qc_ref, qn_ref, doc_ref, don_ref, lc_ref, ln_ref, dc_ref, dn_ref, o_ref):
        mc, _ = _sw_masks(True)
        more = pl.program_id(1) + 1 < nsp
        q_r, do_r = (qc_ref.at[0, 0], qn_ref.at[0, 0]), (doc_ref, don_ref)
        l_r, d_r = (lc_ref.at[0], ln_ref.at[0]), (dc_ref.at[0], dn_ref.at[0])

        def one(probs):
            k2 = _sw_load((k_ref.at[0, 0], None), probs, dil, 0, 0, _both)
            v2 = _sw_load((v_ref, None), probs, dil, 0, 0, _both)
            dk = jnp.zeros((2 * SW_GROUP, SW_BLOCK, LANE), F32)
            dv = jnp.zeros((2 * SW_GROUP, SW_BLOCK, LANE), F32)
            for shift, mk in ((0, mc), (1, _sw_other_masks(probs, nb - 1, more))):
                q2 = _sw_load(q_r, probs, dil, shift, nb - 1, _pair)
                do2 = _sw_load(do_r, probs, dil, shift, nb - 1, _pair)
                lse = _sw_load(l_r, probs, dil, shift, nb - 1, _head_cols)
                dd = _sw_load(d_r, probs, dil, shift, nb - 1, _head_cols)
                pr = jnp.exp(jnp.where(mk, _bdot(q2, k2, 1, 1) * _SW_SCALE, _NEG) - lse)
                dv = dv + _bdot(pr, do2, 0, 0)
                ds = pr * (_bdot(do2, v2, 1, 1) - dd)
                dk = dk + _bdot(ds, q2, 0, 0)
            for n, (i, r) in enumerate(probs):
                o_ref.at[0, 0][_sw_rows(i, r, dil), :] = (dk[2 * n] + dk[2 * n + 1]) * _SW_SCALE
                o_ref.at[1, 0][_sw_rows(i, r, dil), :] = dv[2 * n] + dv[2 * n + 1]

        _sw_groups(dil, nb, one)

    after = lambda j: jnp.minimum((j + 1) * nb, nunits - 1)
    seg = lambda s: pl.BlockSpec((1, 1, unit * nb, LANE), lambda h, j: (s, h, j, 0))
    seg_a = lambda s: pl.BlockSpec((1, 1, unit, LANE), lambda h, j: (s, h, after(j), 0))
    col = lambda c0: pl.BlockSpec((unit * nb, LANE), lambda h, j: (j, c0 + h))
    col_a = lambda c0: pl.BlockSpec((unit, LANE), lambda h, j: (after(j), c0 + h))
    hp = pl.BlockSpec((1, unit * nb, LANE), lambda h, j: (h, j, 0))
    hp_a = pl.BlockSpec((1, unit, LANE), lambda h, j: (h, after(j), 0))
    return _call(
        body, name=f"sw_dkv{p}", grid=(2, nsp),
        in_specs=[seg(2 * p + 1), col(vcol), seg(2 * p), seg_a(2 * p), col(ycol), col_a(ycol), hp, hp_a, hp, hp_a],
        out_specs=pl.BlockSpec((2, 1, unit * nb, LANE), lambda h, j: (0, h, j, 0)),
        out_shape=jax.ShapeDtypeStruct((3, 2, t, LANE), F32), semantics=("parallel", "parallel"),
    )(qk, proj, qk, qk, dy, dy, lg, lg, dm, dm)


def sw_attn_dq(qk, proj, dy, lg, dm, dkvq, p):
    t = proj.shape[0]
    dil, unit, nb, nsp = _sw_geometry(t, p)
    vcol = (C_SW + 3 * SW_WIDTH * p + 2 * SW_WIDTH) // LANE
    ycol = (DN_WIDTH + GM_WIDTH) // LANE

    def body(q_ref, kc_ref, kp_ref, vc_ref, vp_ref, do_ref, l_ref, d_ref, _, dq_ref):
        mc, _ = _sw_masks(True)
        first = pl.program_id(1) != 0
        k_r, v_r = (kc_ref.at[0, 0], kp_ref.at[0, 0]), (vc_ref, vp_ref)

        def one(probs):
            mp = _sw_other_masks(probs, 0, first)
            q2 = _sw_load((q_ref.at[0, 0], None), probs, dil, 0, 0, _pair)
            do2 = _sw_load((do_ref, None), probs, dil, 0, 0, _pair)
            lse = _sw_load((l_ref.at[0], None), probs, dil, 0, 0, _head_cols)
            dd = _sw_load((d_ref.at[0], None), probs, dil, 0, 0, _head_cols)
            kc = _sw_load(k_r, probs, dil, 0, 0, _both)
            kp = _sw_load(k_r, probs, dil, -1, 0, _both)
            pc = jnp.exp(jnp.where(mc, _bdot(q2, kc, 1, 1) * _SW_SCALE, _NEG) - lse)
            pp = jnp.exp(jnp.where(mp, _bdot(q2, kp, 1, 1) * _SW_SCALE, _NEG) - lse)
            dsc = pc * (_bdot(do2, _sw_load(v_r, probs, dil, 0, 0, _both), 1, 1) - dd)
            dsp = pp * (_bdot(do2, _sw_load(v_r, probs, dil, -1, 0, _both), 1, 1) - dd)
            dq2 = (_bdot(dsc, kc) + _bdot(dsp, kp)) * _SW_SCALE
            for n, (i, r) in enumerate(probs):
                dq_ref.at[0, 0][_sw_rows(i, r, dil), :] = _unpair(dq2[2 * n:2 * n + 2])

        _sw_groups(dil, nb, one)

    before = lambda j: jnp.maximum(j * nb - 1, 0)
    seg = lambda s: pl.BlockSpec((1, 1, unit * nb, LANE), lambda h, j: (s, h, j, 0))
    seg_b = lambda s: pl.BlockSpec((1, 1, unit, LANE), lambda h, j: (s, h, before(j), 0))
    col = lambda c0: pl.BlockSpec((unit * nb, LANE), lambda h, j: (j, c0 + h))
    col_b = lambda c0: pl.BlockSpec((unit, LANE), lambda h, j: (before(j), c0 + h))
    hp = pl.BlockSpec((1, unit * nb, LANE), lambda h, j: (h, j, 0))
    return _call(
        body, name=f"sw_dq{p}", grid=(2, nsp),
        in_specs=[seg(2 * p), seg(2 * p + 1), seg_b(2 * p + 1), col(vcol), col_b(vcol), col(ycol), hp, hp, ANY],
        out_specs=pl.BlockSpec((1, 1, unit * nb, LANE), lambda h, j: (2, h, j, 0)),
        out_shape=jax.ShapeDtypeStruct(dkvq.shape, F32), semantics=("parallel", "parallel"), aliases={8: 0},
    )(qk, qk, qk, proj, proj, dy, lg, dm, dkvq)


def sw_merge(outs, lses, ybuf, *, tm=512):
    t = ybuf.shape[0]

    def body(o0, o1, o2, l0_ref, l1_ref, l2_ref, _, y_ref, lg_ref):
        l0, l1, l2 = l0_ref[...], l1_ref[...], l2_ref[...]
        mx = jnp.maximum(jnp.maximum(l0, l1), l2)
        lg = mx + jnp.log(jnp.exp(l0 - mx) + jnp.exp(l1 - mx) + jnp.exp(l2 - mx))
        lg_ref[...] = lg
        y = jnp.exp(l0 - lg) * o0[...] + jnp.exp(l1 - lg) * o1[...] + jnp.exp(l2 - lg) * o2[...]
        y_ref[...] = jnp.concatenate([y[0], y[1]], axis=1)

    hp = pl.BlockSpec((2, tm, LANE), lambda i: (0, i, 0))
    return _call(
        body, name="sw_merge", grid=(t // tm,), in_specs=[hp] * 6 + [ANY],
        out_specs=[pl.BlockSpec((tm, SW_WIDTH), lambda i: (i, (DN_WIDTH + GM_WIDTH) // SW_WIDTH)), hp],
        out_shape=[jax.ShapeDtypeStruct(ybuf.shape, F32), jax.ShapeDtypeStruct((2, t, LANE), F32)],
        semantics=("parallel",), aliases={6: 0},
    )(*outs, *lses, ybuf)


def sw_delta(dy, ybuf, *, tm=512):
    t = ybuf.shape[0]

    def body(dy_ref, y_ref, o_ref):
        same, _ = _head_mats()
        o_ref[0], o_ref[1] = _halves(_xdot(dy_ref[...] * y_ref[...], same))

    b1 = pl.BlockSpec((tm, SW_WIDTH), lambda i: (i, (DN_WIDTH + GM_WIDTH) // SW_WIDTH))
    return _call(body, name="sw_delta", grid=(t // tm,), in_specs=[b1, b1],
                 out_specs=pl.BlockSpec((2, tm, LANE), lambda i: (0, i, 0)),
                 out_shape=jax.ShapeDtypeStruct((2, t, LANE), F32), semantics=("parallel",))(dy, ybuf)


def _rope_tables(t):
    inv = ROPE_THETA ** (-jnp.arange(0, ROPE_DIM, 2, dtype=F32) / ROPE_DIM)
    ang = jnp.arange(t, dtype=F32)[:, None] * inv[None, :]
    pad1 = jnp.ones((t, SW_HEAD_DIM - ROPE_DIM), F32)
    pad0 = jnp.zeros((t, SW_HEAD_DIM - ROPE_DIM), F32)
    cos_h = jnp.concatenate([jnp.cos(ang), jnp.cos(ang), pad1], axis=1)
    sin_h = jnp.concatenate([jnp.sin(ang), jnp.sin(ang), pad0], axis=1)
    return jnp.tile(cos_h, (1, SW_HEADS)), jnp.tile(sin_h, (1, SW_HEADS))


def sw_forward(proj, nw2, cos_t, sin_t, ybuf):
    qk = sw_prep(proj, nw2, cos_t, sin_t)
    outs, lses = [], []
    for p in range(len(SW_DILATIONS)):
        o, lse = sw_attn(qk, proj, p)
        outs.append(o)
        lses.append(lse)
    ybuf, lg = sw_merge(outs, lses, ybuf)
    return ybuf, (qk, lg)


def sw_backward(proj, nw2, cos_t, sin_t, res, ybuf, dy, dproj):
    qk, lg = res
    dm = sw_delta(dy, ybuf)
    dnw = jnp.zeros((2, 1, SW_WIDTH), F32)
    for p in range(len(SW_DILATIONS)):
        dkvq = sw_attn_dkv(qk, proj, dy, lg, dm, p)
        dkvq = sw_attn_dq(qk, proj, dy, lg, dm, dkvq, p)
        dproj, dnw = sw_prep_bwd(proj, nw2, cos_t, sin_t, dkvq, dproj, dnw, p)
    return dproj, dnw[::-1, 0]


def _pad_rows(a, rows):
    return jnp.zeros((rows,) + a.shape[1:], a.dtype).at[:a.shape[0]].set(a)


def _consts(sp):
    d = {}
    d["mix_nw"] = sp["mix_norm_w"][:, None, :]
    d["ffn_nw"] = sp["ffn_norm_w"][:, None, :]
    d["cw8"] = jnp.pad(sp["dn_conv_w"], ((0, 0), (0, 8 - DN_CONV), (0, 0)))
    d["scal"] = jnp.pad(jnp.stack([sp["dn_a_log"], sp["dn_dt_bias"]], axis=1), ((0, 0), (0, 6), (0, LANE - DN_HEADS)))
    d["wn"] = sp["dn_out_norm_w"][:, None, :]
    d["lng"] = sp["gm_ln_g"][:, None, :]
    d["lnb"] = sp["gm_ln_b"][:, None, :]
    d["w_s"] = sp["gm_w_s"]
    d["bst"] = jnp.pad(jnp.swapaxes(sp["gm_b_s"], 1, 2), ((0, 0), (0, 0), (0, LANE - GM_GROUPS)))
    d["nw2"] = jnp.stack([jnp.tile(sp["sw_q_norm_w"], (1, SW_HEADS)),
                          jnp.tile(sp["sw_k_norm_w"], (1, SW_HEADS))], axis=1)[:, :, None, :]
    return d


def _layer_fwd(x, mod, get_w, cs, tabs):
    wb = dict(get_w("w_in", x))
    h1, proj = norm_mm(x, cs["mix_nw"], mod[1], mod[0], wb["w_in"], swiglu=False, name="in_proj")
    act = dn_conv(proj, cs["cw8"])
    y, states, tinvs = dn_fwd(act, proj, cs["scal"], cs["wn"])
    y = gm_fwd(proj, cs["lng"], cs["lnb"], cs["w_s"], cs["bst"], y)
    y, swres = sw_forward(proj, cs["nw2"], *tabs, y)
    wb.update(get_w("w_out", y))
    x1, o1 = resid_mm(y, wb["w_out"], x, mod[2], name="out_proj")
    wb.update(get_w("ffn", x1))
    h2, gu, actf = norm_mm(x1, cs["ffn_nw"], mod[4], mod[3], wb["w_ffn_in"], swiglu=True, name="ffn_in")
    x2, o2 = resid_mm(actf, wb["w_ffn_out"], x1, mod[5], name="ffn_out")
    res = dict(x=x, h1=h1, proj=proj, act=act, states=states, tinvs=tinvs, swres=swres, y=y, x1=x1, o1=o1, h2=h2, gu=gu,
               actf=actf, o2=o2)
    return x2, res, wb


def _layer_bwd(dx2, res, mod, wb, cs, tabs, grads_done):
    dgu, gx2, dgate2 = resid_mm_bwd(dx2, mod[5], res["o2"], wb["w_ffn_out"], res["gu"], name="ffn_out_bwd", tm=512)
    g_wfo = mm_tn(res["actf"], gx2, name="wg_ffn_out")
    g_wfi = mm_tn(res["h2"], dgu, name="wg_ffn_in")
    token = grads_done("ffn", dict(w_ffn_in=g_wfi, w_ffn_out=g_wfo))
    dx1, d_ffn_nw, dscale2, dshift2 = norm_mm_bwd(dgu, wb["w_ffn_in"], res["x1"], cs["ffn_nw"], mod[4] + token, dx2,
                                                  name="ffn_in_bwd")
    dy, gx1, dgate1 = resid_mm_bwd(dx1, mod[2], res["o1"], wb["w_out"], None, name="out_proj_bwd", tm=512)
    g_wout = mm_tn(res["y"], gx1, name="wg_out")
    proj = res["proj"]
    dact, dproj, dpar = dn_bwd(res["act"], proj, cs["scal"], cs["wn"], res["states"], res["tinvs"], dy)
    dproj, dcw = dn_conv_bwd(proj, cs["cw8"], dact, dproj)
    dproj, dws, dbst, dln = gm_bwd(proj, cs["lng"], cs["lnb"], cs["w_s"], cs["bst"], dy, dproj)
    dproj, dnw = sw_backward(proj, cs["nw2"], *tabs, res["swres"], res["y"], dy, dproj)
    g_win = mm_tn(res["h1"], dproj, name="wg_in")
    dx, d_mix_nw, dscale1, dshift1 = norm_mm_bwd(dproj, wb["w_in"], res["x"], cs["mix_nw"], mod[1], dx1,
                                                 name="in_proj_bwd")
    dmod = jnp.concatenate([dshift1, dscale1, dgate1, dshift2, dscale2, dgate2], axis=1)
    dnw = dnw.reshape(2, SW_HEADS, SW_HEAD_DIM).sum(1)
    small = dict(mix_norm_w=d_mix_nw[0], ffn_norm_w=d_ffn_nw[0], dn_conv_w=dcw[:DN_CONV],
                 dn_a_log=dpar[0, :DN_HEADS], dn_dt_bias=dpar[1, :DN_HEADS], dn_out_norm_w=dpar[2],
                 gm_ln_g=dln[0], gm_ln_b=dln[1], gm_w_s=dws, gm_b_s=dbst[:, :GM_GROUPS].T,
                 sw_q_norm_w=dnw[0], sw_k_norm_w=dnw[1])
    token = grads_done("mix", dict(w_in=g_win, w_out=g_wout))
    return dx, small, dmod, token


def _permute_w_in(w):
    pad = jnp.zeros(w.shape[:-1] + (AB_PAD - 8,), w.dtype)
    return jnp.concatenate([w[..., 0:2056], pad, w[..., 2568:IN_WIDTH], w[..., 2056:2568]], axis=-1)


def _unpermute_w_in(g):
    return jnp.concatenate([g[..., 0:2056], g[..., C_UV:IN_PAD], g[..., C_SW:C_UV]], axis=-1)


def _local_step(x, target, mods, weights_of, grads_done, sp):
    layers = mods.shape[0]
    t, d = x.shape
    tabs = _rope_tables(t)
    consts = _consts(sp)
    saved = []
    for layer in range(layers):
        mod = mods[layer].reshape(6, 1, d)
        cs = {k: v[layer] for k, v in consts.items()}
        x, res, wb = _layer_fwd(x, mod, functools.partial(weights_of, layer), cs, tabs)
        saved.append((res, mod, wb, cs))
    dx, loss = loss_head(x, target)
    smalls, dmods = [], []
    token = jnp.zeros((1, 1), F32)
    for layer in reversed(range(layers)):
        res, mod, wb, cs = saved[layer]
        dx, small, dmod, token = _layer_bwd(dx, res, mod + token, wb, cs, tabs, functools.partial(grads_done, layer))
        smalls.append(small)
        dmods.append(dmod[0])
    smalls, dmods = smalls[::-1], dmods[::-1]
    small = {k: jnp.stack([s[k] for s in smalls]) for k in smalls[0]}
    return loss, dx, small, jnp.stack(dmods) + token


def mod_fwd(c_all, w_mod, b_shard):
    layers, d, n = w_mod.shape

    def body(c_ref, w_ref, b_ref, o_ref):
        ca = _silu(c_ref[...]).astype(BF16)
        o_ref[0] = _dot(ca, w_ref[0].astype(BF16), 1, 0) + b_ref[0]

    return _call(
        body, name="mod_fwd", grid=(layers,),
        in_specs=[_full((8, d)), pl.BlockSpec((1, d, n), lambda i: (i, 0, 0)),
                  pl.BlockSpec((1, 1, n), lambda i: (i, 0, 0))],
        out_specs=pl.BlockSpec((1, 8, n), lambda i: (i, 0, 0)),
        out_shape=jax.ShapeDtypeStruct((layers, 8, n), F32), semantics=("parallel",),
    )(c_all, w_mod, b_shard)


def mod_bwd(c_all, dmod):
    layers, _, n = dmod.shape
    d = c_all.shape[1]

    def body(c_ref, g_ref, o_ref):
        ca = _silu(c_ref[...]).astype(BF16)
        o_ref[0] = _dot(ca, g_ref[0].astype(BF16), 0, 0)

    return _call(
        body, name="mod_bwd", grid=(layers,),
        in_specs=[_full((8, d)), pl.BlockSpec((1, 8, n), lambda i: (i, 0, 0))],
        out_specs=pl.BlockSpec((1, d, n), lambda i: (i, 0, 0)),
        out_shape=jax.ShapeDtypeStruct((layers, d, n), F32), semantics=("parallel",),
    )(c_all, dmod)


N_DEV = 8


def _place():
    return lax.axis_index("x"), lax.axis_index("y"), lax.axis_index("c")


def _other_chips(x, y):
    return [(1 - x, y), (x, 1 - y), (1 - x, 1 - y)]


def allgather8(x_shard, *, name):
    m_per, n = x_shard.shape

    def body(x_ref, out_ref, send_sems, recv_sems, local_sem):
        x, y, c = _place()
        me, sibling = (x, y, c), (x, y, 1 - c)
        chips = _other_chips(x, y)

        def rows(px, py, pc):
            return out_ref.at[pl.ds((4 * px + 2 * py + pc) * m_per, m_per), :]

        def copy(k, block, to, src=None):
            return pltpu.make_async_remote_copy(
                src_ref=rows(*block) if src is None else src, dst_ref=rows(*block),
                send_sem=send_sems.at[k], recv_sem=recv_sems.at[k], device_id=to, device_id_type=MESH)

        mine = pltpu.make_async_copy(x_ref, rows(*me), local_sem)
        mine.start()
        first = [copy(0, me, sibling, src=x_ref)]
        first += [copy(1 + j, me, (*chip, c), src=x_ref) for j, chip in enumerate(chips)]
        for cp in first:
            cp.start()
        passed = [copy(4 + j, (*chip, c), sibling) for j, chip in enumerate(chips)]
        for j, chip in enumerate(chips):
            copy(1 + j, (*chip, c), me).wait_recv()
            passed[j].start()
        copy(0, sibling, me).wait_recv()
        for j, chip in enumerate(chips):
            copy(4 + j, (*chip, 1 - c), me).wait_recv()
        for cp in first + passed:
            cp.wait_send()
        mine.wait()

    return pl.pallas_call(
        body, name=name, out_shape=jax.ShapeDtypeStruct((N_DEV * m_per, n), x_shard.dtype),
        in_specs=[pl.BlockSpec(memory_space=pltpu.VMEM)], out_specs=pl.BlockSpec(memory_space=pltpu.VMEM),
        scratch_shapes=[pltpu.SemaphoreType.DMA((7,)), pltpu.SemaphoreType.DMA((7,)), pltpu.SemaphoreType.DMA],
    )(x_shard)


HBM = pl.BlockSpec(memory_space=pltpu.HBM)
SEM = pl.BlockSpec(memory_space=pltpu.SEMAPHORE)
_EFFECT = pltpu.SideEffectType.DATAFLOW_SIDE_EFFECTING


def _piece(ref, sliced, chip):
    return ref.at[2 * chip[0] + chip[1]] if sliced else ref


def exchange_start(srcs, after, *, sliced, name):
    n = len(srcs)
    piece = lambda s: s.shape[1:] if sliced else s.shape

    def body(*refs):
        ins, lands = refs[:n], refs[n:2 * n]
        send_sems, recv_sems = refs[2 * n + len(after):2 * n + len(after) + 2]
        token = refs[-1]
        x, y, c = _place()
        me_s = 2 * x + y
        for a in range(n):
            for j, chip in enumerate(_other_chips(x, y)):
                pltpu.make_async_remote_copy(
                    src_ref=_piece(ins[a], sliced, chip), dst_ref=lands[a].at[me_s], send_sem=send_sems.at[3 * a + j],
                    recv_sem=recv_sems.at[3 * a + j], device_id=(*chip, c), device_id_type=MESH).start()
        token[...] = jnp.zeros_like(token)

    zones = [pltpu.with_memory_space_constraint(lax.empty((4,) + piece(s), s.dtype), pltpu.HBM) for s in srcs]
    srcs = [pltpu.with_memory_space_constraint(s, pltpu.HBM) for s in srcs]
    out = pl.pallas_call(
        body, name=name,
        out_shape=(pltpu.SemaphoreType.DMA((3 * n,)), pltpu.SemaphoreType.DMA((3 * n,)),
                   *[pltpu.HBM(s.shape, s.dtype) for s in srcs], *[pltpu.HBM(z.shape, z.dtype) for z in zones],
                   jax.ShapeDtypeStruct((8, LANE), F32)),
        in_specs=[HBM] * (2 * n) + [ANY] * len(after),
        out_specs=(SEM, SEM, *[HBM] * (2 * n), pl.BlockSpec(memory_space=pltpu.VMEM)),
        input_output_aliases={i: 2 + i for i in range(2 * n)},
        compiler_params=pltpu.CompilerParams(has_side_effects=_EFFECT),
    )(*srcs, *zones, *after)
    return out[0], out[1], out[2:2 + n], out[2 + n:2 + 2 * n], out[-1]


def exchange_wait(send_sems, recv_sems, srcs, zones, after, *, which, sliced, name):
    n = len(srcs)

    def body(*refs):
        ins, lands = refs[:n], refs[n:2 * n]
        send_sems, recv_sems = refs[2 * n:2 * n + 2]
        x, y, c = _place()
        for a in range(n):
            for j, chip in enumerate(_other_chips(x, y)):
                copy = pltpu.make_async_remote_copy(
                    src_ref=_piece(ins[a], sliced, chip), dst_ref=lands[a].at[2 * chip[0] + chip[1]],
                    send_sem=send_sems.at[3 * which[a] + j], recv_sem=recv_sems.at[3 * which[a] + j],
                    device_id=(*chip, c), device_id_type=MESH)
                copy.wait_send()
                copy.wait_recv()

    out = pl.pallas_call(
        body, name=name,
        out_shape=tuple(pltpu.HBM(s.shape, s.dtype) for s in (*srcs, *zones)),
        in_specs=[HBM] * (2 * n) + [SEM, SEM, ANY], out_specs=tuple([HBM] * (2 * n)),
        input_output_aliases={i: i for i in range(2 * n)},
        compiler_params=pltpu.CompilerParams(has_side_effects=_EFFECT),
    )(*srcs, *zones, send_sems, recv_sems, after)
    return out[n:]


def sibling_swap(parts):
    n = len(parts)

    def body(*refs):
        ins, outs = refs[:n], refs[n:2 * n]
        send_sems, recv_sems = refs[2 * n:]
        x, y, c = _place()
        cps = []
        for a in range(n):
            cp = pltpu.make_async_remote_copy(
                src_ref=ins[a], dst_ref=outs[a], send_sem=send_sems.at[a], recv_sem=recv_sems.at[a],
                device_id=(x, y, 1 - c), device_id_type=MESH)
            cp.start()
            cps.append(cp)
        for cp in cps:
            cp.wait()

    return pl.pallas_call(
        body, name="sibling_swap", out_shape=[jax.ShapeDtypeStruct(p.shape, p.dtype) for p in parts],
        in_specs=[ANY] * n, out_specs=[ANY] * n,
        scratch_shapes=[pltpu.SemaphoreType.DMA((n,)), pltpu.SemaphoreType.DMA((n,))],
    )(*parts)


def _row_block(rows, cols, budget=1 << 20):
    best = rows if rows % 8 else 8
    for tr in range(8, rows + 1, 8):
        if rows % tr == 0 and tr * cols * 4 <= budget:
            best = tr
    return best


def chip_sum(own, recv, me_s, buf, layer, layers, *, name):
    r, n = own.shape
    tr = _row_block(r, n)
    steps = r // tr

    def body(me_ref, own_ref, recv_ref, *rest):
        o_ref = rest[-1]
        me = me_ref[0]
        acc = jnp.zeros((tr, n), F32)
        for s in range(4):
            acc = acc + jnp.where(me == s, own_ref[...], recv_ref[s].astype(F32))
        o_ref[...] = acc

    in_specs = [pl.BlockSpec((tr, n), lambda i, me: (i, 0)), pl.BlockSpec((4, tr, n), lambda i, me: (0, i, 0))]
    args = [me_s, own, recv]
    aliases = {}
    if buf is not None:
        in_specs.append(ANY)
        args.append(buf)
        aliases = {3: 0}
    return pl.pallas_call(
        body, name=name, out_shape=jax.ShapeDtypeStruct((layers * r, n), F32),
        grid_spec=pltpu.PrefetchScalarGridSpec(
            num_scalar_prefetch=1, grid=(steps,), in_specs=in_specs,
            out_specs=pl.BlockSpec((tr, n), lambda i, me: (layer * steps + i, 0))),
        input_output_aliases=aliases,
        compiler_params=pltpu.CompilerParams(dimension_semantics=("parallel",)),
    )(*args)


def _adam_update(w, g, m, v):
    m2 = ADAM_B1 * m + (1.0 - ADAM_B1) * g
    v2 = ADAM_B2 * v + (1.0 - ADAM_B2) * (g * g)
    m_hat = m2 / (1.0 - ADAM_B1 ** ADAM_STEP)
    v_hat = v2 / (1.0 - ADAM_B2 ** ADAM_STEP)
    delta = -ADAM_LR * (m_hat / (jnp.sqrt(v_hat) + ADAM_EPS) + ADAM_WD * w)
    return delta, m2, v2


def adamw(w, g_parts, m, v, *, name):
    r, n = w.shape
    tr = _row_block(r, n)
    k = len(g_parts)

    def body(*refs):
        w_ref, m_ref, v_ref = refs[k], refs[k + 1], refs[k + 2]
        g_ref, d_ref, m2_ref, v2_ref = refs[k + 3:]
        g = refs[0][...]
        for p in refs[1:k]:
            g = g + p[...]
        g_ref[...] = g
        d_ref[...], m2_ref[...], v2_ref[...] = _adam_update(w_ref[...], g, m_ref[...], v_ref[...])

    blk = pl.BlockSpec((tr, n), lambda i: (i, 0))
    shp = jax.ShapeDtypeStruct((r, n), F32)
    return _call(body, name=name, grid=(r // tr,), in_specs=[blk] * (k + 3), out_specs=[blk] * 4,
                 out_shape=[shp] * 4, semantics=("parallel",))(*g_parts, w, m, v)


def adamw_gathered(g_all, w, m, v, *, name):
    _, r, n = g_all.shape
    tr = _row_block(r, n * 4)

    def body(ga_ref, w_ref, m_ref, v_ref, g_ref, d_ref, m2_ref, v2_ref):
        g = ga_ref[0]
        for dev in range(1, N_DEV):
            g = g + ga_ref[dev]
        g_ref[...] = g
        d_ref[...], m2_ref[...], v2_ref[...] = _adam_update(w_ref[...], g, m_ref[...], v_ref[...])

    blk = pl.BlockSpec((tr, n), lambda i: (i, 0))
    shp = jax.ShapeDtypeStruct((r, n), F32)
    return _call(body, name=name, grid=(r // tr,),
                 in_specs=[pl.BlockSpec((N_DEV, tr, n), lambda i: (0, i, 0)), blk, blk, blk], out_specs=[blk] * 4,
                 out_shape=[shp] * 4, semantics=("parallel",))(g_all, w, m, v)


BIG = ("w_in", "w_out", "w_ffn_in", "w_ffn_out")
SMALL = ("b_mod", "mix_norm_w", "ffn_norm_w", "dn_conv_w", "dn_a_log", "dn_dt_bias", "dn_out_norm_w", "gm_ln_g",
         "gm_ln_b", "gm_w_s", "gm_b_s", "sw_q_norm_w", "sw_k_norm_w")
WEIGHTS = ("w_mod", "b_mod", "mix_norm_w", "ffn_norm_w", "w_in", "w_out", "dn_conv_w", "dn_a_log", "dn_dt_bias",
           "dn_out_norm_w", "gm_ln_g", "gm_ln_b", "gm_w_s", "gm_b_s", "sw_q_norm_w", "sw_k_norm_w", "w_ffn_in",
           "w_ffn_out")
PACK_ROWS = 8


def _pack(arrs):
    out = []
    for a in arrs:
        flat = a.reshape(-1).astype(F32)
        rows = -(-flat.shape[0] // (LANE * PACK_ROWS)) * PACK_ROWS
        out.append(jnp.pad(flat, (0, rows * LANE - flat.shape[0])).reshape(rows, LANE))
    return jnp.concatenate(out, axis=0)


def _unpack(packed, shapes):
    out, r0 = [], 0
    for shp in shapes:
        size = math.prod(shp)
        rows = -(-size // (LANE * PACK_ROWS)) * PACK_ROWS
        out.append(packed[r0:r0 + rows].reshape(-1)[:size].reshape(shp))
        r0 += rows
    return out


def kernel(x, c, w_mod, b_mod, mix_norm_w, ffn_norm_w, w_in, w_out, dn_conv_w, dn_a_log, dn_dt_bias, dn_out_norm_w, gm_ln_g, gm_ln_b, gm_w_s, gm_b_s, sw_q_norm_w, sw_k_norm_w, w_ffn_in, w_ffn_out, loss_target, m_w_mod, m_b_mod, m_mix_norm_w, m_ffn_norm_w, m_w_in, m_w_out, m_dn_conv_w, m_dn_a_log, m_dn_dt_bias, m_dn_out_norm_w, m_gm_ln_g, m_gm_ln_b, m_gm_w_s, m_gm_b_s, m_sw_q_norm_w, m_sw_k_norm_w, m_w_ffn_in, m_w_ffn_out, v_w_mod, v_b_mod, v_mix_norm_w, v_ffn_norm_w, v_w_in, v_w_out, v_dn_conv_w, v_dn_a_log, v_dn_dt_bias, v_dn_out_norm_w, v_gm_ln_g, v_gm_ln_b, v_gm_w_s, v_gm_b_s, v_sw_q_norm_w, v_sw_k_norm_w, v_w_ffn_in, v_w_ffn_out):
    w = dict(w_mod=w_mod, b_mod=b_mod, mix_norm_w=mix_norm_w, ffn_norm_w=ffn_norm_w, w_in=w_in, w_out=w_out,
             dn_conv_w=dn_conv_w, dn_a_log=dn_a_log, dn_dt_bias=dn_dt_bias, dn_out_norm_w=dn_out_norm_w,
             gm_ln_g=gm_ln_g, gm_ln_b=gm_ln_b, gm_w_s=gm_w_s, gm_b_s=gm_b_s, sw_q_norm_w=sw_q_norm_w,
             sw_k_norm_w=sw_k_norm_w, w_ffn_in=w_ffn_in, w_ffn_out=w_ffn_out)
    m = dict(w_mod=m_w_mod, b_mod=m_b_mod, mix_norm_w=m_mix_norm_w, ffn_norm_w=m_ffn_norm_w, w_in=m_w_in,
             w_out=m_w_out, dn_conv_w=m_dn_conv_w, dn_a_log=m_dn_a_log, dn_dt_bias=m_dn_dt_bias,
             dn_out_norm_w=m_dn_out_norm_w, gm_ln_g=m_gm_ln_g, gm_ln_b=m_gm_ln_b, gm_w_s=m_gm_w_s, gm_b_s=m_gm_b_s,
             sw_q_norm_w=m_sw_q_norm_w, sw_k_norm_w=m_sw_k_norm_w, w_ffn_in=m_w_ffn_in, w_ffn_out=m_w_ffn_out)
    v = dict(w_mod=v_w_mod, b_mod=v_b_mod, mix_norm_w=v_mix_norm_w, ffn_norm_w=v_ffn_norm_w, w_in=v_w_in,
             w_out=v_w_out, dn_conv_w=v_dn_conv_w, dn_a_log=v_dn_a_log, dn_dt_bias=v_dn_dt_bias,
             dn_out_norm_w=v_dn_out_norm_w, gm_ln_g=v_gm_ln_g, gm_ln_b=v_gm_ln_b, gm_w_s=v_gm_w_s, gm_b_s=v_gm_b_s,
             sw_q_norm_w=v_sw_q_norm_w, sw_k_norm_w=v_sw_k_norm_w, w_ffn_in=v_w_ffn_in, w_ffn_out=v_w_ffn_out)
    layers, d, mod_n = w_mod.shape
    mx, my, mc = _place()
    me_s = 2 * mx + my
    me_dev = 4 * mx + 2 * my + mc

    c_all = allgather8(_pad_rows(c, 8), name="gather_c").reshape(N_DEV, 8, d)[:, 0]
    b_shard = lax.dynamic_slice_in_dim(b_mod, me_s * mod_n, mod_n, axis=1)[:, None, :]
    mod_part = mod_fwd(c_all, w_mod, b_shard)
    mod_parts = allgather8(mod_part.reshape(layers * 8, mod_n), name="gather_mod")
    mod_parts = mod_parts.reshape(4, 2, layers, 8, mod_n)[:, 0]
    mod_all = mod_parts.transpose(1, 2, 0, 3).reshape(layers, 8, 4 * mod_n)
    mods = lax.dynamic_index_in_dim(mod_all, me_dev, axis=1, keepdims=False)

    cw = dn_conv_w.shape[-1]
    conv_rows = -(-layers * DN_CONV // 8) * 8
    conv_parts = allgather8(_pad_rows(dn_conv_w.reshape(layers * DN_CONV, cw), conv_rows), name="gather_conv")
    conv_parts = conv_parts.reshape(4, 2, conv_rows, cw)[:, 0, :layers * DN_CONV]
    conv_full = conv_parts.reshape(4, layers, DN_CONV, cw).transpose(1, 2, 0, 3).reshape(layers, DN_CONV, 4 * cw)

    shards = {k: w[k].astype(BF16) for k in BIG}
    groups = dict(w_in=(0,), w_out=(1,), ffn=(2, 3))
    gathers = [exchange_start([shards[k][layer] for k in BIG], [mods, conv_full], sliced=False, name=f"gather_start{layer}")
               for layer in range(layers)]
    mods = mods + sum(g[4][0, 0] for g in gathers)

    def weights_of(layer, group, after):
        send_sems, recv_sems, srcs, zones, _ = gathers[layer]
        which = groups[group]
        got = exchange_wait(send_sems, recv_sems, [srcs[a] for a in which], [zones[a] for a in which], after,
                            which=which, sliced=False, name=f"gather_wait_{group}{layer}")
        full = {BIG[a]: lax.dynamic_update_index_in_dim(z, shards[BIG[a]][layer], me_s, 0) for a, z in zip(which, got)}
        cols = lambda g: jnp.concatenate([g[s] for s in range(4)], axis=-1)
        shape = dict(w_in=lambda g: _permute_w_in(cols(g)), w_out=lambda g: g.reshape(-1, d), w_ffn_in=cols,
                     w_ffn_out=lambda g: g.reshape(-1, d))
        return {k: shape[k](g) for k, g in full.items()}

    scatters = {}
    last_scatter = []
    shard_axis = dict(w_in=1, w_out=0, w_ffn_in=1, w_ffn_out=0)

    def grads_done(layer, group, grads):
        grads = {k: _unpermute_w_in(g) if k == "w_in" else g for k, g in grads.items()}
        send = [jnp.stack(jnp.split(g.astype(BF16), 4, axis=shard_axis[k])) for k, g in grads.items()]
        own = {}
        for k, g in grads.items():
            size = g.shape[shard_axis[k]] // 4
            own[k] = lax.dynamic_slice_in_dim(g, me_s * size, size, axis=shard_axis[k])
        if (layer, group) == (0, "mix"):
            last_scatter.append((send, own))
            return jnp.zeros((1, 1), F32)
        started = exchange_start(send, [], sliced=True, name=f"scatter_start_{group}{layer}")
        scatters[layer, group] = (started, own)
        return started[4][:1, :1]

    sp = {k: w[k] for k in SMALL}
    sp["dn_conv_w"] = conv_full
    loss_blk, grad_x, small, dmods = _local_step(x[0], loss_target[0], mods, weights_of, grads_done, sp)
    loss = lax.psum(loss_blk[0, 0], ("x", "y", "c"))

    outs = {}
    small = dict(small, b_mod=dmods)
    packed = _pack([small[k] for k in SMALL])
    rows = packed.shape[0]
    g_all = allgather8(packed, name="gather_small").reshape(N_DEV, rows, LANE)
    send, own = last_scatter[0]
    scatters[0, "mix"] = (exchange_start(send, [g_all], sliced=True, name="scatter_start_mix0"), own)
    g_all = g_all + scatters[0, "mix"][0][4][0, 0]
    conv_zero = jnp.zeros((layers, DN_CONV, 3 * DN_WIDTH), F32)
    pk = lambda src: _pack([conv_zero if k == "dn_conv_w" else src[k] for k in SMALL])
    res = adamw_gathered(g_all, pk(w), pk(m), pk(v), name="adamw_small")
    shapes = [small[k].shape for k in SMALL]
    un = [_unpack(a, shapes) for a in res]
    for i, k in enumerate(SMALL):
        outs[k] = [un[j][i] for j in range(4)]
    g_conv = lax.dynamic_slice_in_dim(outs["dn_conv_w"][0], me_s * cw, cw, axis=2)
    flat = lambda a: a.reshape(-1, cw)
    res = adamw(flat(dn_conv_w), [flat(g_conv)], flat(m["dn_conv_w"]), flat(v["dn_conv_w"]), name="adamw_conv")
    outs["dn_conv_w"] = [a.reshape(dn_conv_w.shape) for a in res]

    b_rows = layers * 6 * d // LANE
    dmod_all = g_all[:, :b_rows].reshape(N_DEV, layers, 6 * d).transpose(1, 0, 2)
    dmod_shard = lax.dynamic_slice_in_dim(dmod_all, me_s * mod_n, mod_n, axis=2)
    g_wmod = mod_bwd(c_all, dmod_shard)
    flat = lambda a: a.reshape(-1, mod_n)
    res = adamw(flat(w_mod), [flat(g_wmod)], flat(m_w_mod), flat(v_w_mod), name="adamw_w_mod")
    outs["w_mod"] = [a.reshape(w_mod.shape) for a in res]

    me_arr = jnp.reshape(me_s, (1,)).astype(jnp.int32)
    partial = {k: None for k in BIG}
    for layer in range(layers):
        for group in ("ffn", "mix"):
            (send_sems, recv_sems, srcs, zones, _), own = scatters[layer, group]
            zones = exchange_wait(send_sems, recv_sems, srcs, zones, res[0], which=tuple(range(len(srcs))),
                                  sliced=True, name=f"scatter_wait_{group}{layer}")
            for k, z in zip(own, zones):
                partial[k] = chip_sum(own[k], z, me_arr, partial[k], layer, layers, name=f"chip_sum_{k}{layer}")
    partial = [partial[k] for k in BIG]
    theirs = sibling_swap(partial)
    for k, mine, other in zip(BIG, partial, theirs):
        shp = w[k].shape
        flat = lambda a: a.reshape(-1, shp[-1])
        res = adamw(flat(w[k]), [mine, other], flat(m[k]), flat(v[k]), name="adamw_" + k)
        outs[k] = [a.reshape(shp) for a in res]

    result = [loss, grad_x[None]]
    for j in range(4):
        result += [outs[k][j] for k in WEIGHTS]
    return tuple(result)
```

```python
import functools
import math

import jax
import jax.numpy as jnp
from jax import lax
from jax.experimental import pallas as pl
from jax.experimental.pallas import tpu as pltpu

F32 = jnp.float32
BF16 = jnp.bfloat16
HI = lax.Precision.HIGH

NORM_EPS = 1e-6
DN_HEADS = 4
DN_HEAD_DIM = 128
DN_WIDTH = 512
DN_CHUNK = 64
DN_CONV = 4
GM_WIDTH = 256
GM_GROUPS = 4
GM_GROUP_DIM = 64
GM_CHUNK = 128
SW_HEADS = 4
SW_HEAD_DIM = 64
SW_WIDTH = 256
SW_DILATIONS = (1, 4, 16)
SW_BLOCK = 128
ROPE_THETA = 500000.0
ROPE_DIM = 16
LANE = 128

C_QKV = 0
C_Z = 1536
C_AB = 2048
C_SW = 2304
C_UV = 4608
IN_WIDTH = 4872
IN_PAD = 5120
AB_PAD = C_SW - C_AB
MIX_WIDTH = 1024

ADAM_LR = 0.001
ADAM_B1 = 0.9
ADAM_B2 = 0.999
ADAM_EPS = 1e-08
ADAM_WD = 0.01
ADAM_STEP = 10

MESH = pl.DeviceIdType.MESH


BIG_VMEM = 56 << 20


def _call(body, *, name, grid, in_specs, out_specs, out_shape, scratch_shapes=(), semantics=None, aliases=None,
          vmem=None):
    if semantics is None:
        semantics = ("arbitrary",) * len(grid)
    return pl.pallas_call(
        body, name=name, grid=grid, in_specs=in_specs, out_specs=out_specs, out_shape=out_shape,
        scratch_shapes=list(scratch_shapes), input_output_aliases=aliases or {},
        compiler_params=pltpu.CompilerParams(dimension_semantics=semantics, vmem_limit_bytes=vmem),
    )


def _dot(a, b, ca, cb, prec=None):
    if a.ndim == 3:
        dims = (((ca + 1,), (cb + 1,)), ((0,), (0,)))
    else:
        dims = (((ca,), (cb,)), ((), ()))
    return lax.dot_general(a, b, dims, preferred_element_type=F32, precision=prec)


def _bdot(a, b, ca=1, cb=0):
    return _dot(a.astype(BF16), b.astype(BF16), ca, cb)


def _hdot(a, b, ca=1, cb=0):
    return _dot(a.astype(F32), b.astype(F32), ca, cb, HI)


def _split(x):
    hi = x.astype(BF16)
    return hi, (x - hi.astype(F32)).astype(BF16)


def _xdot(a, b, ca=1, cb=0, exact=1):
    if exact == 1:
        hi, lo = _split(a)
        e = b.astype(BF16)
        return _dot(hi, e, ca, cb) + _dot(lo, e, ca, cb)
    hi, lo = _split(b)
    e = a.astype(BF16)
    return _dot(e, hi, ca, cb) + _dot(e, lo, ca, cb)


def _sigmoid(x):
    return 0.5 * jnp.tanh(0.5 * x) + 0.5


def _silu(x):
    return x * _sigmoid(x)


def _dsilu(x):
    s = _sigmoid(x)
    return s * (1.0 + x * (1.0 - s))


def _softplus(x):
    return jnp.maximum(x, 0.0) + jnp.log(1.0 + jnp.exp(-jnp.abs(x)))


def _iota2(shape, dim):
    return lax.broadcasted_iota(jnp.int32, shape, dim)


def _rowsum(x):
    return jnp.sum(x, axis=-1, keepdims=True)


def _colsum(x):
    return jnp.sum(x, axis=-2, keepdims=True)


def _full(shape):
    return pl.BlockSpec(shape, lambda *_: (0,) * len(shape))


def _resident(shape):
    return pl.BlockSpec(shape, lambda *_: (0,) * len(shape), pipeline_mode=pl.Buffered(1))


ANY = pl.BlockSpec(memory_space=pl.ANY)


def _norm_mod(x, nw, scale, shift):
    r = lax.rsqrt(jnp.mean(x * x, axis=-1, keepdims=True) + NORM_EPS)
    xn = x * r
    return xn, r, (xn * nw) * (1.0 + scale) + shift


def norm_mm(x, nw, scale, shift, w, *, swiglu, name, tm=512):
    t, d = x.shape
    n = w.shape[1]
    half = n // 2

    def body(x_ref, nw_ref, sc_ref, sh_ref, w_ref, h_ref, y_ref, *act_ref):
        _, _, h = _norm_mod(x_ref[...], nw_ref[...], sc_ref[...], sh_ref[...])
        hb = h.astype(BF16)
        h_ref[...] = hb
        y = _dot(hb, w_ref[...], 1, 0)
        y_ref[...] = y.astype(y_ref.dtype)
        if swiglu:
            act_ref[0][...] = (_silu(y[:, :half]) * y[:, half:]).astype(BF16)

    row = lambda i: (i, 0)
    out_shape = [jax.ShapeDtypeStruct((t, d), BF16), jax.ShapeDtypeStruct((t, n), BF16 if swiglu else F32)]
    out_specs = [pl.BlockSpec((tm, d), row), pl.BlockSpec((tm, n), row)]
    if swiglu:
        out_shape.append(jax.ShapeDtypeStruct((t, half), BF16))
        out_specs.append(pl.BlockSpec((tm, half), row))
    return _call(
        body, name=name, grid=(t // tm,),
        in_specs=[pl.BlockSpec((tm, d), row), _full((1, d)), _full((1, d)), _full((1, d)), _resident((d, n))],
        out_specs=out_specs, out_shape=out_shape, semantics=("parallel",), vmem=BIG_VMEM,
    )(x, nw, scale, shift, w)


def resid_mm(y, w, x, gate, *, name, tm=512):
    t, k = y.shape
    d = w.shape[1]

    def body(y_ref, w_ref, x_ref, g_ref, xo_ref, o_ref):
        o = _dot(y_ref[...].astype(BF16), w_ref[...], 1, 0)
        o_ref[...] = o
        xo_ref[...] = x_ref[...] + g_ref[...] * o

    row = lambda i: (i, 0)
    return _call(
        body, name=name, grid=(t // tm,),
        in_specs=[pl.BlockSpec((tm, k), row), _resident((k, d)), pl.BlockSpec((tm, d), row), _full((1, d))],
        out_specs=[pl.BlockSpec((tm, d), row), pl.BlockSpec((tm, d), row)],
        out_shape=[jax.ShapeDtypeStruct((t, d), F32), jax.ShapeDtypeStruct((t, d), F32)],
        semantics=("parallel",), vmem=BIG_VMEM,
    )(y, w, x, gate)


def resid_mm_bwd(dx, gate, o, w, gu, *, name, tm):
    t, d = dx.shape
    k = w.shape[0]
    swiglu = gu is not None

    def body(dx_ref, g_ref, o_ref, w_ref, *rest):
        if swiglu:
            gu_ref, dy_ref, gx_ref, dg_ref = rest
        else:
            dy_ref, gx_ref, dg_ref = rest
        i = pl.program_id(0)
        dxv = dx_ref[...]
        gx = (dxv * g_ref[...]).astype(BF16)
        gx_ref[...] = gx
        part = _colsum(dxv * o_ref[...])

        @pl.when(i == 0)
        def _():
            dg_ref[...] = jnp.zeros_like(dg_ref)

        dg_ref[...] += part
        da = _dot(gx, w_ref[...], 1, 1)
        if swiglu:
            g = gu_ref[:, :k].astype(F32)
            u = gu_ref[:, k:].astype(F32)
            dy_ref[:, :k] = (da * u * _dsilu(g)).astype(BF16)
            dy_ref[:, k:] = (da * _silu(g)).astype(BF16)
        else:
            dy_ref[...] = da

    row = lambda i: (i, 0)
    in_specs = [pl.BlockSpec((tm, d), row), _full((1, d)), pl.BlockSpec((tm, d), row), _resident((k, d))]
    args = [dx, gate, o, w]
    if swiglu:
        in_specs.append(pl.BlockSpec((tm, 2 * k), row))
        args.append(gu)
        dy_shape = jax.ShapeDtypeStruct((t, 2 * k), BF16)
        dy_spec = pl.BlockSpec((tm, 2 * k), row)
    else:
        dy_shape = jax.ShapeDtypeStruct((t, k), F32)
        dy_spec = pl.BlockSpec((tm, k), row)
    return _call(
        body, name=name, grid=(t // tm,), in_specs=in_specs,
        out_specs=[dy_spec, pl.BlockSpec((tm, d), row), _full((1, d))],
        out_shape=[dy_shape, jax.ShapeDtypeStruct((t, d), BF16), jax.ShapeDtypeStruct((1, d), F32)], vmem=BIG_VMEM,
    )(*args)


def norm_mm_bwd(dy, w, x, nw, scale, dres, *, name, tm=512):
    t, n = dy.shape
    d = x.shape[1]
    steps = t // tm

    def body(dy_ref, w_ref, x_ref, nw_ref, sc_ref, dres_ref, dx_ref, dnw_ref, dsc_ref, dsh_ref):
        i = pl.program_id(0)
        dh = _dot(dy_ref[...].astype(BF16), w_ref[...], 1, 1)
        x = x_ref[...]
        r = lax.rsqrt(jnp.mean(x * x, axis=-1, keepdims=True) + NORM_EPS)
        xn = x * r
        a = nw_ref[...] * (1.0 + sc_ref[...])

        @pl.when(i == 0)
        def _():
            dnw_ref[...] = jnp.zeros_like(dnw_ref)
            dsh_ref[...] = jnp.zeros_like(dsh_ref)

        dnw_ref[...] += _colsum(dh * xn)
        dsh_ref[...] += _colsum(dh)
        dxn = dh * a
        dx_ref[...] = r * (dxn - xn * jnp.mean(dxn * xn, axis=-1, keepdims=True)) + dres_ref[...]

        @pl.when(i == steps - 1)
        def _():
            da = dnw_ref[...]
            dsc_ref[...] = da * nw_ref[...]
            dnw_ref[...] = da * (1.0 + sc_ref[...])

    row = lambda i: (i, 0)
    vec = jax.ShapeDtypeStruct((1, d), F32)
    return _call(
        body, name=name, grid=(steps,),
        in_specs=[pl.BlockSpec((tm, n), row), _resident((d, n)), pl.BlockSpec((tm, d), row), _full((1, d)),
                  _full((1, d)), pl.BlockSpec((tm, d), row)],
        out_specs=[pl.BlockSpec((tm, d), row), _full((1, d)), _full((1, d)), _full((1, d))],
        out_shape=[jax.ShapeDtypeStruct((t, d), F32), vec, vec, vec], vmem=BIG_VMEM,
    )(dy, w, x, nw, scale, dres)


def _pick_tn(n, k, budget=6 << 20):
    best = LANE
    for m in range(1, n // LANE + 1):
        tn = m * LANE
        if n % tn == 0 and k * tn * 4 <= budget:
            best = tn
    return best


def mm_tn(a, g, *, name, tt=2048):
    t, k = a.shape
    n = g.shape[1]
    tn = _pick_tn(n, k)
    tt = min(tt, t)
    steps = t // tt

    def body(a_ref, g_ref, o_ref, b_ref):
        i = pl.program_id(1)

        @pl.when(i == 0)
        def _():
            o_ref[...] = jnp.zeros_like(o_ref)

        o_ref[...] += _dot(a_ref[...].astype(BF16), g_ref[...].astype(BF16), 0, 0)

        @pl.when(i == steps - 1)
        def _():
            b_ref[...] = o_ref[...].astype(BF16)

    out = pl.BlockSpec((k, tn), lambda j, i: (0, j))
    return _call(
        body, name=name, grid=(n // tn, steps),
        in_specs=[pl.BlockSpec((tt, k), lambda j, i: (i, 0)), pl.BlockSpec((tt, tn), lambda j, i: (i, j))],
        out_specs=[out, out],
        out_shape=[jax.ShapeDtypeStruct((k, n), F32), jax.ShapeDtypeStruct((k, n), BF16)],
        semantics=("parallel", "arbitrary"), vmem=BIG_VMEM,
    )(a, g)


def loss_head(y, target, *, tm=512):
    t, d = y.shape
    steps = t // tm

    def body(y_ref, t_ref, dy_ref, l_ref, acc_ref):
        i = pl.program_id(0)

        @pl.when(i == 0)
        def _():
            acc_ref[...] = jnp.zeros_like(acc_ref)

        e = y_ref[...] - t_ref[...]
        dy_ref[...] = e * (1.0 / d)
        acc_ref[...] += _colsum(e * e)

        @pl.when(i == steps - 1)
        def _():
            tot = jnp.sum(acc_ref[...], axis=-1, keepdims=True) * (0.5 / d)
            l_ref[...] = jnp.broadcast_to(tot, l_ref.shape)

    row = lambda i: (i, 0)
    return _call(
        body, name="loss_head", grid=(steps,),
        in_specs=[pl.BlockSpec((tm, d), row), pl.BlockSpec((tm, d), row)],
        out_specs=[pl.BlockSpec((tm, d), row), _full((8, LANE))],
        out_shape=[jax.ShapeDtypeStruct((t, d), F32), jax.ShapeDtypeStruct((8, LANE), F32)],
        scratch_shapes=[pltpu.VMEM((1, d), F32)],
    )(y, target)


def _shift_rows(x, s):
    if s == 0:
        return x
    t = x.shape[0]
    ri = _iota2(x.shape, 0)
    rolled = pltpu.roll(x, s % t, axis=0)
    if s > 0:
        return jnp.where(ri >= s, rolled, 0.0)
    return jnp.where(ri < t + s, rolled, 0.0)


def _conv_pre(x, w):
    acc = x * w[DN_CONV - 1:DN_CONV, :]
    for j in range(DN_CONV - 1):
        acc = acc + _shift_rows(x, DN_CONV - 1 - j) * w[j:j + 1, :]
    return acc


def dn_conv(proj, conv_w):
    t = proj.shape[0]
    width = 3 * DN_WIDTH

    def body(x_ref, w_ref, o_ref):
        o_ref[...] = _silu(_conv_pre(x_ref[...], w_ref[...]))

    col = lambda j: (0, j)
    return _call(
        body, name="dn_conv", grid=(width // LANE,),
        in_specs=[pl.BlockSpec((t, LANE), col), pl.BlockSpec((8, LANE), col)],
        out_specs=pl.BlockSpec((t, LANE), col),
        out_shape=jax.ShapeDtypeStruct((t, width), F32), semantics=("parallel",),
    )(proj, conv_w)


def dn_conv_bwd(proj, conv_w, dact, dproj):
    t = proj.shape[0]
    width = 3 * DN_WIDTH

    def body(x_ref, w_ref, d_ref, _, dx_ref, dw_ref):
        x = x_ref[...]
        w = w_ref[...]
        dc = d_ref[...] * _dsilu(_conv_pre(x, w))
        dx = dc * w[DN_CONV - 1:DN_CONV, :]
        rows = []
        for j in range(DN_CONV - 1):
            s = DN_CONV - 1 - j
            dx = dx + _shift_rows(dc, -s) * w[j:j + 1, :]
            rows.append(_colsum(dc * _shift_rows(x, s)))
        rows.append(_colsum(dc * x))
        dx_ref[...] = dx.astype(BF16)
        ri = _iota2((8, LANE), 0)
        dw = jnp.zeros((8, LANE), F32)
        for j in range(DN_CONV):
            dw = dw + jnp.where(ri == j, rows[j], 0.0)
        dw_ref[...] = dw

    col = lambda j: (0, j)
    return _call(
        body, name="dn_conv_bwd", grid=(width // LANE,),
        in_specs=[pl.BlockSpec((t, LANE), col), pl.BlockSpec((8, LANE), col), pl.BlockSpec((t, LANE), col), ANY],
        out_specs=[pl.BlockSpec((t, LANE), col), pl.BlockSpec((8, LANE), col)],
        out_shape=[jax.ShapeDtypeStruct(dproj.shape, dproj.dtype), jax.ShapeDtypeStruct((8, width), F32)],
        semantics=("parallel",), aliases={3: 0},
    )(proj, conv_w, dact, dproj)


def _t(x):
    return jnp.swapaxes(x, -1, -2)


def _inv_unit_lower(a):
    c = a.shape[-1]
    eye = (_iota2((c, c), 0) == _iota2((c, c), 1)).astype(F32)
    x = eye - a
    p = _hdot(a, a)
    steps = int(math.log2(c)) - 1
    for i in range(steps):
        x = x + _hdot(x, p)
        if i < steps - 1:
            p = _hdot(p, p)
    return x


def _dn_local(q, k, v, a, b, alog, dtb, tinv=None):
    nh, c, d = q.shape
    rq = lax.rsqrt(_rowsum(q * q) + NORM_EPS)
    rk = lax.rsqrt(_rowsum(k * k) + NORM_EPS)
    qh = q * rq
    kn = k * rk
    qs = qh * (d ** -0.5)
    g = -jnp.exp(alog) * _softplus(a + dtb)
    beta = _sigmoid(b)
    ri = _iota2((c, c), 0)
    ci = _iota2((c, c), 1)
    causal = ri >= ci
    strict = ri > ci
    gb = jnp.broadcast_to(g, (nh, c, d))
    gcb = _xdot(jnp.broadcast_to(causal.astype(F32), (nh, c, c)), gb, exact=0)
    gc = gcb[..., :1]
    gl = _colsum(gb)[..., :1]
    dec = jnp.exp(jnp.where(causal, gc - _t(gcb)[:, :c, :], -1e30))
    kb = kn * beta
    amat = jnp.where(strict, _bdot(kb, kn, 1, 1) * dec, 0.0)
    if tinv is None:
        tinv = _inv_unit_lower(amat)
    e = jnp.exp(gc)
    f = jnp.exp(gl - gc)
    rw = kb * e
    sol = _hdot(tinv, jnp.concatenate([v * beta, rw], axis=-1))
    pmat = jnp.where(causal, _bdot(qs, kn, 1, 1) * dec, 0.0)
    return dict(rq=rq, rk=rk, qh=qh, kn=kn, qs=qs, g=g, beta=beta, causal=causal, strict=strict, gl=gl,
                dec=dec, kb=kb, amat=amat, tinv=tinv, e=e, f=f, rw=rw, u=sol[..., :d], w=sol[..., d:], pmat=pmat,
                qd=qs * e, kd=kn * f)


_DN_FIELDS = ("u", "w", "qd", "kd", "pmat", "gl")


def _dn_state(m, s_in):
    vnew = m["u"] - _bdot(m["w"], s_in)
    o = _bdot(m["qd"], s_in) + _bdot(m["pmat"], vnew)
    return vnew, o, s_in * jnp.exp(m["gl"]) + _bdot(m["kd"], vnew, 0, 0)


def _dn_state_bwd(m, s_in, do, ds_out):
    el = jnp.exp(m["gl"])
    dvnew = _bdot(m["pmat"], do, 0, 0) + _bdot(m["kd"], ds_out)
    dkd = _bdot(m["vnew"], ds_out, 1, 1)
    ds_in = _bdot(m["qd"], do, 0, 0) + el * ds_out - _bdot(m["w"], dvnew, 0, 0)
    dgl = el * _colsum(_rowsum(s_in * ds_out))
    return dvnew, dkd, dgl, ds_in


def _dn_local_bwd(m, q, v, a, alog, dtb, s_in, vnew, do, dvnew, dkd, dgl):
    nh, c, d = q.shape
    kn, qs, kb, u, w, e, f = m["kn"], m["qs"], m["kb"], m["u"], m["w"], m["e"], m["f"]
    beta, dec, tinv, kd, qd = m["beta"], m["dec"], m["tinv"], m["kd"], m["qd"]
    dp = jnp.where(m["causal"], _bdot(do, vnew, 1, 1), 0.0)
    dqd = _bdot(do, s_in, 1, 1)
    dw = -_bdot(dvnew, s_in, 1, 1)
    dsol = _hdot(tinv, jnp.concatenate([dvnew, dw], axis=-1), 0, 0)
    dru = dsol[..., :d]
    drw = dsol[..., d:]
    da_m = -jnp.where(m["strict"], _bdot(dsol, jnp.concatenate([u, w], axis=-1), 1, 1), 0.0)
    db_m = da_m * dec
    dq_m = dp * dec
    dkb = _bdot(db_m, kn)
    dkn = _bdot(db_m, kb, 0, 0) + _bdot(dq_m, qs, 0, 0)
    dqs = _bdot(dq_m, kn)
    gmat = da_m * m["amat"] + dp * m["pmat"]
    ones = jnp.ones((nh, c, d), F32)
    dgam = (_xdot(gmat, ones) - _xdot(gmat, ones, 0, 0))[..., :1]
    dqs = dqs + dqd * e
    dgam = dgam + _rowsum(dqd * qd)
    dkn = dkn + dkd * f
    tk = _rowsum(dkd * kd)
    dgam = dgam - tk
    dgl = dgl + _colsum(tk)
    dkb = dkb + drw * e
    dgam = dgam + _rowsum(drw * m["rw"])
    dv = dru * beta
    dbeta = _rowsum(dru * v) + _rowsum(dkb * kn)
    dkn = dkn + dkb * beta
    last = (_iota2((c, 1), 0) == c - 1).astype(F32)
    dgam = dgam + last * dgl
    upper = (_iota2((c, c), 0) <= _iota2((c, c), 1)).astype(F32)
    dg = _xdot(jnp.broadcast_to(upper, (nh, c, c)), jnp.broadcast_to(dgam, (nh, c, d)), exact=0)[..., :1]
    dqh = dqs * (d ** -0.5)
    dq = m["rq"] * (dqh - m["qh"] * _rowsum(dqh * m["qh"]))
    dk = m["rk"] * (dkn - kn * _rowsum(dkn * kn))
    sg = _sigmoid(a + dtb)
    da = dg * (-jnp.exp(alog)) * sg
    dalog = _colsum(dg * m["g"])
    ddtb = _colsum(da)
    db = dbeta * beta * (1.0 - beta)
    return dq, dk, dv, da, db, dalog, ddtb


def _dn_gate(o, z, wn):
    ro = lax.rsqrt(jnp.mean(o * o, axis=-1, keepdims=True) + NORM_EPS)
    n = o * ro
    return n, ro, n * wn * _silu(z)


DN_PAIR = 4


def _heads(ref, col0):
    d = DN_HEAD_DIM
    return jnp.stack([ref[j * DN_CHUNK:(j + 1) * DN_CHUNK, col0 + h * d:col0 + (h + 1) * d]
                      for j in range(DN_PAIR) for h in range(DN_HEADS)])


def _dn_inputs(act_ref, ab_ref, sc_ref):
    ab = ab_ref[...]
    sc = sc_ref[...]
    rows = lambda j: slice(j * DN_CHUNK, (j + 1) * DN_CHUNK)
    both = [(j, h) for j in range(DN_PAIR) for h in range(DN_HEADS)]
    q = _heads(act_ref, 0)
    k = _heads(act_ref, DN_WIDTH)
    v = _heads(act_ref, 2 * DN_WIDTH)
    a = jnp.stack([ab[rows(j), h:h + 1] for j, h in both])
    b = jnp.stack([ab[rows(j), DN_HEADS + h:DN_HEADS + h + 1] for j, h in both])
    alog = jnp.stack([sc[0:1, h:h + 1] for _, h in both])
    dtb = jnp.stack([sc[1:2, h:h + 1] for _, h in both])
    return q, k, v, a, b, alog, dtb


def _chunk_of(m, j, fields):
    return {f: m[f][j * DN_HEADS:(j + 1) * DN_HEADS] for f in fields}


def dn_fwd(act, proj, scal, wn):
    t = act.shape[0]
    n = t // DN_CHUNK
    d = DN_HEAD_DIM
    rows = DN_PAIR * DN_CHUNK

    def body(act_ref, z_ref, ab_ref, sc_ref, wn_ref, y_ref, st_ref, ti_ref, s_ref):
        @pl.when(pl.program_id(0) == 0)
        def _():
            s_ref[...] = jnp.zeros_like(s_ref)

        m = _dn_local(*_dn_inputs(act_ref, ab_ref, sc_ref))
        s = s_ref[...]
        outs = []
        for j in range(DN_PAIR):
            st_ref[j] = s
            ti_ref[j] = m["tinv"][j * DN_HEADS:(j + 1) * DN_HEADS]
            _, o, s = _dn_state(_chunk_of(m, j, _DN_FIELDS), s)
            outs.append(o)
        s_ref[...] = s
        y = _dn_gate(jnp.concatenate(outs, axis=0), _heads(z_ref, 0), wn_ref[...])[2]
        for j in range(DN_PAIR):
            for h in range(DN_HEADS):
                y_ref[j * DN_CHUNK:(j + 1) * DN_CHUNK, h * d:(h + 1) * d] = y[j * DN_HEADS + h]

    return _call(
        body, name="dn_fwd", grid=(n // DN_PAIR,),
        in_specs=[pl.BlockSpec((rows, 3 * DN_WIDTH), lambda i: (i, 0)),
                  pl.BlockSpec((rows, DN_WIDTH), lambda i: (i, C_Z // DN_WIDTH)),
                  pl.BlockSpec((rows, LANE), lambda i: (i, C_AB // LANE)),
                  _full((8, LANE)), _full((1, d))],
        out_specs=[pl.BlockSpec((rows, DN_WIDTH), lambda i: (i, 0)),
                   pl.BlockSpec((DN_PAIR, DN_HEADS, d, d), lambda i: (i, 0, 0, 0)),
                   pl.BlockSpec((DN_PAIR, DN_HEADS, DN_CHUNK, DN_CHUNK), lambda i: (i, 0, 0, 0))],
        out_shape=[jax.ShapeDtypeStruct((t, MIX_WIDTH), F32), jax.ShapeDtypeStruct((n, DN_HEADS, d, d), F32),
                   jax.ShapeDtypeStruct((n, DN_HEADS, DN_CHUNK, DN_CHUNK), F32)],
        scratch_shapes=[pltpu.VMEM((DN_HEADS, d, d), F32)],
    )(act, proj, proj, scal, wn)


def dn_bwd(act, proj, scal, wn, states, tinvs, dy):
    t = act.shape[0]
    n = t // DN_CHUNK
    steps = n // DN_PAIR
    d = DN_HEAD_DIM
    zab = DN_WIDTH + AB_PAD
    rows = DN_PAIR * DN_CHUNK

    def body(act_ref, z_ref, ab_ref, sc_ref, wn_ref, st_ref, ti_ref, dy_ref, dact_ref, dzab_ref, dpar_ref, ds_ref):
        @pl.when(pl.program_id(0) == 0)
        def _():
            ds_ref[...] = jnp.zeros_like(ds_ref)
            dpar_ref[...] = jnp.zeros_like(dpar_ref)

        wnv = wn_ref[...]
        q, k, v, a, b, alog, dtb = _dn_inputs(act_ref, ab_ref, sc_ref)
        batch = (DN_PAIR * DN_HEADS,)
        s_in = st_ref[...].reshape(batch + (d, d))
        m = _dn_local(q, k, v, a, b, alog, dtb, ti_ref[...].reshape(batch + (DN_CHUNK, DN_CHUNK)))
        vnew, o, _ = _dn_state(m, s_in)
        z = _heads(z_ref, 0)
        dyh = _heads(dy_ref, 0)
        nrm, ro, _ = _dn_gate(o, z, wnv)
        sz = _silu(z)
        dz = dyh * nrm * wnv * _dsilu(z)
        dn = dyh * wnv * sz
        dwn = _colsum(dyh * nrm * sz)
        do = ro * (dn - nrm * jnp.mean(dn * nrm, axis=-1, keepdims=True))
        ds = ds_ref[...]
        parts = [None] * DN_PAIR
        for j in reversed(range(DN_PAIR)):
            mj = dict(_chunk_of(m, j, _DN_FIELDS), vnew=vnew[j * DN_HEADS:(j + 1) * DN_HEADS])
            dvnew, dkd, dgl, ds = _dn_state_bwd(mj, s_in[j * DN_HEADS:(j + 1) * DN_HEADS],
                                                do[j * DN_HEADS:(j + 1) * DN_HEADS], ds)
            parts[j] = (dvnew, dkd, dgl)
        ds_ref[...] = ds
        dvnew, dkd, dgl = (jnp.concatenate([p[i] for p in parts], axis=0) for i in range(3))
        dq, dk, dv, da, db, dalog, ddtb = _dn_local_bwd(m, q, v, a, alog, dtb, s_in, vnew, do, dvnew, dkd, dgl)
        lane = _iota2((DN_CHUNK, LANE), 1)
        prow = _iota2((8, LANE), 0)
        plane = _iota2((8, LANE), 1)
        dpar = jnp.zeros((8, LANE), F32)
        for j in range(DN_PAIR):
            rs = slice(j * DN_CHUNK, (j + 1) * DN_CHUNK)
            dab = jnp.zeros((DN_CHUNK, LANE), F32)
            for h in range(DN_HEADS):
                n_ = j * DN_HEADS + h
                dzab_ref[rs, h * d:(h + 1) * d] = dz[n_].astype(BF16)
                dact_ref[rs, h * d:(h + 1) * d] = dq[n_]
                dact_ref[rs, DN_WIDTH + h * d:DN_WIDTH + (h + 1) * d] = dk[n_]
                dact_ref[rs, 2 * DN_WIDTH + h * d:2 * DN_WIDTH + (h + 1) * d] = dv[n_]
                dab = dab + jnp.where(lane == h, da[n_], 0.0) + jnp.where(lane == DN_HEADS + h, db[n_], 0.0)
                dpar = dpar + jnp.where((prow == 0) & (plane == h), dalog[n_], 0.0)
                dpar = dpar + jnp.where((prow == 1) & (plane == h), ddtb[n_], 0.0)
                dpar = dpar + jnp.where(prow == 2, dwn[n_], 0.0)
            dzab_ref[rs, DN_WIDTH:DN_WIDTH + LANE] = dab.astype(BF16)
            dzab_ref[rs, DN_WIDTH + LANE:] = jnp.zeros((DN_CHUNK, AB_PAD - LANE), BF16)
        dpar_ref[...] += dpar

    rev = lambda i: (steps - 1 - i, 0)
    rev4 = lambda i: (steps - 1 - i, 0, 0, 0)
    return _call(
        body, name="dn_bwd", grid=(steps,),
        in_specs=[pl.BlockSpec((rows, 3 * DN_WIDTH), rev),
                  pl.BlockSpec((rows, DN_WIDTH), lambda i: (steps - 1 - i, C_Z // DN_WIDTH)),
                  pl.BlockSpec((rows, LANE), lambda i: (steps - 1 - i, C_AB // LANE)),
                  _full((8, LANE)), _full((1, d)),
                  pl.BlockSpec((DN_PAIR, DN_HEADS, d, d), rev4),
                  pl.BlockSpec((DN_PAIR, DN_HEADS, DN_CHUNK, DN_CHUNK), rev4),
                  pl.BlockSpec((rows, DN_WIDTH), rev)],
        out_specs=[pl.BlockSpec((rows, 3 * DN_WIDTH), rev),
                   pl.BlockSpec((rows, zab), lambda i: (steps - 1 - i, C_Z // zab)), _full((8, LANE))],
        out_shape=[jax.ShapeDtypeStruct((t, 3 * DN_WIDTH), F32), jax.ShapeDtypeStruct((t, IN_PAD), BF16),
                   jax.ShapeDtypeStruct((8, LANE), F32)],
        scratch_shapes=[pltpu.VMEM((DN_HEADS, d, d), F32)],
    )(act, proj, proj, scal, wn, states, tinvs, dy)


_INV_SQRT2 = 0.7071067811865476
_INV_SQRT2PI = 0.3989422804014327


def _gelu(x):
    return 0.5 * x * (1.0 + lax.erf(x * _INV_SQRT2))


def _dgelu(x):
    return 0.5 * (1.0 + lax.erf(x * _INV_SQRT2)) + x * jnp.exp(-0.5 * x * x) * _INV_SQRT2PI


def _gm_core(uv, lng, lnb, ws_ref, bst):
    c = uv.shape[0]
    zz = _gelu(uv)
    u = zz[:, :GM_WIDTH]
    vv = zz[:, GM_WIDTH:]
    xc = vv - jnp.mean(vv, axis=-1, keepdims=True)
    rs = lax.rsqrt(jnp.mean(xc * xc, axis=-1, keepdims=True) + NORM_EPS)
    xh = xc * rs
    vn = xh * lng + lnb
    grp = _iota2((c, GM_WIDTH), 1) // GM_GROUP_DIM
    tril = _iota2((c, c), 0) >= _iota2((c, c), 1)
    sv = jnp.zeros((c, GM_WIDTH), F32)
    masks = []
    for g in range(GM_GROUPS):
        mk = grp == g
        masks.append(mk)
        ws = jnp.where(tril, ws_ref[g], 0.0)
        sv = sv + _bdot(ws, jnp.where(mk, vn, 0.0)) + jnp.where(mk, bst[:, g:g + 1], 0.0)
    return u, xh, rs, vn, sv, masks, tril


def gm_fwd(proj, lng, lnb, w_s, bst, ybuf):
    t = proj.shape[0]

    def body(uv_ref, g_ref, b_ref, ws_ref, bst_ref, _, y_ref):
        u, _, _, _, sv, _, _ = _gm_core(uv_ref[...], g_ref[...], b_ref[...], ws_ref, bst_ref[...])
        y_ref[...] = u * sv

    return _call(
        body, name="gm_fwd", grid=(t // GM_CHUNK,),
        in_specs=[pl.BlockSpec((GM_CHUNK, 2 * GM_WIDTH), lambda i: (i, C_UV // (2 * GM_WIDTH))),
                  _full((1, GM_WIDTH)), _full((1, GM_WIDTH)), _full((GM_GROUPS, GM_CHUNK, GM_CHUNK)),
                  _full((GM_CHUNK, LANE)), ANY],
        out_specs=pl.BlockSpec((GM_CHUNK, GM_WIDTH), lambda i: (i, DN_WIDTH // GM_WIDTH)),
        out_shape=jax.ShapeDtypeStruct(ybuf.shape, F32), semantics=("parallel",), aliases={5: 0},
    )(proj, lng, lnb, w_s, bst, ybuf)


def gm_bwd(proj, lng, lnb, w_s, bst, dy, dproj):
    t = proj.shape[0]

    def body(uv_ref, g_ref, b_ref, ws_ref, bst_ref, dy_ref, _, duv_ref, dws_ref, dbst_ref, dln_ref):
        @pl.when(pl.program_id(0) == 0)
        def _():
            dws_ref[...] = jnp.zeros_like(dws_ref)
            dbst_ref[...] = jnp.zeros_like(dbst_ref)
            dln_ref[...] = jnp.zeros_like(dln_ref)

        uv = uv_ref[...]
        lng = g_ref[...]
        u, xh, rs, vn, sv, masks, tril = _gm_core(uv, lng, b_ref[...], ws_ref, bst_ref[...])
        dyv = dy_ref[...]
        dsv = dyv * u
        lane = _iota2((GM_CHUNK, LANE), 1)
        dvn = jnp.zeros_like(dsv)
        dbst = jnp.zeros((GM_CHUNK, LANE), F32)
        for g in range(GM_GROUPS):
            ws = jnp.where(tril, ws_ref[g], 0.0)
            dsg = jnp.where(masks[g], dsv, 0.0)
            dvn = dvn + jnp.where(masks[g], _bdot(ws, dsv, 0, 0), 0.0)
            dws_ref[g] += jnp.where(tril, _bdot(dsg, vn, 1, 1), 0.0)
            dbst = dbst + jnp.where(lane == g, _rowsum(dsg), 0.0)
        dbst_ref[...] += dbst
        row = _iota2((8, GM_WIDTH), 0)
        dln_ref[...] += jnp.where(row == 0, _colsum(dvn * xh), 0.0) + jnp.where(row == 1, _colsum(dvn), 0.0)
        dxh = dvn * lng
        dvv = rs * (dxh - jnp.mean(dxh, axis=-1, keepdims=True) - xh * jnp.mean(dxh * xh, axis=-1, keepdims=True))
        dg = _dgelu(uv)
        duv_ref[:, :GM_WIDTH] = (dyv * sv * dg[:, :GM_WIDTH]).astype(BF16)
        duv_ref[:, GM_WIDTH:] = (dvv * dg[:, GM_WIDTH:]).astype(BF16)

    return _call(
        body, name="gm_bwd", grid=(t // GM_CHUNK,),
        in_specs=[pl.BlockSpec((GM_CHUNK, 2 * GM_WIDTH), lambda i: (i, C_UV // (2 * GM_WIDTH))),
                  _full((1, GM_WIDTH)), _full((1, GM_WIDTH)), _full((GM_GROUPS, GM_CHUNK, GM_CHUNK)),
                  _full((GM_CHUNK, LANE)),
                  pl.BlockSpec((GM_CHUNK, GM_WIDTH), lambda i: (i, DN_WIDTH // GM_WIDTH)), ANY],
        out_specs=[pl.BlockSpec((GM_CHUNK, 2 * GM_WIDTH), lambda i: (i, C_UV // (2 * GM_WIDTH))),
                   _full((GM_GROUPS, GM_CHUNK, GM_CHUNK)), _full((GM_CHUNK, LANE)), _full((8, GM_WIDTH))],
        out_shape=[jax.ShapeDtypeStruct(dproj.shape, dproj.dtype),
                   jax.ShapeDtypeStruct((GM_GROUPS, GM_CHUNK, GM_CHUNK), F32),
                   jax.ShapeDtypeStruct((GM_CHUNK, LANE), F32), jax.ShapeDtypeStruct((8, GM_WIDTH), F32)],
        aliases={6: 0},
    )(proj, lng, lnb, w_s, bst, dy, dproj)


def _head_mats():
    r = _iota2((SW_WIDTH, SW_WIDTH), 0)
    c = _iota2((SW_WIDTH, SW_WIDTH), 1)
    same = (r // SW_HEAD_DIM) == (c // SW_HEAD_DIM)
    cc = c % SW_HEAD_DIM
    half = ROPE_DIM // 2
    rot = jnp.where((cc < half) & (r == c + half), -1.0, 0.0) + jnp.where((cc >= half) & (cc < ROPE_DIM) & (r == c - half), 1.0, 0.0)
    return same.astype(F32), rot


def _seg_col(s):
    return C_SW // SW_WIDTH + (s // 2) * 3 + s % 2


def _halves(x):
    return x[:, :LANE], x[:, LANE:]


def sw_prep(proj, nw2, cos_t, sin_t, *, tm=512):
    t = proj.shape[0]

    def body(x_ref, w_ref, c_ref, s_ref, o_ref):
        same, rot = _head_mats()
        x = x_ref[...]
        r = lax.rsqrt(_xdot(x * x, same) * (1.0 / SW_HEAD_DIM) + NORM_EPS)
        xn = x * r * w_ref[0]
        o_ref[0, 0], o_ref[0, 1] = _halves(xn * c_ref[...] + _xdot(xn, rot) * s_ref[...])

    return _call(
        body, name="sw_prep", grid=(6, t // tm),
        in_specs=[pl.BlockSpec((tm, SW_WIDTH), lambda s, i: (i, _seg_col(s))),
                  pl.BlockSpec((1, 1, SW_WIDTH), lambda s, i: (s % 2, 0, 0)),
                  pl.BlockSpec((tm, SW_WIDTH), lambda s, i: (i, 0)),
                  pl.BlockSpec((tm, SW_WIDTH), lambda s, i: (i, 0))],
        out_specs=pl.BlockSpec((1, 2, tm, LANE), lambda s, i: (s, 0, i, 0)),
        out_shape=jax.ShapeDtypeStruct((6, 2, t, LANE), F32), semantics=("parallel", "parallel"),
    )(proj, nw2, cos_t, sin_t)


def sw_prep_bwd(proj, nw2, cos_t, sin_t, dkvq, dproj, dnw, p, *, tm=512):
    t = proj.shape[0]
    col0 = C_SW // SW_WIDTH + 3 * p
    seg_col = lambda s: col0 + (s + 1) % 3

    def body(x_ref, w_ref, c_ref, s_ref, d_ref, _, dw0_ref, dx_ref, dw_ref):
        s = pl.program_id(0)
        dout = jnp.concatenate([d_ref[0, 0], d_ref[0, 1]], axis=1)

        @pl.when(s == 1)
        def _():
            dx_ref[...] = dout.astype(BF16)

        @pl.when((s != 1) & (pl.program_id(1) == 0))
        def _():
            dw_ref[...] = dw0_ref[...]

        @pl.when(s != 1)
        def _():
            same, rot = _head_mats()
            x = x_ref[...]
            w = w_ref[0]
            r = lax.rsqrt(_xdot(x * x, same) * (1.0 / SW_HEAD_DIM) + NORM_EPS)
            xh = x * r
            dxn = dout * c_ref[...] + _xdot(dout * s_ref[...], rot, 1, 1)
            dw_ref[0] += _colsum(dxn * xh)
            dxh = dxn * w
            dx_ref[...] = (r * (dxh - xh * (_xdot(dxh * xh, same) * (1.0 / SW_HEAD_DIM)))).astype(BF16)

    return _call(
        body, name=f"sw_prep_bwd{p}", grid=(3, t // tm),
        in_specs=[pl.BlockSpec((tm, SW_WIDTH), lambda s, i: (i, seg_col(s))),
                  pl.BlockSpec((1, 1, SW_WIDTH), lambda s, i: (1 - s // 2, 0, 0)),
                  pl.BlockSpec((tm, SW_WIDTH), lambda s, i: (i, 0)),
                  pl.BlockSpec((tm, SW_WIDTH), lambda s, i: (i, 0)),
                  pl.BlockSpec((1, 2, tm, LANE), lambda s, i: (s, 0, i, 0)), ANY,
                  pl.BlockSpec((1, 1, SW_WIDTH), lambda s, i: (s // 2, 0, 0))],
        out_specs=[pl.BlockSpec((tm, SW_WIDTH), lambda s, i: (i, seg_col(s))),
                   pl.BlockSpec((1, 1, SW_WIDTH), lambda s, i: (s // 2, 0, 0))],
        out_shape=[jax.ShapeDtypeStruct(dproj.shape, dproj.dtype), jax.ShapeDtypeStruct((2, 1, SW_WIDTH), F32)],
        semantics=("arbitrary", "arbitrary"), aliases={5: 0},
    )(proj, nw2, cos_t, sin_t, dkvq, dproj, dnw)


_SW_SCALE = SW_HEAD_DIM ** -0.5
_NEG = -1e30


def _sw_masks(has_other):
    ri = _iota2((SW_BLOCK, SW_BLOCK), 0)
    ci = _iota2((SW_BLOCK, SW_BLOCK), 1)
    return ri >= ci, (ci >= ri) & has_other


def _pair(x):
    first = _iota2((1, LANE), 1) < SW_HEAD_DIM
    return jnp.stack([jnp.where(first, x, 0.0), jnp.where(first, 0.0, x)])


def _both(x):
    return jnp.broadcast_to(x.astype(BF16)[None], (2,) + x.shape)


def _unpair(x2):
    first = _iota2((1, LANE), 1) < SW_HEAD_DIM
    return jnp.where(first, x2[0], x2[1])


def _head_cols(x):
    return jnp.stack([x[:, 0:1], x[:, SW_HEAD_DIM:SW_HEAD_DIM + 1]])


SW_GROUP = 8


def _sw_geometry(t, p):
    dil = SW_DILATIONS[p]
    unit = SW_BLOCK * dil
    nb = max(1, SW_GROUP // dil)
    return dil, unit, nb, t // (unit * nb)


def _sw_groups(dil, nb, body):
    if nb * dil == SW_GROUP:
        body([(k // dil, k % dil) for k in range(SW_GROUP)])
    else:
        for g in range(nb * dil // SW_GROUP):
            body([(0, SW_GROUP * g + k) for k in range(SW_GROUP)])


def _sw_rows(i, r, dil):
    start = i * SW_BLOCK * dil + r
    return pl.ds(start, SW_BLOCK) if dil == 1 else pl.ds(start, SW_BLOCK, stride=dil)


def _sw_load(refs, probs, dil, shift, wrap, fn):
    out = []
    for i, r in probs:
        if shift != 0 and i == wrap:
            out.append(fn(refs[1][_sw_rows(0, r, dil), :]))
        else:
            out.append(fn(refs[0][_sw_rows(i + shift, r, dil), :]))
    return jnp.concatenate(out, axis=0)


def _sw_other_masks(probs, wrap, edge_ok):
    _, other = _sw_masks(edge_ok)
    _, always = _sw_masks(True)
    return jnp.stack([other if i == wrap else always for i, _ in probs for _ in range(2)])


def sw_attn(qk, proj, p):
    t = proj.shape[0]
    dil, unit, nb, nsp = _sw_geometry(t, p)
    vcol = (C_SW + 3 * SW_WIDTH * p + 2 * SW_WIDTH) // LANE

    def body(q_ref, kc_ref, kp_ref, vc_ref, vp_ref, o_ref, l_ref):
        mc, _ = _sw_masks(True)
        first = pl.program_id(1) != 0
        q_r, k_r, v_r = (q_ref.at[0, 0], None), (kc_ref.at[0, 0], kp_ref.at[0, 0]), (vc_ref, vp_ref)

        def one(probs):
            mp = _sw_other_masks(probs, 0, first)
            q2 = _sw_load(q_r, probs, dil, 0, 0, _pair)
            sc = jnp.where(mc, _bdot(q2, _sw_load(k_r, probs, dil, 0, 0, _both), 1, 1) * _SW_SCALE, _NEG)
            sp = jnp.where(mp, _bdot(q2, _sw_load(k_r, probs, dil, -1, 0, _both), 1, 1) * _SW_SCALE, _NEG)
            mx = jnp.maximum(jnp.max(sc, axis=-1, keepdims=True), jnp.max(sp, axis=-1, keepdims=True))
            pc = jnp.exp(sc - mx)
            pp = jnp.exp(sp - mx)
            den = _rowsum(pc) + _rowsum(pp)
            o2 = (_bdot(pc, _sw_load(v_r, probs, dil, 0, 0, _both))
                  + _bdot(pp, _sw_load(v_r, probs, dil, -1, 0, _both))) * (1.0 / den)
            l2 = jnp.broadcast_to(mx + jnp.log(den), o2.shape)
            for n, (i, r) in enumerate(probs):
                o_ref.at[0][_sw_rows(i, r, dil), :] = _unpair(o2[2 * n:2 * n + 2])
                l_ref.at[0][_sw_rows(i, r, dil), :] = _unpair(l2[2 * n:2 * n + 2])

        _sw_groups(dil, nb, one)

    before = lambda j: jnp.maximum(j * nb - 1, 0)
    seg = lambda s: pl.BlockSpec((1, 1, unit * nb, LANE), lambda h, j: (s, h, j, 0))
    seg_b = lambda s: pl.BlockSpec((1, 1, unit, LANE), lambda h, j: (s, h, before(j), 0))
    out = pl.BlockSpec((1, unit * nb, LANE), lambda h, j: (h, j, 0))
    shp = jax.ShapeDtypeStruct((2, t, LANE), F32)
    return _call(
        body, name=f"sw_attn{p}", grid=(2, nsp),
        in_specs=[seg(2 * p), seg(2 * p + 1), seg_b(2 * p + 1),
                  pl.BlockSpec((unit * nb, LANE), lambda h, j: (j, vcol + h)),
                  pl.BlockSpec((unit, LANE), lambda h, j: (before(j), vcol + h))],
        out_specs=[out, out], out_shape=[shp, shp], semantics=("parallel", "parallel"),
    )(qk, qk, qk, proj, proj)


def sw_attn_dkv(qk, proj, dy, lg, dm, p):
    t = proj.shape[0]
    dil, unit, nb, nsp = _sw_geometry(t, p)
    nunits = t // unit
    vcol = (C_SW + 3 * SW_WIDTH * p + 2 * SW_WIDTH) // LANE
    ycol = (DN_WIDTH + GM_WIDTH) // LANE

    def body(k_ref, v_ref, qc_ref, qn_ref, doc_ref, don_ref, lc_ref, ln_ref, dc_ref, dn_ref, o_ref):
        mc, _ = _sw_masks(True)
        more = pl.program_id(1) + 1 < nsp
        q_r, do_r = (qc_ref.at[0, 0], qn_ref.at[0, 0]), (doc_ref, don_ref)
        l_r, d_r = (lc_ref.at[0], ln_ref.at[0]), (dc_ref.at[0], dn_ref.at[0])

        def one(probs):
            k2 = _sw_load((k_ref.at[0, 0], None), probs, dil, 0, 0, _both)
            v2 = _sw_load((v_ref, None), probs, dil, 0, 0, _both)
            dk = jnp.zeros((2 * SW_GROUP, SW_BLOCK, LANE), F32)
            dv = jnp.zeros((2 * SW_GROUP, SW_BLOCK, LANE), F32)
            for shift, mk in ((0, mc), (1, _sw_other_masks(probs, nb - 1, more))):
                q2 = _sw_load(q_r, probs, dil, shift, nb - 1, _pair)
                do2 = _sw_load(do_r, probs, dil, shift, nb - 1, _pair)
                lse = _sw_load(l_r, probs, dil, shift, nb - 1, _head_cols)
                dd = _sw_load(d_r, probs, dil, shift, nb - 1, _head_cols)
                pr = jnp.exp(jnp.where(mk, _bdot(q2, k2, 1, 1) * _SW_SCALE, _NEG) - lse)
                dv = dv + _bdot(pr, do2, 0, 0)
                ds = pr * (_bdot(do2, v2, 1, 1) - dd)
                dk = dk + _bdot(ds, q2, 0, 0)
            for n, (i, r) in enumerate(probs):
                o_ref.at[0, 0][_sw_rows(i, r, dil), :] = (dk[2 * n] + dk[2 * n + 1]) * _SW_SCALE
                o_ref.at[1, 0][_sw_rows(i, r, dil), :] = dv[2 * n] + dv[2 * n + 1]

        _sw_groups(dil, nb, one)

    after = lambda j: jnp.minimum((j + 1) * nb, nunits - 1)
    seg = lambda s: pl.BlockSpec((1, 1, unit * nb, LANE), lambda h, j: (s, h, j, 0))
    seg_a = lambda s: pl.BlockSpec((1, 1, unit, LANE), lambda h, j: (s, h, after(j), 0))
    col = lambda c0: pl.BlockSpec((unit * nb, LANE), lambda h, j: (j, c0 + h))
    col_a = lambda c0: pl.BlockSpec((unit, LANE), lambda h, j: (after(j), c0 + h))
    hp = pl.BlockSpec((1, unit * nb, LANE), lambda h, j: (h, j, 0))
    hp_a = pl.BlockSpec((1, unit, LANE), lambda h, j: (h, after(j), 0))
    return _call(
        body, name=f"sw_dkv{p}", grid=(2, nsp),
        in_specs=[seg(2 * p + 1), col(vcol), seg(2 * p), seg_a(2 * p), col(ycol), col_a(ycol), hp, hp_a, hp, hp_a],
        out_specs=pl.BlockSpec((2, 1, unit * nb, LANE), lambda h, j: (0, h, j, 0)),
        out_shape=jax.ShapeDtypeStruct((3, 2, t, LANE), F32), semantics=("parallel", "parallel"),
    )(qk, proj, qk, qk, dy, dy, lg, lg, dm, dm)


def sw_attn_dq(qk, proj, dy, lg, dm, dkvq, p):
    t = proj.shape[0]
    dil, unit, nb, nsp = _sw_geometry(t, p)
    vcol = (C_SW + 3 * SW_WIDTH * p + 2 * SW_WIDTH) // LANE
    ycol = (DN_WIDTH + GM_WIDTH) // LANE

    def body(q_ref, kc_ref, kp_ref, vc_ref, vp_ref, do_ref, l_ref, d_ref, _, dq_ref):
        mc, _ = _sw_masks(True)
        first = pl.program_id(1) != 0
        k_r, v_r = (kc_ref.at[0, 0], kp_ref.at[0, 0]), (vc_ref, vp_ref)

        def one(probs):
            mp = _sw_other_masks(probs, 0, first)
            q2 = _sw_load((q_ref.at[0, 0], None), probs, dil, 0, 0, _pair)
            do2 = _sw_load((do_ref, None), probs, dil, 0, 0, _pair)
            lse = _sw_load((l_ref.at[0], None), probs, dil, 0, 0, _head_cols)
            dd = _sw_load((d_ref.at[0], None), probs, dil, 0, 0, _head_cols)
            kc = _sw_load(k_r, probs, dil, 0, 0, _both)
            kp = _sw_load(k_r, probs, dil, -1, 0, _both)
            pc = jnp.exp(jnp.where(mc, _bdot(q2, kc, 1, 1) * _SW_SCALE, _NEG) - lse)
            pp = jnp.exp(jnp.where(mp, _bdot(q2, kp, 1, 1) * _SW_SCALE, _NEG) - lse)
            dsc = pc * (_bdot(do2, _sw_load(v_r, probs, dil, 0, 0, _both), 1, 1) - dd)
            dsp = pp * (_bdot(do2, _sw_load(v_r, probs, dil, -1, 0, _both), 1, 1) - dd)
            dq2 = (_bdot(dsc, kc) + _bdot(dsp, kp)) * _SW_SCALE
            for n, (i, r) in enumerate(probs):
                dq_ref.at[0, 0][_sw_rows(i, r, dil), :] = _unpair(dq2[2 * n:2 * n + 2])

        _sw_groups(dil, nb, one)

    before = lambda j: jnp.maximum(j * nb - 1, 0)
    seg = lambda s: pl.BlockSpec((1, 1, unit * nb, LANE), lambda h, j: (s, h, j, 0))
    seg_b = lambda s: pl.BlockSpec((1, 1, unit, LANE), lambda h, j: (s, h, before(j), 0))
    col = lambda c0: pl.BlockSpec((unit * nb, LANE), lambda h, j: (j, c0 + h))
    col_b = lambda c0: pl.BlockSpec((unit, LANE), lambda h, j: (before(j), c0 + h))
    hp = pl.BlockSpec((1, unit * nb, LANE), lambda h, j: (h, j, 0))
    return _call(
        body, name=f"sw_dq{p}", grid=(2, nsp),
        in_specs=[seg(2 * p), seg(2 * p + 1), seg_b(2 * p + 1), col(vcol), col_b(vcol), col(ycol), hp, hp, ANY],
        out_specs=pl.BlockSpec((1, 1, unit * nb, LANE), lambda h, j: (2, h, j, 0)),
        out_shape=jax.ShapeDtypeStruct(dkvq.shape, F32), semantics=("parallel", "parallel"), aliases={8: 0},
    )(qk, qk, qk, proj, proj, dy, lg, dm, dkvq)


def sw_merge(outs, lses, ybuf, *, tm=512):
    t = ybuf.shape[0]

    def body(o0, o1, o2, l0_ref, l1_ref, l2_ref, _, y_ref, lg_ref):
        l0, l1, l2 = l0_ref[...], l1_ref[...], l2_ref[...]
        mx = jnp.maximum(jnp.maximum(l0, l1), l2)
        lg = mx + jnp.log(jnp.exp(l0 - mx) + jnp.exp(l1 - mx) + jnp.exp(l2 - mx))
        lg_ref[...] = lg
        y = jnp.exp(l0 - lg) * o0[...] + jnp.exp(l1 - lg) * o1[...] + jnp.exp(l2 - lg) * o2[...]
        y_ref[...] = jnp.concatenate([y[0], y[1]], axis=1)

    hp = pl.BlockSpec((2, tm, LANE), lambda i: (0, i, 0))
    return _call(
        body, name="sw_merge", grid=(t // tm,), in_specs=[hp] * 6 + [ANY],
        out_specs=[pl.BlockSpec((tm, SW_WIDTH), lambda i: (i, (DN_WIDTH + GM_WIDTH) // SW_WIDTH)), hp],
        out_shape=[jax.ShapeDtypeStruct(ybuf.shape, F32), jax.ShapeDtypeStruct((2, t, LANE), F32)],
        semantics=("parallel",), aliases={6: 0},
    )(*outs, *lses, ybuf)


def sw_delta(dy, ybuf, *, tm=512):
    t = ybuf.shape[0]

    def body(dy_ref, y_ref, o_ref):
        same, _ = _head_mats()
        o_ref[0], o_ref[1] = _halves(_xdot(dy_ref[...] * y_ref[...], same))

    b1 = pl.BlockSpec((tm, SW_WIDTH), lambda i: (i, (DN_WIDTH + GM_WIDTH) // SW_WIDTH))
    return _call(body, name="sw_delta", grid=(t // tm,), in_specs=[b1, b1],
                 out_specs=pl.BlockSpec((2, tm, LANE), lambda i: (0, i, 0)),
                 out_shape=jax.ShapeDtypeStruct((2, t, LANE), F32), semantics=("parallel",))(dy, ybuf)


def _rope_tables(t):
    inv = ROPE_THETA ** (-jnp.arange(0, ROPE_DIM, 2, dtype=F32) / ROPE_DIM)
    ang = jnp.arange(t, dtype=F32)[:, None] * inv[None, :]
    pad1 = jnp.ones((t, SW_HEAD_DIM - ROPE_DIM), F32)
    pad0 = jnp.zeros((t, SW_HEAD_DIM - ROPE_DIM), F32)
    cos_h = jnp.concatenate([jnp.cos(ang), jnp.cos(ang), pad1], axis=1)
    sin_h = jnp.concatenate([jnp.sin(ang), jnp.sin(ang), pad0], axis=1)
    return jnp.tile(cos_h, (1, SW_HEADS)), jnp.tile(sin_h, (1, SW_HEADS))


def sw_forward(proj, nw2, cos_t, sin_t, ybuf):
    qk = sw_prep(proj, nw2, cos_t, sin_t)
    outs, lses = [], []
    for p in range(len(SW_DILATIONS)):
        o, lse = sw_attn(qk, proj, p)
        outs.append(o)
        lses.append(lse)
    ybuf, lg = sw_merge(outs, lses, ybuf)
    return ybuf, (qk, lg)


def sw_backward(proj, nw2, cos_t, sin_t, res, ybuf, dy, dproj):
    qk, lg = res
    dm = sw_delta(dy, ybuf)
    dnw = jnp.zeros((2, 1, SW_WIDTH), F32)
    for p in range(len(SW_DILATIONS)):
        dkvq = sw_attn_dkv(qk, proj, dy, lg, dm, p)
        dkvq = sw_attn_dq(qk, proj, dy, lg, dm, dkvq, p)
        dproj, dnw = sw_prep_bwd(proj, nw2, cos_t, sin_t, dkvq, dproj, dnw, p)
    return dproj, dnw[::-1, 0]


def _pad_rows(a, rows):
    return jnp.zeros((rows,) + a.shape[1:], a.dtype).at[:a.shape[0]].set(a)


def _consts(sp):
    d = {}
    d["mix_nw"] = sp["mix_norm_w"][:, None, :]
    d["ffn_nw"] = sp["ffn_norm_w"][:, None, :]
    d["cw8"] = jnp.pad(sp["dn_conv_w"], ((0, 0), (0, 8 - DN_CONV), (0, 0)))
    d["scal"] = jnp.pad(jnp.stack([sp["dn_a_log"], sp["dn_dt_bias"]], axis=1), ((0, 0), (0, 6), (0, LANE - DN_HEADS)))
    d["wn"] = sp["dn_out_norm_w"][:, None, :]
    d["lng"] = sp["gm_ln_g"][:, None, :]
    d["lnb"] = sp["gm_ln_b"][:, None, :]
    d["w_s"] = sp["gm_w_s"]
    d["bst"] = jnp.pad(jnp.swapaxes(sp["gm_b_s"], 1, 2), ((0, 0), (0, 0), (0, LANE - GM_GROUPS)))
    d["nw2"] = jnp.stack([jnp.tile(sp["sw_q_norm_w"], (1, SW_HEADS)),
                          jnp.tile(sp["sw_k_norm_w"], (1, SW_HEADS))], axis=1)[:, :, None, :]
    return d


def _layer_fwd(x, mod, get_w, cs, tabs):
    wb = dict(get_w("w_in", x))
    h1, proj = norm_mm(x, cs["mix_nw"], mod[1], mod[0], wb["w_in"], swiglu=False, name="in_proj")
    act = dn_conv(proj, cs["cw8"])
    y, states, tinvs = dn_fwd(act, proj, cs["scal"], cs["wn"])
    y = gm_fwd(proj, cs["lng"], cs["lnb"], cs["w_s"], cs["bst"], y)
    y, swres = sw_forward(proj, cs["nw2"], *tabs, y)
    wb.update(get_w("w_out", y))
    x1, o1 = resid_mm(y, wb["w_out"], x, mod[2], name="out_proj")
    wb.update(get_w("ffn", x1))
    h2, gu, actf = norm_mm(x1, cs["ffn_nw"], mod[4], mod[3], wb["w_ffn_in"], swiglu=True, name="ffn_in")
    x2, o2 = resid_mm(actf, wb["w_ffn_out"], x1, mod[5], name="ffn_out")
    res = dict(x=x, h1=h1, proj=proj, act=act, states=states, tinvs=tinvs, swres=swres, y=y, x1=x1, o1=o1, h2=h2, gu=gu,
               actf=actf, o2=o2)
    return x2, res, wb


def _layer_bwd(dx2, res, mod, wb, cs, tabs, grads_done):
    dgu, gx2, dgate2 = resid_mm_bwd(dx2, mod[5], res["o2"], wb["w_ffn_out"], res["gu"], name="ffn_out_bwd", tm=512)
    g_wfo = mm_tn(res["actf"], gx2, name="wg_ffn_out")
    g_wfi = mm_tn(res["h2"], dgu, name="wg_ffn_in")
    token = grads_done("ffn", dict(w_ffn_in=g_wfi, w_ffn_out=g_wfo))
    dx1, d_ffn_nw, dscale2, dshift2 = norm_mm_bwd(dgu, wb["w_ffn_in"], res["x1"], cs["ffn_nw"], mod[4] + token, dx2,
                                                  name="ffn_in_bwd")
    dy, gx1, dgate1 = resid_mm_bwd(dx1, mod[2], res["o1"], wb["w_out"], None, name="out_proj_bwd", tm=512)
    g_wout = mm_tn(res["y"], gx1, name="wg_out")
    proj = res["proj"]
    dact, dproj, dpar = dn_bwd(res["act"], proj, cs["scal"], cs["wn"], res["states"], res["tinvs"], dy)
    dproj, dcw = dn_conv_bwd(proj, cs["cw8"], dact, dproj)
    dproj, dws, dbst, dln = gm_bwd(proj, cs["lng"], cs["lnb"], cs["w_s"], cs["bst"], dy, dproj)
    dproj, dnw = sw_backward(proj, cs["nw2"], *tabs, res["swres"], res["y"], dy, dproj)
    g_win = mm_tn(res["h1"], dproj, name="wg_in")
    dx, d_mix_nw, dscale1, dshift1 = norm_mm_bwd(dproj, wb["w_in"], res["x"], cs["mix_nw"], mod[1], dx1,
                                                 name="in_proj_bwd")
    dmod = jnp.concatenate([dshift1, dscale1, dgate1, dshift2, dscale2, dgate2], axis=1)
    dnw = dnw.reshape(2, SW_HEADS, SW_HEAD_DIM).sum(1)
    small = dict(mix_norm_w=d_mix_nw[0], ffn_norm_w=d_ffn_nw[0], dn_conv_w=dcw[:DN_CONV],
                 dn_a_log=dpar[0, :DN_HEADS], dn_dt_bias=dpar[1, :DN_HEADS], dn_out_norm_w=dpar[2],
                 gm_ln_g=dln[0], gm_ln_b=dln[1], gm_w_s=dws, gm_b_s=dbst[:, :GM_GROUPS].T,
                 sw_q_norm_w=dnw[0], sw_k_norm_w=dnw[1])
    token = grads_done("mix", dict(w_in=g_win, w_out=g_wout))
    return dx, small, dmod, token


def _permute_w_in(w):
    pad = jnp.zeros(w.shape[:-1] + (AB_PAD - 8,), w.dtype)
    return jnp.concatenate([w[..., 0:2056], pad, w[..., 2568:IN_WIDTH], w[..., 2056:2568]], axis=-1)


def _unpermute_w_in(g):
    return jnp.concatenate([g[..., 0:2056], g[..., C_UV:IN_PAD], g[..., C_SW:C_UV]], axis=-1)


def _local_step(x, target, mods, weights_of, grads_done, sp):
    layers = mods.shape[0]
    t, d = x.shape
    tabs = _rope_tables(t)
    consts = _consts(sp)
    saved = []
    for layer in range(layers):
        mod = mods[layer].reshape(6, 1, d)
        cs = {k: v[layer] for k, v in consts.items()}
        x, res, wb = _layer_fwd(x, mod, functools.partial(weights_of, layer), cs, tabs)
        saved.append((res, mod, wb, cs))
    dx, loss = loss_head(x, target)
    smalls, dmods = [], []
    token = jnp.zeros((1, 1), F32)
    for layer in reversed(range(layers)):
        res, mod, wb, cs = saved[layer]
        dx, small, dmod, token = _layer_bwd(dx, res, mod + token, wb, cs, tabs, functools.partial(grads_done, layer))
        smalls.append(small)
        dmods.append(dmod[0])
    smalls, dmods = smalls[::-1], dmods[::-1]
    small = {k: jnp.stack([s[k] for s in smalls]) for k in smalls[0]}
    return loss, dx, small, jnp.stack(dmods) + token


def mod_fwd(c_all, w_mod, b_shard):
    layers, d, n = w_mod.shape

    def body(c_ref, w_ref, b_ref, o_ref):
        ca = _silu(c_ref[...]).astype(BF16)
        o_ref[0] = _dot(ca, w_ref[0].astype(BF16), 1, 0) + b_ref[0]

    return _call(
        body, name="mod_fwd", grid=(layers,),
        in_specs=[_full((8, d)), pl.BlockSpec((1, d, n), lambda i: (i, 0, 0)),
                  pl.BlockSpec((1, 1, n), lambda i: (i, 0, 0))],
        out_specs=pl.BlockSpec((1, 8, n), lambda i: (i, 0, 0)),
        out_shape=jax.ShapeDtypeStruct((layers, 8, n), F32), semantics=("parallel",),
    )(c_all, w_mod, b_shard)


def mod_bwd(c_all, dmod):
    layers, _, n = dmod.shape
    d = c_all.shape[1]

    def body(c_ref, g_ref, o_ref):
        ca = _silu(c_ref[...]).astype(BF16)
        o_ref[0] = _dot(ca, g_ref[0].astype(BF16), 0, 0)

    return _call(
        body, name="mod_bwd", grid=(layers,),
        in_specs=[_full((8, d)), pl.BlockSpec((1, 8, n), lambda i: (i, 0, 0))],
        out_specs=pl.BlockSpec((1, d, n), lambda i: (i, 0, 0)),
        out_shape=jax.ShapeDtypeStruct((layers, d, n), F32), semantics=("parallel",),
    )(c_all, dmod)


N_DEV = 8


def _place():
    return lax.axis_index("x"), lax.axis_index("y"), lax.axis_index("c")


def _other_chips(x, y):
    return [(1 - x, y), (x, 1 - y), (1 - x, 1 - y)]


def allgather8(x_shard, *, name):
    m_per, n = x_shard.shape

    def body(x_ref, out_ref, send_sems, recv_sems, local_sem):
        x, y, c = _place()
        me, sibling = (x, y, c), (x, y, 1 - c)
        chips = _other_chips(x, y)

        def rows(px, py, pc):
            return out_ref.at[pl.ds((4 * px + 2 * py + pc) * m_per, m_per), :]

        def copy(k, block, to, src=None):
            return pltpu.make_async_remote_copy(
                src_ref=rows(*block) if src is None else src, dst_ref=rows(*block),
                send_sem=send_sems.at[k], recv_sem=recv_sems.at[k], device_id=to, device_id_type=MESH)

        mine = pltpu.make_async_copy(x_ref, rows(*me), local_sem)
        mine.start()
        first = [copy(0, me, sibling, src=x_ref)]
        first += [copy(1 + j, me, (*chip, c), src=x_ref) for j, chip in enumerate(chips)]
        for cp in first:
            cp.start()
        passed = [copy(4 + j, (*chip, c), sibling) for j, chip in enumerate(chips)]
        for j, chip in enumerate(chips):
            copy(1 + j, (*chip, c), me).wait_recv()
            passed[j].start()
        copy(0, sibling, me).wait_recv()
        for j, chip in enumerate(chips):
            copy(4 + j, (*chip, 1 - c), me).wait_recv()
        for cp in first + passed:
            cp.wait_send()
        mine.wait()

    return pl.pallas_call(
        body, name=name, out_shape=jax.ShapeDtypeStruct((N_DEV * m_per, n), x_shard.dtype),
        in_specs=[pl.BlockSpec(memory_space=pltpu.VMEM)], out_specs=pl.BlockSpec(memory_space=pltpu.VMEM),
        scratch_shapes=[pltpu.SemaphoreType.DMA((7,)), pltpu.SemaphoreType.DMA((7,)), pltpu.SemaphoreType.DMA],
    )(x_shard)


HBM = pl.BlockSpec(memory_space=pltpu.HBM)
SEM = pl.BlockSpec(memory_space=pltpu.SEMAPHORE)
_EFFECT = pltpu.SideEffectType.DATAFLOW_SIDE_EFFECTING


def _piece(ref, sliced, chip):
    return ref.at[2 * chip[0] + chip[1]] if sliced else ref


def exchange_start(srcs, after, *, sliced, name):
    n = len(srcs)
    piece = lambda s: s.shape[1:] if sliced else s.shape

    def body(*refs):
        ins, lands = refs[:n], refs[n:2 * n]
        send_sems, recv_sems = refs[2 * n + len(after):2 * n + len(after) + 2]
        token = refs[-1]
        x, y, c = _place()
        me_s = 2 * x + y
        for a in range(n):
            for j, chip in enumerate(_other_chips(x, y)):
                pltpu.make_async_remote_copy(
                    src_ref=_piece(ins[a], sliced, chip), dst_ref=lands[a].at[me_s], send_sem=send_sems.at[3 * a + j],
                    recv_sem=recv_sems.at[3 * a + j], device_id=(*chip, c), device_id_type=MESH).start()
        token[...] = jnp.zeros_like(token)

    zones = [pltpu.with_memory_space_constraint(lax.empty((4,) + piece(s), s.dtype), pltpu.HBM) for s in srcs]
    srcs = [pltpu.with_memory_space_constraint(s, pltpu.HBM) for s in srcs]
    out = pl.pallas_call(
        body, name=name,
        out_shape=(pltpu.SemaphoreType.DMA((3 * n,)), pltpu.SemaphoreType.DMA((3 * n,)),
                   *[pltpu.HBM(s.shape, s.dtype) for s in srcs], *[pltpu.HBM(z.shape, z.dtype) for z in zones],
                   jax.ShapeDtypeStruct((8, LANE), F32)),
        in_specs=[HBM] * (2 * n) + [ANY] * len(after),
        out_specs=(SEM, SEM, *[HBM] * (2 * n), pl.BlockSpec(memory_space=pltpu.VMEM)),
        input_output_aliases={i: 2 + i for i in range(2 * n)},
        compiler_params=pltpu.CompilerParams(has_side_effects=_EFFECT),
    )(*srcs, *zones, *after)
    return out[0], out[1], out[2:2 + n], out[2 + n:2 + 2 * n], out[-1]


def exchange_wait(send_sems, recv_sems, srcs, zones, after, *, which, sliced, name):
    n = len(srcs)

    def body(*refs):
        ins, lands = refs[:n], refs[n:2 * n]
        send_sems, recv_sems = refs[2 * n:2 * n + 2]
        x, y, c = _place()
        for a in range(n):
            for j, chip in enumerate(_other_chips(x, y)):
                copy = pltpu.make_async_remote_copy(
                    src_ref=_piece(ins[a], sliced, chip), dst_ref=lands[a].at[2 * chip[0] + chip[1]],
                    send_sem=send_sems.at[3 * which[a] + j], recv_sem=recv_sems.at[3 * which[a] + j],
                    device_id=(*chip, c), device_id_type=MESH)
                copy.wait_send()
                copy.wait_recv()

    out = pl.pallas_call(
        body, name=name,
        out_shape=tuple(pltpu.HBM(s.shape, s.dtype) for s in (*srcs, *zones)),
        in_specs=[HBM] * (2 * n) + [SEM, SEM, ANY], out_specs=tuple([HBM] * (2 * n)),
        input_output_aliases={i: i for i in range(2 * n)},
        compiler_params=pltpu.CompilerParams(has_side_effects=_EFFECT),
    )(*srcs, *zones, send_sems, recv_sems, after)
    return out[n:]


def sibling_swap(parts):
    n = len(parts)

    def body(*refs):
        ins, outs = refs[:n], refs[n:2 * n]
        send_sems, recv_sems = refs[2 * n:]
        x, y, c = _place()
        cps = []
        for a in range(n):
            cp = pltpu.make_async_remote_copy(
                src_ref=ins[a], dst_ref=outs[a], send_sem=send_sems.at[a], recv_sem=recv_sems.at[a],
                device_id=(x, y, 1 - c), device_id_type=MESH)
            cp.start()
            cps.append(cp)
        for cp in cps:
            cp.wait()

    return pl.pallas_call(
        body, name="sibling_swap", out_shape=[jax.ShapeDtypeStruct(p.shape, p.dtype) for p in parts],
        in_specs=[ANY] * n, out_specs=[ANY] * n,
        scratch_shapes=[pltpu.SemaphoreType.DMA((n,)), pltpu.SemaphoreType.DMA((n,))],
    )(*parts)


def _row_block(rows, cols, budget=1 << 20):
    best = rows if rows % 8 else 8
    for tr in range(8, rows + 1, 8):
        if rows % tr == 0 and tr * cols * 4 <= budget:
            best = tr
    return best


def chip_sum(own, recv, me_s, buf, layer, layers, *, name):
    r, n = own.shape
    tr = _row_block(r, n)
    steps = r // tr

    def body(me_ref, own_ref, recv_ref, *rest):
        o_ref = rest[-1]
        me = me_ref[0]
        acc = jnp.zeros((tr, n), F32)
        for s in range(4):
            acc = acc + jnp.where(me == s, own_ref[...], recv_ref[s].astype(F32))
        o_ref[...] = acc

    in_specs = [pl.BlockSpec((tr, n), lambda i, me: (i, 0)), pl.BlockSpec((4, tr, n), lambda i, me: (0, i, 0))]
    args = [me_s, own, recv]
    aliases = {}
    if buf is not None:
        in_specs.append(ANY)
        args.append(buf)
        aliases = {3: 0}
    return pl.pallas_call(
        body, name=name, out_shape=jax.ShapeDtypeStruct((layers * r, n), F32),
        grid_spec=pltpu.PrefetchScalarGridSpec(
            num_scalar_prefetch=1, grid=(steps,), in_specs=in_specs,
            out_specs=pl.BlockSpec((tr, n), lambda i, me: (layer * steps + i, 0))),
        input_output_aliases=aliases,
        compiler_params=pltpu.CompilerParams(dimension_semantics=("parallel",)),
    )(*args)


def _adam_update(w, g, m, v):
    m2 = ADAM_B1 * m + (1.0 - ADAM_B1) * g
    v2 = ADAM_B2 * v + (1.0 - ADAM_B2) * (g * g)
    m_hat = m2 / (1.0 - ADAM_B1 ** ADAM_STEP)
    v_hat = v2 / (1.0 - ADAM_B2 ** ADAM_STEP)
    delta = -ADAM_LR * (m_hat / (jnp.sqrt(v_hat) + ADAM_EPS) + ADAM_WD * w)
    return delta, m2, v2


def adamw(w, g_parts, m, v, *, name):
    r, n = w.shape
    tr = _row_block(r, n)
    k = len(g_parts)

    def body(*refs):
        w_ref, m_ref, v_ref = refs[k], refs[k + 1], refs[k + 2]
        g_ref, d_ref, m2_ref, v2_ref = refs[k + 3:]
        g = refs[0][...]
        for p in refs[1:k]:
            g = g + p[...]
        g_ref[...] = g
        d_ref[...], m2_ref[...], v2_ref[...] = _adam_update(w_ref[...], g, m_ref[...], v_ref[...])

    blk = pl.BlockSpec((tr, n), lambda i: (i, 0))
    shp = jax.ShapeDtypeStruct((r, n), F32)
    return _call(body, name=name, grid=(r // tr,), in_specs=[blk] * (k + 3), out_specs=[blk] * 4,
                 out_shape=[shp] * 4, semantics=("parallel",))(*g_parts, w, m, v)


def adamw_gathered(g_all, w, m, v, *, name):
    _, r, n = g_all.shape
    tr = _row_block(r, n * 4)

    def body(ga_ref, w_ref, m_ref, v_ref, g_ref, d_ref, m2_ref, v2_ref):
        g = ga_ref[0]
        for dev in range(1, N_DEV):
            g = g + ga_ref[dev]
        g_ref[...] = g
        d_ref[...], m2_ref[...], v2_ref[...] = _adam_update(w_ref[...], g, m_ref[...], v_ref[...])

    blk = pl.BlockSpec((tr, n), lambda i: (i, 0))
    shp = jax.ShapeDtypeStruct((r, n), F32)
    return _call(body, name=name, grid=(r // tr,),
                 in_specs=[pl.BlockSpec((N_DEV, tr, n), lambda i: (0, i, 0)), blk, blk, blk], out_specs=[blk] * 4,
                 out_shape=[shp] * 4, semantics=("parallel",))(g_all, w, m, v)


BIG = ("w_in", "w_out", "w_ffn_in", "w_ffn_out")
SMALL = ("b_mod", "mix_norm_w", "ffn_norm_w", "dn_conv_w", "dn_a_log", "dn_dt_bias", "dn_out_norm_w", "gm_ln_g",
         "gm_ln_b", "gm_w_s", "gm_b_s", "sw_q_norm_w", "sw_k_norm_w")
WEIGHTS = ("w_mod", "b_mod", "mix_norm_w", "ffn_norm_w", "w_in", "w_out", "dn_conv_w", "dn_a_log", "dn_dt_bias",
           "dn_out_norm_w", "gm_ln_g", "gm_ln_b", "gm_w_s", "gm_b_s", "sw_q_norm_w", "sw_k_norm_w", "w_ffn_in",
           "w_ffn_out")
PACK_ROWS = 8


def _pack(arrs):
    out = []
    for a in arrs:
        flat = a.reshape(-1).astype(F32)
        rows = -(-flat.shape[0] // (LANE * PACK_ROWS)) * PACK_ROWS
        out.append(jnp.pad(flat, (0, rows * LANE - flat.shape[0])).reshape(rows, LANE))
    return jnp.concatenate(out, axis=0)


def _unpack(packed, shapes):
    out, r0 = [], 0
    for shp in shapes:
        size = math.prod(shp)
        rows = -(-size // (LANE * PACK_ROWS)) * PACK_ROWS
        out.append(packed[r0:r0 + rows].reshape(-1)[:size].reshape(shp))
        r0 += rows
    return out


def kernel(x, c, w_mod, b_mod, mix_norm_w, ffn_norm_w, w_in, w_out, dn_conv_w, dn_a_log, dn_dt_bias, dn_out_norm_w, gm_ln_g, gm_ln_b, gm_w_s, gm_b_s, sw_q_norm_w, sw_k_norm_w, w_ffn_in, w_ffn_out, loss_target, m_w_mod, m_b_mod, m_mix_norm_w, m_ffn_norm_w, m_w_in, m_w_out, m_dn_conv_w, m_dn_a_log, m_dn_dt_bias, m_dn_out_norm_w, m_gm_ln_g, m_gm_ln_b, m_gm_w_s, m_gm_b_s, m_sw_q_norm_w, m_sw_k_norm_w, m_w_ffn_in, m_w_ffn_out, v_w_mod, v_b_mod, v_mix_norm_w, v_ffn_norm_w, v_w_in, v_w_out, v_dn_conv_w, v_dn_a_log, v_dn_dt_bias, v_dn_out_norm_w, v_gm_ln_g, v_gm_ln_b, v_gm_w_s, v_gm_b_s, v_sw_q_norm_w, v_sw_k_norm_w, v_w_ffn_in, v_w_ffn_out):
    w = dict(w_mod=w_mod, b_mod=b_mod, mix_norm_w=mix_norm_w, ffn_norm_w=ffn_norm_w, w_in=w_in, w_out=w_out,
             dn_conv_w=dn_conv_w, dn_a_log=dn_a_log, dn_dt_bias=dn_dt_bias, dn_out_norm_w=dn_out_norm_w,
             gm_ln_g=gm_ln_g, gm_ln_b=gm_ln_b, gm_w_s=gm_w_s, gm_b_s=gm_b_s, sw_q_norm_w=sw_q_norm_w,
             sw_k_norm_w=sw_k_norm_w, w_ffn_in=w_ffn_in, w_ffn_out=w_ffn_out)
    m = dict(w_mod=m_w_mod, b_mod=m_b_mod, mix_norm_w=m_mix_norm_w, ffn_norm_w=m_ffn_norm_w, w_in=m_w_in,
             w_out=m_w_out, dn_conv_w=m_dn_conv_w, dn_a_log=m_dn_a_log, dn_dt_bias=m_dn_dt_bias,
             dn_out_norm_w=m_dn_out_norm_w, gm_ln_g=m_gm_ln_g, gm_ln_b=m_gm_ln_b, gm_w_s=m_gm_w_s, gm_b_s=m_gm_b_s,
             sw_q_norm_w=m_sw_q_norm_w, sw_k_norm_w=m_sw_k_norm_w, w_ffn_in=m_w_ffn_in, w_ffn_out=m_w_ffn_out)
    v = dict(w_mod=v_w_mod, b_mod=v_b_mod, mix_norm_w=v_mix_norm_w, ffn_norm_w=v_ffn_norm_w, w_in=v_w_in,
             w_out=v_w_out, dn_conv_w=v_dn_conv_w, dn_a_log=v_dn_a_log, dn_dt_bias=v_dn_dt_bias,
             dn_out_norm_w=v_dn_out_norm_w, gm_ln_g=v_gm_ln_g, gm_ln_b=v_gm_ln_b, gm_w_s=v_gm_w_s, gm_b_s=v_gm_b_s,
             sw_q_norm_w=v_sw_q_norm_w, sw_k_norm_w=v_sw_k_norm_w, w_ffn_in=v_w_ffn_in, w_ffn_out=v_w_ffn_out)
    layers, d, mod_n = w_mod.shape
    mx, my, mc = _place()
    me_s = 2 * mx + my
    me_dev = 4 * mx + 2 * my + mc

    c_all = allgather8(_pad_rows(c, 8), name="gather_c").reshape(N_DEV, 8, d)[:, 0]
    b_shard = lax.dynamic_slice_in_dim(b_mod, me_s * mod_n, mod_n, axis=1)[:, None, :]
    mod_part = mod_fwd(c_all, w_mod, b_shard)
    mod_parts = allgather8(mod_part.reshape(layers * 8, mod_n), name="gather_mod")
    mod_parts = mod_parts.reshape(4, 2, layers, 8, mod_n)[:, 0]
    mod_all = mod_parts.transpose(1, 2, 0, 3).reshape(layers, 8, 4 * mod_n)
    mods = lax.dynamic_index_in_dim(mod_all, me_dev, axis=1, keepdims=False)

    cw = dn_conv_w.shape[-1]
    conv_rows = -(-layers * DN_CONV // 8) * 8
    conv_parts = allgather8(_pad_rows(dn_conv_w.reshape(layers * DN_CONV, cw), conv_rows), name="gather_conv")
    conv_parts = conv_parts.reshape(4, 2, conv_rows, cw)[:, 0, :layers * DN_CONV]
    conv_full = conv_parts.reshape(4, layers, DN_CONV, cw).transpose(1, 2, 0, 3).reshape(layers, DN_CONV, 4 * cw)

    shards = {k: w[k].astype(BF16) for k in BIG}
    groups = dict(w_in=(0,), w_out=(1,), ffn=(2, 3))
    gathers = [exchange_start([shards[k][layer] for k in BIG], [mods, conv_full], sliced=False, name=f"gather_start{layer}")
               for layer in range(layers)]
    mods = mods + sum(g[4][0, 0] for g in gathers)

    def weights_of(layer, group, after):
        send_sems, recv_sems, srcs, zones, _ = gathers[layer]
        which = groups[group]
        got = exchange_wait(send_sems, recv_sems, [srcs[a] for a in which], [zones[a] for a in which], after,
                            which=which, sliced=False, name=f"gather_wait_{group}{layer}")
        full = {BIG[a]: lax.dynamic_update_index_in_dim(z, shards[BIG[a]][layer], me_s, 0) for a, z in zip(which, got)}
        cols = lambda g: jnp.concatenate([g[s] for s in range(4)], axis=-1)
        shape = dict(w_in=lambda g: _permute_w_in(cols(g)), w_out=lambda g: g.reshape(-1, d), w_ffn_in=cols,
                     w_ffn_out=lambda g: g.reshape(-1, d))
        return {k: shape[k](g) for k, g in full.items()}

    scatters = {}
    last_scatter = []
    shard_axis = dict(w_in=1, w_out=0, w_ffn_in=1, w_ffn_out=0)

    def grads_done(layer, group, grads):
        fix = lambda k, g: _unpermute_w_in(g) if k == "w_in" else g
        send = [jnp.stack(jnp.split(fix(k, g16), 4, axis=shard_axis[k])) for k, (_, g16) in grads.items()]
        own = {}
        for k, (g32, _) in grads.items():
            g32 = fix(k, g32)
            size = g32.shape[shard_axis[k]] // 4
            own[k] = lax.dynamic_slice_in_dim(g32, me_s * size, size, axis=shard_axis[k])
        if (layer, group) == (0, "mix"):
            last_scatter.append((send, own))
            return jnp.zeros((1, 1), F32)
        started = exchange_start(send, [], sliced=True, name=f"scatter_start_{group}{layer}")
        scatters[layer, group] = (started, own)
        return started[4][:1, :1]

    sp = {k: w[k] for k in SMALL}
    sp["dn_conv_w"] = conv_full
    loss_blk, grad_x, small, dmods = _local_step(x[0], loss_target[0], mods, weights_of, grads_done, sp)
    loss = lax.psum(loss_blk[0, 0], ("x", "y", "c"))

    outs = {}
    small = dict(small, b_mod=dmods)
    packed = _pack([small[k] for k in SMALL])
    rows = packed.shape[0]
    g_all = allgather8(packed, name="gather_small").reshape(N_DEV, rows, LANE)
    send, own = last_scatter[0]
    scatters[0, "mix"] = (exchange_start(send, [g_all], sliced=True, name="scatter_start_mix0"), own)
    g_all = g_all + scatters[0, "mix"][0][4][0, 0]
    conv_zero = jnp.zeros((layers, DN_CONV, 3 * DN_WIDTH), F32)
    pk = lambda src: _pack([conv_zero if k == "dn_conv_w" else src[k] for k in SMALL])
    res = adamw_gathered(g_all, pk(w), pk(m), pk(v), name="adamw_small")
    shapes = [small[k].shape for k in SMALL]
    un = [_unpack(a, shapes) for a in res]
    for i, k in enumerate(SMALL):
        outs[k] = [un[j][i] for j in range(4)]
    g_conv = lax.dynamic_slice_in_dim(outs["dn_conv_w"][0], me_s * cw, cw, axis=2)
    flat = lambda a: a.reshape(-1, cw)
    res = adamw(flat(dn_conv_w), [flat(g_conv)], flat(m["dn_conv_w"]), flat(v["dn_conv_w"]), name="adamw_conv")
    outs["dn_conv_w"] = [a.reshape(dn_conv_w.shape) for a in res]

    b_rows = layers * 6 * d // LANE
    dmod_all = g_all[:, :b_rows].reshape(N_DEV, layers, 6 * d).transpose(1, 0, 2)
    dmod_shard = lax.dynamic_slice_in_dim(dmod_all, me_s * mod_n, mod_n, axis=2)
    g_wmod = mod_bwd(c_all, dmod_shard)
    flat = lambda a: a.reshape(-1, mod_n)
    res = adamw(flat(w_mod), [flat(g_wmod)], flat(m_w_mod), flat(v_w_mod), name="adamw_w_mod")
    outs["w_mod"] = [a.reshape(w_mod.shape) for a in res]

    me_arr = jnp.reshape(me_s, (1,)).astype(jnp.int32)
    partial = {k: None for k in BIG}
    for layer in range(layers):
        for group in ("ffn", "mix"):
            (send_sems, recv_sems, srcs, zones, _), own = scatters[layer, group]
            zones = exchange_wait(send_sems, recv_sems, srcs, zones, res[0], which=tuple(range(len(srcs))),
                                  sliced=True, name=f"scatter_wait_{group}{layer}")
            for k, z in zip(own, zones):
                partial[k] = chip_sum(own[k], z, me_arr, partial[k], layer, layers, name=f"chip_sum_{k}{layer}")
    partial = [partial[k] for k in BIG]
    theirs = sibling_swap(partial)
    for k, mine, other in zip(BIG, partial, theirs):
        shp = w[k].shape
        flat = lambda a: a.reshape(-1, shp[-1])
        res = adamw(flat(w[k]), [mine, other], flat(m[k]), flat(v[k]), name="adamw_" + k)
        outs[k] = [a.reshape(shp) for a in res]

    result = [loss, grad_x[None]]
    for j in range(4):
        result += [outs[k][j] for k in WEIGHTS]
    return tuple(result)
```

```python
import functools
import math

import jax
import jax.numpy as jnp
from jax import lax
from jax.experimental import pallas as pl
from jax.experimental.pallas import tpu as pltpu

F32 = jnp.float32
BF16 = jnp.bfloat16
HI = lax.Precision.HIGH

NORM_EPS = 1e-6
DN_HEADS = 4
DN_HEAD_DIM = 128
DN_WIDTH = 512
DN_CHUNK = 64
DN_CONV = 4
GM_WIDTH = 256
GM_GROUPS = 4
GM_GROUP_DIM = 64
GM_CHUNK = 128
SW_HEADS = 4
SW_HEAD_DIM = 64
SW_WIDTH = 256
SW_DILATIONS = (1, 4, 16)
SW_BLOCK = 128
ROPE_THETA = 500000.0
ROPE_DIM = 16
LANE = 128

C_QKV = 0
C_Z = 1536
C_AB = 2048
C_SW = 2304
C_UV = 4608
IN_WIDTH = 4872
IN_PAD = 5120
AB_PAD = C_SW - C_AB
MIX_WIDTH = 1024

ADAM_LR = 0.001
ADAM_B1 = 0.9
ADAM_B2 = 0.999
ADAM_EPS = 1e-08
ADAM_WD = 0.01
ADAM_STEP = 10

MESH = pl.DeviceIdType.MESH


BIG_VMEM = 56 << 20


def _call(body, *, name, grid, in_specs, out_specs, out_shape, scratch_shapes=(), semantics=None, aliases=None,
          vmem=None):
    if semantics is None:
        semantics = ("arbitrary",) * len(grid)
    return pl.pallas_call(
        body, name=name, grid=grid, in_specs=in_specs, out_specs=out_specs, out_shape=out_shape,
        scratch_shapes=list(scratch_shapes), input_output_aliases=aliases or {},
        compiler_params=pltpu.CompilerParams(dimension_semantics=semantics, vmem_limit_bytes=vmem),
    )


def _dot(a, b, ca, cb, prec=None):
    if a.ndim == 3:
        dims = (((ca + 1,), (cb + 1,)), ((0,), (0,)))
    else:
        dims = (((ca,), (cb,)), ((), ()))
    return lax.dot_general(a, b, dims, preferred_element_type=F32, precision=prec)


def _bdot(a, b, ca=1, cb=0):
    return _dot(a.astype(BF16), b.astype(BF16), ca, cb)


def _hdot(a, b, ca=1, cb=0):
    return _dot(a.astype(F32), b.astype(F32), ca, cb, HI)


def _split(x):
    hi = x.astype(BF16)
    return hi, (x - hi.astype(F32)).astype(BF16)


def _xdot(a, b, ca=1, cb=0, exact=1):
    if exact == 1:
        hi, lo = _split(a)
        e = b.astype(BF16)
        return _dot(hi, e, ca, cb) + _dot(lo, e, ca, cb)
    hi, lo = _split(b)
    e = a.astype(BF16)
    return _dot(e, hi, ca, cb) + _dot(e, lo, ca, cb)


def _sigmoid(x):
    return 0.5 * jnp.tanh(0.5 * x) + 0.5


def _silu(x):
    return x * _sigmoid(x)


def _dsilu(x):
    s = _sigmoid(x)
    return s * (1.0 + x * (1.0 - s))


def _softplus(x):
    return jnp.maximum(x, 0.0) + jnp.log(1.0 + jnp.exp(-jnp.abs(x)))


def _iota2(shape, dim):
    return lax.broadcasted_iota(jnp.int32, shape, dim)


def _rowsum(x):
    return jnp.sum(x, axis=-1, keepdims=True)


def _colsum(x):
    return jnp.sum(x, axis=-2, keepdims=True)


def _full(shape):
    return pl.BlockSpec(shape, lambda *_: (0,) * len(shape))


def _resident(shape):
    return pl.BlockSpec(shape, lambda *_: (0,) * len(shape), pipeline_mode=pl.Buffered(1))


ANY = pl.BlockSpec(memory_space=pl.ANY)


def _norm_mod(x, nw, scale, shift):
    r = lax.rsqrt(jnp.mean(x * x, axis=-1, keepdims=True) + NORM_EPS)
    xn = x * r
    return xn, r, (xn * nw) * (1.0 + scale) + shift


def norm_mm(x, nw, scale, shift, w, *, swiglu, name, tm=512):
    t, d = x.shape
    n = w.shape[1]
    half = n // 2

    def body(x_ref, nw_ref, sc_ref, sh_ref, w_ref, h_ref, y_ref, *act_ref):
        _, _, h = _norm_mod(x_ref[...], nw_ref[...], sc_ref[...], sh_ref[...])
        hb = h.astype(BF16)
        h_ref[...] = hb
        y = _dot(hb, w_ref[...], 1, 0)
        y_ref[...] = y.astype(y_ref.dtype)
        if swiglu:
            act_ref[0][...] = (_silu(y[:, :half]) * y[:, half:]).astype(BF16)

    row = lambda i: (i, 0)
    out_shape = [jax.ShapeDtypeStruct((t, d), BF16), jax.ShapeDtypeStruct((t, n), BF16 if swiglu else F32)]
    out_specs = [pl.BlockSpec((tm, d), row), pl.BlockSpec((tm, n), row)]
    if swiglu:
        out_shape.append(jax.ShapeDtypeStruct((t, half), BF16))
        out_specs.append(pl.BlockSpec((tm, half), row))
    return _call(
        body, name=name, grid=(t // tm,),
        in_specs=[pl.BlockSpec((tm, d), row), _full((1, d)), _full((1, d)), _full((1, d)), _resident((d, n))],
        out_specs=out_specs, out_shape=out_shape, semantics=("parallel",), vmem=BIG_VMEM,
    )(x, nw, scale, shift, w)


def resid_mm(y, w, x, gate, *, name, tm=512):
    t, k = y.shape
    d = w.shape[1]

    def body(y_ref, w_ref, x_ref, g_ref, xo_ref, o_ref):
        o = _dot(y_ref[...].astype(BF16), w_ref[...], 1, 0)
        o_ref[...] = o
        xo_ref[...] = x_ref[...] + g_ref[...] * o

    row = lambda i: (i, 0)
    return _call(
        body, name=name, grid=(t // tm,),
        in_specs=[pl.BlockSpec((tm, k), row), _resident((k, d)), pl.BlockSpec((tm, d), row), _full((1, d))],
        out_specs=[pl.BlockSpec((tm, d), row), pl.BlockSpec((tm, d), row)],
        out_shape=[jax.ShapeDtypeStruct((t, d), F32), jax.ShapeDtypeStruct((t, d), F32)],
        semantics=("parallel",), vmem=BIG_VMEM,
    )(y, w, x, gate)


def resid_mm_bwd(dx, gate, o, w, gu, *, name, tm):
    t, d = dx.shape
    k = w.shape[0]
    swiglu = gu is not None

    def body(dx_ref, g_ref, o_ref, w_ref, *rest):
        if swiglu:
            gu_ref, dy_ref, gx_ref, dg_ref = rest
        else:
            dy_ref, gx_ref, dg_ref = rest
        i = pl.program_id(0)
        dxv = dx_ref[...]
        gx = (dxv * g_ref[...]).astype(BF16)
        gx_ref[...] = gx
        part = _colsum(dxv * o_ref[...])

        @pl.when(i == 0)
        def _():
            dg_ref[...] = jnp.zeros_like(dg_ref)

        dg_ref[...] += part
        da = _dot(gx, w_ref[...], 1, 1)
        if swiglu:
            g = gu_ref[:, :k].astype(F32)
            u = gu_ref[:, k:].astype(F32)
            dy_ref[:, :k] = (da * u * _dsilu(g)).astype(BF16)
            dy_ref[:, k:] = (da * _silu(g)).astype(BF16)
        else:
            dy_ref[...] = da

    row = lambda i: (i, 0)
    in_specs = [pl.BlockSpec((tm, d), row), _full((1, d)), pl.BlockSpec((tm, d), row), _resident((k, d))]
    args = [dx, gate, o, w]
    if swiglu:
        in_specs.append(pl.BlockSpec((tm, 2 * k), row))
        args.append(gu)
        dy_shape = jax.ShapeDtypeStruct((t, 2 * k), BF16)
        dy_spec = pl.BlockSpec((tm, 2 * k), row)
    else:
        dy_shape = jax.ShapeDtypeStruct((t, k), F32)
        dy_spec = pl.BlockSpec((tm, k), row)
    return _call(
        body, name=name, grid=(t // tm,), in_specs=in_specs,
        out_specs=[dy_spec, pl.BlockSpec((tm, d), row), _full((1, d))],
        out_shape=[dy_shape, jax.ShapeDtypeStruct((t, d), BF16), jax.ShapeDtypeStruct((1, d), F32)], vmem=BIG_VMEM,
    )(*args)


def norm_mm_bwd(dy, w, x, nw, scale, dres, *, name, tm=512):
    t, n = dy.shape
    d = x.shape[1]
    steps = t // tm

    def body(dy_ref, w_ref, x_ref, nw_ref, sc_ref, dres_ref, dx_ref, dnw_ref, dsc_ref, dsh_ref):
        i = pl.program_id(0)
        dh = _dot(dy_ref[...].astype(BF16), w_ref[...], 1, 1)
        x = x_ref[...]
        r = lax.rsqrt(jnp.mean(x * x, axis=-1, keepdims=True) + NORM_EPS)
        xn = x * r
        a = nw_ref[...] * (1.0 + sc_ref[...])

        @pl.when(i == 0)
        def _():
            dnw_ref[...] = jnp.zeros_like(dnw_ref)
            dsh_ref[...] = jnp.zeros_like(dsh_ref)

        dnw_ref[...] += _colsum(dh * xn)
        dsh_ref[...] += _colsum(dh)
        dxn = dh * a
        dx_ref[...] = r * (dxn - xn * jnp.mean(dxn * xn, axis=-1, keepdims=True)) + dres_ref[...]

        @pl.when(i == steps - 1)
        def _():
            da = dnw_ref[...]
            dsc_ref[...] = da * nw_ref[...]
            dnw_ref[...] = da * (1.0 + sc_ref[...])

    row = lambda i: (i, 0)
    vec = jax.ShapeDtypeStruct((1, d), F32)
    return _call(
        body, name=name, grid=(steps,),
        in_specs=[pl.BlockSpec((tm, n), row), _resident((d, n)), pl.BlockSpec((tm, d), row), _full((1, d)),
                  _full((1, d)), pl.BlockSpec((tm, d), row)],
        out_specs=[pl.BlockSpec((tm, d), row), _full((1, d)), _full((1, d)), _full((1, d))],
        out_shape=[jax.ShapeDtypeStruct((t, d), F32), vec, vec, vec], vmem=BIG_VMEM,
    )(dy, w, x, nw, scale, dres)


def _pick_tn(n, k, budget=6 << 20):
    best = LANE
    for m in range(1, n // LANE + 1):
        tn = m * LANE
        if n % tn == 0 and k * tn * 4 <= budget:
            best = tn
    return best


def mm_tn(a, g, *, name, tt=2048):
    t, k = a.shape
    n = g.shape[1]
    tn = _pick_tn(n, k)
    tt = min(tt, t)
    steps = t // tt

    def body(a_ref, g_ref, o_ref, b_ref):
        i = pl.program_id(1)

        @pl.when(i == 0)
        def _():
            o_ref[...] = jnp.zeros_like(o_ref)

        o_ref[...] += _dot(a_ref[...].astype(BF16), g_ref[...].astype(BF16), 0, 0)

        @pl.when(i == steps - 1)
        def _():
            b_ref[...] = o_ref[...].astype(BF16)

    out = pl.BlockSpec((k, tn), lambda j, i: (0, j))
    return _call(
        body, name=name, grid=(n // tn, steps),
        in_specs=[pl.BlockSpec((tt, k), lambda j, i: (i, 0)), pl.BlockSpec((tt, tn), lambda j, i: (i, j))],
        out_specs=[out, out],
        out_shape=[jax.ShapeDtypeStruct((k, n), F32), jax.ShapeDtypeStruct((k, n), BF16)],
        semantics=("parallel", "arbitrary"), vmem=BIG_VMEM,
    )(a, g)


def loss_head(y, target, *, tm=512):
    t, d = y.shape
    steps = t // tm

    def body(y_ref, t_ref, dy_ref, l_ref, acc_ref):
        i = pl.program_id(0)

        @pl.when(i == 0)
        def _():
            acc_ref[...] = jnp.zeros_like(acc_ref)

        e = y_ref[...] - t_ref[...]
        dy_ref[...] = e * (1.0 / d)
        acc_ref[...] += _colsum(e * e)

        @pl.when(i == steps - 1)
        def _():
            tot = jnp.sum(acc_ref[...], axis=-1, keepdims=True) * (0.5 / d)
            l_ref[...] = jnp.broadcast_to(tot, l_ref.shape)

    row = lambda i: (i, 0)
    return _call(
        body, name="loss_head", grid=(steps,),
        in_specs=[pl.BlockSpec((tm, d), row), pl.BlockSpec((tm, d), row)],
        out_specs=[pl.BlockSpec((tm, d), row), _full((8, LANE))],
        out_shape=[jax.ShapeDtypeStruct((t, d), F32), jax.ShapeDtypeStruct((8, LANE), F32)],
        scratch_shapes=[pltpu.VMEM((1, d), F32)],
    )(y, target)


def _shift_rows(x, s):
    if s == 0:
        return x
    t = x.shape[0]
    ri = _iota2(x.shape, 0)
    rolled = pltpu.roll(x, s % t, axis=0)
    if s > 0:
        return jnp.where(ri >= s, rolled, 0.0)
    return jnp.where(ri < t + s, rolled, 0.0)


def _conv_pre(x, w):
    acc = x * w[DN_CONV - 1:DN_CONV, :]
    for j in range(DN_CONV - 1):
        acc = acc + _shift_rows(x, DN_CONV - 1 - j) * w[j:j + 1, :]
    return acc


def dn_conv(proj, conv_w):
    t = proj.shape[0]
    width = 3 * DN_WIDTH

    def body(x_ref, w_ref, o_ref):
        o_ref[...] = _silu(_conv_pre(x_ref[...], w_ref[...]))

    col = lambda j: (0, j)
    return _call(
        body, name="dn_conv", grid=(width // LANE,),
        in_specs=[pl.BlockSpec((t, LANE), col), pl.BlockSpec((8, LANE), col)],
        out_specs=pl.BlockSpec((t, LANE), col),
        out_shape=jax.ShapeDtypeStruct((t, width), F32), semantics=("parallel",),
    )(proj, conv_w)


def dn_conv_bwd(proj, conv_w, dact, dproj):
    t = proj.shape[0]
    width = 3 * DN_WIDTH

    def body(x_ref, w_ref, d_ref, _, dx_ref, dw_ref):
        x = x_ref[...]
        w = w_ref[...]
        dc = d_ref[...] * _dsilu(_conv_pre(x, w))
        dx = dc * w[DN_CONV - 1:DN_CONV, :]
        rows = []
        for j in range(DN_CONV - 1):
            s = DN_CONV - 1 - j
            dx = dx + _shift_rows(dc, -s) * w[j:j + 1, :]
            rows.append(_colsum(dc * _shift_rows(x, s)))
        rows.append(_colsum(dc * x))
        dx_ref[...] = dx.astype(BF16)
        ri = _iota2((8, LANE), 0)
        dw = jnp.zeros((8, LANE), F32)
        for j in range(DN_CONV):
            dw = dw + jnp.where(ri == j, rows[j], 0.0)
        dw_ref[...] = dw

    col = lambda j: (0, j)
    return _call(
        body, name="dn_conv_bwd", grid=(width // LANE,),
        in_specs=[pl.BlockSpec((t, LANE), col), pl.BlockSpec((8, LANE), col), pl.BlockSpec((t, LANE), col), ANY],
        out_specs=[pl.BlockSpec((t, LANE), col), pl.BlockSpec((8, LANE), col)],
        out_shape=[jax.ShapeDtypeStruct(dproj.shape, dproj.dtype), jax.ShapeDtypeStruct((8, width), F32)],
        semantics=("parallel",), aliases={3: 0},
    )(proj, conv_w, dact, dproj)


def _t(x):
    return jnp.swapaxes(x, -1, -2)


def _inv_unit_lower(a):
    c = a.shape[-1]
    eye = (_iota2((c, c), 0) == _iota2((c, c), 1)).astype(F32)
    x = eye - a
    p = _hdot(a, a)
    steps = int(math.log2(c)) - 1
    for i in range(steps):
        x = x + _hdot(x, p)
        if i < steps - 1:
            p = _hdot(p, p)
    return x


def _dn_local(q, k, v, a, b, alog, dtb, tinv=None):
    nh, c, d = q.shape
    rq = lax.rsqrt(_rowsum(q * q) + NORM_EPS)
    rk = lax.rsqrt(_rowsum(k * k) + NORM_EPS)
    qh = q * rq
    kn = k * rk
    qs = qh * (d ** -0.5)
    g = -jnp.exp(alog) * _softplus(a + dtb)
    beta = _sigmoid(b)
    ri = _iota2((c, c), 0)
    ci = _iota2((c, c), 1)
    causal = ri >= ci
    strict = ri > ci
    gb = jnp.broadcast_to(g, (nh, c, d))
    gcb = _xdot(jnp.broadcast_to(causal.astype(F32), (nh, c, c)), gb, exact=0)
    gc = gcb[..., :1]
    gl = _colsum(gb)[..., :1]
    dec = jnp.exp(jnp.where(causal, gc - _t(gcb)[:, :c, :], -1e30))
    kb = kn * beta
    amat = jnp.where(strict, _bdot(kb, kn, 1, 1) * dec, 0.0)
    if tinv is None:
        tinv = _inv_unit_lower(amat)
    e = jnp.exp(gc)
    f = jnp.exp(gl - gc)
    rw = kb * e
    sol = _hdot(tinv, jnp.concatenate([v * beta, rw], axis=-1))
    pmat = jnp.where(causal, _bdot(qs, kn, 1, 1) * dec, 0.0)
    return dict(rq=rq, rk=rk, qh=qh, kn=kn, qs=qs, g=g, beta=beta, causal=causal, strict=strict, gl=gl,
                dec=dec, kb=kb, amat=amat, tinv=tinv, e=e, f=f, rw=rw, u=sol[..., :d], w=sol[..., d:], pmat=pmat,
                qd=qs * e, kd=kn * f)


_DN_FIELDS = ("u", "w", "qd", "kd", "pmat", "gl")


def _dn_state(m, s_in):
    vnew = m["u"] - _bdot(m["w"], s_in)
    o = _bdot(m["qd"], s_in) + _bdot(m["pmat"], vnew)
    return vnew, o, s_in * jnp.exp(m["gl"]) + _bdot(m["kd"], vnew, 0, 0)


def _dn_state_bwd(m, s_in, do, ds_out):
    el = jnp.exp(m["gl"])
    dvnew = _bdot(m["pmat"], do, 0, 0) + _bdot(m["kd"], ds_out)
    dkd = _bdot(m["vnew"], ds_out, 1, 1)
    ds_in = _bdot(m["qd"], do, 0, 0) + el * ds_out - _bdot(m["w"], dvnew, 0, 0)
    dgl = el * _colsum(_rowsum(s_in * ds_out))
    return dvnew, dkd, dgl, ds_in


def _dn_local_bwd(m, q, v, a, alog, dtb, s_in, vnew, do, dvnew, dkd, dgl):
    nh, c, d = q.shape
    kn, qs, kb, u, w, e, f = m["kn"], m["qs"], m["kb"], m["u"], m["w"], m["e"], m["f"]
    beta, dec, tinv, kd, qd = m["beta"], m["dec"], m["tinv"], m["kd"], m["qd"]
    dp = jnp.where(m["causal"], _bdot(do, vnew, 1, 1), 0.0)
    dqd = _bdot(do, s_in, 1, 1)
    dw = -_bdot(dvnew, s_in, 1, 1)
    dsol = _hdot(tinv, jnp.concatenate([dvnew, dw], axis=-1), 0, 0)
    dru = dsol[..., :d]
    drw = dsol[..., d:]
    da_m = -jnp.where(m["strict"], _bdot(dsol, jnp.concatenate([u, w], axis=-1), 1, 1), 0.0)
    db_m = da_m * dec
    dq_m = dp * dec
    dkb = _bdot(db_m, kn)
    dkn = _bdot(db_m, kb, 0, 0) + _bdot(dq_m, qs, 0, 0)
    dqs = _bdot(dq_m, kn)
    gmat = da_m * m["amat"] + dp * m["pmat"]
    ones = jnp.ones((nh, c, d), F32)
    dgam = (_xdot(gmat, ones) - _xdot(gmat, ones, 0, 0))[..., :1]
    dqs = dqs + dqd * e
    dgam = dgam + _rowsum(dqd * qd)
    dkn = dkn + dkd * f
    tk = _rowsum(dkd * kd)
    dgam = dgam - tk
    dgl = dgl + _colsum(tk)
    dkb = dkb + drw * e
    dgam = dgam + _rowsum(drw * m["rw"])
    dv = dru * beta
    dbeta = _rowsum(dru * v) + _rowsum(dkb * kn)
    dkn = dkn + dkb * beta
    last = (_iota2((c, 1), 0) == c - 1).astype(F32)
    dgam = dgam + last * dgl
    upper = (_iota2((c, c), 0) <= _iota2((c, c), 1)).astype(F32)
    dg = _xdot(jnp.broadcast_to(upper, (nh, c, c)), jnp.broadcast_to(dgam, (nh, c, d)), exact=0)[..., :1]
    dqh = dqs * (d ** -0.5)
    dq = m["rq"] * (dqh - m["qh"] * _rowsum(dqh * m["qh"]))
    dk = m["rk"] * (dkn - kn * _rowsum(dkn * kn))
    sg = _sigmoid(a + dtb)
    da = dg * (-jnp.exp(alog)) * sg
    dalog = _colsum(dg * m["g"])
    ddtb = _colsum(da)
    db = dbeta * beta * (1.0 - beta)
    return dq, dk, dv, da, db, dalog, ddtb


def _dn_gate(o, z, wn):
    ro = lax.rsqrt(jnp.mean(o * o, axis=-1, keepdims=True) + NORM_EPS)
    n = o * ro
    return n, ro, n * wn * _silu(z)


DN_PAIR = 8


def _heads(ref, col0):
    d = DN_HEAD_DIM
    return jnp.stack([ref[j * DN_CHUNK:(j + 1) * DN_CHUNK, col0 + h * d:col0 + (h + 1) * d]
                      for j in range(DN_PAIR) for h in range(DN_HEADS)])


def _dn_inputs(act_ref, ab_ref, sc_ref):
    ab = ab_ref[...]
    sc = sc_ref[...]
    rows = lambda j: slice(j * DN_CHUNK, (j + 1) * DN_CHUNK)
    both = [(j, h) for j in range(DN_PAIR) for h in range(DN_HEADS)]
    q = _heads(act_ref, 0)
    k = _heads(act_ref, DN_WIDTH)
    v = _heads(act_ref, 2 * DN_WIDTH)
    a = jnp.stack([ab[rows(j), h:h + 1] for j, h in both])
    b = jnp.stack([ab[rows(j), DN_HEADS + h:DN_HEADS + h + 1] for j, h in both])
    alog = jnp.stack([sc[0:1, h:h + 1] for _, h in both])
    dtb = jnp.stack([sc[1:2, h:h + 1] for _, h in both])
    return q, k, v, a, b, alog, dtb


def _chunk_of(m, j, fields):
    return {f: m[f][j * DN_HEADS:(j + 1) * DN_HEADS] for f in fields}


def dn_fwd(act, proj, scal, wn):
    t = act.shape[0]
    n = t // DN_CHUNK
    d = DN_HEAD_DIM
    rows = DN_PAIR * DN_CHUNK

    def body(act_ref, z_ref, ab_ref, sc_ref, wn_ref, y_ref, st_ref, ti_ref, s_ref):
        @pl.when(pl.program_id(0) == 0)
        def _():
            s_ref[...] = jnp.zeros_like(s_ref)

        m = _dn_local(*_dn_inputs(act_ref, ab_ref, sc_ref))
        s = s_ref[...]
        outs = []
        for j in range(DN_PAIR):
            st_ref[j] = s
            ti_ref[j] = m["tinv"][j * DN_HEADS:(j + 1) * DN_HEADS]
            _, o, s = _dn_state(_chunk_of(m, j, _DN_FIELDS), s)
            outs.append(o)
        s_ref[...] = s
        y = _dn_gate(jnp.concatenate(outs, axis=0), _heads(z_ref, 0), wn_ref[...])[2]
        for j in range(DN_PAIR):
            for h in range(DN_HEADS):
                y_ref[j * DN_CHUNK:(j + 1) * DN_CHUNK, h * d:(h + 1) * d] = y[j * DN_HEADS + h]

    return _call(
        body, name="dn_fwd", grid=(n // DN_PAIR,),
        in_specs=[pl.BlockSpec((rows, 3 * DN_WIDTH), lambda i: (i, 0)),
                  pl.BlockSpec((rows, DN_WIDTH), lambda i: (i, C_Z // DN_WIDTH)),
                  pl.BlockSpec((rows, LANE), lambda i: (i, C_AB // LANE)),
                  _full((8, LANE)), _full((1, d))],
        out_specs=[pl.BlockSpec((rows, DN_WIDTH), lambda i: (i, 0)),
                   pl.BlockSpec((DN_PAIR, DN_HEADS, d, d), lambda i: (i, 0, 0, 0)),
                   pl.BlockSpec((DN_PAIR, DN_HEADS, DN_CHUNK, DN_CHUNK), lambda i: (i, 0, 0, 0))],
        out_shape=[jax.ShapeDtypeStruct((t, MIX_WIDTH), F32), jax.ShapeDtypeStruct((n, DN_HEADS, d, d), F32),
                   jax.ShapeDtypeStruct((n, DN_HEADS, DN_CHUNK, DN_CHUNK), F32)],
        scratch_shapes=[pltpu.VMEM((DN_HEADS, d, d), F32)],
    )(act, proj, proj, scal, wn)


def dn_bwd(act, proj, scal, wn, states, tinvs, dy):
    t = act.shape[0]
    n = t // DN_CHUNK
    steps = n // DN_PAIR
    d = DN_HEAD_DIM
    zab = DN_WIDTH + AB_PAD
    rows = DN_PAIR * DN_CHUNK

    def body(act_ref, z_ref, ab_ref, sc_ref, wn_ref, st_ref, ti_ref, dy_ref, dact_ref, dzab_ref, dpar_ref, ds_ref):
        @pl.when(pl.program_id(0) == 0)
        def _():
            ds_ref[...] = jnp.zeros_like(ds_ref)
            dpar_ref[...] = jnp.zeros_like(dpar_ref)

        wnv = wn_ref[...]
        q, k, v, a, b, alog, dtb = _dn_inputs(act_ref, ab_ref, sc_ref)
        batch = (DN_PAIR * DN_HEADS,)
        s_in = st_ref[...].reshape(batch + (d, d))
        m = _dn_local(q, k, v, a, b, alog, dtb, ti_ref[...].reshape(batch + (DN_CHUNK, DN_CHUNK)))
        vnew, o, _ = _dn_state(m, s_in)
        z = _heads(z_ref, 0)
        dyh = _heads(dy_ref, 0)
        nrm, ro, _ = _dn_gate(o, z, wnv)
        sz = _silu(z)
        dz = dyh * nrm * wnv * _dsilu(z)
        dn = dyh * wnv * sz
        dwn = _colsum(dyh * nrm * sz)
        do = ro * (dn - nrm * jnp.mean(dn * nrm, axis=-1, keepdims=True))
        ds = ds_ref[...]
        parts = [None] * DN_PAIR
        for j in reversed(range(DN_PAIR)):
            mj = dict(_chunk_of(m, j, _DN_FIELDS), vnew=vnew[j * DN_HEADS:(j + 1) * DN_HEADS])
            dvnew, dkd, dgl, ds = _dn_state_bwd(mj, s_in[j * DN_HEADS:(j + 1) * DN_HEADS],
                                                do[j * DN_HEADS:(j + 1) * DN_HEADS], ds)
            parts[j] = (dvnew, dkd, dgl)
        ds_ref[...] = ds
        dvnew, dkd, dgl = (jnp.concatenate([p[i] for p in parts], axis=0) for i in range(3))
        dq, dk, dv, da, db, dalog, ddtb = _dn_local_bwd(m, q, v, a, alog, dtb, s_in, vnew, do, dvnew, dkd, dgl)
        lane = _iota2((DN_CHUNK, LANE), 1)
        prow = _iota2((8, LANE), 0)
        plane = _iota2((8, LANE), 1)
        dpar = jnp.zeros((8, LANE), F32)
        for j in range(DN_PAIR):
            rs = slice(j * DN_CHUNK, (j + 1) * DN_CHUNK)
            dab = jnp.zeros((DN_CHUNK, LANE), F32)
            for h in range(DN_HEADS):
                n_ = j * DN_HEADS + h
                dzab_ref[rs, h * d:(h + 1) * d] = dz[n_].astype(BF16)
                dact_ref[rs, h * d:(h + 1) * d] = dq[n_]
                dact_ref[rs, DN_WIDTH + h * d:DN_WIDTH + (h + 1) * d] = dk[n_]
                dact_ref[rs, 2 * DN_WIDTH + h * d:2 * DN_WIDTH + (h + 1) * d] = dv[n_]
                dab = dab + jnp.where(lane == h, da[n_], 0.0) + jnp.where(lane == DN_HEADS + h, db[n_], 0.0)
                dpar = dpar + jnp.where((prow == 0) & (plane == h), dalog[n_], 0.0)
                dpar = dpar + jnp.where((prow == 1) & (plane == h), ddtb[n_], 0.0)
                dpar = dpar + jnp.where(prow == 2, dwn[n_], 0.0)
            dzab_ref[rs, DN_WIDTH:DN_WIDTH + LANE] = dab.astype(BF16)
            dzab_ref[rs, DN_WIDTH + LANE:] = jnp.zeros((DN_CHUNK, AB_PAD - LANE), BF16)
        dpar_ref[...] += dpar

    rev = lambda i: (steps - 1 - i, 0)
    rev4 = lambda i: (steps - 1 - i, 0, 0, 0)
    return _call(
        body, name="dn_bwd", grid=(steps,),
        in_specs=[pl.BlockSpec((rows, 3 * DN_WIDTH), rev),
                  pl.BlockSpec((rows, DN_WIDTH), lambda i: (steps - 1 - i, C_Z // DN_WIDTH)),
                  pl.BlockSpec((rows, LANE), lambda i: (steps - 1 - i, C_AB // LANE)),
                  _full((8, LANE)), _full((1, d)),
                  pl.BlockSpec((DN_PAIR, DN_HEADS, d, d), rev4),
                  pl.BlockSpec((DN_PAIR, DN_HEADS, DN_CHUNK, DN_CHUNK), rev4),
                  pl.BlockSpec((rows, DN_WIDTH), rev)],
        out_specs=[pl.BlockSpec((rows, 3 * DN_WIDTH), rev),
                   pl.BlockSpec((rows, zab), lambda i: (steps - 1 - i, C_Z // zab)), _full((8, LANE))],
        out_shape=[jax.ShapeDtypeStruct((t, 3 * DN_WIDTH), F32), jax.ShapeDtypeStruct((t, IN_PAD), BF16),
                   jax.ShapeDtypeStruct((8, LANE), F32)],
        scratch_shapes=[pltpu.VMEM((DN_HEADS, d, d), F32)],
    )(act, proj, proj, scal, wn, states, tinvs, dy)


_INV_SQRT2 = 0.7071067811865476
_INV_SQRT2PI = 0.3989422804014327


def _gelu(x):
    return 0.5 * x * (1.0 + lax.erf(x * _INV_SQRT2))


def _dgelu(x):
    return 0.5 * (1.0 + lax.erf(x * _INV_SQRT2)) + x * jnp.exp(-0.5 * x * x) * _INV_SQRT2PI


def _gm_core(uv, lng, lnb, ws_ref, bst):
    c = uv.shape[0]
    zz = _gelu(uv)
    u = zz[:, :GM_WIDTH]
    vv = zz[:, GM_WIDTH:]
    xc = vv - jnp.mean(vv, axis=-1, keepdims=True)
    rs = lax.rsqrt(jnp.mean(xc * xc, axis=-1, keepdims=True) + NORM_EPS)
    xh = xc * rs
    vn = xh * lng + lnb
    grp = _iota2((c, GM_WIDTH), 1) // GM_GROUP_DIM
    tril = _iota2((c, c), 0) >= _iota2((c, c), 1)
    sv = jnp.zeros((c, GM_WIDTH), F32)
    masks = []
    for g in range(GM_GROUPS):
        mk = grp == g
        masks.append(mk)
        ws = jnp.where(tril, ws_ref[g], 0.0)
        sv = sv + _bdot(ws, jnp.where(mk, vn, 0.0)) + jnp.where(mk, bst[:, g:g + 1], 0.0)
    return u, xh, rs, vn, sv, masks, tril


def gm_fwd(proj, lng, lnb, w_s, bst, ybuf):
    t = proj.shape[0]

    def body(uv_ref, g_ref, b_ref, ws_ref, bst_ref, _, y_ref):
        u, _, _, _, sv, _, _ = _gm_core(uv_ref[...], g_ref[...], b_ref[...], ws_ref, bst_ref[...])
        y_ref[...] = u * sv

    return _call(
        body, name="gm_fwd", grid=(t // GM_CHUNK,),
        in_specs=[pl.BlockSpec((GM_CHUNK, 2 * GM_WIDTH), lambda i: (i, C_UV // (2 * GM_WIDTH))),
                  _full((1, GM_WIDTH)), _full((1, GM_WIDTH)), _full((GM_GROUPS, GM_CHUNK, GM_CHUNK)),
                  _full((GM_CHUNK, LANE)), ANY],
        out_specs=pl.BlockSpec((GM_CHUNK, GM_WIDTH), lambda i: (i, DN_WIDTH // GM_WIDTH)),
        out_shape=jax.ShapeDtypeStruct(ybuf.shape, F32), semantics=("parallel",), aliases={5: 0},
    )(proj, lng, lnb, w_s, bst, ybuf)


def gm_bwd(proj, lng, lnb, w_s, bst, dy, dproj):
    t = proj.shape[0]

    def body(uv_ref, g_ref, b_ref, ws_ref, bst_ref, dy_ref, _, duv_ref, dws_ref, dbst_ref, dln_ref):
        @pl.when(pl.program_id(0) == 0)
        def _():
            dws_ref[...] = jnp.zeros_like(dws_ref)
            dbst_ref[...] = jnp.zeros_like(dbst_ref)
            dln_ref[...] = jnp.zeros_like(dln_ref)

        uv = uv_ref[...]
        lng = g_ref[...]
        u, xh, rs, vn, sv, masks, tril = _gm_core(uv, lng, b_ref[...], ws_ref, bst_ref[...])
        dyv = dy_ref[...]
        dsv = dyv * u
        lane = _iota2((GM_CHUNK, LANE), 1)
        dvn = jnp.zeros_like(dsv)
        dbst = jnp.zeros((GM_CHUNK, LANE), F32)
        for g in range(GM_GROUPS):
            ws = jnp.where(tril, ws_ref[g], 0.0)
            dsg = jnp.where(masks[g], dsv, 0.0)
            dvn = dvn + jnp.where(masks[g], _bdot(ws, dsv, 0, 0), 0.0)
            dws_ref[g] += jnp.where(tril, _bdot(dsg, vn, 1, 1), 0.0)
            dbst = dbst + jnp.where(lane == g, _rowsum(dsg), 0.0)
        dbst_ref[...] += dbst
        row = _iota2((8, GM_WIDTH), 0)
        dln_ref[...] += jnp.where(row == 0, _colsum(dvn * xh), 0.0) + jnp.where(row == 1, _colsum(dvn), 0.0)
        dxh = dvn * lng
        dvv = rs * (dxh - jnp.mean(dxh, axis=-1, keepdims=True) - xh * jnp.mean(dxh * xh, axis=-1, keepdims=True))
        dg = _dgelu(uv)
        duv_ref[:, :GM_WIDTH] = (dyv * sv * dg[:, :GM_WIDTH]).astype(BF16)
        duv_ref[:, GM_WIDTH:] = (dvv * dg[:, GM_WIDTH:]).astype(BF16)

    return _call(
        body, name="gm_bwd", grid=(t // GM_CHUNK,),
        in_specs=[pl.BlockSpec((GM_CHUNK, 2 * GM_WIDTH), lambda i: (i, C_UV // (2 * GM_WIDTH))),
                  _full((1, GM_WIDTH)), _full((1, GM_WIDTH)), _full((GM_GROUPS, GM_CHUNK, GM_CHUNK)),
                  _full((GM_CHUNK, LANE)),
                  pl.BlockSpec((GM_CHUNK, GM_WIDTH), lambda i: (i, DN_WIDTH // GM_WIDTH)), ANY],
        out_specs=[pl.BlockSpec((GM_CHUNK, 2 * GM_WIDTH), lambda i: (i, C_UV // (2 * GM_WIDTH))),
                   _full((GM_GROUPS, GM_CHUNK, GM_CHUNK)), _full((GM_CHUNK, LANE)), _full((8, GM_WIDTH))],
        out_shape=[jax.ShapeDtypeStruct(dproj.shape, dproj.dtype),
                   jax.ShapeDtypeStruct((GM_GROUPS, GM_CHUNK, GM_CHUNK), F32),
                   jax.ShapeDtypeStruct((GM_CHUNK, LANE), F32), jax.ShapeDtypeStruct((8, GM_WIDTH), F32)],
        aliases={6: 0},
    )(proj, lng, lnb, w_s, bst, dy, dproj)


def _head_mats():
    r = _iota2((SW_WIDTH, SW_WIDTH), 0)
    c = _iota2((SW_WIDTH, SW_WIDTH), 1)
    same = (r // SW_HEAD_DIM) == (c // SW_HEAD_DIM)
    cc = c % SW_HEAD_DIM
    half = ROPE_DIM // 2
    rot = jnp.where((cc < half) & (r == c + half), -1.0, 0.0) + jnp.where((cc >= half) & (cc < ROPE_DIM) & (r == c - half), 1.0, 0.0)
    return same.astype(F32), rot


def _seg_col(s):
    return C_SW // SW_WIDTH + (s // 2) * 3 + s % 2


def _halves(x):
    return x[:, :LANE], x[:, LANE:]


def sw_prep(proj, nw2, cos_t, sin_t, *, tm=512):
    t = proj.shape[0]

    def body(x_ref, w_ref, c_ref, s_ref, o_ref):
        same, rot = _head_mats()
        x = x_ref[...]
        r = lax.rsqrt(_xdot(x * x, same) * (1.0 / SW_HEAD_DIM) + NORM_EPS)
        xn = x * r * w_ref[0]
        o_ref[0, 0], o_ref[0, 1] = _halves(xn * c_ref[...] + _xdot(xn, rot) * s_ref[...])

    return _call(
        body, name="sw_prep", grid=(6, t // tm),
        in_specs=[pl.BlockSpec((tm, SW_WIDTH), lambda s, i: (i, _seg_col(s))),
                  pl.BlockSpec((1, 1, SW_WIDTH), lambda s, i: (s % 2, 0, 0)),
                  pl.BlockSpec((tm, SW_WIDTH), lambda s, i: (i, 0)),
                  pl.BlockSpec((tm, SW_WIDTH), lambda s, i: (i, 0))],
        out_specs=pl.BlockSpec((1, 2, tm, LANE), lambda s, i: (s, 0, i, 0)),
        out_shape=jax.ShapeDtypeStruct((6, 2, t, LANE), F32), semantics=("parallel", "parallel"),
    )(proj, nw2, cos_t, sin_t)


def sw_prep_bwd(proj, nw2, cos_t, sin_t, dkvq, dproj, dnw, p, *, tm=512):
    t = proj.shape[0]
    col0 = C_SW // SW_WIDTH + 3 * p
    seg_col = lambda s: col0 + (s + 1) % 3

    def body(x_ref, w_ref, c_ref, s_ref, d_ref, _, dw0_ref, dx_ref, dw_ref):
        s = pl.program_id(0)
        dout = jnp.concatenate([d_ref[0, 0], d_ref[0, 1]], axis=1)

        @pl.when(s == 1)
        def _():
            dx_ref[...] = dout.astype(BF16)

        @pl.when((s != 1) & (pl.program_id(1) == 0))
        def _():
            dw_ref[...] = dw0_ref[...]

        @pl.when(s != 1)
        def _():
            same, rot = _head_mats()
            x = x_ref[...]
            w = w_ref[0]
            r = lax.rsqrt(_xdot(x * x, same) * (1.0 / SW_HEAD_DIM) + NORM_EPS)
            xh = x * r
            dxn = dout * c_ref[...] + _xdot(dout * s_ref[...], rot, 1, 1)
            dw_ref[0] += _colsum(dxn * xh)
            dxh = dxn * w
            dx_ref[...] = (r * (dxh - xh * (_xdot(dxh * xh, same) * (1.0 / SW_HEAD_DIM)))).astype(BF16)

    return _call(
        body, name=f"sw_prep_bwd{p}", grid=(3, t // tm),
        in_specs=[pl.BlockSpec((tm, SW_WIDTH), lambda s, i: (i, seg_col(s))),
                  pl.BlockSpec((1, 1, SW_WIDTH), lambda s, i: (1 - s // 2, 0, 0)),
                  pl.BlockSpec((tm, SW_WIDTH), lambda s, i: (i, 0)),
                  pl.BlockSpec((tm, SW_WIDTH), lambda s, i: (i, 0)),
                  pl.BlockSpec((1, 2, tm, LANE), lambda s, i: (s, 0, i, 0)), ANY,
                  pl.BlockSpec((1, 1, SW_WIDTH), lambda s, i: (s // 2, 0, 0))],
        out_specs=[pl.BlockSpec((tm, SW_WIDTH), lambda s, i: (i, seg_col(s))),
                   pl.BlockSpec((1, 1, SW_WIDTH), lambda s, i: (s // 2, 0, 0))],
        out_shape=[jax.ShapeDtypeStruct(dproj.shape, dproj.dtype), jax.ShapeDtypeStruct((2, 1, SW_WIDTH), F32)],
        semantics=("arbitrary", "arbitrary"), aliases={5: 0},
    )(proj, nw2, cos_t, sin_t, dkvq, dproj, dnw)


_SW_SCALE = SW_HEAD_DIM ** -0.5
_NEG = -1e30


def _sw_masks(has_other):
    ri = _iota2((SW_BLOCK, SW_BLOCK), 0)
    ci = _iota2((SW_BLOCK, SW_BLOCK), 1)
    return ri >= ci, (ci >= ri) & has_other


def _pair(x):
    first = _iota2((1, LANE), 1) < SW_HEAD_DIM
    return jnp.stack([jnp.where(first, x, 0.0), jnp.where(first, 0.0, x)])


def _both(x):
    return jnp.broadcast_to(x.astype(BF16)[None], (2,) + x.shape)


def _unpair(x2):
    first = _iota2((1, LANE), 1) < SW_HEAD_DIM
    return jnp.where(first, x2[0], x2[1])


def _head_cols(x):
    return jnp.stack([x[:, 0:1], x[:, SW_HEAD_DIM:SW_HEAD_DIM + 1]])


SW_GROUP = 8


def _sw_geometry(t, p):
    dil = SW_DILATIONS[p]
    unit = SW_BLOCK * dil
    nb = max(1, SW_GROUP // dil)
    return dil, unit, nb, t // (unit * nb)


def _sw_groups(dil, nb, body):
    if nb * dil == SW_GROUP:
        body([(k // dil, k % dil) for k in range(SW_GROUP)])
    else:
        for g in range(nb * dil // SW_GROUP):
            body([(0, SW_GROUP * g + k) for k in range(SW_GROUP)])


def _sw_rows(i, r, dil):
    start = i * SW_BLOCK * dil + r
    return pl.ds(start, SW_BLOCK) if dil == 1 else pl.ds(start, SW_BLOCK, stride=dil)


def _sw_load(refs, probs, dil, shift, wrap, fn):
    out = []
    for i, r in probs:
        if shift != 0 and i == wrap:
            out.append(fn(refs[1][_sw_rows(0, r, dil), :]))
        else:
            out.append(fn(refs[0][_sw_rows(i + shift, r, dil), :]))
    return jnp.concatenate(out, axis=0)


def _sw_other_masks(probs, wrap, edge_ok):
    _, other = _sw_masks(edge_ok)
    _, always = _sw_masks(True)
    return jnp.stack([other if i == wrap else always for i, _ in probs for _ in range(2)])


def sw_attn(qk, proj, p):
    t = proj.shape[0]
    dil, unit, nb, nsp = _sw_geometry(t, p)
    vcol = (C_SW + 3 * SW_WIDTH * p + 2 * SW_WIDTH) // LANE

    def body(q_ref, kc_ref, kp_ref, vc_ref, vp_ref, o_ref, l_ref):
        mc, _ = _sw_masks(True)
        first = pl.program_id(1) != 0
        q_r, k_r, v_r = (q_ref.at[0, 0], None), (kc_ref.at[0, 0], kp_ref.at[0, 0]), (vc_ref, vp_ref)

        def one(probs):
            mp = _sw_other_masks(probs, 0, first)
            q2 = _sw_load(q_r, probs, dil, 0, 0, _pair)
            sc = jnp.where(mc, _bdot(q2, _sw_load(k_r, probs, dil, 0, 0, _both), 1, 1) * _SW_SCALE, _NEG)
            sp = jnp.where(mp, _bdot(q2, _sw_load(k_r, probs, dil, -1, 0, _both), 1, 1) * _SW_SCALE, _NEG)
            mx = jnp.maximum(jnp.max(sc, axis=-1, keepdims=True), jnp.max(sp, axis=-1, keepdims=True))
            pc = jnp.exp(sc - mx)
            pp = jnp.exp(sp - mx)
            den = _rowsum(pc) + _rowsum(pp)
            o2 = (_bdot(pc, _sw_load(v_r, probs, dil, 0, 0, _both))
                  + _bdot(pp, _sw_load(v_r, probs, dil, -1, 0, _both))) * (1.0 / den)
            l2 = jnp.broadcast_to(mx + jnp.log(den), o2.shape)
            for n, (i, r) in enumerate(probs):
                o_ref.at[0][_sw_rows(i, r, dil), :] = _unpair(o2[2 * n:2 * n + 2])
                l_ref.at[0][_sw_rows(i, r, dil), :] = _unpair(l2[2 * n:2 * n + 2])

        _sw_groups(dil, nb, one)

    before = lambda j: jnp.maximum(j * nb - 1, 0)
    seg = lambda s: pl.BlockSpec((1, 1, unit * nb, LANE), lambda h, j: (s, h, j, 0))
    seg_b = lambda s: pl.BlockSpec((1, 1, unit, LANE), lambda h, j: (s, h, before(j), 0))
    out = pl.BlockSpec((1, unit * nb, LANE), lambda h, j: (h, j, 0))
    shp = jax.ShapeDtypeStruct((2, t, LANE), F32)
    return _call(
        body, name=f"sw_attn{p}", grid=(2, nsp),
        in_specs=[seg(2 * p), seg(2 * p + 1), seg_b(2 * p + 1),
                  pl.BlockSpec((unit * nb, LANE), lambda h, j: (j, vcol + h)),
                  pl.BlockSpec((unit, LANE), lambda h, j: (before(j), vcol + h))],
        out_specs=[out, out], out_shape=[shp, shp], semantics=("parallel", "parallel"),
    )(qk, qk, qk, proj, proj)


def sw_attn_dkv(qk, proj, dy, lg, dm, p):
    t = proj.shape[0]
    dil, unit, nb, nsp = _sw_geometry(t, p)
    nunits = t // unit
    vcol = (C_SW + 3 * SW_WIDTH * p + 2 * SW_WIDTH) // LANE
    ycol = (DN_WIDTH + GM_WIDTH) // LANE

    def body(k_ref, v_ref, qc_ref, qn_ref, doc_ref, don_ref, lc_ref, ln_ref, dc_ref, dn_ref, o_ref):
        mc, _ = _sw_masks(True)
        more = pl.program_id(1) + 1 < nsp
        q_r, do_r = (qc_ref.at[0, 0], qn_ref.at[0, 0]), (doc_ref, don_ref)
        l_r, d_r = (lc_ref.at[0], ln_ref.at[0]), (dc_ref.at[0], dn_ref.at[0])

        def one(probs):
            k2 = _sw_load((k_ref.at[0, 0], None), probs, dil, 0, 0, _both)
            v2 = _sw_load((v_ref, None), probs, dil, 0, 0, _both)
            dk = jnp.zeros((2 * SW_GROUP, SW_BLOCK, LANE), F32)
            dv = jnp.zeros((2 * SW_GROUP, SW_BLOCK, LANE), F32)
            for shift, mk in ((0, mc), (1, _sw_other_masks(probs, nb - 1, more))):
                q2 = _sw_load(q_r, probs, dil, shift, nb - 1, _pair)
                do2 = _sw_load(do_r, probs, dil, shift, nb - 1, _pair)
                lse = _sw_load(l_r, probs, dil, shift, nb - 1, _head_cols)
                dd = _sw_load(d_r, probs, dil, shift, nb - 1, _head_cols)
                pr = jnp.exp(jnp.where(mk, _bdot(q2, k2, 1, 1) * _SW_SCALE, _NEG) - lse)
                dv = dv + _bdot(pr, do2, 0, 0)
                ds = pr * (_bdot(do2, v2, 1, 1) - dd)
                dk = dk + _bdot(ds, q2, 0, 0)
            for n, (i, r) in enumerate(probs):
                o_ref.at[0, 0][_sw_rows(i, r, dil), :] = (dk[2 * n] + dk[2 * n + 1]) * _SW_SCALE
                o_ref.at[1, 0][_sw_rows(i, r, dil), :] = dv[2 * n] + dv[2 * n + 1]

        _sw_groups(dil, nb, one)

    after = lambda j: jnp.minimum((j + 1) * nb, nunits - 1)
    seg = lambda s: pl.BlockSpec((1, 1, unit * nb, LANE), lambda h, j: (s, h, j, 0))
    seg_a = lambda s: pl.BlockSpec((1, 1, unit, LANE), lambda h, j: (s, h, after(j), 0))
    col = lambda c0: pl.BlockSpec((unit * nb, LANE), lambda h, j: (j, c0 + h))
    col_a = lambda c0: pl.BlockSpec((unit, LANE), lambda h, j: (after(j), c0 + h))
    hp = pl.BlockSpec((1, unit * nb, LANE), lambda h, j: (h, j, 0))
    hp_a = pl.BlockSpec((1, unit, LANE), lambda h, j: (h, after(j), 0))
    return _call(
        body, name=f"sw_dkv{p}", grid=(2, nsp),
        in_specs=[seg(2 * p + 1), col(vcol), seg(2 * p), seg_a(2 * p), col(ycol), col_a(ycol), hp, hp_a, hp, hp_a],
        out_specs=pl.BlockSpec((2, 1, unit * nb, LANE), lambda h, j: (0, h, j, 0)),
        out_shape=jax.ShapeDtypeStruct((3, 2, t, LANE), F32), semantics=("parallel", "parallel"),
    )(qk, proj, qk, qk, dy, dy, lg, lg, dm, dm)


def sw_attn_dq(qk, proj, dy, lg, dm, dkvq, p):
    t = proj.shape[0]
    dil, unit, nb, nsp = _sw_geometry(t, p)
    vcol = (C_SW + 3 * SW_WIDTH * p + 2 * SW_WIDTH) // LANE
    ycol = (DN_WIDTH + GM_WIDTH) // LANE

    def body(q_ref, kc_ref, kp_ref, vc_ref, vp_ref, do_ref, l_ref, d_ref, _, dq_ref):
        mc, _ = _sw_masks(True)
        first = pl.program_id(1) != 0
        k_r, v_r = (kc_ref.at[0, 0], kp_ref.at[0, 0]), (vc_ref, vp_ref)

        def one(probs):
            mp = _sw_other_masks(probs, 0, first)
            q2 = _sw_load((q_ref.at[0, 0], None), probs, dil, 0, 0, _pair)
            do2 = _sw_load((do_ref, None), probs, dil, 0, 0, _pair)
            lse = _sw_load((l_ref.at[0], None), probs, dil, 0, 0, _head_cols)
            dd = _sw_load((d_ref.at[0], None), probs, dil, 0, 0, _head_cols)
            kc = _sw_load(k_r, probs, dil, 0, 0, _both)
            kp = _sw_load(k_r, probs, dil, -1, 0, _both)
            pc = jnp.exp(jnp.where(mc, _bdot(q2, kc, 1, 1) * _SW_SCALE, _NEG) - lse)
            pp = jnp.exp(jnp.where(mp, _bdot(q2, kp, 1, 1) * _SW_SCALE, _NEG) - lse)
            dsc = pc * (_bdot(do2, _sw_load(v_r, probs, dil, 0, 0, _both), 1, 1) - dd)
            dsp = pp * (_bdot(do2, _sw_load(v_r, probs, dil, -1, 0, _both), 1, 1) - dd)
            dq2 = (_bdot(dsc, kc) + _bdot(dsp, kp)) * _SW_SCALE
            for n, (i, r) in enumerate(probs):
                dq_ref.at[0, 0][_sw_rows(i, r, dil), :] = _unpair(dq2[2 * n:2 * n + 2])

        _sw_groups(dil, nb, one)

    before = lambda j: jnp.maximum(j * nb - 1, 0)
    seg = lambda s: pl.BlockSpec((1, 1, unit * nb, LANE), lambda h, j: (s, h, j, 0))
    seg_b = lambda s: pl.BlockSpec((1, 1, unit, LANE), lambda h, j: (s, h, before(j), 0))
    col = lambda c0: pl.BlockSpec((unit * nb, LANE), lambda h, j: (j, c0 + h))
    col_b = lambda c0: pl.BlockSpec((unit, LANE), lambda h, j: (before(j), c0 + h))
    hp = pl.BlockSpec((1, unit * nb, LANE), lambda h, j: (h, j, 0))
    return _call(
        body, name=f"sw_dq{p}", grid=(2, nsp),
        in_specs=[seg(2 * p), seg(2 * p + 1), seg_b(2 * p + 1), col(vcol), col_b(vcol), col(ycol), hp, hp, ANY],
        out_specs=pl.BlockSpec((1, 1, unit * nb, LANE), lambda h, j: (2, h, j, 0)),
        out_shape=jax.ShapeDtypeStruct(dkvq.shape, F32), semantics=("parallel", "parallel"), aliases={8: 0},
    )(qk, qk, qk, proj, proj, dy, lg, dm, dkvq)


def sw_merge(outs, lses, ybuf, *, tm=512):
    t = ybuf.shape[0]

    def body(o0, o1, o2, l0_ref, l1_ref, l2_ref, _, y_ref, lg_ref):
        l0, l1, l2 = l0_ref[...], l1_ref[...], l2_ref[...]
        mx = jnp.maximum(jnp.maximum(l0, l1), l2)
        lg = mx + jnp.log(jnp.exp(l0 - mx) + jnp.exp(l1 - mx) + jnp.exp(l2 - mx))
        lg_ref[...] = lg
        y = jnp.exp(l0 - lg) * o0[...] + jnp.exp(l1 - lg) * o1[...] + jnp.exp(l2 - lg) * o2[...]
        y_ref[...] = jnp.concatenate([y[0], y[1]], axis=1)

    hp = pl.BlockSpec((2, tm, LANE), lambda i: (0, i, 0))
    return _call(
        body, name="sw_merge", grid=(t // tm,), in_specs=[hp] * 6 + [ANY],
        out_specs=[pl.BlockSpec((tm, SW_WIDTH), lambda i: (i, (DN_WIDTH + GM_WIDTH) // SW_WIDTH)), hp],
        out_shape=[jax.ShapeDtypeStruct(ybuf.shape, F32), jax.ShapeDtypeStruct((2, t, LANE), F32)],
        semantics=("parallel",), aliases={6: 0},
    )(*outs, *lses, ybuf)


def sw_delta(dy, ybuf, *, tm=512):
    t = ybuf.shape[0]

    def body(dy_ref, y_ref, o_ref):
        same, _ = _head_mats()
        o_ref[0], o_ref[1] = _halves(_xdot(dy_ref[...] * y_ref[...], same))

    b1 = pl.BlockSpec((tm, SW_WIDTH), lambda i: (i, (DN_WIDTH + GM_WIDTH) // SW_WIDTH))
    return _call(body, name="sw_delta", grid=(t // tm,), in_specs=[b1, b1],
                 out_specs=pl.BlockSpec((2, tm, LANE), lambda i: (0, i, 0)),
                 out_shape=jax.ShapeDtypeStruct((2, t, LANE), F32), semantics=("parallel",))(dy, ybuf)


def _rope_tables(t):
    inv = ROPE_THETA ** (-jnp.arange(0, ROPE_DIM, 2, dtype=F32) / ROPE_DIM)
    ang = jnp.arange(t, dtype=F32)[:, None] * inv[None, :]
    pad1 = jnp.ones((t, SW_HEAD_DIM - ROPE_DIM), F32)
    pad0 = jnp.zeros((t, SW_HEAD_DIM - ROPE_DIM), F32)
    cos_h = jnp.concatenate([jnp.cos(ang), jnp.cos(ang), pad1], axis=1)
    sin_h = jnp.concatenate([jnp.sin(ang), jnp.sin(ang), pad0], axis=1)
    return jnp.tile(cos_h, (1, SW_HEADS)), jnp.tile(sin_h, (1, SW_HEADS))


def sw_forward(proj, nw2, cos_t, sin_t, ybuf):
    qk = sw_prep(proj, nw2, cos_t, sin_t)
    outs, lses = [], []
    for p in range(len(SW_DILATIONS)):
        o, lse = sw_attn(qk, proj, p)
        outs.append(o)
        lses.append(lse)
    ybuf, lg = sw_merge(outs, lses, ybuf)
    return ybuf, (qk, lg)


def sw_backward(proj, nw2, cos_t, sin_t, res, ybuf, dy, dproj):
    qk, lg = res
    dm = sw_delta(dy, ybuf)
    dnw = jnp.zeros((2, 1, SW_WIDTH), F32)
    for p in range(len(SW_DILATIONS)):
        dkvq = sw_attn_dkv(qk, proj, dy, lg, dm, p)
        dkvq = sw_attn_dq(qk, proj, dy, lg, dm, dkvq, p)
        dproj, dnw = sw_prep_bwd(proj, nw2, cos_t, sin_t, dkvq, dproj, dnw, p)
    return dproj, dnw[::-1, 0]


def _pad_rows(a, rows):
    return jnp.zeros((rows,) + a.shape[1:], a.dtype).at[:a.shape[0]].set(a)


def _consts(sp):
    d = {}
    d["mix_nw"] = sp["mix_norm_w"][:, None, :]
    d["ffn_nw"] = sp["ffn_norm_w"][:, None, :]
    d["cw8"] = jnp.pad(sp["dn_conv_w"], ((0, 0), (0, 8 - DN_CONV), (0, 0)))
    d["scal"] = jnp.pad(jnp.stack([sp["dn_a_log"], sp["dn_dt_bias"]], axis=1), ((0, 0), (0, 6), (0, LANE - DN_HEADS)))
    d["wn"] = sp["dn_out_norm_w"][:, None, :]
    d["lng"] = sp["gm_ln_g"][:, None, :]
    d["lnb"] = sp["gm_ln_b"][:, None, :]
    d["w_s"] = sp["gm_w_s"]
    d["bst"] = jnp.pad(jnp.swapaxes(sp["gm_b_s"], 1, 2), ((0, 0), (0, 0), (0, LANE - GM_GROUPS)))
    d["nw2"] = jnp.stack([jnp.tile(sp["sw_q_norm_w"], (1, SW_HEADS)),
                          jnp.tile(sp["sw_k_norm_w"], (1, SW_HEADS))], axis=1)[:, :, None, :]
    return d


def _layer_fwd(x, mod, get_w, cs, tabs):
    wb = dict(get_w("w_in", x))
    h1, proj = norm_mm(x, cs["mix_nw"], mod[1], mod[0], wb["w_in"], swiglu=False, name="in_proj")
    act = dn_conv(proj, cs["cw8"])
    y, states, tinvs = dn_fwd(act, proj, cs["scal"], cs["wn"])
    y = gm_fwd(proj, cs["lng"], cs["lnb"], cs["w_s"], cs["bst"], y)
    y, swres = sw_forward(proj, cs["nw2"], *tabs, y)
    wb.update(get_w("w_out", y))
    x1, o1 = resid_mm(y, wb["w_out"], x, mod[2], name="out_proj")
    wb.update(get_w("ffn", x1))
    h2, gu, actf = norm_mm(x1, cs["ffn_nw"], mod[4], mod[3], wb["w_ffn_in"], swiglu=True, name="ffn_in")
    x2, o2 = resid_mm(actf, wb["w_ffn_out"], x1, mod[5], name="ffn_out")
    res = dict(x=x, h1=h1, proj=proj, act=act, states=states, tinvs=tinvs, swres=swres, y=y, x1=x1, o1=o1, h2=h2, gu=gu,
               actf=actf, o2=o2)
    return x2, res, wb


def _layer_bwd(dx2, res, mod, wb, cs, tabs, grads_done):
    dgu, gx2, dgate2 = resid_mm_bwd(dx2, mod[5], res["o2"], wb["w_ffn_out"], res["gu"], name="ffn_out_bwd", tm=512)
    g_wfo = mm_tn(res["actf"], gx2, name="wg_ffn_out")
    g_wfi = mm_tn(res["h2"], dgu, name="wg_ffn_in")
    token = grads_done("ffn", dict(w_ffn_in=g_wfi, w_ffn_out=g_wfo))
    dx1, d_ffn_nw, dscale2, dshift2 = norm_mm_bwd(dgu, wb["w_ffn_in"], res["x1"], cs["ffn_nw"], mod[4] + token, dx2,
                                                  name="ffn_in_bwd")
    dy, gx1, dgate1 = resid_mm_bwd(dx1, mod[2], res["o1"], wb["w_out"], None, name="out_proj_bwd", tm=512)
    g_wout = mm_tn(res["y"], gx1, name="wg_out")
    proj = res["proj"]
    dact, dproj, dpar = dn_bwd(res["act"], proj, cs["scal"], cs["wn"], res["states"], res["tinvs"], dy)
    dproj, dcw = dn_conv_bwd(proj, cs["cw8"], dact, dproj)
    dproj, dws, dbst, dln = gm_bwd(proj, cs["lng"], cs["lnb"], cs["w_s"], cs["bst"], dy, dproj)
    dproj, dnw = sw_backward(proj, cs["nw2"], *tabs, res["swres"], res["y"], dy, dproj)
    g_win = mm_tn(res["h1"], dproj, name="wg_in")
    dx, d_mix_nw, dscale1, dshift1 = norm_mm_bwd(dproj, wb["w_in"], res["x"], cs["mix_nw"], mod[1], dx1,
                                                 name="in_proj_bwd")
    dmod = jnp.concatenate([dshift1, dscale1, dgate1, dshift2, dscale2, dgate2], axis=1)
    dnw = dnw.reshape(2, SW_HEADS, SW_HEAD_DIM).sum(1)
    small = dict(mix_norm_w=d_mix_nw[0], ffn_norm_w=d_ffn_nw[0], dn_conv_w=dcw[:DN_CONV],
                 dn_a_log=dpar[0, :DN_HEADS], dn_dt_bias=dpar[1, :DN_HEADS], dn_out_norm_w=dpar[2],
                 gm_ln_g=dln[0], gm_ln_b=dln[1], gm_w_s=dws, gm_b_s=dbst[:, :GM_GROUPS].T,
                 sw_q_norm_w=dnw[0], sw_k_norm_w=dnw[1])
    token = grads_done("mix", dict(w_in=g_win, w_out=g_wout))
    return dx, small, dmod, token


def _permute_w_in(w):
    pad = jnp.zeros(w.shape[:-1] + (AB_PAD - 8,), w.dtype)
    return jnp.concatenate([w[..., 0:2056], pad, w[..., 2568:IN_WIDTH], w[..., 2056:2568]], axis=-1)


def _unpermute_w_in(g):
    return jnp.concatenate([g[..., 0:2056], g[..., C_UV:IN_PAD], g[..., C_SW:C_UV]], axis=-1)


def _local_step(x, target, mods, weights_of, grads_done, sp):
    layers = mods.shape[0]
    t, d = x.shape
    tabs = _rope_tables(t)
    consts = _consts(sp)
    saved = []
    for layer in range(layers):
        mod = mods[layer].reshape(6, 1, d)
        cs = {k: v[layer] for k, v in consts.items()}
        x, res, wb = _layer_fwd(x, mod, functools.partial(weights_of, layer), cs, tabs)
        saved.append((res, mod, wb, cs))
    dx, loss = loss_head(x, target)
    smalls, dmods = [], []
    token = jnp.zeros((1, 1), F32)
    for layer in reversed(range(layers)):
        res, mod, wb, cs = saved[layer]
        dx, small, dmod, token = _layer_bwd(dx, res, mod + token, wb, cs, tabs, functools.partial(grads_done, layer))
        smalls.append(small)
        dmods.append(dmod[0])
    smalls, dmods = smalls[::-1], dmods[::-1]
    small = {k: jnp.stack([s[k] for s in smalls]) for k in smalls[0]}
    return loss, dx, small, jnp.stack(dmods) + token


def mod_fwd(c_all, w_mod, b_shard):
    layers, d, n = w_mod.shape

    def body(c_ref, w_ref, b_ref, o_ref):
        ca = _silu(c_ref[...]).astype(BF16)
        o_ref[0] = _dot(ca, w_ref[0].astype(BF16), 1, 0) + b_ref[0]

    return _call(
        body, name="mod_fwd", grid=(layers,),
        in_specs=[_full((8, d)), pl.BlockSpec((1, d, n), lambda i: (i, 0, 0)),
                  pl.BlockSpec((1, 1, n), lambda i: (i, 0, 0))],
        out_specs=pl.BlockSpec((1, 8, n), lambda i: (i, 0, 0)),
        out_shape=jax.ShapeDtypeStruct((layers, 8, n), F32), semantics=("parallel",),
    )(c_all, w_mod, b_shard)


def mod_bwd(c_all, dmod):
    layers, _, n = dmod.shape
    d = c_all.shape[1]

    def body(c_ref, g_ref, o_ref):
        ca = _silu(c_ref[...]).astype(BF16)
        o_ref[0] = _dot(ca, g_ref[0].astype(BF16), 0, 0)

    return _call(
        body, name="mod_bwd", grid=(layers,),
        in_specs=[_full((8, d)), pl.BlockSpec((1, 8, n), lambda i: (i, 0, 0))],
        out_specs=pl.BlockSpec((1, d, n), lambda i: (i, 0, 0)),
        out_shape=jax.ShapeDtypeStruct((layers, d, n), F32), semantics=("parallel",),
    )(c_all, dmod)


N_DEV = 8


def _place():
    return lax.axis_index("x"), lax.axis_index("y"), lax.axis_index("c")


def _other_chips(x, y):
    return [(1 - x, y), (x, 1 - y), (1 - x, 1 - y)]


def allgather8(x_shard, *, name):
    m_per, n = x_shard.shape

    def body(x_ref, out_ref, send_sems, recv_sems, local_sem):
        x, y, c = _place()
        me, sibling = (x, y, c), (x, y, 1 - c)
        chips = _other_chips(x, y)

        def rows(px, py, pc):
            return out_ref.at[pl.ds((4 * px + 2 * py + pc) * m_per, m_per), :]

        def copy(k, block, to, src=None):
            return pltpu.make_async_remote_copy(
                src_ref=rows(*block) if src is None else src, dst_ref=rows(*block),
                send_sem=send_sems.at[k], recv_sem=recv_sems.at[k], device_id=to, device_id_type=MESH)

        mine = pltpu.make_async_copy(x_ref, rows(*me), local_sem)
        mine.start()
        first = [copy(0, me, sibling, src=x_ref)]
        first += [copy(1 + j, me, (*chip, c), src=x_ref) for j, chip in enumerate(chips)]
        for cp in first:
            cp.start()
        passed = [copy(4 + j, (*chip, c), sibling) for j, chip in enumerate(chips)]
        for j, chip in enumerate(chips):
            copy(1 + j, (*chip, c), me).wait_recv()
            passed[j].start()
        copy(0, sibling, me).wait_recv()
        for j, chip in enumerate(chips):
            copy(4 + j, (*chip, 1 - c), me).wait_recv()
        for cp in first + passed:
            cp.wait_send()
        mine.wait()

    return pl.pallas_call(
        body, name=name, out_shape=jax.ShapeDtypeStruct((N_DEV * m_per, n), x_shard.dtype),
        in_specs=[pl.BlockSpec(memory_space=pltpu.VMEM)], out_specs=pl.BlockSpec(memory_space=pltpu.VMEM),
        scratch_shapes=[pltpu.SemaphoreType.DMA((7,)), pltpu.SemaphoreType.DMA((7,)), pltpu.SemaphoreType.DMA],
    )(x_shard)


HBM = pl.BlockSpec(memory_space=pltpu.HBM)
SEM = pl.BlockSpec(memory_space=pltpu.SEMAPHORE)
_EFFECT = pltpu.SideEffectType.DATAFLOW_SIDE_EFFECTING


def _piece(ref, sliced, chip):
    return ref.at[2 * chip[0] + chip[1]] if sliced else ref


def exchange_start(srcs, after, *, sliced, name):
    n = len(srcs)
    piece = lambda s: s.shape[1:] if sliced else s.shape

    def body(*refs):
        ins, lands = refs[:n], refs[n:2 * n]
        send_sems, recv_sems = refs[2 * n + len(after):2 * n + len(after) + 2]
        token = refs[-1]
        x, y, c = _place()
        me_s = 2 * x + y
        for a in range(n):
            for j, chip in enumerate(_other_chips(x, y)):
                pltpu.make_async_remote_copy(
                    src_ref=_piece(ins[a], sliced, chip), dst_ref=lands[a].at[me_s], send_sem=send_sems.at[3 * a + j],
                    recv_sem=recv_sems.at[3 * a + j], device_id=(*chip, c), device_id_type=MESH).start()
        token[...] = jnp.zeros_like(token)

    zones = [pltpu.with_memory_space_constraint(lax.empty((4,) + piece(s), s.dtype), pltpu.HBM) for s in srcs]
    srcs = [pltpu.with_memory_space_constraint(s, pltpu.HBM) for s in srcs]
    out = pl.pallas_call(
        body, name=name,
        out_shape=(pltpu.SemaphoreType.DMA((3 * n,)), pltpu.SemaphoreType.DMA((3 * n,)),
                   *[pltpu.HBM(s.shape, s.dtype) for s in srcs], *[pltpu.HBM(z.shape, z.dtype) for z in zones],
                   jax.ShapeDtypeStruct((8, LANE), F32)),
        in_specs=[HBM] * (2 * n) + [ANY] * len(after),
        out_specs=(SEM, SEM, *[HBM] * (2 * n), pl.BlockSpec(memory_space=pltpu.VMEM)),
        input_output_aliases={i: 2 + i for i in range(2 * n)},
        compiler_params=pltpu.CompilerParams(has_side_effects=_EFFECT),
    )(*srcs, *zones, *after)
    return out[0], out[1], out[2:2 + n], out[2 + n:2 + 2 * n], out[-1]


def exchange_wait(send_sems, recv_sems, srcs, zones, after, *, which, sliced, name):
    n = len(srcs)

    def body(*refs):
        ins, lands = refs[:n], refs[n:2 * n]
        send_sems, recv_sems = refs[2 * n:2 * n + 2]
        x, y, c = _place()
        for a in range(n):
            for j, chip in enumerate(_other_chips(x, y)):
                copy = pltpu.make_async_remote_copy(
                    src_ref=_piece(ins[a], sliced, chip), dst_ref=lands[a].at[2 * chip[0] + chip[1]],
                    send_sem=send_sems.at[3 * which[a] + j], recv_sem=recv_sems.at[3 * which[a] + j],
                    device_id=(*chip, c), device_id_type=MESH)
                copy.wait_send()
                copy.wait_recv()

    out = pl.pallas_call(
        body, name=name,
        out_shape=tuple(pltpu.HBM(s.shape, s.dtype) for s in (*srcs, *zones)),
        in_specs=[HBM] * (2 * n) + [SEM, SEM, ANY], out_specs=tuple([HBM] * (2 * n)),
        input_output_aliases={i: i for i in range(2 * n)},
        compiler_params=pltpu.CompilerParams(has_side_effects=_EFFECT),
    )(*srcs, *zones, send_sems, recv_sems, after)
    return out[n:]


def sibling_swap(parts):
    n = len(parts)

    def body(*refs):
        ins, outs = refs[:n], refs[n:2 * n]
        send_sems, recv_sems = refs[2 * n:]
        x, y, c = _place()
        cps = []
        for a in range(n):
            cp = pltpu.make_async_remote_copy(
                src_ref=ins[a], dst_ref=outs[a], send_sem=send_sems.at[a], recv_sem=recv_sems.at[a],
                device_id=(x, y, 1 - c), device_id_type=MESH)
            cp.start()
            cps.append(cp)
        for cp in cps:
            cp.wait()

    return pl.pallas_call(
        body, name="sibling_swap", out_shape=[jax.ShapeDtypeStruct(p.shape, p.dtype) for p in parts],
        in_specs=[ANY] * n, out_specs=[ANY] * n,
        scratch_shapes=[pltpu.SemaphoreType.DMA((n,)), pltpu.SemaphoreType.DMA((n,))],
    )(*parts)


def _row_block(rows, cols, budget=1 << 20):
    best = rows if rows % 8 else 8
    for tr in range(8, rows + 1, 8):
        if rows % tr == 0 and tr * cols * 4 <= budget:
            best = tr
    return best


def chip_sum(own, recv, me_s, buf, layer, layers, *, name):
    r, n = own.shape
    tr = _row_block(r, n)
    steps = r // tr

    def body(me_ref, own_ref, recv_ref, *rest):
        o_ref = rest[-1]
        me = me_ref[0]
        acc = jnp.zeros((tr, n), F32)
        for s in range(4):
            acc = acc + jnp.where(me == s, own_ref[...], recv_ref[s].astype(F32))
        o_ref[...] = acc

    in_specs = [pl.BlockSpec((tr, n), lambda i, me: (i, 0)), pl.BlockSpec((4, tr, n), lambda i, me: (0, i, 0))]
    args = [me_s, own, recv]
    aliases = {}
    if buf is not None:
        in_specs.append(ANY)
        args.append(buf)
        aliases = {3: 0}
    return pl.pallas_call(
        body, name=name, out_shape=jax.ShapeDtypeStruct((layers * r, n), F32),
        grid_spec=pltpu.PrefetchScalarGridSpec(
            num_scalar_prefetch=1, grid=(steps,), in_specs=in_specs,
            out_specs=pl.BlockSpec((tr, n), lambda i, me: (layer * steps + i, 0))),
        input_output_aliases=aliases,
        compiler_params=pltpu.CompilerParams(dimension_semantics=("parallel",)),
    )(*args)


def _adam_update(w, g, m, v):
    m2 = ADAM_B1 * m + (1.0 - ADAM_B1) * g
    v2 = ADAM_B2 * v + (1.0 - ADAM_B2) * (g * g)
    m_hat = m2 / (1.0 - ADAM_B1 ** ADAM_STEP)
    v_hat = v2 / (1.0 - ADAM_B2 ** ADAM_STEP)
    delta = -ADAM_LR * (m_hat / (jnp.sqrt(v_hat) + ADAM_EPS) + ADAM_WD * w)
    return delta, m2, v2


def adamw(w, g_parts, m, v, *, name):
    r, n = w.shape
    tr = _row_block(r, n)
    k = len(g_parts)

    def body(*refs):
        w_ref, m_ref, v_ref = refs[k], refs[k + 1], refs[k + 2]
        g_ref, d_ref, m2_ref, v2_ref = refs[k + 3:]
        g = refs[0][...]
        for p in refs[1:k]:
            g = g + p[...]
        g_ref[...] = g
        d_ref[...], m2_ref[...], v2_ref[...] = _adam_update(w_ref[...], g, m_ref[...], v_ref[...])

    blk = pl.BlockSpec((tr, n), lambda i: (i, 0))
    shp = jax.ShapeDtypeStruct((r, n), F32)
    return _call(body, name=name, grid=(r // tr,), in_specs=[blk] * (k + 3), out_specs=[blk] * 4,
                 out_shape=[shp] * 4, semantics=("parallel",))(*g_parts, w, m, v)


def adamw_gathered(g_all, w, m, v, *, name):
    _, r, n = g_all.shape
    tr = _row_block(r, n * 4)

    def body(ga_ref, w_ref, m_ref, v_ref, g_ref, d_ref, m2_ref, v2_ref):
        g = ga_ref[0]
        for dev in range(1, N_DEV):
            g = g + ga_ref[dev]
        g_ref[...] = g
        d_ref[...], m2_ref[...], v2_ref[...] = _adam_update(w_ref[...], g, m_ref[...], v_ref[...])

    blk = pl.BlockSpec((tr, n), lambda i: (i, 0))
    shp = jax.ShapeDtypeStruct((r, n), F32)
    return _call(body, name=name, grid=(r // tr,),
                 in_specs=[pl.BlockSpec((N_DEV, tr, n), lambda i: (0, i, 0)), blk, blk, blk], out_specs=[blk] * 4,
                 out_shape=[shp] * 4, semantics=("parallel",))(g_all, w, m, v)


BIG = ("w_in", "w_out", "w_ffn_in", "w_ffn_out")
SMALL = ("b_mod", "mix_norm_w", "ffn_norm_w", "dn_conv_w", "dn_a_log", "dn_dt_bias", "dn_out_norm_w", "gm_ln_g",
         "gm_ln_b", "gm_w_s", "gm_b_s", "sw_q_norm_w", "sw_k_norm_w")
WEIGHTS = ("w_mod", "b_mod", "mix_norm_w", "ffn_norm_w", "w_in", "w_out", "dn_conv_w", "dn_a_log", "dn_dt_bias",
           "dn_out_norm_w", "gm_ln_g", "gm_ln_b", "gm_w_s", "gm_b_s", "sw_q_norm_w", "sw_k_norm_w", "w_ffn_in",
           "w_ffn_out")
PACK_ROWS = 8


def _pack(arrs):
    out = []
    for a in arrs:
        flat = a.reshape(-1).astype(F32)
        rows = -(-flat.shape[0] // (LANE * PACK_ROWS)) * PACK_ROWS
        out.append(jnp.pad(flat, (0, rows * LANE - flat.shape[0])).reshape(rows, LANE))
    return jnp.concatenate(out, axis=0)


def _unpack(packed, shapes):
    out, r0 = [], 0
    for shp in shapes:
        size = math.prod(shp)
        rows = -(-size // (LANE * PACK_ROWS)) * PACK_ROWS
        out.append(packed[r0:r0 + rows].reshape(-1)[:size].reshape(shp))
        r0 += rows
    return out


def kernel(x, c, w_mod, b_mod, mix_norm_w, ffn_norm_w, w_in, w_out, dn_conv_w, dn_a_log, dn_dt_bias, dn_out_norm_w, gm_ln_g, gm_ln_b, gm_w_s, gm_b_s, sw_q_norm_w, sw_k_norm_w, w_ffn_in, w_ffn_out, loss_target, m_w_mod, m_b_mod, m_mix_norm_w, m_ffn_norm_w, m_w_in, m_w_out, m_dn_conv_w, m_dn_a_log, m_dn_dt_bias, m_dn_out_norm_w, m_gm_ln_g, m_gm_ln_b, m_gm_w_s, m_gm_b_s, m_sw_q_norm_w, m_sw_k_norm_w, m_w_ffn_in, m_w_ffn_out, v_w_mod, v_b_mod, v_mix_norm_w, v_ffn_norm_w, v_w_in, v_w_out, v_dn_conv_w, v_dn_a_log, v_dn_dt_bias, v_dn_out_norm_w, v_gm_ln_g, v_gm_ln_b, v_gm_w_s, v_gm_b_s, v_sw_q_norm_w, v_sw_k_norm_w, v_w_ffn_in, v_w_ffn_out):
    w = dict(w_mod=w_mod, b_mod=b_mod, mix_norm_w=mix_norm_w, ffn_norm_w=ffn_norm_w, w_in=w_in, w_out=w_out,
             dn_conv_w=dn_conv_w, dn_a_log=dn_a_log, dn_dt_bias=dn_dt_bias, dn_out_norm_w=dn_out_norm_w,
             gm_ln_g=gm_ln_g, gm_ln_b=gm_ln_b, gm_w_s=gm_w_s, gm_b_s=gm_b_s, sw_q_norm_w=sw_q_norm_w,
             sw_k_norm_w=sw_k_norm_w, w_ffn_in=w_ffn_in, w_ffn_out=w_ffn_out)
    m = dict(w_mod=m_w_mod, b_mod=m_b_mod, mix_norm_w=m_mix_norm_w, ffn_norm_w=m_ffn_norm_w, w_in=m_w_in,
             w_out=m_w_out, dn_conv_w=m_dn_conv_w, dn_a_log=m_dn_a_log, dn_dt_bias=m_dn_dt_bias,
             dn_out_norm_w=m_dn_out_norm_w, gm_ln_g=m_gm_ln_g, gm_ln_b=m_gm_ln_b, gm_w_s=m_gm_w_s, gm_b_s=m_gm_b_s,
             sw_q_norm_w=m_sw_q_norm_w, sw_k_norm_w=m_sw_k_norm_w, w_ffn_in=m_w_ffn_in, w_ffn_out=m_w_ffn_out)
    v = dict(w_mod=v_w_mod, b_mod=v_b_mod, mix_norm_w=v_mix_norm_w, ffn_norm_w=v_ffn_norm_w, w_in=v_w_in,
             w_out=v_w_out, dn_conv_w=v_dn_conv_w, dn_a_log=v_dn_a_log, dn_dt_bias=v_dn_dt_bias,
             dn_out_norm_w=v_dn_out_norm_w, gm_ln_g=v_gm_ln_g, gm_ln_b=v_gm_ln_b, gm_w_s=v_gm_w_s, gm_b_s=v_gm_b_s,
             sw_q_norm_w=v_sw_q_norm_w, sw_k_norm_w=v_sw_k_norm_w, w_ffn_in=v_w_ffn_in, w_ffn_out=v_w_ffn_out)
    layers, d, mod_n = w_mod.shape
    mx, my, mc = _place()
    me_s = 2 * mx + my
    me_dev = 4 * mx + 2 * my + mc

    c_all = allgather8(_pad_rows(c, 8), name="gather_c").reshape(N_DEV, 8, d)[:, 0]
    b_shard = lax.dynamic_slice_in_dim(b_mod, me_s * mod_n, mod_n, axis=1)[:, None, :]
    mod_part = mod_fwd(c_all, w_mod, b_shard)
    mod_parts = allgather8(mod_part.reshape(layers * 8, mod_n), name="gather_mod")
    mod_parts = mod_parts.reshape(4, 2, layers, 8, mod_n)[:, 0]
    mod_all = mod_parts.transpose(1, 2, 0, 3).reshape(layers, 8, 4 * mod_n)
    mods = lax.dynamic_index_in_dim(mod_all, me_dev, axis=1, keepdims=False)

    cw = dn_conv_w.shape[-1]
    conv_rows = -(-layers * DN_CONV // 8) * 8
    conv_parts = allgather8(_pad_rows(dn_conv_w.reshape(layers * DN_CONV, cw), conv_rows), name="gather_conv")
    conv_parts = conv_parts.reshape(4, 2, conv_rows, cw)[:, 0, :layers * DN_CONV]
    conv_full = conv_parts.reshape(4, layers, DN_CONV, cw).transpose(1, 2, 0, 3).reshape(layers, DN_CONV, 4 * cw)

    shards = {k: w[k].astype(BF16) for k in BIG}
    groups = dict(w_in=(0,), w_out=(1,), ffn=(2, 3))
    gathers = [exchange_start([shards[k][layer] for k in BIG], [mods, conv_full], sliced=False, name=f"gather_start{layer}")
               for layer in range(layers)]
    mods = mods + sum(g[4][0, 0] for g in gathers)

    def weights_of(layer, group, after):
        send_sems, recv_sems, srcs, zones, _ = gathers[layer]
        which = groups[group]
        got = exchange_wait(send_sems, recv_sems, [srcs[a] for a in which], [zones[a] for a in which], after,
                            which=which, sliced=False, name=f"gather_wait_{group}{layer}")
        full = {BIG[a]: lax.dynamic_update_index_in_dim(z, shards[BIG[a]][layer], me_s, 0) for a, z in zip(which, got)}
        cols = lambda g: jnp.concatenate([g[s] for s in range(4)], axis=-1)
        shape = dict(w_in=lambda g: _permute_w_in(cols(g)), w_out=lambda g: g.reshape(-1, d), w_ffn_in=cols,
                     w_ffn_out=lambda g: g.reshape(-1, d))
        return {k: shape[k](g) for k, g in full.items()}

    scatters = {}
    last_scatter = []
    shard_axis = dict(w_in=1, w_out=0, w_ffn_in=1, w_ffn_out=0)

    def grads_done(layer, group, grads):
        fix = lambda k, g: _unpermute_w_in(g) if k == "w_in" else g
        send = [jnp.stack(jnp.split(fix(k, g16), 4, axis=shard_axis[k])) for k, (_, g16) in grads.items()]
        own = {}
        for k, (g32, _) in grads.items():
            g32 = fix(k, g32)
            size = g32.shape[shard_axis[k]] // 4
            own[k] = lax.dynamic_slice_in_dim(g32, me_s * size, size, axis=shard_axis[k])
        if (layer, group) == (0, "mix"):
            last_scatter.append((send, own))
            return jnp.zeros((1, 1), F32)
        started = exchange_start(send, [], sliced=True, name=f"scatter_start_{group}{layer}")
        scatters[layer, group] = (started, own)
        return started[4][:1, :1]

    sp = {k: w[k] for k in SMALL}
    sp["dn_conv_w"] = conv_full
    loss_blk, grad_x, small, dmods = _local_step(x[0], loss_target[0], mods, weights_of, grads_done, sp)
    loss = lax.psum(loss_blk[0, 0], ("x", "y", "c"))

    outs = {}
    small = dict(small, b_mod=dmods)
    packed = _pack([small[k] for k in SMALL])
    rows = packed.shape[0]
    g_all = allgather8(packed, name="gather_small").reshape(N_DEV, rows, LANE)
    send, own = last_scatter[0]
    scatters[0, "mix"] = (exchange_start(send, [g_all], sliced=True, name="scatter_start_mix0"), own)
    g_all = g_all + scatters[0, "mix"][0][4][0, 0]
    conv_zero = jnp.zeros((layers, DN_CONV, 3 * DN_WIDTH), F32)
    pk = lambda src: _pack([conv_zero if k == "dn_conv_w" else src[k] for k in SMALL])
    res = adamw_gathered(g_all, pk(w), pk(m), pk(v), name="adamw_small")
    shapes = [small[k].shape for k in SMALL]
    un = [_unpack(a, shapes) for a in res]
    for i, k in enumerate(SMALL):
        outs[k] = [un[j][i] for j in range(4)]
    g_conv = lax.dynamic_slice_in_dim(outs["dn_conv_w"][0], me_s * cw, cw, axis=2)
    flat = lambda a: a.reshape(-1, cw)
    res = adamw(flat(dn_conv_w), [flat(g_conv)], flat(m["dn_conv_w"]), flat(v["dn_conv_w"]), name="adamw_conv")
    outs["dn_conv_w"] = [a.reshape(dn_conv_w.shape) for a in res]

    b_rows = layers * 6 * d // LANE
    dmod_all = g_all[:, :b_rows].reshape(N_DEV, layers, 6 * d).transpose(1, 0, 2)
    dmod_shard = lax.dynamic_slice_in_dim(dmod_all, me_s * mod_n, mod_n, axis=2)
    g_wmod = mod_bwd(c_all, dmod_shard)
    flat = lambda a: a.reshape(-1, mod_n)
    res = adamw(flat(w_mod), [flat(g_wmod)], flat(m_w_mod), flat(v_w_mod), name="adamw_w_mod")
    outs["w_mod"] = [a.reshape(w_mod.shape) for a in res]

    me_arr = jnp.reshape(me_s, (1,)).astype(jnp.int32)
    partial = {k: None for k in BIG}
    for layer in range(layers):
        for group in ("ffn", "mix"):
            (send_sems, recv_sems, srcs, zones, _), own = scatters[layer, group]
            zones = exchange_wait(send_sems, recv_sems, srcs, zones, res[0], which=tuple(range(len(srcs))),
                                  sliced=True, name=f"scatter_wait_{group}{layer}")
            for k, z in zip(own, zones):
                partial[k] = chip_sum(own[k], z, me_arr, partial[k], layer, layers, name=f"chip_sum_{k}{layer}")
    partial = [partial[k] for k in BIG]
    theirs = sibling_swap(partial)
    for k, mine, other in zip(BIG, partial, theirs):
        shp = w[k].shape
        flat = lambda a: a.reshape(-1, shp[-1])
        res = adamw(flat(w[k]), [mine, other], flat(m[k]), flat(v[k]), name="adamw_" + k)
        outs[k] = [a.reshape(shp) for a in res]

    result = [loss, grad_x[None]]
    for j in range(4):
        result += [outs[k][j] for k in WEIGHTS]
    return tuple(result)
```

```python
import functools
import math

import jax
import jax.numpy as jnp
from jax import lax
from jax.experimental import pallas as pl
from jax.experimental.pallas import tpu as pltpu

F32 = jnp.float32
BF16 = jnp.bfloat16
HI = lax.Precision.HIGH

NORM_EPS = 1e-6
DN_HEADS = 4
DN_HEAD_DIM = 128
DN_WIDTH = 512
DN_CHUNK = 64
DN_CONV = 4
GM_WIDTH = 256
GM_GROUPS = 4
GM_GROUP_DIM = 64
GM_CHUNK = 128
SW_HEADS = 4
SW_HEAD_DIM = 64
SW_WIDTH = 256
SW_DILATIONS = (1, 4, 16)
SW_BLOCK = 128
ROPE_THETA = 500000.0
ROPE_DIM = 16
LANE = 128

C_QKV = 0
C_Z = 1536
C_AB = 2048
C_SW = 2304
C_UV = 4608
IN_WIDTH = 4872
IN_PAD = 5120
AB_PAD = C_SW - C_AB
MIX_WIDTH = 1024

ADAM_LR = 0.001
ADAM_B1 = 0.9
ADAM_B2 = 0.999
ADAM_EPS = 1e-08
ADAM_WD = 0.01
ADAM_STEP = 10

MESH = pl.DeviceIdType.MESH


BIG_VMEM = 56 << 20


def _call(body, *, name, grid, in_specs, out_specs, out_shape, scratch_shapes=(), semantics=None, aliases=None,
          vmem=None):
    if semantics is None:
        semantics = ("arbitrary",) * len(grid)
    return pl.pallas_call(
        body, name=name, grid=grid, in_specs=in_specs, out_specs=out_specs, out_shape=out_shape,
        scratch_shapes=list(scratch_shapes), input_output_aliases=aliases or {},
        compiler_params=pltpu.CompilerParams(dimension_semantics=semantics, vmem_limit_bytes=vmem),
    )


def _dot(a, b, ca, cb, prec=None):
    if a.ndim == 3:
        dims = (((ca + 1,), (cb + 1,)), ((0,), (0,)))
    else:
        dims = (((ca,), (cb,)), ((), ()))
    return lax.dot_general(a, b, dims, preferred_element_type=F32, precision=prec)


def _bdot(a, b, ca=1, cb=0):
    return _dot(a.astype(BF16), b.astype(BF16), ca, cb)


def _hdot(a, b, ca=1, cb=0):
    return _dot(a.astype(F32), b.astype(F32), ca, cb, HI)


def _split(x):
    hi = x.astype(BF16)
    return hi, (x - hi.astype(F32)).astype(BF16)


def _xdot(a, b, ca=1, cb=0, exact=1):
    if exact == 1:
        hi, lo = _split(a)
        e = b.astype(BF16)
        return _dot(hi, e, ca, cb) + _dot(lo, e, ca, cb)
    hi, lo = _split(b)
    e = a.astype(BF16)
    return _dot(e, hi, ca, cb) + _dot(e, lo, ca, cb)


def _sigmoid(x):
    return 0.5 * jnp.tanh(0.5 * x) + 0.5


def _silu(x):
    return x * _sigmoid(x)


def _dsilu(x):
    s = _sigmoid(x)
    return s * (1.0 + x * (1.0 - s))


def _softplus(x):
    return jnp.maximum(x, 0.0) + jnp.log(1.0 + jnp.exp(-jnp.abs(x)))


def _iota2(shape, dim):
    return lax.broadcasted_iota(jnp.int32, shape, dim)


def _rowsum(x):
    return jnp.sum(x, axis=-1, keepdims=True)


def _colsum(x):
    return jnp.sum(x, axis=-2, keepdims=True)


def _full(shape):
    return pl.BlockSpec(shape, lambda *_: (0,) * len(shape))


def _resident(shape):
    return pl.BlockSpec(shape, lambda *_: (0,) * len(shape), pipeline_mode=pl.Buffered(1))


ANY = pl.BlockSpec(memory_space=pl.ANY)


def _norm_mod(x, nw, scale, shift):
    r = lax.rsqrt(jnp.mean(x * x, axis=-1, keepdims=True) + NORM_EPS)
    xn = x * r
    return xn, r, (xn * nw) * (1.0 + scale) + shift


def norm_mm(x, nw, scale, shift, w, *, swiglu, name, tm=512):
    t, d = x.shape
    n = w.shape[1]
    half = n // 2

    def body(x_ref, nw_ref, sc_ref, sh_ref, w_ref, h_ref, y_ref, *act_ref):
        _, _, h = _norm_mod(x_ref[...], nw_ref[...], sc_ref[...], sh_ref[...])
        hb = h.astype(BF16)
        h_ref[...] = hb
        y = _dot(hb, w_ref[...], 1, 0)
        y_ref[...] = y.astype(y_ref.dtype)
        if swiglu:
            act_ref[0][...] = (_silu(y[:, :half]) * y[:, half:]).astype(BF16)

    row = lambda i: (i, 0)
    out_shape = [jax.ShapeDtypeStruct((t, d), BF16), jax.ShapeDtypeStruct((t, n), BF16 if swiglu else F32)]
    out_specs = [pl.BlockSpec((tm, d), row), pl.BlockSpec((tm, n), row)]
    if swiglu:
        out_shape.append(jax.ShapeDtypeStruct((t, half), BF16))
        out_specs.append(pl.BlockSpec((tm, half), row))
    return _call(
        body, name=name, grid=(t // tm,),
        in_specs=[pl.BlockSpec((tm, d), row), _full((1, d)), _full((1, d)), _full((1, d)), _resident((d, n))],
        out_specs=out_specs, out_shape=out_shape, semantics=("parallel",), vmem=BIG_VMEM,
    )(x, nw, scale, shift, w)


def resid_mm(y, w, x, gate, *, name, tm=512):
    t, k = y.shape
    d = w.shape[1]

    def body(y_ref, w_ref, x_ref, g_ref, xo_ref, o_ref):
        o = _dot(y_ref[...].astype(BF16), w_ref[...], 1, 0)
        o_ref[...] = o
        xo_ref[...] = x_ref[...] + g_ref[...] * o

    row = lambda i: (i, 0)
    return _call(
        body, name=name, grid=(t // tm,),
        in_specs=[pl.BlockSpec((tm, k), row), _resident((k, d)), pl.BlockSpec((tm, d), row), _full((1, d))],
        out_specs=[pl.BlockSpec((tm, d), row), pl.BlockSpec((tm, d), row)],
        out_shape=[jax.ShapeDtypeStruct((t, d), F32), jax.ShapeDtypeStruct((t, d), F32)],
        semantics=("parallel",), vmem=BIG_VMEM,
    )(y, w, x, gate)


def resid_mm_bwd(dx, gate, o, w, gu, *, name, tm):
    t, d = dx.shape
    k = w.shape[0]
    swiglu = gu is not None

    def body(dx_ref, g_ref, o_ref, w_ref, *rest):
        if swiglu:
            gu_ref, dy_ref, gx_ref, dg_ref = rest
        else:
            dy_ref, gx_ref, dg_ref = rest
        i = pl.program_id(0)
        dxv = dx_ref[...]
        gx = (dxv * g_ref[...]).astype(BF16)
        gx_ref[...] = gx
        part = _colsum(dxv * o_ref[...])

        @pl.when(i == 0)
        def _():
            dg_ref[...] = jnp.zeros_like(dg_ref)

        dg_ref[...] += part
        da = _dot(gx, w_ref[...], 1, 1)
        if swiglu:
            g = gu_ref[:, :k].astype(F32)
            u = gu_ref[:, k:].astype(F32)
            dy_ref[:, :k] = (da * u * _dsilu(g)).astype(BF16)
            dy_ref[:, k:] = (da * _silu(g)).astype(BF16)
        else:
            dy_ref[...] = da

    row = lambda i: (i, 0)
    in_specs = [pl.BlockSpec((tm, d), row), _full((1, d)), pl.BlockSpec((tm, d), row), _resident((k, d))]
    args = [dx, gate, o, w]
    if swiglu:
        in_specs.append(pl.BlockSpec((tm, 2 * k), row))
        args.append(gu)
        dy_shape = jax.ShapeDtypeStruct((t, 2 * k), BF16)
        dy_spec = pl.BlockSpec((tm, 2 * k), row)
    else:
        dy_shape = jax.ShapeDtypeStruct((t, k), F32)
        dy_spec = pl.BlockSpec((tm, k), row)
    return _call(
        body, name=name, grid=(t // tm,), in_specs=in_specs,
        out_specs=[dy_spec, pl.BlockSpec((tm, d), row), _full((1, d))],
        out_shape=[dy_shape, jax.ShapeDtypeStruct((t, d), BF16), jax.ShapeDtypeStruct((1, d), F32)], vmem=BIG_VMEM,
    )(*args)


def norm_mm_bwd(dy, w, x, nw, scale, dres, *, name, tm=512):
    t, n = dy.shape
    d = x.shape[1]
    steps = t // tm

    def body(dy_ref, w_ref, x_ref, nw_ref, sc_ref, dres_ref, dx_ref, dnw_ref, dsc_ref, dsh_ref):
        i = pl.program_id(0)
        dh = _dot(dy_ref[...].astype(BF16), w_ref[...], 1, 1)
        x = x_ref[...]
        r = lax.rsqrt(jnp.mean(x * x, axis=-1, keepdims=True) + NORM_EPS)
        xn = x * r
        a = nw_ref[...] * (1.0 + sc_ref[...])

        @pl.when(i == 0)
        def _():
            dnw_ref[...] = jnp.zeros_like(dnw_ref)
            dsh_ref[...] = jnp.zeros_like(dsh_ref)

        dnw_ref[...] += _colsum(dh * xn)
        dsh_ref[...] += _colsum(dh)
        dxn = dh * a
        dx_ref[...] = r * (dxn - xn * jnp.mean(dxn * xn, axis=-1, keepdims=True)) + dres_ref[...]

        @pl.when(i == steps - 1)
        def _():
            da = dnw_ref[...]
            dsc_ref[...] = da * nw_ref[...]
            dnw_ref[...] = da * (1.0 + sc_ref[...])

    row = lambda i: (i, 0)
    vec = jax.ShapeDtypeStruct((1, d), F32)
    return _call(
        body, name=name, grid=(steps,),
        in_specs=[pl.BlockSpec((tm, n), row), _resident((d, n)), pl.BlockSpec((tm, d), row), _full((1, d)),
                  _full((1, d)), pl.BlockSpec((tm, d), row)],
        out_specs=[pl.BlockSpec((tm, d), row), _full((1, d)), _full((1, d)), _full((1, d))],
        out_shape=[jax.ShapeDtypeStruct((t, d), F32), vec, vec, vec], vmem=BIG_VMEM,
    )(dy, w, x, nw, scale, dres)


def _pick_tn(n, k, budget=6 << 20):
    best = LANE
    for m in range(1, n // LANE + 1):
        tn = m * LANE
        if n % tn == 0 and k * tn * 4 <= budget:
            best = tn
    return best


def mm_tn(a, g, *, name, tt=2048):
    t, k = a.shape
    n = g.shape[1]
    tn = _pick_tn(n, k)
    tt = min(tt, t)
    steps = t // tt

    def body(a_ref, g_ref, o_ref, b_ref):
        i = pl.program_id(1)

        @pl.when(i == 0)
        def _():
            o_ref[...] = jnp.zeros_like(o_ref)

        o_ref[...] += _dot(a_ref[...].astype(BF16), g_ref[...].astype(BF16), 0, 0)

        @pl.when(i == steps - 1)
        def _():
            b_ref[...] = o_ref[...].astype(BF16)

    out = pl.BlockSpec((k, tn), lambda j, i: (0, j))
    return _call(
        body, name=name, grid=(n // tn, steps),
        in_specs=[pl.BlockSpec((tt, k), lambda j, i: (i, 0)), pl.BlockSpec((tt, tn), lambda j, i: (i, j))],
        out_specs=[out, out],
        out_shape=[jax.ShapeDtypeStruct((k, n), F32), jax.ShapeDtypeStruct((k, n), BF16)],
        semantics=("parallel", "arbitrary"), vmem=BIG_VMEM,
    )(a, g)


def loss_head(y, target, *, tm=512):
    t, d = y.shape
    steps = t // tm

    def body(y_ref, t_ref, dy_ref, l_ref, acc_ref):
        i = pl.program_id(0)

        @pl.when(i == 0)
        def _():
            acc_ref[...] = jnp.zeros_like(acc_ref)

        e = y_ref[...] - t_ref[...]
        dy_ref[...] = e * (1.0 / d)
        acc_ref[...] += _colsum(e * e)

        @pl.when(i == steps - 1)
        def _():
            tot = jnp.sum(acc_ref[...], axis=-1, keepdims=True) * (0.5 / d)
            l_ref[...] = jnp.broadcast_to(tot, l_ref.shape)

    row = lambda i: (i, 0)
    return _call(
        body, name="loss_head", grid=(steps,),
        in_specs=[pl.BlockSpec((tm, d), row), pl.BlockSpec((tm, d), row)],
        out_specs=[pl.BlockSpec((tm, d), row), _full((8, LANE))],
        out_shape=[jax.ShapeDtypeStruct((t, d), F32), jax.ShapeDtypeStruct((8, LANE), F32)],
        scratch_shapes=[pltpu.VMEM((1, d), F32)],
    )(y, target)


def _shift_rows(x, s):
    if s == 0:
        return x
    t = x.shape[0]
    ri = _iota2(x.shape, 0)
    rolled = pltpu.roll(x, s % t, axis=0)
    if s > 0:
        return jnp.where(ri >= s, rolled, 0.0)
    return jnp.where(ri < t + s, rolled, 0.0)


def _conv_pre(x, w):
    acc = x * w[DN_CONV - 1:DN_CONV, :]
    for j in range(DN_CONV - 1):
        acc = acc + _shift_rows(x, DN_CONV - 1 - j) * w[j:j + 1, :]
    return acc


def dn_conv(proj, conv_w):
    t = proj.shape[0]
    width = 3 * DN_WIDTH

    def body(x_ref, w_ref, o_ref):
        o_ref[...] = _silu(_conv_pre(x_ref[...], w_ref[...]))

    col = lambda j: (0, j)
    return _call(
        body, name="dn_conv", grid=(width // LANE,),
        in_specs=[pl.BlockSpec((t, LANE), col), pl.BlockSpec((8, LANE), col)],
        out_specs=pl.BlockSpec((t, LANE), col),
        out_shape=jax.ShapeDtypeStruct((t, width), F32), semantics=("parallel",),
    )(proj, conv_w)


def dn_conv_bwd(proj, conv_w, dact, dproj):
    t = proj.shape[0]
    width = 3 * DN_WIDTH

    def body(x_ref, w_ref, d_ref, _, dx_ref, dw_ref):
        x = x_ref[...]
        w = w_ref[...]
        dc = d_ref[...] * _dsilu(_conv_pre(x, w))
        dx = dc * w[DN_CONV - 1:DN_CONV, :]
        rows = []
        for j in range(DN_CONV - 1):
            s = DN_CONV - 1 - j
            dx = dx + _shift_rows(dc, -s) * w[j:j + 1, :]
            rows.append(_colsum(dc * _shift_rows(x, s)))
        rows.append(_colsum(dc * x))
        dx_ref[...] = dx.astype(BF16)
        ri = _iota2((8, LANE), 0)
        dw = jnp.zeros((8, LANE), F32)
        for j in range(DN_CONV):
            dw = dw + jnp.where(ri == j, rows[j], 0.0)
        dw_ref[...] = dw

    col = lambda j: (0, j)
    return _call(
        body, name="dn_conv_bwd", grid=(width // LANE,),
        in_specs=[pl.BlockSpec((t, LANE), col), pl.BlockSpec((8, LANE), col), pl.BlockSpec((t, LANE), col), ANY],
        out_specs=[pl.BlockSpec((t, LANE), col), pl.BlockSpec((8, LANE), col)],
        out_shape=[jax.ShapeDtypeStruct(dproj.shape, dproj.dtype), jax.ShapeDtypeStruct((8, width), F32)],
        semantics=("parallel",), aliases={3: 0},
    )(proj, conv_w, dact, dproj)


def _t(x):
    return jnp.swapaxes(x, -1, -2)


def _inv_unit_lower(a):
    c = a.shape[-1]
    eye = (_iota2((c, c), 0) == _iota2((c, c), 1)).astype(F32)
    x = eye - a
    p = _hdot(a, a)
    steps = int(math.log2(c)) - 1
    for i in range(steps):
        x = x + _hdot(x, p)
        if i < steps - 1:
            p = _hdot(p, p)
    return x


def _dn_local(q, k, v, a, b, alog, dtb, tinv=None):
    nh, c, d = q.shape
    rq = lax.rsqrt(_rowsum(q * q) + NORM_EPS)
    rk = lax.rsqrt(_rowsum(k * k) + NORM_EPS)
    qh = q * rq
    kn = k * rk
    qs = qh * (d ** -0.5)
    g = -jnp.exp(alog) * _softplus(a + dtb)
    beta = _sigmoid(b)
    ri = _iota2((c, c), 0)
    ci = _iota2((c, c), 1)
    causal = ri >= ci
    strict = ri > ci
    gb = jnp.broadcast_to(g, (nh, c, d))
    gcb = _xdot(jnp.broadcast_to(causal.astype(F32), (nh, c, c)), gb, exact=0)
    gc = gcb[..., :1]
    gl = _colsum(gb)[..., :1]
    dec = jnp.exp(jnp.where(causal, gc - _t(gcb)[:, :c, :], -1e30))
    kb = kn * beta
    amat = jnp.where(strict, _bdot(kb, kn, 1, 1) * dec, 0.0)
    if tinv is None:
        tinv = _inv_unit_lower(amat)
    e = jnp.exp(gc)
    f = jnp.exp(gl - gc)
    rw = kb * e
    sol = _hdot(tinv, jnp.concatenate([v * beta, rw], axis=-1))
    pmat = jnp.where(causal, _bdot(qs, kn, 1, 1) * dec, 0.0)
    return dict(rq=rq, rk=rk, qh=qh, kn=kn, qs=qs, g=g, beta=beta, causal=causal, strict=strict, gl=gl,
                dec=dec, kb=kb, amat=amat, tinv=tinv, e=e, f=f, rw=rw, u=sol[..., :d], w=sol[..., d:], pmat=pmat,
                qd=qs * e, kd=kn * f)


_DN_FIELDS = ("u", "w", "qd", "kd", "pmat", "gl")


def _dn_state(m, s_in):
    vnew = m["u"] - _bdot(m["w"], s_in)
    o = _bdot(m["qd"], s_in) + _bdot(m["pmat"], vnew)
    return vnew, o, s_in * jnp.exp(m["gl"]) + _bdot(m["kd"], vnew, 0, 0)


def _dn_state_bwd(m, s_in, do, ds_out):
    el = jnp.exp(m["gl"])
    dvnew = _bdot(m["pmat"], do, 0, 0) + _bdot(m["kd"], ds_out)
    dkd = _bdot(m["vnew"], ds_out, 1, 1)
    ds_in = _bdot(m["qd"], do, 0, 0) + el * ds_out - _bdot(m["w"], dvnew, 0, 0)
    dgl = el * _colsum(_rowsum(s_in * ds_out))
    return dvnew, dkd, dgl, ds_in


def _dn_local_bwd(m, q, v, a, alog, dtb, s_in, vnew, do, dvnew, dkd, dgl):
    nh, c, d = q.shape
    kn, qs, kb, u, w, e, f = m["kn"], m["qs"], m["kb"], m["u"], m["w"], m["e"], m["f"]
    beta, dec, tinv, kd, qd = m["beta"], m["dec"], m["tinv"], m["kd"], m["qd"]
    dp = jnp.where(m["causal"], _bdot(do, vnew, 1, 1), 0.0)
    dqd = _bdot(do, s_in, 1, 1)
    dw = -_bdot(dvnew, s_in, 1, 1)
    dsol = _hdot(tinv, jnp.concatenate([dvnew, dw], axis=-1), 0, 0)
    dru = dsol[..., :d]
    drw = dsol[..., d:]
    da_m = -jnp.where(m["strict"], _bdot(dsol, jnp.concatenate([u, w], axis=-1), 1, 1), 0.0)
    db_m = da_m * dec
    dq_m = dp * dec
    dkb = _bdot(db_m, kn)
    dkn = _bdot(db_m, kb, 0, 0) + _bdot(dq_m, qs, 0, 0)
    dqs = _bdot(dq_m, kn)
    gmat = da_m * m["amat"] + dp * m["pmat"]
    ones = jnp.ones((nh, c, d), F32)
    dgam = (_xdot(gmat, ones) - _xdot(gmat, ones, 0, 0))[..., :1]
    dqs = dqs + dqd * e
    dgam = dgam + _rowsum(dqd * qd)
    dkn = dkn + dkd * f
    tk = _rowsum(dkd * kd)
    dgam = dgam - tk
    dgl = dgl + _colsum(tk)
    dkb = dkb + drw * e
    dgam = dgam + _rowsum(drw * m["rw"])
    dv = dru * beta
    dbeta = _rowsum(dru * v) + _rowsum(dkb * kn)
    dkn = dkn + dkb * beta
    last = (_iota2((c, 1), 0) == c - 1).astype(F32)
    dgam = dgam + last * dgl
    upper = (_iota2((c, c), 0) <= _iota2((c, c), 1)).astype(F32)
    dg = _xdot(jnp.broadcast_to(upper, (nh, c, c)), jnp.broadcast_to(dgam, (nh, c, d)), exact=0)[..., :1]
    dqh = dqs * (d ** -0.5)
    dq = m["rq"] * (dqh - m["qh"] * _rowsum(dqh * m["qh"]))
    dk = m["rk"] * (dkn - kn * _rowsum(dkn * kn))
    sg = _sigmoid(a + dtb)
    da = dg * (-jnp.exp(alog)) * sg
    dalog = _colsum(dg * m["g"])
    ddtb = _colsum(da)
    db = dbeta * beta * (1.0 - beta)
    return dq, dk, dv, da, db, dalog, ddtb


def _dn_gate(o, z, wn):
    ro = lax.rsqrt(jnp.mean(o * o, axis=-1, keepdims=True) + NORM_EPS)
    n = o * ro
    return n, ro, n * wn * _silu(z)


DN_PAIR = 4


def _heads(ref, col0):
    d = DN_HEAD_DIM
    return jnp.stack([ref[j * DN_CHUNK:(j + 1) * DN_CHUNK, col0 + h * d:col0 + (h + 1) * d]
                      for j in range(DN_PAIR) for h in range(DN_HEADS)])


def _dn_inputs(act_ref, ab_ref, sc_ref):
    ab = ab_ref[...]
    sc = sc_ref[...]
    rows = lambda j: slice(j * DN_CHUNK, (j + 1) * DN_CHUNK)
    both = [(j, h) for j in range(DN_PAIR) for h in range(DN_HEADS)]
    q = _heads(act_ref, 0)
    k = _heads(act_ref, DN_WIDTH)
    v = _heads(act_ref, 2 * DN_WIDTH)
    a = jnp.stack([ab[rows(j), h:h + 1] for j, h in both])
    b = jnp.stack([ab[rows(j), DN_HEADS + h:DN_HEADS + h + 1] for j, h in both])
    alog = jnp.stack([sc[0:1, h:h + 1] for _, h in both])
    dtb = jnp.stack([sc[1:2, h:h + 1] for _, h in both])
    return q, k, v, a, b, alog, dtb


def _chunk_of(m, j, fields):
    return {f: m[f][j * DN_HEADS:(j + 1) * DN_HEADS] for f in fields}


def dn_fwd(act, proj, scal, wn):
    t = act.shape[0]
    n = t // DN_CHUNK
    d = DN_HEAD_DIM
    rows = DN_PAIR * DN_CHUNK

    def body(act_ref, z_ref, ab_ref, sc_ref, wn_ref, y_ref, st_ref, ti_ref, s_ref):
        @pl.when(pl.program_id(0) == 0)
        def _():
            s_ref[...] = jnp.zeros_like(s_ref)

        m = _dn_local(*_dn_inputs(act_ref, ab_ref, sc_ref))
        s = s_ref[...]
        outs = []
        for j in range(DN_PAIR):
            st_ref[j] = s
            ti_ref[j] = m["tinv"][j * DN_HEADS:(j + 1) * DN_HEADS]
            _, o, s = _dn_state(_chunk_of(m, j, _DN_FIELDS), s)
            outs.append(o)
        s_ref[...] = s
        y = _dn_gate(jnp.concatenate(outs, axis=0), _heads(z_ref, 0), wn_ref[...])[2]
        for j in range(DN_PAIR):
            for h in range(DN_HEADS):
                y_ref[j * DN_CHUNK:(j + 1) * DN_CHUNK, h * d:(h + 1) * d] = y[j * DN_HEADS + h]

    return _call(
        body, name="dn_fwd", grid=(n // DN_PAIR,),
        in_specs=[pl.BlockSpec((rows, 3 * DN_WIDTH), lambda i: (i, 0)),
                  pl.BlockSpec((rows, DN_WIDTH), lambda i: (i, C_Z // DN_WIDTH)),
                  pl.BlockSpec((rows, LANE), lambda i: (i, C_AB // LANE)),
                  _full((8, LANE)), _full((1, d))],
        out_specs=[pl.BlockSpec((rows, DN_WIDTH), lambda i: (i, 0)),
                   pl.BlockSpec((DN_PAIR, DN_HEADS, d, d), lambda i: (i, 0, 0, 0)),
                   pl.BlockSpec((DN_PAIR, DN_HEADS, DN_CHUNK, DN_CHUNK), lambda i: (i, 0, 0, 0))],
        out_shape=[jax.ShapeDtypeStruct((t, MIX_WIDTH), F32), jax.ShapeDtypeStruct((n, DN_HEADS, d, d), F32),
                   jax.ShapeDtypeStruct((n, DN_HEADS, DN_CHUNK, DN_CHUNK), F32)],
        scratch_shapes=[pltpu.VMEM((DN_HEADS, d, d), F32)],
    )(act, proj, proj, scal, wn)


def dn_bwd(act, proj, scal, wn, states, tinvs, dy):
    t = act.shape[0]
    n = t // DN_CHUNK
    steps = n // DN_PAIR
    d = DN_HEAD_DIM
    zab = DN_WIDTH + AB_PAD
    rows = DN_PAIR * DN_CHUNK

    def body(act_ref, z_ref, ab_ref, sc_ref, wn_ref, st_ref, ti_ref, dy_ref, dact_ref, dzab_ref, dpar_ref, ds_ref):
        @pl.when(pl.program_id(0) == 0)
        def _():
            ds_ref[...] = jnp.zeros_like(ds_ref)
            dpar_ref[...] = jnp.zeros_like(dpar_ref)

        wnv = wn_ref[...]
        q, k, v, a, b, alog, dtb = _dn_inputs(act_ref, ab_ref, sc_ref)
        batch = (DN_PAIR * DN_HEADS,)
        s_in = st_ref[...].reshape(batch + (d, d))
        m = _dn_local(q, k, v, a, b, alog, dtb, ti_ref[...].reshape(batch + (DN_CHUNK, DN_CHUNK)))
        vnew, o, _ = _dn_state(m, s_in)
        z = _heads(z_ref, 0)
        dyh = _heads(dy_ref, 0)
        nrm, ro, _ = _dn_gate(o, z, wnv)
        sz = _silu(z)
        dz = dyh * nrm * wnv * _dsilu(z)
        dn = dyh * wnv * sz
        dwn = _colsum(dyh * nrm * sz)
        do = ro * (dn - nrm * jnp.mean(dn * nrm, axis=-1, keepdims=True))
        ds = ds_ref[...]
        parts = [None] * DN_PAIR
        for j in reversed(range(DN_PAIR)):
            mj = dict(_chunk_of(m, j, _DN_FIELDS), vnew=vnew[j * DN_HEADS:(j + 1) * DN_HEADS])
            dvnew, dkd, dgl, ds = _dn_state_bwd(mj, s_in[j * DN_HEADS:(j + 1) * DN_HEADS],
                                                do[j * DN_HEADS:(j + 1) * DN_HEADS], ds)
            parts[j] = (dvnew, dkd, dgl)
        ds_ref[...] = ds
        dvnew, dkd, dgl = (jnp.concatenate([p[i] for p in parts], axis=0) for i in range(3))
        dq, dk, dv, da, db, dalog, ddtb = _dn_local_bwd(m, q, v, a, alog, dtb, s_in, vnew, do, dvnew, dkd, dgl)
        lane = _iota2((DN_CHUNK, LANE), 1)
        prow = _iota2((8, LANE), 0)
        plane = _iota2((8, LANE), 1)
        dpar = jnp.zeros((8, LANE), F32)
        for j in range(DN_PAIR):
            rs = slice(j * DN_CHUNK, (j + 1) * DN_CHUNK)
            dab = jnp.zeros((DN_CHUNK, LANE), F32)
            for h in range(DN_HEADS):
                n_ = j * DN_HEADS + h
                dzab_ref[rs, h * d:(h + 1) * d] = dz[n_].astype(BF16)
                dact_ref[rs, h * d:(h + 1) * d] = dq[n_]
                dact_ref[rs, DN_WIDTH + h * d:DN_WIDTH + (h + 1) * d] = dk[n_]
                dact_ref[rs, 2 * DN_WIDTH + h * d:2 * DN_WIDTH + (h + 1) * d] = dv[n_]
                dab = dab + jnp.where(lane == h, da[n_], 0.0) + jnp.where(lane == DN_HEADS + h, db[n_], 0.0)
                dpar = dpar + jnp.where((prow == 0) & (plane == h), dalog[n_], 0.0)
                dpar = dpar + jnp.where((prow == 1) & (plane == h), ddtb[n_], 0.0)
                dpar = dpar + jnp.where(prow == 2, dwn[n_], 0.0)
            dzab_ref[rs, DN_WIDTH:DN_WIDTH + LANE] = dab.astype(BF16)
            dzab_ref[rs, DN_WIDTH + LANE:] = jnp.zeros((DN_CHUNK, AB_PAD - LANE), BF16)
        dpar_ref[...] += dpar

    rev = lambda i: (steps - 1 - i, 0)
    rev4 = lambda i: (steps - 1 - i, 0, 0, 0)
    return _call(
        body, name="dn_bwd", grid=(steps,),
        in_specs=[pl.BlockSpec((rows, 3 * DN_WIDTH), rev),
                  pl.BlockSpec((rows, DN_WIDTH), lambda i: (steps - 1 - i, C_Z // DN_WIDTH)),
                  pl.BlockSpec((rows, LANE), lambda i: (steps - 1 - i, C_AB // LANE)),
                  _full((8, LANE)), _full((1, d)),
                  pl.BlockSpec((DN_PAIR, DN_HEADS, d, d), rev4),
                  pl.BlockSpec((DN_PAIR, DN_HEADS, DN_CHUNK, DN_CHUNK), rev4),
                  pl.BlockSpec((rows, DN_WIDTH), rev)],
        out_specs=[pl.BlockSpec((rows, 3 * DN_WIDTH), rev),
                   pl.BlockSpec((rows, zab), lambda i: (steps - 1 - i, C_Z // zab)), _full((8, LANE))],
        out_shape=[jax.ShapeDtypeStruct((t, 3 * DN_WIDTH), F32), jax.ShapeDtypeStruct((t, IN_PAD), BF16),
                   jax.ShapeDtypeStruct((8, LANE), F32)],
        scratch_shapes=[pltpu.VMEM((DN_HEADS, d, d), F32)],
    )(act, proj, proj, scal, wn, states, tinvs, dy)


_INV_SQRT2 = 0.7071067811865476
_INV_SQRT2PI = 0.3989422804014327


def _gelu(x):
    return 0.5 * x * (1.0 + lax.erf(x * _INV_SQRT2))


def _dgelu(x):
    return 0.5 * (1.0 + lax.erf(x * _INV_SQRT2)) + x * jnp.exp(-0.5 * x * x) * _INV_SQRT2PI


def _gm_core(uv, lng, lnb, ws_ref, bst):
    c = uv.shape[0]
    zz = _gelu(uv)
    u = zz[:, :GM_WIDTH]
    vv = zz[:, GM_WIDTH:]
    xc = vv - jnp.mean(vv, axis=-1, keepdims=True)
    rs = lax.rsqrt(jnp.mean(xc * xc, axis=-1, keepdims=True) + NORM_EPS)
    xh = xc * rs
    vn = xh * lng + lnb
    grp = _iota2((c, GM_WIDTH), 1) // GM_GROUP_DIM
    tril = _iota2((c, c), 0) >= _iota2((c, c), 1)
    sv = jnp.zeros((c, GM_WIDTH), F32)
    masks = []
    for g in range(GM_GROUPS):
        mk = grp == g
        masks.append(mk)
        ws = jnp.where(tril, ws_ref[g], 0.0)
        sv = sv + _bdot(ws, jnp.where(mk, vn, 0.0)) + jnp.where(mk, bst[:, g:g + 1], 0.0)
    return u, xh, rs, vn, sv, masks, tril


def gm_fwd(proj, lng, lnb, w_s, bst, ybuf):
    t = proj.shape[0]

    def body(uv_ref, g_ref, b_ref, ws_ref, bst_ref, _, y_ref):
        u, _, _, _, sv, _, _ = _gm_core(uv_ref[...], g_ref[...], b_ref[...], ws_ref, bst_ref[...])
        y_ref[...] = u * sv

    return _call(
        body, name="gm_fwd", grid=(t // GM_CHUNK,),
        in_specs=[pl.BlockSpec((GM_CHUNK, 2 * GM_WIDTH), lambda i: (i, C_UV // (2 * GM_WIDTH))),
                  _full((1, GM_WIDTH)), _full((1, GM_WIDTH)), _full((GM_GROUPS, GM_CHUNK, GM_CHUNK)),
                  _full((GM_CHUNK, LANE)), ANY],
        out_specs=pl.BlockSpec((GM_CHUNK, GM_WIDTH), lambda i: (i, DN_WIDTH // GM_WIDTH)),
        out_shape=jax.ShapeDtypeStruct(ybuf.shape, F32), semantics=("parallel",), aliases={5: 0},
    )(proj, lng, lnb, w_s, bst, ybuf)


def gm_bwd(proj, lng, lnb, w_s, bst, dy, dproj):
    t = proj.shape[0]

    def body(uv_ref, g_ref, b_ref, ws_ref, bst_ref, dy_ref, _, duv_ref, dws_ref, dbst_ref, dln_ref):
        @pl.when(pl.program_id(0) == 0)
        def _():
            dws_ref[...] = jnp.zeros_like(dws_ref)
            dbst_ref[...] = jnp.zeros_like(dbst_ref)
            dln_ref[...] = jnp.zeros_like(dln_ref)

        uv = uv_ref[...]
        lng = g_ref[...]
        u, xh, rs, vn, sv, masks, tril = _gm_core(uv, lng, b_ref[...], ws_ref, bst_ref[...])
        dyv = dy_ref[...]
        dsv = dyv * u
        lane = _iota2((GM_CHUNK, LANE), 1)
        dvn = jnp.zeros_like(dsv)
        dbst = jnp.zeros((GM_CHUNK, LANE), F32)
        for g in range(GM_GROUPS):
            ws = jnp.where(tril, ws_ref[g], 0.0)
            dsg = jnp.where(masks[g], dsv, 0.0)
            dvn = dvn + jnp.where(masks[g], _bdot(ws, dsv, 0, 0), 0.0)
            dws_ref[g] += jnp.where(tril, _bdot(dsg, vn, 1, 1), 0.0)
            dbst = dbst + jnp.where(lane == g, _rowsum(dsg), 0.0)
        dbst_ref[...] += dbst
        row = _iota2((8, GM_WIDTH), 0)
        dln_ref[...] += jnp.where(row == 0, _colsum(dvn * xh), 0.0) + jnp.where(row == 1, _colsum(dvn), 0.0)
        dxh = dvn * lng
        dvv = rs * (dxh - jnp.mean(dxh, axis=-1, keepdims=True) - xh * jnp.mean(dxh * xh, axis=-1, keepdims=True))
        dg = _dgelu(uv)
        duv_ref[:, :GM_WIDTH] = (dyv * sv * dg[:, :GM_WIDTH]).astype(BF16)
        duv_ref[:, GM_WIDTH:] = (dvv * dg[:, GM_WIDTH:]).astype(BF16)

    return _call(
        body, name="gm_bwd", grid=(t // GM_CHUNK,),
        in_specs=[pl.BlockSpec((GM_CHUNK, 2 * GM_WIDTH), lambda i: (i, C_UV // (2 * GM_WIDTH))),
                  _full((1, GM_WIDTH)), _full((1, GM_WIDTH)), _full((GM_GROUPS, GM_CHUNK, GM_CHUNK)),
                  _full((GM_CHUNK, LANE)),
                  pl.BlockSpec((GM_CHUNK, GM_WIDTH), lambda i: (i, DN_WIDTH // GM_WIDTH)), ANY],
        out_specs=[pl.BlockSpec((GM_CHUNK, 2 * GM_WIDTH), lambda i: (i, C_UV // (2 * GM_WIDTH))),
                   _full((GM_GROUPS, GM_CHUNK, GM_CHUNK)), _full((GM_CHUNK, LANE)), _full((8, GM_WIDTH))],
        out_shape=[jax.ShapeDtypeStruct(dproj.shape, dproj.dtype),
                   jax.ShapeDtypeStruct((GM_GROUPS, GM_CHUNK, GM_CHUNK), F32),
                   jax.ShapeDtypeStruct((GM_CHUNK, LANE), F32), jax.ShapeDtypeStruct((8, GM_WIDTH), F32)],
        aliases={6: 0},
    )(proj, lng, lnb, w_s, bst, dy, dproj)


def _head_mats():
    r = _iota2((SW_WIDTH, SW_WIDTH), 0)
    c = _iota2((SW_WIDTH, SW_WIDTH), 1)
    same = (r // SW_HEAD_DIM) == (c // SW_HEAD_DIM)
    cc = c % SW_HEAD_DIM
    half = ROPE_DIM // 2
    rot = jnp.where((cc < half) & (r == c + half), -1.0, 0.0) + jnp.where((cc >= half) & (cc < ROPE_DIM) & (r == c - half), 1.0, 0.0)
    return same.astype(F32), rot


def _seg_col(s):
    return C_SW // SW_WIDTH + (s // 2) * 3 + s % 2


def _halves(x):
    return x[:, :LANE], x[:, LANE:]


def sw_prep(proj, nw2, cos_t, sin_t, *, tm=512):
    t = proj.shape[0]

    def body(x_ref, w_ref, c_ref, s_ref, o_ref):
        same, rot = _head_mats()
        x = x_ref[...]
        r = lax.rsqrt(_xdot(x * x, same) * (1.0 / SW_HEAD_DIM) + NORM_EPS)
        xn = x * r * w_ref[0]
        o_ref[0, 0], o_ref[0, 1] = _halves(xn * c_ref[...] + _xdot(xn, rot) * s_ref[...])

    return _call(
        body, name="sw_prep", grid=(6, t // tm),
        in_specs=[pl.BlockSpec((tm, SW_WIDTH), lambda s, i: (i, _seg_col(s))),
                  pl.BlockSpec((1, 1, SW_WIDTH), lambda s, i: (s % 2, 0, 0)),
                  pl.BlockSpec((tm, SW_WIDTH), lambda s, i: (i, 0)),
                  pl.BlockSpec((tm, SW_WIDTH), lambda s, i: (i, 0))],
        out_specs=pl.BlockSpec((1, 2, tm, LANE), lambda s, i: (s, 0, i, 0)),
        out_shape=jax.ShapeDtypeStruct((6, 2, t, LANE), F32), semantics=("parallel", "parallel"),
    )(proj, nw2, cos_t, sin_t)


def sw_prep_bwd(proj, nw2, cos_t, sin_t, dkvq, dproj, dnw, p, *, tm=512):
    t = proj.shape[0]
    col0 = C_SW // SW_WIDTH + 3 * p
    seg_col = lambda s: col0 + (s + 1) % 3

    def body(x_ref, w_ref, c_ref, s_ref, d_ref, _, dw0_ref, dx_ref, dw_ref):
        s = pl.program_id(0)
        dout = jnp.concatenate([d_ref[0, 0], d_ref[0, 1]], axis=1)

        @pl.when(s == 1)
        def _():
            dx_ref[...] = dout.astype(BF16)

        @pl.when((s != 1) & (pl.program_id(1) == 0))
        def _():
            dw_ref[...] = dw0_ref[...]

        @pl.when(s != 1)
        def _():
            same, rot = _head_mats()
            x = x_ref[...]
            w = w_ref[0]
            r = lax.rsqrt(_xdot(x * x, same) * (1.0 / SW_HEAD_DIM) + NORM_EPS)
            xh = x * r
            dxn = dout * c_ref[...] + _xdot(dout * s_ref[...], rot, 1, 1)
            dw_ref[0] += _colsum(dxn * xh)
            dxh = dxn * w
            dx_ref[...] = (r * (dxh - xh * (_xdot(dxh * xh, same) * (1.0 / SW_HEAD_DIM)))).astype(BF16)

    return _call(
        body, name=f"sw_prep_bwd{p}", grid=(3, t // tm),
        in_specs=[pl.BlockSpec((tm, SW_WIDTH), lambda s, i: (i, seg_col(s))),
                  pl.BlockSpec((1, 1, SW_WIDTH), lambda s, i: (1 - s // 2, 0, 0)),
                  pl.BlockSpec((tm, SW_WIDTH), lambda s, i: (i, 0)),
                  pl.BlockSpec((tm, SW_WIDTH), lambda s, i: (i, 0)),
                  pl.BlockSpec((1, 2, tm, LANE), lambda s, i: (s, 0, i, 0)), ANY,
                  pl.BlockSpec((1, 1, SW_WIDTH), lambda s, i: (s // 2, 0, 0))],
        out_specs=[pl.BlockSpec((tm, SW_WIDTH), lambda s, i: (i, seg_col(s))),
                   pl.BlockSpec((1, 1, SW_WIDTH), lambda s, i: (s // 2, 0, 0))],
        out_shape=[jax.ShapeDtypeStruct(dproj.shape, dproj.dtype), jax.ShapeDtypeStruct((2, 1, SW_WIDTH), F32)],
        semantics=("arbitrary", "arbitrary"), aliases={5: 0},
    )(proj, nw2, cos_t, sin_t, dkvq, dproj, dnw)


_SW_SCALE = SW_HEAD_DIM ** -0.5
_NEG = -1e30


def _sw_masks(has_other):
    ri = _iota2((SW_BLOCK, SW_BLOCK), 0)
    ci = _iota2((SW_BLOCK, SW_BLOCK), 1)
    return ri >= ci, (ci >= ri) & has_other


def _pair(x):
    first = _iota2((1, LANE), 1) < SW_HEAD_DIM
    return jnp.stack([jnp.where(first, x, 0.0), jnp.where(first, 0.0, x)])


def _both(x):
    return jnp.broadcast_to(x.astype(BF16)[None], (2,) + x.shape)


def _unpair(x2):
    first = _iota2((1, LANE), 1) < SW_HEAD_DIM
    return jnp.where(first, x2[0], x2[1])


def _head_cols(x):
    return jnp.stack([x[:, 0:1], x[:, SW_HEAD_DIM:SW_HEAD_DIM + 1]])


SW_GROUP = 16


def _sw_geometry(t, p):
    dil = SW_DILATIONS[p]
    unit = SW_BLOCK * dil
    nb = max(1, SW_GROUP // dil)
    return dil, unit, nb, t // (unit * nb)


def _sw_groups(dil, nb, body):
    if nb * dil == SW_GROUP:
        body([(k // dil, k % dil) for k in range(SW_GROUP)])
    else:
        for g in range(nb * dil // SW_GROUP):
            body([(0, SW_GROUP * g + k) for k in range(SW_GROUP)])


def _sw_rows(i, r, dil):
    start = i * SW_BLOCK * dil + r
    return pl.ds(start, SW_BLOCK) if dil == 1 else pl.ds(start, SW_BLOCK, stride=dil)


def _sw_load(refs, probs, dil, shift, wrap, fn):
    out = []
    for i, r in probs:
        if shift != 0 and i == wrap:
            out.append(fn(refs[1][_sw_rows(0, r, dil), :]))
        else:
            out.append(fn(refs[0][_sw_rows(i + shift, r, dil), :]))
    return jnp.concatenate(out, axis=0)


def _sw_other_masks(probs, wrap, edge_ok):
    _, other = _sw_masks(edge_ok)
    _, always = _sw_masks(True)
    return jnp.stack([other if i == wrap else always for i, _ in probs for _ in range(2)])


def sw_attn(qk, proj, p):
    t = proj.shape[0]
    dil, unit, nb, nsp = _sw_geometry(t, p)
    vcol = (C_SW + 3 * SW_WIDTH * p + 2 * SW_WIDTH) // LANE

    def body(q_ref, kc_ref, kp_ref, vc_ref, vp_ref, o_ref, l_ref):
        mc, _ = _sw_masks(True)
        first = pl.program_id(1) != 0
        q_r, k_r, v_r = (q_ref.at[0, 0], None), (kc_ref.at[0, 0], kp_ref.at[0, 0]), (vc_ref, vp_ref)

        def one(probs):
            mp = _sw_other_masks(probs, 0, first)
            q2 = _sw_load(q_r, probs, dil, 0, 0, _pair)
            sc = jnp.where(mc, _bdot(q2, _sw_load(k_r, probs, dil, 0, 0, _both), 1, 1) * _SW_SCALE, _NEG)
            sp = jnp.where(mp, _bdot(q2, _sw_load(k_r, probs, dil, -1, 0, _both), 1, 1) * _SW_SCALE, _NEG)
            mx = jnp.maximum(jnp.max(sc, axis=-1, keepdims=True), jnp.max(sp, axis=-1, keepdims=True))
            pc = jnp.exp(sc - mx)
            pp = jnp.exp(sp - mx)
            den = _rowsum(pc) + _rowsum(pp)
            o2 = (_bdot(pc, _sw_load(v_r, probs, dil, 0, 0, _both))
                  + _bdot(pp, _sw_load(v_r, probs, dil, -1, 0, _both))) * (1.0 / den)
            l2 = jnp.broadcast_to(mx + jnp.log(den), o2.shape)
            for n, (i, r) in enumerate(probs):
                o_ref.at[0][_sw_rows(i, r, dil), :] = _unpair(o2[2 * n:2 * n + 2])
                l_ref.at[0][_sw_rows(i, r, dil), :] = _unpair(l2[2 * n:2 * n + 2])

        _sw_groups(dil, nb, one)

    before = lambda j: jnp.maximum(j * nb - 1, 0)
    seg = lambda s: pl.BlockSpec((1, 1, unit * nb, LANE), lambda h, j: (s, h, j, 0))
    seg_b = lambda s: pl.BlockSpec((1, 1, unit, LANE), lambda h, j: (s, h, before(j), 0))
    out = pl.BlockSpec((1, unit * nb, LANE), lambda h, j: (h, j, 0))
    shp = jax.ShapeDtypeStruct((2, t, LANE), F32)
    return _call(
        body, name=f"sw_attn{p}", grid=(2, nsp),
        in_specs=[seg(2 * p), seg(2 * p + 1), seg_b(2 * p + 1),
                  pl.BlockSpec((unit * nb, LANE), lambda h, j: (j, vcol + h)),
                  pl.BlockSpec((unit, LANE), lambda h, j: (before(j), vcol + h))],
        out_specs=[out, out], out_shape=[shp, shp], semantics=("parallel", "parallel"),
    )(qk, qk, qk, proj, proj)


def sw_attn_dkv(qk, proj, dy, lg, dm, p):
    t = proj.shape[0]
    dil, unit, nb, nsp = _sw_geometry(t, p)
    nunits = t // unit
    vcol = (C_SW + 3 * SW_WIDTH * p + 2 * SW_WIDTH) // LANE
    ycol = (DN_WIDTH + GM_WIDTH) // LANE

    def body(k_ref, v_ref, qc_ref, qn_ref, doc_ref, don_ref, lc_ref, ln_ref, dc_ref, dn_ref, o_ref):
        mc, _ = _sw_masks(True)
        more = pl.program_id(1) + 1 < nsp
        q_r, do_r = (qc_ref.at[0, 0], qn_ref.at[0, 0]), (doc_ref, don_ref)
        l_r, d_r = (lc_ref.at[0], ln_ref.at[0]), (dc_ref.at[0], dn_ref.at[0])

        def one(probs):
            k2 = _sw_load((k_ref.at[0, 0], None), probs, dil, 0, 0, _both)
            v2 = _sw_load((v_ref, None), probs, dil, 0, 0, _both)
            dk = jnp.zeros((2 * SW_GROUP, SW_BLOCK, LANE), F32)
            dv = jnp.zeros((2 * SW_GROUP, SW_BLOCK, LANE), F32)
            for shift, mk in ((0, mc), (1, _sw_other_masks(probs, nb - 1, more))):
                q2 = _sw_load(q_r, probs, dil, shift, nb - 1, _pair)
                do2 = _sw_load(do_r, probs, dil, shift, nb - 1, _pair)
                lse = _sw_load(l_r, probs, dil, shift, nb - 1, _head_cols)
                dd = _sw_load(d_r, probs, dil, shift, nb - 1, _head_cols)
                pr = jnp.exp(jnp.where(mk, _bdot(q2, k2, 1, 1) * _SW_SCALE, _NEG) - lse)
                dv = dv + _bdot(pr, do2, 0, 0)
                ds = pr * (_bdot(do2, v2, 1, 1) - dd)
                dk = dk + _bdot(ds, q2, 0, 0)
            for n, (i, r) in enumerate(probs):
                o_ref.at[0, 0][_sw_rows(i, r, dil), :] = (dk[2 * n] + dk[2 * n + 1]) * _SW_SCALE
                o_ref.at[1, 0][_sw_rows(i, r, dil), :] = dv[2 * n] + dv[2 * n + 1]

        _sw_groups(dil, nb, one)

    after = lambda j: jnp.minimum((j + 1) * nb, nunits - 1)
    seg = lambda s: pl.BlockSpec((1, 1, unit * nb, LANE), lambda h, j: (s, h, j, 0))
    seg_a = lambda s: pl.BlockSpec((1, 1, unit, LANE), lambda h, j: (s, h, after(j), 0))
    col = lambda c0: pl.BlockSpec((unit * nb, LANE), lambda h, j: (j, c0 + h))
    col_a = lambda c0: pl.BlockSpec((unit, LANE), lambda h, j: (after(j), c0 + h))
    hp = pl.BlockSpec((1, unit * nb, LANE), lambda h, j: (h, j, 0))
    hp_a = pl.BlockSpec((1, unit, LANE), lambda h, j: (h, after(j), 0))
    return _call(
        body, name=f"sw_dkv{p}", grid=(2, nsp),
        in_specs=[seg(2 * p + 1), col(vcol), seg(2 * p), seg_a(2 * p), col(ycol), col_a(ycol), hp, hp_a, hp, hp_a],
        out_specs=pl.BlockSpec((2, 1, unit * nb, LANE), lambda h, j: (0, h, j, 0)),
        out_shape=jax.ShapeDtypeStruct((3, 2, t, LANE), F32), semantics=("parallel", "parallel"),
    )(qk, proj, qk, qk, dy, dy, lg, lg, dm, dm)


def sw_attn_dq(qk, proj, dy, lg, dm, dkvq, p):
    t = proj.shape[0]
    dil, unit, nb, nsp = _sw_geometry(t, p)
    vcol = (C_SW + 3 * SW_WIDTH * p + 2 * SW_WIDTH) // LANE
    ycol = (DN_WIDTH + GM_WIDTH) // LANE

    def body(q_ref, kc_ref, kp_ref, vc_ref, vp_ref, do_ref, l_ref, d_ref, _, dq_ref):
        mc, _ = _sw_masks(True)
        first = pl.program_id(1) != 0
        k_r, v_r = (kc_ref.at[0, 0], kp_ref.at[0, 0]), (vc_ref, vp_ref)

        def one(probs):
            mp = _sw_other_masks(probs, 0, first)
            q2 = _sw_load((q_ref.at[0, 0], None), probs, dil, 0, 0, _pair)
            do2 = _sw_load((do_ref, None), probs, dil, 0, 0, _pair)
            lse = _sw_load((l_ref.at[0], None), probs, dil, 0, 0, _head_cols)
            dd = _sw_load((d_ref.at[0], None), probs, dil, 0, 0, _head_cols)
            kc = _sw_load(k_r, probs, dil, 0, 0, _both)
            kp = _sw_load(k_r, probs, dil, -1, 0, _both)
            pc = jnp.exp(jnp.where(mc, _bdot(q2, kc, 1, 1) * _SW_SCALE, _NEG) - lse)
            pp = jnp.exp(jnp.where(mp, _bdot(q2, kp, 1, 1) * _SW_SCALE, _NEG) - lse)
            dsc = pc * (_bdot(do2, _sw_load(v_r, probs, dil, 0, 0, _both), 1, 1) - dd)
            dsp = pp * (_bdot(do2, _sw_load(v_r, probs, dil, -1, 0, _both), 1, 1) - dd)
            dq2 = (_bdot(dsc, kc) + _bdot(dsp, kp)) * _SW_SCALE
            for n, (i, r) in enumerate(probs):
                dq_ref.at[0, 0][_sw_rows(i, r, dil), :] = _unpair(dq2[2 * n:2 * n + 2])

        _sw_groups(dil, nb, one)

    before = lambda j: jnp.maximum(j * nb - 1, 0)
    seg = lambda s: pl.BlockSpec((1, 1, unit * nb, LANE), lambda h, j: (s, h, j, 0))
    seg_b = lambda s: pl.BlockSpec((1, 1, unit, LANE), lambda h, j: (s, h, before(j), 0))
    col = lambda c0: pl.BlockSpec((unit * nb, LANE), lambda h, j: (j, c0 + h))
    col_b = lambda c0: pl.BlockSpec((unit, LANE), lambda h, j: (before(j), c0 + h))
    hp = pl.BlockSpec((1, unit * nb, LANE), lambda h, j: (h, j, 0))
    return _call(
        body, name=f"sw_dq{p}", grid=(2, nsp),
        in_specs=[seg(2 * p), seg(2 * p + 1), seg_b(2 * p + 1), col(vcol), col_b(vcol), col(ycol), hp, hp, ANY],
        out_specs=pl.BlockSpec((1, 1, unit * nb, LANE), lambda h, j: (2, h, j, 0)),
        out_shape=jax.ShapeDtypeStruct(dkvq.shape, F32), semantics=("parallel", "parallel"), aliases={8: 0},
    )(qk, qk, qk, proj, proj, dy, lg, dm, dkvq)


def sw_merge(outs, lses, ybuf, *, tm=512):
    t = ybuf.shape[0]

    def body(o0, o1, o2, l0_ref, l1_ref, l2_ref, _, y_ref, lg_ref):
        l0, l1, l2 = l0_ref[...], l1_ref[...], l2_ref[...]
        mx = jnp.maximum(jnp.maximum(l0, l1), l2)
        lg = mx + jnp.log(jnp.exp(l0 - mx) + jnp.exp(l1 - mx) + jnp.exp(l2 - mx))
        lg_ref[...] = lg
        y = jnp.exp(l0 - lg) * o0[...] + jnp.exp(l1 - lg) * o1[...] + jnp.exp(l2 - lg) * o2[...]
        y_ref[...] = jnp.concatenate([y[0], y[1]], axis=1)

    hp = pl.BlockSpec((2, tm, LANE), lambda i: (0, i, 0))
    return _call(
        body, name="sw_merge", grid=(t // tm,), in_specs=[hp] * 6 + [ANY],
        out_specs=[pl.BlockSpec((tm, SW_WIDTH), lambda i: (i, (DN_WIDTH + GM_WIDTH) // SW_WIDTH)), hp],
        out_shape=[jax.ShapeDtypeStruct(ybuf.shape, F32), jax.ShapeDtypeStruct((2, t, LANE), F32)],
        semantics=("parallel",), aliases={6: 0},
    )(*outs, *lses, ybuf)


def sw_delta(dy, ybuf, *, tm=512):
    t = ybuf.shape[0]

    def body(dy_ref, y_ref, o_ref):
        same, _ = _head_mats()
        o_ref[0], o_ref[1] = _halves(_xdot(dy_ref[...] * y_ref[...], same))

    b1 = pl.BlockSpec((tm, SW_WIDTH), lambda i: (i, (DN_WIDTH + GM_WIDTH) // SW_WIDTH))
    return _call(body, name="sw_delta", grid=(t // tm,), in_specs=[b1, b1],
                 out_specs=pl.BlockSpec((2, tm, LANE), lambda i: (0, i, 0)),
                 out_shape=jax.ShapeDtypeStruct((2, t, LANE), F32), semantics=("parallel",))(dy, ybuf)


def _rope_tables(t):
    inv = ROPE_THETA ** (-jnp.arange(0, ROPE_DIM, 2, dtype=F32) / ROPE_DIM)
    ang = jnp.arange(t, dtype=F32)[:, None] * inv[None, :]
    pad1 = jnp.ones((t, SW_HEAD_DIM - ROPE_DIM), F32)
    pad0 = jnp.zeros((t, SW_HEAD_DIM - ROPE_DIM), F32)
    cos_h = jnp.concatenate([jnp.cos(ang), jnp.cos(ang), pad1], axis=1)
    sin_h = jnp.concatenate([jnp.sin(ang), jnp.sin(ang), pad0], axis=1)
    return jnp.tile(cos_h, (1, SW_HEADS)), jnp.tile(sin_h, (1, SW_HEADS))


def sw_forward(proj, nw2, cos_t, sin_t, ybuf):
    qk = sw_prep(proj, nw2, cos_t, sin_t)
    outs, lses = [], []
    for p in range(len(SW_DILATIONS)):
        o, lse = sw_attn(qk, proj, p)
        outs.append(o)
        lses.append(lse)
    ybuf, lg = sw_merge(outs, lses, ybuf)
    return ybuf, (qk, lg)


def sw_backward(proj, nw2, cos_t, sin_t, res, ybuf, dy, dproj):
    qk, lg = res
    dm = sw_delta(dy, ybuf)
    dnw = jnp.zeros((2, 1, SW_WIDTH), F32)
    for p in range(len(SW_DILATIONS)):
        dkvq = sw_attn_dkv(qk, proj, dy, lg, dm, p)
        dkvq = sw_attn_dq(qk, proj, dy, lg, dm, dkvq, p)
        dproj, dnw = sw_prep_bwd(proj, nw2, cos_t, sin_t, dkvq, dproj, dnw, p)
    return dproj, dnw[::-1, 0]


def _pad_rows(a, rows):
    return jnp.zeros((rows,) + a.shape[1:], a.dtype).at[:a.shape[0]].set(a)


def _consts(sp):
    d = {}
    d["mix_nw"] = sp["mix_norm_w"][:, None, :]
    d["ffn_nw"] = sp["ffn_norm_w"][:, None, :]
    d["cw8"] = jnp.pad(sp["dn_conv_w"], ((0, 0), (0, 8 - DN_CONV), (0, 0)))
    d["scal"] = jnp.pad(jnp.stack([sp["dn_a_log"], sp["dn_dt_bias"]], axis=1), ((0, 0), (0, 6), (0, LANE - DN_HEADS)))
    d["wn"] = sp["dn_out_norm_w"][:, None, :]
    d["lng"] = sp["gm_ln_g"][:, None, :]
    d["lnb"] = sp["gm_ln_b"][:, None, :]
    d["w_s"] = sp["gm_w_s"]
    d["bst"] = jnp.pad(jnp.swapaxes(sp["gm_b_s"], 1, 2), ((0, 0), (0, 0), (0, LANE - GM_GROUPS)))
    d["nw2"] = jnp.stack([jnp.tile(sp["sw_q_norm_w"], (1, SW_HEADS)),
                          jnp.tile(sp["sw_k_norm_w"], (1, SW_HEADS))], axis=1)[:, :, None, :]
    return d


def _layer_fwd(x, mod, get_w, cs, tabs):
    wb = dict(get_w("w_in", x))
    h1, proj = norm_mm(x, cs["mix_nw"], mod[1], mod[0], wb["w_in"], swiglu=False, name="in_proj")
    act = dn_conv(proj, cs["cw8"])
    y, states, tinvs = dn_fwd(act, proj, cs["scal"], cs["wn"])
    y = gm_fwd(proj, cs["lng"], cs["lnb"], cs["w_s"], cs["bst"], y)
    y, swres = sw_forward(proj, cs["nw2"], *tabs, y)
    wb.update(get_w("w_out", y))
    x1, o1 = resid_mm(y, wb["w_out"], x, mod[2], name="out_proj")
    wb.update(get_w("ffn", x1))
    h2, gu, actf = norm_mm(x1, cs["ffn_nw"], mod[4], mod[3], wb["w_ffn_in"], swiglu=True, name="ffn_in")
    x2, o2 = resid_mm(actf, wb["w_ffn_out"], x1, mod[5], name="ffn_out")
    res = dict(x=x, h1=h1, proj=proj, act=act, states=states, tinvs=tinvs, swres=swres, y=y, x1=x1, o1=o1, h2=h2, gu=gu,
               actf=actf, o2=o2)
    return x2, res, wb


def _layer_bwd(dx2, res, mod, wb, cs, tabs, grads_done):
    dgu, gx2, dgate2 = resid_mm_bwd(dx2, mod[5], res["o2"], wb["w_ffn_out"], res["gu"], name="ffn_out_bwd", tm=512)
    g_wfo = mm_tn(res["actf"], gx2, name="wg_ffn_out")
    g_wfi = mm_tn(res["h2"], dgu, name="wg_ffn_in")
    token = grads_done("ffn", dict(w_ffn_in=g_wfi, w_ffn_out=g_wfo))
    dx1, d_ffn_nw, dscale2, dshift2 = norm_mm_bwd(dgu, wb["w_ffn_in"], res["x1"], cs["ffn_nw"], mod[4] + token, dx2,
                                                  name="ffn_in_bwd")
    dy, gx1, dgate1 = resid_mm_bwd(dx1, mod[2], res["o1"], wb["w_out"], None, name="out_proj_bwd", tm=512)
    g_wout = mm_tn(res["y"], gx1, name="wg_out")
    proj = res["proj"]
    dact, dproj, dpar = dn_bwd(res["act"], proj, cs["scal"], cs["wn"], res["states"], res["tinvs"], dy)
    dproj, dcw = dn_conv_bwd(proj, cs["cw8"], dact, dproj)
    dproj, dws, dbst, dln = gm_bwd(proj, cs["lng"], cs["lnb"], cs["w_s"], cs["bst"], dy, dproj)
    dproj, dnw = sw_backward(proj, cs["nw2"], *tabs, res["swres"], res["y"], dy, dproj)
    g_win = mm_tn(res["h1"], dproj, name="wg_in")
    dx, d_mix_nw, dscale1, dshift1 = norm_mm_bwd(dproj, wb["w_in"], res["x"], cs["mix_nw"], mod[1], dx1,
                                                 name="in_proj_bwd")
    dmod = jnp.concatenate([dshift1, dscale1, dgate1, dshift2, dscale2, dgate2], axis=1)
    dnw = dnw.reshape(2, SW_HEADS, SW_HEAD_DIM).sum(1)
    small = dict(mix_norm_w=d_mix_nw[0], ffn_norm_w=d_ffn_nw[0], dn_conv_w=dcw[:DN_CONV],
                 dn_a_log=dpar[0, :DN_HEADS], dn_dt_bias=dpar[1, :DN_HEADS], dn_out_norm_w=dpar[2],
                 gm_ln_g=dln[0], gm_ln_b=dln[1], gm_w_s=dws, gm_b_s=dbst[:, :GM_GROUPS].T,
                 sw_q_norm_w=dnw[0], sw_k_norm_w=dnw[1])
    token = grads_done("mix", dict(w_in=g_win, w_out=g_wout))
    return dx, small, dmod, token


def _permute_w_in(w):
    pad = jnp.zeros(w.shape[:-1] + (AB_PAD - 8,), w.dtype)
    return jnp.concatenate([w[..., 0:2056], pad, w[..., 2568:IN_WIDTH], w[..., 2056:2568]], axis=-1)


def _unpermute_w_in(g):
    return jnp.concatenate([g[..., 0:2056], g[..., C_UV:IN_PAD], g[..., C_SW:C_UV]], axis=-1)


def _local_step(x, target, mods, weights_of, grads_done, sp):
    layers = mods.shape[0]
    t, d = x.shape
    tabs = _rope_tables(t)
    consts = _consts(sp)
    saved = []
    for layer in range(layers):
        mod = mods[layer].reshape(6, 1, d)
        cs = {k: v[layer] for k, v in consts.items()}
        x, res, wb = _layer_fwd(x, mod, functools.partial(weights_of, layer), cs, tabs)
        saved.append((res, mod, wb, cs))
    dx, loss = loss_head(x, target)
    smalls, dmods = [], []
    token = jnp.zeros((1, 1), F32)
    for layer in reversed(range(layers)):
        res, mod, wb, cs = saved[layer]
        dx, small, dmod, token = _layer_bwd(dx, res, mod + token, wb, cs, tabs, functools.partial(grads_done, layer))
        smalls.append(small)
        dmods.append(dmod[0])
    smalls, dmods = smalls[::-1], dmods[::-1]
    small = {k: jnp.stack([s[k] for s in smalls]) for k in smalls[0]}
    return loss, dx, small, jnp.stack(dmods) + token


def mod_fwd(c_all, w_mod, b_shard):
    layers, d, n = w_mod.shape

    def body(c_ref, w_ref, b_ref, o_ref):
        ca = _silu(c_ref[...]).astype(BF16)
        o_ref[0] = _dot(ca, w_ref[0].astype(BF16), 1, 0) + b_ref[0]

    return _call(
        body, name="mod_fwd", grid=(layers,),
        in_specs=[_full((8, d)), pl.BlockSpec((1, d, n), lambda i: (i, 0, 0)),
                  pl.BlockSpec((1, 1, n), lambda i: (i, 0, 0))],
        out_specs=pl.BlockSpec((1, 8, n), lambda i: (i, 0, 0)),
        out_shape=jax.ShapeDtypeStruct((layers, 8, n), F32), semantics=("parallel",),
    )(c_all, w_mod, b_shard)


def mod_bwd(c_all, dmod):
    layers, _, n = dmod.shape
    d = c_all.shape[1]

    def body(c_ref, g_ref, o_ref):
        ca = _silu(c_ref[...]).astype(BF16)
        o_ref[0] = _dot(ca, g_ref[0].astype(BF16), 0, 0)

    return _call(
        body, name="mod_bwd", grid=(layers,),
        in_specs=[_full((8, d)), pl.BlockSpec((1, 8, n), lambda i: (i, 0, 0))],
        out_specs=pl.BlockSpec((1, d, n), lambda i: (i, 0, 0)),
        out_shape=jax.ShapeDtypeStruct((layers, d, n), F32), semantics=("parallel",),
    )(c_all, dmod)


N_DEV = 8


def _place():
    return lax.axis_index("x"), lax.axis_index("y"), lax.axis_index("c")


def _other_chips(x, y):
    return [(1 - x, y), (x, 1 - y), (1 - x, 1 - y)]


def allgather8(x_shard, *, name):
    m_per, n = x_shard.shape

    def body(x_ref, out_ref, send_sems, recv_sems, local_sem):
        x, y, c = _place()
        me, sibling = (x, y, c), (x, y, 1 - c)
        chips = _other_chips(x, y)

        def rows(px, py, pc):
            return out_ref.at[pl.ds((4 * px + 2 * py + pc) * m_per, m_per), :]

        def copy(k, block, to, src=None):
            return pltpu.make_async_remote_copy(
                src_ref=rows(*block) if src is None else src, dst_ref=rows(*block),
                send_sem=send_sems.at[k], recv_sem=recv_sems.at[k], device_id=to, device_id_type=MESH)

        mine = pltpu.make_async_copy(x_ref, rows(*me), local_sem)
        mine.start()
        first = [copy(0, me, sibling, src=x_ref)]
        first += [copy(1 + j, me, (*chip, c), src=x_ref) for j, chip in enumerate(chips)]
        for cp in first:
            cp.start()
        passed = [copy(4 + j, (*chip, c), sibling) for j, chip in enumerate(chips)]
        for j, chip in enumerate(chips):
            copy(1 + j, (*chip, c), me).wait_recv()
            passed[j].start()
        copy(0, sibling, me).wait_recv()
        for j, chip in enumerate(chips):
            copy(4 + j, (*chip, 1 - c), me).wait_recv()
        for cp in first + passed:
            cp.wait_send()
        mine.wait()

    return pl.pallas_call(
        body, name=name, out_shape=jax.ShapeDtypeStruct((N_DEV * m_per, n), x_shard.dtype),
        in_specs=[pl.BlockSpec(memory_space=pltpu.VMEM)], out_specs=pl.BlockSpec(memory_space=pltpu.VMEM),
        scratch_shapes=[pltpu.SemaphoreType.DMA((7,)), pltpu.SemaphoreType.DMA((7,)), pltpu.SemaphoreType.DMA],
    )(x_shard)


HBM = pl.BlockSpec(memory_space=pltpu.HBM)
SEM = pl.BlockSpec(memory_space=pltpu.SEMAPHORE)
_EFFECT = pltpu.SideEffectType.DATAFLOW_SIDE_EFFECTING


def _piece(ref, sliced, chip):
    return ref.at[2 * chip[0] + chip[1]] if sliced else ref


def exchange_start(srcs, after, *, sliced, name):
    n = len(srcs)
    piece = lambda s: s.shape[1:] if sliced else s.shape

    def body(*refs):
        ins, lands = refs[:n], refs[n:2 * n]
        send_sems, recv_sems = refs[2 * n + len(after):2 * n + len(after) + 2]
        token = refs[-1]
        x, y, c = _place()
        me_s = 2 * x + y
        for a in range(n):
            for j, chip in enumerate(_other_chips(x, y)):
                pltpu.make_async_remote_copy(
                    src_ref=_piece(ins[a], sliced, chip), dst_ref=lands[a].at[me_s], send_sem=send_sems.at[3 * a + j],
                    recv_sem=recv_sems.at[3 * a + j], device_id=(*chip, c), device_id_type=MESH).start()
        token[...] = jnp.zeros_like(token)

    zones = [pltpu.with_memory_space_constraint(lax.empty((4,) + piece(s), s.dtype), pltpu.HBM) for s in srcs]
    srcs = [pltpu.with_memory_space_constraint(s, pltpu.HBM) for s in srcs]
    out = pl.pallas_call(
        body, name=name,
        out_shape=(pltpu.SemaphoreType.DMA((3 * n,)), pltpu.SemaphoreType.DMA((3 * n,)),
                   *[pltpu.HBM(s.shape, s.dtype) for s in srcs], *[pltpu.HBM(z.shape, z.dtype) for z in zones],
                   jax.ShapeDtypeStruct((8, LANE), F32)),
        in_specs=[HBM] * (2 * n) + [ANY] * len(after),
        out_specs=(SEM, SEM, *[HBM] * (2 * n), pl.BlockSpec(memory_space=pltpu.VMEM)),
        input_output_aliases={i: 2 + i for i in range(2 * n)},
        compiler_params=pltpu.CompilerParams(has_side_effects=_EFFECT),
    )(*srcs, *zones, *after)
    return out[0], out[1], out[2:2 + n], out[2 + n:2 + 2 * n], out[-1]


def exchange_wait(send_sems, recv_sems, srcs, zones, after, *, which, sliced, name):
    n = len(srcs)

    def body(*refs):
        ins, lands = refs[:n], refs[n:2 * n]
        send_sems, recv_sems = refs[2 * n:2 * n + 2]
        x, y, c = _place()
        for a in range(n):
            for j, chip in enumerate(_other_chips(x, y)):
                copy = pltpu.make_async_remote_copy(
                    src_ref=_piece(ins[a], sliced, chip), dst_ref=lands[a].at[2 * chip[0] + chip[1]],
                    send_sem=send_sems.at[3 * which[a] + j], recv_sem=recv_sems.at[3 * which[a] + j],
                    device_id=(*chip, c), device_id_type=MESH)
                copy.wait_send()
                copy.wait_recv()

    out = pl.pallas_call(
        body, name=name,
        out_shape=tuple(pltpu.HBM(s.shape, s.dtype) for s in (*srcs, *zones)),
        in_specs=[HBM] * (2 * n) + [SEM, SEM, ANY], out_specs=tuple([HBM] * (2 * n)),
        input_output_aliases={i: i for i in range(2 * n)},
        compiler_params=pltpu.CompilerParams(has_side_effects=_EFFECT),
    )(*srcs, *zones, send_sems, recv_sems, after)
    return out[n:]


def sibling_swap(parts):
    n = len(parts)

    def body(*refs):
        ins, outs = refs[:n], refs[n:2 * n]
        send_sems, recv_sems = refs[2 * n:]
        x, y, c = _place()
        cps = []
        for a in range(n):
            cp = pltpu.make_async_remote_copy(
                src_ref=ins[a], dst_ref=outs[a], send_sem=send_sems.at[a], recv_sem=recv_sems.at[a],
                device_id=(x, y, 1 - c), device_id_type=MESH)
            cp.start()
            cps.append(cp)
        for cp in cps:
            cp.wait()

    return pl.pallas_call(
        body, name="sibling_swap", out_shape=[jax.ShapeDtypeStruct(p.shape, p.dtype) for p in parts],
        in_specs=[ANY] * n, out_specs=[ANY] * n,
        scratch_shapes=[pltpu.SemaphoreType.DMA((n,)), pltpu.SemaphoreType.DMA((n,))],
    )(*parts)


def _row_block(rows, cols, budget=1 << 20):
    best = rows if rows % 8 else 8
    for tr in range(8, rows + 1, 8):
        if rows % tr == 0 and tr * cols * 4 <= budget:
            best = tr
    return best


def chip_sum(own, recv, me_s, buf, layer, layers, *, name):
    r, n = own.shape
    tr = _row_block(r, n)
    steps = r // tr

    def body(me_ref, own_ref, recv_ref, *rest):
        o_ref = rest[-1]
        me = me_ref[0]
        acc = jnp.zeros((tr, n), F32)
        for s in range(4):
            acc = acc + jnp.where(me == s, own_ref[...], recv_ref[s].astype(F32))
        o_ref[...] = acc

    in_specs = [pl.BlockSpec((tr, n), lambda i, me: (i, 0)), pl.BlockSpec((4, tr, n), lambda i, me: (0, i, 0))]
    args = [me_s, own, recv]
    aliases = {}
    if buf is not None:
        in_specs.append(ANY)
        args.append(buf)
        aliases = {3: 0}
    return pl.pallas_call(
        body, name=name, out_shape=jax.ShapeDtypeStruct((layers * r, n), F32),
        grid_spec=pltpu.PrefetchScalarGridSpec(
            num_scalar_prefetch=1, grid=(steps,), in_specs=in_specs,
            out_specs=pl.BlockSpec((tr, n), lambda i, me: (layer * steps + i, 0))),
        input_output_aliases=aliases,
        compiler_params=pltpu.CompilerParams(dimension_semantics=("parallel",)),
    )(*args)


def _adam_update(w, g, m, v):
    m2 = ADAM_B1 * m + (1.0 - ADAM_B1) * g
    v2 = ADAM_B2 * v + (1.0 - ADAM_B2) * (g * g)
    m_hat = m2 / (1.0 - ADAM_B1 ** ADAM_STEP)
    v_hat = v2 / (1.0 - ADAM_B2 ** ADAM_STEP)
    delta = -ADAM_LR * (m_hat / (jnp.sqrt(v_hat) + ADAM_EPS) + ADAM_WD * w)
    return delta, m2, v2


def adamw(w, g_parts, m, v, *, name):
    r, n = w.shape
    tr = _row_block(r, n)
    k = len(g_parts)

    def body(*refs):
        w_ref, m_ref, v_ref = refs[k], refs[k + 1], refs[k + 2]
        g_ref, d_ref, m2_ref, v2_ref = refs[k + 3:]
        g = refs[0][...]
        for p in refs[1:k]:
            g = g + p[...]
        g_ref[...] = g
        d_ref[...], m2_ref[...], v2_ref[...] = _adam_update(w_ref[...], g, m_ref[...], v_ref[...])

    blk = pl.BlockSpec((tr, n), lambda i: (i, 0))
    shp = jax.ShapeDtypeStruct((r, n), F32)
    return _call(body, name=name, grid=(r // tr,), in_specs=[blk] * (k + 3), out_specs=[blk] * 4,
                 out_shape=[shp] * 4, semantics=("parallel",))(*g_parts, w, m, v)


def adamw_gathered(g_all, w, m, v, *, name):
    _, r, n = g_all.shape
    tr = _row_block(r, n * 4)

    def body(ga_ref, w_ref, m_ref, v_ref, g_ref, d_ref, m2_ref, v2_ref):
        g = ga_ref[0]
        for dev in range(1, N_DEV):
            g = g + ga_ref[dev]
        g_ref[...] = g
        d_ref[...], m2_ref[...], v2_ref[...] = _adam_update(w_ref[...], g, m_ref[...], v_ref[...])

    blk = pl.BlockSpec((tr, n), lambda i: (i, 0))
    shp = jax.ShapeDtypeStruct((r, n), F32)
    return _call(body, name=name, grid=(r // tr,),
                 in_specs=[pl.BlockSpec((N_DEV, tr, n), lambda i: (0, i, 0)), blk, blk, blk], out_specs=[blk] * 4,
                 out_shape=[shp] * 4, semantics=("parallel",))(g_all, w, m, v)


BIG = ("w_in", "w_out", "w_ffn_in", "w_ffn_out")
SMALL = ("b_mod", "mix_norm_w", "ffn_norm_w", "dn_conv_w", "dn_a_log", "dn_dt_bias", "dn_out_norm_w", "gm_ln_g",
         "gm_ln_b", "gm_w_s", "gm_b_s", "sw_q_norm_w", "sw_k_norm_w")
WEIGHTS = ("w_mod", "b_mod", "mix_norm_w", "ffn_norm_w", "w_in", "w_out", "dn_conv_w", "dn_a_log", "dn_dt_bias",
           "dn_out_norm_w", "gm_ln_g", "gm_ln_b", "gm_w_s", "gm_b_s", "sw_q_norm_w", "sw_k_norm_w", "w_ffn_in",
           "w_ffn_out")
PACK_ROWS = 8


def _pack(arrs):
    out = []
    for a in arrs:
        flat = a.reshape(-1).astype(F32)
        rows = -(-flat.shape[0] // (LANE * PACK_ROWS)) * PACK_ROWS
        out.append(jnp.pad(flat, (0, rows * LANE - flat.shape[0])).reshape(rows, LANE))
    return jnp.concatenate(out, axis=0)


def _unpack(packed, shapes):
    out, r0 = [], 0
    for shp in shapes:
        size = math.prod(shp)
        rows = -(-size // (LANE * PACK_ROWS)) * PACK_ROWS
        out.append(packed[r0:r0 + rows].reshape(-1)[:size].reshape(shp))
        r0 += rows
    return out


def kernel(x, c, w_mod, b_mod, mix_norm_w, ffn_norm_w, w_in, w_out, dn_conv_w, dn_a_log, dn_dt_bias, dn_out_norm_w, gm_ln_g, gm_ln_b, gm_w_s, gm_b_s, sw_q_norm_w, sw_k_norm_w, w_ffn_in, w_ffn_out, loss_target, m_w_mod, m_b_mod, m_mix_norm_w, m_ffn_norm_w, m_w_in, m_w_out, m_dn_conv_w, m_dn_a_log, m_dn_dt_bias, m_dn_out_norm_w, m_gm_ln_g, m_gm_ln_b, m_gm_w_s, m_gm_b_s, m_sw_q_norm_w, m_sw_k_norm_w, m_w_ffn_in, m_w_ffn_out, v_w_mod, v_b_mod, v_mix_norm_w, v_ffn_norm_w, v_w_in, v_w_out, v_dn_conv_w, v_dn_a_log, v_dn_dt_bias, v_dn_out_norm_w, v_gm_ln_g, v_gm_ln_b, v_gm_w_s, v_gm_b_s, v_sw_q_norm_w, v_sw_k_norm_w, v_w_ffn_in, v_w_ffn_out):
    w = dict(w_mod=w_mod, b_mod=b_mod, mix_norm_w=mix_norm_w, ffn_norm_w=ffn_norm_w, w_in=w_in, w_out=w_out,
             dn_conv_w=dn_conv_w, dn_a_log=dn_a_log, dn_dt_bias=dn_dt_bias, dn_out_norm_w=dn_out_norm_w,
             gm_ln_g=gm_ln_g, gm_ln_b=gm_ln_b, gm_w_s=gm_w_s, gm_b_s=gm_b_s, sw_q_norm_w=sw_q_norm_w,
             sw_k_norm_w=sw_k_norm_w, w_ffn_in=w_ffn_in, w_ffn_out=w_ffn_out)
    m = dict(w_mod=m_w_mod, b_mod=m_b_mod, mix_norm_w=m_mix_norm_w, ffn_norm_w=m_ffn_norm_w, w_in=m_w_in,
             w_out=m_w_out, dn_conv_w=m_dn_conv_w, dn_a_log=m_dn_a_log, dn_dt_bias=m_dn_dt_bias,
             dn_out_norm_w=m_dn_out_norm_w, gm_ln_g=m_gm_ln_g, gm_ln_b=m_gm_ln_b, gm_w_s=m_gm_w_s, gm_b_s=m_gm_b_s,
             sw_q_norm_w=m_sw_q_norm_w, sw_k_norm_w=m_sw_k_norm_w, w_ffn_in=m_w_ffn_in, w_ffn_out=m_w_ffn_out)
    v = dict(w_mod=v_w_mod, b_mod=v_b_mod, mix_norm_w=v_mix_norm_w, ffn_norm_w=v_ffn_norm_w, w_in=v_w_in,
             w_out=v_w_out, dn_conv_w=v_dn_conv_w, dn_a_log=v_dn_a_log, dn_dt_bias=v_dn_dt_bias,
             dn_out_norm_w=v_dn_out_norm_w, gm_ln_g=v_gm_ln_g, gm_ln_b=v_gm_ln_b, gm_w_s=v_gm_w_s, gm_b_s=v_gm_b_s,
             sw_q_norm_w=v_sw_q_norm_w, sw_k_norm_w=v_sw_k_norm_w, w_ffn_in=v_w_ffn_in, w_ffn_out=v_w_ffn_out)
    layers, d, mod_n = w_mod.shape
    mx, my, mc = _place()
    me_s = 2 * mx + my
    me_dev = 4 * mx + 2 * my + mc

    c_all = allgather8(_pad_rows(c, 8), name="gather_c").reshape(N_DEV, 8, d)[:, 0]
    b_shard = lax.dynamic_slice_in_dim(b_mod, me_s * mod_n, mod_n, axis=1)[:, None, :]
    mod_part = mod_fwd(c_all, w_mod, b_shard)
    mod_parts = allgather8(mod_part.reshape(layers * 8, mod_n), name="gather_mod")
    mod_parts = mod_parts.reshape(4, 2, layers, 8, mod_n)[:, 0]
    mod_all = mod_parts.transpose(1, 2, 0, 3).reshape(layers, 8, 4 * mod_n)
    mods = lax.dynamic_index_in_dim(mod_all, me_dev, axis=1, keepdims=False)

    cw = dn_conv_w.shape[-1]
    conv_rows = -(-layers * DN_CONV // 8) * 8
    conv_parts = allgather8(_pad_rows(dn_conv_w.reshape(layers * DN_CONV, cw), conv_rows), name="gather_conv")
    conv_parts = conv_parts.reshape(4, 2, conv_rows, cw)[:, 0, :layers * DN_CONV]
    conv_full = conv_parts.reshape(4, layers, DN_CONV, cw).transpose(1, 2, 0, 3).reshape(layers, DN_CONV, 4 * cw)

    shards = {k: w[k].astype(BF16) for k in BIG}
    groups = dict(w_in=(0,), w_out=(1,), ffn=(2, 3))
    gathers = [exchange_start([shards[k][layer] for k in BIG], [mods, conv_full], sliced=False, name=f"gather_start{layer}")
               for layer in range(layers)]
    mods = mods + sum(g[4][0, 0] for g in gathers)

    def weights_of(layer, group, after):
        send_sems, recv_sems, srcs, zones, _ = gathers[layer]
        which = groups[group]
        got = exchange_wait(send_sems, recv_sems, [srcs[a] for a in which], [zones[a] for a in which], after,
                            which=which, sliced=False, name=f"gather_wait_{group}{layer}")
        full = {BIG[a]: lax.dynamic_update_index_in_dim(z, shards[BIG[a]][layer], me_s, 0) for a, z in zip(which, got)}
        cols = lambda g: jnp.concatenate([g[s] for s in range(4)], axis=-1)
        shape = dict(w_in=lambda g: _permute_w_in(cols(g)), w_out=lambda g: g.reshape(-1, d), w_ffn_in=cols,
                     w_ffn_out=lambda g: g.reshape(-1, d))
        return {k: shape[k](g) for k, g in full.items()}

    scatters = {}
    last_scatter = []
    shard_axis = dict(w_in=1, w_out=0, w_ffn_in=1, w_ffn_out=0)

    def grads_done(layer, group, grads):
        fix = lambda k, g: _unpermute_w_in(g) if k == "w_in" else g
        send = [jnp.stack(jnp.split(fix(k, g16), 4, axis=shard_axis[k])) for k, (_, g16) in grads.items()]
        own = {}
        for k, (g32, _) in grads.items():
            g32 = fix(k, g32)
            size = g32.shape[shard_axis[k]] // 4
            own[k] = lax.dynamic_slice_in_dim(g32, me_s * size, size, axis=shard_axis[k])
        if (layer, group) == (0, "mix"):
            last_scatter.append((send, own))
            return jnp.zeros((1, 1), F32)
        started = exchange_start(send, [], sliced=True, name=f"scatter_start_{group}{layer}")
        scatters[layer, group] = (started, own)
        return started[4][:1, :1]

    sp = {k: w[k] for k in SMALL}
    sp["dn_conv_w"] = conv_full
    loss_blk, grad_x, small, dmods = _local_step(x[0], loss_target[0], mods, weights_of, grads_done, sp)
    loss = lax.psum(loss_blk[0, 0], ("x", "y", "c"))

    outs = {}
    small = dict(small, b_mod=dmods)
    packed = _pack([small[k] for k in SMALL])
    rows = packed.shape[0]
    g_all = allgather8(packed, name="gather_small").reshape(N_DEV, rows, LANE)
    send, own = last_scatter[0]
    scatters[0, "mix"] = (exchange_start(send, [g_all], sliced=True, name="scatter_start_mix0"), own)
    g_all = g_all + scatters[0, "mix"][0][4][0, 0]
    conv_zero = jnp.zeros((layers, DN_CONV, 3 * DN_WIDTH), F32)
    pk = lambda src: _pack([conv_zero if k == "dn_conv_w" else src[k] for k in SMALL])
    res = adamw_gathered(g_all, pk(w), pk(m), pk(v), name="adamw_small")
    shapes = [small[k].shape for k in SMALL]
    un = [_unpack(a, shapes) for a in res]
    for i, k in enumerate(SMALL):
        outs[k] = [un[j][i] for j in range(4)]
    g_conv = lax.dynamic_slice_in_dim(outs["dn_conv_w"][0], me_s * cw, cw, axis=2)
    flat = lambda a: a.reshape(-1, cw)
    res = adamw(flat(dn_conv_w), [flat(g_conv)], flat(m["dn_conv_w"]), flat(v["dn_conv_w"]), name="adamw_conv")
    outs["dn_conv_w"] = [a.reshape(dn_conv_w.shape) for a in res]

    b_rows = layers * 6 * d // LANE
    dmod_all = g_all[:, :b_rows].reshape(N_DEV, layers, 6 * d).transpose(1, 0, 2)
    dmod_shard = lax.dynamic_slice_in_dim(dmod_all, me_s * mod_n, mod_n, axis=2)
    g_wmod = mod_bwd(c_all, dmod_shard)
    flat = lambda a: a.reshape(-1, mod_n)
    res = adamw(flat(w_mod), [flat(g_wmod)], flat(m_w_mod), flat(v_w_mod), name="adamw_w_mod")
    outs["w_mod"] = [a.reshape(w_mod.shape) for a in res]

    me_arr = jnp.reshape(me_s, (1,)).astype(jnp.int32)
    partial = {k: None for k in BIG}
    for layer in range(layers):
        for group in ("ffn", "mix"):
            (send_sems, recv_sems, srcs, zones, _), own = scatters[layer, group]
            zones = exchange_wait(send_sems, recv_sems, srcs, zones, res[0], which=tuple(range(len(srcs))),
                                  sliced=True, name=f"scatter_wait_{group}{layer}")
            for k, z in zip(own, zones):
                partial[k] = chip_sum(own[k], z, me_arr, partial[k], layer, layers, name=f"chip_sum_{k}{layer}")
    partial = [partial[k] for k in BIG]
    theirs = sibling_swap(partial)
    for k, mine, other in zip(BIG, partial, theirs):
        shp = w[k].shape
        flat = lambda a: a.reshape(-1, shp[-1])
        res = adamw(flat(w[k]), [mine, other], flat(m[k]), flat(v[k]), name="adamw_" + k)
        outs[k] = [a.reshape(shp) for a in res]

    result = [loss, grad_x[None]]
    for j in range(4):
        result += [outs[k][j] for k in WEIGHTS]
    return tuple(result)
```

```python
import functools
import math

import jax
import jax.numpy as jnp
from jax import lax
from jax.experimental import pallas as pl
from jax.experimental.pallas import tpu as pltpu

F32 = jnp.float32
BF16 = jnp.bfloat16
HI = lax.Precision.HIGH

NORM_EPS = 1e-6
DN_HEADS = 4
DN_HEAD_DIM = 128
DN_WIDTH = 512
DN_CHUNK = 64
DN_CONV = 4
GM_WIDTH = 256
GM_GROUPS = 4
GM_GROUP_DIM = 64
GM_CHUNK = 128
SW_HEADS = 4
SW_HEAD_DIM = 64
SW_WIDTH = 256
SW_DILATIONS = (1, 4, 16)
SW_BLOCK = 128
ROPE_THETA = 500000.0
ROPE_DIM = 16
LANE = 128

C_QKV = 0
C_Z = 1536
C_AB = 2048
C_SW = 2304
C_UV = 4608
IN_WIDTH = 4872
IN_PAD = 5120
AB_PAD = C_SW - C_AB
MIX_WIDTH = 1024

ADAM_LR = 0.001
ADAM_B1 = 0.9
ADAM_B2 = 0.999
ADAM_EPS = 1e-08
ADAM_WD = 0.01
ADAM_STEP = 10

MESH = pl.DeviceIdType.MESH


BIG_VMEM = 56 << 20


def _call(body, *, name, grid, in_specs, out_specs, out_shape, scratch_shapes=(), semantics=None, aliases=None,
          vmem=None):
    if semantics is None:
        semantics = ("arbitrary",) * len(grid)
    return pl.pallas_call(
        body, name=name, grid=grid, in_specs=in_specs, out_specs=out_specs, out_shape=out_shape,
        scratch_shapes=list(scratch_shapes), input_output_aliases=aliases or {},
        compiler_params=pltpu.CompilerParams(dimension_semantics=semantics, vmem_limit_bytes=vmem),
    )


def _dot(a, b, ca, cb, prec=None):
    if a.ndim == 3:
        dims = (((ca + 1,), (cb + 1,)), ((0,), (0,)))
    else:
        dims = (((ca,), (cb,)), ((), ()))
    return lax.dot_general(a, b, dims, preferred_element_type=F32, precision=prec)


def _bdot(a, b, ca=1, cb=0):
    return _dot(a.astype(BF16), b.astype(BF16), ca, cb)


def _hdot(a, b, ca=1, cb=0):
    return _dot(a.astype(F32), b.astype(F32), ca, cb, HI)


def _split(x):
    hi = x.astype(BF16)
    return hi, (x - hi.astype(F32)).astype(BF16)


def _xdot(a, b, ca=1, cb=0, exact=1):
    if exact == 1:
        hi, lo = _split(a)
        e = b.astype(BF16)
        return _dot(hi, e, ca, cb) + _dot(lo, e, ca, cb)
    hi, lo = _split(b)
    e = a.astype(BF16)
    return _dot(e, hi, ca, cb) + _dot(e, lo, ca, cb)


def _sigmoid(x):
    return 0.5 * jnp.tanh(0.5 * x) + 0.5


def _silu(x):
    return x * _sigmoid(x)


def _dsilu(x):
    s = _sigmoid(x)
    return s * (1.0 + x * (1.0 - s))


def _softplus(x):
    return jnp.maximum(x, 0.0) + jnp.log(1.0 + jnp.exp(-jnp.abs(x)))


def _iota2(shape, dim):
    return lax.broadcasted_iota(jnp.int32, shape, dim)


def _rowsum(x):
    return jnp.sum(x, axis=-1, keepdims=True)


def _colsum(x):
    return jnp.sum(x, axis=-2, keepdims=True)


def _full(shape):
    return pl.BlockSpec(shape, lambda *_: (0,) * len(shape))


def _resident(shape):
    return pl.BlockSpec(shape, lambda *_: (0,) * len(shape), pipeline_mode=pl.Buffered(1))


ANY = pl.BlockSpec(memory_space=pl.ANY)


def _norm_mod(x, nw, scale, shift):
    r = lax.rsqrt(jnp.mean(x * x, axis=-1, keepdims=True) + NORM_EPS)
    xn = x * r
    return xn, r, (xn * nw) * (1.0 + scale) + shift


def norm_mm(x, nw, scale, shift, w, *, swiglu, name, tm=512):
    t, d = x.shape
    n = w.shape[1]
    half = n // 2

    def body(x_ref, nw_ref, sc_ref, sh_ref, w_ref, h_ref, y_ref, *act_ref):
        _, _, h = _norm_mod(x_ref[...], nw_ref[...], sc_ref[...], sh_ref[...])
        hb = h.astype(BF16)
        h_ref[...] = hb
        y = _dot(hb, w_ref[...], 1, 0)
        y_ref[...] = y.astype(y_ref.dtype)
        if swiglu:
            act_ref[0][...] = (_silu(y[:, :half]) * y[:, half:]).astype(BF16)

    row = lambda i: (i, 0)
    out_shape = [jax.ShapeDtypeStruct((t, d), BF16), jax.ShapeDtypeStruct((t, n), BF16 if swiglu else F32)]
    out_specs = [pl.BlockSpec((tm, d), row), pl.BlockSpec((tm, n), row)]
    if swiglu:
        out_shape.append(jax.ShapeDtypeStruct((t, half), BF16))
        out_specs.append(pl.BlockSpec((tm, half), row))
    return _call(
        body, name=name, grid=(t // tm,),
        in_specs=[pl.BlockSpec((tm, d), row), _full((1, d)), _full((1, d)), _full((1, d)), _resident((d, n))],
        out_specs=out_specs, out_shape=out_shape, semantics=("parallel",), vmem=BIG_VMEM,
    )(x, nw, scale, shift, w)


def resid_mm(y, w, x, gate, *, name, tm=512):
    t, k = y.shape
    d = w.shape[1]

    def body(y_ref, w_ref, x_ref, g_ref, xo_ref, o_ref):
        o = _dot(y_ref[...].astype(BF16), w_ref[...], 1, 0)
        o_ref[...] = o
        xo_ref[...] = x_ref[...] + g_ref[...] * o

    row = lambda i: (i, 0)
    return _call(
        body, name=name, grid=(t // tm,),
        in_specs=[pl.BlockSpec((tm, k), row), _resident((k, d)), pl.BlockSpec((tm, d), row), _full((1, d))],
        out_specs=[pl.BlockSpec((tm, d), row), pl.BlockSpec((tm, d), row)],
        out_shape=[jax.ShapeDtypeStruct((t, d), F32), jax.ShapeDtypeStruct((t, d), F32)],
        semantics=("parallel",), vmem=BIG_VMEM,
    )(y, w, x, gate)


def resid_mm_bwd(dx, gate, o, w, gu, *, name, tm):
    t, d = dx.shape
    k = w.shape[0]
    swiglu = gu is not None

    def body(dx_ref, g_ref, o_ref, w_ref, *rest):
        if swiglu:
            gu_ref, dy_ref, gx_ref, dg_ref = rest
        else:
            dy_ref, gx_ref, dg_ref = rest
        i = pl.program_id(0)
        dxv = dx_ref[...]
        gx = (dxv * g_ref[...]).astype(BF16)
        gx_ref[...] = gx
        part = _colsum(dxv * o_ref[...])

        @pl.when(i == 0)
        def _():
            dg_ref[...] = jnp.zeros_like(dg_ref)

        dg_ref[...] += part
        da = _dot(gx, w_ref[...], 1, 1)
        if swiglu:
            g = gu_ref[:, :k].astype(F32)
            u = gu_ref[:, k:].astype(F32)
            dy_ref[:, :k] = (da * u * _dsilu(g)).astype(BF16)
            dy_ref[:, k:] = (da * _silu(g)).astype(BF16)
        else:
            dy_ref[...] = da

    row = lambda i: (i, 0)
    in_specs = [pl.BlockSpec((tm, d), row), _full((1, d)), pl.BlockSpec((tm, d), row), _resident((k, d))]
    args = [dx, gate, o, w]
    if swiglu:
        in_specs.append(pl.BlockSpec((tm, 2 * k), row))
        args.append(gu)
        dy_shape = jax.ShapeDtypeStruct((t, 2 * k), BF16)
        dy_spec = pl.BlockSpec((tm, 2 * k), row)
    else:
        dy_shape = jax.ShapeDtypeStruct((t, k), F32)
        dy_spec = pl.BlockSpec((tm, k), row)
    return _call(
        body, name=name, grid=(t // tm,), in_specs=in_specs,
        out_specs=[dy_spec, pl.BlockSpec((tm, d), row), _full((1, d))],
        out_shape=[dy_shape, jax.ShapeDtypeStruct((t, d), BF16), jax.ShapeDtypeStruct((1, d), F32)], vmem=BIG_VMEM,
    )(*args)


def norm_mm_bwd(dy, w, x, nw, scale, dres, *, name, tm=512):
    t, n = dy.shape
    d = x.shape[1]
    steps = t // tm

    def body(dy_ref, w_ref, x_ref, nw_ref, sc_ref, dres_ref, dx_ref, dnw_ref, dsc_ref, dsh_ref):
        i = pl.program_id(0)
        dh = _dot(dy_ref[...].astype(BF16), w_ref[...], 1, 1)
        x = x_ref[...]
        r = lax.rsqrt(jnp.mean(x * x, axis=-1, keepdims=True) + NORM_EPS)
        xn = x * r
        a = nw_ref[...] * (1.0 + sc_ref[...])

        @pl.when(i == 0)
        def _():
            dnw_ref[...] = jnp.zeros_like(dnw_ref)
            dsh_ref[...] = jnp.zeros_like(dsh_ref)

        dnw_ref[...] += _colsum(dh * xn)
        dsh_ref[...] += _colsum(dh)
        dxn = dh * a
        dx_ref[...] = r * (dxn - xn * jnp.mean(dxn * xn, axis=-1, keepdims=True)) + dres_ref[...]

        @pl.when(i == steps - 1)
        def _():
            da = dnw_ref[...]
            dsc_ref[...] = da * nw_ref[...]
            dnw_ref[...] = da * (1.0 + sc_ref[...])

    row = lambda i: (i, 0)
    vec = jax.ShapeDtypeStruct((1, d), F32)
    return _call(
        body, name=name, grid=(steps,),
        in_specs=[pl.BlockSpec((tm, n), row), _resident((d, n)), pl.BlockSpec((tm, d), row), _full((1, d)),
                  _full((1, d)), pl.BlockSpec((tm, d), row)],
        out_specs=[pl.BlockSpec((tm, d), row), _full((1, d)), _full((1, d)), _full((1, d))],
        out_shape=[jax.ShapeDtypeStruct((t, d), F32), vec, vec, vec], vmem=BIG_VMEM,
    )(dy, w, x, nw, scale, dres)


def _pick_tn(n, k, budget=6 << 20):
    best = LANE
    for m in range(1, n // LANE + 1):
        tn = m * LANE
        if n % tn == 0 and k * tn * 4 <= budget:
            best = tn
    return best


def mm_tn(a, g, *, name, tt=2048):
    t, k = a.shape
    n = g.shape[1]
    tn = _pick_tn(n, k)
    tt = min(tt, t)
    steps = t // tt

    def body(a_ref, g_ref, o_ref, b_ref):
        i = pl.program_id(1)

        @pl.when(i == 0)
        def _():
            o_ref[...] = jnp.zeros_like(o_ref)

        o_ref[...] += _dot(a_ref[...].astype(BF16), g_ref[...].astype(BF16), 0, 0)

        @pl.when(i == steps - 1)
        def _():
            b_ref[...] = o_ref[...].astype(BF16)

    out = pl.BlockSpec((k, tn), lambda j, i: (0, j))
    return _call(
        body, name=name, grid=(n // tn, steps),
        in_specs=[pl.BlockSpec((tt, k), lambda j, i: (i, 0)), pl.BlockSpec((tt, tn), lambda j, i: (i, j))],
        out_specs=[out, out],
        out_shape=[jax.ShapeDtypeStruct((k, n), F32), jax.ShapeDtypeStruct((k, n), BF16)],
        semantics=("parallel", "arbitrary"), vmem=BIG_VMEM,
    )(a, g)


def loss_head(y, target, *, tm=512):
    t, d = y.shape
    steps = t // tm

    def body(y_ref, t_ref, dy_ref, l_ref, acc_ref):
        i = pl.program_id(0)

        @pl.when(i == 0)
        def _():
            acc_ref[...] = jnp.zeros_like(acc_ref)

        e = y_ref[...] - t_ref[...]
        dy_ref[...] = e * (1.0 / d)
        acc_ref[...] += _colsum(e * e)

        @pl.when(i == steps - 1)
        def _():
            tot = jnp.sum(acc_ref[...], axis=-1, keepdims=True) * (0.5 / d)
            l_ref[...] = jnp.broadcast_to(tot, l_ref.shape)

    row = lambda i: (i, 0)
    return _call(
        body, name="loss_head", grid=(steps,),
        in_specs=[pl.BlockSpec((tm, d), row), pl.BlockSpec((tm, d), row)],
        out_specs=[pl.BlockSpec((tm, d), row), _full((8, LANE))],
        out_shape=[jax.ShapeDtypeStruct((t, d), F32), jax.ShapeDtypeStruct((8, LANE), F32)],
        scratch_shapes=[pltpu.VMEM((1, d), F32)],
    )(y, target)


def _shift_rows(x, s):
    if s == 0:
        return x
    t = x.shape[0]
    ri = _iota2(x.shape, 0)
    rolled = pltpu.roll(x, s % t, axis=0)
    if s > 0:
        return jnp.where(ri >= s, rolled, 0.0)
    return jnp.where(ri < t + s, rolled, 0.0)


def _conv_pre(x, w):
    acc = x * w[DN_CONV - 1:DN_CONV, :]
    for j in range(DN_CONV - 1):
        acc = acc + _shift_rows(x, DN_CONV - 1 - j) * w[j:j + 1, :]
    return acc


def dn_conv(proj, conv_w):
    t = proj.shape[0]
    width = 3 * DN_WIDTH

    def body(x_ref, w_ref, o_ref):
        o_ref[...] = _silu(_conv_pre(x_ref[...], w_ref[...]))

    col = lambda j: (0, j)
    return _call(
        body, name="dn_conv", grid=(width // LANE,),
        in_specs=[pl.BlockSpec((t, LANE), col), pl.BlockSpec((8, LANE), col)],
        out_specs=pl.BlockSpec((t, LANE), col),
        out_shape=jax.ShapeDtypeStruct((t, width), F32), semantics=("parallel",),
    )(proj, conv_w)


def dn_conv_bwd(proj, conv_w, dact, dproj):
    t = proj.shape[0]
    width = 3 * DN_WIDTH

    def body(x_ref, w_ref, d_ref, _, dx_ref, dw_ref):
        x = x_ref[...]
        w = w_ref[...]
        dc = d_ref[...] * _dsilu(_conv_pre(x, w))
        dx = dc * w[DN_CONV - 1:DN_CONV, :]
        rows = []
        for j in range(DN_CONV - 1):
            s = DN_CONV - 1 - j
            up = _shift_rows(dc, -s)
            dx = dx + up * w[j:j + 1, :]
            rows.append(_colsum(up * x))
        rows.append(_colsum(dc * x))
        dx_ref[...] = dx.astype(BF16)
        ri = _iota2((8, LANE), 0)
        dw = jnp.zeros((8, LANE), F32)
        for j in range(DN_CONV):
            dw = dw + jnp.where(ri == j, rows[j], 0.0)
        dw_ref[...] = dw

    col = lambda j: (0, j)
    return _call(
        body, name="dn_conv_bwd", grid=(width // LANE,),
        in_specs=[pl.BlockSpec((t, LANE), col), pl.BlockSpec((8, LANE), col), pl.BlockSpec((t, LANE), col), ANY],
        out_specs=[pl.BlockSpec((t, LANE), col), pl.BlockSpec((8, LANE), col)],
        out_shape=[jax.ShapeDtypeStruct(dproj.shape, dproj.dtype), jax.ShapeDtypeStruct((8, width), F32)],
        semantics=("parallel",), aliases={3: 0},
    )(proj, conv_w, dact, dproj)


def _t(x):
    return jnp.swapaxes(x, -1, -2)


def _inv_unit_lower(a):
    c = a.shape[-1]
    eye = (_iota2((c, c), 0) == _iota2((c, c), 1)).astype(F32)
    x = eye - a
    p = _hdot(a, a)
    steps = int(math.log2(c)) - 1
    for i in range(steps):
        x = x + _hdot(x, p)
        if i < steps - 1:
            p = _hdot(p, p)
    return x


def _dn_local(q, k, v, a, b, alog, dtb, tinv=None):
    nh, c, d = q.shape
    rq = lax.rsqrt(_rowsum(q * q) + NORM_EPS)
    rk = lax.rsqrt(_rowsum(k * k) + NORM_EPS)
    qh = q * rq
    kn = k * rk
    qs = qh * (d ** -0.5)
    g = -jnp.exp(alog) * _softplus(a + dtb)
    beta = _sigmoid(b)
    ri = _iota2((c, c), 0)
    ci = _iota2((c, c), 1)
    causal = ri >= ci
    strict = ri > ci
    gb = jnp.broadcast_to(g, (nh, c, d))
    gcb = _xdot(jnp.broadcast_to(causal.astype(F32), (nh, c, c)), gb, exact=0)
    gc = gcb[..., :1]
    gl = _colsum(gb)[..., :1]
    dec = jnp.exp(jnp.where(causal, gc - _t(gcb)[:, :c, :], -1e30))
    kb = kn * beta
    amat = jnp.where(strict, _bdot(kb, kn, 1, 1) * dec, 0.0)
    if tinv is None:
        tinv = _inv_unit_lower(amat)
    e = jnp.exp(gc)
    f = jnp.exp(gl - gc)
    rw = kb * e
    sol = _hdot(tinv, jnp.concatenate([v * beta, rw], axis=-1))
    pmat = jnp.where(causal, _bdot(qs, kn, 1, 1) * dec, 0.0)
    return dict(rq=rq, rk=rk, qh=qh, kn=kn, qs=qs, g=g, beta=beta, causal=causal, strict=strict, gl=gl,
                dec=dec, kb=kb, amat=amat, tinv=tinv, e=e, f=f, rw=rw, u=sol[..., :d], w=sol[..., d:], pmat=pmat,
                qd=qs * e, kd=kn * f)


_DN_FIELDS = ("u", "w", "qd", "kd", "pmat", "gl")


def _dn_state(m, s_in):
    vnew = m["u"] - _bdot(m["w"], s_in)
    o = _bdot(m["qd"], s_in) + _bdot(m["pmat"], vnew)
    return vnew, o, s_in * jnp.exp(m["gl"]) + _bdot(m["kd"], vnew, 0, 0)


def _dn_state_bwd(m, s_in, do, ds_out):
    el = jnp.exp(m["gl"])
    dvnew = _bdot(m["pmat"], do, 0, 0) + _bdot(m["kd"], ds_out)
    dkd = _bdot(m["vnew"], ds_out, 1, 1)
    ds_in = _bdot(m["qd"], do, 0, 0) + el * ds_out - _bdot(m["w"], dvnew, 0, 0)
    dgl = el * _colsum(_rowsum(s_in * ds_out))
    return dvnew, dkd, dgl, ds_in


def _dn_local_bwd(m, q, v, a, alog, dtb, s_in, vnew, do, dvnew, dkd, dgl):
    nh, c, d = q.shape
    kn, qs, kb, u, w, e, f = m["kn"], m["qs"], m["kb"], m["u"], m["w"], m["e"], m["f"]
    beta, dec, tinv, kd, qd = m["beta"], m["dec"], m["tinv"], m["kd"], m["qd"]
    dp = jnp.where(m["causal"], _bdot(do, vnew, 1, 1), 0.0)
    dqd = _bdot(do, s_in, 1, 1)
    dw = -_bdot(dvnew, s_in, 1, 1)
    dsol = _hdot(tinv, jnp.concatenate([dvnew, dw], axis=-1), 0, 0)
    dru = dsol[..., :d]
    drw = dsol[..., d:]
    da_m = -jnp.where(m["strict"], _bdot(dsol, jnp.concatenate([u, w], axis=-1), 1, 1), 0.0)
    db_m = da_m * dec
    dq_m = dp * dec
    dkb = _bdot(db_m, kn)
    dkn = _bdot(db_m, kb, 0, 0) + _bdot(dq_m, qs, 0, 0)
    dqs = _bdot(dq_m, kn)
    gmat = da_m * m["amat"] + dp * m["pmat"]
    ones = jnp.ones((nh, c, d), F32)
    dgam = (_xdot(gmat, ones) - _xdot(gmat, ones, 0, 0))[..., :1]
    dqs = dqs + dqd * e
    dgam = dgam + _rowsum(dqd * qd)
    dkn = dkn + dkd * f
    tk = _rowsum(dkd * kd)
    dgam = dgam - tk
    dgl = dgl + _colsum(tk)
    dkb = dkb + drw * e
    dgam = dgam + _rowsum(drw * m["rw"])
    dv = dru * beta
    dbeta = _rowsum(dru * v) + _rowsum(dkb * kn)
    dkn = dkn + dkb * beta
    last = (_iota2((c, 1), 0) == c - 1).astype(F32)
    dgam = dgam + last * dgl
    upper = (_iota2((c, c), 0) <= _iota2((c, c), 1)).astype(F32)
    dg = _xdot(jnp.broadcast_to(upper, (nh, c, c)), jnp.broadcast_to(dgam, (nh, c, d)), exact=0)[..., :1]
    dqh = dqs * (d ** -0.5)
    dq = m["rq"] * (dqh - m["qh"] * _rowsum(dqh * m["qh"]))
    dk = m["rk"] * (dkn - kn * _rowsum(dkn * kn))
    sg = _sigmoid(a + dtb)
    da = dg * (-jnp.exp(alog)) * sg
    dalog = _colsum(dg * m["g"])
    ddtb = _colsum(da)
    db = dbeta * beta * (1.0 - beta)
    return dq, dk, dv, da, db, dalog, ddtb


def _dn_gate(o, z, wn):
    ro = lax.rsqrt(jnp.mean(o * o, axis=-1, keepdims=True) + NORM_EPS)
    n = o * ro
    return n, ro, n * wn * _silu(z)


DN_PAIR = 4


def _heads(ref, col0):
    d = DN_HEAD_DIM
    return jnp.stack([ref[j * DN_CHUNK:(j + 1) * DN_CHUNK, col0 + h * d:col0 + (h + 1) * d]
                      for j in range(DN_PAIR) for h in range(DN_HEADS)])


def _dn_inputs(act_ref, ab_ref, sc_ref):
    ab = ab_ref[...]
    sc = sc_ref[...]
    rows = lambda j: slice(j * DN_CHUNK, (j + 1) * DN_CHUNK)
    both = [(j, h) for j in range(DN_PAIR) for h in range(DN_HEADS)]
    q = _heads(act_ref, 0)
    k = _heads(act_ref, DN_WIDTH)
    v = _heads(act_ref, 2 * DN_WIDTH)
    a = jnp.stack([ab[rows(j), h:h + 1] for j, h in both])
    b = jnp.stack([ab[rows(j), DN_HEADS + h:DN_HEADS + h + 1] for j, h in both])
    alog = jnp.stack([sc[0:1, h:h + 1] for _, h in both])
    dtb = jnp.stack([sc[1:2, h:h + 1] for _, h in both])
    return q, k, v, a, b, alog, dtb


def _chunk_of(m, j, fields):
    return {f: m[f][j * DN_HEADS:(j + 1) * DN_HEADS] for f in fields}


def dn_fwd(act, proj, scal, wn):
    t = act.shape[0]
    n = t // DN_CHUNK
    d = DN_HEAD_DIM
    rows = DN_PAIR * DN_CHUNK

    def body(act_ref, z_ref, ab_ref, sc_ref, wn_ref, y_ref, st_ref, ti_ref, s_ref):
        @pl.when(pl.program_id(0) == 0)
        def _():
            s_ref[...] = jnp.zeros_like(s_ref)

        m = _dn_local(*_dn_inputs(act_ref, ab_ref, sc_ref))
        s = s_ref[...]
        outs = []
        for j in range(DN_PAIR):
            st_ref[j] = s
            ti_ref[j] = m["tinv"][j * DN_HEADS:(j + 1) * DN_HEADS]
            _, o, s = _dn_state(_chunk_of(m, j, _DN_FIELDS), s)
            outs.append(o)
        s_ref[...] = s
        y = _dn_gate(jnp.concatenate(outs, axis=0), _heads(z_ref, 0), wn_ref[...])[2]
        for j in range(DN_PAIR):
            for h in range(DN_HEADS):
                y_ref[j * DN_CHUNK:(j + 1) * DN_CHUNK, h * d:(h + 1) * d] = y[j * DN_HEADS + h]

    return _call(
        body, name="dn_fwd", grid=(n // DN_PAIR,),
        in_specs=[pl.BlockSpec((rows, 3 * DN_WIDTH), lambda i: (i, 0)),
                  pl.BlockSpec((rows, DN_WIDTH), lambda i: (i, C_Z // DN_WIDTH)),
                  pl.BlockSpec((rows, LANE), lambda i: (i, C_AB // LANE)),
                  _full((8, LANE)), _full((1, d))],
        out_specs=[pl.BlockSpec((rows, DN_WIDTH), lambda i: (i, 0)),
                   pl.BlockSpec((DN_PAIR, DN_HEADS, d, d), lambda i: (i, 0, 0, 0)),
                   pl.BlockSpec((DN_PAIR, DN_HEADS, DN_CHUNK, DN_CHUNK), lambda i: (i, 0, 0, 0))],
        out_shape=[jax.ShapeDtypeStruct((t, MIX_WIDTH), F32), jax.ShapeDtypeStruct((n, DN_HEADS, d, d), F32),
                   jax.ShapeDtypeStruct((n, DN_HEADS, DN_CHUNK, DN_CHUNK), F32)],
        scratch_shapes=[pltpu.VMEM((DN_HEADS, d, d), F32)],
    )(act, proj, proj, scal, wn)


def dn_bwd(act, proj, scal, wn, states, tinvs, dy):
    t = act.shape[0]
    n = t // DN_CHUNK
    steps = n // DN_PAIR
    d = DN_HEAD_DIM
    zab = DN_WIDTH + AB_PAD
    rows = DN_PAIR * DN_CHUNK

    def body(act_ref, z_ref, ab_ref, sc_ref, wn_ref, st_ref, ti_ref, dy_ref, dact_ref, dzab_ref, dpar_ref, ds_ref):
        @pl.when(pl.program_id(0) == 0)
        def _():
            ds_ref[...] = jnp.zeros_like(ds_ref)
            dpar_ref[...] = jnp.zeros_like(dpar_ref)

        wnv = wn_ref[...]
        q, k, v, a, b, alog, dtb = _dn_inputs(act_ref, ab_ref, sc_ref)
        batch = (DN_PAIR * DN_HEADS,)
        s_in = st_ref[...].reshape(batch + (d, d))
        m = _dn_local(q, k, v, a, b, alog, dtb, ti_ref[...].reshape(batch + (DN_CHUNK, DN_CHUNK)))
        vnew, o, _ = _dn_state(m, s_in)
        z = _heads(z_ref, 0)
        dyh = _heads(dy_ref, 0)
        nrm, ro, _ = _dn_gate(o, z, wnv)
        sz = _silu(z)
        dz = dyh * nrm * wnv * _dsilu(z)
        dn = dyh * wnv * sz
        dwn = _colsum(dyh * nrm * sz)
        do = ro * (dn - nrm * jnp.mean(dn * nrm, axis=-1, keepdims=True))
        ds = ds_ref[...]
        parts = [None] * DN_PAIR
        for j in reversed(range(DN_PAIR)):
            mj = dict(_chunk_of(m, j, _DN_FIELDS), vnew=vnew[j * DN_HEADS:(j + 1) * DN_HEADS])
            dvnew, dkd, dgl, ds = _dn_state_bwd(mj, s_in[j * DN_HEADS:(j + 1) * DN_HEADS],
                                                do[j * DN_HEADS:(j + 1) * DN_HEADS], ds)
            parts[j] = (dvnew, dkd, dgl)
        ds_ref[...] = ds
        dvnew, dkd, dgl = (jnp.concatenate([p[i] for p in parts], axis=0) for i in range(3))
        dq, dk, dv, da, db, dalog, ddtb = _dn_local_bwd(m, q, v, a, alog, dtb, s_in, vnew, do, dvnew, dkd, dgl)
        lane = _iota2((DN_CHUNK, LANE), 1)
        prow = _iota2((8, LANE), 0)
        plane = _iota2((8, LANE), 1)
        dpar = jnp.zeros((8, LANE), F32)
        for j in range(DN_PAIR):
            rs = slice(j * DN_CHUNK, (j + 1) * DN_CHUNK)
            dab = jnp.zeros((DN_CHUNK, LANE), F32)
            for h in range(DN_HEADS):
                n_ = j * DN_HEADS + h
                dzab_ref[rs, h * d:(h + 1) * d] = dz[n_].astype(BF16)
                dact_ref[rs, h * d:(h + 1) * d] = dq[n_]
                dact_ref[rs, DN_WIDTH + h * d:DN_WIDTH + (h + 1) * d] = dk[n_]
                dact_ref[rs, 2 * DN_WIDTH + h * d:2 * DN_WIDTH + (h + 1) * d] = dv[n_]
                dab = dab + jnp.where(lane == h, da[n_], 0.0) + jnp.where(lane == DN_HEADS + h, db[n_], 0.0)
                dpar = dpar + jnp.where((prow == 0) & (plane == h), dalog[n_], 0.0)
                dpar = dpar + jnp.where((prow == 1) & (plane == h), ddtb[n_], 0.0)
                dpar = dpar + jnp.where(prow == 2, dwn[n_], 0.0)
            dzab_ref[rs, DN_WIDTH:DN_WIDTH + LANE] = dab.astype(BF16)
            dzab_ref[rs, DN_WIDTH + LANE:] = jnp.zeros((DN_CHUNK, AB_PAD - LANE), BF16)
        dpar_ref[...] += dpar

    rev = lambda i: (steps - 1 - i, 0)
    rev4 = lambda i: (steps - 1 - i, 0, 0, 0)
    return _call(
        body, name="dn_bwd", grid=(steps,),
        in_specs=[pl.BlockSpec((rows, 3 * DN_WIDTH), rev),
                  pl.BlockSpec((rows, DN_WIDTH), lambda i: (steps - 1 - i, C_Z // DN_WIDTH)),
                  pl.BlockSpec((rows, LANE), lambda i: (steps - 1 - i, C_AB // LANE)),
                  _full((8, LANE)), _full((1, d)),
                  pl.BlockSpec((DN_PAIR, DN_HEADS, d, d), rev4),
                  pl.BlockSpec((DN_PAIR, DN_HEADS, DN_CHUNK, DN_CHUNK), rev4),
                  pl.BlockSpec((rows, DN_WIDTH), rev)],
        out_specs=[pl.BlockSpec((rows, 3 * DN_WIDTH), rev),
                   pl.BlockSpec((rows, zab), lambda i: (steps - 1 - i, C_Z // zab)), _full((8, LANE))],
        out_shape=[jax.ShapeDtypeStruct((t, 3 * DN_WIDTH), F32), jax.ShapeDtypeStruct((t, IN_PAD), BF16),
                   jax.ShapeDtypeStruct((8, LANE), F32)],
        scratch_shapes=[pltpu.VMEM((DN_HEADS, d, d), F32)],
    )(act, proj, proj, scal, wn, states, tinvs, dy)


_INV_SQRT2 = 0.7071067811865476
_INV_SQRT2PI = 0.3989422804014327


def _gelu_parts(x):
    half = 0.5 * (1.0 + lax.erf(x * _INV_SQRT2))
    return x * half, half + x * jnp.exp(-0.5 * x * x) * _INV_SQRT2PI


def _gm_core(uv, lng, lnb, ws_ref, bst):
    c = uv.shape[0]
    zz, dgelu = _gelu_parts(uv)
    u = zz[:, :GM_WIDTH]
    vv = zz[:, GM_WIDTH:]
    xc = vv - jnp.mean(vv, axis=-1, keepdims=True)
    rs = lax.rsqrt(jnp.mean(xc * xc, axis=-1, keepdims=True) + NORM_EPS)
    xh = xc * rs
    vn = xh * lng + lnb
    grp = _iota2((c, GM_WIDTH), 1) // GM_GROUP_DIM
    tril = _iota2((c, c), 0) >= _iota2((c, c), 1)
    sv = jnp.zeros((c, GM_WIDTH), F32)
    masks = []
    for g in range(GM_GROUPS):
        mk = grp == g
        masks.append(mk)
        ws = jnp.where(tril, ws_ref[g], 0.0)
        sv = sv + _bdot(ws, jnp.where(mk, vn, 0.0)) + jnp.where(mk, bst[:, g:g + 1], 0.0)
    return u, xh, rs, vn, sv, masks, tril, dgelu


def gm_fwd(proj, lng, lnb, w_s, bst, ybuf):
    t = proj.shape[0]

    def body(uv_ref, g_ref, b_ref, ws_ref, bst_ref, _, y_ref):
        u, _, _, _, sv, _, _, _ = _gm_core(uv_ref[...], g_ref[...], b_ref[...], ws_ref, bst_ref[...])
        y_ref[...] = u * sv

    return _call(
        body, name="gm_fwd", grid=(t // GM_CHUNK,),
        in_specs=[pl.BlockSpec((GM_CHUNK, 2 * GM_WIDTH), lambda i: (i, C_UV // (2 * GM_WIDTH))),
                  _full((1, GM_WIDTH)), _full((1, GM_WIDTH)), _full((GM_GROUPS, GM_CHUNK, GM_CHUNK)),
                  _full((GM_CHUNK, LANE)), ANY],
        out_specs=pl.BlockSpec((GM_CHUNK, GM_WIDTH), lambda i: (i, DN_WIDTH // GM_WIDTH)),
        out_shape=jax.ShapeDtypeStruct(ybuf.shape, F32), semantics=("parallel",), aliases={5: 0},
    )(proj, lng, lnb, w_s, bst, ybuf)


def gm_bwd(proj, lng, lnb, w_s, bst, dy, dproj):
    t = proj.shape[0]

    def body(uv_ref, g_ref, b_ref, ws_ref, bst_ref, dy_ref, _, duv_ref, dws_ref, dbst_ref, dln_ref):
        @pl.when(pl.program_id(0) == 0)
        def _():
            dws_ref[...] = jnp.zeros_like(dws_ref)
            dbst_ref[...] = jnp.zeros_like(dbst_ref)
            dln_ref[...] = jnp.zeros_like(dln_ref)

        uv = uv_ref[...]
        lng = g_ref[...]
        u, xh, rs, vn, sv, masks, tril, dg = _gm_core(uv, lng, b_ref[...], ws_ref, bst_ref[...])
        dyv = dy_ref[...]
        dsv = dyv * u
        lane = _iota2((GM_CHUNK, LANE), 1)
        dvn = jnp.zeros_like(dsv)
        dbst = jnp.zeros((GM_CHUNK, LANE), F32)
        for g in range(GM_GROUPS):
            ws = jnp.where(tril, ws_ref[g], 0.0)
            dsg = jnp.where(masks[g], dsv, 0.0)
            dvn = dvn + jnp.where(masks[g], _bdot(ws, dsv, 0, 0), 0.0)
            dws_ref[g] += jnp.where(tril, _bdot(dsg, vn, 1, 1), 0.0)
            dbst = dbst + jnp.where(lane == g, _rowsum(dsg), 0.0)
        dbst_ref[...] += dbst
        row = _iota2((8, GM_WIDTH), 0)
        dln_ref[...] += jnp.where(row == 0, _colsum(dvn * xh), 0.0) + jnp.where(row == 1, _colsum(dvn), 0.0)
        dxh = dvn * lng
        dvv = rs * (dxh - jnp.mean(dxh, axis=-1, keepdims=True) - xh * jnp.mean(dxh * xh, axis=-1, keepdims=True))
        duv_ref[:, :GM_WIDTH] = (dyv * sv * dg[:, :GM_WIDTH]).astype(BF16)
        duv_ref[:, GM_WIDTH:] = (dvv * dg[:, GM_WIDTH:]).astype(BF16)

    return _call(
        body, name="gm_bwd", grid=(t // GM_CHUNK,),
        in_specs=[pl.BlockSpec((GM_CHUNK, 2 * GM_WIDTH), lambda i: (i, C_UV // (2 * GM_WIDTH))),
                  _full((1, GM_WIDTH)), _full((1, GM_WIDTH)), _full((GM_GROUPS, GM_CHUNK, GM_CHUNK)),
                  _full((GM_CHUNK, LANE)),
                  pl.BlockSpec((GM_CHUNK, GM_WIDTH), lambda i: (i, DN_WIDTH // GM_WIDTH)), ANY],
        out_specs=[pl.BlockSpec((GM_CHUNK, 2 * GM_WIDTH), lambda i: (i, C_UV // (2 * GM_WIDTH))),
                   _full((GM_GROUPS, GM_CHUNK, GM_CHUNK)), _full((GM_CHUNK, LANE)), _full((8, GM_WIDTH))],
        out_shape=[jax.ShapeDtypeStruct(dproj.shape, dproj.dtype),
                   jax.ShapeDtypeStruct((GM_GROUPS, GM_CHUNK, GM_CHUNK), F32),
                   jax.ShapeDtypeStruct((GM_CHUNK, LANE), F32), jax.ShapeDtypeStruct((8, GM_WIDTH), F32)],
        aliases={6: 0},
    )(proj, lng, lnb, w_s, bst, dy, dproj)


def _head_mats():
    r = _iota2((SW_WIDTH, SW_WIDTH), 0)
    c = _iota2((SW_WIDTH, SW_WIDTH), 1)
    same = (r // SW_HEAD_DIM) == (c // SW_HEAD_DIM)
    cc = c % SW_HEAD_DIM
    half = ROPE_DIM // 2
    rot = jnp.where((cc < half) & (r == c + half), -1.0, 0.0) + jnp.where((cc >= half) & (cc < ROPE_DIM) & (r == c - half), 1.0, 0.0)
    return same.astype(F32), rot


def _seg_col(s):
    return C_SW // SW_WIDTH + (s // 2) * 3 + s % 2


def _halves(x):
    return x[:, :LANE], x[:, LANE:]


def sw_prep(proj, nw2, cos_t, sin_t, *, tm=512):
    t = proj.shape[0]

    def body(x_ref, w_ref, c_ref, s_ref, o_ref):
        same, rot = _head_mats()
        x = x_ref[...]
        r = lax.rsqrt(_xdot(x * x, same) * (1.0 / SW_HEAD_DIM) + NORM_EPS)
        xn = x * r * w_ref[0]
        o_ref[0, 0], o_ref[0, 1] = _halves(xn * c_ref[...] + _xdot(xn, rot) * s_ref[...])

    return _call(
        body, name="sw_prep", grid=(6, t // tm),
        in_specs=[pl.BlockSpec((tm, SW_WIDTH), lambda s, i: (i, _seg_col(s))),
                  pl.BlockSpec((1, 1, SW_WIDTH), lambda s, i: (s % 2, 0, 0)),
                  pl.BlockSpec((tm, SW_WIDTH), lambda s, i: (i, 0)),
                  pl.BlockSpec((tm, SW_WIDTH), lambda s, i: (i, 0))],
        out_specs=pl.BlockSpec((1, 2, tm, LANE), lambda s, i: (s, 0, i, 0)),
        out_shape=jax.ShapeDtypeStruct((6, 2, t, LANE), F32), semantics=("parallel", "parallel"),
    )(proj, nw2, cos_t, sin_t)


def sw_prep_bwd(proj, nw2, cos_t, sin_t, dkvq, dproj, dnw, p, *, tm=512):
    t = proj.shape[0]
    col0 = C_SW // SW_WIDTH + 3 * p
    seg_col = lambda s: col0 + (s + 1) % 3

    def body(x_ref, w_ref, c_ref, s_ref, d_ref, _, dw0_ref, dx_ref, dw_ref):
        s = pl.program_id(0)
        dout = jnp.concatenate([d_ref[0, 0], d_ref[0, 1]], axis=1)

        @pl.when(s == 1)
        def _():
            dx_ref[...] = dout.astype(BF16)

        @pl.when((s != 1) & (pl.program_id(1) == 0))
        def _():
            dw_ref[...] = dw0_ref[...]

        @pl.when(s != 1)
        def _():
            same, rot = _head_mats()
            x = x_ref[...]
            w = w_ref[0]
            r = lax.rsqrt(_xdot(x * x, same) * (1.0 / SW_HEAD_DIM) + NORM_EPS)
            xh = x * r
            dxn = dout * c_ref[...] + _xdot(dout * s_ref[...], rot, 1, 1)
            dw_ref[0] += _colsum(dxn * xh)
            dxh = dxn * w
            dx_ref[...] = (r * (dxh - xh * (_xdot(dxh * xh, same) * (1.0 / SW_HEAD_DIM)))).astype(BF16)

    return _call(
        body, name=f"sw_prep_bwd{p}", grid=(3, t // tm),
        in_specs=[pl.BlockSpec((tm, SW_WIDTH), lambda s, i: (i, seg_col(s))),
                  pl.BlockSpec((1, 1, SW_WIDTH), lambda s, i: (1 - s // 2, 0, 0)),
                  pl.BlockSpec((tm, SW_WIDTH), lambda s, i: (i, 0)),
                  pl.BlockSpec((tm, SW_WIDTH), lambda s, i: (i, 0)),
                  pl.BlockSpec((1, 2, tm, LANE), lambda s, i: (s, 0, i, 0)), ANY,
                  pl.BlockSpec((1, 1, SW_WIDTH), lambda s, i: (s // 2, 0, 0))],
        out_specs=[pl.BlockSpec((tm, SW_WIDTH), lambda s, i: (i, seg_col(s))),
                   pl.BlockSpec((1, 1, SW_WIDTH), lambda s, i: (s // 2, 0, 0))],
        out_shape=[jax.ShapeDtypeStruct(dproj.shape, dproj.dtype), jax.ShapeDtypeStruct((2, 1, SW_WIDTH), F32)],
        semantics=("arbitrary", "arbitrary"), aliases={5: 0},
    )(proj, nw2, cos_t, sin_t, dkvq, dproj, dnw)


_SW_SCALE = SW_HEAD_DIM ** -0.5
_NEG = -1e30


def _sw_masks(has_other):
    ri = _iota2((SW_BLOCK, SW_BLOCK), 0)
    ci = _iota2((SW_BLOCK, SW_BLOCK), 1)
    return ri >= ci, (ci >= ri) & has_other


def _pair(x):
    first = _iota2((1, LANE), 1) < SW_HEAD_DIM
    return jnp.stack([jnp.where(first, x, 0.0), jnp.where(first, 0.0, x)])


def _both(x):
    return jnp.broadcast_to(x.astype(BF16)[None], (2,) + x.shape)


def _unpair(x2):
    first = _iota2((1, LANE), 1) < SW_HEAD_DIM
    return jnp.where(first, x2[0], x2[1])


def _head_cols(x):
    return jnp.stack([x[:, 0:1], x[:, SW_HEAD_DIM:SW_HEAD_DIM + 1]])


SW_GROUP = 16


def _sw_geometry(t, p):
    dil = SW_DILATIONS[p]
    unit = SW_BLOCK * dil
    nb = max(1, SW_GROUP // dil)
    return dil, unit, nb, t // (unit * nb)


def _sw_groups(dil, nb, body):
    if nb * dil == SW_GROUP:
        body([(k // dil, k % dil) for k in range(SW_GROUP)])
    else:
        for g in range(nb * dil // SW_GROUP):
            body([(0, SW_GROUP * g + k) for k in range(SW_GROUP)])


def _sw_rows(i, r, dil):
    start = i * SW_BLOCK * dil + r
    return pl.ds(start, SW_BLOCK) if dil == 1 else pl.ds(start, SW_BLOCK, stride=dil)


def _sw_load(refs, probs, dil, shift, wrap, fn):
    out = []
    for i, r in probs:
        if shift != 0 and i == wrap:
            out.append(fn(refs[1][_sw_rows(0, r, dil), :]))
        else:
            out.append(fn(refs[0][_sw_rows(i + shift, r, dil), :]))
    return jnp.concatenate(out, axis=0)


def _sw_other_masks(probs, wrap, edge_ok):
    _, other = _sw_masks(edge_ok)
    _, always = _sw_masks(True)
    return jnp.stack([other if i == wrap else always for i, _ in probs for _ in range(2)])


def sw_attn(qk, proj, p):
    t = proj.shape[0]
    dil, unit, nb, nsp = _sw_geometry(t, p)
    vcol = (C_SW + 3 * SW_WIDTH * p + 2 * SW_WIDTH) // LANE

    def body(q_ref, kc_ref, kp_ref, vc_ref, vp_ref, o_ref, l_ref):
        mc, _ = _sw_masks(True)
        first = pl.program_id(1) != 0
        q_r, k_r, v_r = (q_ref.at[0, 0], None), (kc_ref.at[0, 0], kp_ref.at[0, 0]), (vc_ref, vp_ref)

        def one(probs):
            mp = _sw_other_masks(probs, 0, first)
            q2 = _sw_load(q_r, probs, dil, 0, 0, _pair)
            sc = jnp.where(mc, _bdot(q2, _sw_load(k_r, probs, dil, 0, 0, _both), 1, 1) * _SW_SCALE, _NEG)
            sp = jnp.where(mp, _bdot(q2, _sw_load(k_r, probs, dil, -1, 0, _both), 1, 1) * _SW_SCALE, _NEG)
            mx = jnp.maximum(jnp.max(sc, axis=-1, keepdims=True), jnp.max(sp, axis=-1, keepdims=True))
            pc = jnp.exp(sc - mx)
            pp = jnp.exp(sp - mx)
            den = _rowsum(pc) + _rowsum(pp)
            o2 = (_bdot(pc, _sw_load(v_r, probs, dil, 0, 0, _both))
                  + _bdot(pp, _sw_load(v_r, probs, dil, -1, 0, _both))) * (1.0 / den)
            l2 = jnp.broadcast_to(mx + jnp.log(den), o2.shape)
            for n, (i, r) in enumerate(probs):
                o_ref.at[0][_sw_rows(i, r, dil), :] = _unpair(o2[2 * n:2 * n + 2])
                l_ref.at[0][_sw_rows(i, r, dil), :] = _unpair(l2[2 * n:2 * n + 2])

        _sw_groups(dil, nb, one)

    before = lambda j: jnp.maximum(j * nb - 1, 0)
    seg = lambda s: pl.BlockSpec((1, 1, unit * nb, LANE), lambda h, j: (s, h, j, 0))
    seg_b = lambda s: pl.BlockSpec((1, 1, unit, LANE), lambda h, j: (s, h, before(j), 0))
    out = pl.BlockSpec((1, unit * nb, LANE), lambda h, j: (h, j, 0))
    shp = jax.ShapeDtypeStruct((2, t, LANE), F32)
    return _call(
        body, name=f"sw_attn{p}", grid=(2, nsp),
        in_specs=[seg(2 * p), seg(2 * p + 1), seg_b(2 * p + 1),
                  pl.BlockSpec((unit * nb, LANE), lambda h, j: (j, vcol + h)),
                  pl.BlockSpec((unit, LANE), lambda h, j: (before(j), vcol + h))],
        out_specs=[out, out], out_shape=[shp, shp], semantics=("parallel", "parallel"),
    )(qk, qk, qk, proj, proj)


def sw_attn_dkv(qk, proj, dy, lg, dm, p):
    t = proj.shape[0]
    dil, unit, nb, nsp = _sw_geometry(t, p)
    nunits = t // unit
    vcol = (C_SW + 3 * SW_WIDTH * p + 2 * SW_WIDTH) // LANE
    ycol = (DN_WIDTH + GM_WIDTH) // LANE

    def body(k_ref, v_ref, qc_ref, qn_ref, doc_ref, don_ref, lc_ref, ln_ref, dc_ref, dn_ref, o_ref):
        mc, _ = _sw_masks(True)
        more = pl.program_id(1) + 1 < nsp
        q_r, do_r = (qc_ref.at[0, 0], qn_ref.at[0, 0]), (doc_ref, don_ref)
        l_r, d_r = (lc_ref.at[0], ln_ref.at[0]), (dc_ref.at[0], dn_ref.at[0])

        def one(probs):
            k2 = _sw_load((k_ref.at[0, 0], None), probs, dil, 0, 0, _both)
            v2 = _sw_load((v_ref, None), probs, dil, 0, 0, _both)
            dk = jnp.zeros((2 * SW_GROUP, SW_BLOCK, LANE), F32)
            dv = jnp.zeros((2 * SW_GROUP, SW_BLOCK, LANE), F32)
            for shift, mk in ((0, mc), (1, _sw_other_masks(probs, nb - 1, more))):
                q2 = _sw_load(q_r, probs, dil, shift, nb - 1, _pair)
                do2 = _sw_load(do_r, probs, dil, shift, nb - 1, _pair)
                lse = _sw_load(l_r, probs, dil, shift, nb - 1, _head_cols)
                dd = _sw_load(d_r, probs, dil, shift, nb - 1, _head_cols)
                pr = jnp.exp(jnp.where(mk, _bdot(q2, k2, 1, 1) * _SW_SCALE, _NEG) - lse)
                dv = dv + _bdot(pr, do2, 0, 0)
                ds = pr * (_bdot(do2, v2, 1, 1) - dd)
                dk = dk + _bdot(ds, q2, 0, 0)
            for n, (i, r) in enumerate(probs):
                o_ref.at[0, 0][_sw_rows(i, r, dil), :] = (dk[2 * n] + dk[2 * n + 1]) * _SW_SCALE
                o_ref.at[1, 0][_sw_rows(i, r, dil), :] = dv[2 * n] + dv[2 * n + 1]

        _sw_groups(dil, nb, one)

    after = lambda j: jnp.minimum((j + 1) * nb, nunits - 1)
    seg = lambda s: pl.BlockSpec((1, 1, unit * nb, LANE), lambda h, j: (s, h, j, 0))
    seg_a = lambda s: pl.BlockSpec((1, 1, unit, LANE), lambda h, j: (s, h, after(j), 0))
    col = lambda c0: pl.BlockSpec((unit * nb, LANE), lambda h, j: (j, c0 + h))
    col_a = lambda c0: pl.BlockSpec((unit, LANE), lambda h, j: (after(j), c0 + h))
    hp = pl.BlockSpec((1, unit * nb, LANE), lambda h, j: (h, j, 0))
    hp_a = pl.BlockSpec((1, unit, LANE), lambda h, j: (h, after(j), 0))
    return _call(
        body, name=f"sw_dkv{p}", grid=(2, nsp),
        in_specs=[seg(2 * p + 1), col(vcol), seg(2 * p), seg_a(2 * p), col(ycol), col_a(ycol), hp, hp_a, hp, hp_a],
        out_specs=pl.BlockSpec((2, 1, unit * nb, LANE), lambda h, j: (0, h, j, 0)),
        out_shape=jax.ShapeDtypeStruct((3, 2, t, LANE), F32), semantics=("parallel", "parallel"),
    )(qk, proj, qk, qk, dy, dy, lg, lg, dm, dm)


def sw_attn_dq(qk, proj, dy, lg, dm, dkvq, p):
    t = proj.shape[0]
    dil, unit, nb, nsp = _sw_geometry(t, p)
    vcol = (C_SW + 3 * SW_WIDTH * p + 2 * SW_WIDTH) // LANE
    ycol = (DN_WIDTH + GM_WIDTH) // LANE

    def body(q_ref, kc_ref, kp_ref, vc_ref, vp_ref, do_ref, l_ref, d_ref, _, dq_ref):
        mc, _ = _sw_masks(True)
        first = pl.program_id(1) != 0
        k_r, v_r = (kc_ref.at[0, 0], kp_ref.at[0, 0]), (vc_ref, vp_ref)

        def one(probs):
            mp = _sw_other_masks(probs, 0, first)
            q2 = _sw_load((q_ref.at[0, 0], None), probs, dil, 0, 0, _pair)
            do2 = _sw_load((do_ref, None), probs, dil, 0, 0, _pair)
            lse = _sw_load((l_ref.at[0], None), probs, dil, 0, 0, _head_cols)
            dd = _sw_load((d_ref.at[0], None), probs, dil, 0, 0, _head_cols)
            kc = _sw_load(k_r, probs, dil, 0, 0, _both)
            kp = _sw_load(k_r, probs, dil, -1, 0, _both)
            pc = jnp.exp(jnp.where(mc, _bdot(q2, kc, 1, 1) * _SW_SCALE, _NEG) - lse)
            pp = jnp.exp(jnp.where(mp, _bdot(q2, kp, 1, 1) * _SW_SCALE, _NEG) - lse)
            dsc = pc * (_bdot(do2, _sw_load(v_r, probs, dil, 0, 0, _both), 1, 1) - dd)
            dsp = pp * (_bdot(do2, _sw_load(v_r, probs, dil, -1, 0, _both), 1, 1) - dd)
            dq2 = (_bdot(dsc, kc) + _bdot(dsp, kp)) * _SW_SCALE
            for n, (i, r) in enumerate(probs):
                dq_ref.at[0, 0][_sw_rows(i, r, dil), :] = _unpair(dq2[2 * n:2 * n + 2])

        _sw_groups(dil, nb, one)

    before = lambda j: jnp.maximum(j * nb - 1, 0)
    seg = lambda s: pl.BlockSpec((1, 1, unit * nb, LANE), lambda h, j: (s, h, j, 0))
    seg_b = lambda s: pl.BlockSpec((1, 1, unit, LANE), lambda h, j: (s, h, before(j), 0))
    col = lambda c0: pl.BlockSpec((unit * nb, LANE), lambda h, j: (j, c0 + h))
    col_b = lambda c0: pl.BlockSpec((unit, LANE), lambda h, j: (before(j), c0 + h))
    hp = pl.BlockSpec((1, unit * nb, LANE), lambda h, j: (h, j, 0))
    return _call(
        body, name=f"sw_dq{p}", grid=(2, nsp),
        in_specs=[seg(2 * p), seg(2 * p + 1), seg_b(2 * p + 1), col(vcol), col_b(vcol), col(ycol), hp, hp, ANY],
        out_specs=pl.BlockSpec((1, 1, unit * nb, LANE), lambda h, j: (2, h, j, 0)),
        out_shape=jax.ShapeDtypeStruct(dkvq.shape, F32), semantics=("parallel", "parallel"), aliases={8: 0},
    )(qk, qk, qk, proj, proj, dy, lg, dm, dkvq)


def sw_merge(outs, lses, ybuf, *, tm=512):
    t = ybuf.shape[0]

    def body(o0, o1, o2, l0_ref, l1_ref, l2_ref, _, y_ref, lg_ref):
        l0, l1, l2 = l0_ref[...], l1_ref[...], l2_ref[...]
        mx = jnp.maximum(jnp.maximum(l0, l1), l2)
        lg = mx + jnp.log(jnp.exp(l0 - mx) + jnp.exp(l1 - mx) + jnp.exp(l2 - mx))
        lg_ref[...] = lg
        y = jnp.exp(l0 - lg) * o0[...] + jnp.exp(l1 - lg) * o1[...] + jnp.exp(l2 - lg) * o2[...]
        y_ref[...] = jnp.concatenate([y[0], y[1]], axis=1)

    hp = pl.BlockSpec((2, tm, LANE), lambda i: (0, i, 0))
    return _call(
        body, name="sw_merge", grid=(t // tm,), in_specs=[hp] * 6 + [ANY],
        out_specs=[pl.BlockSpec((tm, SW_WIDTH), lambda i: (i, (DN_WIDTH + GM_WIDTH) // SW_WIDTH)), hp],
        out_shape=[jax.ShapeDtypeStruct(ybuf.shape, F32), jax.ShapeDtypeStruct((2, t, LANE), F32)],
        semantics=("parallel",), aliases={6: 0},
    )(*outs, *lses, ybuf)


def sw_delta(dy, ybuf, *, tm=512):
    t = ybuf.shape[0]

    def body(dy_ref, y_ref, o_ref):
        same, _ = _head_mats()
        o_ref[0], o_ref[1] = _halves(_xdot(dy_ref[...] * y_ref[...], same))

    b1 = pl.BlockSpec((tm, SW_WIDTH), lambda i: (i, (DN_WIDTH + GM_WIDTH) // SW_WIDTH))
    return _call(body, name="sw_delta", grid=(t // tm,), in_specs=[b1, b1],
                 out_specs=pl.BlockSpec((2, tm, LANE), lambda i: (0, i, 0)),
                 out_shape=jax.ShapeDtypeStruct((2, t, LANE), F32), semantics=("parallel",))(dy, ybuf)


def _rope_tables(t):
    inv = ROPE_THETA ** (-jnp.arange(0, ROPE_DIM, 2, dtype=F32) / ROPE_DIM)
    ang = jnp.arange(t, dtype=F32)[:, None] * inv[None, :]
    pad1 = jnp.ones((t, SW_HEAD_DIM - ROPE_DIM), F32)
    pad0 = jnp.zeros((t, SW_HEAD_DIM - ROPE_DIM), F32)
    cos_h = jnp.concatenate([jnp.cos(ang), jnp.cos(ang), pad1], axis=1)
    sin_h = jnp.concatenate([jnp.sin(ang), jnp.sin(ang), pad0], axis=1)
    return jnp.tile(cos_h, (1, SW_HEADS)), jnp.tile(sin_h, (1, SW_HEADS))


def sw_forward(proj, nw2, cos_t, sin_t, ybuf):
    qk = sw_prep(proj, nw2, cos_t, sin_t)
    outs, lses = [], []
    for p in range(len(SW_DILATIONS)):
        o, lse = sw_attn(qk, proj, p)
        outs.append(o)
        lses.append(lse)
    ybuf, lg = sw_merge(outs, lses, ybuf)
    return ybuf, (qk, lg)


def sw_backward(proj, nw2, cos_t, sin_t, res, ybuf, dy, dproj):
    qk, lg = res
    dm = sw_delta(dy, ybuf)
    dnw = jnp.zeros((2, 1, SW_WIDTH), F32)
    for p in range(len(SW_DILATIONS)):
        dkvq = sw_attn_dkv(qk, proj, dy, lg, dm, p)
        dkvq = sw_attn_dq(qk, proj, dy, lg, dm, dkvq, p)
        dproj, dnw = sw_prep_bwd(proj, nw2, cos_t, sin_t, dkvq, dproj, dnw, p)
    return dproj, dnw[::-1, 0]


def _pad_rows(a, rows):
    return jnp.zeros((rows,) + a.shape[1:], a.dtype).at[:a.shape[0]].set(a)


def _consts(sp):
    d = {}
    d["mix_nw"] = sp["mix_norm_w"][:, None, :]
    d["ffn_nw"] = sp["ffn_norm_w"][:, None, :]
    d["cw8"] = jnp.pad(sp["dn_conv_w"], ((0, 0), (0, 8 - DN_CONV), (0, 0)))
    d["scal"] = jnp.pad(jnp.stack([sp["dn_a_log"], sp["dn_dt_bias"]], axis=1), ((0, 0), (0, 6), (0, LANE - DN_HEADS)))
    d["wn"] = sp["dn_out_norm_w"][:, None, :]
    d["lng"] = sp["gm_ln_g"][:, None, :]
    d["lnb"] = sp["gm_ln_b"][:, None, :]
    d["w_s"] = sp["gm_w_s"]
    d["bst"] = jnp.pad(jnp.swapaxes(sp["gm_b_s"], 1, 2), ((0, 0), (0, 0), (0, LANE - GM_GROUPS)))
    d["nw2"] = jnp.stack([jnp.tile(sp["sw_q_norm_w"], (1, SW_HEADS)),
                          jnp.tile(sp["sw_k_norm_w"], (1, SW_HEADS))], axis=1)[:, :, None, :]
    return d


def _layer_fwd(x, mod, get_w, cs, tabs):
    wb = dict(get_w("w_in", x))
    h1, proj = norm_mm(x, cs["mix_nw"], mod[1], mod[0], wb["w_in"], swiglu=False, name="in_proj")
    act = dn_conv(proj, cs["cw8"])
    y, states, tinvs = dn_fwd(act, proj, cs["scal"], cs["wn"])
    y = gm_fwd(proj, cs["lng"], cs["lnb"], cs["w_s"], cs["bst"], y)
    y, swres = sw_forward(proj, cs["nw2"], *tabs, y)
    wb.update(get_w("w_out", y))
    x1, o1 = resid_mm(y, wb["w_out"], x, mod[2], name="out_proj")
    wb.update(get_w("ffn", x1))
    h2, gu, actf = norm_mm(x1, cs["ffn_nw"], mod[4], mod[3], wb["w_ffn_in"], swiglu=True, name="ffn_in")
    x2, o2 = resid_mm(actf, wb["w_ffn_out"], x1, mod[5], name="ffn_out")
    res = dict(x=x, h1=h1, proj=proj, act=act, states=states, tinvs=tinvs, swres=swres, y=y, x1=x1, o1=o1, h2=h2, gu=gu,
               actf=actf, o2=o2)
    return x2, res, wb


def _layer_bwd(dx2, res, mod, wb, cs, tabs, grads_done):
    dgu, gx2, dgate2 = resid_mm_bwd(dx2, mod[5], res["o2"], wb["w_ffn_out"], res["gu"], name="ffn_out_bwd", tm=512)
    g_wfo = mm_tn(res["actf"], gx2, name="wg_ffn_out")
    g_wfi = mm_tn(res["h2"], dgu, name="wg_ffn_in")
    token = grads_done("ffn", dict(w_ffn_in=g_wfi, w_ffn_out=g_wfo))
    dx1, d_ffn_nw, dscale2, dshift2 = norm_mm_bwd(dgu, wb["w_ffn_in"], res["x1"], cs["ffn_nw"], mod[4] + token, dx2,
                                                  name="ffn_in_bwd")
    dy, gx1, dgate1 = resid_mm_bwd(dx1, mod[2], res["o1"], wb["w_out"], None, name="out_proj_bwd", tm=512)
    g_wout = mm_tn(res["y"], gx1, name="wg_out")
    proj = res["proj"]
    dact, dproj, dpar = dn_bwd(res["act"], proj, cs["scal"], cs["wn"], res["states"], res["tinvs"], dy)
    dproj, dcw = dn_conv_bwd(proj, cs["cw8"], dact, dproj)
    dproj, dws, dbst, dln = gm_bwd(proj, cs["lng"], cs["lnb"], cs["w_s"], cs["bst"], dy, dproj)
    dproj, dnw = sw_backward(proj, cs["nw2"], *tabs, res["swres"], res["y"], dy, dproj)
    g_win = mm_tn(res["h1"], dproj, name="wg_in")
    dx, d_mix_nw, dscale1, dshift1 = norm_mm_bwd(dproj, wb["w_in"], res["x"], cs["mix_nw"], mod[1], dx1,
                                                 name="in_proj_bwd")
    dmod = jnp.concatenate([dshift1, dscale1, dgate1, dshift2, dscale2, dgate2], axis=1)
    dnw = dnw.reshape(2, SW_HEADS, SW_HEAD_DIM).sum(1)
    small = dict(mix_norm_w=d_mix_nw[0], ffn_norm_w=d_ffn_nw[0], dn_conv_w=dcw[:DN_CONV],
                 dn_a_log=dpar[0, :DN_HEADS], dn_dt_bias=dpar[1, :DN_HEADS], dn_out_norm_w=dpar[2],
                 gm_ln_g=dln[0], gm_ln_b=dln[1], gm_w_s=dws, gm_b_s=dbst[:, :GM_GROUPS].T,
                 sw_q_norm_w=dnw[0], sw_k_norm_w=dnw[1])
    token = grads_done("mix", dict(w_in=g_win, w_out=g_wout))
    return dx, small, dmod, token


def _permute_w_in(w):
    pad = jnp.zeros(w.shape[:-1] + (AB_PAD - 8,), w.dtype)
    return jnp.concatenate([w[..., 0:2056], pad, w[..., 2568:IN_WIDTH], w[..., 2056:2568]], axis=-1)


def _unpermute_w_in(g):
    return jnp.concatenate([g[..., 0:2056], g[..., C_UV:IN_PAD], g[..., C_SW:C_UV]], axis=-1)


def _local_step(x, target, mods, weights_of, grads_done, sp):
    layers = mods.shape[0]
    t, d = x.shape
    tabs = _rope_tables(t)
    consts = _consts(sp)
    saved = []
    for layer in range(layers):
        mod = mods[layer].reshape(6, 1, d)
        cs = {k: v[layer] for k, v in consts.items()}
        x, res, wb = _layer_fwd(x, mod, functools.partial(weights_of, layer), cs, tabs)
        saved.append((res, mod, wb, cs))
    dx, loss = loss_head(x, target)
    smalls, dmods = [], []
    token = jnp.zeros((1, 1), F32)
    for layer in reversed(range(layers)):
        res, mod, wb, cs = saved[layer]
        dx, small, dmod, token = _layer_bwd(dx, res, mod + token, wb, cs, tabs, functools.partial(grads_done, layer))
        smalls.append(small)
        dmods.append(dmod[0])
    smalls, dmods = smalls[::-1], dmods[::-1]
    small = {k: jnp.stack([s[k] for s in smalls]) for k in smalls[0]}
    return loss, dx, small, jnp.stack(dmods) + token


def mod_fwd(c_all, w_mod, b_shard):
    layers, d, n = w_mod.shape

    def body(c_ref, w_ref, b_ref, o_ref):
        ca = _silu(c_ref[...]).astype(BF16)
        o_ref[0] = _dot(ca, w_ref[0].astype(BF16), 1, 0) + b_ref[0]

    return _call(
        body, name="mod_fwd", grid=(layers,),
        in_specs=[_full((8, d)), pl.BlockSpec((1, d, n), lambda i: (i, 0, 0)),
                  pl.BlockSpec((1, 1, n), lambda i: (i, 0, 0))],
        out_specs=pl.BlockSpec((1, 8, n), lambda i: (i, 0, 0)),
        out_shape=jax.ShapeDtypeStruct((layers, 8, n), F32), semantics=("parallel",),
    )(c_all, w_mod, b_shard)


def mod_bwd(c_all, dmod):
    layers, _, n = dmod.shape
    d = c_all.shape[1]

    def body(c_ref, g_ref, o_ref):
        ca = _silu(c_ref[...]).astype(BF16)
        o_ref[0] = _dot(ca, g_ref[0].astype(BF16), 0, 0)

    return _call(
        body, name="mod_bwd", grid=(layers,),
        in_specs=[_full((8, d)), pl.BlockSpec((1, 8, n), lambda i: (i, 0, 0))],
        out_specs=pl.BlockSpec((1, d, n), lambda i: (i, 0, 0)),
        out_shape=jax.ShapeDtypeStruct((layers, d, n), F32), semantics=("parallel",),
    )(c_all, dmod)


N_DEV = 8


def _place():
    return lax.axis_index("x"), lax.axis_index("y"), lax.axis_index("c")


def _other_chips(x, y):
    return [(1 - x, y), (x, 1 - y), (1 - x, 1 - y)]


def allgather8(x_shard, *, name):
    m_per, n = x_shard.shape

    def body(x_ref, out_ref, send_sems, recv_sems, local_sem):
        x, y, c = _place()
        me, sibling = (x, y, c), (x, y, 1 - c)
        chips = _other_chips(x, y)

        def rows(px, py, pc):
            return out_ref.at[pl.ds((4 * px + 2 * py + pc) * m_per, m_per), :]

        def copy(k, block, to, src=None):
            return pltpu.make_async_remote_copy(
                src_ref=rows(*block) if src is None else src, dst_ref=rows(*block),
                send_sem=send_sems.at[k], recv_sem=recv_sems.at[k], device_id=to, device_id_type=MESH)

        mine = pltpu.make_async_copy(x_ref, rows(*me), local_sem)
        mine.start()
        first = [copy(0, me, sibling, src=x_ref)]
        first += [copy(1 + j, me, (*chip, c), src=x_ref) for j, chip in enumerate(chips)]
        for cp in first:
            cp.start()
        passed = [copy(4 + j, (*chip, c), sibling) for j, chip in enumerate(chips)]
        for j, chip in enumerate(chips):
            copy(1 + j, (*chip, c), me).wait_recv()
            passed[j].start()
        copy(0, sibling, me).wait_recv()
        for j, chip in enumerate(chips):
            copy(4 + j, (*chip, 1 - c), me).wait_recv()
        for cp in first + passed:
            cp.wait_send()
        mine.wait()

    return pl.pallas_call(
        body, name=name, out_shape=jax.ShapeDtypeStruct((N_DEV * m_per, n), x_shard.dtype),
        in_specs=[pl.BlockSpec(memory_space=pltpu.VMEM)], out_specs=pl.BlockSpec(memory_space=pltpu.VMEM),
        scratch_shapes=[pltpu.SemaphoreType.DMA((7,)), pltpu.SemaphoreType.DMA((7,)), pltpu.SemaphoreType.DMA],
    )(x_shard)


HBM = pl.BlockSpec(memory_space=pltpu.HBM)
SEM = pl.BlockSpec(memory_space=pltpu.SEMAPHORE)
_EFFECT = pltpu.SideEffectType.DATAFLOW_SIDE_EFFECTING


def _piece(ref, sliced, chip):
    return ref.at[2 * chip[0] + chip[1]] if sliced else ref


def exchange_start(srcs, after, *, sliced, name):
    n = len(srcs)
    piece = lambda s: s.shape[1:] if sliced else s.shape

    def body(*refs):
        ins, lands = refs[:n], refs[n:2 * n]
        send_sems, recv_sems = refs[2 * n + len(after):2 * n + len(after) + 2]
        token = refs[-1]
        x, y, c = _place()
        me_s = 2 * x + y
        for a in range(n):
            for j, chip in enumerate(_other_chips(x, y)):
                pltpu.make_async_remote_copy(
                    src_ref=_piece(ins[a], sliced, chip), dst_ref=lands[a].at[me_s], send_sem=send_sems.at[3 * a + j],
                    recv_sem=recv_sems.at[3 * a + j], device_id=(*chip, c), device_id_type=MESH).start()
        token[...] = jnp.zeros_like(token)

    zones = [pltpu.with_memory_space_constraint(lax.empty((4,) + piece(s), s.dtype), pltpu.HBM) for s in srcs]
    srcs = [pltpu.with_memory_space_constraint(s, pltpu.HBM) for s in srcs]
    out = pl.pallas_call(
        body, name=name,
        out_shape=(pltpu.SemaphoreType.DMA((3 * n,)), pltpu.SemaphoreType.DMA((3 * n,)),
                   *[pltpu.HBM(s.shape, s.dtype) for s in srcs], *[pltpu.HBM(z.shape, z.dtype) for z in zones],
                   jax.ShapeDtypeStruct((8, LANE), F32)),
        in_specs=[HBM] * (2 * n) + [ANY] * len(after),
        out_specs=(SEM, SEM, *[HBM] * (2 * n), pl.BlockSpec(memory_space=pltpu.VMEM)),
        input_output_aliases={i: 2 + i for i in range(2 * n)},
        compiler_params=pltpu.CompilerParams(has_side_effects=_EFFECT),
    )(*srcs, *zones, *after)
    return out[0], out[1], out[2:2 + n], out[2 + n:2 + 2 * n], out[-1]


def exchange_wait(send_sems, recv_sems, srcs, zones, after, *, which, sliced, name):
    n = len(srcs)

    def body(*refs):
        ins, lands = refs[:n], refs[n:2 * n]
        send_sems, recv_sems = refs[2 * n:2 * n + 2]
        x, y, c = _place()
        for a in range(n):
            for j, chip in enumerate(_other_chips(x, y)):
                copy = pltpu.make_async_remote_copy(
                    src_ref=_piece(ins[a], sliced, chip), dst_ref=lands[a].at[2 * chip[0] + chip[1]],
                    send_sem=send_sems.at[3 * which[a] + j], recv_sem=recv_sems.at[3 * which[a] + j],
                    device_id=(*chip, c), device_id_type=MESH)
                copy.wait_send()
                copy.wait_recv()

    out = pl.pallas_call(
        body, name=name,
        out_shape=tuple(pltpu.HBM(s.shape, s.dtype) for s in (*srcs, *zones)),
        in_specs=[HBM] * (2 * n) + [SEM, SEM, ANY], out_specs=tuple([HBM] * (2 * n)),
        input_output_aliases={i: i for i in range(2 * n)},
        compiler_params=pltpu.CompilerParams(has_side_effects=_EFFECT),
    )(*srcs, *zones, send_sems, recv_sems, after)
    return out[n:]


def sibling_swap(parts):
    n = len(parts)

    def body(*refs):
        ins, outs = refs[:n], refs[n:2 * n]
        send_sems, recv_sems = refs[2 * n:]
        x, y, c = _place()
        cps = []
        for a in range(n):
            cp = pltpu.make_async_remote_copy(
                src_ref=ins[a], dst_ref=outs[a], send_sem=send_sems.at[a], recv_sem=recv_sems.at[a],
                device_id=(x, y, 1 - c), device_id_type=MESH)
            cp.start()
            cps.append(cp)
        for cp in cps:
            cp.wait()

    return pl.pallas_call(
        body, name="sibling_swap", out_shape=[jax.ShapeDtypeStruct(p.shape, p.dtype) for p in parts],
        in_specs=[ANY] * n, out_specs=[ANY] * n,
        scratch_shapes=[pltpu.SemaphoreType.DMA((n,)), pltpu.SemaphoreType.DMA((n,))],
    )(*parts)


def _row_block(rows, cols, budget=1 << 20):
    best = rows if rows % 8 else 8
    for tr in range(8, rows + 1, 8):
        if rows % tr == 0 and tr * cols * 4 <= budget:
            best = tr
    return best


def chip_sum(own, recv, me_s, buf, layer, layers, *, name):
    r, n = own.shape
    tr = _row_block(r, n)
    steps = r // tr

    def body(me_ref, own_ref, recv_ref, *rest):
        o_ref = rest[-1]
        me = me_ref[0]
        acc = jnp.zeros((tr, n), F32)
        for s in range(4):
            acc = acc + jnp.where(me == s, own_ref[...], recv_ref[s].astype(F32))
        o_ref[...] = acc

    in_specs = [pl.BlockSpec((tr, n), lambda i, me: (i, 0)), pl.BlockSpec((4, tr, n), lambda i, me: (0, i, 0))]
    args = [me_s, own, recv]
    aliases = {}
    if buf is not None:
        in_specs.append(ANY)
        args.append(buf)
        aliases = {3: 0}
    return pl.pallas_call(
        body, name=name, out_shape=jax.ShapeDtypeStruct((layers * r, n), F32),
        grid_spec=pltpu.PrefetchScalarGridSpec(
            num_scalar_prefetch=1, grid=(steps,), in_specs=in_specs,
            out_specs=pl.BlockSpec((tr, n), lambda i, me: (layer * steps + i, 0))),
        input_output_aliases=aliases,
        compiler_params=pltpu.CompilerParams(dimension_semantics=("parallel",)),
    )(*args)


def _adam_update(w, g, m, v):
    m2 = ADAM_B1 * m + (1.0 - ADAM_B1) * g
    v2 = ADAM_B2 * v + (1.0 - ADAM_B2) * (g * g)
    m_hat = m2 / (1.0 - ADAM_B1 ** ADAM_STEP)
    v_hat = v2 / (1.0 - ADAM_B2 ** ADAM_STEP)
    delta = -ADAM_LR * (m_hat / (jnp.sqrt(v_hat) + ADAM_EPS) + ADAM_WD * w)
    return delta, m2, v2


def adamw(w, g_parts, m, v, *, name):
    r, n = w.shape
    tr = _row_block(r, n)
    k = len(g_parts)

    def body(*refs):
        w_ref, m_ref, v_ref = refs[k], refs[k + 1], refs[k + 2]
        g_ref, d_ref, m2_ref, v2_ref = refs[k + 3:]
        g = refs[0][...]
        for p in refs[1:k]:
            g = g + p[...]
        g_ref[...] = g
        d_ref[...], m2_ref[...], v2_ref[...] = _adam_update(w_ref[...], g, m_ref[...], v_ref[...])

    blk = pl.BlockSpec((tr, n), lambda i: (i, 0))
    shp = jax.ShapeDtypeStruct((r, n), F32)
    return _call(body, name=name, grid=(r // tr,), in_specs=[blk] * (k + 3), out_specs=[blk] * 4,
                 out_shape=[shp] * 4, semantics=("parallel",))(*g_parts, w, m, v)


def adamw_gathered(g_all, w, m, v, *, name):
    _, r, n = g_all.shape
    tr = _row_block(r, n * 4)

    def body(ga_ref, w_ref, m_ref, v_ref, g_ref, d_ref, m2_ref, v2_ref):
        g = ga_ref[0]
        for dev in range(1, N_DEV):
            g = g + ga_ref[dev]
        g_ref[...] = g
        d_ref[...], m2_ref[...], v2_ref[...] = _adam_update(w_ref[...], g, m_ref[...], v_ref[...])

    blk = pl.BlockSpec((tr, n), lambda i: (i, 0))
    shp = jax.ShapeDtypeStruct((r, n), F32)
    return _call(body, name=name, grid=(r // tr,),
                 in_specs=[pl.BlockSpec((N_DEV, tr, n), lambda i: (0, i, 0)), blk, blk, blk], out_specs=[blk] * 4,
                 out_shape=[shp] * 4, semantics=("parallel",))(g_all, w, m, v)


BIG = ("w_in", "w_out", "w_ffn_in", "w_ffn_out")
SMALL = ("b_mod", "mix_norm_w", "ffn_norm_w", "dn_conv_w", "dn_a_log", "dn_dt_bias", "dn_out_norm_w", "gm_ln_g",
         "gm_ln_b", "gm_w_s", "gm_b_s", "sw_q_norm_w", "sw_k_norm_w")
WEIGHTS = ("w_mod", "b_mod", "mix_norm_w", "ffn_norm_w", "w_in", "w_out", "dn_conv_w", "dn_a_log", "dn_dt_bias",
           "dn_out_norm_w", "gm_ln_g", "gm_ln_b", "gm_w_s", "gm_b_s", "sw_q_norm_w", "sw_k_norm_w", "w_ffn_in",
           "w_ffn_out")
PACK_ROWS = 8


def _pack(arrs):
    out = []
    for a in arrs:
        flat = a.reshape(-1).astype(F32)
        rows = -(-flat.shape[0] // (LANE * PACK_ROWS)) * PACK_ROWS
        out.append(jnp.pad(flat, (0, rows * LANE - flat.shape[0])).reshape(rows, LANE))
    return jnp.concatenate(out, axis=0)


def _unpack(packed, shapes):
    out, r0 = [], 0
    for shp in shapes:
        size = math.prod(shp)
        rows = -(-size // (LANE * PACK_ROWS)) * PACK_ROWS
        out.append(packed[r0:r0 + rows].reshape(-1)[:size].reshape(shp))
        r0 += rows
    return out


def kernel(x, c, w_mod, b_mod, mix_norm_w, ffn_norm_w, w_in, w_out, dn_conv_w, dn_a_log, dn_dt_bias, dn_out_norm_w, gm_ln_g, gm_ln_b, gm_w_s, gm_b_s, sw_q_norm_w, sw_k_norm_w, w_ffn_in, w_ffn_out, loss_target, m_w_mod, m_b_mod, m_mix_norm_w, m_ffn_norm_w, m_w_in, m_w_out, m_dn_conv_w, m_dn_a_log, m_dn_dt_bias, m_dn_out_norm_w, m_gm_ln_g, m_gm_ln_b, m_gm_w_s, m_gm_b_s, m_sw_q_norm_w, m_sw_k_norm_w, m_w_ffn_in, m_w_ffn_out, v_w_mod, v_b_mod, v_mix_norm_w, v_ffn_norm_w, v_w_in, v_w_out, v_dn_conv_w, v_dn_a_log, v_dn_dt_bias, v_dn_out_norm_w, v_gm_ln_g, v_gm_ln_b, v_gm_w_s, v_gm_b_s, v_sw_q_norm_w, v_sw_k_norm_w, v_w_ffn_in, v_w_ffn_out):
    w = dict(w_mod=w_mod, b_mod=b_mod, mix_norm_w=mix_norm_w, ffn_norm_w=ffn_norm_w, w_in=w_in, w_out=w_out,
             dn_conv_w=dn_conv_w, dn_a_log=dn_a_log, dn_dt_bias=dn_dt_bias, dn_out_norm_w=dn_out_norm_w,
             gm_ln_g=gm_ln_g, gm_ln_b=gm_ln_b, gm_w_s=gm_w_s, gm_b_s=gm_b_s, sw_q_norm_w=sw_q_norm_w,
             sw_k_norm_w=sw_k_norm_w, w_ffn_in=w_ffn_in, w_ffn_out=w_ffn_out)
    m = dict(w_mod=m_w_mod, b_mod=m_b_mod, mix_norm_w=m_mix_norm_w, ffn_norm_w=m_ffn_norm_w, w_in=m_w_in,
             w_out=m_w_out, dn_conv_w=m_dn_conv_w, dn_a_log=m_dn_a_log, dn_dt_bias=m_dn_dt_bias,
             dn_out_norm_w=m_dn_out_norm_w, gm_ln_g=m_gm_ln_g, gm_ln_b=m_gm_ln_b, gm_w_s=m_gm_w_s, gm_b_s=m_gm_b_s,
             sw_q_norm_w=m_sw_q_norm_w, sw_k_norm_w=m_sw_k_norm_w, w_ffn_in=m_w_ffn_in, w_ffn_out=m_w_ffn_out)
    v = dict(w_mod=v_w_mod, b_mod=v_b_mod, mix_norm_w=v_mix_norm_w, ffn_norm_w=v_ffn_norm_w, w_in=v_w_in,
             w_out=v_w_out, dn_conv_w=v_dn_conv_w, dn_a_log=v_dn_a_log, dn_dt_bias=v_dn_dt_bias,
             dn_out_norm_w=v_dn_out_norm_w, gm_ln_g=v_gm_ln_g, gm_ln_b=v_gm_ln_b, gm_w_s=v_gm_w_s, gm_b_s=v_gm_b_s,
             sw_q_norm_w=v_sw_q_norm_w, sw_k_norm_w=v_sw_k_norm_w, w_ffn_in=v_w_ffn_in, w_ffn_out=v_w_ffn_out)
    layers, d, mod_n = w_mod.shape
    mx, my, mc = _place()
    me_s = 2 * mx + my
    me_dev = 4 * mx + 2 * my + mc

    c_all = allgather8(_pad_rows(c, 8), name="gather_c").reshape(N_DEV, 8, d)[:, 0]
    b_shard = lax.dynamic_slice_in_dim(b_mod, me_s * mod_n, mod_n, axis=1)[:, None, :]
    mod_part = mod_fwd(c_all, w_mod, b_shard)
    mod_parts = allgather8(mod_part.reshape(layers * 8, mod_n), name="gather_mod")
    mod_parts = mod_parts.reshape(4, 2, layers, 8, mod_n)[:, 0]
    mod_all = mod_parts.transpose(1, 2, 0, 3).reshape(layers, 8, 4 * mod_n)
    mods = lax.dynamic_index_in_dim(mod_all, me_dev, axis=1, keepdims=False)

    cw = dn_conv_w.shape[-1]
    conv_rows = -(-layers * DN_CONV // 8) * 8
    conv_parts = allgather8(_pad_rows(dn_conv_w.reshape(layers * DN_CONV, cw), conv_rows), name="gather_conv")
    conv_parts = conv_parts.reshape(4, 2, conv_rows, cw)[:, 0, :layers * DN_CONV]
    conv_full = conv_parts.reshape(4, layers, DN_CONV, cw).transpose(1, 2, 0, 3).reshape(layers, DN_CONV, 4 * cw)

    shards = {k: w[k].astype(BF16) for k in BIG}
    groups = dict(w_in=(0,), w_out=(1,), ffn=(2, 3))
    gathers = [exchange_start([shards[k][layer] for k in BIG], [mods, conv_full], sliced=False, name=f"gather_start{layer}")
               for layer in range(layers)]
    mods = mods + sum(g[4][0, 0] for g in gathers)

    def weights_of(layer, group, after):
        send_sems, recv_sems, srcs, zones, _ = gathers[layer]
        which = groups[group]
        got = exchange_wait(send_sems, recv_sems, [srcs[a] for a in which], [zones[a] for a in which], after,
                            which=which, sliced=False, name=f"gather_wait_{group}{layer}")
        full = {BIG[a]: lax.dynamic_update_index_in_dim(z, shards[BIG[a]][layer], me_s, 0) for a, z in zip(which, got)}
        cols = lambda g: jnp.concatenate([g[s] for s in range(4)], axis=-1)
        shape = dict(w_in=lambda g: _permute_w_in(cols(g)), w_out=lambda g: g.reshape(-1, d), w_ffn_in=cols,
                     w_ffn_out=lambda g: g.reshape(-1, d))
        return {k: shape[k](g) for k, g in full.items()}

    scatters = {}
    last_scatter = []
    shard_axis = dict(w_in=1, w_out=0, w_ffn_in=1, w_ffn_out=0)

    def grads_done(layer, group, grads):
        fix = lambda k, g: _unpermute_w_in(g) if k == "w_in" else g
        send = [jnp.stack(jnp.split(fix(k, g16), 4, axis=shard_axis[k])) for k, (_, g16) in grads.items()]
        own = {}
        for k, (g32, _) in grads.items():
            g32 = fix(k, g32)
            size = g32.shape[shard_axis[k]] // 4
            own[k] = lax.dynamic_slice_in_dim(g32, me_s * size, size, axis=shard_axis[k])
        if (layer, group) == (0, "mix"):
            last_scatter.append((send, own))
            return jnp.zeros((1, 1), F32)
        started = exchange_start(send, [], sliced=True, name=f"scatter_start_{group}{layer}")
        scatters[layer, group] = (started, own)
        return started[4][:1, :1]

    sp = {k: w[k] for k in SMALL}
    sp["dn_conv_w"] = conv_full
    loss_blk, grad_x, small, dmods = _local_step(x[0], loss_target[0], mods, weights_of, grads_done, sp)
    loss = lax.psum(loss_blk[0, 0], ("x", "y", "c"))

    outs = {}
    small = dict(small, b_mod=dmods)
    packed = _pack([small[k] for k in SMALL])
    rows = packed.shape[0]
    g_all = allgather8(packed, name="gather_small").reshape(N_DEV, rows, LANE)
    send, own = last_scatter[0]
    scatters[0, "mix"] = (exchange_start(send, [g_all], sliced=True, name="scatter_start_mix0"), own)
    g_all = g_all + scatters[0, "mix"][0][4][0, 0]
    conv_zero = jnp.zeros((layers, DN_CONV, 3 * DN_WIDTH), F32)
    pk = lambda src: _pack([conv_zero if k == "dn_conv_w" else src[k] for k in SMALL])
    res = adamw_gathered(g_all, pk(w), pk(m), pk(v), name="adamw_small")
    shapes = [small[k].shape for k in SMALL]
    un = [_unpack(a, shapes) for a in res]
    for i, k in enumerate(SMALL):
        outs[k] = [un[j][i] for j in range(4)]
    g_conv = lax.dynamic_slice_in_dim(outs["dn_conv_w"][0], me_s * cw, cw, axis=2)
    flat = lambda a: a.reshape(-1, cw)
    res = adamw(flat(dn_conv_w), [flat(g_conv)], flat(m["dn_conv_w"]), flat(v["dn_conv_w"]), name="adamw_conv")
    outs["dn_conv_w"] = [a.reshape(dn_conv_w.shape) for a in res]

    b_rows = layers * 6 * d // LANE
    dmod_all = g_all[:, :b_rows].reshape(N_DEV, layers, 6 * d).transpose(1, 0, 2)
    dmod_shard = lax.dynamic_slice_in_dim(dmod_all, me_s * mod_n, mod_n, axis=2)
    g_wmod = mod_bwd(c_all, dmod_shard)
    flat = lambda a: a.reshape(-1, mod_n)
    res = adamw(flat(w_mod), [flat(g_wmod)], flat(m_w_mod), flat(v_w_mod), name="adamw_w_mod")
    outs["w_mod"] = [a.reshape(w_mod.shape) for a in res]

    me_arr = jnp.reshape(me_s, (1,)).astype(jnp.int32)
    partial = {k: None for k in BIG}
    for layer in range(layers):
        for group in ("ffn", "mix"):
            (send_sems, recv_sems, srcs, zones, _), own = scatters[layer, group]
            zones = exchange_wait(send_sems, recv_sems, srcs, zones, res[0], which=tuple(range(len(srcs))),
                                  sliced=True, name=f"scatter_wait_{group}{layer}")
            for k, z in zip(own, zones):
                partial[k] = chip_sum(own[k], z, me_arr, partial[k], layer, layers, name=f"chip_sum_{k}{layer}")
    partial = [partial[k] for k in BIG]
    theirs = sibling_swap(partial)
    for k, mine, other in zip(BIG, partial, theirs):
        shp = w[k].shape
        flat = lambda a: a.reshape(-1, shp[-1])
        res = adamw(flat(w[k]), [mine, other], flat(m[k]), flat(v[k]), name="adamw_" + k)
        outs[k] = [a.reshape(shp) for a in res]

    result = [loss, grad_x[None]]
    for j in range(4):
        result += [outs[k][j] for k in WEIGHTS]
    return tuple(result)
```

```python
import functools
import math

import jax
import jax.numpy as jnp
from jax import lax
from jax.experimental import pallas as pl
from jax.experimental.pallas import tpu as pltpu

F32 = jnp.float32
BF16 = jnp.bfloat16
HI = lax.Precision.HIGH

NORM_EPS = 1e-6
DN_HEADS = 4
DN_HEAD_DIM = 128
DN_WIDTH = 512
DN_CHUNK = 64
DN_CONV = 4
GM_WIDTH = 256
GM_GROUPS = 4
GM_GROUP_DIM = 64
GM_CHUNK = 128
SW_HEADS = 4
SW_HEAD_DIM = 64
SW_WIDTH = 256
SW_DILATIONS = (1, 4, 16)
SW_BLOCK = 128
ROPE_THETA = 500000.0
ROPE_DIM = 16
LANE = 128

C_QKV = 0
C_Z = 1536
C_AB = 2048
C_SW = 2304
C_UV = 4608
IN_WIDTH = 4872
IN_PAD = 5120
AB_PAD = C_SW - C_AB
MIX_WIDTH = 1024

ADAM_LR = 0.001
ADAM_B1 = 0.9
ADAM_B2 = 0.999
ADAM_EPS = 1e-08
ADAM_WD = 0.01
ADAM_STEP = 10

MESH = pl.DeviceIdType.MESH


BIG_VMEM = 56 << 20


def _call(body, *, name, grid, in_specs, out_specs, out_shape, scratch_shapes=(), semantics=None, aliases=None,
          vmem=None):
    if semantics is None:
        semantics = ("arbitrary",) * len(grid)
    return pl.pallas_call(
        body, name=name, grid=grid, in_specs=in_specs, out_specs=out_specs, out_shape=out_shape,
        scratch_shapes=list(scratch_shapes), input_output_aliases=aliases or {},
        compiler_params=pltpu.CompilerParams(dimension_semantics=semantics, vmem_limit_bytes=vmem),
    )


def _dot(a, b, ca, cb, prec=None):
    if a.ndim == 3:
        dims = (((ca + 1,), (cb + 1,)), ((0,), (0,)))
    else:
        dims = (((ca,), (cb,)), ((), ()))
    return lax.dot_general(a, b, dims, preferred_element_type=F32, precision=prec)


def _bdot(a, b, ca=1, cb=0):
    return _dot(a.astype(BF16), b.astype(BF16), ca, cb)


def _hdot(a, b, ca=1, cb=0):
    return _dot(a.astype(F32), b.astype(F32), ca, cb, HI)


def _split(x):
    hi = x.astype(BF16)
    return hi, (x - hi.astype(F32)).astype(BF16)


def _xdot(a, b, ca=1, cb=0, exact=1):
    if exact == 1:
        hi, lo = _split(a)
        e = b.astype(BF16)
        return _dot(hi, e, ca, cb) + _dot(lo, e, ca, cb)
    hi, lo = _split(b)
    e = a.astype(BF16)
    return _dot(e, hi, ca, cb) + _dot(e, lo, ca, cb)


def _sigmoid(x):
    return 0.5 * jnp.tanh(0.5 * x) + 0.5


def _silu(x):
    return x * _sigmoid(x)


def _dsilu(x):
    s = _sigmoid(x)
    return s * (1.0 + x * (1.0 - s))


def _softplus(x):
    return jnp.maximum(x, 0.0) + jnp.log(1.0 + jnp.exp(-jnp.abs(x)))


def _iota2(shape, dim):
    return lax.broadcasted_iota(jnp.int32, shape, dim)


def _rowsum(x):
    return jnp.sum(x, axis=-1, keepdims=True)


def _colsum(x):
    return jnp.sum(x, axis=-2, keepdims=True)


def _full(shape):
    return pl.BlockSpec(shape, lambda *_: (0,) * len(shape))


def _resident(shape):
    return pl.BlockSpec(shape, lambda *_: (0,) * len(shape), pipeline_mode=pl.Buffered(1))


ANY = pl.BlockSpec(memory_space=pl.ANY)


def _norm_mod(x, nw, scale, shift):
    r = lax.rsqrt(jnp.mean(x * x, axis=-1, keepdims=True) + NORM_EPS)
    xn = x * r
    return xn, r, (xn * nw) * (1.0 + scale) + shift


def norm_mm(x, nw, scale, shift, w, *, swiglu, name, tm=512):
    t, d = x.shape
    n = w.shape[1]
    half = n // 2

    def body(x_ref, nw_ref, sc_ref, sh_ref, w_ref, h_ref, y_ref, *act_ref):
        _, _, h = _norm_mod(x_ref[...], nw_ref[...], sc_ref[...], sh_ref[...])
        hb = h.astype(BF16)
        h_ref[...] = hb
        y = _dot(hb, w_ref[...], 1, 0)
        y_ref[...] = y.astype(y_ref.dtype)
        if swiglu:
            act_ref[0][...] = (_silu(y[:, :half]) * y[:, half:]).astype(BF16)

    row = lambda i: (i, 0)
    out_shape = [jax.ShapeDtypeStruct((t, d), BF16), jax.ShapeDtypeStruct((t, n), BF16 if swiglu else F32)]
    out_specs = [pl.BlockSpec((tm, d), row), pl.BlockSpec((tm, n), row)]
    if swiglu:
        out_shape.append(jax.ShapeDtypeStruct((t, half), BF16))
        out_specs.append(pl.BlockSpec((tm, half), row))
    return _call(
        body, name=name, grid=(t // tm,),
        in_specs=[pl.BlockSpec((tm, d), row), _full((1, d)), _full((1, d)), _full((1, d)), _resident((d, n))],
        out_specs=out_specs, out_shape=out_shape, semantics=("parallel",), vmem=BIG_VMEM,
    )(x, nw, scale, shift, w)


def resid_mm(y, w, x, gate, *, name, tm=512, target=None):
    t, k = y.shape
    d = w.shape[1]
    steps = t // tm
    head = target is not None

    def body(y_ref, w_ref, x_ref, g_ref, *rest):
        o = _dot(y_ref[...].astype(BF16), w_ref[...], 1, 0)
        xo = x_ref[...] + g_ref[...] * o
        if not head:
            xo_ref, o_ref = rest
            o_ref[...] = o
            xo_ref[...] = xo
            return
        t_ref, dy_ref, o_ref, l_ref, acc_ref = rest
        i = pl.program_id(0)

        @pl.when(i == 0)
        def _():
            acc_ref[...] = jnp.zeros_like(acc_ref)

        o_ref[...] = o
        e = xo - t_ref[...]
        dy_ref[...] = e * (1.0 / d)
        acc_ref[...] += _colsum(e * e)

        @pl.when(i == steps - 1)
        def _():
            tot = jnp.sum(acc_ref[...], axis=-1, keepdims=True) * (0.5 / d)
            l_ref[...] = jnp.broadcast_to(tot, l_ref.shape)

    row = lambda i: (i, 0)
    blk = pl.BlockSpec((tm, d), row)
    in_specs = [pl.BlockSpec((tm, k), row), _resident((k, d)), blk, _full((1, d))]
    out_specs = [blk, blk]
    out_shape = [jax.ShapeDtypeStruct((t, d), F32), jax.ShapeDtypeStruct((t, d), F32)]
    args = [y, w, x, gate]
    if head:
        in_specs.append(blk)
        args.append(target)
        out_specs.append(_full((8, LANE)))
        out_shape.append(jax.ShapeDtypeStruct((8, LANE), F32))
    return _call(
        body, name=name, grid=(steps,), in_specs=in_specs, out_specs=out_specs, out_shape=out_shape,
        scratch_shapes=[pltpu.VMEM((1, d), F32)] if head else (),
        semantics=("arbitrary",) if head else ("parallel",), vmem=BIG_VMEM,
    )(*args)


def resid_mm_bwd(dx, gate, o, w, gu, *, name, tm):
    t, d = dx.shape
    k = w.shape[0]
    swiglu = gu is not None

    def body(dx_ref, g_ref, o_ref, w_ref, *rest):
        if swiglu:
            gu_ref, dy_ref, gx_ref, dg_ref = rest
        else:
            dy_ref, gx_ref, dg_ref = rest
        i = pl.program_id(0)
        dxv = dx_ref[...]
        gx = (dxv * g_ref[...]).astype(BF16)
        gx_ref[...] = gx
        part = _colsum(dxv * o_ref[...])

        @pl.when(i == 0)
        def _():
            dg_ref[...] = jnp.zeros_like(dg_ref)

        dg_ref[...] += part
        da = _dot(gx, w_ref[...], 1, 1)
        if swiglu:
            g = gu_ref[:, :k].astype(F32)
            u = gu_ref[:, k:].astype(F32)
            dy_ref[:, :k] = (da * u * _dsilu(g)).astype(BF16)
            dy_ref[:, k:] = (da * _silu(g)).astype(BF16)
        else:
            dy_ref[...] = da

    row = lambda i: (i, 0)
    in_specs = [pl.BlockSpec((tm, d), row), _full((1, d)), pl.BlockSpec((tm, d), row), _resident((k, d))]
    args = [dx, gate, o, w]
    if swiglu:
        in_specs.append(pl.BlockSpec((tm, 2 * k), row))
        args.append(gu)
        dy_shape = jax.ShapeDtypeStruct((t, 2 * k), BF16)
        dy_spec = pl.BlockSpec((tm, 2 * k), row)
    else:
        dy_shape = jax.ShapeDtypeStruct((t, k), F32)
        dy_spec = pl.BlockSpec((tm, k), row)
    return _call(
        body, name=name, grid=(t // tm,), in_specs=in_specs,
        out_specs=[dy_spec, pl.BlockSpec((tm, d), row), _full((1, d))],
        out_shape=[dy_shape, jax.ShapeDtypeStruct((t, d), BF16), jax.ShapeDtypeStruct((1, d), F32)], vmem=BIG_VMEM,
    )(*args)


def norm_mm_bwd(dy, w, x, nw, scale, dres, *, name, tm=512):
    t, n = dy.shape
    d = x.shape[1]
    steps = t // tm

    def body(dy_ref, w_ref, x_ref, nw_ref, sc_ref, dres_ref, dx_ref, dnw_ref, dsc_ref, dsh_ref):
        i = pl.program_id(0)
        dh = _dot(dy_ref[...].astype(BF16), w_ref[...], 1, 1)
        x = x_ref[...]
        r = lax.rsqrt(jnp.mean(x * x, axis=-1, keepdims=True) + NORM_EPS)
        xn = x * r
        a = nw_ref[...] * (1.0 + sc_ref[...])

        @pl.when(i == 0)
        def _():
            dnw_ref[...] = jnp.zeros_like(dnw_ref)
            dsh_ref[...] = jnp.zeros_like(dsh_ref)

        dnw_ref[...] += _colsum(dh * xn)
        dsh_ref[...] += _colsum(dh)
        dxn = dh * a
        dx_ref[...] = r * (dxn - xn * jnp.mean(dxn * xn, axis=-1, keepdims=True)) + dres_ref[...]

        @pl.when(i == steps - 1)
        def _():
            da = dnw_ref[...]
            dsc_ref[...] = da * nw_ref[...]
            dnw_ref[...] = da * (1.0 + sc_ref[...])

    row = lambda i: (i, 0)
    vec = jax.ShapeDtypeStruct((1, d), F32)
    return _call(
        body, name=name, grid=(steps,),
        in_specs=[pl.BlockSpec((tm, n), row), _resident((d, n)), pl.BlockSpec((tm, d), row), _full((1, d)),
                  _full((1, d)), pl.BlockSpec((tm, d), row)],
        out_specs=[pl.BlockSpec((tm, d), row), _full((1, d)), _full((1, d)), _full((1, d))],
        out_shape=[jax.ShapeDtypeStruct((t, d), F32), vec, vec, vec], vmem=BIG_VMEM,
    )(dy, w, x, nw, scale, dres)


def _pick_tn(n, k, budget=6 << 20):
    best = LANE
    for m in range(1, n // LANE + 1):
        tn = m * LANE
        if n % tn == 0 and k * tn * 4 <= budget:
            best = tn
    return best


def mm_tn(a, g, *, name, tt=2048):
    t, k = a.shape
    n = g.shape[1]
    tn = _pick_tn(n, k)
    tt = min(tt, t)
    steps = t // tt

    def body(a_ref, g_ref, o_ref, b_ref):
        i = pl.program_id(1)

        @pl.when(i == 0)
        def _():
            o_ref[...] = jnp.zeros_like(o_ref)

        o_ref[...] += _dot(a_ref[...].astype(BF16), g_ref[...].astype(BF16), 0, 0)

        @pl.when(i == steps - 1)
        def _():
            b_ref[...] = o_ref[...].astype(BF16)

    out = pl.BlockSpec((k, tn), lambda j, i: (0, j))
    return _call(
        body, name=name, grid=(n // tn, steps),
        in_specs=[pl.BlockSpec((tt, k), lambda j, i: (i, 0)), pl.BlockSpec((tt, tn), lambda j, i: (i, j))],
        out_specs=[out, out],
        out_shape=[jax.ShapeDtypeStruct((k, n), F32), jax.ShapeDtypeStruct((k, n), BF16)],
        semantics=("parallel", "arbitrary"), vmem=BIG_VMEM,
    )(a, g)


def _shift_rows(x, s):
    if s == 0:
        return x
    t = x.shape[0]
    ri = _iota2(x.shape, 0)
    rolled = pltpu.roll(x, s % t, axis=0)
    if s > 0:
        return jnp.where(ri >= s, rolled, 0.0)
    return jnp.where(ri < t + s, rolled, 0.0)


def _conv_pre(x, w):
    acc = x * w[DN_CONV - 1:DN_CONV, :]
    for j in range(DN_CONV - 1):
        acc = acc + _shift_rows(x, DN_CONV - 1 - j) * w[j:j + 1, :]
    return acc


def dn_conv(proj, conv_w):
    t = proj.shape[0]
    width = 3 * DN_WIDTH

    def body(x_ref, w_ref, o_ref):
        o_ref[...] = _silu(_conv_pre(x_ref[...], w_ref[...]))

    col = lambda j: (0, j)
    return _call(
        body, name="dn_conv", grid=(width // LANE,),
        in_specs=[pl.BlockSpec((t, LANE), col), pl.BlockSpec((8, LANE), col)],
        out_specs=pl.BlockSpec((t, LANE), col),
        out_shape=jax.ShapeDtypeStruct((t, width), F32), semantics=("parallel",),
    )(proj, conv_w)


def dn_conv_bwd(proj, conv_w, dact, dproj):
    t = proj.shape[0]
    width = 3 * DN_WIDTH

    def body(x_ref, w_ref, d_ref, _, dx_ref, dw_ref):
        x = x_ref[...]
        w = w_ref[...]
        dc = d_ref[...] * _dsilu(_conv_pre(x, w))
        dx = dc * w[DN_CONV - 1:DN_CONV, :]
        rows = []
        for j in range(DN_CONV - 1):
            s = DN_CONV - 1 - j
            up = _shift_rows(dc, -s)
            dx = dx + up * w[j:j + 1, :]
            rows.append(_colsum(up * x))
        rows.append(_colsum(dc * x))
        dx_ref[...] = dx.astype(BF16)
        ri = _iota2((8, LANE), 0)
        dw = jnp.zeros((8, LANE), F32)
        for j in range(DN_CONV):
            dw = dw + jnp.where(ri == j, rows[j], 0.0)
        dw_ref[...] = dw

    col = lambda j: (0, j)
    return _call(
        body, name="dn_conv_bwd", grid=(width // LANE,),
        in_specs=[pl.BlockSpec((t, LANE), col), pl.BlockSpec((8, LANE), col), pl.BlockSpec((t, LANE), col), ANY],
        out_specs=[pl.BlockSpec((t, LANE), col), pl.BlockSpec((8, LANE), col)],
        out_shape=[jax.ShapeDtypeStruct(dproj.shape, dproj.dtype), jax.ShapeDtypeStruct((8, width), F32)],
        semantics=("parallel",), aliases={3: 0},
    )(proj, conv_w, dact, dproj)


def _t(x):
    return jnp.swapaxes(x, -1, -2)


def _inv_unit_lower(a):
    c = a.shape[-1]
    eye = (_iota2((c, c), 0) == _iota2((c, c), 1)).astype(F32)
    x = eye - a
    p = _hdot(a, a)
    steps = int(math.log2(c)) - 1
    for i in range(steps):
        x = x + _hdot(x, p)
        if i < steps - 1:
            p = _hdot(p, p)
    return x


def _dn_local(q, k, v, a, b, alog, dtb, tinv=None):
    nh, c, d = q.shape
    rq = lax.rsqrt(_rowsum(q * q) + NORM_EPS)
    rk = lax.rsqrt(_rowsum(k * k) + NORM_EPS)
    qh = q * rq
    kn = k * rk
    qs = qh * (d ** -0.5)
    g = -jnp.exp(alog) * _softplus(a + dtb)
    beta = _sigmoid(b)
    ri = _iota2((c, c), 0)
    ci = _iota2((c, c), 1)
    causal = ri >= ci
    strict = ri > ci
    gb = jnp.broadcast_to(g, (nh, c, d))
    gcb = _xdot(jnp.broadcast_to(causal.astype(F32), (nh, c, c)), gb, exact=0)
    gc = gcb[..., :1]
    gl = _colsum(gb)[..., :1]
    dec = jnp.exp(jnp.where(causal, gc - _t(gcb)[:, :c, :], -1e30))
    kb = kn * beta
    amat = jnp.where(strict, _bdot(kb, kn, 1, 1) * dec, 0.0)
    if tinv is None:
        tinv = _inv_unit_lower(amat)
    e = jnp.exp(gc)
    f = jnp.exp(gl - gc)
    rw = kb * e
    sol = _hdot(tinv, jnp.concatenate([v * beta, rw], axis=-1))
    pmat = jnp.where(causal, _bdot(qs, kn, 1, 1) * dec, 0.0)
    return dict(rq=rq, rk=rk, qh=qh, kn=kn, qs=qs, g=g, beta=beta, causal=causal, strict=strict, gl=gl,
                dec=dec, kb=kb, amat=amat, tinv=tinv, e=e, f=f, rw=rw, u=sol[..., :d], w=sol[..., d:], pmat=pmat,
                qd=qs * e, kd=kn * f)


_DN_FIELDS = ("u", "w", "qd", "kd", "pmat", "gl")


def _dn_state(m, s_in):
    vnew = m["u"] - _bdot(m["w"], s_in)
    o = _bdot(m["qd"], s_in) + _bdot(m["pmat"], vnew)
    return vnew, o, s_in * jnp.exp(m["gl"]) + _bdot(m["kd"], vnew, 0, 0)


def _dn_state_bwd(m, s_in, do, ds_out):
    el = jnp.exp(m["gl"])
    dvnew = _bdot(m["pmat"], do, 0, 0) + _bdot(m["kd"], ds_out)
    dkd = _bdot(m["vnew"], ds_out, 1, 1)
    ds_in = _bdot(m["qd"], do, 0, 0) + el * ds_out - _bdot(m["w"], dvnew, 0, 0)
    dgl = el * _colsum(_rowsum(s_in * ds_out))
    return dvnew, dkd, dgl, ds_in


def _dn_local_bwd(m, q, v, a, alog, dtb, s_in, vnew, do, dvnew, dkd, dgl):
    nh, c, d = q.shape
    kn, qs, kb, u, w, e, f = m["kn"], m["qs"], m["kb"], m["u"], m["w"], m["e"], m["f"]
    beta, dec, tinv, kd, qd = m["beta"], m["dec"], m["tinv"], m["kd"], m["qd"]
    dp = jnp.where(m["causal"], _bdot(do, vnew, 1, 1), 0.0)
    dqd = _bdot(do, s_in, 1, 1)
    dw = -_bdot(dvnew, s_in, 1, 1)
    dsol = _hdot(tinv, jnp.concatenate([dvnew, dw], axis=-1), 0, 0)
    dru = dsol[..., :d]
    drw = dsol[..., d:]
    da_m = -jnp.where(m["strict"], _bdot(dsol, jnp.concatenate([u, w], axis=-1), 1, 1), 0.0)
    db_m = da_m * dec
    dq_m = dp * dec
    dkb = _bdot(db_m, kn)
    dkn = _bdot(db_m, kb, 0, 0) + _bdot(dq_m, qs, 0, 0)
    dqs = _bdot(dq_m, kn)
    gmat = da_m * m["amat"] + dp * m["pmat"]
    ones = jnp.ones((nh, c, d), F32)
    dgam = (_xdot(gmat, ones) - _xdot(gmat, ones, 0, 0))[..., :1]
    dqs = dqs + dqd * e
    dgam = dgam + _rowsum(dqd * qd)
    dkn = dkn + dkd * f
    tk = _rowsum(dkd * kd)
    dgam = dgam - tk
    dgl = dgl + _colsum(tk)
    dkb = dkb + drw * e
    dgam = dgam + _rowsum(drw * m["rw"])
    dv = dru * beta
    dbeta = _rowsum(dru * v) + _rowsum(dkb * kn)
    dkn = dkn + dkb * beta
    last = (_iota2((c, 1), 0) == c - 1).astype(F32)
    dgam = dgam + last * dgl
    upper = (_iota2((c, c), 0) <= _iota2((c, c), 1)).astype(F32)
    dg = _xdot(jnp.broadcast_to(upper, (nh, c, c)), jnp.broadcast_to(dgam, (nh, c, d)), exact=0)[..., :1]
    dqh = dqs * (d ** -0.5)
    dq = m["rq"] * (dqh - m["qh"] * _rowsum(dqh * m["qh"]))
    dk = m["rk"] * (dkn - kn * _rowsum(dkn * kn))
    sg = _sigmoid(a + dtb)
    da = dg * (-jnp.exp(alog)) * sg
    dalog = _colsum(dg * m["g"])
    ddtb = _colsum(da)
    db = dbeta * beta * (1.0 - beta)
    return dq, dk, dv, da, db, dalog, ddtb


def _dn_gate(o, z, wn):
    ro = lax.rsqrt(jnp.mean(o * o, axis=-1, keepdims=True) + NORM_EPS)
    n = o * ro
    return n, ro, n * wn * _silu(z)


DN_PAIR = 4


def _heads(ref, col0):
    d = DN_HEAD_DIM
    return jnp.stack([ref[j * DN_CHUNK:(j + 1) * DN_CHUNK, col0 + h * d:col0 + (h + 1) * d]
                      for j in range(DN_PAIR) for h in range(DN_HEADS)])


def _dn_inputs(act_ref, ab_ref, sc_ref):
    ab = ab_ref[...]
    sc = sc_ref[...]
    rows = lambda j: slice(j * DN_CHUNK, (j + 1) * DN_CHUNK)
    both = [(j, h) for j in range(DN_PAIR) for h in range(DN_HEADS)]
    q = _heads(act_ref, 0)
    k = _heads(act_ref, DN_WIDTH)
    v = _heads(act_ref, 2 * DN_WIDTH)
    a = jnp.stack([ab[rows(j), h:h + 1] for j, h in both])
    b = jnp.stack([ab[rows(j), DN_HEADS + h:DN_HEADS + h + 1] for j, h in both])
    alog = jnp.stack([sc[0:1, h:h + 1] for _, h in both])
    dtb = jnp.stack([sc[1:2, h:h + 1] for _, h in both])
    return q, k, v, a, b, alog, dtb


def _chunk_of(m, j, fields):
    return {f: m[f][j * DN_HEADS:(j + 1) * DN_HEADS] for f in fields}


def dn_fwd(act, proj, scal, wn):
    t = act.shape[0]
    n = t // DN_CHUNK
    d = DN_HEAD_DIM
    rows = DN_PAIR * DN_CHUNK

    def body(act_ref, z_ref, ab_ref, sc_ref, wn_ref, y_ref, st_ref, ti_ref, s_ref):
        @pl.when(pl.program_id(0) == 0)
        def _():
            s_ref[...] = jnp.zeros_like(s_ref)

        m = _dn_local(*_dn_inputs(act_ref, ab_ref, sc_ref))
        s = s_ref[...]
        outs = []
        for j in range(DN_PAIR):
            st_ref[j] = s
            ti_ref[j] = m["tinv"][j * DN_HEADS:(j + 1) * DN_HEADS]
            _, o, s = _dn_state(_chunk_of(m, j, _DN_FIELDS), s)
            outs.append(o)
        s_ref[...] = s
        y = _dn_gate(jnp.concatenate(outs, axis=0), _heads(z_ref, 0), wn_ref[...])[2]
        for j in range(DN_PAIR):
            for h in range(DN_HEADS):
                y_ref[j * DN_CHUNK:(j + 1) * DN_CHUNK, h * d:(h + 1) * d] = y[j * DN_HEADS + h]

    return _call(
        body, name="dn_fwd", grid=(n // DN_PAIR,),
        in_specs=[pl.BlockSpec((rows, 3 * DN_WIDTH), lambda i: (i, 0)),
                  pl.BlockSpec((rows, DN_WIDTH), lambda i: (i, C_Z // DN_WIDTH)),
                  pl.BlockSpec((rows, LANE), lambda i: (i, C_AB // LANE)),
                  _full((8, LANE)), _full((1, d))],
        out_specs=[pl.BlockSpec((rows, DN_WIDTH), lambda i: (i, 0)),
                   pl.BlockSpec((DN_PAIR, DN_HEADS, d, d), lambda i: (i, 0, 0, 0)),
                   pl.BlockSpec((DN_PAIR, DN_HEADS, DN_CHUNK, DN_CHUNK), lambda i: (i, 0, 0, 0))],
        out_shape=[jax.ShapeDtypeStruct((t, MIX_WIDTH), F32), jax.ShapeDtypeStruct((n, DN_HEADS, d, d), F32),
                   jax.ShapeDtypeStruct((n, DN_HEADS, DN_CHUNK, DN_CHUNK), F32)],
        scratch_shapes=[pltpu.VMEM((DN_HEADS, d, d), F32)],
    )(act, proj, proj, scal, wn)


def dn_bwd(act, proj, scal, wn, states, tinvs, dy):
    t = act.shape[0]
    n = t // DN_CHUNK
    steps = n // DN_PAIR
    d = DN_HEAD_DIM
    zab = DN_WIDTH + AB_PAD
    rows = DN_PAIR * DN_CHUNK

    def body(act_ref, z_ref, ab_ref, sc_ref, wn_ref, st_ref, ti_ref, dy_ref, dact_ref, dzab_ref, dpar_ref, ds_ref):
        @pl.when(pl.program_id(0) == 0)
        def _():
            ds_ref[...] = jnp.zeros_like(ds_ref)
            dpar_ref[...] = jnp.zeros_like(dpar_ref)

        wnv = wn_ref[...]
        q, k, v, a, b, alog, dtb = _dn_inputs(act_ref, ab_ref, sc_ref)
        batch = (DN_PAIR * DN_HEADS,)
        s_in = st_ref[...].reshape(batch + (d, d))
        m = _dn_local(q, k, v, a, b, alog, dtb, ti_ref[...].reshape(batch + (DN_CHUNK, DN_CHUNK)))
        vnew, o, _ = _dn_state(m, s_in)
        z = _heads(z_ref, 0)
        dyh = _heads(dy_ref, 0)
        nrm, ro, _ = _dn_gate(o, z, wnv)
        sz = _silu(z)
        dz = dyh * nrm * wnv * _dsilu(z)
        dn = dyh * wnv * sz
        dwn = _colsum(dyh * nrm * sz)
        do = ro * (dn - nrm * jnp.mean(dn * nrm, axis=-1, keepdims=True))
        ds = ds_ref[...]
        parts = [None] * DN_PAIR
        for j in reversed(range(DN_PAIR)):
            mj = dict(_chunk_of(m, j, _DN_FIELDS), vnew=vnew[j * DN_HEADS:(j + 1) * DN_HEADS])
            dvnew, dkd, dgl, ds = _dn_state_bwd(mj, s_in[j * DN_HEADS:(j + 1) * DN_HEADS],
                                                do[j * DN_HEADS:(j + 1) * DN_HEADS], ds)
            parts[j] = (dvnew, dkd, dgl)
        ds_ref[...] = ds
        dvnew, dkd, dgl = (jnp.concatenate([p[i] for p in parts], axis=0) for i in range(3))
        dq, dk, dv, da, db, dalog, ddtb = _dn_local_bwd(m, q, v, a, alog, dtb, s_in, vnew, do, dvnew, dkd, dgl)
        lane = _iota2((DN_CHUNK, LANE), 1)
        prow = _iota2((8, LANE), 0)
        plane = _iota2((8, LANE), 1)
        dpar = jnp.zeros((8, LANE), F32)
        for j in range(DN_PAIR):
            rs = slice(j * DN_CHUNK, (j + 1) * DN_CHUNK)
            dab = jnp.zeros((DN_CHUNK, LANE), F32)
            for h in range(DN_HEADS):
                n_ = j * DN_HEADS + h
                dzab_ref[rs, h * d:(h + 1) * d] = dz[n_].astype(BF16)
                dact_ref[rs, h * d:(h + 1) * d] = dq[n_]
                dact_ref[rs, DN_WIDTH + h * d:DN_WIDTH + (h + 1) * d] = dk[n_]
                dact_ref[rs, 2 * DN_WIDTH + h * d:2 * DN_WIDTH + (h + 1) * d] = dv[n_]
                dab = dab + jnp.where(lane == h, da[n_], 0.0) + jnp.where(lane == DN_HEADS + h, db[n_], 0.0)
                dpar = dpar + jnp.where((prow == 0) & (plane == h), dalog[n_], 0.0)
                dpar = dpar + jnp.where((prow == 1) & (plane == h), ddtb[n_], 0.0)
                dpar = dpar + jnp.where(prow == 2, dwn[n_], 0.0)
            dzab_ref[rs, DN_WIDTH:DN_WIDTH + LANE] = dab.astype(BF16)
            dzab_ref[rs, DN_WIDTH + LANE:] = jnp.zeros((DN_CHUNK, AB_PAD - LANE), BF16)
        dpar_ref[...] += dpar

    rev = lambda i: (steps - 1 - i, 0)
    rev4 = lambda i: (steps - 1 - i, 0, 0, 0)
    return _call(
        body, name="dn_bwd", grid=(steps,),
        in_specs=[pl.BlockSpec((rows, 3 * DN_WIDTH), rev),
                  pl.BlockSpec((rows, DN_WIDTH), lambda i: (steps - 1 - i, C_Z // DN_WIDTH)),
                  pl.BlockSpec((rows, LANE), lambda i: (steps - 1 - i, C_AB // LANE)),
                  _full((8, LANE)), _full((1, d)),
                  pl.BlockSpec((DN_PAIR, DN_HEADS, d, d), rev4),
                  pl.BlockSpec((DN_PAIR, DN_HEADS, DN_CHUNK, DN_CHUNK), rev4),
                  pl.BlockSpec((rows, DN_WIDTH), rev)],
        out_specs=[pl.BlockSpec((rows, 3 * DN_WIDTH), rev),
                   pl.BlockSpec((rows, zab), lambda i: (steps - 1 - i, C_Z // zab)), _full((8, LANE))],
        out_shape=[jax.ShapeDtypeStruct((t, 3 * DN_WIDTH), F32), jax.ShapeDtypeStruct((t, IN_PAD), BF16),
                   jax.ShapeDtypeStruct((8, LANE), F32)],
        scratch_shapes=[pltpu.VMEM((DN_HEADS, d, d), F32)],
    )(act, proj, proj, scal, wn, states, tinvs, dy)


_INV_SQRT2 = 0.7071067811865476
_INV_SQRT2PI = 0.3989422804014327


def _gelu_parts(x):
    half = 0.5 * (1.0 + lax.erf(x * _INV_SQRT2))
    return x * half, half + x * jnp.exp(-0.5 * x * x) * _INV_SQRT2PI


def _gm_core(uv, lng, lnb, ws_ref, bst):
    c = uv.shape[0]
    zz, dgelu = _gelu_parts(uv)
    u = zz[:, :GM_WIDTH]
    vv = zz[:, GM_WIDTH:]
    xc = vv - jnp.mean(vv, axis=-1, keepdims=True)
    rs = lax.rsqrt(jnp.mean(xc * xc, axis=-1, keepdims=True) + NORM_EPS)
    xh = xc * rs
    vn = xh * lng + lnb
    grp = _iota2((c, GM_WIDTH), 1) // GM_GROUP_DIM
    tril = _iota2((c, c), 0) >= _iota2((c, c), 1)
    sv = jnp.zeros((c, GM_WIDTH), F32)
    masks = []
    for g in range(GM_GROUPS):
        mk = grp == g
        masks.append(mk)
        ws = jnp.where(tril, ws_ref[g], 0.0)
        sv = sv + _bdot(ws, jnp.where(mk, vn, 0.0)) + jnp.where(mk, bst[:, g:g + 1], 0.0)
    return u, xh, rs, vn, sv, masks, tril, dgelu


def gm_fwd(proj, lng, lnb, w_s, bst, ybuf):
    t = proj.shape[0]

    def body(uv_ref, g_ref, b_ref, ws_ref, bst_ref, _, y_ref):
        u, _, _, _, sv, _, _, _ = _gm_core(uv_ref[...], g_ref[...], b_ref[...], ws_ref, bst_ref[...])
        y_ref[...] = u * sv

    return _call(
        body, name="gm_fwd", grid=(t // GM_CHUNK,),
        in_specs=[pl.BlockSpec((GM_CHUNK, 2 * GM_WIDTH), lambda i: (i, C_UV // (2 * GM_WIDTH))),
                  _full((1, GM_WIDTH)), _full((1, GM_WIDTH)), _full((GM_GROUPS, GM_CHUNK, GM_CHUNK)),
                  _full((GM_CHUNK, LANE)), ANY],
        out_specs=pl.BlockSpec((GM_CHUNK, GM_WIDTH), lambda i: (i, DN_WIDTH // GM_WIDTH)),
        out_shape=jax.ShapeDtypeStruct(ybuf.shape, F32), semantics=("parallel",), aliases={5: 0},
    )(proj, lng, lnb, w_s, bst, ybuf)


def gm_bwd(proj, lng, lnb, w_s, bst, dy, dproj):
    t = proj.shape[0]

    def body(uv_ref, g_ref, b_ref, ws_ref, bst_ref, dy_ref, _, duv_ref, dws_ref, dbst_ref, dln_ref):
        @pl.when(pl.program_id(0) == 0)
        def _():
            dws_ref[...] = jnp.zeros_like(dws_ref)
            dbst_ref[...] = jnp.zeros_like(dbst_ref)
            dln_ref[...] = jnp.zeros_like(dln_ref)

        uv = uv_ref[...]
        lng = g_ref[...]
        u, xh, rs, vn, sv, masks, tril, dg = _gm_core(uv, lng, b_ref[...], ws_ref, bst_ref[...])
        dyv = dy_ref[...]
        dsv = dyv * u
        lane = _iota2((GM_CHUNK, LANE), 1)
        dvn = jnp.zeros_like(dsv)
        dbst = jnp.zeros((GM_CHUNK, LANE), F32)
        for g in range(GM_GROUPS):
            ws = jnp.where(tril, ws_ref[g], 0.0)
            dsg = jnp.where(masks[g], dsv, 0.0)
            dvn = dvn + jnp.where(masks[g], _bdot(ws, dsv, 0, 0), 0.0)
            dws_ref[g] += jnp.where(tril, _bdot(dsg, vn, 1, 1), 0.0)
            dbst = dbst + jnp.where(lane == g, _rowsum(dsg), 0.0)
        dbst_ref[...] += dbst
        row = _iota2((8, GM_WIDTH), 0)
        dln_ref[...] += jnp.where(row == 0, _colsum(dvn * xh), 0.0) + jnp.where(row == 1, _colsum(dvn), 0.0)
        dxh = dvn * lng
        dvv = rs * (dxh - jnp.mean(dxh, axis=-1, keepdims=True) - xh * jnp.mean(dxh * xh, axis=-1, keepdims=True))
        duv_ref[:, :GM_WIDTH] = (dyv * sv * dg[:, :GM_WIDTH]).astype(BF16)
        duv_ref[:, GM_WIDTH:] = (dvv * dg[:, GM_WIDTH:]).astype(BF16)

    return _call(
        body, name="gm_bwd", grid=(t // GM_CHUNK,),
        in_specs=[pl.BlockSpec((GM_CHUNK, 2 * GM_WIDTH), lambda i: (i, C_UV // (2 * GM_WIDTH))),
                  _full((1, GM_WIDTH)), _full((1, GM_WIDTH)), _full((GM_GROUPS, GM_CHUNK, GM_CHUNK)),
                  _full((GM_CHUNK, LANE)),
                  pl.BlockSpec((GM_CHUNK, GM_WIDTH), lambda i: (i, DN_WIDTH // GM_WIDTH)), ANY],
        out_specs=[pl.BlockSpec((GM_CHUNK, 2 * GM_WIDTH), lambda i: (i, C_UV // (2 * GM_WIDTH))),
                   _full((GM_GROUPS, GM_CHUNK, GM_CHUNK)), _full((GM_CHUNK, LANE)), _full((8, GM_WIDTH))],
        out_shape=[jax.ShapeDtypeStruct(dproj.shape, dproj.dtype),
                   jax.ShapeDtypeStruct((GM_GROUPS, GM_CHUNK, GM_CHUNK), F32),
                   jax.ShapeDtypeStruct((GM_CHUNK, LANE), F32), jax.ShapeDtypeStruct((8, GM_WIDTH), F32)],
        aliases={6: 0},
    )(proj, lng, lnb, w_s, bst, dy, dproj)


def _head_mats():
    r = _iota2((SW_WIDTH, SW_WIDTH), 0)
    c = _iota2((SW_WIDTH, SW_WIDTH), 1)
    same = (r // SW_HEAD_DIM) == (c // SW_HEAD_DIM)
    cc = c % SW_HEAD_DIM
    half = ROPE_DIM // 2
    rot = jnp.where((cc < half) & (r == c + half), -1.0, 0.0) + jnp.where((cc >= half) & (cc < ROPE_DIM) & (r == c - half), 1.0, 0.0)
    return same.astype(F32), rot


def _seg_col(s):
    return C_SW // SW_WIDTH + (s // 2) * 3 + s % 2


def _halves(x):
    return x[:, :LANE], x[:, LANE:]


def sw_prep(proj, nw2, cos_t, sin_t, *, tm=512):
    t = proj.shape[0]

    def body(x_ref, w_ref, c_ref, s_ref, o_ref):
        same, rot = _head_mats()
        x = x_ref[...]
        r = lax.rsqrt(_xdot(x * x, same) * (1.0 / SW_HEAD_DIM) + NORM_EPS)
        xn = x * r * w_ref[0]
        o_ref[0, 0], o_ref[0, 1] = _halves(xn * c_ref[...] + _xdot(xn, rot) * s_ref[...])

    return _call(
        body, name="sw_prep", grid=(6, t // tm),
        in_specs=[pl.BlockSpec((tm, SW_WIDTH), lambda s, i: (i, _seg_col(s))),
                  pl.BlockSpec((1, 1, SW_WIDTH), lambda s, i: (s % 2, 0, 0)),
                  pl.BlockSpec((tm, SW_WIDTH), lambda s, i: (i, 0)),
                  pl.BlockSpec((tm, SW_WIDTH), lambda s, i: (i, 0))],
        out_specs=pl.BlockSpec((1, 2, tm, LANE), lambda s, i: (s, 0, i, 0)),
        out_shape=jax.ShapeDtypeStruct((6, 2, t, LANE), F32), semantics=("parallel", "parallel"),
    )(proj, nw2, cos_t, sin_t)


def sw_prep_bwd(proj, nw2, cos_t, sin_t, dkvq, dproj, dnw, p, *, tm=512):
    t = proj.shape[0]
    col0 = C_SW // SW_WIDTH + 3 * p
    seg_col = lambda s: col0 + (s + 1) % 3

    def body(x_ref, w_ref, c_ref, s_ref, d_ref, _, dw0_ref, dx_ref, dw_ref):
        s = pl.program_id(0)
        dout = jnp.concatenate([d_ref[0, 0], d_ref[0, 1]], axis=1)

        @pl.when(s == 1)
        def _():
            dx_ref[...] = dout.astype(BF16)

        @pl.when((s != 1) & (pl.program_id(1) == 0))
        def _():
            dw_ref[...] = dw0_ref[...]

        @pl.when(s != 1)
        def _():
            same, rot = _head_mats()
            x = x_ref[...]
            w = w_ref[0]
            r = lax.rsqrt(_xdot(x * x, same) * (1.0 / SW_HEAD_DIM) + NORM_EPS)
            xh = x * r
            dxn = dout * c_ref[...] + _xdot(dout * s_ref[...], rot, 1, 1)
            dw_ref[0] += _colsum(dxn * xh)
            dxh = dxn * w
            dx_ref[...] = (r * (dxh - xh * (_xdot(dxh * xh, same) * (1.0 / SW_HEAD_DIM)))).astype(BF16)

    return _call(
        body, name=f"sw_prep_bwd{p}", grid=(3, t // tm),
        in_specs=[pl.BlockSpec((tm, SW_WIDTH), lambda s, i: (i, seg_col(s))),
                  pl.BlockSpec((1, 1, SW_WIDTH), lambda s, i: (1 - s // 2, 0, 0)),
                  pl.BlockSpec((tm, SW_WIDTH), lambda s, i: (i, 0)),
                  pl.BlockSpec((tm, SW_WIDTH), lambda s, i: (i, 0)),
                  pl.BlockSpec((1, 2, tm, LANE), lambda s, i: (s, 0, i, 0)), ANY,
                  pl.BlockSpec((1, 1, SW_WIDTH), lambda s, i: (s // 2, 0, 0))],
        out_specs=[pl.BlockSpec((tm, SW_WIDTH), lambda s, i: (i, seg_col(s))),
                   pl.BlockSpec((1, 1, SW_WIDTH), lambda s, i: (s // 2, 0, 0))],
        out_shape=[jax.ShapeDtypeStruct(dproj.shape, dproj.dtype), jax.ShapeDtypeStruct((2, 1, SW_WIDTH), F32)],
        semantics=("arbitrary", "arbitrary"), aliases={5: 0},
    )(proj, nw2, cos_t, sin_t, dkvq, dproj, dnw)


_SW_SCALE = SW_HEAD_DIM ** -0.5
_NEG = -1e30


def _sw_masks(has_other):
    ri = _iota2((SW_BLOCK, SW_BLOCK), 0)
    ci = _iota2((SW_BLOCK, SW_BLOCK), 1)
    return ri >= ci, (ci >= ri) & has_other


def _pair(x):
    first = _iota2((1, LANE), 1) < SW_HEAD_DIM
    return jnp.stack([jnp.where(first, x, 0.0), jnp.where(first, 0.0, x)])


def _both(x):
    return jnp.broadcast_to(x.astype(BF16)[None], (2,) + x.shape)


def _unpair(x2):
    first = _iota2((1, LANE), 1) < SW_HEAD_DIM
    return jnp.where(first, x2[0], x2[1])


def _head_cols(x):
    return jnp.stack([x[:, 0:1], x[:, SW_HEAD_DIM:SW_HEAD_DIM + 1]])


SW_GROUP = 16


def _sw_geometry(t, p):
    dil = SW_DILATIONS[p]
    unit = SW_BLOCK * dil
    nb = max(1, SW_GROUP // dil)
    return dil, unit, nb, t // (unit * nb)


def _sw_groups(dil, nb, body):
    if nb * dil == SW_GROUP:
        body([(k // dil, k % dil) for k in range(SW_GROUP)])
    else:
        for g in range(nb * dil // SW_GROUP):
            body([(0, SW_GROUP * g + k) for k in range(SW_GROUP)])


def _sw_rows(i, r, dil):
    start = i * SW_BLOCK * dil + r
    return pl.ds(start, SW_BLOCK) if dil == 1 else pl.ds(start, SW_BLOCK, stride=dil)


def _sw_load(refs, probs, dil, shift, wrap, fn):
    out = []
    for i, r in probs:
        if shift != 0 and i == wrap:
            out.append(fn(refs[1][_sw_rows(0, r, dil), :]))
        else:
            out.append(fn(refs[0][_sw_rows(i + shift, r, dil), :]))
    return jnp.concatenate(out, axis=0)


def _sw_other_masks(probs, wrap, edge_ok):
    _, other = _sw_masks(edge_ok)
    _, always = _sw_masks(True)
    return jnp.stack([other if i == wrap else always for i, _ in probs for _ in range(2)])


def sw_attn(qk, proj, p):
    t = proj.shape[0]
    dil, unit, nb, nsp = _sw_geometry(t, p)
    vcol = (C_SW + 3 * SW_WIDTH * p + 2 * SW_WIDTH) // LANE

    def body(q_ref, kc_ref, kp_ref, vc_ref, vp_ref, o_ref, l_ref):
        mc, _ = _sw_masks(True)
        first = pl.program_id(1) != 0
        q_r, k_r, v_r = (q_ref.at[0, 0], None), (kc_ref.at[0, 0], kp_ref.at[0, 0]), (vc_ref, vp_ref)

        def one(probs):
            mp = _sw_other_masks(probs, 0, first)
            q2 = _sw_load(q_r, probs, dil, 0, 0, _pair)
            sc = jnp.where(mc, _bdot(q2, _sw_load(k_r, probs, dil, 0, 0, _both), 1, 1) * _SW_SCALE, _NEG)
            sp = jnp.where(mp, _bdot(q2, _sw_load(k_r, probs, dil, -1, 0, _both), 1, 1) * _SW_SCALE, _NEG)
            mx = jnp.maximum(jnp.max(sc, axis=-1, keepdims=True), jnp.max(sp, axis=-1, keepdims=True))
            pc = jnp.exp(sc - mx)
            pp = jnp.exp(sp - mx)
            den = _rowsum(pc) + _rowsum(pp)
            o2 = (_bdot(pc, _sw_load(v_r, probs, dil, 0, 0, _both))
                  + _bdot(pp, _sw_load(v_r, probs, dil, -1, 0, _both))) * (1.0 / den)
            l2 = jnp.broadcast_to(mx + jnp.log(den), o2.shape)
            for n, (i, r) in enumerate(probs):
                o_ref.at[0][_sw_rows(i, r, dil), :] = _unpair(o2[2 * n:2 * n + 2])
                l_ref.at[0][_sw_rows(i, r, dil), :] = _unpair(l2[2 * n:2 * n + 2])

        _sw_groups(dil, nb, one)

    before = lambda j: jnp.maximum(j * nb - 1, 0)
    seg = lambda s: pl.BlockSpec((1, 1, unit * nb, LANE), lambda h, j: (s, h, j, 0))
    seg_b = lambda s: pl.BlockSpec((1, 1, unit, LANE), lambda h, j: (s, h, before(j), 0))
    out = pl.BlockSpec((1, unit * nb, LANE), lambda h, j: (h, j, 0))
    shp = jax.ShapeDtypeStruct((2, t, LANE), F32)
    return _call(
        body, name=f"sw_attn{p}", grid=(2, nsp),
        in_specs=[seg(2 * p), seg(2 * p + 1), seg_b(2 * p + 1),
                  pl.BlockSpec((unit * nb, LANE), lambda h, j: (j, vcol + h)),
                  pl.BlockSpec((unit, LANE), lambda h, j: (before(j), vcol + h))],
        out_specs=[out, out], out_shape=[shp, shp], semantics=("parallel", "parallel"),
    )(qk, qk, qk, proj, proj)


def sw_attn_dkv(qk, proj, dy, lg, dm, p):
    t = proj.shape[0]
    dil, unit, nb, nsp = _sw_geometry(t, p)
    nunits = t // unit
    vcol = (C_SW + 3 * SW_WIDTH * p + 2 * SW_WIDTH) // LANE
    ycol = (DN_WIDTH + GM_WIDTH) // LANE

    def body(k_ref, v_ref, qc_ref, qn_ref, doc_ref, don_ref, lc_ref, ln_ref, dc_ref, dn_ref, o_ref):
        mc, _ = _sw_masks(True)
        more = pl.program_id(1) + 1 < nsp
        q_r, do_r = (qc_ref.at[0, 0], qn_ref.at[0, 0]), (doc_ref, don_ref)
        l_r, d_r = (lc_ref.at[0], ln_ref.at[0]), (dc_ref.at[0], dn_ref.at[0])

        def one(probs):
            k2 = _sw_load((k_ref.at[0, 0], None), probs, dil, 0, 0, _both)
            v2 = _sw_load((v_ref, None), probs, dil, 0, 0, _both)
            dk = jnp.zeros((2 * SW_GROUP, SW_BLOCK, LANE), F32)
            dv = jnp.zeros((2 * SW_GROUP, SW_BLOCK, LANE), F32)
            for shift, mk in ((0, mc), (1, _sw_other_masks(probs, nb - 1, more))):
                q2 = _sw_load(q_r, probs, dil, shift, nb - 1, _pair)
                do2 = _sw_load(do_r, probs, dil, shift, nb - 1, _pair)
                lse = _sw_load(l_r, probs, dil, shift, nb - 1, _head_cols)
                dd = _sw_load(d_r, probs, dil, shift, nb - 1, _head_cols)
                pr = jnp.exp(jnp.where(mk, _bdot(q2, k2, 1, 1) * _SW_SCALE, _NEG) - lse)
                dv = dv + _bdot(pr, do2, 0, 0)
                ds = pr * (_bdot(do2, v2, 1, 1) - dd)
                dk = dk + _bdot(ds, q2, 0, 0)
            for n, (i, r) in enumerate(probs):
                o_ref.at[0, 0][_sw_rows(i, r, dil), :] = (dk[2 * n] + dk[2 * n + 1]) * _SW_SCALE
                o_ref.at[1, 0][_sw_rows(i, r, dil), :] = dv[2 * n] + dv[2 * n + 1]

        _sw_groups(dil, nb, one)

    after = lambda j: jnp.minimum((j + 1) * nb, nunits - 1)
    seg = lambda s: pl.BlockSpec((1, 1, unit * nb, LANE), lambda h, j: (s, h, j, 0))
    seg_a = lambda s: pl.BlockSpec((1, 1, unit, LANE), lambda h, j: (s, h, after(j), 0))
    col = lambda c0: pl.BlockSpec((unit * nb, LANE), lambda h, j: (j, c0 + h))
    col_a = lambda c0: pl.BlockSpec((unit, LANE), lambda h, j: (after(j), c0 + h))
    hp = pl.BlockSpec((1, unit * nb, LANE), lambda h, j: (h, j, 0))
    hp_a = pl.BlockSpec((1, unit, LANE), lambda h, j: (h, after(j), 0))
    return _call(
        body, name=f"sw_dkv{p}", grid=(2, nsp),
        in_specs=[seg(2 * p + 1), col(vcol), seg(2 * p), seg_a(2 * p), col(ycol), col_a(ycol), hp, hp_a, hp, hp_a],
        out_specs=pl.BlockSpec((2, 1, unit * nb, LANE), lambda h, j: (0, h, j, 0)),
        out_shape=jax.ShapeDtypeStruct((3, 2, t, LANE), F32), semantics=("parallel", "parallel"),
    )(qk, proj, qk, qk, dy, dy, lg, lg, dm, dm)


def sw_attn_dq(qk, proj, dy, lg, dm, dkvq, p):
    t = proj.shape[0]
    dil, unit, nb, nsp = _sw_geometry(t, p)
    vcol = (C_SW + 3 * SW_WIDTH * p + 2 * SW_WIDTH) // LANE
    ycol = (DN_WIDTH + GM_WIDTH) // LANE

    def body(q_ref, kc_ref, kp_ref, vc_ref, vp_ref, do_ref, l_ref, d_ref, _, dq_ref):
        mc, _ = _sw_masks(True)
        first = pl.program_id(1) != 0
        k_r, v_r = (kc_ref.at[0, 0], kp_ref.at[0, 0]), (vc_ref, vp_ref)

        def one(probs):
            mp = _sw_other_masks(probs, 0, first)
            q2 = _sw_load((q_ref.at[0, 0], None), probs, dil, 0, 0, _pair)
            do2 = _sw_load((do_ref, None), probs, dil, 0, 0, _pair)
            lse = _sw_load((l_ref.at[0], None), probs, dil, 0, 0, _head_cols)
            dd = _sw_load((d_ref.at[0], None), probs, dil, 0, 0, _head_cols)
            kc = _sw_load(k_r, probs, dil, 0, 0, _both)
            kp = _sw_load(k_r, probs, dil, -1, 0, _both)
            pc = jnp.exp(jnp.where(mc, _bdot(q2, kc, 1, 1) * _SW_SCALE, _NEG) - lse)
            pp = jnp.exp(jnp.where(mp, _bdot(q2, kp, 1, 1) * _SW_SCALE, _NEG) - lse)
            dsc = pc * (_bdot(do2, _sw_load(v_r, probs, dil, 0, 0, _both), 1, 1) - dd)
            dsp = pp * (_bdot(do2, _sw_load(v_r, probs, dil, -1, 0, _both), 1, 1) - dd)
            dq2 = (_bdot(dsc, kc) + _bdot(dsp, kp)) * _SW_SCALE
            for n, (i, r) in enumerate(probs):
                dq_ref.at[0, 0][_sw_rows(i, r, dil), :] = _unpair(dq2[2 * n:2 * n + 2])

        _sw_groups(dil, nb, one)

    before = lambda j: jnp.maximum(j * nb - 1, 0)
    seg = lambda s: pl.BlockSpec((1, 1, unit * nb, LANE), lambda h, j: (s, h, j, 0))
    seg_b = lambda s: pl.BlockSpec((1, 1, unit, LANE), lambda h, j: (s, h, before(j), 0))
    col = lambda c0: pl.BlockSpec((unit * nb, LANE), lambda h, j: (j, c0 + h))
    col_b = lambda c0: pl.BlockSpec((unit, LANE), lambda h, j: (before(j), c0 + h))
    hp = pl.BlockSpec((1, unit * nb, LANE), lambda h, j: (h, j, 0))
    return _call(
        body, name=f"sw_dq{p}", grid=(2, nsp),
        in_specs=[seg(2 * p), seg(2 * p + 1), seg_b(2 * p + 1), col(vcol), col_b(vcol), col(ycol), hp, hp, ANY],
        out_specs=pl.BlockSpec((1, 1, unit * nb, LANE), lambda h, j: (2, h, j, 0)),
        out_shape=jax.ShapeDtypeStruct(dkvq.shape, F32), semantics=("parallel", "parallel"), aliases={8: 0},
    )(qk, qk, qk, proj, proj, dy, lg, dm, dkvq)


def sw_merge(outs, lses, ybuf, *, tm=512):
    t = ybuf.shape[0]

    def body(o0, o1, o2, l0_ref, l1_ref, l2_ref, _, y_ref, lg_ref):
        l0, l1, l2 = l0_ref[...], l1_ref[...], l2_ref[...]
        mx = jnp.maximum(jnp.maximum(l0, l1), l2)
        lg = mx + jnp.log(jnp.exp(l0 - mx) + jnp.exp(l1 - mx) + jnp.exp(l2 - mx))
        lg_ref[...] = lg
        y = jnp.exp(l0 - lg) * o0[...] + jnp.exp(l1 - lg) * o1[...] + jnp.exp(l2 - lg) * o2[...]
        y_ref[...] = jnp.concatenate([y[0], y[1]], axis=1)

    hp = pl.BlockSpec((2, tm, LANE), lambda i: (0, i, 0))
    return _call(
        body, name="sw_merge", grid=(t // tm,), in_specs=[hp] * 6 + [ANY],
        out_specs=[pl.BlockSpec((tm, SW_WIDTH), lambda i: (i, (DN_WIDTH + GM_WIDTH) // SW_WIDTH)), hp],
        out_shape=[jax.ShapeDtypeStruct(ybuf.shape, F32), jax.ShapeDtypeStruct((2, t, LANE), F32)],
        semantics=("parallel",), aliases={6: 0},
    )(*outs, *lses, ybuf)


def sw_delta(dy, ybuf, *, tm=512):
    t = ybuf.shape[0]

    def body(dy_ref, y_ref, o_ref):
        same, _ = _head_mats()
        o_ref[0], o_ref[1] = _halves(_xdot(dy_ref[...] * y_ref[...], same))

    b1 = pl.BlockSpec((tm, SW_WIDTH), lambda i: (i, (DN_WIDTH + GM_WIDTH) // SW_WIDTH))
    return _call(body, name="sw_delta", grid=(t // tm,), in_specs=[b1, b1],
                 out_specs=pl.BlockSpec((2, tm, LANE), lambda i: (0, i, 0)),
                 out_shape=jax.ShapeDtypeStruct((2, t, LANE), F32), semantics=("parallel",))(dy, ybuf)


def _rope_tables(t):
    inv = ROPE_THETA ** (-jnp.arange(0, ROPE_DIM, 2, dtype=F32) / ROPE_DIM)
    ang = jnp.arange(t, dtype=F32)[:, None] * inv[None, :]
    pad1 = jnp.ones((t, SW_HEAD_DIM - ROPE_DIM), F32)
    pad0 = jnp.zeros((t, SW_HEAD_DIM - ROPE_DIM), F32)
    cos_h = jnp.concatenate([jnp.cos(ang), jnp.cos(ang), pad1], axis=1)
    sin_h = jnp.concatenate([jnp.sin(ang), jnp.sin(ang), pad0], axis=1)
    return jnp.tile(cos_h, (1, SW_HEADS)), jnp.tile(sin_h, (1, SW_HEADS))


def sw_forward(proj, nw2, cos_t, sin_t, ybuf):
    qk = sw_prep(proj, nw2, cos_t, sin_t)
    outs, lses = [], []
    for p in range(len(SW_DILATIONS)):
        o, lse = sw_attn(qk, proj, p)
        outs.append(o)
        lses.append(lse)
    ybuf, lg = sw_merge(outs, lses, ybuf)
    return ybuf, (qk, lg)


def sw_backward(proj, nw2, cos_t, sin_t, res, ybuf, dy, dproj):
    qk, lg = res
    dm = sw_delta(dy, ybuf)
    dnw = jnp.zeros((2, 1, SW_WIDTH), F32)
    for p in range(len(SW_DILATIONS)):
        dkvq = sw_attn_dkv(qk, proj, dy, lg, dm, p)
        dkvq = sw_attn_dq(qk, proj, dy, lg, dm, dkvq, p)
        dproj, dnw = sw_prep_bwd(proj, nw2, cos_t, sin_t, dkvq, dproj, dnw, p)
    return dproj, dnw[::-1, 0]


def _pad_rows(a, rows):
    return jnp.zeros((rows,) + a.shape[1:], a.dtype).at[:a.shape[0]].set(a)


def _consts(sp):
    d = {}
    d["mix_nw"] = sp["mix_norm_w"][:, None, :]
    d["ffn_nw"] = sp["ffn_norm_w"][:, None, :]
    d["cw8"] = jnp.pad(sp["dn_conv_w"], ((0, 0), (0, 8 - DN_CONV), (0, 0)))
    d["scal"] = jnp.pad(jnp.stack([sp["dn_a_log"], sp["dn_dt_bias"]], axis=1), ((0, 0), (0, 6), (0, LANE - DN_HEADS)))
    d["wn"] = sp["dn_out_norm_w"][:, None, :]
    d["lng"] = sp["gm_ln_g"][:, None, :]
    d["lnb"] = sp["gm_ln_b"][:, None, :]
    d["w_s"] = sp["gm_w_s"]
    d["bst"] = jnp.pad(jnp.swapaxes(sp["gm_b_s"], 1, 2), ((0, 0), (0, 0), (0, LANE - GM_GROUPS)))
    d["nw2"] = jnp.stack([jnp.tile(sp["sw_q_norm_w"], (1, SW_HEADS)),
                          jnp.tile(sp["sw_k_norm_w"], (1, SW_HEADS))], axis=1)[:, :, None, :]
    return d


def _layer_fwd(x, mod, get_w, cs, tabs, target=None):
    wb = dict(get_w("w_in", x))
    h1, proj = norm_mm(x, cs["mix_nw"], mod[1], mod[0], wb["w_in"], swiglu=False, name="in_proj")
    act = dn_conv(proj, cs["cw8"])
    y, states, tinvs = dn_fwd(act, proj, cs["scal"], cs["wn"])
    y = gm_fwd(proj, cs["lng"], cs["lnb"], cs["w_s"], cs["bst"], y)
    y, swres = sw_forward(proj, cs["nw2"], *tabs, y)
    wb.update(get_w("w_out", y))
    x1, o1 = resid_mm(y, wb["w_out"], x, mod[2], name="out_proj")
    wb.update(get_w("ffn", x1))
    h2, gu, actf = norm_mm(x1, cs["ffn_nw"], mod[4], mod[3], wb["w_ffn_in"], swiglu=True, name="ffn_in")
    if target is None:
        x2, o2 = resid_mm(actf, wb["w_ffn_out"], x1, mod[5], name="ffn_out")
    else:
        dy, o2, loss = resid_mm(actf, wb["w_ffn_out"], x1, mod[5], name="ffn_out_loss", target=target)
        x2 = (dy, loss)
    res = dict(x=x, h1=h1, proj=proj, act=act, states=states, tinvs=tinvs, swres=swres, y=y, x1=x1, o1=o1, h2=h2, gu=gu,
               actf=actf, o2=o2)
    return x2, res, wb


def _layer_bwd(dx2, res, mod, wb, cs, tabs, grads_done):
    dgu, gx2, dgate2 = resid_mm_bwd(dx2, mod[5], res["o2"], wb["w_ffn_out"], res["gu"], name="ffn_out_bwd", tm=512)
    g_wfo = mm_tn(res["actf"], gx2, name="wg_ffn_out")
    g_wfi = mm_tn(res["h2"], dgu, name="wg_ffn_in")
    token = grads_done("ffn", dict(w_ffn_in=g_wfi, w_ffn_out=g_wfo))
    dx1, d_ffn_nw, dscale2, dshift2 = norm_mm_bwd(dgu, wb["w_ffn_in"], res["x1"], cs["ffn_nw"], mod[4] + token, dx2,
                                                  name="ffn_in_bwd")
    dy, gx1, dgate1 = resid_mm_bwd(dx1, mod[2], res["o1"], wb["w_out"], None, name="out_proj_bwd", tm=512)
    g_wout = mm_tn(res["y"], gx1, name="wg_out")
    proj = res["proj"]
    dact, dproj, dpar = dn_bwd(res["act"], proj, cs["scal"], cs["wn"], res["states"], res["tinvs"], dy)
    dproj, dcw = dn_conv_bwd(proj, cs["cw8"], dact, dproj)
    dproj, dws, dbst, dln = gm_bwd(proj, cs["lng"], cs["lnb"], cs["w_s"], cs["bst"], dy, dproj)
    dproj, dnw = sw_backward(proj, cs["nw2"], *tabs, res["swres"], res["y"], dy, dproj)
    g_win = mm_tn(res["h1"], dproj, name="wg_in")
    dx, d_mix_nw, dscale1, dshift1 = norm_mm_bwd(dproj, wb["w_in"], res["x"], cs["mix_nw"], mod[1], dx1,
                                                 name="in_proj_bwd")
    dmod = jnp.concatenate([dshift1, dscale1, dgate1, dshift2, dscale2, dgate2], axis=1)
    dnw = dnw.reshape(2, SW_HEADS, SW_HEAD_DIM).sum(1)
    small = dict(mix_norm_w=d_mix_nw[0], ffn_norm_w=d_ffn_nw[0], dn_conv_w=dcw[:DN_CONV],
                 dn_a_log=dpar[0, :DN_HEADS], dn_dt_bias=dpar[1, :DN_HEADS], dn_out_norm_w=dpar[2],
                 gm_ln_g=dln[0], gm_ln_b=dln[1], gm_w_s=dws, gm_b_s=dbst[:, :GM_GROUPS].T,
                 sw_q_norm_w=dnw[0], sw_k_norm_w=dnw[1])
    token = grads_done("mix", dict(w_in=g_win, w_out=g_wout))
    return dx, small, dmod, token


def _permute_w_in(w):
    pad = jnp.zeros(w.shape[:-1] + (AB_PAD - 8,), w.dtype)
    return jnp.concatenate([w[..., 0:2056], pad, w[..., 2568:IN_WIDTH], w[..., 2056:2568]], axis=-1)


def _unpermute_w_in(g):
    return jnp.concatenate([g[..., 0:2056], g[..., C_UV:IN_PAD], g[..., C_SW:C_UV]], axis=-1)


def _local_step(x, target, mods, weights_of, grads_done, sp):
    layers = mods.shape[0]
    t, d = x.shape
    tabs = _rope_tables(t)
    consts = _consts(sp)
    saved = []
    for layer in range(layers):
        mod = mods[layer].reshape(6, 1, d)
        cs = {k: v[layer] for k, v in consts.items()}
        x, res, wb = _layer_fwd(x, mod, functools.partial(weights_of, layer), cs, tabs,
                                target if layer == layers - 1 else None)
        saved.append((res, mod, wb, cs))
    dx, loss = x
    smalls, dmods = [], []
    token = jnp.zeros((1, 1), F32)
    for layer in reversed(range(layers)):
        res, mod, wb, cs = saved[layer]
        dx, small, dmod, token = _layer_bwd(dx, res, mod + token, wb, cs, tabs, functools.partial(grads_done, layer))
        smalls.append(small)
        dmods.append(dmod[0])
    smalls, dmods = smalls[::-1], dmods[::-1]
    small = {k: jnp.stack([s[k] for s in smalls]) for k in smalls[0]}
    return loss, dx, small, jnp.stack(dmods) + token


def mod_fwd(c_all, w_mod, b_shard):
    layers, d, n = w_mod.shape

    def body(c_ref, w_ref, b_ref, o_ref):
        ca = _silu(c_ref[...]).astype(BF16)
        o_ref[0] = _dot(ca, w_ref[0].astype(BF16), 1, 0) + b_ref[0]

    return _call(
        body, name="mod_fwd", grid=(layers,),
        in_specs=[_full((8, d)), pl.BlockSpec((1, d, n), lambda i: (i, 0, 0)),
                  pl.BlockSpec((1, 1, n), lambda i: (i, 0, 0))],
        out_specs=pl.BlockSpec((1, 8, n), lambda i: (i, 0, 0)),
        out_shape=jax.ShapeDtypeStruct((layers, 8, n), F32), semantics=("parallel",),
    )(c_all, w_mod, b_shard)


def mod_bwd(c_all, dmod):
    layers, _, n = dmod.shape
    d = c_all.shape[1]

    def body(c_ref, g_ref, o_ref):
        ca = _silu(c_ref[...]).astype(BF16)
        o_ref[0] = _dot(ca, g_ref[0].astype(BF16), 0, 0)

    return _call(
        body, name="mod_bwd", grid=(layers,),
        in_specs=[_full((8, d)), pl.BlockSpec((1, 8, n), lambda i: (i, 0, 0))],
        out_specs=pl.BlockSpec((1, d, n), lambda i: (i, 0, 0)),
        out_shape=jax.ShapeDtypeStruct((layers, d, n), F32), semantics=("parallel",),
    )(c_all, dmod)


N_DEV = 8


def _place():
    return lax.axis_index("x"), lax.axis_index("y"), lax.axis_index("c")


def _other_chips(x, y):
    return [(1 - x, y), (x, 1 - y), (1 - x, 1 - y)]


def allgather8(x_shard, *, name):
    m_per, n = x_shard.shape

    def body(x_ref, out_ref, send_sems, recv_sems, local_sem):
        x, y, c = _place()
        me, sibling = (x, y, c), (x, y, 1 - c)
        chips = _other_chips(x, y)

        def rows(px, py, pc):
            return out_ref.at[pl.ds((4 * px + 2 * py + pc) * m_per, m_per), :]

        def copy(k, block, to, src=None):
            return pltpu.make_async_remote_copy(
                src_ref=rows(*block) if src is None else src, dst_ref=rows(*block),
                send_sem=send_sems.at[k], recv_sem=recv_sems.at[k], device_id=to, device_id_type=MESH)

        mine = pltpu.make_async_copy(x_ref, rows(*me), local_sem)
        mine.start()
        first = [copy(0, me, sibling, src=x_ref)]
        first += [copy(1 + j, me, (*chip, c), src=x_ref) for j, chip in enumerate(chips)]
        for cp in first:
            cp.start()
        passed = [copy(4 + j, (*chip, c), sibling) for j, chip in enumerate(chips)]
        for j, chip in enumerate(chips):
            copy(1 + j, (*chip, c), me).wait_recv()
            passed[j].start()
        copy(0, sibling, me).wait_recv()
        for j, chip in enumerate(chips):
            copy(4 + j, (*chip, 1 - c), me).wait_recv()
        for cp in first + passed:
            cp.wait_send()
        mine.wait()

    return pl.pallas_call(
        body, name=name, out_shape=jax.ShapeDtypeStruct((N_DEV * m_per, n), x_shard.dtype),
        in_specs=[pl.BlockSpec(memory_space=pltpu.VMEM)], out_specs=pl.BlockSpec(memory_space=pltpu.VMEM),
        scratch_shapes=[pltpu.SemaphoreType.DMA((7,)), pltpu.SemaphoreType.DMA((7,)), pltpu.SemaphoreType.DMA],
    )(x_shard)


HBM = pl.BlockSpec(memory_space=pltpu.HBM)
SEM = pl.BlockSpec(memory_space=pltpu.SEMAPHORE)
_EFFECT = pltpu.SideEffectType.DATAFLOW_SIDE_EFFECTING


def _piece(ref, sliced, chip):
    return ref.at[2 * chip[0] + chip[1]] if sliced else ref


def exchange_start(srcs, after, *, sliced, name):
    n = len(srcs)
    piece = lambda s: s.shape[1:] if sliced else s.shape

    def body(*refs):
        ins, lands = refs[:n], refs[n:2 * n]
        send_sems, recv_sems = refs[2 * n + len(after):2 * n + len(after) + 2]
        token = refs[-1]
        x, y, c = _place()
        me_s = 2 * x + y
        for a in range(n):
            for j, chip in enumerate(_other_chips(x, y)):
                pltpu.make_async_remote_copy(
                    src_ref=_piece(ins[a], sliced, chip), dst_ref=lands[a].at[me_s], send_sem=send_sems.at[3 * a + j],
                    recv_sem=recv_sems.at[3 * a + j], device_id=(*chip, c), device_id_type=MESH).start()
        token[...] = jnp.zeros_like(token)

    zones = [pltpu.with_memory_space_constraint(lax.empty((4,) + piece(s), s.dtype), pltpu.HBM) for s in srcs]
    srcs = [pltpu.with_memory_space_constraint(s, pltpu.HBM) for s in srcs]
    out = pl.pallas_call(
        body, name=name,
        out_shape=(pltpu.SemaphoreType.DMA((3 * n,)), pltpu.SemaphoreType.DMA((3 * n,)),
                   *[pltpu.HBM(s.shape, s.dtype) for s in srcs], *[pltpu.HBM(z.shape, z.dtype) for z in zones],
                   jax.ShapeDtypeStruct((8, LANE), F32)),
        in_specs=[HBM] * (2 * n) + [ANY] * len(after),
        out_specs=(SEM, SEM, *[HBM] * (2 * n), pl.BlockSpec(memory_space=pltpu.VMEM)),
        input_output_aliases={i: 2 + i for i in range(2 * n)},
        compiler_params=pltpu.CompilerParams(has_side_effects=_EFFECT),
    )(*srcs, *zones, *after)
    return out[0], out[1], out[2:2 + n], out[2 + n:2 + 2 * n], out[-1]


def exchange_wait(send_sems, recv_sems, srcs, zones, after, *, which, sliced, name):
    n = len(srcs)

    def body(*refs):
        ins, lands = refs[:n], refs[n:2 * n]
        send_sems, recv_sems = refs[2 * n:2 * n + 2]
        x, y, c = _place()
        for a in range(n):
            for j, chip in enumerate(_other_chips(x, y)):
                copy = pltpu.make_async_remote_copy(
                    src_ref=_piece(ins[a], sliced, chip), dst_ref=lands[a].at[2 * chip[0] + chip[1]],
                    send_sem=send_sems.at[3 * which[a] + j], recv_sem=recv_sems.at[3 * which[a] + j],
                    device_id=(*chip, c), device_id_type=MESH)
                copy.wait_send()
                copy.wait_recv()

    out = pl.pallas_call(
        body, name=name,
        out_shape=tuple(pltpu.HBM(s.shape, s.dtype) for s in (*srcs, *zones)),
        in_specs=[HBM] * (2 * n) + [SEM, SEM, ANY], out_specs=tuple([HBM] * (2 * n)),
        input_output_aliases={i: i for i in range(2 * n)},
        compiler_params=pltpu.CompilerParams(has_side_effects=_EFFECT),
    )(*srcs, *zones, send_sems, recv_sems, after)
    return out[n:]


def sibling_swap(parts):
    n = len(parts)

    def body(*refs):
        ins, outs = refs[:n], refs[n:2 * n]
        send_sems, recv_sems = refs[2 * n:]
        x, y, c = _place()
        cps = []
        for a in range(n):
            cp = pltpu.make_async_remote_copy(
                src_ref=ins[a], dst_ref=outs[a], send_sem=send_sems.at[a], recv_sem=recv_sems.at[a],
                device_id=(x, y, 1 - c), device_id_type=MESH)
            cp.start()
            cps.append(cp)
        for cp in cps:
            cp.wait()

    return pl.pallas_call(
        body, name="sibling_swap", out_shape=[jax.ShapeDtypeStruct(p.shape, p.dtype) for p in parts],
        in_specs=[ANY] * n, out_specs=[ANY] * n,
        scratch_shapes=[pltpu.SemaphoreType.DMA((n,)), pltpu.SemaphoreType.DMA((n,))],
    )(*parts)


def _row_block(rows, cols, budget=1 << 20):
    best = rows if rows % 8 else 8
    for tr in range(8, rows + 1, 8):
        if rows % tr == 0 and tr * cols * 4 <= budget:
            best = tr
    return best


def chip_sum(own, recv, me_s, buf, layer, layers, *, name):
    r, n = own.shape
    tr = _row_block(r, n)
    steps = r // tr

    def body(me_ref, own_ref, recv_ref, *rest):
        o_ref = rest[-1]
        me = me_ref[0]
        acc = jnp.zeros((tr, n), F32)
        for s in range(4):
            acc = acc + jnp.where(me == s, own_ref[...], recv_ref[s].astype(F32))
        o_ref[...] = acc

    in_specs = [pl.BlockSpec((tr, n), lambda i, me: (i, 0)), pl.BlockSpec((4, tr, n), lambda i, me: (0, i, 0))]
    args = [me_s, own, recv]
    aliases = {}
    if buf is not None:
        in_specs.append(ANY)
        args.append(buf)
        aliases = {3: 0}
    return pl.pallas_call(
        body, name=name, out_shape=jax.ShapeDtypeStruct((layers * r, n), F32),
        grid_spec=pltpu.PrefetchScalarGridSpec(
            num_scalar_prefetch=1, grid=(steps,), in_specs=in_specs,
            out_specs=pl.BlockSpec((tr, n), lambda i, me: (layer * steps + i, 0))),
        input_output_aliases=aliases,
        compiler_params=pltpu.CompilerParams(dimension_semantics=("parallel",)),
    )(*args)


def _adam_update(w, g, m, v):
    m2 = ADAM_B1 * m + (1.0 - ADAM_B1) * g
    v2 = ADAM_B2 * v + (1.0 - ADAM_B2) * (g * g)
    m_hat = m2 / (1.0 - ADAM_B1 ** ADAM_STEP)
    v_hat = v2 / (1.0 - ADAM_B2 ** ADAM_STEP)
    delta = -ADAM_LR * (m_hat / (jnp.sqrt(v_hat) + ADAM_EPS) + ADAM_WD * w)
    return delta, m2, v2


def adamw(w, g_parts, m, v, *, name):
    r, n = w.shape
    tr = _row_block(r, n)
    k = len(g_parts)

    def body(*refs):
        w_ref, m_ref, v_ref = refs[k], refs[k + 1], refs[k + 2]
        g_ref, d_ref, m2_ref, v2_ref = refs[k + 3:]
        g = refs[0][...]
        for p in refs[1:k]:
            g = g + p[...]
        g_ref[...] = g
        d_ref[...], m2_ref[...], v2_ref[...] = _adam_update(w_ref[...], g, m_ref[...], v_ref[...])

    blk = pl.BlockSpec((tr, n), lambda i: (i, 0))
    shp = jax.ShapeDtypeStruct((r, n), F32)
    return _call(body, name=name, grid=(r // tr,), in_specs=[blk] * (k + 3), out_specs=[blk] * 4,
                 out_shape=[shp] * 4, semantics=("parallel",))(*g_parts, w, m, v)


def adamw_gathered(g_all, w, m, v, *, name):
    _, r, n = g_all.shape
    tr = _row_block(r, n * 4)

    def body(ga_ref, w_ref, m_ref, v_ref, g_ref, d_ref, m2_ref, v2_ref):
        g = ga_ref[0]
        for dev in range(1, N_DEV):
            g = g + ga_ref[dev]
        g_ref[...] = g
        d_ref[...], m2_ref[...], v2_ref[...] = _adam_update(w_ref[...], g, m_ref[...], v_ref[...])

    blk = pl.BlockSpec((tr, n), lambda i: (i, 0))
    shp = jax.ShapeDtypeStruct((r, n), F32)
    return _call(body, name=name, grid=(r // tr,),
                 in_specs=[pl.BlockSpec((N_DEV, tr, n), lambda i: (0, i, 0)), blk, blk, blk], out_specs=[blk] * 4,
                 out_shape=[shp] * 4, semantics=("parallel",))(g_all, w, m, v)


BIG = ("w_in", "w_out", "w_ffn_in", "w_ffn_out")
SMALL = ("b_mod", "mix_norm_w", "ffn_norm_w", "dn_conv_w", "dn_a_log", "dn_dt_bias", "dn_out_norm_w", "gm_ln_g",
         "gm_ln_b", "gm_w_s", "gm_b_s", "sw_q_norm_w", "sw_k_norm_w")
WEIGHTS = ("w_mod", "b_mod", "mix_norm_w", "ffn_norm_w", "w_in", "w_out", "dn_conv_w", "dn_a_log", "dn_dt_bias",
           "dn_out_norm_w", "gm_ln_g", "gm_ln_b", "gm_w_s", "gm_b_s", "sw_q_norm_w", "sw_k_norm_w", "w_ffn_in",
           "w_ffn_out")
PACK_ROWS = 8


def _pack(arrs):
    out = []
    for a in arrs:
        flat = a.reshape(-1).astype(F32)
        rows = -(-flat.shape[0] // (LANE * PACK_ROWS)) * PACK_ROWS
        out.append(jnp.pad(flat, (0, rows * LANE - flat.shape[0])).reshape(rows, LANE))
    return jnp.concatenate(out, axis=0)


def _unpack(packed, shapes):
    out, r0 = [], 0
    for shp in shapes:
        size = math.prod(shp)
        rows = -(-size // (LANE * PACK_ROWS)) * PACK_ROWS
        out.append(packed[r0:r0 + rows].reshape(-1)[:size].reshape(shp))
        r0 += rows
    return out


def kernel(x, c, w_mod, b_mod, mix_norm_w, ffn_norm_w, w_in, w_out, dn_conv_w, dn_a_log, dn_dt_bias, dn_out_norm_w, gm_ln_g, gm_ln_b, gm_w_s, gm_b_s, sw_q_norm_w, sw_k_norm_w, w_ffn_in, w_ffn_out, loss_target, m_w_mod, m_b_mod, m_mix_norm_w, m_ffn_norm_w, m_w_in, m_w_out, m_dn_conv_w, m_dn_a_log, m_dn_dt_bias, m_dn_out_norm_w, m_gm_ln_g, m_gm_ln_b, m_gm_w_s, m_gm_b_s, m_sw_q_norm_w, m_sw_k_norm_w, m_w_ffn_in, m_w_ffn_out, v_w_mod, v_b_mod, v_mix_norm_w, v_ffn_norm_w, v_w_in, v_w_out, v_dn_conv_w, v_dn_a_log, v_dn_dt_bias, v_dn_out_norm_w, v_gm_ln_g, v_gm_ln_b, v_gm_w_s, v_gm_b_s, v_sw_q_norm_w, v_sw_k_norm_w, v_w_ffn_in, v_w_ffn_out):
    w = dict(w_mod=w_mod, b_mod=b_mod, mix_norm_w=mix_norm_w, ffn_norm_w=ffn_norm_w, w_in=w_in, w_out=w_out,
             dn_conv_w=dn_conv_w, dn_a_log=dn_a_log, dn_dt_bias=dn_dt_bias, dn_out_norm_w=dn_out_norm_w,
             gm_ln_g=gm_ln_g, gm_ln_b=gm_ln_b, gm_w_s=gm_w_s, gm_b_s=gm_b_s, sw_q_norm_w=sw_q_norm_w,
             sw_k_norm_w=sw_k_norm_w, w_ffn_in=w_ffn_in, w_ffn_out=w_ffn_out)
    m = dict(w_mod=m_w_mod, b_mod=m_b_mod, mix_norm_w=m_mix_norm_w, ffn_norm_w=m_ffn_norm_w, w_in=m_w_in,
             w_out=m_w_out, dn_conv_w=m_dn_conv_w, dn_a_log=m_dn_a_log, dn_dt_bias=m_dn_dt_bias,
             dn_out_norm_w=m_dn_out_norm_w, gm_ln_g=m_gm_ln_g, gm_ln_b=m_gm_ln_b, gm_w_s=m_gm_w_s, gm_b_s=m_gm_b_s,
             sw_q_norm_w=m_sw_q_norm_w, sw_k_norm_w=m_sw_k_norm_w, w_ffn_in=m_w_ffn_in, w_ffn_out=m_w_ffn_out)
    v = dict(w_mod=v_w_mod, b_mod=v_b_mod, mix_norm_w=v_mix_norm_w, ffn_norm_w=v_ffn_norm_w, w_in=v_w_in,
             w_out=v_w_out, dn_conv_w=v_dn_conv_w, dn_a_log=v_dn_a_log, dn_dt_bias=v_dn_dt_bias,
             dn_out_norm_w=v_dn_out_norm_w, gm_ln_g=v_gm_ln_g, gm_ln_b=v_gm_ln_b, gm_w_s=v_gm_w_s, gm_b_s=v_gm_b_s,
             sw_q_norm_w=v_sw_q_norm_w, sw_k_norm_w=v_sw_k_norm_w, w_ffn_in=v_w_ffn_in, w_ffn_out=v_w_ffn_out)
    layers, d, mod_n = w_mod.shape
    mx, my, mc = _place()
    me_s = 2 * mx + my
    me_dev = 4 * mx + 2 * my + mc

    c_all = allgather8(_pad_rows(c, 8), name="gather_c").reshape(N_DEV, 8, d)[:, 0]
    b_shard = lax.dynamic_slice_in_dim(b_mod, me_s * mod_n, mod_n, axis=1)[:, None, :]
    mod_part = mod_fwd(c_all, w_mod, b_shard)
    mod_parts = allgather8(mod_part.reshape(layers * 8, mod_n), name="gather_mod")
    mod_parts = mod_parts.reshape(4, 2, layers, 8, mod_n)[:, 0]
    mod_all = mod_parts.transpose(1, 2, 0, 3).reshape(layers, 8, 4 * mod_n)
    mods = lax.dynamic_index_in_dim(mod_all, me_dev, axis=1, keepdims=False)

    cw = dn_conv_w.shape[-1]
    conv_rows = -(-layers * DN_CONV // 8) * 8
    conv_parts = allgather8(_pad_rows(dn_conv_w.reshape(layers * DN_CONV, cw), conv_rows), name="gather_conv")
    conv_parts = conv_parts.reshape(4, 2, conv_rows, cw)[:, 0, :layers * DN_CONV]
    conv_full = conv_parts.reshape(4, layers, DN_CONV, cw).transpose(1, 2, 0, 3).reshape(layers, DN_CONV, 4 * cw)

    shards = {k: w[k].astype(BF16) for k in BIG}
    groups = dict(w_in=(0,), w_out=(1,), ffn=(2, 3))
    gathers = [exchange_start([shards[k][layer] for k in BIG], [mods, conv_full], sliced=False, name=f"gather_start{layer}")
               for layer in range(layers)]
    mods = mods + sum(g[4][0, 0] for g in gathers)

    def weights_of(layer, group, after):
        send_sems, recv_sems, srcs, zones, _ = gathers[layer]
        which = groups[group]
        got = exchange_wait(send_sems, recv_sems, [srcs[a] for a in which], [zones[a] for a in which], after,
                            which=which, sliced=False, name=f"gather_wait_{group}{layer}")
        full = {BIG[a]: lax.dynamic_update_index_in_dim(z, shards[BIG[a]][layer], me_s, 0) for a, z in zip(which, got)}
        cols = lambda g: jnp.concatenate([g[s] for s in range(4)], axis=-1)
        shape = dict(w_in=lambda g: _permute_w_in(cols(g)), w_out=lambda g: g.reshape(-1, d), w_ffn_in=cols,
                     w_ffn_out=lambda g: g.reshape(-1, d))
        return {k: shape[k](g) for k, g in full.items()}

    scatters = {}
    last_scatter = []
    shard_axis = dict(w_in=1, w_out=0, w_ffn_in=1, w_ffn_out=0)

    def grads_done(layer, group, grads):
        fix = lambda k, g: _unpermute_w_in(g) if k == "w_in" else g
        send = [jnp.stack(jnp.split(fix(k, g16), 4, axis=shard_axis[k])) for k, (_, g16) in grads.items()]
        own = {}
        for k, (g32, _) in grads.items():
            g32 = fix(k, g32)
            size = g32.shape[shard_axis[k]] // 4
            own[k] = lax.dynamic_slice_in_dim(g32, me_s * size, size, axis=shard_axis[k])
        if (layer, group) == (0, "mix"):
            last_scatter.append((send, own))
            return jnp.zeros((1, 1), F32)
        started = exchange_start(send, [], sliced=True, name=f"scatter_start_{group}{layer}")
        scatters[layer, group] = (started, own)
        return started[4][:1, :1]

    sp = {k: w[k] for k in SMALL}
    sp["dn_conv_w"] = conv_full
    loss_blk, grad_x, small, dmods = _local_step(x[0], loss_target[0], mods, weights_of, grads_done, sp)
    loss = lax.psum(loss_blk[0, 0], ("x", "y", "c"))

    outs = {}
    small = dict(small, b_mod=dmods)
    packed = _pack([small[k] for k in SMALL])
    rows = packed.shape[0]
    g_all = allgather8(packed, name="gather_small").reshape(N_DEV, rows, LANE)
    send, own = last_scatter[0]
    scatters[0, "mix"] = (exchange_start(send, [g_all], sliced=True, name="scatter_start_mix0"), own)
    g_all = g_all + scatters[0, "mix"][0][4][0, 0]
    conv_zero = jnp.zeros((layers, DN_CONV, 3 * DN_WIDTH), F32)
    pk = lambda src: _pack([conv_zero if k == "dn_conv_w" else src[k] for k in SMALL])
    res = adamw_gathered(g_all, pk(w), pk(m), pk(v), name="adamw_small")
    shapes = [small[k].shape for k in SMALL]
    un = [_unpack(a, shapes) for a in res]
    for i, k in enumerate(SMALL):
        outs[k] = [un[j][i] for j in range(4)]
    g_conv = lax.dynamic_slice_in_dim(outs["dn_conv_w"][0], me_s * cw, cw, axis=2)
    flat = lambda a: a.reshape(-1, cw)
    res = adamw(flat(dn_conv_w), [flat(g_conv)], flat(m["dn_conv_w"]), flat(v["dn_conv_w"]), name="adamw_conv")
    outs["dn_conv_w"] = [a.reshape(dn_conv_w.shape) for a in res]

    b_rows = layers * 6 * d // LANE
    dmod_all = g_all[:, :b_rows].reshape(N_DEV, layers, 6 * d).transpose(1, 0, 2)
    dmod_shard = lax.dynamic_slice_in_dim(dmod_all, me_s * mod_n, mod_n, axis=2)
    g_wmod = mod_bwd(c_all, dmod_shard)
    flat = lambda a: a.reshape(-1, mod_n)
    res = adamw(flat(w_mod), [flat(g_wmod)], flat(m_w_mod), flat(v_w_mod), name="adamw_w_mod")
    outs["w_mod"] = [a.reshape(w_mod.shape) for a in res]

    me_arr = jnp.reshape(me_s, (1,)).astype(jnp.int32)
    partial = {k: None for k in BIG}
    for layer in range(layers):
        for group in ("ffn", "mix"):
            (send_sems, recv_sems, srcs, zones, _), own = scatters[layer, group]
            zones = exchange_wait(send_sems, recv_sems, srcs, zones, res[0], which=tuple(range(len(srcs))),
                                  sliced=True, name=f"scatter_wait_{group}{layer}")
            for k, z in zip(own, zones):
                partial[k] = chip_sum(own[k], z, me_arr, partial[k], layer, layers, name=f"chip_sum_{k}{layer}")
    partial = [partial[k] for k in BIG]
    theirs = sibling_swap(partial)
    for k, mine, other in zip(BIG, partial, theirs):
        shp = w[k].shape
        flat = lambda a: a.reshape(-1, shp[-1])
        res = adamw(flat(w[k]), [mine, other], flat(m[k]), flat(v[k]), name="adamw_" + k)
        outs[k] = [a.reshape(shp) for a in res]

    result = [loss, grad_x[None]]
    for j in range(4):
        result += [outs[k][j] for k in WEIGHTS]
    return tuple(result)
```
